```python
import math, functools
import jax, jax.numpy as jnp
from jax import lax
import numpy as np

D_MODEL = 2048
BATCH = 16
SEQ = 2048
DEPTH = 1
DEC_BATCH = 128
DEC_SEQ = 1
PAST_LEN = 16384
PAGE_SIZE = 128

GDN_HEADS = 8
GDN_DK = 128
GDN_DV = 128
GDN_CONV = 4
GDN_CHUNK = 64
SWA_HEADS = 16
SWA_KV_HEADS = 4
SWA_HD = 64
WINDOW = 128
ROPE_THETA = 10000.0
MEM_TOKENS = 256
MEM_HEADS = 4
MEM_HD = 128
D_FF = 5632
EPS = 1e-6
L2_EPS = 1e-6

GDN_QK_W = GDN_HEADS * GDN_DK
GDN_V_W = GDN_HEADS * GDN_DV
CONV_CH = 2 * GDN_QK_W + GDN_V_W
SWA_Q_W = SWA_HEADS * SWA_HD
SWA_KV_W = SWA_KV_HEADS * SWA_HD
SWA_GROUP = SWA_HEADS // SWA_KV_HEADS
MEM_W = MEM_HEADS * MEM_HD
IN_SIZES = (CONV_CH, GDN_V_W, GDN_HEADS, GDN_HEADS, SWA_Q_W, SWA_KV_W, SWA_KV_W, D_MODEL, D_MODEL)
D_IN = CONV_CH + GDN_V_W + 2 * GDN_HEADS + SWA_Q_W + 2 * SWA_KV_W + 2 * D_MODEL

kernel_name = 'hybrid_gdn_swa_sink_macaron_decode_step'


def rms_norm(x, g):
    xf = x.astype(jnp.float32)
    y = xf * lax.rsqrt(jnp.mean(xf * xf, -1, keepdims=True) + EPS)
    return (y * g.astype(jnp.float32)).astype(x.dtype)


def l2_normalize(x):
    xf = x.astype(jnp.float32)
    return xf * lax.rsqrt(jnp.sum(xf * xf, -1, keepdims=True) + L2_EPS)


def swiglu(h, w1, w3, w2):
    return (jax.nn.silu(h @ w1) * (h @ w3)) @ w2


def rope(x, pos):
    half = x.shape[-1] // 2
    inv_freq = ROPE_THETA ** (-jnp.arange(half, dtype=jnp.float32) / half)
    ang = pos.astype(jnp.float32)[:, None] * inv_freq[None, :]
    cos = jnp.cos(ang)[:, None, :]
    sin = jnp.sin(ang)[:, None, :]
    xf = x.astype(jnp.float32)
    x1, x2 = xf[..., :half], xf[..., half:]
    return jnp.concatenate([x1 * cos - x2 * sin, x2 * cos + x1 * sin], -1).astype(x.dtype)


def causal_conv(xc, buf, w):
    T = xc.shape[1]
    full = jnp.concatenate([buf.astype(xc.dtype), xc], 1)
    y = full[:, 0:T] * w[0]
    for i in range(1, GDN_CONV):
        y = y + full[:, i:i + T] * w[i]
    return jax.nn.silu(y), full[:, -(GDN_CONV - 1):]


def gated_delta_rule(q, k, v, g, beta, s0):
    B, T, H, DK = q.shape
    DV = v.shape[-1]
    C = math.gcd(T, GDN_CHUNK)
    N = T // C
    f32 = jnp.float32

    def chunkify(a):
        return a.astype(f32).reshape(B, N, C, H, a.shape[-1]).transpose(1, 0, 3, 2, 4)

    qc, kc, vc = chunkify(q), chunkify(k), chunkify(v)
    gc = g.astype(f32).reshape(B, N, C, H).transpose(1, 0, 3, 2)
    bc = beta.astype(f32).reshape(B, N, C, H).transpose(1, 0, 3, 2)
    G = jnp.cumsum(gc, -1)
    diff = G[..., :, None] - G[..., None, :]
    idx = jnp.arange(C)
    strict = idx[:, None] > idx[None, :]
    incl = idx[:, None] >= idx[None, :]
    dec_strict = jnp.where(strict, jnp.exp(jnp.where(strict, diff, 0.0)), 0.0)
    dec_incl = jnp.where(incl, jnp.exp(jnp.where(incl, diff, 0.0)), 0.0)
    kk = jnp.einsum('nbhid,nbhjd->nbhij', kc, kc)
    lmat = jnp.eye(C, dtype=f32) + bc[..., :, None] * kk * dec_strict
    gamma = jnp.exp(G)[..., None]
    rhs = jnp.concatenate([bc[..., None] * vc, bc[..., None] * gamma * kc], -1)
    sol = lax.linalg.triangular_solve(lmat, rhs, left_side=True, lower=True, unit_diagonal=True)
    u_v, w_k = sol[..., :DV], sol[..., DV:]
    qk = jnp.einsum('nbhid,nbhjd->nbhij', qc, kc) * dec_incl
    q_g = qc * gamma
    k_end = kc * jnp.exp(G[..., -1:] - G)[..., None]
    g_end = jnp.exp(G[..., -1])

    def step(S, xs):
        u_v_n, w_n, qk_n, q_g_n, k_end_n, g_end_n = xs
        u = u_v_n - jnp.einsum('bhcd,bhde->bhce', w_n, S)
        o = jnp.einsum('bhcd,bhde->bhce', q_g_n, S) + jnp.einsum('bhij,bhje->bhie', qk_n, u)
        S = g_end_n[..., None, None] * S + jnp.einsum('bhcd,bhce->bhde', k_end_n, u)
        return S, o

    S, o = lax.scan(step, s0.astype(f32), (u_v, w_k, qk, q_g, k_end, g_end))
    o = o.transpose(1, 0, 3, 2, 4).reshape(B, T, H, DV)
    return o, S


def sink_softmax(s, sink):
    m = jnp.maximum(jnp.max(s, -1, keepdims=True), sink)
    p = jnp.exp(s - m)
    denom = jnp.sum(p, -1, keepdims=True) + jnp.exp(sink - m)
    return p / denom


def swa_prompt(q, k, v, sinks):
    B, T, _, D = q.shape
    NB = T // WINDOW
    qb = q.reshape(B, NB, WINDOW, SWA_KV_HEADS, SWA_GROUP, D)
    kb = k.reshape(B, NB, WINDOW, SWA_KV_HEADS, D)
    vb = v.reshape(B, NB, WINDOW, SWA_KV_HEADS, D)
    kk = jnp.concatenate([jnp.concatenate([jnp.zeros_like(kb[:, :1]), kb[:, :-1]], 1), kb], 2)
    vv = jnp.concatenate([jnp.concatenate([jnp.zeros_like(vb[:, :1]), vb[:, :-1]], 1), vb], 2)
    s = jnp.einsum('bnqhgd,bnkhd->bnhgqk', qb, kk).astype(jnp.float32) * (D ** -0.5)
    qi = jnp.arange(WINDOW)[:, None] + WINDOW
    kj = jnp.arange(2 * WINDOW)[None, :]
    rel = qi - kj
    band = (rel >= 0) & (rel < WINDOW)
    has_prev = jnp.arange(NB)[:, None, None] > 0
    mask = band[None] & ((kj >= WINDOW)[None] | has_prev)
    s = jnp.where(mask[None, :, None, None], s, -jnp.inf)
    p = sink_softmax(s, sinks.astype(jnp.float32).reshape(SWA_KV_HEADS, SWA_GROUP, 1, 1))
    o = jnp.einsum('bnhgqk,bnkhd->bnqhgd', p.astype(vv.dtype), vv).reshape(B, T, SWA_Q_W)
    wb = min(WINDOW, T)
    return o, k[:, -wb:], v[:, -wb:]


def swa_sample(q, k, v, sinks, k_buf, v_buf):
    B, T, _, D = q.shape
    wb = k_buf.shape[1]
    kk = jnp.concatenate([k_buf.astype(k.dtype), k], 1)
    vv = jnp.concatenate([v_buf.astype(v.dtype), v], 1)
    qg = q.reshape(B, T, SWA_KV_HEADS, SWA_GROUP, D)
    s = jnp.einsum('bthgd,bkhd->bhgtk', qg, kk).astype(jnp.float32) * (D ** -0.5)
    rel = (jnp.arange(T)[:, None] + wb) - jnp.arange(wb + T)[None, :]
    mask = (rel >= 0) & (rel < WINDOW)
    s = jnp.where(mask, s, -jnp.inf)
    p = sink_softmax(s, sinks.astype(jnp.float32).reshape(SWA_KV_HEADS, SWA_GROUP, 1, 1))
    o = jnp.einsum('bhgtk,bkhd->bthgd', p.astype(vv.dtype), vv).reshape(B, T, SWA_Q_W)
    return o, kk[:, -wb:], vv[:, -wb:]


def mem_attend(h, mem_k, mem_v, w_q, w_o):
    B, T, _ = h.shape
    q = (h @ w_q).reshape(B, T, MEM_HEADS, MEM_HD)
    s = jnp.einsum('bthd,bmhd->bhtm', q, mem_k.astype(q.dtype)).astype(jnp.float32) * (MEM_HD ** -0.5)
    p = jax.nn.softmax(s, -1).astype(h.dtype)
    o = jnp.einsum('bhtm,bmhd->bthd', p, mem_v.astype(h.dtype)).reshape(B, T, MEM_W)
    return o @ w_o


def mixer_block(h, pos, conv_buf, s0, swa_fn, lp):
    B, T, _ = h.shape
    offsets = np.cumsum(IN_SIZES)[:-1].tolist()
    proj = h @ lp['w_in']
    qkv_c, z, b_raw, a_raw, q_s, k_s, v_s, gate_gdn, gate_swa = jnp.split(proj, offsets, -1)
    qkv_c, conv_new = causal_conv(qkv_c, conv_buf, lp['conv_w'])
    q_g, k_g, v_g = jnp.split(qkv_c, [GDN_QK_W, 2 * GDN_QK_W], -1)
    q_g = l2_normalize(q_g.reshape(B, T, GDN_HEADS, GDN_DK)) * (GDN_DK ** -0.5)
    k_g = l2_normalize(k_g.reshape(B, T, GDN_HEADS, GDN_DK))
    v_g = v_g.reshape(B, T, GDN_HEADS, GDN_DV)
    beta = jax.nn.sigmoid(b_raw.astype(jnp.float32))
    g = -jnp.exp(lp['gdn_A_log'].astype(jnp.float32)) * jax.nn.softplus(a_raw.astype(jnp.float32) + lp['gdn_dt_bias'].astype(jnp.float32))
    o_g, s_new = gated_delta_rule(q_g, k_g, v_g, g, beta, s0)
    o_g = rms_norm(o_g, lp['gdn_norm']) * jax.nn.silu(z.reshape(B, T, GDN_HEADS, GDN_DV).astype(jnp.float32))
    o_g = o_g.reshape(B, T, GDN_V_W).astype(h.dtype)
    q_s = rope(q_s.reshape(B, T, SWA_HEADS, SWA_HD), pos)
    k_s = rope(k_s.reshape(B, T, SWA_KV_HEADS, SWA_HD), pos)
    v_s = v_s.reshape(B, T, SWA_KV_HEADS, SWA_HD)
    o_s, k_buf_new, v_buf_new = swa_fn(q_s, k_s, v_s, lp['swa_sinks'])
    merged = jax.nn.sigmoid(gate_gdn) * (o_g @ lp['w_br_gdn']) + jax.nn.sigmoid(gate_swa) * (o_s @ lp['w_br_swa'])
    return merged @ lp['w_out'], s_new.astype(h.dtype), conv_new, k_buf_new, v_buf_new


def layer_forward(x, pos, conv_buf, s0, swa_fn, mem_k, mem_v, lp):
    x = x + 0.5 * swiglu(rms_norm(x, lp['norm_ffn1']), lp['ffn1_w1'], lp['ffn1_w3'], lp['ffn1_w2'])
    mix, s_new, conv_new, k_buf_new, v_buf_new = mixer_block(rms_norm(x, lp['norm_mix']), pos, conv_buf, s0, swa_fn, lp)
    x = x + mix
    x = x + mem_attend(rms_norm(x, lp['norm_mem_q']), mem_k, mem_v, lp['w_mem_q'], lp['w_mem_o'])
    x = x + 0.5 * swiglu(rms_norm(x, lp['norm_ffn2']), lp['ffn2_w1'], lp['ffn2_w3'], lp['ffn2_w2'])
    return x, s_new, conv_new, k_buf_new, v_buf_new


def setup_inputs(seed: int = 0) -> dict:
    key = jax.random.key(seed)
    keys = iter(jax.random.split(key, 48))
    f32 = jnp.float32
    L = DEPTH
    wb = min(WINDOW, PAST_LEN)

    def normal(shape, scale=1.0):
        return jax.random.normal(next(keys), shape, f32) * scale

    def gain(shape):
        return 1.0 + 0.05 * jax.random.normal(next(keys), shape, f32)

    dt = jnp.exp(jax.random.uniform(next(keys), (L, GDN_HEADS), f32, math.log(1e-3), math.log(1e-1)))
    inp = {}
    inp['x_prompt'] = normal((BATCH, SEQ, D_MODEL))
    inp['x_sample'] = normal((DEC_BATCH, DEC_SEQ, D_MODEL))
    inp['state_gdn'] = normal((L, DEC_BATCH, GDN_HEADS, GDN_DK, GDN_DV), 0.5)
    inp['state_conv'] = normal((L, DEC_BATCH, GDN_CONV - 1, CONV_CH))
    inp['cache_swa_k'] = normal((L, DEC_BATCH, wb, SWA_KV_HEADS, SWA_HD))
    inp['cache_swa_v'] = normal((L, DEC_BATCH, wb, SWA_KV_HEADS, SWA_HD))
    inp['cache_mem_k'] = normal((L, DEC_BATCH, MEM_TOKENS, MEM_HEADS, MEM_HD))
    inp['cache_mem_v'] = normal((L, DEC_BATCH, MEM_TOKENS, MEM_HEADS, MEM_HD))
    inp['mem_prompt'] = normal((BATCH, MEM_TOKENS, D_MODEL))
    inp['norm_ffn1'] = gain((L, D_MODEL))
    inp['ffn1_w1'] = normal((L, D_MODEL, D_FF), D_MODEL ** -0.5)
    inp['ffn1_w3'] = normal((L, D_MODEL, D_FF), D_MODEL ** -0.5)
    inp['ffn1_w2'] = normal((L, D_FF, D_MODEL), D_FF ** -0.5)
    inp['norm_mix'] = gain((L, D_MODEL))
    inp['w_in'] = normal((L, D_MODEL, D_IN), D_MODEL ** -0.5)
    inp['conv_w'] = normal((L, GDN_CONV, CONV_CH), GDN_CONV ** -0.5)
    inp['gdn_A_log'] = jnp.log(jax.random.uniform(next(keys), (L, GDN_HEADS), f32, 1.0, 16.0))
    inp['gdn_dt_bias'] = dt + jnp.log(-jnp.expm1(-dt))
    inp['gdn_norm'] = gain((L, GDN_DV))
    inp['swa_sinks'] = normal((L, SWA_HEADS), 0.5)
    inp['w_br_gdn'] = normal((L, GDN_V_W, D_MODEL), GDN_V_W ** -0.5)
    inp['w_br_swa'] = normal((L, SWA_Q_W, D_MODEL), SWA_Q_W ** -0.5)
    inp['w_out'] = normal((L, D_MODEL, D_MODEL), D_MODEL ** -0.5)
    inp['norm_mem_q'] = gain((L, D_MODEL))
    inp['norm_mem_kv'] = gain((L, D_MODEL))
    inp['w_mem_q'] = normal((L, D_MODEL, MEM_W), D_MODEL ** -0.5)
    inp['w_mem_k'] = normal((L, D_MODEL, MEM_W), D_MODEL ** -0.5)
    inp['w_mem_v'] = normal((L, D_MODEL, MEM_W), D_MODEL ** -0.5)
    inp['w_mem_o'] = normal((L, MEM_W, D_MODEL), MEM_W ** -0.5)
    inp['norm_ffn2'] = gain((L, D_MODEL))
    inp['ffn2_w1'] = normal((L, D_MODEL, D_FF), D_MODEL ** -0.5)
    inp['ffn2_w3'] = normal((L, D_MODEL, D_FF), D_MODEL ** -0.5)
    inp['ffn2_w2'] = normal((L, D_FF, D_MODEL), D_FF ** -0.5)
    inp['norm_final'] = gain((D_MODEL,))
    return inp


def reference(x_prompt, x_sample, state_gdn, state_conv, cache_swa_k, cache_swa_v, cache_mem_k, cache_mem_v,
              mem_prompt, norm_ffn1, ffn1_w1, ffn1_w3, ffn1_w2, norm_mix, w_in, conv_w, gdn_A_log, gdn_dt_bias,
              gdn_norm, swa_sinks, w_br_gdn, w_br_swa, w_out, norm_mem_q, norm_mem_kv, w_mem_q, w_mem_k, w_mem_v,
              w_mem_o, norm_ffn2, ffn2_w1, ffn2_w3, ffn2_w2, norm_final):
    Bp, Tp, _ = x_prompt.shape
    Bs, Ts, _ = x_sample.shape
    n_mem = mem_prompt.shape[1]
    pos_p = jnp.arange(Tp, dtype=jnp.int32)
    pos_s = PAST_LEN + jnp.arange(Ts, dtype=jnp.int32)
    hp, hs = x_prompt, x_sample
    p_gdn, p_conv, p_k, p_v, p_mk, p_mv = [], [], [], [], [], []
    s_gdn, s_conv, s_k, s_v = [], [], [], []
    for l in range(DEPTH):
        lp = {
            'norm_ffn1': norm_ffn1[l], 'ffn1_w1': ffn1_w1[l], 'ffn1_w3': ffn1_w3[l], 'ffn1_w2': ffn1_w2[l],
            'norm_mix': norm_mix[l], 'w_in': w_in[l], 'conv_w': conv_w[l], 'gdn_A_log': gdn_A_log[l],
            'gdn_dt_bias': gdn_dt_bias[l], 'gdn_norm': gdn_norm[l], 'swa_sinks': swa_sinks[l],
            'w_br_gdn': w_br_gdn[l], 'w_br_swa': w_br_swa[l], 'w_out': w_out[l],
            'norm_mem_q': norm_mem_q[l], 'w_mem_q': w_mem_q[l], 'w_mem_o': w_mem_o[l],
            'norm_ffn2': norm_ffn2[l], 'ffn2_w1': ffn2_w1[l], 'ffn2_w3': ffn2_w3[l], 'ffn2_w2': ffn2_w2[l],
        }
        mem_h = rms_norm(mem_prompt, norm_mem_kv[l])
        mk_p = (mem_h @ w_mem_k[l]).reshape(Bp, n_mem, MEM_HEADS, MEM_HD)
        mv_p = (mem_h @ w_mem_v[l]).reshape(Bp, n_mem, MEM_HEADS, MEM_HD)
        conv0 = jnp.zeros((Bp, GDN_CONV - 1, CONV_CH), x_prompt.dtype)
        s0 = jnp.zeros((Bp, GDN_HEADS, GDN_DK, GDN_DV), jnp.float32)
        hp, sg, sc, kb, vb = layer_forward(hp, pos_p, conv0, s0, swa_prompt, mk_p, mv_p, lp)
        p_gdn.append(sg); p_conv.append(sc); p_k.append(kb); p_v.append(vb); p_mk.append(mk_p); p_mv.append(mv_p)
        swa_s = functools.partial(swa_sample, k_buf=cache_swa_k[l], v_buf=cache_swa_v[l])
        hs, sg, sc, kb, vb = layer_forward(hs, pos_s, state_conv[l], state_gdn[l], swa_s, cache_mem_k[l], cache_mem_v[l], lp)
        s_gdn.append(sg); s_conv.append(sc); s_k.append(kb); s_v.append(vb)
    y_prompt = rms_norm(hp, norm_final)
    y_sample = rms_norm(hs, norm_final)
    return (y_prompt, y_sample,
            jnp.stack(p_gdn), jnp.stack(p_conv), jnp.stack(p_k), jnp.stack(p_v), jnp.stack(p_mk), jnp.stack(p_mv),
            jnp.stack(s_gdn), jnp.stack(s_conv), jnp.stack(s_k), jnp.stack(s_v))
```

```python
import functools
import math

import jax
import jax.numpy as jnp
from jax import lax
from jax.experimental import pallas as pl
from jax.experimental.pallas import tpu as pltpu

f32 = jnp.float32
bf16 = jnp.bfloat16

PAST_LEN = 16384
GDN_HEADS = 8
GDN_D = 128
GDN_CONV = 4
SWA_HEADS = 16
SWA_KV_HEADS = 4
SWA_HD = 64
WINDOW = 128
ROPE_THETA = 10000.0
MEM_HEADS = 4
MEM_HD = 128
EPS = 1e-6
L2_EPS = 1e-6

LANES = 128
SUBLANES = 8
CHUNK = 128
VMEM_LIMIT = 52 * 1024 * 1024

GDN_W = GDN_HEADS * GDN_D
CONV_CH = 3 * GDN_W
SWA_Q_W = SWA_HEADS * SWA_HD
SWA_KV_W = SWA_KV_HEADS * SWA_HD
MEM_W = MEM_HEADS * MEM_HD

OFF_Z = CONV_CH
OFF_GG = OFF_Z + GDN_W


def _cparams(semantics):
    return pltpu.CompilerParams(dimension_semantics=semantics, vmem_limit_bytes=VMEM_LIMIT)


def _dot(a, b):
    return jnp.dot(a, b, preferred_element_type=f32)


def _dot_nt(a, b):
    return lax.dot_general(a, b, (((1,), (1,)), ((), ())), preferred_element_type=f32)


def _rmsnorm(x, g):
    return x * lax.rsqrt(jnp.mean(x * x, -1, keepdims=True) + EPS) * g


def _silu(x):
    return x * jax.nn.sigmoid(x)


def _softplus(x):
    return jnp.maximum(x, 0.0) + jnp.log1p(jnp.exp(-jnp.abs(x)))


def _ffn_kernel(x_ref, g_ref, w1_ref, w3_ref, w2_ref, gf_ref, o_ref, h_ref, acc_ref, *, final_norm):
    j = pl.program_id(1)

    @pl.when(j == 0)
    def _():
        h_ref[...] = _rmsnorm(x_ref[...], g_ref[...]).astype(bf16)
        acc_ref[...] = jnp.zeros_like(acc_ref)

    h = h_ref[...]
    a = _dot(h, w1_ref[...])
    b = _dot(h, w3_ref[...])
    acc_ref[...] += _dot((_silu(a) * b).astype(bf16), w2_ref[...])

    @pl.when(j == pl.num_programs(1) - 1)
    def _():
        y = x_ref[...] + 0.5 * acc_ref[...]
        if final_norm:
            y = _rmsnorm(y, gf_ref[...])
        o_ref[...] = y


def _ffn(x, g, w1, w3, w2, gf, final_norm):
    M, D = x.shape
    F = w1.shape[1]
    tm = min(512, M)
    tf = 512
    assert M % tm == 0 and F % tf == 0
    return pl.pallas_call(
        functools.partial(_ffn_kernel, final_norm=final_norm),
        grid=(M // tm, F // tf),
        in_specs=[
            pl.BlockSpec((tm, D), lambda i, j: (i, 0)),
            pl.BlockSpec((1, D), lambda i, j: (0, 0)),
            pl.BlockSpec((D, tf), lambda i, j: (0, j)),
            pl.BlockSpec((D, tf), lambda i, j: (0, j)),
            pl.BlockSpec((tf, D), lambda i, j: (j, 0)),
            pl.BlockSpec((1, D), lambda i, j: (0, 0)),
        ],
        out_specs=pl.BlockSpec((tm, D), lambda i, j: (i, 0)),
        out_shape=jax.ShapeDtypeStruct((M, D), f32),
        scratch_shapes=[pltpu.VMEM((tm, D), bf16), pltpu.VMEM((tm, D), f32)],
        compiler_params=_cparams(("parallel", "arbitrary")),
        name="ffn",
    )(x, g, w1, w3, w2, gf)


def _norm_matmul_kernel(x_ref, g_ref, w_ref, o_ref, h_ref):
    @pl.when(pl.program_id(1) == 0)
    def _():
        h_ref[...] = _rmsnorm(x_ref[...], g_ref[...]).astype(bf16)

    o_ref[...] = _dot(h_ref[...], w_ref[...])


def _norm_matmul(x, g, w, tn):
    M, D = x.shape
    N = w.shape[1]
    tm = min(1024, M)
    assert M % tm == 0 and N % tn == 0
    return pl.pallas_call(
        _norm_matmul_kernel,
        grid=(M // tm, N // tn),
        in_specs=[
            pl.BlockSpec((tm, D), lambda i, j: (i, 0)),
            pl.BlockSpec((1, D), lambda i, j: (0, 0)),
            pl.BlockSpec((D, tn), lambda i, j: (0, j)),
        ],
        out_specs=pl.BlockSpec((tm, tn), lambda i, j: (i, j)),
        out_shape=jax.ShapeDtypeStruct((M, N), f32),
        scratch_shapes=[pltpu.VMEM((tm, D), bf16)],
        compiler_params=_cparams(("parallel", "arbitrary")),
        name="norm_matmul",
    )(x, g, w)


def _split_bf16(a):
    hi = a.astype(bf16)
    lo = (a - hi.astype(f32)).astype(bf16)
    return hi, lo


def _dot_split(a, b):
    ah, al = _split_bf16(a)
    bh, bl = _split_bf16(b)
    return _dot(ah, bh) + (_dot(ah, bl) + _dot(al, bh))


def _unit_lower_inverse(nmat, row, col):
    eye = jnp.where(row == col, 1.0, 0.0).astype(f32)

    def same_block(size):
        return (row // size) == (col // size)

    blk = same_block(SUBLANES)
    a1 = jnp.where(blk, nmat, 0.0)
    a2 = _dot_split(a1, a1)
    a4 = _dot_split(a2, a2)
    t = _dot_split(_dot_split(eye - a1, eye + a2), eye + a4)
    size = SUBLANES
    while size < CHUNK:
        nxt = same_block(2 * size)
        off = jnp.where(jnp.logical_and(nxt, jnp.logical_not(blk)), nmat, 0.0)
        t = t - _dot_split(t, _dot_split(off, t))
        blk = nxt
        size *= 2
    return t


def _conv_silu_slab(cur, prev8, w, row8):
    y = cur * w[GDN_CONV - 1 : GDN_CONV, :]
    y_top = y[0:SUBLANES]
    for i in range(1, GDN_CONV):
        wi = w[GDN_CONV - 1 - i : GDN_CONV - i, :]
        shifted = pltpu.roll(cur, i, 0)
        y = y + shifted * wi
        top = jnp.where(row8 < i, pltpu.roll(prev8, i, 0), shifted[0:SUBLANES])
        y_top = y_top + top * wi
    y = jnp.concatenate([y_top, y[SUBLANES:]], axis=0)
    return _silu(y)


def _gdn_prompt_kernel(qkv_ref, z_ref, ba_ref, cw_ref, av_ref, dv_ref, gn_ref,
                       og_ref, so_ref, s_ref, carry_ref, q_ref, k_ref, v_ref):
    c = pl.program_id(1)

    @pl.when(c == 0)
    def _():
        s_ref[...] = jnp.zeros_like(s_ref)
        carry_ref[...] = jnp.zeros_like(carry_ref)

    row8 = lax.broadcasted_iota(jnp.int32, (SUBLANES, LANES), 0)
    for j in range(3 * GDN_HEADS):
        sl = slice(j * LANES, (j + 1) * LANES)
        y = _conv_silu_slab(qkv_ref[:, sl], carry_ref[:, sl], cw_ref[:, sl], row8)
        h = j % GDN_HEADS
        if j < 2 * GDN_HEADS:
            y = y * lax.rsqrt(jnp.sum(y * y, -1, keepdims=True) + L2_EPS)
            if j < GDN_HEADS:
                q_ref[h] = y * (GDN_D ** -0.5)
            else:
                k_ref[h] = y
        else:
            v_ref[h] = y
    carry_ref[...] = qkv_ref[CHUNK - SUBLANES : CHUNK, :]

    row = lax.broadcasted_iota(jnp.int32, (CHUNK, CHUNK), 0)
    col = lax.broadcasted_iota(jnp.int32, (CHUNK, CHUNK), 1)
    incl = row >= col
    strict = row > col

    ba = ba_ref[...]
    beta_t = jax.nn.sigmoid(ba)
    g_t = -jnp.exp(av_ref[...]) * _softplus(ba + dv_ref[...])
    gcum = g_t
    shift = 1
    while shift < CHUNK:
        gcum = gcum + jnp.where(row >= shift, pltpu.roll(gcum, shift, 0), 0.0)
        shift *= 2
    gcum_t = gcum.T

    for h in range(GDN_HEADS):
        q = q_ref[h]
        k = k_ref[h]
        v = v_ref[h]
        gc = gcum[:, GDN_HEADS + h : GDN_HEADS + h + 1]
        gr = gcum_t[GDN_HEADS + h : GDN_HEADS + h + 1, :]
        beta = beta_t[:, h : h + 1]
        e = jnp.exp(jnp.where(incl, gc - gr, 0.0))
        dec_incl = jnp.where(incl, e, 0.0)
        dec_strict = jnp.where(strict, e, 0.0)
        kb = k.astype(bf16)
        kk = _dot_nt(kb, kb)
        tinv = _unit_lower_inverse(beta * kk * dec_strict, row, col)
        gamma = jnp.exp(gc)
        rhs = jnp.concatenate([beta * v, (beta * gamma) * k], axis=-1)
        sol = _dot_split(tinv, rhs)
        u_v = sol[:, :GDN_D]
        w_k = sol[:, GDN_D:]
        s = s_ref[h]
        sb = s.astype(bf16)
        u = u_v - _dot(w_k.astype(bf16), sb)
        ub = u.astype(bf16)
        qk = _dot_nt(q.astype(bf16), kb) * dec_incl
        o = _dot((q * gamma).astype(bf16), sb) + _dot(qk.astype(bf16), ub)
        g_last = gc[CHUNK - 1 : CHUNK, :]
        k_end = k * jnp.exp(g_last - gc)
        s_ref[h] = jnp.exp(g_last) * s + _dot(k_end.T.astype(bf16), ub)
        zh = z_ref[:, h * GDN_D : (h + 1) * GDN_D]
        og_ref[:, h * GDN_D : (h + 1) * GDN_D] = (_rmsnorm(o, gn_ref[...]) * _silu(zh)).astype(bf16)

    @pl.when(c == pl.num_programs(1) - 1)
    def _():
        so_ref[0] = s_ref[...]


def _gdn_prompt(proj, B, T, off_ba, cw, avec, dvec, gn):
    assert T % CHUNK == 0
    nc = T // CHUNK
    return pl.pallas_call(
        _gdn_prompt_kernel,
        grid=(B, nc),
        in_specs=[
            pl.BlockSpec((CHUNK, CONV_CH), lambda b, c: (b * nc + c, 0)),
            pl.BlockSpec((CHUNK, GDN_W), lambda b, c: (b * nc + c, OFF_Z // GDN_W)),
            pl.BlockSpec((CHUNK, LANES), lambda b, c: (b * nc + c, off_ba // LANES)),
            pl.BlockSpec((GDN_CONV, CONV_CH), lambda b, c: (0, 0)),
            pl.BlockSpec((1, LANES), lambda b, c: (0, 0)),
            pl.BlockSpec((1, LANES), lambda b, c: (0, 0)),
            pl.BlockSpec((1, GDN_D), lambda b, c: (0, 0)),
        ],
        out_specs=[
            pl.BlockSpec((CHUNK, GDN_W), lambda b, c: (b * nc + c, 0)),
            pl.BlockSpec((1, GDN_HEADS, GDN_D, GDN_D), lambda b, c: (b, 0, 0, 0)),
        ],
        out_shape=[
            jax.ShapeDtypeStruct((B * T, GDN_W), bf16),
            jax.ShapeDtypeStruct((B, GDN_HEADS, GDN_D, GDN_D), f32),
        ],
        scratch_shapes=[
            pltpu.VMEM((GDN_HEADS, GDN_D, GDN_D), f32),
            pltpu.VMEM((SUBLANES, CONV_CH), f32),
            pltpu.VMEM((GDN_HEADS, CHUNK, GDN_D), f32),
            pltpu.VMEM((GDN_HEADS, CHUNK, GDN_D), f32),
            pltpu.VMEM((GDN_HEADS, CHUNK, GDN_D), f32),
        ],
        compiler_params=_cparams(("parallel", "arbitrary")),
        name="gdn_prompt",
    )(proj, proj, proj, cw, avec, dvec, gn)


def _gdn_step_kernel(qkv_ref, z_ref, ba_ref, sc_ref, s0_ref, cw_ref, av_ref, dv_ref, gn_ref, og_ref, so_ref):
    x_new = qkv_ref[0]
    taps = cw_ref[...]
    y = jnp.sum(sc_ref[0] * taps[0 : GDN_CONV - 1, :], axis=0, keepdims=True) + x_new * taps[GDN_CONV - 1 : GDN_CONV, :]
    y = _silu(y)
    ba = ba_ref[0]
    beta_t = jax.nn.sigmoid(ba)
    gamma_t = jnp.exp(-jnp.exp(av_ref[...]) * _softplus(ba + dv_ref[...]))
    z = z_ref[0]
    for h in range(GDN_HEADS):
        q = y[:, h * GDN_D : (h + 1) * GDN_D]
        k = y[:, GDN_W + h * GDN_D : GDN_W + (h + 1) * GDN_D]
        v = y[:, 2 * GDN_W + h * GDN_D : 2 * GDN_W + (h + 1) * GDN_D]
        q = q * lax.rsqrt(jnp.sum(q * q, -1, keepdims=True) + L2_EPS) * (GDN_D ** -0.5)
        k = k * lax.rsqrt(jnp.sum(k * k, -1, keepdims=True) + L2_EPS)
        beta = beta_t[:, h : h + 1]
        gamma = gamma_t[:, GDN_HEADS + h : GDN_HEADS + h + 1]
        s = s0_ref[0, h]
        k_col = jnp.broadcast_to(k, (GDN_D, GDN_D)).T
        q_col = jnp.broadcast_to(q, (GDN_D, GDN_D)).T
        k_s = jnp.sum(k_col * s, axis=0, keepdims=True)
        q_s = jnp.sum(q_col * s, axis=0, keepdims=True)
        u = beta * v - (beta * gamma) * k_s
        o = gamma * q_s + jnp.sum(q * k, -1, keepdims=True) * u
        so_ref[0, h] = gamma * s + k_col * u
        zh = z[:, h * GDN_D : (h + 1) * GDN_D]
        og_ref[0, :, h * GDN_D : (h + 1) * GDN_D] = (_rmsnorm(o, gn_ref[...]) * _silu(zh)).astype(bf16)


def _gdn_step(proj3, off_ba, state_conv, state_gdn, cw, avec, dvec, gn):
    B = proj3.shape[0]
    return pl.pallas_call(
        _gdn_step_kernel,
        grid=(B,),
        in_specs=[
            pl.BlockSpec((1, 1, CONV_CH), lambda b: (b, 0, 0)),
            pl.BlockSpec((1, 1, GDN_W), lambda b: (b, 0, OFF_Z // GDN_W)),
            pl.BlockSpec((1, 1, LANES), lambda b: (b, 0, off_ba // LANES)),
            pl.BlockSpec((1, GDN_CONV - 1, CONV_CH), lambda b: (b, 0, 0)),
            pl.BlockSpec((1, GDN_HEADS, GDN_D, GDN_D), lambda b: (b, 0, 0, 0)),
            pl.BlockSpec((GDN_CONV, CONV_CH), lambda b: (0, 0)),
            pl.BlockSpec((1, LANES), lambda b: (0, 0)),
            pl.BlockSpec((1, LANES), lambda b: (0, 0)),
            pl.BlockSpec((1, GDN_D), lambda b: (0, 0)),
        ],
        out_specs=[
            pl.BlockSpec((1, 1, GDN_W), lambda b: (b, 0, 0)),
            pl.BlockSpec((1, GDN_HEADS, GDN_D, GDN_D), lambda b: (b, 0, 0, 0)),
        ],
        out_shape=[
            jax.ShapeDtypeStruct((B, 1, GDN_W), bf16),
            jax.ShapeDtypeStruct((B, GDN_HEADS, GDN_D, GDN_D), f32),
        ],
        compiler_params=_cparams(("parallel",)),
        name="gdn_step",
    )(proj3, proj3, proj3, state_conv, state_gdn, cw, avec, dvec, gn)


def _rope(x, cos, sin_signed):
    width = x.shape[-1]
    lane = lax.broadcasted_iota(jnp.int32, x.shape, x.ndim - 1)
    first_half = (lane % SWA_HD) < (SWA_HD // 2)
    rot = jnp.where(first_half, pltpu.roll(x, width - SWA_HD // 2, x.ndim - 1), pltpu.roll(x, SWA_HD // 2, x.ndim - 1))
    return x * cos + rot * sin_signed


def _head_halves(x2, head_parity, lane):
    swapped = pltpu.roll(x2, SWA_HD, 1)
    lo_src, hi_src = (x2, swapped) if head_parity == 0 else (swapped, x2)
    return jnp.where(lane < SWA_HD, lo_src, 0.0), jnp.where(lane >= SWA_HD, hi_src, 0.0)


def _swa_prompt_kernel(sinks_ref, q_ref, kv_ref, cos_ref, sin_ref, os_ref, kc_ref, kprev_ref, vprev_ref):
    n = pl.program_id(1)

    @pl.when(n == 0)
    def _():
        kprev_ref[...] = jnp.zeros_like(kprev_ref)
        vprev_ref[...] = jnp.zeros_like(vprev_ref)

    cos = cos_ref[...]
    sin = sin_ref[...]
    kv = kv_ref[...]
    k_cur = _rope(kv[:, :SWA_KV_W], cos, sin)
    v_cur = kv[:, SWA_KV_W:]
    kc_ref[0] = k_cur
    k_all = jnp.concatenate([kprev_ref[...], k_cur], axis=0)
    v_all = jnp.concatenate([vprev_ref[...], v_cur], axis=0)
    kprev_ref[...] = k_cur
    vprev_ref[...] = v_cur

    row = lax.broadcasted_iota(jnp.int32, (WINDOW, 2 * WINDOW), 0)
    col = lax.broadcasted_iota(jnp.int32, (WINDOW, 2 * WINDOW), 1)
    prev_visible = jnp.logical_and(jnp.logical_and(col > row, col < WINDOW), n > 0)
    own_visible = jnp.logical_and(col >= WINDOW, (col - WINDOW) <= row)
    mask = jnp.logical_or(prev_visible, own_visible)
    lane = lax.broadcasted_iota(jnp.int32, (2 * WINDOW, LANES), 1)
    scale = SWA_HD ** -0.5
    group = SWA_HEADS // SWA_KV_HEADS
    for h in range(SWA_KV_HEADS):
        pair = slice((h // 2) * LANES, (h // 2 + 1) * LANES)
        k_lo, k_hi = _head_halves(k_all[:, pair], h % 2, lane)
        v_lo, v_hi = _head_halves(v_all[:, pair], h % 2, lane)
        k_lo, k_hi, v_lo, v_hi = (a.astype(bf16) for a in (k_lo, k_hi, v_lo, v_hi))
        q_h = _rope(q_ref[:, h * group * SWA_HD : (h + 1) * group * SWA_HD], cos, sin)
        for j in range(group // 2):
            q2 = q_h[:, j * LANES : (j + 1) * LANES].astype(bf16)
            out = None
            for par, (k_half, v_half) in enumerate(((k_lo, v_lo), (k_hi, v_hi))):
                sink = sinks_ref[h * group + 2 * j + par]
                s = jnp.where(mask, _dot_nt(q2, k_half) * scale, -jnp.inf)
                m = jnp.maximum(jnp.max(s, -1, keepdims=True), sink)
                p = jnp.exp(s - m)
                denom = jnp.sum(p, -1, keepdims=True) + jnp.exp(sink - m)
                o = _dot((p / denom).astype(bf16), v_half)
                out = o if out is None else out + o
            lo = (h * group + 2 * j) * SWA_HD
            os_ref[:, lo : lo + LANES] = out.astype(bf16)


def _swa_prompt(proj, B, T, off_q, off_kv, cos, sin, sinks):
    assert T % WINDOW == 0 and SWA_KV_W == 2 * LANES
    nb = T // WINDOW
    return pl.pallas_call(
        _swa_prompt_kernel,
        grid=(B, nb),
        in_specs=[
            pl.BlockSpec(memory_space=pltpu.SMEM),
            pl.BlockSpec((WINDOW, SWA_Q_W), lambda b, n: (b * nb + n, off_q // SWA_Q_W)),
            pl.BlockSpec((WINDOW, 2 * SWA_KV_W), lambda b, n: (b * nb + n, off_kv // (2 * SWA_KV_W))),
            pl.BlockSpec((WINDOW, SWA_KV_W), lambda b, n: (n, 0)),
            pl.BlockSpec((WINDOW, SWA_KV_W), lambda b, n: (n, 0)),
        ],
        out_specs=[
            pl.BlockSpec((WINDOW, SWA_Q_W), lambda b, n: (b * nb + n, 0)),
            pl.BlockSpec((1, WINDOW, SWA_KV_W), lambda b, n: (b, 0, 0)),
        ],
        out_shape=[
            jax.ShapeDtypeStruct((B * T, SWA_Q_W), bf16),
            jax.ShapeDtypeStruct((B, WINDOW, SWA_KV_W), f32),
        ],
        scratch_shapes=[pltpu.VMEM((WINDOW, SWA_KV_W), f32), pltpu.VMEM((WINDOW, SWA_KV_W), f32)],
        compiler_params=_cparams(("parallel", "arbitrary")),
        name="swa_prompt",
    )(sinks, proj, proj, cos, sin)


SWA_STEP_BATCH = 8


def _swa_step_kernel(qe_ref, kv_ref, ck_ref, cv_ref, cos_ref, sin_ref, sinks_ref, r_ref, nk_ref, nv_ref):
    cos = cos_ref[...]
    sin = sin_ref[...]
    sink = sinks_ref[...]
    row = lax.broadcasted_iota(jnp.int32, (WINDOW, SWA_KV_W), 0)
    scale = SWA_HD ** -0.5
    for i in range(qe_ref.shape[0]):
        kv = kv_ref[i]
        k_new = _rope(kv[:, :SWA_KV_W], cos, sin)
        v_new = kv[:, SWA_KV_W:]
        keys = jnp.where(row == WINDOW - 1, k_new, pltpu.roll(ck_ref[i], WINDOW - 1, 0))
        vals = jnp.where(row == WINDOW - 1, v_new, pltpu.roll(cv_ref[i], WINDOW - 1, 0))
        nk_ref[i] = keys
        nv_ref[i] = vals
        q = _rope(qe_ref[i], cos, sin)
        s = _dot_nt(q.astype(bf16), keys.astype(bf16)) * scale
        m = jnp.maximum(jnp.max(s, -1, keepdims=True), sink)
        p = jnp.exp(s - m)
        denom = jnp.sum(p, -1, keepdims=True) + jnp.exp(sink - m)
        r_ref[i] = _dot((p / denom).astype(bf16), vals.astype(bf16))


def _swa_step(q_exp, proj3, off_kv, cache_k, cache_v, cos, sin, sinks_col):
    B = q_exp.shape[0]
    bb = math.gcd(B, SWA_STEP_BATCH)
    assert cache_k.shape[1] == WINDOW
    return pl.pallas_call(
        _swa_step_kernel,
        grid=(B // bb,),
        in_specs=[
            pl.BlockSpec((bb, SWA_HEADS, SWA_KV_W), lambda i: (i, 0, 0)),
            pl.BlockSpec((bb, 1, 2 * SWA_KV_W), lambda i: (i, 0, off_kv // (2 * SWA_KV_W))),
            pl.BlockSpec((bb, WINDOW, SWA_KV_W), lambda i: (i, 0, 0)),
            pl.BlockSpec((bb, WINDOW, SWA_KV_W), lambda i: (i, 0, 0)),
            pl.BlockSpec((1, SWA_KV_W), lambda i: (0, 0)),
            pl.BlockSpec((1, SWA_KV_W), lambda i: (0, 0)),
            pl.BlockSpec((SWA_HEADS, 1), lambda i: (0, 0)),
        ],
        out_specs=[
            pl.BlockSpec((bb, SWA_HEADS, SWA_KV_W), lambda i: (i, 0, 0)),
            pl.BlockSpec((bb, WINDOW, SWA_KV_W), lambda i: (i, 0, 0)),
            pl.BlockSpec((bb, WINDOW, SWA_KV_W), lambda i: (i, 0, 0)),
        ],
        out_shape=[
            jax.ShapeDtypeStruct((B, SWA_HEADS, SWA_KV_W), f32),
            jax.ShapeDtypeStruct((B, WINDOW, SWA_KV_W), f32),
            jax.ShapeDtypeStruct((B, WINDOW, SWA_KV_W), f32),
        ],
        compiler_params=_cparams(("parallel",)),
        name="swa_step",
    )(q_exp, proj3, cache_k, cache_v, cos, sin, sinks_col)


def _merge_kernel(og_ref, os_ref, gg_ref, gs_ref, x_ref, wg_ref, ws_ref, wo_ref, gq_ref, wq_ref, xo_ref, qm_ref):
    p_gdn = _dot(og_ref[...], wg_ref[...])
    p_swa = _dot(os_ref[...], ws_ref[...])
    merged = jax.nn.sigmoid(gg_ref[...]) * p_gdn + jax.nn.sigmoid(gs_ref[...]) * p_swa
    x_new = x_ref[...] + _dot(merged.astype(bf16), wo_ref[...])
    xo_ref[...] = x_new
    qm_ref[...] = _dot(_rmsnorm(x_new, gq_ref[...]).astype(bf16), wq_ref[...])


def _merge(og, os_, proj, off_gg, x, wg, ws, wo, gq, wq):
    M, D = x.shape
    tm = min(256, M)
    assert M % tm == 0 and off_gg % D == 0
    const = lambda shape: pl.BlockSpec(shape, lambda i: (0, 0), pipeline_mode=pl.Buffered(1))
    return pl.pallas_call(
        _merge_kernel,
        grid=(M // tm,),
        in_specs=[
            pl.BlockSpec((tm, GDN_W), lambda i: (i, 0)),
            pl.BlockSpec((tm, SWA_Q_W), lambda i: (i, 0)),
            pl.BlockSpec((tm, D), lambda i: (i, off_gg // D)),
            pl.BlockSpec((tm, D), lambda i: (i, off_gg // D + 1)),
            pl.BlockSpec((tm, D), lambda i: (i, 0)),
            const((GDN_W, D)),
            const((SWA_Q_W, D)),
            const((D, D)),
            const((1, D)),
            const((D, MEM_W)),
        ],
        out_specs=[
            pl.BlockSpec((tm, D), lambda i: (i, 0)),
            pl.BlockSpec((tm, MEM_W), lambda i: (i, 0)),
        ],
        out_shape=[jax.ShapeDtypeStruct((M, D), f32), jax.ShapeDtypeStruct((M, MEM_W), f32)],
        compiler_params=_cparams(("parallel",)),
        name="merge",
    )(og, os_, proj, proj, x, wg, ws, wo, gq, wq)


def _mem_attn_kernel(q_ref, k_ref, v_ref, x_ref, wo_ref, o_ref):
    tq = q_ref.shape[1]
    q = q_ref[0]
    if tq < SUBLANES:
        q = jnp.broadcast_to(q[0:1], (SUBLANES, q.shape[-1]))
    q = q.astype(bf16)
    scale = MEM_HD ** -0.5
    outs = []
    for h in range(MEM_HEADS):
        sl = slice(h * MEM_HD, (h + 1) * MEM_HD)
        s = _dot_nt(q[:, sl], k_ref[0, :, sl].astype(bf16)) * scale
        m = jnp.max(s, -1, keepdims=True)
        p = jnp.exp(s - m)
        p = p / jnp.sum(p, -1, keepdims=True)
        outs.append(_dot(p.astype(bf16), v_ref[0, :, sl].astype(bf16)))
    o = _dot(jnp.concatenate(outs, axis=-1).astype(bf16), wo_ref[...])
    o_ref[0] = x_ref[0] + o[0:tq]


def _mem_attn(qm, mem_k, mem_v, x, wo):
    B, T, D = x.shape
    mt = mem_k.shape[1]
    tq = min(512, T)
    assert T % tq == 0
    return pl.pallas_call(
        _mem_attn_kernel,
        grid=(B, T // tq),
        in_specs=[
            pl.BlockSpec((1, tq, MEM_W), lambda b, t: (b, t, 0)),
            pl.BlockSpec((1, mt, MEM_W), lambda b, t: (b, 0, 0)),
            pl.BlockSpec((1, mt, MEM_W), lambda b, t: (b, 0, 0)),
            pl.BlockSpec((1, tq, D), lambda b, t: (b, t, 0)),
            pl.BlockSpec((MEM_W, D), lambda b, t: (0, 0)),
        ],
        out_specs=pl.BlockSpec((1, tq, D), lambda b, t: (b, t, 0)),
        out_shape=jax.ShapeDtypeStruct((B, T, D), f32),
        compiler_params=_cparams(("parallel", "parallel")),
        name="mem_attn",
    )(qm, mem_k, mem_v, x, wo)


def _rope_tables(pos):
    half = SWA_HD // 2
    inv_freq = ROPE_THETA ** (-jnp.arange(half, dtype=f32) / half)
    ang = pos.astype(f32)[:, None] * inv_freq[None, :]
    cos = jnp.cos(ang)
    sin = jnp.sin(ang)
    reps = SWA_KV_W // SWA_HD
    return jnp.tile(jnp.concatenate([cos, cos], -1), (1, reps)), jnp.tile(jnp.concatenate([-sin, sin], -1), (1, reps))


def kernel(x_prompt, x_sample, state_gdn, state_conv, cache_swa_k, cache_swa_v, cache_mem_k, cache_mem_v, mem_prompt, norm_ffn1, ffn1_w1, ffn1_w3, ffn1_w2, norm_mix, w_in, conv_w, gdn_A_log, gdn_dt_bias, gdn_norm, swa_sinks, w_br_gdn, w_br_swa, w_out, norm_mem_q, norm_mem_kv, w_mem_q, w_mem_k, w_mem_v, w_mem_o, norm_ffn2, ffn2_w1, ffn2_w3, ffn2_w2, norm_final):
    Bp, Tp, D = x_prompt.shape
    Bs, Ts, _ = x_sample.shape
    assert Ts == 1
    depth = norm_ffn1.shape[0]
    n_mem = mem_prompt.shape[1]
    group = SWA_HEADS // SWA_KV_HEADS

    off_gs = OFF_GG + D
    off_q = off_gs + D
    off_kv = off_q + SWA_Q_W
    off_ba = off_kv + 2 * SWA_KV_W
    d_in_pad = off_ba + LANES
    o_b = CONV_CH + GDN_W
    o_q = o_b + 2 * GDN_HEADS
    o_gg = o_q + SWA_Q_W + 2 * SWA_KV_W

    cos_p, sin_p = _rope_tables(jnp.arange(Tp, dtype=jnp.int32))
    cos_s, sin_s = _rope_tables(PAST_LEN + jnp.arange(Ts, dtype=jnp.int32))
    eye_kv = jnp.eye(SWA_KV_HEADS, dtype=f32)
    row = lambda v: v.reshape(1, -1)

    hp = x_prompt.reshape(Bp * Tp, D)
    hs = x_sample.reshape(Bs, D)
    outs = [[] for _ in range(10)]
    for l in range(depth):
        wi = w_in[l]
        w_in_r = jnp.concatenate(
            [wi[:, :o_b], wi[:, o_gg:], wi[:, o_q:o_gg], wi[:, o_b:o_q], jnp.zeros((D, LANES - 2 * GDN_HEADS), f32)], axis=1
        ).astype(bf16)
        assert w_in_r.shape[1] == d_in_pad
        ffn1 = (row(norm_ffn1[l]), ffn1_w1[l].astype(bf16), ffn1_w3[l].astype(bf16), ffn1_w2[l].astype(bf16))
        ffn2 = (row(norm_ffn2[l]), ffn2_w1[l].astype(bf16), ffn2_w3[l].astype(bf16), ffn2_w2[l].astype(bf16))
        last = l == depth - 1
        gfin = row(norm_final)
        avec = jnp.zeros((1, LANES), f32).at[0, GDN_HEADS : 2 * GDN_HEADS].set(gdn_A_log[l])
        dvec = jnp.zeros((1, LANES), f32).at[0, GDN_HEADS : 2 * GDN_HEADS].set(gdn_dt_bias[l])
        gdn_common = (conv_w[l], avec, dvec, row(gdn_norm[l]))
        merge_w = (w_br_gdn[l].astype(bf16), w_br_swa[l].astype(bf16), w_out[l].astype(bf16),
                   row(norm_mem_q[l]), w_mem_q[l].astype(bf16))
        wmo = w_mem_o[l].astype(bf16)
        tn_in = LANES * 7 if d_in_pad % (LANES * 7) == 0 else LANES

        x1 = _ffn(hp, *ffn1, gfin, False)
        proj = _norm_matmul(x1, row(norm_mix[l]), w_in_r, tn_in)
        og, s_new = _gdn_prompt(proj, Bp, Tp, off_ba, *gdn_common)
        os_, kc = _swa_prompt(proj, Bp, Tp, off_q, off_kv, cos_p, sin_p, swa_sinks[l])
        x2, qm = _merge(og, os_, proj, OFF_GG, x1, *merge_w)
        mem_x = mem_prompt.reshape(Bp * n_mem, D)
        mk = _norm_matmul(mem_x, row(norm_mem_kv[l]), w_mem_k[l].astype(bf16), MEM_W)
        mv = _norm_matmul(mem_x, row(norm_mem_kv[l]), w_mem_v[l].astype(bf16), MEM_W)
        x3 = _mem_attn(qm.reshape(Bp, Tp, MEM_W), mk.reshape(Bp, n_mem, MEM_W), mv.reshape(Bp, n_mem, MEM_W),
                       x2.reshape(Bp, Tp, D), wmo)
        hp = _ffn(x3.reshape(Bp * Tp, D), *ffn2, gfin, last)
        proj_b = proj.reshape(Bp, Tp, d_in_pad)
        outs[0].append(s_new)
        outs[1].append(proj_b[:, Tp - (GDN_CONV - 1) :, :CONV_CH])
        outs[2].append(kc.reshape(Bp, WINDOW, SWA_KV_HEADS, SWA_HD))
        outs[3].append(proj_b[:, Tp - WINDOW :, off_kv + SWA_KV_W : off_kv + 2 * SWA_KV_W].reshape(Bp, WINDOW, SWA_KV_HEADS, SWA_HD))
        outs[4].append(mk.reshape(Bp, n_mem, MEM_HEADS, MEM_HD))
        outs[5].append(mv.reshape(Bp, n_mem, MEM_HEADS, MEM_HD))

        x1 = _ffn(hs, *ffn1, gfin, False)
        proj = _norm_matmul(x1, row(norm_mix[l]), w_in_r, tn_in)
        proj3 = proj.reshape(Bs, 1, d_in_pad)
        og, s_new = _gdn_step(proj3, off_ba, state_conv[l], state_gdn[l], *gdn_common)
        q_raw = proj[:, off_q : off_q + SWA_Q_W].reshape(Bs, SWA_KV_HEADS, group, 1, SWA_HD)
        q_exp = (q_raw * eye_kv[None, :, None, :, None]).reshape(Bs, SWA_HEADS, SWA_KV_W)
        ck = cache_swa_k[l].reshape(Bs, WINDOW, SWA_KV_W)
        cv = cache_swa_v[l].reshape(Bs, WINDOW, SWA_KV_W)
        r, nk, nv = _swa_step(q_exp, proj3, off_kv, ck, cv, cos_s, sin_s, swa_sinks[l].reshape(SWA_HEADS, 1))
        r5 = r.reshape(Bs, SWA_KV_HEADS, group, SWA_KV_HEADS, SWA_HD)
        kvh = jnp.arange(SWA_KV_HEADS)
        os_ = jnp.transpose(r5[:, kvh, :, kvh, :], (1, 0, 2, 3)).reshape(Bs, SWA_Q_W).astype(bf16)
        x2, qm = _merge(og.reshape(Bs, GDN_W), os_, proj, OFF_GG, x1, *merge_w)
        x3 = _mem_attn(qm.reshape(Bs, 1, MEM_W), cache_mem_k[l].reshape(Bs, n_mem, MEM_W),
                       cache_mem_v[l].reshape(Bs, n_mem, MEM_W), x2.reshape(Bs, 1, D), wmo)
        hs = _ffn(x3.reshape(Bs, D), *ffn2, gfin, last)
        outs[6].append(s_new)
        outs[7].append(jnp.concatenate([state_conv[l][:, 1:], proj3[:, :, :CONV_CH]], axis=1))
        outs[8].append(nk.reshape(Bs, WINDOW, SWA_KV_HEADS, SWA_HD))
        outs[9].append(nv.reshape(Bs, WINDOW, SWA_KV_HEADS, SWA_HD))

    return (hp.reshape(Bp, Tp, D), hs.reshape(Bs, Ts, D), *(jnp.stack(o) for o in outs))
```

```python
import functools
import math

import jax
import jax.numpy as jnp
from jax import lax
from jax.experimental import pallas as pl
from jax.experimental.pallas import tpu as pltpu

f32 = jnp.float32
bf16 = jnp.bfloat16

PAST_LEN = 16384
GDN_HEADS = 8
GDN_D = 128
GDN_CONV = 4
SWA_HEADS = 16
SWA_KV_HEADS = 4
SWA_HD = 64
WINDOW = 128
ROPE_THETA = 10000.0
MEM_HEADS = 4
MEM_HD = 128
EPS = 1e-6
L2_EPS = 1e-6

LANES = 128
SUBLANES = 8
CHUNK = 128
VMEM_LIMIT = 52 * 1024 * 1024

GDN_W = GDN_HEADS * GDN_D
CONV_CH = 3 * GDN_W
SWA_Q_W = SWA_HEADS * SWA_HD
SWA_KV_W = SWA_KV_HEADS * SWA_HD
MEM_W = MEM_HEADS * MEM_HD

OFF_Z = CONV_CH
OFF_GG = OFF_Z + GDN_W


def _cparams(semantics):
    return pltpu.CompilerParams(dimension_semantics=semantics, vmem_limit_bytes=VMEM_LIMIT)


def _dot(a, b):
    return jnp.dot(a, b, preferred_element_type=f32)


def _dot_nt(a, b):
    return lax.dot_general(a, b, (((1,), (1,)), ((), ())), preferred_element_type=f32)


def _rmsnorm(x, g):
    return x * lax.rsqrt(jnp.mean(x * x, -1, keepdims=True) + EPS) * g


def _silu(x):
    return x * jax.nn.sigmoid(x)


def _softplus(x):
    return jnp.maximum(x, 0.0) + jnp.log1p(jnp.exp(-jnp.abs(x)))


def _ffn_kernel(x_ref, g_ref, w1_ref, w3_ref, w2_ref, gf_ref, o_ref, h_ref, acc_ref, *, final_norm):
    j = pl.program_id(1)

    @pl.when(j == 0)
    def _():
        h_ref[...] = _rmsnorm(x_ref[...], g_ref[...]).astype(bf16)
        acc_ref[...] = jnp.zeros_like(acc_ref)

    h = h_ref[...]
    a = _dot(h, w1_ref[...])
    b = _dot(h, w3_ref[...])
    acc_ref[...] += _dot((_silu(a) * b).astype(bf16), w2_ref[...])

    @pl.when(j == pl.num_programs(1) - 1)
    def _():
        y = x_ref[...] + 0.5 * acc_ref[...]
        if final_norm:
            y = _rmsnorm(y, gf_ref[...])
        o_ref[...] = y


def _ffn(x, g, w1, w3, w2, gf, final_norm):
    M, D = x.shape
    F = w1.shape[1]
    tm = min(512, M)
    tf = 512
    assert M % tm == 0 and F % tf == 0
    return pl.pallas_call(
        functools.partial(_ffn_kernel, final_norm=final_norm),
        grid=(M // tm, F // tf),
        in_specs=[
            pl.BlockSpec((tm, D), lambda i, j: (i, 0)),
            pl.BlockSpec((1, D), lambda i, j: (0, 0)),
            pl.BlockSpec((D, tf), lambda i, j: (0, j)),
            pl.BlockSpec((D, tf), lambda i, j: (0, j)),
            pl.BlockSpec((tf, D), lambda i, j: (j, 0)),
            pl.BlockSpec((1, D), lambda i, j: (0, 0)),
        ],
        out_specs=pl.BlockSpec((tm, D), lambda i, j: (i, 0)),
        out_shape=jax.ShapeDtypeStruct((M, D), f32),
        scratch_shapes=[pltpu.VMEM((tm, D), bf16), pltpu.VMEM((tm, D), f32)],
        compiler_params=_cparams(("parallel", "arbitrary")),
        name="ffn",
    )(x, g, w1, w3, w2, gf)


def _norm_matmul_kernel(x_ref, g_ref, w_ref, o_ref, h_ref):
    @pl.when(pl.program_id(1) == 0)
    def _():
        h_ref[...] = _rmsnorm(x_ref[...], g_ref[...]).astype(bf16)

    o_ref[...] = _dot(h_ref[...], w_ref[...])


def _norm_matmul(x, g, w, tn):
    M, D = x.shape
    N = w.shape[1]
    tm = min(1024, M)
    assert M % tm == 0 and N % tn == 0
    return pl.pallas_call(
        _norm_matmul_kernel,
        grid=(M // tm, N // tn),
        in_specs=[
            pl.BlockSpec((tm, D), lambda i, j: (i, 0)),
            pl.BlockSpec((1, D), lambda i, j: (0, 0)),
            pl.BlockSpec((D, tn), lambda i, j: (0, j)),
        ],
        out_specs=pl.BlockSpec((tm, tn), lambda i, j: (i, j)),
        out_shape=jax.ShapeDtypeStruct((M, N), f32),
        scratch_shapes=[pltpu.VMEM((tm, D), bf16)],
        compiler_params=_cparams(("parallel", "arbitrary")),
        name="norm_matmul",
    )(x, g, w)


def _bdot(a, b):
    return _dot(a.astype(bf16), b.astype(bf16))


def _unit_lower_inverse(nmats, row, col):
    eye = jnp.where(row == col, 1.0, 0.0).astype(f32)

    def same_block(size):
        return (row // size) == (col // size)

    blk = same_block(SUBLANES)
    a1 = [jnp.where(blk, n, 0.0) for n in nmats]
    a2 = [_bdot(a, a) for a in a1]
    a4 = [_bdot(a, a) for a in a2]
    ts = [_bdot(eye - x1, eye + x2) for x1, x2 in zip(a1, a2)]
    ts = [_bdot(t, eye + x4) for t, x4 in zip(ts, a4)]
    size = SUBLANES
    while size < CHUNK:
        nxt = same_block(2 * size)
        sel = jnp.logical_and(nxt, jnp.logical_not(blk))
        tbs = [t.astype(bf16) for t in ts]
        xs = [_dot(jnp.where(sel, n, 0.0).astype(bf16), tb) for n, tb in zip(nmats, tbs)]
        ts = [t - _dot(tb, x.astype(bf16)) for t, tb, x in zip(ts, tbs, xs)]
        blk = nxt
        size *= 2
    return ts


def _conv_silu_slab(cur, prev8, w, row8):
    y = cur * w[GDN_CONV - 1 : GDN_CONV, :]
    y_top = y[0:SUBLANES]
    for i in range(1, GDN_CONV):
        wi = w[GDN_CONV - 1 - i : GDN_CONV - i, :]
        shifted = pltpu.roll(cur, i, 0)
        y = y + shifted * wi
        top = jnp.where(row8 < i, pltpu.roll(prev8, i, 0), shifted[0:SUBLANES])
        y_top = y_top + top * wi
    y = jnp.concatenate([y_top, y[SUBLANES:]], axis=0)
    return _silu(y)


def _gdn_prompt_kernel(qkv_ref, z_ref, ba_ref, cw_ref, av_ref, dv_ref, gn_ref,
                       og_ref, so_ref, s_ref, carry_ref, q_ref, k_ref, v_ref):
    c = pl.program_id(1)

    @pl.when(c == 0)
    def _():
        s_ref[...] = jnp.zeros_like(s_ref)
        carry_ref[...] = jnp.zeros_like(carry_ref)

    row8 = lax.broadcasted_iota(jnp.int32, (SUBLANES, LANES), 0)
    for j in range(3 * GDN_HEADS):
        sl = slice(j * LANES, (j + 1) * LANES)
        y = _conv_silu_slab(qkv_ref[:, sl], carry_ref[:, sl], cw_ref[:, sl], row8)
        h = j % GDN_HEADS
        if j < 2 * GDN_HEADS:
            y = y * lax.rsqrt(jnp.sum(y * y, -1, keepdims=True) + L2_EPS)
            if j < GDN_HEADS:
                q_ref[h] = y * (GDN_D ** -0.5)
            else:
                k_ref[h] = y
        else:
            v_ref[h] = y
    carry_ref[...] = qkv_ref[CHUNK - SUBLANES : CHUNK, :]

    row = lax.broadcasted_iota(jnp.int32, (CHUNK, CHUNK), 0)
    col = lax.broadcasted_iota(jnp.int32, (CHUNK, CHUNK), 1)
    incl = row >= col
    strict = row > col

    ba = ba_ref[...]
    beta_t = jax.nn.sigmoid(ba)
    g_t = -jnp.exp(av_ref[...]) * _softplus(ba + dv_ref[...])
    gcum = g_t
    shift = 1
    while shift < CHUNK:
        gcum = gcum + jnp.where(row >= shift, pltpu.roll(gcum, shift, 0), 0.0)
        shift *= 2
    gcum_t = gcum.T

    heads = range(GDN_HEADS)
    gc = [gcum[:, GDN_HEADS + h : GDN_HEADS + h + 1] for h in heads]
    gr = [gcum_t[GDN_HEADS + h : GDN_HEADS + h + 1, :] for h in heads]
    beta = [beta_t[:, h : h + 1] for h in heads]
    kb = [k_ref[h].astype(bf16) for h in heads]
    kk = [_dot_nt(kb[h], kb[h]) for h in heads]
    qk = [_dot_nt(q_ref[h].astype(bf16), kb[h]) for h in heads]
    e = [jnp.exp(jnp.where(incl, gc[h] - gr[h], 0.0)) for h in heads]
    nmat = [beta[h] * kk[h] * jnp.where(strict, e[h], 0.0) for h in heads]
    qkd = [(qk[h] * jnp.where(incl, e[h], 0.0)).astype(bf16) for h in heads]
    tinv = _unit_lower_inverse(nmat, row, col)
    gamma = [jnp.exp(gc[h]) for h in heads]
    rhs = [jnp.concatenate([beta[h] * v_ref[h], (beta[h] * gamma[h]) * k_ref[h]], axis=-1).astype(bf16) for h in heads]
    sol = [_dot(tinv[h].astype(bf16), rhs[h]) for h in heads]
    sb = [s_ref[h].astype(bf16) for h in heads]
    ub = [(sol[h][:, :GDN_D] - _dot(sol[h][:, GDN_D:].astype(bf16), sb[h])).astype(bf16) for h in heads]
    o = [_dot((q_ref[h] * gamma[h]).astype(bf16), sb[h]) + _dot(qkd[h], ub[h]) for h in heads]
    for h in heads:
        g_last = gc[h][CHUNK - 1 : CHUNK, :]
        k_end = k_ref[h] * jnp.exp(g_last - gc[h])
        s_ref[h] = jnp.exp(g_last) * s_ref[h] + _dot(k_end.T.astype(bf16), ub[h])
        zh = z_ref[:, h * GDN_D : (h + 1) * GDN_D]
        og_ref[:, h * GDN_D : (h + 1) * GDN_D] = (_rmsnorm(o[h], gn_ref[...]) * _silu(zh)).astype(bf16)

    @pl.when(c == pl.num_programs(1) - 1)
    def _():
        so_ref[0] = s_ref[...]


def _gdn_prompt(proj, B, T, off_ba, cw, avec, dvec, gn):
    assert T % CHUNK == 0
    nc = T // CHUNK
    return pl.pallas_call(
        _gdn_prompt_kernel,
        grid=(B, nc),
        in_specs=[
            pl.BlockSpec((CHUNK, CONV_CH), lambda b, c: (b * nc + c, 0)),
            pl.BlockSpec((CHUNK, GDN_W), lambda b, c: (b * nc + c, OFF_Z // GDN_W)),
            pl.BlockSpec((CHUNK, LANES), lambda b, c: (b * nc + c, off_ba // LANES)),
            pl.BlockSpec((GDN_CONV, CONV_CH), lambda b, c: (0, 0)),
            pl.BlockSpec((1, LANES), lambda b, c: (0, 0)),
            pl.BlockSpec((1, LANES), lambda b, c: (0, 0)),
            pl.BlockSpec((1, GDN_D), lambda b, c: (0, 0)),
        ],
        out_specs=[
            pl.BlockSpec((CHUNK, GDN_W), lambda b, c: (b * nc + c, 0)),
            pl.BlockSpec((1, GDN_HEADS, GDN_D, GDN_D), lambda b, c: (b, 0, 0, 0)),
        ],
        out_shape=[
            jax.ShapeDtypeStruct((B * T, GDN_W), bf16),
            jax.ShapeDtypeStruct((B, GDN_HEADS, GDN_D, GDN_D), f32),
        ],
        scratch_shapes=[
            pltpu.VMEM((GDN_HEADS, GDN_D, GDN_D), f32),
            pltpu.VMEM((SUBLANES, CONV_CH), f32),
            pltpu.VMEM((GDN_HEADS, CHUNK, GDN_D), f32),
            pltpu.VMEM((GDN_HEADS, CHUNK, GDN_D), f32),
            pltpu.VMEM((GDN_HEADS, CHUNK, GDN_D), f32),
        ],
        compiler_params=_cparams(("parallel", "arbitrary")),
        name="gdn_prompt",
    )(proj, proj, proj, cw, avec, dvec, gn)


def _gdn_step_kernel(qkv_ref, z_ref, ba_ref, sc_ref, s0_ref, cw_ref, av_ref, dv_ref, gn_ref, og_ref, so_ref):
    x_new = qkv_ref[0]
    taps = cw_ref[...]
    y = jnp.sum(sc_ref[0] * taps[0 : GDN_CONV - 1, :], axis=0, keepdims=True) + x_new * taps[GDN_CONV - 1 : GDN_CONV, :]
    y = _silu(y)
    ba = ba_ref[0]
    beta_t = jax.nn.sigmoid(ba)
    gamma_t = jnp.exp(-jnp.exp(av_ref[...]) * _softplus(ba + dv_ref[...]))
    z = z_ref[0]
    for h in range(GDN_HEADS):
        q = y[:, h * GDN_D : (h + 1) * GDN_D]
        k = y[:, GDN_W + h * GDN_D : GDN_W + (h + 1) * GDN_D]
        v = y[:, 2 * GDN_W + h * GDN_D : 2 * GDN_W + (h + 1) * GDN_D]
        q = q * lax.rsqrt(jnp.sum(q * q, -1, keepdims=True) + L2_EPS) * (GDN_D ** -0.5)
        k = k * lax.rsqrt(jnp.sum(k * k, -1, keepdims=True) + L2_EPS)
        beta = beta_t[:, h : h + 1]
        gamma = gamma_t[:, GDN_HEADS + h : GDN_HEADS + h + 1]
        s = s0_ref[0, h]
        k_col = jnp.broadcast_to(k, (GDN_D, GDN_D)).T
        q_col = jnp.broadcast_to(q, (GDN_D, GDN_D)).T
        k_s = jnp.sum(k_col * s, axis=0, keepdims=True)
        q_s = jnp.sum(q_col * s, axis=0, keepdims=True)
        u = beta * v - (beta * gamma) * k_s
        o = gamma * q_s + jnp.sum(q * k, -1, keepdims=True) * u
        so_ref[0, h] = gamma * s + k_col * u
        zh = z[:, h * GDN_D : (h + 1) * GDN_D]
        og_ref[0, :, h * GDN_D : (h + 1) * GDN_D] = (_rmsnorm(o, gn_ref[...]) * _silu(zh)).astype(bf16)


def _gdn_step(proj3, off_ba, state_conv, state_gdn, cw, avec, dvec, gn):
    B = proj3.shape[0]
    return pl.pallas_call(
        _gdn_step_kernel,
        grid=(B,),
        in_specs=[
            pl.BlockSpec((1, 1, CONV_CH), lambda b: (b, 0, 0)),
            pl.BlockSpec((1, 1, GDN_W), lambda b: (b, 0, OFF_Z // GDN_W)),
            pl.BlockSpec((1, 1, LANES), lambda b: (b, 0, off_ba // LANES)),
            pl.BlockSpec((1, GDN_CONV - 1, CONV_CH), lambda b: (b, 0, 0)),
            pl.BlockSpec((1, GDN_HEADS, GDN_D, GDN_D), lambda b: (b, 0, 0, 0)),
            pl.BlockSpec((GDN_CONV, CONV_CH), lambda b: (0, 0)),
            pl.BlockSpec((1, LANES), lambda b: (0, 0)),
            pl.BlockSpec((1, LANES), lambda b: (0, 0)),
            pl.BlockSpec((1, GDN_D), lambda b: (0, 0)),
        ],
        out_specs=[
            pl.BlockSpec((1, 1, GDN_W), lambda b: (b, 0, 0)),
            pl.BlockSpec((1, GDN_HEADS, GDN_D, GDN_D), lambda b: (b, 0, 0, 0)),
        ],
        out_shape=[
            jax.ShapeDtypeStruct((B, 1, GDN_W), bf16),
            jax.ShapeDtypeStruct((B, GDN_HEADS, GDN_D, GDN_D), f32),
        ],
        compiler_params=_cparams(("parallel",)),
        name="gdn_step",
    )(proj3, proj3, proj3, state_conv, state_gdn, cw, avec, dvec, gn)


def _rope(x, cos, sin_signed):
    width = x.shape[-1]
    lane = lax.broadcasted_iota(jnp.int32, x.shape, x.ndim - 1)
    first_half = (lane % SWA_HD) < (SWA_HD // 2)
    rot = jnp.where(first_half, pltpu.roll(x, width - SWA_HD // 2, x.ndim - 1), pltpu.roll(x, SWA_HD // 2, x.ndim - 1))
    return x * cos + rot * sin_signed


def _head_halves(x2, head_parity, lane):
    swapped = pltpu.roll(x2, SWA_HD, 1)
    lo_src, hi_src = (x2, swapped) if head_parity == 0 else (swapped, x2)
    return jnp.where(lane < SWA_HD, lo_src, 0.0), jnp.where(lane >= SWA_HD, hi_src, 0.0)


def _swa_prompt_kernel(sinks_ref, q_ref, kv_ref, cos_ref, sin_ref, os_ref, kc_ref, kprev_ref, vprev_ref):
    n = pl.program_id(1)

    @pl.when(n == 0)
    def _():
        kprev_ref[...] = jnp.zeros_like(kprev_ref)
        vprev_ref[...] = jnp.zeros_like(vprev_ref)

    cos = cos_ref[...]
    sin = sin_ref[...]
    kv = kv_ref[...]
    k_cur = _rope(kv[:, :SWA_KV_W], cos, sin)
    v_cur = kv[:, SWA_KV_W:]
    kc_ref[0] = k_cur
    k_all = jnp.concatenate([kprev_ref[...], k_cur], axis=0)
    v_all = jnp.concatenate([vprev_ref[...], v_cur], axis=0)
    kprev_ref[...] = k_cur
    vprev_ref[...] = v_cur

    row = lax.broadcasted_iota(jnp.int32, (WINDOW, 2 * WINDOW), 0)
    col = lax.broadcasted_iota(jnp.int32, (WINDOW, 2 * WINDOW), 1)
    prev_visible = jnp.logical_and(jnp.logical_and(col > row, col < WINDOW), n > 0)
    own_visible = jnp.logical_and(col >= WINDOW, (col - WINDOW) <= row)
    mask = jnp.logical_or(prev_visible, own_visible)
    lane = lax.broadcasted_iota(jnp.int32, (2 * WINDOW, LANES), 1)
    scale = SWA_HD ** -0.5
    group = SWA_HEADS // SWA_KV_HEADS
    for h in range(SWA_KV_HEADS):
        pair = slice((h // 2) * LANES, (h // 2 + 1) * LANES)
        k_lo, k_hi = _head_halves(k_all[:, pair], h % 2, lane)
        v_lo, v_hi = _head_halves(v_all[:, pair], h % 2, lane)
        k_lo, k_hi, v_lo, v_hi = (a.astype(bf16) for a in (k_lo, k_hi, v_lo, v_hi))
        q_h = _rope(q_ref[:, h * group * SWA_HD : (h + 1) * group * SWA_HD], cos, sin)
        for j in range(group // 2):
            q2 = q_h[:, j * LANES : (j + 1) * LANES].astype(bf16)
            out = None
            for par, (k_half, v_half) in enumerate(((k_lo, v_lo), (k_hi, v_hi))):
                sink = sinks_ref[h * group + 2 * j + par]
                s = jnp.where(mask, _dot_nt(q2, k_half) * scale, -jnp.inf)
                m = jnp.maximum(jnp.max(s, -1, keepdims=True), sink)
                p = jnp.exp(s - m)
                denom = jnp.sum(p, -1, keepdims=True) + jnp.exp(sink - m)
                o = _dot((p / denom).astype(bf16), v_half)
                out = o if out is None else out + o
            lo = (h * group + 2 * j) * SWA_HD
            os_ref[:, lo : lo + LANES] = out.astype(bf16)


def _swa_prompt(proj, B, T, off_q, off_kv, cos, sin, sinks):
    assert T % WINDOW == 0 and SWA_KV_W == 2 * LANES
    nb = T // WINDOW
    return pl.pallas_call(
        _swa_prompt_kernel,
        grid=(B, nb),
        in_specs=[
            pl.BlockSpec(memory_space=pltpu.SMEM),
            pl.BlockSpec((WINDOW, SWA_Q_W), lambda b, n: (b * nb + n, off_q // SWA_Q_W)),
            pl.BlockSpec((WINDOW, 2 * SWA_KV_W), lambda b, n: (b * nb + n, off_kv // (2 * SWA_KV_W))),
            pl.BlockSpec((WINDOW, SWA_KV_W), lambda b, n: (n, 0)),
            pl.BlockSpec((WINDOW, SWA_KV_W), lambda b, n: (n, 0)),
        ],
        out_specs=[
            pl.BlockSpec((WINDOW, SWA_Q_W), lambda b, n: (b * nb + n, 0)),
            pl.BlockSpec((1, WINDOW, SWA_KV_W), lambda b, n: (b, 0, 0)),
        ],
        out_shape=[
            jax.ShapeDtypeStruct((B * T, SWA_Q_W), bf16),
            jax.ShapeDtypeStruct((B, WINDOW, SWA_KV_W), f32),
        ],
        scratch_shapes=[pltpu.VMEM((WINDOW, SWA_KV_W), f32), pltpu.VMEM((WINDOW, SWA_KV_W), f32)],
        compiler_params=_cparams(("parallel", "arbitrary")),
        name="swa_prompt",
    )(sinks, proj, proj, cos, sin)


SWA_STEP_BATCH = 8


def _swa_step_kernel(qe_ref, kv_ref, ck_ref, cv_ref, cos_ref, sin_ref, sinks_ref, r_ref, nk_ref, nv_ref):
    cos = cos_ref[...]
    sin = sin_ref[...]
    sink = sinks_ref[...]
    row = lax.broadcasted_iota(jnp.int32, (WINDOW, SWA_KV_W), 0)
    scale = SWA_HD ** -0.5
    for i in range(qe_ref.shape[0]):
        kv = kv_ref[i]
        k_new = _rope(kv[:, :SWA_KV_W], cos, sin)
        v_new = kv[:, SWA_KV_W:]
        keys = jnp.where(row == WINDOW - 1, k_new, pltpu.roll(ck_ref[i], WINDOW - 1, 0))
        vals = jnp.where(row == WINDOW - 1, v_new, pltpu.roll(cv_ref[i], WINDOW - 1, 0))
        nk_ref[i] = keys
        nv_ref[i] = vals
        q = _rope(qe_ref[i], cos, sin)
        s = _dot_nt(q.astype(bf16), keys.astype(bf16)) * scale
        m = jnp.maximum(jnp.max(s, -1, keepdims=True), sink)
        p = jnp.exp(s - m)
        denom = jnp.sum(p, -1, keepdims=True) + jnp.exp(sink - m)
        r_ref[i] = _dot((p / denom).astype(bf16), vals.astype(bf16))


def _swa_step(q_exp, proj3, off_kv, cache_k, cache_v, cos, sin, sinks_col):
    B = q_exp.shape[0]
    bb = math.gcd(B, SWA_STEP_BATCH)
    assert cache_k.shape[1] == WINDOW
    return pl.pallas_call(
        _swa_step_kernel,
        grid=(B // bb,),
        in_specs=[
            pl.BlockSpec((bb, SWA_HEADS, SWA_KV_W), lambda i: (i, 0, 0)),
            pl.BlockSpec((bb, 1, 2 * SWA_KV_W), lambda i: (i, 0, off_kv // (2 * SWA_KV_W))),
            pl.BlockSpec((bb, WINDOW, SWA_KV_W), lambda i: (i, 0, 0)),
            pl.BlockSpec((bb, WINDOW, SWA_KV_W), lambda i: (i, 0, 0)),
            pl.BlockSpec((1, SWA_KV_W), lambda i: (0, 0)),
            pl.BlockSpec((1, SWA_KV_W), lambda i: (0, 0)),
            pl.BlockSpec((SWA_HEADS, 1), lambda i: (0, 0)),
        ],
        out_specs=[
            pl.BlockSpec((bb, SWA_HEADS, SWA_KV_W), lambda i: (i, 0, 0)),
            pl.BlockSpec((bb, WINDOW, SWA_KV_W), lambda i: (i, 0, 0)),
            pl.BlockSpec((bb, WINDOW, SWA_KV_W), lambda i: (i, 0, 0)),
        ],
        out_shape=[
            jax.ShapeDtypeStruct((B, SWA_HEADS, SWA_KV_W), f32),
            jax.ShapeDtypeStruct((B, WINDOW, SWA_KV_W), f32),
            jax.ShapeDtypeStruct((B, WINDOW, SWA_KV_W), f32),
        ],
        compiler_params=_cparams(("parallel",)),
        name="swa_step",
    )(q_exp, proj3, cache_k, cache_v, cos, sin, sinks_col)


def _merge_kernel(og_ref, os_ref, gg_ref, gs_ref, x_ref, wg_ref, ws_ref, wo_ref, gq_ref, wq_ref, xo_ref, qm_ref):
    p_gdn = _dot(og_ref[...], wg_ref[...])
    p_swa = _dot(os_ref[...], ws_ref[...])
    merged = jax.nn.sigmoid(gg_ref[...]) * p_gdn + jax.nn.sigmoid(gs_ref[...]) * p_swa
    x_new = x_ref[...] + _dot(merged.astype(bf16), wo_ref[...])
    xo_ref[...] = x_new
    qm_ref[...] = _dot(_rmsnorm(x_new, gq_ref[...]).astype(bf16), wq_ref[...])


def _merge(og, os_, proj, off_gg, x, wg, ws, wo, gq, wq):
    M, D = x.shape
    tm = min(256, M)
    assert M % tm == 0 and off_gg % D == 0
    const = lambda shape: pl.BlockSpec(shape, lambda i: (0, 0), pipeline_mode=pl.Buffered(1))
    return pl.pallas_call(
        _merge_kernel,
        grid=(M // tm,),
        in_specs=[
            pl.BlockSpec((tm, GDN_W), lambda i: (i, 0)),
            pl.BlockSpec((tm, SWA_Q_W), lambda i: (i, 0)),
            pl.BlockSpec((tm, D), lambda i: (i, off_gg // D)),
            pl.BlockSpec((tm, D), lambda i: (i, off_gg // D + 1)),
            pl.BlockSpec((tm, D), lambda i: (i, 0)),
            const((GDN_W, D)),
            const((SWA_Q_W, D)),
            const((D, D)),
            const((1, D)),
            const((D, MEM_W)),
        ],
        out_specs=[
            pl.BlockSpec((tm, D), lambda i: (i, 0)),
            pl.BlockSpec((tm, MEM_W), lambda i: (i, 0)),
        ],
        out_shape=[jax.ShapeDtypeStruct((M, D), f32), jax.ShapeDtypeStruct((M, MEM_W), f32)],
        compiler_params=_cparams(("parallel",)),
        name="merge",
    )(og, os_, proj, proj, x, wg, ws, wo, gq, wq)


def _mem_attn_kernel(q_ref, k_ref, v_ref, x_ref, wo_ref, o_ref):
    tq = q_ref.shape[1]
    q = q_ref[0]
    if tq < SUBLANES:
        q = jnp.broadcast_to(q[0:1], (SUBLANES, q.shape[-1]))
    q = q.astype(bf16)
    scale = MEM_HD ** -0.5
    outs = []
    for h in range(MEM_HEADS):
        sl = slice(h * MEM_HD, (h + 1) * MEM_HD)
        s = _dot_nt(q[:, sl], k_ref[0, :, sl].astype(bf16)) * scale
        m = jnp.max(s, -1, keepdims=True)
        p = jnp.exp(s - m)
        p = p / jnp.sum(p, -1, keepdims=True)
        outs.append(_dot(p.astype(bf16), v_ref[0, :, sl].astype(bf16)))
    o = _dot(jnp.concatenate(outs, axis=-1).astype(bf16), wo_ref[...])
    o_ref[0] = x_ref[0] + o[0:tq]


def _mem_attn(qm, mem_k, mem_v, x, wo):
    B, T, D = x.shape
    mt = mem_k.shape[1]
    tq = min(512, T)
    assert T % tq == 0
    return pl.pallas_call(
        _mem_attn_kernel,
        grid=(B, T // tq),
        in_specs=[
            pl.BlockSpec((1, tq, MEM_W), lambda b, t: (b, t, 0)),
            pl.BlockSpec((1, mt, MEM_W), lambda b, t: (b, 0, 0)),
            pl.BlockSpec((1, mt, MEM_W), lambda b, t: (b, 0, 0)),
            pl.BlockSpec((1, tq, D), lambda b, t: (b, t, 0)),
            pl.BlockSpec((MEM_W, D), lambda b, t: (0, 0)),
        ],
        out_specs=pl.BlockSpec((1, tq, D), lambda b, t: (b, t, 0)),
        out_shape=jax.ShapeDtypeStruct((B, T, D), f32),
        compiler_params=_cparams(("parallel", "parallel")),
        name="mem_attn",
    )(qm, mem_k, mem_v, x, wo)


def _rope_tables(pos):
    half = SWA_HD // 2
    inv_freq = ROPE_THETA ** (-jnp.arange(half, dtype=f32) / half)
    ang = pos.astype(f32)[:, None] * inv_freq[None, :]
    cos = jnp.cos(ang)
    sin = jnp.sin(ang)
    reps = SWA_KV_W // SWA_HD
    return jnp.tile(jnp.concatenate([cos, cos], -1), (1, reps)), jnp.tile(jnp.concatenate([-sin, sin], -1), (1, reps))


def kernel(x_prompt, x_sample, state_gdn, state_conv, cache_swa_k, cache_swa_v, cache_mem_k, cache_mem_v, mem_prompt, norm_ffn1, ffn1_w1, ffn1_w3, ffn1_w2, norm_mix, w_in, conv_w, gdn_A_log, gdn_dt_bias, gdn_norm, swa_sinks, w_br_gdn, w_br_swa, w_out, norm_mem_q, norm_mem_kv, w_mem_q, w_mem_k, w_mem_v, w_mem_o, norm_ffn2, ffn2_w1, ffn2_w3, ffn2_w2, norm_final):
    Bp, Tp, D = x_prompt.shape
    Bs, Ts, _ = x_sample.shape
    assert Ts == 1
    depth = norm_ffn1.shape[0]
    n_mem = mem_prompt.shape[1]
    group = SWA_HEADS // SWA_KV_HEADS

    off_gs = OFF_GG + D
    off_q = off_gs + D
    off_kv = off_q + SWA_Q_W
    off_ba = off_kv + 2 * SWA_KV_W
    d_in_pad = off_ba + LANES
    o_b = CONV_CH + GDN_W
    o_q = o_b + 2 * GDN_HEADS
    o_gg = o_q + SWA_Q_W + 2 * SWA_KV_W

    cos_p, sin_p = _rope_tables(jnp.arange(Tp, dtype=jnp.int32))
    cos_s, sin_s = _rope_tables(PAST_LEN + jnp.arange(Ts, dtype=jnp.int32))
    eye_kv = jnp.eye(SWA_KV_HEADS, dtype=f32)
    row = lambda v: v.reshape(1, -1)

    hp = x_prompt.reshape(Bp * Tp, D)
    hs = x_sample.reshape(Bs, D)
    outs = [[] for _ in range(10)]
    for l in range(depth):
        wi = w_in[l]
        w_in_r = jnp.concatenate(
            [wi[:, :o_b], wi[:, o_gg:], wi[:, o_q:o_gg], wi[:, o_b:o_q], jnp.zeros((D, LANES - 2 * GDN_HEADS), f32)], axis=1
        ).astype(bf16)
        assert w_in_r.shape[1] == d_in_pad
        ffn1 = (row(norm_ffn1[l]), ffn1_w1[l].astype(bf16), ffn1_w3[l].astype(bf16), ffn1_w2[l].astype(bf16))
        ffn2 = (row(norm_ffn2[l]), ffn2_w1[l].astype(bf16), ffn2_w3[l].astype(bf16), ffn2_w2[l].astype(bf16))
        last = l == depth - 1
        gfin = row(norm_final)
        avec = jnp.zeros((1, LANES), f32).at[0, GDN_HEADS : 2 * GDN_HEADS].set(gdn_A_log[l])
        dvec = jnp.zeros((1, LANES), f32).at[0, GDN_HEADS : 2 * GDN_HEADS].set(gdn_dt_bias[l])
        gdn_common = (conv_w[l], avec, dvec, row(gdn_norm[l]))
        merge_w = (w_br_gdn[l].astype(bf16), w_br_swa[l].astype(bf16), w_out[l].astype(bf16),
                   row(norm_mem_q[l]), w_mem_q[l].astype(bf16))
        wmo = w_mem_o[l].astype(bf16)
        tn_in = LANES * 7 if d_in_pad % (LANES * 7) == 0 else LANES

        x1 = _ffn(hp, *ffn1, gfin, False)
        proj = _norm_matmul(x1, row(norm_mix[l]), w_in_r, tn_in)
        og, s_new = _gdn_prompt(proj, Bp, Tp, off_ba, *gdn_common)
        os_, kc = _swa_prompt(proj, Bp, Tp, off_q, off_kv, cos_p, sin_p, swa_sinks[l])
        x2, qm = _merge(og, os_, proj, OFF_GG, x1, *merge_w)
        mem_x = mem_prompt.reshape(Bp * n_mem, D)
        mk = _norm_matmul(mem_x, row(norm_mem_kv[l]), w_mem_k[l].astype(bf16), MEM_W)
        mv = _norm_matmul(mem_x, row(norm_mem_kv[l]), w_mem_v[l].astype(bf16), MEM_W)
        x3 = _mem_attn(qm.reshape(Bp, Tp, MEM_W), mk.reshape(Bp, n_mem, MEM_W), mv.reshape(Bp, n_mem, MEM_W),
                       x2.reshape(Bp, Tp, D), wmo)
        hp = _ffn(x3.reshape(Bp * Tp, D), *ffn2, gfin, last)
        proj_b = proj.reshape(Bp, Tp, d_in_pad)
        outs[0].append(s_new)
        outs[1].append(proj_b[:, Tp - (GDN_CONV - 1) :, :CONV_CH])
        outs[2].append(kc.reshape(Bp, WINDOW, SWA_KV_HEADS, SWA_HD))
        outs[3].append(proj_b[:, Tp - WINDOW :, off_kv + SWA_KV_W : off_kv + 2 * SWA_KV_W].reshape(Bp, WINDOW, SWA_KV_HEADS, SWA_HD))
        outs[4].append(mk.reshape(Bp, n_mem, MEM_HEADS, MEM_HD))
        outs[5].append(mv.reshape(Bp, n_mem, MEM_HEADS, MEM_HD))

        x1 = _ffn(hs, *ffn1, gfin, False)
        proj = _norm_matmul(x1, row(norm_mix[l]), w_in_r, tn_in)
        proj3 = proj.reshape(Bs, 1, d_in_pad)
        og, s_new = _gdn_step(proj3, off_ba, state_conv[l], state_gdn[l], *gdn_common)
        q_raw = proj[:, off_q : off_q + SWA_Q_W].reshape(Bs, SWA_KV_HEADS, group, 1, SWA_HD)
        q_exp = (q_raw * eye_kv[None, :, None, :, None]).reshape(Bs, SWA_HEADS, SWA_KV_W)
        ck = cache_swa_k[l].reshape(Bs, WINDOW, SWA_KV_W)
        cv = cache_swa_v[l].reshape(Bs, WINDOW, SWA_KV_W)
        r, nk, nv = _swa_step(q_exp, proj3, off_kv, ck, cv, cos_s, sin_s, swa_sinks[l].reshape(SWA_HEADS, 1))
        r5 = r.reshape(Bs, SWA_KV_HEADS, group, SWA_KV_HEADS, SWA_HD)
        kvh = jnp.arange(SWA_KV_HEADS)
        os_ = jnp.transpose(r5[:, kvh, :, kvh, :], (1, 0, 2, 3)).reshape(Bs, SWA_Q_W).astype(bf16)
        x2, qm = _merge(og.reshape(Bs, GDN_W), os_, proj, OFF_GG, x1, *merge_w)
        x3 = _mem_attn(qm.reshape(Bs, 1, MEM_W), cache_mem_k[l].reshape(Bs, n_mem, MEM_W),
                       cache_mem_v[l].reshape(Bs, n_mem, MEM_W), x2.reshape(Bs, 1, D), wmo)
        hs = _ffn(x3.reshape(Bs, D), *ffn2, gfin, last)
        outs[6].append(s_new)
        outs[7].append(jnp.concatenate([state_conv[l][:, 1:], proj3[:, :, :CONV_CH]], axis=1))
        outs[8].append(nk.reshape(Bs, WINDOW, SWA_KV_HEADS, SWA_HD))
        outs[9].append(nv.reshape(Bs, WINDOW, SWA_KV_HEADS, SWA_HD))

    return (hp.reshape(Bp, Tp, D), hs.reshape(Bs, Ts, D), *(jnp.stack(o) for o in outs))
```

```python
import functools
import math

import jax
import jax.numpy as jnp
from jax import lax
from jax.experimental import pallas as pl
from jax.experimental.pallas import tpu as pltpu

f32 = jnp.float32
bf16 = jnp.bfloat16

PAST_LEN = 16384
GDN_HEADS = 8
GDN_D = 128
GDN_CONV = 4
SWA_HEADS = 16
SWA_KV_HEADS = 4
SWA_HD = 64
WINDOW = 128
ROPE_THETA = 10000.0
MEM_HEADS = 4
MEM_HD = 128
EPS = 1e-6
L2_EPS = 1e-6

LANES = 128
SUBLANES = 8
CHUNK = 128
GDN_HEAD_GROUP = 8
PROJ_TN = 768
VMEM_LIMIT = 52 * 1024 * 1024

GDN_W = GDN_HEADS * GDN_D
CONV_CH = 3 * GDN_W
SWA_Q_W = SWA_HEADS * SWA_HD
SWA_KV_W = SWA_KV_HEADS * SWA_HD
MEM_W = MEM_HEADS * MEM_HD

OFF_Z = CONV_CH
OFF_GG = OFF_Z + GDN_W


def _cparams(semantics):
    return pltpu.CompilerParams(dimension_semantics=semantics, vmem_limit_bytes=VMEM_LIMIT)


def _dot(a, b):
    return jnp.dot(a, b, preferred_element_type=f32)


def _dot_nt(a, b):
    return lax.dot_general(a, b, (((1,), (1,)), ((), ())), preferred_element_type=f32)


def _rmsnorm(x, g):
    return x * lax.rsqrt(jnp.mean(x * x, -1, keepdims=True) + EPS) * g


def _silu(x):
    return x * jax.nn.sigmoid(x)


def _softplus(x):
    return jnp.maximum(x, 0.0) + jnp.log1p(jnp.exp(-jnp.abs(x)))


def _ffn_kernel(x_ref, g_ref, w1_ref, w3_ref, w2_ref, gf_ref, o_ref, h_ref, *, final_norm):
    j = pl.program_id(1)

    @pl.when(j == 0)
    def _():
        h_ref[...] = _rmsnorm(x_ref[...], g_ref[...]).astype(bf16)
        o_ref[...] = jnp.zeros_like(o_ref)

    h = h_ref[...]
    a = _dot(h, w1_ref[...])
    b = _dot(h, w3_ref[...])
    o_ref[...] += _dot((_silu(a) * b).astype(bf16), w2_ref[...])

    @pl.when(j == pl.num_programs(1) - 1)
    def _():
        y = x_ref[...] + 0.5 * o_ref[...]
        if final_norm:
            y = _rmsnorm(y, gf_ref[...])
        o_ref[...] = y


def _ffn(x, g, w1, w3, w2, gf, final_norm):
    M, D = x.shape
    F = w1.shape[1]
    tm = min(512, M)
    tf = 1024 if F % 1024 == 0 else 512
    assert M % tm == 0 and F % tf == 0
    return pl.pallas_call(
        functools.partial(_ffn_kernel, final_norm=final_norm),
        grid=(M // tm, F // tf),
        in_specs=[
            pl.BlockSpec((tm, D), lambda i, j: (i, 0)),
            pl.BlockSpec((1, D), lambda i, j: (0, 0)),
            pl.BlockSpec((D, tf), lambda i, j: (0, j)),
            pl.BlockSpec((D, tf), lambda i, j: (0, j)),
            pl.BlockSpec((tf, D), lambda i, j: (j, 0)),
            pl.BlockSpec((1, D), lambda i, j: (0, 0)),
        ],
        out_specs=pl.BlockSpec((tm, D), lambda i, j: (i, 0)),
        out_shape=jax.ShapeDtypeStruct((M, D), f32),
        scratch_shapes=[pltpu.VMEM((tm, D), bf16)],
        compiler_params=_cparams(("parallel", "arbitrary")),
        name="ffn",
    )(x, g, w1, w3, w2, gf)


def _norm_matmul_kernel(x_ref, g_ref, w_ref, o_ref, h_ref):
    @pl.when(pl.program_id(1) == 0)
    def _():
        h_ref[...] = _rmsnorm(x_ref[...], g_ref[...]).astype(bf16)

    o_ref[...] = _dot(h_ref[...], w_ref[...])


def _norm_matmul(x, g, w, tn):
    M, D = x.shape
    N = w.shape[1]
    tm = min(1024, M)
    assert M % tm == 0 and N % tn == 0
    return pl.pallas_call(
        _norm_matmul_kernel,
        grid=(M // tm, N // tn),
        in_specs=[
            pl.BlockSpec((tm, D), lambda i, j: (i, 0)),
            pl.BlockSpec((1, D), lambda i, j: (0, 0)),
            pl.BlockSpec((D, tn), lambda i, j: (0, j)),
        ],
        out_specs=pl.BlockSpec((tm, tn), lambda i, j: (i, j)),
        out_shape=jax.ShapeDtypeStruct((M, N), f32),
        scratch_shapes=[pltpu.VMEM((tm, D), bf16)],
        compiler_params=_cparams(("parallel", "arbitrary")),
        name="norm_matmul",
    )(x, g, w)


def _bdot(a, b):
    return _dot(a.astype(bf16), b.astype(bf16))


def _unit_lower_inverse(nmats, row, col):
    eye = jnp.where(row == col, 1.0, 0.0).astype(f32)

    def same_block(size):
        return (row // size) == (col // size)

    blk = same_block(SUBLANES)
    a1 = [jnp.where(blk, n, 0.0) for n in nmats]
    a2 = [_bdot(a, a) for a in a1]
    a4 = [_bdot(a, a) for a in a2]
    ts = [_bdot(eye - x1, eye + x2) for x1, x2 in zip(a1, a2)]
    ts = [_bdot(t, eye + x4) for t, x4 in zip(ts, a4)]
    size = SUBLANES
    while size < CHUNK:
        nxt = same_block(2 * size)
        sel = jnp.logical_and(nxt, jnp.logical_not(blk))
        tbs = [t.astype(bf16) for t in ts]
        xs = [_dot(jnp.where(sel, n, 0.0).astype(bf16), tb) for n, tb in zip(nmats, tbs)]
        ts = [t - _dot(tb, x.astype(bf16)) for t, tb, x in zip(ts, tbs, xs)]
        blk = nxt
        size *= 2
    return ts


def _conv_silu_slab(x_ref, top_ref, w_ref, sl):
    y = None
    for tap in range(GDN_CONV):
        back = GDN_CONV - 1 - tap
        window = jnp.concatenate(
            [top_ref[SUBLANES - back : 2 * SUBLANES - back, sl], x_ref[SUBLANES - back : CHUNK - back, sl]], axis=0)
        term = window * w_ref[tap : tap + 1, sl]
        y = term if y is None else y + term
    return _silu(y)


def _delta_rule_chunk(heads, gcum, gcum_t, beta_t, row, col, q_ref, k_ref, v_ref, z_ref, gn_ref, s_ref, og_ref):
    heads = list(heads)
    n = range(len(heads))
    incl = row >= col
    strict = row > col
    gc = [gcum[:, GDN_HEADS + h : GDN_HEADS + h + 1] for h in heads]
    gr = [gcum_t[GDN_HEADS + h : GDN_HEADS + h + 1, :] for h in heads]
    beta = [beta_t[:, h : h + 1] for h in heads]
    kb = [k_ref[h].astype(bf16) for h in heads]
    kk = [_dot_nt(kb[i], kb[i]) for i in n]
    qk = [_dot_nt(q_ref[h].astype(bf16), kb[i]) for i, h in enumerate(heads)]
    e = [jnp.exp(jnp.where(incl, gc[i] - gr[i], 0.0)) for i in n]
    nmat = [beta[i] * kk[i] * jnp.where(strict, e[i], 0.0) for i in n]
    qkd = [(qk[i] * jnp.where(incl, e[i], 0.0)).astype(bf16) for i in n]
    tinv = _unit_lower_inverse(nmat, row, col)
    gamma = [jnp.exp(gc[i]) for i in n]
    rhs = [jnp.concatenate([beta[i] * v_ref[h], (beta[i] * gamma[i]) * k_ref[h]], axis=-1).astype(bf16)
           for i, h in enumerate(heads)]
    sol = [_dot(tinv[i].astype(bf16), rhs[i]) for i in n]
    sb = [s_ref[h].astype(bf16) for h in heads]
    ub = [(sol[i][:, :GDN_D] - _dot(sol[i][:, GDN_D:].astype(bf16), sb[i])).astype(bf16) for i in n]
    o = [_dot((q_ref[h] * gamma[i]).astype(bf16), sb[i]) + _dot(qkd[i], ub[i]) for i, h in enumerate(heads)]
    for i, h in enumerate(heads):
        g_last = gc[i][CHUNK - 1 : CHUNK, :]
        k_end = k_ref[h] * jnp.exp(g_last - gc[i])
        s_ref[h] = jnp.exp(g_last) * s_ref[h] + _dot(k_end.T.astype(bf16), ub[i])
        zh = z_ref[:, h * GDN_D : (h + 1) * GDN_D]
        og_ref[:, h * GDN_D : (h + 1) * GDN_D] = (_rmsnorm(o[i], gn_ref[...]) * _silu(zh)).astype(bf16)


def _gdn_prompt_kernel(qkv_ref, z_ref, ba_ref, cw_ref, av_ref, dv_ref, gn_ref,
                       og_ref, so_ref, s_ref, carry_ref, q_ref, k_ref, v_ref):
    c = pl.program_id(1)

    @pl.when(c == 0)
    def _():
        s_ref[...] = jnp.zeros_like(s_ref)
        carry_ref[0:SUBLANES, :] = jnp.zeros((SUBLANES, CONV_CH), f32)

    carry_ref[SUBLANES : 2 * SUBLANES, :] = qkv_ref[0:SUBLANES, :]
    for j in range(3 * GDN_HEADS):
        sl = slice(j * LANES, (j + 1) * LANES)
        y = _conv_silu_slab(qkv_ref, carry_ref, cw_ref, sl)
        h = j % GDN_HEADS
        if j < 2 * GDN_HEADS:
            y = y * lax.rsqrt(jnp.sum(y * y, -1, keepdims=True) + L2_EPS)
            if j < GDN_HEADS:
                q_ref[h] = y * (GDN_D ** -0.5)
            else:
                k_ref[h] = y
        else:
            v_ref[h] = y
    carry_ref[0:SUBLANES, :] = qkv_ref[CHUNK - SUBLANES : CHUNK, :]

    row = lax.broadcasted_iota(jnp.int32, (CHUNK, CHUNK), 0)
    col = lax.broadcasted_iota(jnp.int32, (CHUNK, CHUNK), 1)

    ba = ba_ref[...]
    beta_t = jax.nn.sigmoid(ba)
    g_t = -jnp.exp(av_ref[...]) * _softplus(ba + dv_ref[...])
    gcum = g_t
    shift = 1
    while shift < CHUNK:
        gcum = gcum + jnp.where(row >= shift, pltpu.roll(gcum, shift, 0), 0.0)
        shift *= 2
    gcum_t = gcum.T

    for first in range(0, GDN_HEADS, GDN_HEAD_GROUP):
        _delta_rule_chunk(range(first, first + GDN_HEAD_GROUP), gcum, gcum_t, beta_t, row, col,
                          q_ref, k_ref, v_ref, z_ref, gn_ref, s_ref, og_ref)

    @pl.when(c == pl.num_programs(1) - 1)
    def _():
        so_ref[0] = s_ref[...]


def _gdn_prompt(proj, B, T, off_ba, cw, avec, dvec, gn):
    assert T % CHUNK == 0
    nc = T // CHUNK
    return pl.pallas_call(
        _gdn_prompt_kernel,
        grid=(B, nc),
        in_specs=[
            pl.BlockSpec((CHUNK, CONV_CH), lambda b, c: (b * nc + c, 0)),
            pl.BlockSpec((CHUNK, GDN_W), lambda b, c: (b * nc + c, OFF_Z // GDN_W)),
            pl.BlockSpec((CHUNK, LANES), lambda b, c: (b * nc + c, off_ba // LANES)),
            pl.BlockSpec((GDN_CONV, CONV_CH), lambda b, c: (0, 0)),
            pl.BlockSpec((1, LANES), lambda b, c: (0, 0)),
            pl.BlockSpec((1, LANES), lambda b, c: (0, 0)),
            pl.BlockSpec((1, GDN_D), lambda b, c: (0, 0)),
        ],
        out_specs=[
            pl.BlockSpec((CHUNK, GDN_W), lambda b, c: (b * nc + c, 0)),
            pl.BlockSpec((1, GDN_HEADS, GDN_D, GDN_D), lambda b, c: (b, 0, 0, 0)),
        ],
        out_shape=[
            jax.ShapeDtypeStruct((B * T, GDN_W), bf16),
            jax.ShapeDtypeStruct((B, GDN_HEADS, GDN_D, GDN_D), f32),
        ],
        scratch_shapes=[
            pltpu.VMEM((GDN_HEADS, GDN_D, GDN_D), f32),
            pltpu.VMEM((2 * SUBLANES, CONV_CH), f32),
            pltpu.VMEM((GDN_HEADS, CHUNK, GDN_D), f32),
            pltpu.VMEM((GDN_HEADS, CHUNK, GDN_D), f32),
            pltpu.VMEM((GDN_HEADS, CHUNK, GDN_D), f32),
        ],
        compiler_params=_cparams(("parallel", "arbitrary")),
        name="gdn_prompt",
    )(proj, proj, proj, cw, avec, dvec, gn)


def _gdn_step_kernel(qkv_ref, z_ref, ba_ref, sc_ref, s0_ref, cw_ref, av_ref, dv_ref, gn_ref, og_ref, so_ref):
    x_new = qkv_ref[0]
    taps = cw_ref[...]
    y = jnp.sum(sc_ref[0] * taps[0 : GDN_CONV - 1, :], axis=0, keepdims=True) + x_new * taps[GDN_CONV - 1 : GDN_CONV, :]
    y = _silu(y)
    ba = ba_ref[0]
    beta_t = jax.nn.sigmoid(ba)
    gamma_t = jnp.exp(-jnp.exp(av_ref[...]) * _softplus(ba + dv_ref[...]))
    z = z_ref[0]
    for h in range(GDN_HEADS):
        q = y[:, h * GDN_D : (h + 1) * GDN_D]
        k = y[:, GDN_W + h * GDN_D : GDN_W + (h + 1) * GDN_D]
        v = y[:, 2 * GDN_W + h * GDN_D : 2 * GDN_W + (h + 1) * GDN_D]
        q = q * lax.rsqrt(jnp.sum(q * q, -1, keepdims=True) + L2_EPS) * (GDN_D ** -0.5)
        k = k * lax.rsqrt(jnp.sum(k * k, -1, keepdims=True) + L2_EPS)
        beta = beta_t[:, h : h + 1]
        gamma = gamma_t[:, GDN_HEADS + h : GDN_HEADS + h + 1]
        s = s0_ref[0, h]
        k_col = jnp.broadcast_to(k, (GDN_D, GDN_D)).T
        q_col = jnp.broadcast_to(q, (GDN_D, GDN_D)).T
        k_s = jnp.sum(k_col * s, axis=0, keepdims=True)
        q_s = jnp.sum(q_col * s, axis=0, keepdims=True)
        u = beta * v - (beta * gamma) * k_s
        o = gamma * q_s + jnp.sum(q * k, -1, keepdims=True) * u
        so_ref[0, h] = gamma * s + k_col * u
        zh = z[:, h * GDN_D : (h + 1) * GDN_D]
        og_ref[0, :, h * GDN_D : (h + 1) * GDN_D] = (_rmsnorm(o, gn_ref[...]) * _silu(zh)).astype(bf16)


def _gdn_step(proj3, off_ba, state_conv, state_gdn, cw, avec, dvec, gn):
    B = proj3.shape[0]
    return pl.pallas_call(
        _gdn_step_kernel,
        grid=(B,),
        in_specs=[
            pl.BlockSpec((1, 1, CONV_CH), lambda b: (b, 0, 0)),
            pl.BlockSpec((1, 1, GDN_W), lambda b: (b, 0, OFF_Z // GDN_W)),
            pl.BlockSpec((1, 1, LANES), lambda b: (b, 0, off_ba // LANES)),
            pl.BlockSpec((1, GDN_CONV - 1, CONV_CH), lambda b: (b, 0, 0)),
            pl.BlockSpec((1, GDN_HEADS, GDN_D, GDN_D), lambda b: (b, 0, 0, 0)),
            pl.BlockSpec((GDN_CONV, CONV_CH), lambda b: (0, 0)),
            pl.BlockSpec((1, LANES), lambda b: (0, 0)),
            pl.BlockSpec((1, LANES), lambda b: (0, 0)),
            pl.BlockSpec((1, GDN_D), lambda b: (0, 0)),
        ],
        out_specs=[
            pl.BlockSpec((1, 1, GDN_W), lambda b: (b, 0, 0)),
            pl.BlockSpec((1, GDN_HEADS, GDN_D, GDN_D), lambda b: (b, 0, 0, 0)),
        ],
        out_shape=[
            jax.ShapeDtypeStruct((B, 1, GDN_W), bf16),
            jax.ShapeDtypeStruct((B, GDN_HEADS, GDN_D, GDN_D), f32),
        ],
        compiler_params=_cparams(("parallel",)),
        name="gdn_step",
    )(proj3, proj3, proj3, state_conv, state_gdn, cw, avec, dvec, gn)


def _rope(x, cos, sin_signed):
    width = x.shape[-1]
    lane = lax.broadcasted_iota(jnp.int32, x.shape, x.ndim - 1)
    first_half = (lane % SWA_HD) < (SWA_HD // 2)
    rot = jnp.where(first_half, pltpu.roll(x, width - SWA_HD // 2, x.ndim - 1), pltpu.roll(x, SWA_HD // 2, x.ndim - 1))
    return x * cos + rot * sin_signed


def _head_halves(x2, head_parity, lane):
    swapped = pltpu.roll(x2, SWA_HD, 1)
    lo_src, hi_src = (x2, swapped) if head_parity == 0 else (swapped, x2)
    return jnp.where(lane < SWA_HD, lo_src, 0.0), jnp.where(lane >= SWA_HD, hi_src, 0.0)


def _swa_prompt_kernel(sinks_ref, q_ref, kv_ref, cos_ref, sin_ref, os_ref, kc_ref, kprev_ref, vprev_ref):
    n = pl.program_id(1)

    @pl.when(n == 0)
    def _():
        kprev_ref[...] = jnp.zeros_like(kprev_ref)
        vprev_ref[...] = jnp.zeros_like(vprev_ref)

    cos = cos_ref[...]
    sin = sin_ref[...]
    kv = kv_ref[...]
    k_cur = _rope(kv[:, :SWA_KV_W], cos, sin)
    v_cur = kv[:, SWA_KV_W:]
    kc_ref[0] = k_cur
    k_all = jnp.concatenate([kprev_ref[...], k_cur], axis=0)
    v_all = jnp.concatenate([vprev_ref[...], v_cur], axis=0)
    kprev_ref[...] = k_cur
    vprev_ref[...] = v_cur

    row = lax.broadcasted_iota(jnp.int32, (WINDOW, 2 * WINDOW), 0)
    col = lax.broadcasted_iota(jnp.int32, (WINDOW, 2 * WINDOW), 1)
    prev_visible = jnp.logical_and(jnp.logical_and(col > row, col < WINDOW), n > 0)
    own_visible = jnp.logical_and(col >= WINDOW, (col - WINDOW) <= row)
    mask = jnp.logical_or(prev_visible, own_visible)
    lane = lax.broadcasted_iota(jnp.int32, (2 * WINDOW, LANES), 1)
    scale = SWA_HD ** -0.5
    group = SWA_HEADS // SWA_KV_HEADS
    k_half, v_half, q2 = [], [], []
    for h in range(SWA_KV_HEADS):
        pair = slice((h // 2) * LANES, (h // 2 + 1) * LANES)
        k_half.append([a.astype(bf16) for a in _head_halves(k_all[:, pair], h % 2, lane)])
        v_half.append([a.astype(bf16) for a in _head_halves(v_all[:, pair], h % 2, lane)])
        q_h = _rope(q_ref[:, h * group * SWA_HD : (h + 1) * group * SWA_HD], cos, sin) * scale
        q2.append([q_h[:, j * LANES : (j + 1) * LANES].astype(bf16) for j in range(group // 2)])
    heads = [(h, j, par) for h in range(SWA_KV_HEADS) for j in range(group // 2) for par in range(2)]
    sink = [sinks_ref[h * group + 2 * j + par] for h, j, par in heads]
    s = [jnp.where(mask, _dot_nt(q2[h][j], k_half[h][par]), -jnp.inf) for h, j, par in heads]
    m = [jnp.maximum(jnp.max(s[i], -1, keepdims=True), sink[i]) for i in range(len(heads))]
    p = [jnp.exp(s[i] - m[i]) for i in range(len(heads))]
    inv = [1.0 / (jnp.sum(p[i], -1, keepdims=True) + jnp.exp(sink[i] - m[i])) for i in range(len(heads))]
    o = [_dot(p[i].astype(bf16), v_half[h][par]) * inv[i] for i, (h, j, par) in enumerate(heads)]
    for i in range(0, len(heads), 2):
        os_ref[:, i * SWA_HD : i * SWA_HD + LANES] = (o[i] + o[i + 1]).astype(bf16)


def _swa_prompt(proj, B, T, off_q, off_kv, cos, sin, sinks):
    assert T % WINDOW == 0 and SWA_KV_W == 2 * LANES
    nb = T // WINDOW
    return pl.pallas_call(
        _swa_prompt_kernel,
        grid=(B, nb),
        in_specs=[
            pl.BlockSpec(memory_space=pltpu.SMEM),
            pl.BlockSpec((WINDOW, SWA_Q_W), lambda b, n: (b * nb + n, off_q // SWA_Q_W)),
            pl.BlockSpec((WINDOW, 2 * SWA_KV_W), lambda b, n: (b * nb + n, off_kv // (2 * SWA_KV_W))),
            pl.BlockSpec((WINDOW, SWA_KV_W), lambda b, n: (n, 0)),
            pl.BlockSpec((WINDOW, SWA_KV_W), lambda b, n: (n, 0)),
        ],
        out_specs=[
            pl.BlockSpec((WINDOW, SWA_Q_W), lambda b, n: (b * nb + n, 0)),
            pl.BlockSpec((1, WINDOW, SWA_KV_W), lambda b, n: (b, 0, 0)),
        ],
        out_shape=[
            jax.ShapeDtypeStruct((B * T, SWA_Q_W), bf16),
            jax.ShapeDtypeStruct((B, WINDOW, SWA_KV_W), f32),
        ],
        scratch_shapes=[pltpu.VMEM((WINDOW, SWA_KV_W), f32), pltpu.VMEM((WINDOW, SWA_KV_W), f32)],
        compiler_params=_cparams(("parallel", "arbitrary")),
        name="swa_prompt",
    )(sinks, proj, proj, cos, sin)


SWA_STEP_BATCH = 8


def _swa_step_kernel(qe_ref, kv_ref, ck_ref, cv_ref, cos_ref, sin_ref, sinks_ref, r_ref, nk_ref, nv_ref):
    cos = cos_ref[...]
    sin = sin_ref[...]
    sink = sinks_ref[...]
    row = lax.broadcasted_iota(jnp.int32, (WINDOW, SWA_KV_W), 0)
    scale = SWA_HD ** -0.5
    for i in range(qe_ref.shape[0]):
        kv = kv_ref[i]
        k_new = _rope(kv[:, :SWA_KV_W], cos, sin)
        v_new = kv[:, SWA_KV_W:]
        keys = jnp.where(row == WINDOW - 1, k_new, pltpu.roll(ck_ref[i], WINDOW - 1, 0))
        vals = jnp.where(row == WINDOW - 1, v_new, pltpu.roll(cv_ref[i], WINDOW - 1, 0))
        nk_ref[i] = keys
        nv_ref[i] = vals
        q = _rope(qe_ref[i], cos, sin)
        s = _dot_nt(q.astype(bf16), keys.astype(bf16)) * scale
        m = jnp.maximum(jnp.max(s, -1, keepdims=True), sink)
        p = jnp.exp(s - m)
        denom = jnp.sum(p, -1, keepdims=True) + jnp.exp(sink - m)
        r_ref[i] = _dot((p / denom).astype(bf16), vals.astype(bf16))


def _swa_step(q_exp, proj3, off_kv, cache_k, cache_v, cos, sin, sinks_col):
    B = q_exp.shape[0]
    bb = math.gcd(B, SWA_STEP_BATCH)
    assert cache_k.shape[1] == WINDOW
    return pl.pallas_call(
        _swa_step_kernel,
        grid=(B // bb,),
        in_specs=[
            pl.BlockSpec((bb, SWA_HEADS, SWA_KV_W), lambda i: (i, 0, 0)),
            pl.BlockSpec((bb, 1, 2 * SWA_KV_W), lambda i: (i, 0, off_kv // (2 * SWA_KV_W))),
            pl.BlockSpec((bb, WINDOW, SWA_KV_W), lambda i: (i, 0, 0)),
            pl.BlockSpec((bb, WINDOW, SWA_KV_W), lambda i: (i, 0, 0)),
            pl.BlockSpec((1, SWA_KV_W), lambda i: (0, 0)),
            pl.BlockSpec((1, SWA_KV_W), lambda i: (0, 0)),
            pl.BlockSpec((SWA_HEADS, 1), lambda i: (0, 0)),
        ],
        out_specs=[
            pl.BlockSpec((bb, SWA_HEADS, SWA_KV_W), lambda i: (i, 0, 0)),
            pl.BlockSpec((bb, WINDOW, SWA_KV_W), lambda i: (i, 0, 0)),
            pl.BlockSpec((bb, WINDOW, SWA_KV_W), lambda i: (i, 0, 0)),
        ],
        out_shape=[
            jax.ShapeDtypeStruct((B, SWA_HEADS, SWA_KV_W), f32),
            jax.ShapeDtypeStruct((B, WINDOW, SWA_KV_W), f32),
            jax.ShapeDtypeStruct((B, WINDOW, SWA_KV_W), f32),
        ],
        compiler_params=_cparams(("parallel",)),
        name="swa_step",
    )(q_exp, proj3, cache_k, cache_v, cos, sin, sinks_col)


def _merge_kernel(og_ref, os_ref, gg_ref, gs_ref, x_ref, wg_ref, ws_ref, wo_ref, gq_ref, wq_ref, xo_ref, qm_ref):
    p_gdn = _dot(og_ref[...], wg_ref[...])
    p_swa = _dot(os_ref[...], ws_ref[...])
    merged = jax.nn.sigmoid(gg_ref[...]) * p_gdn + jax.nn.sigmoid(gs_ref[...]) * p_swa
    x_new = x_ref[...] + _dot(merged.astype(bf16), wo_ref[...])
    xo_ref[...] = x_new
    qm_ref[...] = _dot(_rmsnorm(x_new, gq_ref[...]).astype(bf16), wq_ref[...])


def _merge(og, os_, proj, off_gg, x, wg, ws, wo, gq, wq):
    M, D = x.shape
    tm = min(256, M)
    assert M % tm == 0 and off_gg % D == 0
    const = lambda shape: pl.BlockSpec(shape, lambda i: (0, 0), pipeline_mode=pl.Buffered(1))
    return pl.pallas_call(
        _merge_kernel,
        grid=(M // tm,),
        in_specs=[
            pl.BlockSpec((tm, GDN_W), lambda i: (i, 0)),
            pl.BlockSpec((tm, SWA_Q_W), lambda i: (i, 0)),
            pl.BlockSpec((tm, D), lambda i: (i, off_gg // D)),
            pl.BlockSpec((tm, D), lambda i: (i, off_gg // D + 1)),
            pl.BlockSpec((tm, D), lambda i: (i, 0)),
            const((GDN_W, D)),
            const((SWA_Q_W, D)),
            const((D, D)),
            const((1, D)),
            const((D, MEM_W)),
        ],
        out_specs=[
            pl.BlockSpec((tm, D), lambda i: (i, 0)),
            pl.BlockSpec((tm, MEM_W), lambda i: (i, 0)),
        ],
        out_shape=[jax.ShapeDtypeStruct((M, D), f32), jax.ShapeDtypeStruct((M, MEM_W), f32)],
        compiler_params=_cparams(("parallel",)),
        name="merge",
    )(og, os_, proj, proj, x, wg, ws, wo, gq, wq)


def _mem_attn_kernel(q_ref, k_ref, v_ref, x_ref, wo_ref, o_ref):
    tq = q_ref.shape[1]
    q = q_ref[0]
    if tq < SUBLANES:
        q = jnp.broadcast_to(q[0:1], (SUBLANES, q.shape[-1]))
    q = q.astype(bf16)
    scale = MEM_HD ** -0.5
    outs = []
    for h in range(MEM_HEADS):
        sl = slice(h * MEM_HD, (h + 1) * MEM_HD)
        s = _dot_nt(q[:, sl], k_ref[0, :, sl].astype(bf16)) * scale
        m = jnp.max(s, -1, keepdims=True)
        p = jnp.exp(s - m)
        p = p / jnp.sum(p, -1, keepdims=True)
        outs.append(_dot(p.astype(bf16), v_ref[0, :, sl].astype(bf16)))
    o = _dot(jnp.concatenate(outs, axis=-1).astype(bf16), wo_ref[...])
    o_ref[0] = x_ref[0] + o[0:tq]


def _mem_attn(qm, mem_k, mem_v, x, wo):
    B, T, D = x.shape
    mt = mem_k.shape[1]
    tq = min(512, T)
    assert T % tq == 0
    return pl.pallas_call(
        _mem_attn_kernel,
        grid=(B, T // tq),
        in_specs=[
            pl.BlockSpec((1, tq, MEM_W), lambda b, t: (b, t, 0)),
            pl.BlockSpec((1, mt, MEM_W), lambda b, t: (b, 0, 0)),
            pl.BlockSpec((1, mt, MEM_W), lambda b, t: (b, 0, 0)),
            pl.BlockSpec((1, tq, D), lambda b, t: (b, t, 0)),
            pl.BlockSpec((MEM_W, D), lambda b, t: (0, 0)),
        ],
        out_specs=pl.BlockSpec((1, tq, D), lambda b, t: (b, t, 0)),
        out_shape=jax.ShapeDtypeStruct((B, T, D), f32),
        compiler_params=_cparams(("parallel", "parallel")),
        name="mem_attn",
    )(qm, mem_k, mem_v, x, wo)


def _rope_tables(pos):
    half = SWA_HD // 2
    inv_freq = ROPE_THETA ** (-jnp.arange(half, dtype=f32) / half)
    ang = pos.astype(f32)[:, None] * inv_freq[None, :]
    cos = jnp.cos(ang)
    sin = jnp.sin(ang)
    reps = SWA_KV_W // SWA_HD
    return jnp.tile(jnp.concatenate([cos, cos], -1), (1, reps)), jnp.tile(jnp.concatenate([-sin, sin], -1), (1, reps))


def kernel(x_prompt, x_sample, state_gdn, state_conv, cache_swa_k, cache_swa_v, cache_mem_k, cache_mem_v, mem_prompt, norm_ffn1, ffn1_w1, ffn1_w3, ffn1_w2, norm_mix, w_in, conv_w, gdn_A_log, gdn_dt_bias, gdn_norm, swa_sinks, w_br_gdn, w_br_swa, w_out, norm_mem_q, norm_mem_kv, w_mem_q, w_mem_k, w_mem_v, w_mem_o, norm_ffn2, ffn2_w1, ffn2_w3, ffn2_w2, norm_final):
    Bp, Tp, D = x_prompt.shape
    Bs, Ts, _ = x_sample.shape
    assert Ts == 1
    depth = norm_ffn1.shape[0]
    n_mem = mem_prompt.shape[1]
    group = SWA_HEADS // SWA_KV_HEADS

    off_gs = OFF_GG + D
    off_q = off_gs + D
    off_kv = off_q + SWA_Q_W
    off_ba = off_kv + 2 * SWA_KV_W
    d_in_pad = -(-(off_ba + LANES) // PROJ_TN) * PROJ_TN
    o_b = CONV_CH + GDN_W
    o_q = o_b + 2 * GDN_HEADS
    o_gg = o_q + SWA_Q_W + 2 * SWA_KV_W

    cos_p, sin_p = _rope_tables(jnp.arange(Tp, dtype=jnp.int32))
    cos_s, sin_s = _rope_tables(PAST_LEN + jnp.arange(Ts, dtype=jnp.int32))
    eye_kv = jnp.eye(SWA_KV_HEADS, dtype=f32)
    row = lambda v: v.reshape(1, -1)

    hp = x_prompt.reshape(Bp * Tp, D)
    hs = x_sample.reshape(Bs, D)
    outs = [[] for _ in range(10)]
    for l in range(depth):
        wi = w_in[l]
        w_in_r = jnp.concatenate(
            [wi[:, :o_b].astype(bf16), wi[:, o_gg:].astype(bf16), wi[:, o_q:o_gg].astype(bf16), wi[:, o_b:o_q].astype(bf16),
             jnp.zeros((D, d_in_pad - off_ba - 2 * GDN_HEADS), bf16)], axis=1)
        ffn1 = (row(norm_ffn1[l]), ffn1_w1[l].astype(bf16), ffn1_w3[l].astype(bf16), ffn1_w2[l].astype(bf16))
        ffn2 = (row(norm_ffn2[l]), ffn2_w1[l].astype(bf16), ffn2_w3[l].astype(bf16), ffn2_w2[l].astype(bf16))
        last = l == depth - 1
        gfin = row(norm_final)
        avec = jnp.zeros((1, LANES), f32).at[0, GDN_HEADS : 2 * GDN_HEADS].set(gdn_A_log[l])
        dvec = jnp.zeros((1, LANES), f32).at[0, GDN_HEADS : 2 * GDN_HEADS].set(gdn_dt_bias[l])
        gdn_common = (conv_w[l], avec, dvec, row(gdn_norm[l]))
        merge_w = (w_br_gdn[l].astype(bf16), w_br_swa[l].astype(bf16), w_out[l].astype(bf16),
                   row(norm_mem_q[l]), w_mem_q[l].astype(bf16))
        wmo = w_mem_o[l].astype(bf16)
        tn_in = PROJ_TN

        x1 = _ffn(hp, *ffn1, gfin, False)
        proj = _norm_matmul(x1, row(norm_mix[l]), w_in_r, tn_in)
        og, s_new = _gdn_prompt(proj, Bp, Tp, off_ba, *gdn_common)
        os_, kc = _swa_prompt(proj, Bp, Tp, off_q, off_kv, cos_p, sin_p, swa_sinks[l])
        x2, qm = _merge(og, os_, proj, OFF_GG, x1, *merge_w)
        mem_x = mem_prompt.reshape(Bp * n_mem, D)
        mk = _norm_matmul(mem_x, row(norm_mem_kv[l]), w_mem_k[l].astype(bf16), MEM_W)
        mv = _norm_matmul(mem_x, row(norm_mem_kv[l]), w_mem_v[l].astype(bf16), MEM_W)
        x3 = _mem_attn(qm.reshape(Bp, Tp, MEM_W), mk.reshape(Bp, n_mem, MEM_W), mv.reshape(Bp, n_mem, MEM_W),
                       x2.reshape(Bp, Tp, D), wmo)
        hp = _ffn(x3.reshape(Bp * Tp, D), *ffn2, gfin, last)
        proj_b = proj.reshape(Bp, Tp, d_in_pad)
        outs[0].append(s_new)
        outs[1].append(proj_b[:, Tp - (GDN_CONV - 1) :, :CONV_CH])
        outs[2].append(kc.reshape(Bp, WINDOW, SWA_KV_HEADS, SWA_HD))
        outs[3].append(proj_b[:, Tp - WINDOW :, off_kv + SWA_KV_W : off_kv + 2 * SWA_KV_W].reshape(Bp, WINDOW, SWA_KV_HEADS, SWA_HD))
        outs[4].append(mk.reshape(Bp, n_mem, MEM_HEADS, MEM_HD))
        outs[5].append(mv.reshape(Bp, n_mem, MEM_HEADS, MEM_HD))

        x1 = _ffn(hs, *ffn1, gfin, False)
        proj = _norm_matmul(x1, row(norm_mix[l]), w_in_r, tn_in)
        proj3 = proj.reshape(Bs, 1, d_in_pad)
        og, s_new = _gdn_step(proj3, off_ba, state_conv[l], state_gdn[l], *gdn_common)
        q_raw = proj[:, off_q : off_q + SWA_Q_W].reshape(Bs, SWA_KV_HEADS, group, 1, SWA_HD)
        q_exp = (q_raw * eye_kv[None, :, None, :, None]).reshape(Bs, SWA_HEADS, SWA_KV_W)
        ck = cache_swa_k[l].reshape(Bs, WINDOW, SWA_KV_W)
        cv = cache_swa_v[l].reshape(Bs, WINDOW, SWA_KV_W)
        r, nk, nv = _swa_step(q_exp, proj3, off_kv, ck, cv, cos_s, sin_s, swa_sinks[l].reshape(SWA_HEADS, 1))
        r5 = r.reshape(Bs, SWA_KV_HEADS, group, SWA_KV_HEADS, SWA_HD)
        kvh = jnp.arange(SWA_KV_HEADS)
        os_ = jnp.transpose(r5[:, kvh, :, kvh, :], (1, 0, 2, 3)).reshape(Bs, SWA_Q_W).astype(bf16)
        x2, qm = _merge(og.reshape(Bs, GDN_W), os_, proj, OFF_GG, x1, *merge_w)
        x3 = _mem_attn(qm.reshape(Bs, 1, MEM_W), cache_mem_k[l].reshape(Bs, n_mem, MEM_W),
                       cache_mem_v[l].reshape(Bs, n_mem, MEM_W), x2.reshape(Bs, 1, D), wmo)
        hs = _ffn(x3.reshape(Bs, D), *ffn2, gfin, last)
        outs[6].append(s_new)
        outs[7].append(jnp.concatenate([state_conv[l][:, 1:], proj3[:, :, :CONV_CH]], axis=1))
        outs[8].append(nk.reshape(Bs, WINDOW, SWA_KV_HEADS, SWA_HD))
        outs[9].append(nv.reshape(Bs, WINDOW, SWA_KV_HEADS, SWA_HD))

    return (hp.reshape(Bp, Tp, D), hs.reshape(Bs, Ts, D), *(jnp.stack(o) for o in outs))
```

```python
import functools
import math

import jax
import jax.numpy as jnp
from jax import lax
from jax.experimental import pallas as pl
from jax.experimental.pallas import tpu as pltpu

f32 = jnp.float32
bf16 = jnp.bfloat16

PAST_LEN = 16384
GDN_HEADS = 8
GDN_D = 128
GDN_CONV = 4
SWA_HEADS = 16
SWA_KV_HEADS = 4
SWA_HD = 64
WINDOW = 128
ROPE_THETA = 10000.0
MEM_HEADS = 4
MEM_HD = 128
EPS = 1e-6
L2_EPS = 1e-6

LANES = 128
SUBLANES = 8
CHUNK = 128
GDN_HEAD_GROUP = 8
FFN_SLAB = 64
PROJ_TN = 768
VMEM_LIMIT = 58 * 1024 * 1024

GDN_W = GDN_HEADS * GDN_D
CONV_CH = 3 * GDN_W
SWA_Q_W = SWA_HEADS * SWA_HD
SWA_KV_W = SWA_KV_HEADS * SWA_HD
MEM_W = MEM_HEADS * MEM_HD

OFF_Z = CONV_CH
OFF_GG = OFF_Z + GDN_W


def _cparams(semantics):
    return pltpu.CompilerParams(dimension_semantics=semantics, vmem_limit_bytes=VMEM_LIMIT)


def _dot(a, b):
    return jnp.dot(a, b, preferred_element_type=f32)


def _dot_nt(a, b):
    return lax.dot_general(a, b, (((1,), (1,)), ((), ())), preferred_element_type=f32)


def _rmsnorm(x, g):
    return x * lax.rsqrt(jnp.mean(x * x, -1, keepdims=True) + EPS) * g


def _silu(x):
    return x * jax.nn.sigmoid(x)


def _softplus(x):
    return jnp.maximum(x, 0.0) + jnp.log1p(jnp.exp(-jnp.abs(x)))


def _ffn_kernel(x_ref, g_ref, w1_ref, w3_ref, w2_ref, gf_ref, o_ref, h_ref, *, final_norm):
    j = pl.program_id(1)
    slab = min(FFN_SLAB, x_ref.shape[0])
    n_slabs = x_ref.shape[0] // slab

    def over_slabs(body):
        def step(s, carry):
            body(pl.ds(pl.multiple_of(s * slab, slab), slab))
            return carry
        lax.fori_loop(0, n_slabs, step, 0, unroll=min(4, n_slabs))

    @pl.when(j == 0)
    def _():
        def prologue(rows):
            h_ref[rows, :] = _rmsnorm(x_ref[rows, :], g_ref[...]).astype(bf16)
            o_ref[rows, :] = jnp.zeros((slab, o_ref.shape[1]), f32)
        over_slabs(prologue)

    h = h_ref[...]
    a = _dot(h, w1_ref[...])
    b = _dot(h, w3_ref[...])
    o_ref[...] += _dot((_silu(a) * b).astype(bf16), w2_ref[...])

    @pl.when(j == pl.num_programs(1) - 1)
    def _():
        def epilogue(rows):
            y = x_ref[rows, :] + 0.5 * o_ref[rows, :]
            if final_norm:
                y = _rmsnorm(y, gf_ref[...])
            o_ref[rows, :] = y
        over_slabs(epilogue)


def _ffn(x, g, w1, w3, w2, gf, final_norm):
    M, D = x.shape
    F = w1.shape[1]
    tm = min(1024, M)
    tf = 512
    assert M % tm == 0 and F % tf == 0
    return pl.pallas_call(
        functools.partial(_ffn_kernel, final_norm=final_norm),
        grid=(M // tm, F // tf),
        in_specs=[
            pl.BlockSpec((tm, D), lambda i, j: (i, 0)),
            pl.BlockSpec((1, D), lambda i, j: (0, 0)),
            pl.BlockSpec((D, tf), lambda i, j: (0, j)),
            pl.BlockSpec((D, tf), lambda i, j: (0, j)),
            pl.BlockSpec((tf, D), lambda i, j: (j, 0)),
            pl.BlockSpec((1, D), lambda i, j: (0, 0)),
        ],
        out_specs=pl.BlockSpec((tm, D), lambda i, j: (i, 0)),
        out_shape=jax.ShapeDtypeStruct((M, D), f32),
        scratch_shapes=[pltpu.VMEM((tm, D), bf16)],
        compiler_params=_cparams(("parallel", "arbitrary")),
        name="ffn",
    )(x, g, w1, w3, w2, gf)


def _norm_matmul_kernel(x_ref, g_ref, w_ref, o_ref, h_ref):
    @pl.when(pl.program_id(1) == 0)
    def _():
        h_ref[...] = _rmsnorm(x_ref[...], g_ref[...]).astype(bf16)

    o_ref[...] = _dot(h_ref[...], w_ref[...])


def _norm_matmul(x, g, w, tn):
    M, D = x.shape
    N = w.shape[1]
    tm = min(1024, M)
    assert M % tm == 0 and N % tn == 0
    return pl.pallas_call(
        _norm_matmul_kernel,
        grid=(M // tm, N // tn),
        in_specs=[
            pl.BlockSpec((tm, D), lambda i, j: (i, 0)),
            pl.BlockSpec((1, D), lambda i, j: (0, 0)),
            pl.BlockSpec((D, tn), lambda i, j: (0, j)),
        ],
        out_specs=pl.BlockSpec((tm, tn), lambda i, j: (i, j)),
        out_shape=jax.ShapeDtypeStruct((M, N), f32),
        scratch_shapes=[pltpu.VMEM((tm, D), bf16)],
        compiler_params=_cparams(("parallel", "arbitrary")),
        name="norm_matmul",
    )(x, g, w)


def _bdot(a, b):
    return _dot(a.astype(bf16), b.astype(bf16))


def _unit_lower_inverse(nmats, row, col):
    eye = jnp.where(row == col, 1.0, 0.0).astype(f32)

    def same_block(size):
        return (row // size) == (col // size)

    blk = same_block(SUBLANES)
    a1 = [jnp.where(blk, n, 0.0) for n in nmats]
    a2 = [_bdot(a, a) for a in a1]
    a4 = [_bdot(a, a) for a in a2]
    ts = [_bdot(eye - x1, eye + x2) for x1, x2 in zip(a1, a2)]
    ts = [_bdot(t, eye + x4) for t, x4 in zip(ts, a4)]
    size = SUBLANES
    while size < CHUNK:
        nxt = same_block(2 * size)
        sel = jnp.logical_and(nxt, jnp.logical_not(blk))
        tbs = [t.astype(bf16) for t in ts]
        xs = [_dot(jnp.where(sel, n, 0.0).astype(bf16), tb) for n, tb in zip(nmats, tbs)]
        ts = [t - _dot(tb, x.astype(bf16)) for t, tb, x in zip(ts, tbs, xs)]
        blk = nxt
        size *= 2
    return ts


def _conv_silu_slab(x_ref, top_ref, w_ref, sl):
    y = None
    for tap in range(GDN_CONV):
        back = GDN_CONV - 1 - tap
        window = jnp.concatenate(
            [top_ref[SUBLANES - back : 2 * SUBLANES - back, sl], x_ref[SUBLANES - back : CHUNK - back, sl]], axis=0)
        term = window * w_ref[tap : tap + 1, sl]
        y = term if y is None else y + term
    return _silu(y)


def _delta_rule_chunk(heads, gcum, gcum_t, beta_t, row, col, q_ref, k_ref, v_ref, z_ref, gn_ref, s_ref, og_ref):
    heads = list(heads)
    n = range(len(heads))
    incl = row >= col
    strict = row > col
    gc = [gcum[:, GDN_HEADS + h : GDN_HEADS + h + 1] for h in heads]
    gr = [gcum_t[GDN_HEADS + h : GDN_HEADS + h + 1, :] for h in heads]
    beta = [beta_t[:, h : h + 1] for h in heads]
    kb = [k_ref[h].astype(bf16) for h in heads]
    kk = [_dot_nt(kb[i], kb[i]) for i in n]
    qk = [_dot_nt(q_ref[h].astype(bf16), kb[i]) for i, h in enumerate(heads)]
    e = [jnp.exp(jnp.where(incl, gc[i] - gr[i], 0.0)) for i in n]
    nmat = [beta[i] * kk[i] * jnp.where(strict, e[i], 0.0) for i in n]
    qkd = [(qk[i] * jnp.where(incl, e[i], 0.0)).astype(bf16) for i in n]
    tinv = _unit_lower_inverse(nmat, row, col)
    gamma = [jnp.exp(gc[i]) for i in n]
    rhs = [jnp.concatenate([beta[i] * v_ref[h], (beta[i] * gamma[i]) * k_ref[h]], axis=-1).astype(bf16)
           for i, h in enumerate(heads)]
    sol = [_dot(tinv[i].astype(bf16), rhs[i]) for i in n]
    sb = [s_ref[h].astype(bf16) for h in heads]
    ub = [(sol[i][:, :GDN_D] - _dot(sol[i][:, GDN_D:].astype(bf16), sb[i])).astype(bf16) for i in n]
    o = [_dot((q_ref[h] * gamma[i]).astype(bf16), sb[i]) + _dot(qkd[i], ub[i]) for i, h in enumerate(heads)]
    for i, h in enumerate(heads):
        g_last = gc[i][CHUNK - 1 : CHUNK, :]
        k_end = k_ref[h] * jnp.exp(g_last - gc[i])
        s_ref[h] = jnp.exp(g_last) * s_ref[h] + _dot(k_end.T.astype(bf16), ub[i])
        zh = z_ref[:, h * GDN_D : (h + 1) * GDN_D]
        og_ref[:, h * GDN_D : (h + 1) * GDN_D] = (_rmsnorm(o[i], gn_ref[...]) * _silu(zh)).astype(bf16)


def _gdn_prompt_kernel(qkv_ref, z_ref, ba_ref, cw_ref, av_ref, dv_ref, gn_ref,
                       og_ref, so_ref, s_ref, carry_ref, q_ref, k_ref, v_ref):
    c = pl.program_id(1)

    @pl.when(c == 0)
    def _():
        s_ref[...] = jnp.zeros_like(s_ref)
        carry_ref[0:SUBLANES, :] = jnp.zeros((SUBLANES, CONV_CH), f32)

    carry_ref[SUBLANES : 2 * SUBLANES, :] = qkv_ref[0:SUBLANES, :]
    for j in range(3 * GDN_HEADS):
        sl = slice(j * LANES, (j + 1) * LANES)
        y = _conv_silu_slab(qkv_ref, carry_ref, cw_ref, sl)
        h = j % GDN_HEADS
        if j < 2 * GDN_HEADS:
            y = y * lax.rsqrt(jnp.sum(y * y, -1, keepdims=True) + L2_EPS)
            if j < GDN_HEADS:
                q_ref[h] = y * (GDN_D ** -0.5)
            else:
                k_ref[h] = y
        else:
            v_ref[h] = y
    carry_ref[0:SUBLANES, :] = qkv_ref[CHUNK - SUBLANES : CHUNK, :]

    row = lax.broadcasted_iota(jnp.int32, (CHUNK, CHUNK), 0)
    col = lax.broadcasted_iota(jnp.int32, (CHUNK, CHUNK), 1)

    ba = ba_ref[...]
    beta_t = jax.nn.sigmoid(ba)
    g_t = -jnp.exp(av_ref[...]) * _softplus(ba + dv_ref[...])
    gcum = g_t
    shift = 1
    while shift < CHUNK:
        gcum = gcum + jnp.where(row >= shift, pltpu.roll(gcum, shift, 0), 0.0)
        shift *= 2
    gcum_t = gcum.T

    for first in range(0, GDN_HEADS, GDN_HEAD_GROUP):
        _delta_rule_chunk(range(first, first + GDN_HEAD_GROUP), gcum, gcum_t, beta_t, row, col,
                          q_ref, k_ref, v_ref, z_ref, gn_ref, s_ref, og_ref)

    @pl.when(c == pl.num_programs(1) - 1)
    def _():
        so_ref[0] = s_ref[...]


def _gdn_prompt(proj, B, T, off_ba, cw, avec, dvec, gn):
    assert T % CHUNK == 0
    nc = T // CHUNK
    return pl.pallas_call(
        _gdn_prompt_kernel,
        grid=(B, nc),
        in_specs=[
            pl.BlockSpec((CHUNK, CONV_CH), lambda b, c: (b * nc + c, 0)),
            pl.BlockSpec((CHUNK, GDN_W), lambda b, c: (b * nc + c, OFF_Z // GDN_W)),
            pl.BlockSpec((CHUNK, LANES), lambda b, c: (b * nc + c, off_ba // LANES)),
            pl.BlockSpec((GDN_CONV, CONV_CH), lambda b, c: (0, 0)),
            pl.BlockSpec((1, LANES), lambda b, c: (0, 0)),
            pl.BlockSpec((1, LANES), lambda b, c: (0, 0)),
            pl.BlockSpec((1, GDN_D), lambda b, c: (0, 0)),
        ],
        out_specs=[
            pl.BlockSpec((CHUNK, GDN_W), lambda b, c: (b * nc + c, 0)),
            pl.BlockSpec((1, GDN_HEADS, GDN_D, GDN_D), lambda b, c: (b, 0, 0, 0)),
        ],
        out_shape=[
            jax.ShapeDtypeStruct((B * T, GDN_W), bf16),
            jax.ShapeDtypeStruct((B, GDN_HEADS, GDN_D, GDN_D), f32),
        ],
        scratch_shapes=[
            pltpu.VMEM((GDN_HEADS, GDN_D, GDN_D), f32),
            pltpu.VMEM((2 * SUBLANES, CONV_CH), f32),
            pltpu.VMEM((GDN_HEADS, CHUNK, GDN_D), f32),
            pltpu.VMEM((GDN_HEADS, CHUNK, GDN_D), f32),
            pltpu.VMEM((GDN_HEADS, CHUNK, GDN_D), f32),
        ],
        compiler_params=_cparams(("parallel", "arbitrary")),
        name="gdn_prompt",
    )(proj, proj, proj, cw, avec, dvec, gn)


def _gdn_step_kernel(qkv_ref, z_ref, ba_ref, sc_ref, s0_ref, cw_ref, av_ref, dv_ref, gn_ref, og_ref, so_ref):
    x_new = qkv_ref[0]
    taps = cw_ref[...]
    y = jnp.sum(sc_ref[0] * taps[0 : GDN_CONV - 1, :], axis=0, keepdims=True) + x_new * taps[GDN_CONV - 1 : GDN_CONV, :]
    y = _silu(y)
    ba = ba_ref[0]
    beta_t = jax.nn.sigmoid(ba)
    gamma_t = jnp.exp(-jnp.exp(av_ref[...]) * _softplus(ba + dv_ref[...]))
    z = z_ref[0]
    heads = range(GDN_HEADS)
    head = lambda base, h: y[:, base + h * GDN_D : base + (h + 1) * GDN_D]
    q = [head(0, h) for h in heads]
    k = [head(GDN_W, h) for h in heads]
    v = [head(2 * GDN_W, h) for h in heads]
    q = [q[h] * lax.rsqrt(jnp.sum(q[h] * q[h], -1, keepdims=True) + L2_EPS) * (GDN_D ** -0.5) for h in heads]
    k = [k[h] * lax.rsqrt(jnp.sum(k[h] * k[h], -1, keepdims=True) + L2_EPS) for h in heads]
    beta = [beta_t[:, h : h + 1] for h in heads]
    gamma = [gamma_t[:, GDN_HEADS + h : GDN_HEADS + h + 1] for h in heads]
    k_col = [jnp.broadcast_to(k[h], (GDN_D, GDN_D)).T for h in heads]
    q_col = [jnp.broadcast_to(q[h], (GDN_D, GDN_D)).T for h in heads]
    k_s = [jnp.sum(k_col[h] * s0_ref[0, h], axis=0, keepdims=True) for h in heads]
    q_s = [jnp.sum(q_col[h] * s0_ref[0, h], axis=0, keepdims=True) for h in heads]
    u = [beta[h] * v[h] - (beta[h] * gamma[h]) * k_s[h] for h in heads]
    o = [gamma[h] * q_s[h] + jnp.sum(q[h] * k[h], -1, keepdims=True) * u[h] for h in heads]
    for h in heads:
        so_ref[0, h] = gamma[h] * s0_ref[0, h] + k_col[h] * u[h]
        zh = z[:, h * GDN_D : (h + 1) * GDN_D]
        og_ref[0, :, h * GDN_D : (h + 1) * GDN_D] = (_rmsnorm(o[h], gn_ref[...]) * _silu(zh)).astype(bf16)


def _gdn_step(proj3, off_ba, state_conv, state_gdn, cw, avec, dvec, gn):
    B = proj3.shape[0]
    return pl.pallas_call(
        _gdn_step_kernel,
        grid=(B,),
        in_specs=[
            pl.BlockSpec((1, 1, CONV_CH), lambda b: (b, 0, 0)),
            pl.BlockSpec((1, 1, GDN_W), lambda b: (b, 0, OFF_Z // GDN_W)),
            pl.BlockSpec((1, 1, LANES), lambda b: (b, 0, off_ba // LANES)),
            pl.BlockSpec((1, GDN_CONV - 1, CONV_CH), lambda b: (b, 0, 0)),
            pl.BlockSpec((1, GDN_HEADS, GDN_D, GDN_D), lambda b: (b, 0, 0, 0)),
            pl.BlockSpec((GDN_CONV, CONV_CH), lambda b: (0, 0)),
            pl.BlockSpec((1, LANES), lambda b: (0, 0)),
            pl.BlockSpec((1, LANES), lambda b: (0, 0)),
            pl.BlockSpec((1, GDN_D), lambda b: (0, 0)),
        ],
        out_specs=[
            pl.BlockSpec((1, 1, GDN_W), lambda b: (b, 0, 0)),
            pl.BlockSpec((1, GDN_HEADS, GDN_D, GDN_D), lambda b: (b, 0, 0, 0)),
        ],
        out_shape=[
            jax.ShapeDtypeStruct((B, 1, GDN_W), bf16),
            jax.ShapeDtypeStruct((B, GDN_HEADS, GDN_D, GDN_D), f32),
        ],
        compiler_params=_cparams(("parallel",)),
        name="gdn_step",
    )(proj3, proj3, proj3, state_conv, state_gdn, cw, avec, dvec, gn)


def _rope(x, cos, sin_signed):
    width = x.shape[-1]
    lane = lax.broadcasted_iota(jnp.int32, x.shape, x.ndim - 1)
    first_half = (lane % SWA_HD) < (SWA_HD // 2)
    rot = jnp.where(first_half, pltpu.roll(x, width - SWA_HD // 2, x.ndim - 1), pltpu.roll(x, SWA_HD // 2, x.ndim - 1))
    return x * cos + rot * sin_signed


def _head_halves(x2, head_parity, lane):
    swapped = pltpu.roll(x2, SWA_HD, 1)
    lo_src, hi_src = (x2, swapped) if head_parity == 0 else (swapped, x2)
    return jnp.where(lane < SWA_HD, lo_src, 0.0), jnp.where(lane >= SWA_HD, hi_src, 0.0)


def _swa_prompt_kernel(sinks_ref, q_ref, kv_ref, cos_ref, sin_ref, os_ref, kc_ref, kprev_ref, vprev_ref):
    n = pl.program_id(1)

    @pl.when(n == 0)
    def _():
        kprev_ref[...] = jnp.zeros_like(kprev_ref)
        vprev_ref[...] = jnp.zeros_like(vprev_ref)

    cos = cos_ref[...]
    sin = sin_ref[...]
    kv = kv_ref[...]
    k_cur = _rope(kv[:, :SWA_KV_W], cos, sin)
    v_cur = kv[:, SWA_KV_W:]
    kc_ref[0] = k_cur
    k_all = jnp.concatenate([kprev_ref[...], k_cur], axis=0)
    v_all = jnp.concatenate([vprev_ref[...], v_cur], axis=0)
    kprev_ref[...] = k_cur
    vprev_ref[...] = v_cur

    row = lax.broadcasted_iota(jnp.int32, (WINDOW, 2 * WINDOW), 0)
    col = lax.broadcasted_iota(jnp.int32, (WINDOW, 2 * WINDOW), 1)
    prev_visible = jnp.logical_and(jnp.logical_and(col > row, col < WINDOW), n > 0)
    own_visible = jnp.logical_and(col >= WINDOW, (col - WINDOW) <= row)
    mask = jnp.logical_or(prev_visible, own_visible)
    lane = lax.broadcasted_iota(jnp.int32, (2 * WINDOW, LANES), 1)
    scale = SWA_HD ** -0.5
    group = SWA_HEADS // SWA_KV_HEADS
    k_half, v_half, q2 = [], [], []
    for h in range(SWA_KV_HEADS):
        pair = slice((h // 2) * LANES, (h // 2 + 1) * LANES)
        k_half.append([a.astype(bf16) for a in _head_halves(k_all[:, pair], h % 2, lane)])
        v_half.append([a.astype(bf16) for a in _head_halves(v_all[:, pair], h % 2, lane)])
        q_h = _rope(q_ref[:, h * group * SWA_HD : (h + 1) * group * SWA_HD], cos, sin) * scale
        q2.append([q_h[:, j * LANES : (j + 1) * LANES].astype(bf16) for j in range(group // 2)])
    heads = [(h, j, par) for h in range(SWA_KV_HEADS) for j in range(group // 2) for par in range(2)]
    sink = [sinks_ref[h * group + 2 * j + par] for h, j, par in heads]
    s = [jnp.where(mask, _dot_nt(q2[h][j], k_half[h][par]), -jnp.inf) for h, j, par in heads]
    m = [jnp.maximum(jnp.max(s[i], -1, keepdims=True), sink[i]) for i in range(len(heads))]
    p = [jnp.exp(s[i] - m[i]) for i in range(len(heads))]
    inv = [1.0 / (jnp.sum(p[i], -1, keepdims=True) + jnp.exp(sink[i] - m[i])) for i in range(len(heads))]
    o = [_dot(p[i].astype(bf16), v_half[h][par]) * inv[i] for i, (h, j, par) in enumerate(heads)]
    for i in range(0, len(heads), 2):
        os_ref[:, i * SWA_HD : i * SWA_HD + LANES] = (o[i] + o[i + 1]).astype(bf16)


def _swa_prompt(proj, B, T, off_q, off_kv, cos, sin, sinks):
    assert T % WINDOW == 0 and SWA_KV_W == 2 * LANES
    nb = T // WINDOW
    return pl.pallas_call(
        _swa_prompt_kernel,
        grid=(B, nb),
        in_specs=[
            pl.BlockSpec(memory_space=pltpu.SMEM),
            pl.BlockSpec((WINDOW, SWA_Q_W), lambda b, n: (b * nb + n, off_q // SWA_Q_W)),
            pl.BlockSpec((WINDOW, 2 * SWA_KV_W), lambda b, n: (b * nb + n, off_kv // (2 * SWA_KV_W))),
            pl.BlockSpec((WINDOW, SWA_KV_W), lambda b, n: (n, 0)),
            pl.BlockSpec((WINDOW, SWA_KV_W), lambda b, n: (n, 0)),
        ],
        out_specs=[
            pl.BlockSpec((WINDOW, SWA_Q_W), lambda b, n: (b * nb + n, 0)),
            pl.BlockSpec((1, WINDOW, SWA_KV_W), lambda b, n: (b, 0, 0)),
        ],
        out_shape=[
            jax.ShapeDtypeStruct((B * T, SWA_Q_W), bf16),
            jax.ShapeDtypeStruct((B, WINDOW, SWA_KV_W), f32),
        ],
        scratch_shapes=[pltpu.VMEM((WINDOW, SWA_KV_W), f32), pltpu.VMEM((WINDOW, SWA_KV_W), f32)],
        compiler_params=_cparams(("parallel", "arbitrary")),
        name="swa_prompt",
    )(sinks, proj, proj, cos, sin)


SWA_STEP_BATCH = 8


def _swa_step_kernel(qe_ref, kv_ref, ck_ref, cv_ref, cos_ref, sin_ref, sinks_ref, r_ref, nk_ref, nv_ref):
    cos = cos_ref[...]
    sin = sin_ref[...]
    sink = sinks_ref[...]
    row = lax.broadcasted_iota(jnp.int32, (WINDOW, SWA_KV_W), 0)
    scale = SWA_HD ** -0.5
    seqs = range(qe_ref.shape[0])
    kv = [kv_ref[i] for i in seqs]
    k_new = [_rope(kv[i][:, :SWA_KV_W], cos, sin) for i in seqs]
    keys = [jnp.where(row == WINDOW - 1, k_new[i], pltpu.roll(ck_ref[i], WINDOW - 1, 0)) for i in seqs]
    vals = [jnp.where(row == WINDOW - 1, kv[i][:, SWA_KV_W:], pltpu.roll(cv_ref[i], WINDOW - 1, 0)) for i in seqs]
    for i in seqs:
        nk_ref[i] = keys[i]
        nv_ref[i] = vals[i]
    q = [_rope(qe_ref[i], cos, sin) for i in seqs]
    s = [_dot_nt(q[i].astype(bf16), keys[i].astype(bf16)) * scale for i in seqs]
    m = [jnp.maximum(jnp.max(s[i], -1, keepdims=True), sink) for i in seqs]
    p = [jnp.exp(s[i] - m[i]) for i in seqs]
    denom = [jnp.sum(p[i], -1, keepdims=True) + jnp.exp(sink - m[i]) for i in seqs]
    for i in seqs:
        r_ref[i] = _dot((p[i] / denom[i]).astype(bf16), vals[i].astype(bf16))


def _swa_step(q_exp, proj3, off_kv, cache_k, cache_v, cos, sin, sinks_col):
    B = q_exp.shape[0]
    bb = math.gcd(B, SWA_STEP_BATCH)
    assert cache_k.shape[1] == WINDOW
    return pl.pallas_call(
        _swa_step_kernel,
        grid=(B // bb,),
        in_specs=[
            pl.BlockSpec((bb, SWA_HEADS, SWA_KV_W), lambda i: (i, 0, 0)),
            pl.BlockSpec((bb, 1, 2 * SWA_KV_W), lambda i: (i, 0, off_kv // (2 * SWA_KV_W))),
            pl.BlockSpec((bb, WINDOW, SWA_KV_W), lambda i: (i, 0, 0)),
            pl.BlockSpec((bb, WINDOW, SWA_KV_W), lambda i: (i, 0, 0)),
            pl.BlockSpec((1, SWA_KV_W), lambda i: (0, 0)),
            pl.BlockSpec((1, SWA_KV_W), lambda i: (0, 0)),
            pl.BlockSpec((SWA_HEADS, 1), lambda i: (0, 0)),
        ],
        out_specs=[
            pl.BlockSpec((bb, SWA_HEADS, SWA_KV_W), lambda i: (i, 0, 0)),
            pl.BlockSpec((bb, WINDOW, SWA_KV_W), lambda i: (i, 0, 0)),
            pl.BlockSpec((bb, WINDOW, SWA_KV_W), lambda i: (i, 0, 0)),
        ],
        out_shape=[
            jax.ShapeDtypeStruct((B, SWA_HEADS, SWA_KV_W), f32),
            jax.ShapeDtypeStruct((B, WINDOW, SWA_KV_W), f32),
            jax.ShapeDtypeStruct((B, WINDOW, SWA_KV_W), f32),
        ],
        compiler_params=_cparams(("parallel",)),
        name="swa_step",
    )(q_exp, proj3, cache_k, cache_v, cos, sin, sinks_col)


def _merge_kernel(og_ref, os_ref, gg_ref, gs_ref, x_ref, wg_ref, ws_ref, wo_ref, gq_ref, wq_ref, xo_ref, qm_ref):
    p_gdn = _dot(og_ref[...], wg_ref[...])
    p_swa = _dot(os_ref[...], ws_ref[...])
    merged = jax.nn.sigmoid(gg_ref[...]) * p_gdn + jax.nn.sigmoid(gs_ref[...]) * p_swa
    x_new = x_ref[...] + _dot(merged.astype(bf16), wo_ref[...])
    xo_ref[...] = x_new
    qm_ref[...] = _dot(_rmsnorm(x_new, gq_ref[...]).astype(bf16), wq_ref[...])


def _merge(og, os_, proj, off_gg, x, wg, ws, wo, gq, wq):
    M, D = x.shape
    tm = min(256, M)
    assert M % tm == 0 and off_gg % D == 0
    const = lambda shape: pl.BlockSpec(shape, lambda i: (0, 0), pipeline_mode=pl.Buffered(1))
    return pl.pallas_call(
        _merge_kernel,
        grid=(M // tm,),
        in_specs=[
            pl.BlockSpec((tm, GDN_W), lambda i: (i, 0)),
            pl.BlockSpec((tm, SWA_Q_W), lambda i: (i, 0)),
            pl.BlockSpec((tm, D), lambda i: (i, off_gg // D)),
            pl.BlockSpec((tm, D), lambda i: (i, off_gg // D + 1)),
            pl.BlockSpec((tm, D), lambda i: (i, 0)),
            const((GDN_W, D)),
            const((SWA_Q_W, D)),
            const((D, D)),
            const((1, D)),
            const((D, MEM_W)),
        ],
        out_specs=[
            pl.BlockSpec((tm, D), lambda i: (i, 0)),
            pl.BlockSpec((tm, MEM_W), lambda i: (i, 0)),
        ],
        out_shape=[jax.ShapeDtypeStruct((M, D), f32), jax.ShapeDtypeStruct((M, MEM_W), f32)],
        compiler_params=_cparams(("parallel",)),
        name="merge",
    )(og, os_, proj, proj, x, wg, ws, wo, gq, wq)


def _mem_attn_kernel(q_ref, k_ref, v_ref, x_ref, wo_ref, o_ref):
    tq = q_ref.shape[1]
    q = q_ref[0]
    if tq < SUBLANES:
        q = jnp.broadcast_to(q[0:1], (SUBLANES, q.shape[-1]))
    q = q.astype(bf16)
    scale = MEM_HD ** -0.5
    outs = []
    for h in range(MEM_HEADS):
        sl = slice(h * MEM_HD, (h + 1) * MEM_HD)
        s = _dot_nt(q[:, sl], k_ref[0, :, sl].astype(bf16)) * scale
        m = jnp.max(s, -1, keepdims=True)
        p = jnp.exp(s - m)
        p = p / jnp.sum(p, -1, keepdims=True)
        outs.append(_dot(p.astype(bf16), v_ref[0, :, sl].astype(bf16)))
    o = _dot(jnp.concatenate(outs, axis=-1).astype(bf16), wo_ref[...])
    o_ref[0] = x_ref[0] + o[0:tq]


def _mem_attn(qm, mem_k, mem_v, x, wo):
    B, T, D = x.shape
    mt = mem_k.shape[1]
    tq = min(512, T)
    assert T % tq == 0
    return pl.pallas_call(
        _mem_attn_kernel,
        grid=(B, T // tq),
        in_specs=[
            pl.BlockSpec((1, tq, MEM_W), lambda b, t: (b, t, 0)),
            pl.BlockSpec((1, mt, MEM_W), lambda b, t: (b, 0, 0)),
            pl.BlockSpec((1, mt, MEM_W), lambda b, t: (b, 0, 0)),
            pl.BlockSpec((1, tq, D), lambda b, t: (b, t, 0)),
            pl.BlockSpec((MEM_W, D), lambda b, t: (0, 0)),
        ],
        out_specs=pl.BlockSpec((1, tq, D), lambda b, t: (b, t, 0)),
        out_shape=jax.ShapeDtypeStruct((B, T, D), f32),
        compiler_params=_cparams(("parallel", "parallel")),
        name="mem_attn",
    )(qm, mem_k, mem_v, x, wo)


MEM_STEP_BATCH = 4


def _mem_attn_step_kernel(q_ref, k_ref, v_ref, o_ref):
    mt2 = k_ref.shape[1] // SUBLANES
    for i in range(q_ref.shape[0]):
        q8 = q_ref[i] * (MEM_HD ** -0.5)
        s = jnp.sum(k_ref[i].reshape(mt2, SUBLANES, MEM_HD) * q8, axis=-1, keepdims=True)
        m = jnp.max(s, axis=0)
        m = jnp.maximum(m, pltpu.roll(m, MEM_HEADS, 0))
        p = jnp.exp(s - m)
        l = jnp.sum(p, axis=0)
        l = l + pltpu.roll(l, MEM_HEADS, 0)
        o = jnp.sum(p * v_ref[i].reshape(mt2, SUBLANES, MEM_HD), axis=0)
        o_ref[i] = (o + pltpu.roll(o, MEM_HEADS, 0)) / l


def _mem_attn_step(q8, mem_k, mem_v):
    B, rows, _ = mem_k.shape
    assert 2 * MEM_HEADS == SUBLANES and rows % SUBLANES == 0
    bb = math.gcd(B, MEM_STEP_BATCH)
    return pl.pallas_call(
        _mem_attn_step_kernel,
        grid=(B // bb,),
        in_specs=[
            pl.BlockSpec((bb, SUBLANES, MEM_HD), lambda i: (i, 0, 0)),
            pl.BlockSpec((bb, rows, MEM_HD), lambda i: (i, 0, 0)),
            pl.BlockSpec((bb, rows, MEM_HD), lambda i: (i, 0, 0)),
        ],
        out_specs=pl.BlockSpec((bb, SUBLANES, MEM_HD), lambda i: (i, 0, 0)),
        out_shape=jax.ShapeDtypeStruct((B, SUBLANES, MEM_HD), f32),
        compiler_params=_cparams(("parallel",)),
        name="mem_attn_step",
    )(q8, mem_k, mem_v)


def _proj_residual_kernel(a_ref, w_ref, x_ref, o_ref):
    o_ref[...] = x_ref[...] + _dot(a_ref[...].astype(bf16), w_ref[...])


def _proj_residual(a, w, x):
    M, D = x.shape
    return pl.pallas_call(
        _proj_residual_kernel,
        out_shape=jax.ShapeDtypeStruct((M, D), f32),
        compiler_params=pltpu.CompilerParams(vmem_limit_bytes=VMEM_LIMIT),
        name="proj_residual",
    )(a, w, x)


def _rope_tables(pos):
    half = SWA_HD // 2
    inv_freq = ROPE_THETA ** (-jnp.arange(half, dtype=f32) / half)
    ang = pos.astype(f32)[:, None] * inv_freq[None, :]
    cos = jnp.cos(ang)
    sin = jnp.sin(ang)
    reps = SWA_KV_W // SWA_HD
    return jnp.tile(jnp.concatenate([cos, cos], -1), (1, reps)), jnp.tile(jnp.concatenate([-sin, sin], -1), (1, reps))


def kernel(x_prompt, x_sample, state_gdn, state_conv, cache_swa_k, cache_swa_v, cache_mem_k, cache_mem_v, mem_prompt, norm_ffn1, ffn1_w1, ffn1_w3, ffn1_w2, norm_mix, w_in, conv_w, gdn_A_log, gdn_dt_bias, gdn_norm, swa_sinks, w_br_gdn, w_br_swa, w_out, norm_mem_q, norm_mem_kv, w_mem_q, w_mem_k, w_mem_v, w_mem_o, norm_ffn2, ffn2_w1, ffn2_w3, ffn2_w2, norm_final):
    Bp, Tp, D = x_prompt.shape
    Bs, Ts, _ = x_sample.shape
    assert Ts == 1
    depth = norm_ffn1.shape[0]
    n_mem = mem_prompt.shape[1]
    group = SWA_HEADS // SWA_KV_HEADS

    off_gs = OFF_GG + D
    off_q = off_gs + D
    off_kv = off_q + SWA_Q_W
    off_ba = off_kv + 2 * SWA_KV_W
    d_in_pad = -(-(off_ba + LANES) // PROJ_TN) * PROJ_TN
    o_b = CONV_CH + GDN_W
    o_q = o_b + 2 * GDN_HEADS
    o_gg = o_q + SWA_Q_W + 2 * SWA_KV_W

    cos_p, sin_p = _rope_tables(jnp.arange(Tp, dtype=jnp.int32))
    cos_s, sin_s = _rope_tables(PAST_LEN + jnp.arange(Ts, dtype=jnp.int32))
    eye_kv = jnp.eye(SWA_KV_HEADS, dtype=f32)
    row = lambda v: v.reshape(1, -1)

    hp = x_prompt.reshape(Bp * Tp, D)
    hs = x_sample.reshape(Bs, D)
    outs = [[] for _ in range(10)]
    for l in range(depth):
        wi = w_in[l]
        w_in_r = jnp.concatenate(
            [wi[:, :o_b].astype(bf16), wi[:, o_gg:].astype(bf16), wi[:, o_q:o_gg].astype(bf16), wi[:, o_b:o_q].astype(bf16),
             jnp.zeros((D, d_in_pad - off_ba - 2 * GDN_HEADS), bf16)], axis=1)
        ffn1 = (row(norm_ffn1[l]), ffn1_w1[l].astype(bf16), ffn1_w3[l].astype(bf16), ffn1_w2[l].astype(bf16))
        ffn2 = (row(norm_ffn2[l]), ffn2_w1[l].astype(bf16), ffn2_w3[l].astype(bf16), ffn2_w2[l].astype(bf16))
        last = l == depth - 1
        gfin = row(norm_final)
        avec = jnp.zeros((1, LANES), f32).at[0, GDN_HEADS : 2 * GDN_HEADS].set(gdn_A_log[l])
        dvec = jnp.zeros((1, LANES), f32).at[0, GDN_HEADS : 2 * GDN_HEADS].set(gdn_dt_bias[l])
        gdn_common = (conv_w[l], avec, dvec, row(gdn_norm[l]))
        merge_w = (w_br_gdn[l].astype(bf16), w_br_swa[l].astype(bf16), w_out[l].astype(bf16),
                   row(norm_mem_q[l]), w_mem_q[l].astype(bf16))
        wmo = w_mem_o[l].astype(bf16)
        tn_in = PROJ_TN

        x1 = _ffn(hp, *ffn1, gfin, False)
        proj = _norm_matmul(x1, row(norm_mix[l]), w_in_r, tn_in)
        og, s_new = _gdn_prompt(proj, Bp, Tp, off_ba, *gdn_common)
        os_, kc = _swa_prompt(proj, Bp, Tp, off_q, off_kv, cos_p, sin_p, swa_sinks[l])
        x2, qm = _merge(og, os_, proj, OFF_GG, x1, *merge_w)
        mem_x = mem_prompt.reshape(Bp * n_mem, D)
        mk = _norm_matmul(mem_x, row(norm_mem_kv[l]), w_mem_k[l].astype(bf16), MEM_W)
        mv = _norm_matmul(mem_x, row(norm_mem_kv[l]), w_mem_v[l].astype(bf16), MEM_W)
        x3 = _mem_attn(qm.reshape(Bp, Tp, MEM_W), mk.reshape(Bp, n_mem, MEM_W), mv.reshape(Bp, n_mem, MEM_W),
                       x2.reshape(Bp, Tp, D), wmo)
        hp = _ffn(x3.reshape(Bp * Tp, D), *ffn2, gfin, last)
        proj_b = proj.reshape(Bp, Tp, d_in_pad)
        outs[0].append(s_new)
        outs[1].append(proj_b[:, Tp - (GDN_CONV - 1) :, :CONV_CH])
        outs[2].append(kc.reshape(Bp, WINDOW, SWA_KV_HEADS, SWA_HD))
        outs[3].append(proj_b[:, Tp - WINDOW :, off_kv + SWA_KV_W : off_kv + 2 * SWA_KV_W].reshape(Bp, WINDOW, SWA_KV_HEADS, SWA_HD))
        outs[4].append(mk.reshape(Bp, n_mem, MEM_HEADS, MEM_HD))
        outs[5].append(mv.reshape(Bp, n_mem, MEM_HEADS, MEM_HD))

        x1 = _ffn(hs, *ffn1, gfin, False)
        proj = _norm_matmul(x1, row(norm_mix[l]), w_in_r, tn_in)
        proj3 = proj.reshape(Bs, 1, d_in_pad)
        og, s_new = _gdn_step(proj3, off_ba, state_conv[l], state_gdn[l], *gdn_common)
        q_raw = proj[:, off_q : off_q + SWA_Q_W].reshape(Bs, SWA_KV_HEADS, group, 1, SWA_HD)
        q_exp = (q_raw * eye_kv[None, :, None, :, None]).reshape(Bs, SWA_HEADS, SWA_KV_W)
        ck = cache_swa_k[l].reshape(Bs, WINDOW, SWA_KV_W)
        cv = cache_swa_v[l].reshape(Bs, WINDOW, SWA_KV_W)
        r, nk, nv = _swa_step(q_exp, proj3, off_kv, ck, cv, cos_s, sin_s, swa_sinks[l].reshape(SWA_HEADS, 1))
        r5 = r.reshape(Bs, SWA_KV_HEADS, group, SWA_KV_HEADS, SWA_HD)
        kvh = jnp.arange(SWA_KV_HEADS)
        os_ = jnp.transpose(r5[:, kvh, :, kvh, :], (1, 0, 2, 3)).reshape(Bs, SWA_Q_W).astype(bf16)
        x2, qm = _merge(og.reshape(Bs, GDN_W), os_, proj, OFF_GG, x1, *merge_w)
        q8 = jnp.tile(qm.reshape(Bs, MEM_HEADS, MEM_HD), (1, 2, 1))
        om = _mem_attn_step(q8, cache_mem_k[l].reshape(Bs, n_mem * MEM_HEADS, MEM_HD),
                            cache_mem_v[l].reshape(Bs, n_mem * MEM_HEADS, MEM_HD))
        x3 = _proj_residual(om[:, :MEM_HEADS].reshape(Bs, MEM_W), wmo, x2)
        hs = _ffn(x3, *ffn2, gfin, last)
        outs[6].append(s_new)
        outs[7].append(jnp.concatenate([state_conv[l][:, 1:], proj3[:, :, :CONV_CH]], axis=1))
        outs[8].append(nk.reshape(Bs, WINDOW, SWA_KV_HEADS, SWA_HD))
        outs[9].append(nv.reshape(Bs, WINDOW, SWA_KV_HEADS, SWA_HD))

    return (hp.reshape(Bp, Tp, D), hs.reshape(Bs, Ts, D), *(jnp.stack(o) for o in outs))
```

```python
import functools
import math

import jax
import jax.numpy as jnp
from jax import lax
from jax.experimental import pallas as pl
from jax.experimental.pallas import tpu as pltpu

f32 = jnp.float32
bf16 = jnp.bfloat16

PAST_LEN = 16384
GDN_HEADS = 8
GDN_D = 128
GDN_CONV = 4
SWA_HEADS = 16
SWA_KV_HEADS = 4
SWA_HD = 64
WINDOW = 128
ROPE_THETA = 10000.0
MEM_HEADS = 4
MEM_HD = 128
EPS = 1e-6
L2_EPS = 1e-6

LANES = 128
SUBLANES = 8
CHUNK = 128
GDN_STEP_CHUNKS = 2
FFN_SLAB = 64
PROJ_TN = 768
VMEM_LIMIT = 58 * 1024 * 1024

GDN_W = GDN_HEADS * GDN_D
CONV_CH = 3 * GDN_W
SWA_Q_W = SWA_HEADS * SWA_HD
SWA_KV_W = SWA_KV_HEADS * SWA_HD
MEM_W = MEM_HEADS * MEM_HD

OFF_Z = CONV_CH
OFF_GG = OFF_Z + GDN_W


def _cparams(semantics):
    return pltpu.CompilerParams(dimension_semantics=semantics, vmem_limit_bytes=VMEM_LIMIT)


def _dot(a, b):
    return jnp.dot(a, b, preferred_element_type=f32)


def _dot_nt(a, b):
    return lax.dot_general(a, b, (((1,), (1,)), ((), ())), preferred_element_type=f32)


def _rmsnorm(x, g):
    return x * lax.rsqrt(jnp.mean(x * x, -1, keepdims=True) + EPS) * g


def _silu(x):
    return x * jax.nn.sigmoid(x)


def _softplus(x):
    return jnp.maximum(x, 0.0) + jnp.log1p(jnp.exp(-jnp.abs(x)))


def _ffn_kernel(x_ref, g_ref, w1_ref, w3_ref, w2_ref, gf_ref, o_ref, h_ref, *, final_norm):
    j = pl.program_id(1)
    slab = min(FFN_SLAB, x_ref.shape[0])
    n_slabs = x_ref.shape[0] // slab

    def over_slabs(body):
        def step(s, carry):
            body(pl.ds(pl.multiple_of(s * slab, slab), slab))
            return carry
        lax.fori_loop(0, n_slabs, step, 0, unroll=min(4, n_slabs))

    @pl.when(j == 0)
    def _():
        def prologue(rows):
            h_ref[rows, :] = _rmsnorm(x_ref[rows, :], g_ref[...]).astype(bf16)
            o_ref[rows, :] = jnp.zeros((slab, o_ref.shape[1]), f32)
        over_slabs(prologue)

    h = h_ref[...]
    a = _dot(h, w1_ref[...])
    b = _dot(h, w3_ref[...])
    o_ref[...] += _dot((_silu(a) * b).astype(bf16), w2_ref[...])

    @pl.when(j == pl.num_programs(1) - 1)
    def _():
        def epilogue(rows):
            y = x_ref[rows, :] + 0.5 * o_ref[rows, :]
            if final_norm:
                y = _rmsnorm(y, gf_ref[...])
            o_ref[rows, :] = y
        over_slabs(epilogue)


def _ffn(x, g, w1, w3, w2, gf, final_norm):
    M, D = x.shape
    F = w1.shape[1]
    tm = min(1024, M)
    tf = 512
    assert M % tm == 0 and F % tf == 0
    return pl.pallas_call(
        functools.partial(_ffn_kernel, final_norm=final_norm),
        grid=(M // tm, F // tf),
        in_specs=[
            pl.BlockSpec((tm, D), lambda i, j: (i, 0)),
            pl.BlockSpec((1, D), lambda i, j: (0, 0)),
            pl.BlockSpec((D, tf), lambda i, j: (0, j)),
            pl.BlockSpec((D, tf), lambda i, j: (0, j)),
            pl.BlockSpec((tf, D), lambda i, j: (j, 0)),
            pl.BlockSpec((1, D), lambda i, j: (0, 0)),
        ],
        out_specs=pl.BlockSpec((tm, D), lambda i, j: (i, 0)),
        out_shape=jax.ShapeDtypeStruct((M, D), f32),
        scratch_shapes=[pltpu.VMEM((tm, D), bf16)],
        compiler_params=_cparams(("parallel", "arbitrary")),
        name="ffn",
    )(x, g, w1, w3, w2, gf)


def _norm_matmul_kernel(x_ref, g_ref, w_ref, o_ref, h_ref):
    @pl.when(pl.program_id(1) == 0)
    def _():
        h_ref[...] = _rmsnorm(x_ref[...], g_ref[...]).astype(bf16)

    o_ref[...] = _dot(h_ref[...], w_ref[...])


def _norm_matmul(x, g, w, tn):
    M, D = x.shape
    N = w.shape[1]
    tm = min(1024, M)
    assert M % tm == 0 and N % tn == 0
    return pl.pallas_call(
        _norm_matmul_kernel,
        grid=(M // tm, N // tn),
        in_specs=[
            pl.BlockSpec((tm, D), lambda i, j: (i, 0)),
            pl.BlockSpec((1, D), lambda i, j: (0, 0)),
            pl.BlockSpec((D, tn), lambda i, j: (0, j)),
        ],
        out_specs=pl.BlockSpec((tm, tn), lambda i, j: (i, j)),
        out_shape=jax.ShapeDtypeStruct((M, N), f32),
        scratch_shapes=[pltpu.VMEM((tm, D), bf16)],
        compiler_params=_cparams(("parallel", "arbitrary")),
        name="norm_matmul",
    )(x, g, w)


def _bdot(a, b):
    return _dot(a.astype(bf16), b.astype(bf16))


def _unit_lower_inverse(nmats, row, col):
    eye = jnp.where(row == col, 1.0, 0.0).astype(f32)

    def same_block(size):
        return (row // size) == (col // size)

    blk = same_block(SUBLANES)
    a1 = [jnp.where(blk, n, 0.0) for n in nmats]
    a2 = [_bdot(a, a) for a in a1]
    a4 = [_bdot(a, a) for a in a2]
    ts = [_bdot(eye - x1, eye + x2) for x1, x2 in zip(a1, a2)]
    ts = [_bdot(t, eye + x4) for t, x4 in zip(ts, a4)]
    size = SUBLANES
    while size < CHUNK:
        nxt = same_block(2 * size)
        sel = jnp.logical_and(nxt, jnp.logical_not(blk))
        tbs = [t.astype(bf16) for t in ts]
        xs = [_dot(jnp.where(sel, n, 0.0).astype(bf16), tb) for n, tb in zip(nmats, tbs)]
        ts = [t - _dot(tb, x.astype(bf16)) for t, tb, x in zip(ts, tbs, xs)]
        blk = nxt
        size *= 2
    return ts


def _conv_silu_slab(x_ref, top_ref, w_ref, sl):
    rows = x_ref.shape[0]
    y = None
    for tap in range(GDN_CONV):
        back = GDN_CONV - 1 - tap
        window = jnp.concatenate(
            [top_ref[SUBLANES - back : 2 * SUBLANES - back, sl], x_ref[SUBLANES - back : rows - back, sl]], axis=0)
        term = window * w_ref[tap : tap + 1, sl]
        y = term if y is None else y + term
    return _silu(y)


def _delta_rule_prepare(pairs, gcum, gcum_t, beta_t, row, col, q_ref, k_ref, v_ref):
    n = range(len(pairs))
    incl = row >= col
    strict = row > col
    rows_of = lambda ci: slice(ci * CHUNK, (ci + 1) * CHUNK)
    gc = [gcum[rows_of(ci), GDN_HEADS + h : GDN_HEADS + h + 1] for ci, h in pairs]
    gr = [gcum_t[ci][GDN_HEADS + h : GDN_HEADS + h + 1, :] for ci, h in pairs]
    beta = [beta_t[rows_of(ci), h : h + 1] for ci, h in pairs]
    k = [k_ref[h, rows_of(ci), :] for ci, h in pairs]
    kb = [x.astype(bf16) for x in k]
    kk = [_dot_nt(kb[i], kb[i]) for i in n]
    qk = [_dot_nt(q_ref[h, rows_of(ci), :].astype(bf16), kb[i]) for i, (ci, h) in enumerate(pairs)]
    e = [jnp.exp(jnp.where(incl, gc[i] - gr[i], 0.0)) for i in n]
    nmat = [beta[i] * kk[i] * jnp.where(strict, e[i], 0.0) for i in n]
    qkd = [(qk[i] * jnp.where(incl, e[i], 0.0)).astype(bf16) for i in n]
    tinv = _unit_lower_inverse(nmat, row, col)
    gamma = [jnp.exp(gc[i]) for i in n]
    rhs = [jnp.concatenate([beta[i] * v_ref[h, rows_of(ci), :], (beta[i] * gamma[i]) * k[i]], axis=-1).astype(bf16)
           for i, (ci, h) in enumerate(pairs)]
    sol = [_dot(tinv[i].astype(bf16), rhs[i]) for i in n]
    return gc, gamma, sol, qkd


def _delta_rule_apply(ci, prepared, q_ref, k_ref, z_ref, gn_ref, s_ref, og_ref):
    gc, gamma, sol, qkd = prepared
    heads = range(GDN_HEADS)
    rows = slice(ci * CHUNK, (ci + 1) * CHUNK)
    sb = [s_ref[h].astype(bf16) for h in heads]
    ub = [(sol[h][:, :GDN_D] - _dot(sol[h][:, GDN_D:].astype(bf16), sb[h])).astype(bf16) for h in heads]
    o = [_dot((q_ref[h, rows, :] * gamma[h]).astype(bf16), sb[h]) + _dot(qkd[h], ub[h]) for h in heads]
    for h in heads:
        g_last = gc[h][CHUNK - 1 : CHUNK, :]
        k_end = k_ref[h, rows, :] * jnp.exp(g_last - gc[h])
        s_ref[h] = jnp.exp(g_last) * s_ref[h] + _dot(k_end.T.astype(bf16), ub[h])
        zh = z_ref[rows, h * GDN_D : (h + 1) * GDN_D]
        og_ref[rows, h * GDN_D : (h + 1) * GDN_D] = (_rmsnorm(o[h], gn_ref[...]) * _silu(zh)).astype(bf16)


def _gdn_prompt_kernel(qkv_ref, z_ref, ba_ref, cw_ref, av_ref, dv_ref, gn_ref,
                       og_ref, so_ref, s_ref, carry_ref, q_ref, k_ref, v_ref):
    c = pl.program_id(1)
    rows = qkv_ref.shape[0]
    n_chunks = rows // CHUNK

    @pl.when(c == 0)
    def _():
        s_ref[...] = jnp.zeros_like(s_ref)
        carry_ref[0:SUBLANES, :] = jnp.zeros((SUBLANES, CONV_CH), f32)

    carry_ref[SUBLANES : 2 * SUBLANES, :] = qkv_ref[0:SUBLANES, :]
    for j in range(3 * GDN_HEADS):
        sl = slice(j * LANES, (j + 1) * LANES)
        y = _conv_silu_slab(qkv_ref, carry_ref, cw_ref, sl)
        h = j % GDN_HEADS
        if j < 2 * GDN_HEADS:
            y = y * lax.rsqrt(jnp.sum(y * y, -1, keepdims=True) + L2_EPS)
            if j < GDN_HEADS:
                q_ref[h] = y * (GDN_D ** -0.5)
            else:
                k_ref[h] = y
        else:
            v_ref[h] = y
    carry_ref[0:SUBLANES, :] = qkv_ref[rows - SUBLANES : rows, :]

    ba = ba_ref[...]
    beta_t = jax.nn.sigmoid(ba)
    g_t = -jnp.exp(av_ref[...]) * _softplus(ba + dv_ref[...])
    row_in_chunk = lax.broadcasted_iota(jnp.int32, (rows, LANES), 0) % CHUNK
    gcum = g_t
    shift = 1
    while shift < CHUNK:
        gcum = gcum + jnp.where(row_in_chunk >= shift, pltpu.roll(gcum, shift, 0), 0.0)
        shift *= 2
    gcum_t = [gcum[ci * CHUNK : (ci + 1) * CHUNK, :].T for ci in range(n_chunks)]

    row = lax.broadcasted_iota(jnp.int32, (CHUNK, CHUNK), 0)
    col = lax.broadcasted_iota(jnp.int32, (CHUNK, CHUNK), 1)
    pairs = [(ci, h) for ci in range(n_chunks) for h in range(GDN_HEADS)]
    gc, gamma, sol, qkd = _delta_rule_prepare(pairs, gcum, gcum_t, beta_t, row, col, q_ref, k_ref, v_ref)
    for ci in range(n_chunks):
        mine = slice(ci * GDN_HEADS, (ci + 1) * GDN_HEADS)
        _delta_rule_apply(ci, (gc[mine], gamma[mine], sol[mine], qkd[mine]), q_ref, k_ref, z_ref, gn_ref, s_ref, og_ref)

    @pl.when(c == pl.num_programs(1) - 1)
    def _():
        so_ref[0] = s_ref[...]


def _gdn_prompt(proj, B, T, off_ba, cw, avec, dvec, gn):
    rows = GDN_STEP_CHUNKS * CHUNK
    assert T % rows == 0
    ns = T // rows
    head_scratch = pltpu.VMEM((GDN_HEADS, rows, GDN_D), f32)
    return pl.pallas_call(
        _gdn_prompt_kernel,
        grid=(B, ns),
        in_specs=[
            pl.BlockSpec((rows, CONV_CH), lambda b, c: (b * ns + c, 0)),
            pl.BlockSpec((rows, GDN_W), lambda b, c: (b * ns + c, OFF_Z // GDN_W)),
            pl.BlockSpec((rows, LANES), lambda b, c: (b * ns + c, off_ba // LANES)),
            pl.BlockSpec((GDN_CONV, CONV_CH), lambda b, c: (0, 0)),
            pl.BlockSpec((1, LANES), lambda b, c: (0, 0)),
            pl.BlockSpec((1, LANES), lambda b, c: (0, 0)),
            pl.BlockSpec((1, GDN_D), lambda b, c: (0, 0)),
        ],
        out_specs=[
            pl.BlockSpec((rows, GDN_W), lambda b, c: (b * ns + c, 0)),
            pl.BlockSpec((1, GDN_HEADS, GDN_D, GDN_D), lambda b, c: (b, 0, 0, 0)),
        ],
        out_shape=[
            jax.ShapeDtypeStruct((B * T, GDN_W), bf16),
            jax.ShapeDtypeStruct((B, GDN_HEADS, GDN_D, GDN_D), f32),
        ],
        scratch_shapes=[
            pltpu.VMEM((GDN_HEADS, GDN_D, GDN_D), f32),
            pltpu.VMEM((2 * SUBLANES, CONV_CH), f32),
            head_scratch, head_scratch, head_scratch,
        ],
        compiler_params=_cparams(("parallel", "arbitrary")),
        name="gdn_prompt",
    )(proj, proj, proj, cw, avec, dvec, gn)


def _gdn_step_kernel(qkv_ref, z_ref, ba_ref, sc_ref, s0_ref, cw_ref, av_ref, dv_ref, gn_ref, og_ref, so_ref):
    x_new = qkv_ref[0]
    taps = cw_ref[...]
    y = jnp.sum(sc_ref[0] * taps[0 : GDN_CONV - 1, :], axis=0, keepdims=True) + x_new * taps[GDN_CONV - 1 : GDN_CONV, :]
    y = _silu(y)
    ba = ba_ref[0]
    beta_t = jax.nn.sigmoid(ba)
    gamma_t = jnp.exp(-jnp.exp(av_ref[...]) * _softplus(ba + dv_ref[...]))
    z = z_ref[0]
    heads = range(GDN_HEADS)
    head = lambda base, h: y[:, base + h * GDN_D : base + (h + 1) * GDN_D]
    q = [head(0, h) for h in heads]
    k = [head(GDN_W, h) for h in heads]
    v = [head(2 * GDN_W, h) for h in heads]
    q = [q[h] * lax.rsqrt(jnp.sum(q[h] * q[h], -1, keepdims=True) + L2_EPS) * (GDN_D ** -0.5) for h in heads]
    k = [k[h] * lax.rsqrt(jnp.sum(k[h] * k[h], -1, keepdims=True) + L2_EPS) for h in heads]
    beta = [beta_t[:, h : h + 1] for h in heads]
    gamma = [gamma_t[:, GDN_HEADS + h : GDN_HEADS + h + 1] for h in heads]
    k_col = [jnp.broadcast_to(k[h], (GDN_D, GDN_D)).T for h in heads]
    q_col = [jnp.broadcast_to(q[h], (GDN_D, GDN_D)).T for h in heads]
    k_s = [jnp.sum(k_col[h] * s0_ref[0, h], axis=0, keepdims=True) for h in heads]
    q_s = [jnp.sum(q_col[h] * s0_ref[0, h], axis=0, keepdims=True) for h in heads]
    u = [beta[h] * v[h] - (beta[h] * gamma[h]) * k_s[h] for h in heads]
    o = [gamma[h] * q_s[h] + jnp.sum(q[h] * k[h], -1, keepdims=True) * u[h] for h in heads]
    for h in heads:
        so_ref[0, h] = gamma[h] * s0_ref[0, h] + k_col[h] * u[h]
        zh = z[:, h * GDN_D : (h + 1) * GDN_D]
        og_ref[0, :, h * GDN_D : (h + 1) * GDN_D] = (_rmsnorm(o[h], gn_ref[...]) * _silu(zh)).astype(bf16)


def _gdn_step(proj3, off_ba, state_conv, state_gdn, cw, avec, dvec, gn):
    B = proj3.shape[0]
    return pl.pallas_call(
        _gdn_step_kernel,
        grid=(B,),
        in_specs=[
            pl.BlockSpec((1, 1, CONV_CH), lambda b: (b, 0, 0)),
            pl.BlockSpec((1, 1, GDN_W), lambda b: (b, 0, OFF_Z // GDN_W)),
            pl.BlockSpec((1, 1, LANES), lambda b: (b, 0, off_ba // LANES)),
            pl.BlockSpec((1, GDN_CONV - 1, CONV_CH), lambda b: (b, 0, 0)),
            pl.BlockSpec((1, GDN_HEADS, GDN_D, GDN_D), lambda b: (b, 0, 0, 0)),
            pl.BlockSpec((GDN_CONV, CONV_CH), lambda b: (0, 0)),
            pl.BlockSpec((1, LANES), lambda b: (0, 0)),
            pl.BlockSpec((1, LANES), lambda b: (0, 0)),
            pl.BlockSpec((1, GDN_D), lambda b: (0, 0)),
        ],
        out_specs=[
            pl.BlockSpec((1, 1, GDN_W), lambda b: (b, 0, 0)),
            pl.BlockSpec((1, GDN_HEADS, GDN_D, GDN_D), lambda b: (b, 0, 0, 0)),
        ],
        out_shape=[
            jax.ShapeDtypeStruct((B, 1, GDN_W), bf16),
            jax.ShapeDtypeStruct((B, GDN_HEADS, GDN_D, GDN_D), f32),
        ],
        compiler_params=_cparams(("parallel",)),
        name="gdn_step",
    )(proj3, proj3, proj3, state_conv, state_gdn, cw, avec, dvec, gn)


def _rope(x, cos, sin_signed):
    width = x.shape[-1]
    lane = lax.broadcasted_iota(jnp.int32, x.shape, x.ndim - 1)
    first_half = (lane % SWA_HD) < (SWA_HD // 2)
    rot = jnp.where(first_half, pltpu.roll(x, width - SWA_HD // 2, x.ndim - 1), pltpu.roll(x, SWA_HD // 2, x.ndim - 1))
    return x * cos + rot * sin_signed


def _head_halves(x2, head_parity, lane):
    swapped = pltpu.roll(x2, SWA_HD, 1)
    lo_src, hi_src = (x2, swapped) if head_parity == 0 else (swapped, x2)
    return jnp.where(lane < SWA_HD, lo_src, 0.0), jnp.where(lane >= SWA_HD, hi_src, 0.0)


def _swa_prompt_kernel(sinks_ref, q_ref, kv_ref, cos_ref, sin_ref, os_ref, kc_ref, kprev_ref, vprev_ref):
    n = pl.program_id(1)

    @pl.when(n == 0)
    def _():
        kprev_ref[...] = jnp.zeros_like(kprev_ref)
        vprev_ref[...] = jnp.zeros_like(vprev_ref)

    cos = cos_ref[...]
    sin = sin_ref[...]
    kv = kv_ref[...]
    k_cur = _rope(kv[:, :SWA_KV_W], cos, sin)
    v_cur = kv[:, SWA_KV_W:]
    kc_ref[0] = k_cur
    k_all = jnp.concatenate([kprev_ref[...], k_cur], axis=0)
    v_all = jnp.concatenate([vprev_ref[...], v_cur], axis=0)
    kprev_ref[...] = k_cur
    vprev_ref[...] = v_cur

    row = lax.broadcasted_iota(jnp.int32, (WINDOW, 2 * WINDOW), 0)
    col = lax.broadcasted_iota(jnp.int32, (WINDOW, 2 * WINDOW), 1)
    prev_visible = jnp.logical_and(jnp.logical_and(col > row, col < WINDOW), n > 0)
    own_visible = jnp.logical_and(col >= WINDOW, (col - WINDOW) <= row)
    mask = jnp.logical_or(prev_visible, own_visible)
    lane = lax.broadcasted_iota(jnp.int32, (2 * WINDOW, LANES), 1)
    scale = SWA_HD ** -0.5
    group = SWA_HEADS // SWA_KV_HEADS
    k_half, v_half, q2 = [], [], []
    for h in range(SWA_KV_HEADS):
        pair = slice((h // 2) * LANES, (h // 2 + 1) * LANES)
        k_half.append([a.astype(bf16) for a in _head_halves(k_all[:, pair], h % 2, lane)])
        v_half.append([a.astype(bf16) for a in _head_halves(v_all[:, pair], h % 2, lane)])
        q_h = _rope(q_ref[:, h * group * SWA_HD : (h + 1) * group * SWA_HD], cos, sin) * scale
        q2.append([q_h[:, j * LANES : (j + 1) * LANES].astype(bf16) for j in range(group // 2)])
    heads = [(h, j, par) for h in range(SWA_KV_HEADS) for j in range(group // 2) for par in range(2)]
    sink = [sinks_ref[h * group + 2 * j + par] for h, j, par in heads]
    s = [jnp.where(mask, _dot_nt(q2[h][j], k_half[h][par]), -jnp.inf) for h, j, par in heads]
    m = [jnp.maximum(jnp.max(s[i], -1, keepdims=True), sink[i]) for i in range(len(heads))]
    p = [jnp.exp(s[i] - m[i]) for i in range(len(heads))]
    inv = [1.0 / (jnp.sum(p[i], -1, keepdims=True) + jnp.exp(sink[i] - m[i])) for i in range(len(heads))]
    o = [_dot(p[i].astype(bf16), v_half[h][par]) * inv[i] for i, (h, j, par) in enumerate(heads)]
    for i in range(0, len(heads), 2):
        os_ref[:, i * SWA_HD : i * SWA_HD + LANES] = (o[i] + o[i + 1]).astype(bf16)


def _swa_prompt(proj, B, T, off_q, off_kv, cos, sin, sinks):
    assert T % WINDOW == 0 and SWA_KV_W == 2 * LANES
    nb = T // WINDOW
    return pl.pallas_call(
        _swa_prompt_kernel,
        grid=(B, nb),
        in_specs=[
            pl.BlockSpec(memory_space=pltpu.SMEM),
            pl.BlockSpec((WINDOW, SWA_Q_W), lambda b, n: (b * nb + n, off_q // SWA_Q_W)),
            pl.BlockSpec((WINDOW, 2 * SWA_KV_W), lambda b, n: (b * nb + n, off_kv // (2 * SWA_KV_W))),
            pl.BlockSpec((WINDOW, SWA_KV_W), lambda b, n: (n, 0)),
            pl.BlockSpec((WINDOW, SWA_KV_W), lambda b, n: (n, 0)),
        ],
        out_specs=[
            pl.BlockSpec((WINDOW, SWA_Q_W), lambda b, n: (b * nb + n, 0)),
            pl.BlockSpec((1, WINDOW, SWA_KV_W), lambda b, n: (b, 0, 0)),
        ],
        out_shape=[
            jax.ShapeDtypeStruct((B * T, SWA_Q_W), bf16),
            jax.ShapeDtypeStruct((B, WINDOW, SWA_KV_W), f32),
        ],
        scratch_shapes=[pltpu.VMEM((WINDOW, SWA_KV_W), f32), pltpu.VMEM((WINDOW, SWA_KV_W), f32)],
        compiler_params=_cparams(("parallel", "arbitrary")),
        name="swa_prompt",
    )(sinks, proj, proj, cos, sin)


SWA_STEP_BATCH = 8


def _swa_step_kernel(qe_ref, kv_ref, ck_ref, cv_ref, cos_ref, sin_ref, sinks_ref, r_ref, nk_ref, nv_ref):
    cos = cos_ref[...]
    sin = sin_ref[...]
    sink = sinks_ref[...]
    row = lax.broadcasted_iota(jnp.int32, (WINDOW, SWA_KV_W), 0)
    scale = SWA_HD ** -0.5
    seqs = range(qe_ref.shape[0])
    kv = [kv_ref[i] for i in seqs]
    k_new = [_rope(kv[i][:, :SWA_KV_W], cos, sin) for i in seqs]
    keys = [jnp.where(row == WINDOW - 1, k_new[i], pltpu.roll(ck_ref[i], WINDOW - 1, 0)) for i in seqs]
    vals = [jnp.where(row == WINDOW - 1, kv[i][:, SWA_KV_W:], pltpu.roll(cv_ref[i], WINDOW - 1, 0)) for i in seqs]
    for i in seqs:
        nk_ref[i] = keys[i]
        nv_ref[i] = vals[i]
    q = [_rope(qe_ref[i], cos, sin) for i in seqs]
    s = [_dot_nt(q[i].astype(bf16), keys[i].astype(bf16)) * scale for i in seqs]
    m = [jnp.maximum(jnp.max(s[i], -1, keepdims=True), sink) for i in seqs]
    p = [jnp.exp(s[i] - m[i]) for i in seqs]
    denom = [jnp.sum(p[i], -1, keepdims=True) + jnp.exp(sink - m[i]) for i in seqs]
    for i in seqs:
        r_ref[i] = _dot((p[i] / denom[i]).astype(bf16), vals[i].astype(bf16))


def _swa_step(q_exp, proj3, off_kv, cache_k, cache_v, cos, sin, sinks_col):
    B = q_exp.shape[0]
    bb = math.gcd(B, SWA_STEP_BATCH)
    assert cache_k.shape[1] == WINDOW
    return pl.pallas_call(
        _swa_step_kernel,
        grid=(B // bb,),
        in_specs=[
            pl.BlockSpec((bb, SWA_HEADS, SWA_KV_W), lambda i: (i, 0, 0)),
            pl.BlockSpec((bb, 1, 2 * SWA_KV_W), lambda i: (i, 0, off_kv // (2 * SWA_KV_W))),
            pl.BlockSpec((bb, WINDOW, SWA_KV_W), lambda i: (i, 0, 0)),
            pl.BlockSpec((bb, WINDOW, SWA_KV_W), lambda i: (i, 0, 0)),
            pl.BlockSpec((1, SWA_KV_W), lambda i: (0, 0)),
            pl.BlockSpec((1, SWA_KV_W), lambda i: (0, 0)),
            pl.BlockSpec((SWA_HEADS, 1), lambda i: (0, 0)),
        ],
        out_specs=[
            pl.BlockSpec((bb, SWA_HEADS, SWA_KV_W), lambda i: (i, 0, 0)),
            pl.BlockSpec((bb, WINDOW, SWA_KV_W), lambda i: (i, 0, 0)),
            pl.BlockSpec((bb, WINDOW, SWA_KV_W), lambda i: (i, 0, 0)),
        ],
        out_shape=[
            jax.ShapeDtypeStruct((B, SWA_HEADS, SWA_KV_W), f32),
            jax.ShapeDtypeStruct((B, WINDOW, SWA_KV_W), f32),
            jax.ShapeDtypeStruct((B, WINDOW, SWA_KV_W), f32),
        ],
        compiler_params=_cparams(("parallel",)),
        name="swa_step",
    )(q_exp, proj3, cache_k, cache_v, cos, sin, sinks_col)


def _merge_kernel(og_ref, os_ref, gg_ref, gs_ref, x_ref, wg_ref, ws_ref, wo_ref, gq_ref, wq_ref, xo_ref, qm_ref):
    p_gdn = _dot(og_ref[...], wg_ref[...])
    p_swa = _dot(os_ref[...], ws_ref[...])
    merged = jax.nn.sigmoid(gg_ref[...]) * p_gdn + jax.nn.sigmoid(gs_ref[...]) * p_swa
    x_new = x_ref[...] + _dot(merged.astype(bf16), wo_ref[...])
    xo_ref[...] = x_new
    qm_ref[...] = _dot(_rmsnorm(x_new, gq_ref[...]).astype(bf16), wq_ref[...])


def _merge(og, os_, proj, off_gg, x, wg, ws, wo, gq, wq):
    M, D = x.shape
    tm = min(256, M)
    assert M % tm == 0 and off_gg % D == 0
    const = lambda shape: pl.BlockSpec(shape, lambda i: (0, 0), pipeline_mode=pl.Buffered(1))
    return pl.pallas_call(
        _merge_kernel,
        grid=(M // tm,),
        in_specs=[
            pl.BlockSpec((tm, GDN_W), lambda i: (i, 0)),
            pl.BlockSpec((tm, SWA_Q_W), lambda i: (i, 0)),
            pl.BlockSpec((tm, D), lambda i: (i, off_gg // D)),
            pl.BlockSpec((tm, D), lambda i: (i, off_gg // D + 1)),
            pl.BlockSpec((tm, D), lambda i: (i, 0)),
            const((GDN_W, D)),
            const((SWA_Q_W, D)),
            const((D, D)),
            const((1, D)),
            const((D, MEM_W)),
        ],
        out_specs=[
            pl.BlockSpec((tm, D), lambda i: (i, 0)),
            pl.BlockSpec((tm, MEM_W), lambda i: (i, 0)),
        ],
        out_shape=[jax.ShapeDtypeStruct((M, D), f32), jax.ShapeDtypeStruct((M, MEM_W), f32)],
        compiler_params=_cparams(("parallel",)),
        name="merge",
    )(og, os_, proj, proj, x, wg, ws, wo, gq, wq)


def _mem_attn_kernel(q_ref, k_ref, v_ref, x_ref, wo_ref, o_ref):
    tq = q_ref.shape[1]
    q = q_ref[0]
    if tq < SUBLANES:
        q = jnp.broadcast_to(q[0:1], (SUBLANES, q.shape[-1]))
    q = q.astype(bf16)
    scale = MEM_HD ** -0.5
    outs = []
    for h in range(MEM_HEADS):
        sl = slice(h * MEM_HD, (h + 1) * MEM_HD)
        s = _dot_nt(q[:, sl], k_ref[0, :, sl].astype(bf16)) * scale
        m = jnp.max(s, -1, keepdims=True)
        p = jnp.exp(s - m)
        p = p / jnp.sum(p, -1, keepdims=True)
        outs.append(_dot(p.astype(bf16), v_ref[0, :, sl].astype(bf16)))
    o = _dot(jnp.concatenate(outs, axis=-1).astype(bf16), wo_ref[...])
    o_ref[0] = x_ref[0] + o[0:tq]


def _mem_attn(qm, mem_k, mem_v, x, wo):
    B, T, D = x.shape
    mt = mem_k.shape[1]
    tq = min(512, T)
    assert T % tq == 0
    return pl.pallas_call(
        _mem_attn_kernel,
        grid=(B, T // tq),
        in_specs=[
            pl.BlockSpec((1, tq, MEM_W), lambda b, t: (b, t, 0)),
            pl.BlockSpec((1, mt, MEM_W), lambda b, t: (b, 0, 0)),
            pl.BlockSpec((1, mt, MEM_W), lambda b, t: (b, 0, 0)),
            pl.BlockSpec((1, tq, D), lambda b, t: (b, t, 0)),
            pl.BlockSpec((MEM_W, D), lambda b, t: (0, 0)),
        ],
        out_specs=pl.BlockSpec((1, tq, D), lambda b, t: (b, t, 0)),
        out_shape=jax.ShapeDtypeStruct((B, T, D), f32),
        compiler_params=_cparams(("parallel", "parallel")),
        name="mem_attn",
    )(qm, mem_k, mem_v, x, wo)


MEM_STEP_BATCH = 4


def _mem_attn_step_kernel(q_ref, k_ref, v_ref, o_ref):
    mt2 = k_ref.shape[1] // SUBLANES
    for i in range(q_ref.shape[0]):
        q8 = q_ref[i] * (MEM_HD ** -0.5)
        s = jnp.sum(k_ref[i].reshape(mt2, SUBLANES, MEM_HD) * q8, axis=-1, keepdims=True)
        m = jnp.max(s, axis=0)
        m = jnp.maximum(m, pltpu.roll(m, MEM_HEADS, 0))
        p = jnp.exp(s - m)
        l = jnp.sum(p, axis=0)
        l = l + pltpu.roll(l, MEM_HEADS, 0)
        o = jnp.sum(p * v_ref[i].reshape(mt2, SUBLANES, MEM_HD), axis=0)
        o_ref[i] = (o + pltpu.roll(o, MEM_HEADS, 0)) / l


def _mem_attn_step(q8, mem_k, mem_v):
    B, rows, _ = mem_k.shape
    assert 2 * MEM_HEADS == SUBLANES and rows % SUBLANES == 0
    bb = math.gcd(B, MEM_STEP_BATCH)
    return pl.pallas_call(
        _mem_attn_step_kernel,
        grid=(B // bb,),
        in_specs=[
            pl.BlockSpec((bb, SUBLANES, MEM_HD), lambda i: (i, 0, 0)),
            pl.BlockSpec((bb, rows, MEM_HD), lambda i: (i, 0, 0)),
            pl.BlockSpec((bb, rows, MEM_HD), lambda i: (i, 0, 0)),
        ],
        out_specs=pl.BlockSpec((bb, SUBLANES, MEM_HD), lambda i: (i, 0, 0)),
        out_shape=jax.ShapeDtypeStruct((B, SUBLANES, MEM_HD), f32),
        compiler_params=_cparams(("parallel",)),
        name="mem_attn_step",
    )(q8, mem_k, mem_v)


def _proj_residual_kernel(a_ref, w_ref, x_ref, o_ref):
    o_ref[...] = x_ref[...] + _dot(a_ref[...].astype(bf16), w_ref[...])


def _proj_residual(a, w, x):
    M, D = x.shape
    return pl.pallas_call(
        _proj_residual_kernel,
        out_shape=jax.ShapeDtypeStruct((M, D), f32),
        compiler_params=pltpu.CompilerParams(vmem_limit_bytes=VMEM_LIMIT),
        name="proj_residual",
    )(a, w, x)


def _rope_tables(pos):
    half = SWA_HD // 2
    inv_freq = ROPE_THETA ** (-jnp.arange(half, dtype=f32) / half)
    ang = pos.astype(f32)[:, None] * inv_freq[None, :]
    cos = jnp.cos(ang)
    sin = jnp.sin(ang)
    reps = SWA_KV_W // SWA_HD
    return jnp.tile(jnp.concatenate([cos, cos], -1), (1, reps)), jnp.tile(jnp.concatenate([-sin, sin], -1), (1, reps))


def kernel(x_prompt, x_sample, state_gdn, state_conv, cache_swa_k, cache_swa_v, cache_mem_k, cache_mem_v, mem_prompt, norm_ffn1, ffn1_w1, ffn1_w3, ffn1_w2, norm_mix, w_in, conv_w, gdn_A_log, gdn_dt_bias, gdn_norm, swa_sinks, w_br_gdn, w_br_swa, w_out, norm_mem_q, norm_mem_kv, w_mem_q, w_mem_k, w_mem_v, w_mem_o, norm_ffn2, ffn2_w1, ffn2_w3, ffn2_w2, norm_final):
    Bp, Tp, D = x_prompt.shape
    Bs, Ts, _ = x_sample.shape
    assert Ts == 1
    depth = norm_ffn1.shape[0]
    n_mem = mem_prompt.shape[1]
    group = SWA_HEADS // SWA_KV_HEADS

    off_gs = OFF_GG + D
    off_q = off_gs + D
    off_kv = off_q + SWA_Q_W
    off_ba = off_kv + 2 * SWA_KV_W
    d_in_pad = -(-(off_ba + LANES) // PROJ_TN) * PROJ_TN
    o_b = CONV_CH + GDN_W
    o_q = o_b + 2 * GDN_HEADS
    o_gg = o_q + SWA_Q_W + 2 * SWA_KV_W

    cos_p, sin_p = _rope_tables(jnp.arange(Tp, dtype=jnp.int32))
    cos_s, sin_s = _rope_tables(PAST_LEN + jnp.arange(Ts, dtype=jnp.int32))
    eye_kv = jnp.eye(SWA_KV_HEADS, dtype=f32)
    row = lambda v: v.reshape(1, -1)

    hp = x_prompt.reshape(Bp * Tp, D)
    hs = x_sample.reshape(Bs, D)
    outs = [[] for _ in range(10)]
    for l in range(depth):
        wi = w_in[l]
        w_in_r = jnp.concatenate(
            [wi[:, :o_b].astype(bf16), wi[:, o_gg:].astype(bf16), wi[:, o_q:o_gg].astype(bf16), wi[:, o_b:o_q].astype(bf16),
             jnp.zeros((D, d_in_pad - off_ba - 2 * GDN_HEADS), bf16)], axis=1)
        ffn1 = (row(norm_ffn1[l]), ffn1_w1[l].astype(bf16), ffn1_w3[l].astype(bf16), ffn1_w2[l].astype(bf16))
        ffn2 = (row(norm_ffn2[l]), ffn2_w1[l].astype(bf16), ffn2_w3[l].astype(bf16), ffn2_w2[l].astype(bf16))
        last = l == depth - 1
        gfin = row(norm_final)
        avec = jnp.zeros((1, LANES), f32).at[0, GDN_HEADS : 2 * GDN_HEADS].set(gdn_A_log[l])
        dvec = jnp.zeros((1, LANES), f32).at[0, GDN_HEADS : 2 * GDN_HEADS].set(gdn_dt_bias[l])
        gdn_common = (conv_w[l], avec, dvec, row(gdn_norm[l]))
        merge_w = (w_br_gdn[l].astype(bf16), w_br_swa[l].astype(bf16), w_out[l].astype(bf16),
                   row(norm_mem_q[l]), w_mem_q[l].astype(bf16))
        wmo = w_mem_o[l].astype(bf16)
        tn_in = PROJ_TN

        x1 = _ffn(hp, *ffn1, gfin, False)
        proj = _norm_matmul(x1, row(norm_mix[l]), w_in_r, tn_in)
        og, s_new = _gdn_prompt(proj, Bp, Tp, off_ba, *gdn_common)
        os_, kc = _swa_prompt(proj, Bp, Tp, off_q, off_kv, cos_p, sin_p, swa_sinks[l])
        x2, qm = _merge(og, os_, proj, OFF_GG, x1, *merge_w)
        mem_x = mem_prompt.reshape(Bp * n_mem, D)
        mk = _norm_matmul(mem_x, row(norm_mem_kv[l]), w_mem_k[l].astype(bf16), MEM_W)
        mv = _norm_matmul(mem_x, row(norm_mem_kv[l]), w_mem_v[l].astype(bf16), MEM_W)
        x3 = _mem_attn(qm.reshape(Bp, Tp, MEM_W), mk.reshape(Bp, n_mem, MEM_W), mv.reshape(Bp, n_mem, MEM_W),
                       x2.reshape(Bp, Tp, D), wmo)
        hp = _ffn(x3.reshape(Bp * Tp, D), *ffn2, gfin, last)
        proj_b = proj.reshape(Bp, Tp, d_in_pad)
        outs[0].append(s_new)
        outs[1].append(proj_b[:, Tp - (GDN_CONV - 1) :, :CONV_CH])
        outs[2].append(kc.reshape(Bp, WINDOW, SWA_KV_HEADS, SWA_HD))
        outs[3].append(proj_b[:, Tp - WINDOW :, off_kv + SWA_KV_W : off_kv + 2 * SWA_KV_W].reshape(Bp, WINDOW, SWA_KV_HEADS, SWA_HD))
        outs[4].append(mk.reshape(Bp, n_mem, MEM_HEADS, MEM_HD))
        outs[5].append(mv.reshape(Bp, n_mem, MEM_HEADS, MEM_HD))

        x1 = _ffn(hs, *ffn1, gfin, False)
        proj = _norm_matmul(x1, row(norm_mix[l]), w_in_r, tn_in)
        proj3 = proj.reshape(Bs, 1, d_in_pad)
        og, s_new = _gdn_step(proj3, off_ba, state_conv[l], state_gdn[l], *gdn_common)
        q_raw = proj[:, off_q : off_q + SWA_Q_W].reshape(Bs, SWA_KV_HEADS, group, 1, SWA_HD)
        q_exp = (q_raw * eye_kv[None, :, None, :, None]).reshape(Bs, SWA_HEADS, SWA_KV_W)
        ck = cache_swa_k[l].reshape(Bs, WINDOW, SWA_KV_W)
        cv = cache_swa_v[l].reshape(Bs, WINDOW, SWA_KV_W)
        r, nk, nv = _swa_step(q_exp, proj3, off_kv, ck, cv, cos_s, sin_s, swa_sinks[l].reshape(SWA_HEADS, 1))
        r5 = r.reshape(Bs, SWA_KV_HEADS, group, SWA_KV_HEADS, SWA_HD)
        kvh = jnp.arange(SWA_KV_HEADS)
        os_ = jnp.transpose(r5[:, kvh, :, kvh, :], (1, 0, 2, 3)).reshape(Bs, SWA_Q_W).astype(bf16)
        x2, qm = _merge(og.reshape(Bs, GDN_W), os_, proj, OFF_GG, x1, *merge_w)
        q8 = jnp.tile(qm.reshape(Bs, MEM_HEADS, MEM_HD), (1, 2, 1))
        om = _mem_attn_step(q8, cache_mem_k[l].reshape(Bs, n_mem * MEM_HEADS, MEM_HD),
                            cache_mem_v[l].reshape(Bs, n_mem * MEM_HEADS, MEM_HD))
        x3 = _proj_residual(om[:, :MEM_HEADS].reshape(Bs, MEM_W), wmo, x2)
        hs = _ffn(x3, *ffn2, gfin, last)
        outs[6].append(s_new)
        outs[7].append(jnp.concatenate([state_conv[l][:, 1:], proj3[:, :, :CONV_CH]], axis=1))
        outs[8].append(nk.reshape(Bs, WINDOW, SWA_KV_HEADS, SWA_HD))
        outs[9].append(nv.reshape(Bs, WINDOW, SWA_KV_HEADS, SWA_HD))

    return (hp.reshape(Bp, Tp, D), hs.reshape(Bs, Ts, D), *(jnp.stack(o) for o in outs))
```

```python
import functools
import math

import jax
import jax.numpy as jnp
from jax import lax
from jax.experimental import pallas as pl
from jax.experimental.pallas import tpu as pltpu

f32 = jnp.float32
bf16 = jnp.bfloat16

PAST_LEN = 16384
GDN_HEADS = 8
GDN_D = 128
GDN_CONV = 4
SWA_HEADS = 16
SWA_KV_HEADS = 4
SWA_HD = 64
WINDOW = 128
ROPE_THETA = 10000.0
MEM_HEADS = 4
MEM_HD = 128
EPS = 1e-6
L2_EPS = 1e-6

LANES = 128
SUBLANES = 8
CHUNK = 128
GDN_STEP_CHUNKS = 2
FFN_SLAB = 64
PROJ_TN = 768
VMEM_LIMIT = 58 * 1024 * 1024

GDN_W = GDN_HEADS * GDN_D
CONV_CH = 3 * GDN_W
SWA_Q_W = SWA_HEADS * SWA_HD
SWA_KV_W = SWA_KV_HEADS * SWA_HD
MEM_W = MEM_HEADS * MEM_HD

OFF_Z = CONV_CH
OFF_GG = OFF_Z + GDN_W


def _cparams(semantics):
    return pltpu.CompilerParams(dimension_semantics=semantics, vmem_limit_bytes=VMEM_LIMIT)


def _dot(a, b):
    return jnp.dot(a, b, preferred_element_type=f32)


def _dot_nt(a, b):
    return lax.dot_general(a, b, (((1,), (1,)), ((), ())), preferred_element_type=f32)


def _rmsnorm(x, g):
    return x * lax.rsqrt(jnp.mean(x * x, -1, keepdims=True) + EPS) * g


def _silu(x):
    return x * jax.nn.sigmoid(x)


def _softplus(x):
    return jnp.maximum(x, 0.0) + jnp.log1p(jnp.exp(-jnp.abs(x)))


def _ffn_kernel(x_ref, g_ref, w1_ref, w3_ref, w2_ref, gf_ref, o_ref, h_ref, *, final_norm):
    j = pl.program_id(1)
    slab = min(FFN_SLAB, x_ref.shape[0])
    n_slabs = x_ref.shape[0] // slab

    def over_slabs(body):
        def step(s, carry):
            body(pl.ds(pl.multiple_of(s * slab, slab), slab))
            return carry
        lax.fori_loop(0, n_slabs, step, 0, unroll=min(4, n_slabs))

    @pl.when(j == 0)
    def _():
        def prologue(rows):
            h_ref[rows, :] = _rmsnorm(x_ref[rows, :], g_ref[...]).astype(bf16)
            o_ref[rows, :] = jnp.zeros((slab, o_ref.shape[1]), f32)
        over_slabs(prologue)

    h = h_ref[...]
    a = _dot(h, w1_ref[...])
    b = _dot(h, w3_ref[...])
    o_ref[...] += _dot((_silu(a) * b).astype(bf16), w2_ref[...])

    @pl.when(j == pl.num_programs(1) - 1)
    def _():
        def epilogue(rows):
            y = x_ref[rows, :] + 0.5 * o_ref[rows, :]
            if final_norm:
                y = _rmsnorm(y, gf_ref[...])
            o_ref[rows, :] = y
        over_slabs(epilogue)


def _ffn(x, g, w1, w3, w2, gf, final_norm):
    M, D = x.shape
    F = w1.shape[1]
    tm = min(1024, M)
    tf = 512
    assert M % tm == 0 and F % tf == 0
    return pl.pallas_call(
        functools.partial(_ffn_kernel, final_norm=final_norm),
        grid=(M // tm, F // tf),
        in_specs=[
            pl.BlockSpec((tm, D), lambda i, j: (i, 0)),
            pl.BlockSpec((1, D), lambda i, j: (0, 0)),
            pl.BlockSpec((D, tf), lambda i, j: (0, j)),
            pl.BlockSpec((D, tf), lambda i, j: (0, j)),
            pl.BlockSpec((tf, D), lambda i, j: (j, 0)),
            pl.BlockSpec((1, D), lambda i, j: (0, 0)),
        ],
        out_specs=pl.BlockSpec((tm, D), lambda i, j: (i, 0)),
        out_shape=jax.ShapeDtypeStruct((M, D), f32),
        scratch_shapes=[pltpu.VMEM((tm, D), bf16)],
        compiler_params=_cparams(("parallel", "arbitrary")),
        name="ffn",
    )(x, g, w1, w3, w2, gf)


def _norm_matmul_kernel(x_ref, g_ref, w_ref, o_ref, h_ref):
    @pl.when(pl.program_id(1) == 0)
    def _():
        h_ref[...] = _rmsnorm(x_ref[...], g_ref[...]).astype(bf16)

    o_ref[...] = _dot(h_ref[...], w_ref[...])


def _norm_matmul(x, g, w, tn):
    M, D = x.shape
    N = w.shape[1]
    tm = min(1024, M)
    assert M % tm == 0 and N % tn == 0
    return pl.pallas_call(
        _norm_matmul_kernel,
        grid=(M // tm, N // tn),
        in_specs=[
            pl.BlockSpec((tm, D), lambda i, j: (i, 0)),
            pl.BlockSpec((1, D), lambda i, j: (0, 0)),
            pl.BlockSpec((D, tn), lambda i, j: (0, j)),
        ],
        out_specs=pl.BlockSpec((tm, tn), lambda i, j: (i, j)),
        out_shape=jax.ShapeDtypeStruct((M, N), f32),
        scratch_shapes=[pltpu.VMEM((tm, D), bf16)],
        compiler_params=_cparams(("parallel", "arbitrary")),
        name="norm_matmul",
    )(x, g, w)


def _bdot(a, b):
    return _dot(a.astype(bf16), b.astype(bf16))


def _unit_lower_inverse(nmats, row, col):
    eye = jnp.where(row == col, 1.0, 0.0).astype(f32)

    def same_block(size):
        return (row // size) == (col // size)

    blk = same_block(SUBLANES)
    a1 = [jnp.where(blk, n, 0.0) for n in nmats]
    a2 = [_bdot(a, a) for a in a1]
    a4 = [_bdot(a, a) for a in a2]
    ts = [_bdot(eye - x1, eye + x2) for x1, x2 in zip(a1, a2)]
    ts = [_bdot(t, eye + x4) for t, x4 in zip(ts, a4)]
    size = SUBLANES
    while size < CHUNK:
        nxt = same_block(2 * size)
        sel = jnp.logical_and(nxt, jnp.logical_not(blk))
        tbs = [t.astype(bf16) for t in ts]
        xs = [_dot(jnp.where(sel, n, 0.0).astype(bf16), tb) for n, tb in zip(nmats, tbs)]
        ts = [t - _dot(tb, x.astype(bf16)) for t, tb, x in zip(ts, tbs, xs)]
        blk = nxt
        size *= 2
    return ts


def _conv_silu_slab(x_ref, top_ref, w_ref, sl):
    rows = x_ref.shape[0]
    y = None
    for tap in range(GDN_CONV):
        back = GDN_CONV - 1 - tap
        window = jnp.concatenate(
            [top_ref[SUBLANES - back : 2 * SUBLANES - back, sl], x_ref[SUBLANES - back : rows - back, sl]], axis=0)
        term = window * w_ref[tap : tap + 1, sl]
        y = term if y is None else y + term
    return _silu(y)


def _delta_rule_prepare(pairs, gcum, gcum_t, beta_t, row, col, q_ref, k_ref, v_ref):
    n = range(len(pairs))
    incl = row >= col
    strict = row > col
    rows_of = lambda ci: slice(ci * CHUNK, (ci + 1) * CHUNK)
    gc = [gcum[rows_of(ci), GDN_HEADS + h : GDN_HEADS + h + 1] for ci, h in pairs]
    gr = [gcum_t[ci][GDN_HEADS + h : GDN_HEADS + h + 1, :] for ci, h in pairs]
    beta = [beta_t[rows_of(ci), h : h + 1] for ci, h in pairs]
    k = [k_ref[h, rows_of(ci), :] for ci, h in pairs]
    kb = [x.astype(bf16) for x in k]
    kk = [_dot_nt(kb[i], kb[i]) for i in n]
    qk = [_dot_nt(q_ref[h, rows_of(ci), :].astype(bf16), kb[i]) for i, (ci, h) in enumerate(pairs)]
    e = [jnp.exp(jnp.where(incl, gc[i] - gr[i], 0.0)) for i in n]
    nmat = [beta[i] * kk[i] * jnp.where(strict, e[i], 0.0) for i in n]
    qkd = [(qk[i] * jnp.where(incl, e[i], 0.0)).astype(bf16) for i in n]
    tinv = _unit_lower_inverse(nmat, row, col)
    gamma = [jnp.exp(gc[i]) for i in n]
    rhs = [jnp.concatenate([beta[i] * v_ref[h, rows_of(ci), :], (beta[i] * gamma[i]) * k[i]], axis=-1).astype(bf16)
           for i, (ci, h) in enumerate(pairs)]
    sol = [_dot(tinv[i].astype(bf16), rhs[i]) for i in n]
    return gc, gamma, sol, qkd


def _delta_rule_apply(ci, prepared, q_ref, k_ref, z_ref, gn_ref, s_ref, og_ref):
    gc, gamma, sol, qkd = prepared
    heads = range(GDN_HEADS)
    rows = slice(ci * CHUNK, (ci + 1) * CHUNK)
    sb = [s_ref[h].astype(bf16) for h in heads]
    ub = [(sol[h][:, :GDN_D] - _dot(sol[h][:, GDN_D:].astype(bf16), sb[h])).astype(bf16) for h in heads]
    o = [_dot((q_ref[h, rows, :] * gamma[h]).astype(bf16), sb[h]) + _dot(qkd[h], ub[h]) for h in heads]
    for h in heads:
        g_last = gc[h][CHUNK - 1 : CHUNK, :]
        k_end = k_ref[h, rows, :] * jnp.exp(g_last - gc[h])
        s_ref[h] = jnp.exp(g_last) * s_ref[h] + _dot(k_end.T.astype(bf16), ub[h])
        zh = z_ref[rows, h * GDN_D : (h + 1) * GDN_D]
        og_ref[rows, h * GDN_D : (h + 1) * GDN_D] = (_rmsnorm(o[h], gn_ref[...]) * _silu(zh)).astype(bf16)


def _gdn_prompt_kernel(qkv_ref, z_ref, ba_ref, cw_ref, av_ref, dv_ref, gn_ref,
                       og_ref, so_ref, s_ref, carry_ref, q_ref, k_ref, v_ref):
    c = pl.program_id(1)
    rows = qkv_ref.shape[0]
    n_chunks = rows // CHUNK

    @pl.when(c == 0)
    def _():
        s_ref[...] = jnp.zeros_like(s_ref)
        carry_ref[0:SUBLANES, :] = jnp.zeros((SUBLANES, CONV_CH), f32)

    carry_ref[SUBLANES : 2 * SUBLANES, :] = qkv_ref[0:SUBLANES, :]
    for j in range(3 * GDN_HEADS):
        sl = slice(j * LANES, (j + 1) * LANES)
        y = _conv_silu_slab(qkv_ref, carry_ref, cw_ref, sl)
        h = j % GDN_HEADS
        if j < 2 * GDN_HEADS:
            y = y * lax.rsqrt(jnp.sum(y * y, -1, keepdims=True) + L2_EPS)
            if j < GDN_HEADS:
                q_ref[h] = y * (GDN_D ** -0.5)
            else:
                k_ref[h] = y
        else:
            v_ref[h] = y
    carry_ref[0:SUBLANES, :] = qkv_ref[rows - SUBLANES : rows, :]

    ba = ba_ref[...]
    beta_t = jax.nn.sigmoid(ba)
    g_t = -jnp.exp(av_ref[...]) * _softplus(ba + dv_ref[...])
    row_in_chunk = lax.broadcasted_iota(jnp.int32, (rows, LANES), 0) % CHUNK
    gcum = g_t
    shift = 1
    while shift < CHUNK:
        gcum = gcum + jnp.where(row_in_chunk >= shift, pltpu.roll(gcum, shift, 0), 0.0)
        shift *= 2
    gcum_t = [gcum[ci * CHUNK : (ci + 1) * CHUNK, :].T for ci in range(n_chunks)]

    row = lax.broadcasted_iota(jnp.int32, (CHUNK, CHUNK), 0)
    col = lax.broadcasted_iota(jnp.int32, (CHUNK, CHUNK), 1)
    pairs = [(ci, h) for ci in range(n_chunks) for h in range(GDN_HEADS)]
    gc, gamma, sol, qkd = _delta_rule_prepare(pairs, gcum, gcum_t, beta_t, row, col, q_ref, k_ref, v_ref)
    for ci in range(n_chunks):
        mine = slice(ci * GDN_HEADS, (ci + 1) * GDN_HEADS)
        _delta_rule_apply(ci, (gc[mine], gamma[mine], sol[mine], qkd[mine]), q_ref, k_ref, z_ref, gn_ref, s_ref, og_ref)

    @pl.when(c == pl.num_programs(1) - 1)
    def _():
        so_ref[0] = s_ref[...]


def _gdn_prompt(proj, B, T, off_ba, cw, avec, dvec, gn):
    rows = GDN_STEP_CHUNKS * CHUNK
    assert T % rows == 0
    ns = T // rows
    head_scratch = pltpu.VMEM((GDN_HEADS, rows, GDN_D), f32)
    return pl.pallas_call(
        _gdn_prompt_kernel,
        grid=(B, ns),
        in_specs=[
            pl.BlockSpec((rows, CONV_CH), lambda b, c: (b * ns + c, 0)),
            pl.BlockSpec((rows, GDN_W), lambda b, c: (b * ns + c, OFF_Z // GDN_W)),
            pl.BlockSpec((rows, LANES), lambda b, c: (b * ns + c, off_ba // LANES)),
            pl.BlockSpec((GDN_CONV, CONV_CH), lambda b, c: (0, 0)),
            pl.BlockSpec((1, LANES), lambda b, c: (0, 0)),
            pl.BlockSpec((1, LANES), lambda b, c: (0, 0)),
            pl.BlockSpec((1, GDN_D), lambda b, c: (0, 0)),
        ],
        out_specs=[
            pl.BlockSpec((rows, GDN_W), lambda b, c: (b * ns + c, 0)),
            pl.BlockSpec((1, GDN_HEADS, GDN_D, GDN_D), lambda b, c: (b, 0, 0, 0)),
        ],
        out_shape=[
            jax.ShapeDtypeStruct((B * T, GDN_W), bf16),
            jax.ShapeDtypeStruct((B, GDN_HEADS, GDN_D, GDN_D), f32),
        ],
        scratch_shapes=[
            pltpu.VMEM((GDN_HEADS, GDN_D, GDN_D), f32),
            pltpu.VMEM((2 * SUBLANES, CONV_CH), f32),
            head_scratch, head_scratch, head_scratch,
        ],
        compiler_params=_cparams(("parallel", "arbitrary")),
        name="gdn_prompt",
    )(proj, proj, proj, cw, avec, dvec, gn)


def _gdn_step_kernel(qkv_ref, z_ref, ba_ref, sc_ref, s0_ref, cw_ref, av_ref, dv_ref, gn_ref, og_ref, so_ref):
    x_new = qkv_ref[0]
    taps = cw_ref[...]
    y = jnp.sum(sc_ref[0] * taps[0 : GDN_CONV - 1, :], axis=0, keepdims=True) + x_new * taps[GDN_CONV - 1 : GDN_CONV, :]
    y = _silu(y)
    ba = ba_ref[0]
    beta_t = jax.nn.sigmoid(ba)
    gamma_t = jnp.exp(-jnp.exp(av_ref[...]) * _softplus(ba + dv_ref[...]))
    z = z_ref[0]
    heads = range(GDN_HEADS)
    head = lambda base, h: y[:, base + h * GDN_D : base + (h + 1) * GDN_D]
    q = [head(0, h) for h in heads]
    k = [head(GDN_W, h) for h in heads]
    v = [head(2 * GDN_W, h) for h in heads]
    q = [q[h] * lax.rsqrt(jnp.sum(q[h] * q[h], -1, keepdims=True) + L2_EPS) * (GDN_D ** -0.5) for h in heads]
    k = [k[h] * lax.rsqrt(jnp.sum(k[h] * k[h], -1, keepdims=True) + L2_EPS) for h in heads]
    beta = [beta_t[:, h : h + 1] for h in heads]
    gamma = [gamma_t[:, GDN_HEADS + h : GDN_HEADS + h + 1] for h in heads]
    k_col = [jnp.broadcast_to(k[h], (GDN_D, GDN_D)).T for h in heads]
    q_col = [jnp.broadcast_to(q[h], (GDN_D, GDN_D)).T for h in heads]
    k_s = [jnp.sum(k_col[h] * s0_ref[0, h], axis=0, keepdims=True) for h in heads]
    q_s = [jnp.sum(q_col[h] * s0_ref[0, h], axis=0, keepdims=True) for h in heads]
    u = [beta[h] * v[h] - (beta[h] * gamma[h]) * k_s[h] for h in heads]
    o = [gamma[h] * q_s[h] + jnp.sum(q[h] * k[h], -1, keepdims=True) * u[h] for h in heads]
    for h in heads:
        so_ref[0, h] = gamma[h] * s0_ref[0, h] + k_col[h] * u[h]
        zh = z[:, h * GDN_D : (h + 1) * GDN_D]
        og_ref[0, :, h * GDN_D : (h + 1) * GDN_D] = (_rmsnorm(o[h], gn_ref[...]) * _silu(zh)).astype(bf16)


def _gdn_step(proj3, off_ba, state_conv, state_gdn, cw, avec, dvec, gn):
    B = proj3.shape[0]
    return pl.pallas_call(
        _gdn_step_kernel,
        grid=(B,),
        in_specs=[
            pl.BlockSpec((1, 1, CONV_CH), lambda b: (b, 0, 0)),
            pl.BlockSpec((1, 1, GDN_W), lambda b: (b, 0, OFF_Z // GDN_W)),
            pl.BlockSpec((1, 1, LANES), lambda b: (b, 0, off_ba // LANES)),
            pl.BlockSpec((1, GDN_CONV - 1, CONV_CH), lambda b: (b, 0, 0)),
            pl.BlockSpec((1, GDN_HEADS, GDN_D, GDN_D), lambda b: (b, 0, 0, 0)),
            pl.BlockSpec((GDN_CONV, CONV_CH), lambda b: (0, 0)),
            pl.BlockSpec((1, LANES), lambda b: (0, 0)),
            pl.BlockSpec((1, LANES), lambda b: (0, 0)),
            pl.BlockSpec((1, GDN_D), lambda b: (0, 0)),
        ],
        out_specs=[
            pl.BlockSpec((1, 1, GDN_W), lambda b: (b, 0, 0)),
            pl.BlockSpec((1, GDN_HEADS, GDN_D, GDN_D), lambda b: (b, 0, 0, 0)),
        ],
        out_shape=[
            jax.ShapeDtypeStruct((B, 1, GDN_W), bf16),
            jax.ShapeDtypeStruct((B, GDN_HEADS, GDN_D, GDN_D), f32),
        ],
        compiler_params=_cparams(("parallel",)),
        name="gdn_step",
    )(proj3, proj3, proj3, state_conv, state_gdn, cw, avec, dvec, gn)


def _rope(x, cos, sin_signed):
    width = x.shape[-1]
    lane = lax.broadcasted_iota(jnp.int32, x.shape, x.ndim - 1)
    first_half = (lane % SWA_HD) < (SWA_HD // 2)
    rot = jnp.where(first_half, pltpu.roll(x, width - SWA_HD // 2, x.ndim - 1), pltpu.roll(x, SWA_HD // 2, x.ndim - 1))
    return x * cos + rot * sin_signed


def _head_halves(x2, head_parity, lane):
    swapped = pltpu.roll(x2, SWA_HD, 1)
    lo_src, hi_src = (x2, swapped) if head_parity == 0 else (swapped, x2)
    return jnp.where(lane < SWA_HD, lo_src, 0.0), jnp.where(lane >= SWA_HD, hi_src, 0.0)


def _swa_prompt_kernel(sinks_ref, q_ref, kv_ref, cos_ref, sin_ref, os_ref, kc_ref, kprev_ref, vprev_ref):
    n = pl.program_id(1)

    @pl.when(n == 0)
    def _():
        kprev_ref[...] = jnp.zeros_like(kprev_ref)
        vprev_ref[...] = jnp.zeros_like(vprev_ref)

    cos = cos_ref[...]
    sin = sin_ref[...]
    kv = kv_ref[...]
    k_cur = _rope(kv[:, :SWA_KV_W], cos, sin)
    v_cur = kv[:, SWA_KV_W:]
    kc_ref[0] = k_cur
    k_prev = kprev_ref[...]
    v_prev = vprev_ref[...]
    kprev_ref[...] = k_cur
    vprev_ref[...] = v_cur

    row = lax.broadcasted_iota(jnp.int32, (WINDOW, WINDOW), 0)
    col = lax.broadcasted_iota(jnp.int32, (WINDOW, WINDOW), 1)
    own = col <= row
    prev_bias = jnp.where(n > 0, 0.0, -jnp.inf)
    lane = lax.broadcasted_iota(jnp.int32, (WINDOW, LANES), 1)
    scale = SWA_HD ** -0.5
    group = SWA_HEADS // SWA_KV_HEADS
    k_own, k_pre, v_own, v_pre, q2 = [], [], [], [], []
    for h in range(SWA_KV_HEADS):
        pair = slice((h // 2) * LANES, (h // 2 + 1) * LANES)
        k_own.append([a.astype(bf16) for a in _head_halves(k_cur[:, pair], h % 2, lane)])
        k_pre.append([a.astype(bf16) for a in _head_halves(k_prev[:, pair], h % 2, lane)])
        v_own.append([a.astype(bf16) for a in _head_halves(v_cur[:, pair], h % 2, lane)])
        v_pre.append([a.astype(bf16) for a in _head_halves(v_prev[:, pair], h % 2, lane)])
        q_h = _rope(q_ref[:, h * group * SWA_HD : (h + 1) * group * SWA_HD], cos, sin) * scale
        q2.append([q_h[:, j * LANES : (j + 1) * LANES].astype(bf16) for j in range(group // 2)])
    heads = [(h, j, par) for h in range(SWA_KV_HEADS) for j in range(group // 2) for par in range(2)]
    idx = range(len(heads))
    sink = [sinks_ref[h * group + 2 * j + par] for h, j, par in heads]
    s = [jnp.where(own, _dot_nt(q2[h][j], k_own[h][par]), _dot_nt(q2[h][j], k_pre[h][par]) + prev_bias)
         for h, j, par in heads]
    m = [jnp.maximum(jnp.max(s[i], -1, keepdims=True), sink[i]) for i in idx]
    p = [jnp.exp(s[i] - m[i]) for i in idx]
    inv = [1.0 / (jnp.sum(p[i], -1, keepdims=True) + jnp.exp(sink[i] - m[i])) for i in idx]
    o = [(_dot(jnp.where(own, p[i], 0.0).astype(bf16), v_own[h][par])
          + _dot(jnp.where(own, 0.0, p[i]).astype(bf16), v_pre[h][par])) * inv[i] for i, (h, j, par) in enumerate(heads)]
    for i in range(0, len(heads), 2):
        os_ref[:, i * SWA_HD : i * SWA_HD + LANES] = (o[i] + o[i + 1]).astype(bf16)


def _swa_prompt(proj, B, T, off_q, off_kv, cos, sin, sinks):
    assert T % WINDOW == 0 and SWA_KV_W == 2 * LANES
    nb = T // WINDOW
    return pl.pallas_call(
        _swa_prompt_kernel,
        grid=(B, nb),
        in_specs=[
            pl.BlockSpec(memory_space=pltpu.SMEM),
            pl.BlockSpec((WINDOW, SWA_Q_W), lambda b, n: (b * nb + n, off_q // SWA_Q_W)),
            pl.BlockSpec((WINDOW, 2 * SWA_KV_W), lambda b, n: (b * nb + n, off_kv // (2 * SWA_KV_W))),
            pl.BlockSpec((WINDOW, SWA_KV_W), lambda b, n: (n, 0)),
            pl.BlockSpec((WINDOW, SWA_KV_W), lambda b, n: (n, 0)),
        ],
        out_specs=[
            pl.BlockSpec((WINDOW, SWA_Q_W), lambda b, n: (b * nb + n, 0)),
            pl.BlockSpec((1, WINDOW, SWA_KV_W), lambda b, n: (b, 0, 0)),
        ],
        out_shape=[
            jax.ShapeDtypeStruct((B * T, SWA_Q_W), bf16),
            jax.ShapeDtypeStruct((B, WINDOW, SWA_KV_W), f32),
        ],
        scratch_shapes=[pltpu.VMEM((WINDOW, SWA_KV_W), f32), pltpu.VMEM((WINDOW, SWA_KV_W), f32)],
        compiler_params=_cparams(("parallel", "arbitrary")),
        name="swa_prompt",
    )(sinks, proj, proj, cos, sin)


SWA_STEP_BATCH = 8


def _swa_step_kernel(qe_ref, kv_ref, ck_ref, cv_ref, cos_ref, sin_ref, sinks_ref, r_ref, nk_ref, nv_ref):
    cos = cos_ref[...]
    sin = sin_ref[...]
    sink = sinks_ref[...]
    row = lax.broadcasted_iota(jnp.int32, (WINDOW, SWA_KV_W), 0)
    scale = SWA_HD ** -0.5
    seqs = range(qe_ref.shape[0])
    kv = [kv_ref[i] for i in seqs]
    k_new = [_rope(kv[i][:, :SWA_KV_W], cos, sin) for i in seqs]
    keys = [jnp.where(row == WINDOW - 1, k_new[i], pltpu.roll(ck_ref[i], WINDOW - 1, 0)) for i in seqs]
    vals = [jnp.where(row == WINDOW - 1, kv[i][:, SWA_KV_W:], pltpu.roll(cv_ref[i], WINDOW - 1, 0)) for i in seqs]
    for i in seqs:
        nk_ref[i] = keys[i]
        nv_ref[i] = vals[i]
    q = [_rope(qe_ref[i], cos, sin) for i in seqs]
    s = [_dot_nt(q[i].astype(bf16), keys[i].astype(bf16)) * scale for i in seqs]
    m = [jnp.maximum(jnp.max(s[i], -1, keepdims=True), sink) for i in seqs]
    p = [jnp.exp(s[i] - m[i]) for i in seqs]
    denom = [jnp.sum(p[i], -1, keepdims=True) + jnp.exp(sink - m[i]) for i in seqs]
    for i in seqs:
        r_ref[i] = _dot((p[i] / denom[i]).astype(bf16), vals[i].astype(bf16))


def _swa_step(q_exp, proj3, off_kv, cache_k, cache_v, cos, sin, sinks_col):
    B = q_exp.shape[0]
    bb = math.gcd(B, SWA_STEP_BATCH)
    assert cache_k.shape[1] == WINDOW
    return pl.pallas_call(
        _swa_step_kernel,
        grid=(B // bb,),
        in_specs=[
            pl.BlockSpec((bb, SWA_HEADS, SWA_KV_W), lambda i: (i, 0, 0)),
            pl.BlockSpec((bb, 1, 2 * SWA_KV_W), lambda i: (i, 0, off_kv // (2 * SWA_KV_W))),
            pl.BlockSpec((bb, WINDOW, SWA_KV_W), lambda i: (i, 0, 0)),
            pl.BlockSpec((bb, WINDOW, SWA_KV_W), lambda i: (i, 0, 0)),
            pl.BlockSpec((1, SWA_KV_W), lambda i: (0, 0)),
            pl.BlockSpec((1, SWA_KV_W), lambda i: (0, 0)),
            pl.BlockSpec((SWA_HEADS, 1), lambda i: (0, 0)),
        ],
        out_specs=[
            pl.BlockSpec((bb, SWA_HEADS, SWA_KV_W), lambda i: (i, 0, 0)),
            pl.BlockSpec((bb, WINDOW, SWA_KV_W), lambda i: (i, 0, 0)),
            pl.BlockSpec((bb, WINDOW, SWA_KV_W), lambda i: (i, 0, 0)),
        ],
        out_shape=[
            jax.ShapeDtypeStruct((B, SWA_HEADS, SWA_KV_W), f32),
            jax.ShapeDtypeStruct((B, WINDOW, SWA_KV_W), f32),
            jax.ShapeDtypeStruct((B, WINDOW, SWA_KV_W), f32),
        ],
        compiler_params=_cparams(("parallel",)),
        name="swa_step",
    )(q_exp, proj3, cache_k, cache_v, cos, sin, sinks_col)


def _merge_core(og_ref, os_ref, gg_ref, gs_ref, x_ref, wg_ref, ws_ref, wo_ref, gq_ref, wq_ref):
    p_gdn = _dot(og_ref[...], wg_ref[...])
    p_swa = _dot(os_ref[...], ws_ref[...])
    merged = jax.nn.sigmoid(gg_ref[...]) * p_gdn + jax.nn.sigmoid(gs_ref[...]) * p_swa
    x_new = x_ref[...] + _dot(merged.astype(bf16), wo_ref[...])
    return x_new, _dot(_rmsnorm(x_new, gq_ref[...]).astype(bf16), wq_ref[...])


def _merge_kernel(*refs):
    xo_ref, qm_ref = refs[-2:]
    xo_ref[...], qm_ref[...] = _merge_core(*refs[:-2])


def _merge_mem_kernel(*refs):
    mk_ref, mv_ref, wmo_ref, xo_ref = refs[-4:]
    x_new, q = _merge_core(*refs[:-4])
    q = q.astype(bf16)
    scale = MEM_HD ** -0.5
    heads = range(MEM_HEADS)
    cols = lambda h: slice(h * MEM_HD, (h + 1) * MEM_HD)
    s = [_dot_nt(q[:, cols(h)], mk_ref[0, :, cols(h)]) * scale for h in heads]
    p = [jnp.exp(s[h] - jnp.max(s[h], -1, keepdims=True)) for h in heads]
    p = [p[h] / jnp.sum(p[h], -1, keepdims=True) for h in heads]
    o = [_dot(p[h].astype(bf16), mv_ref[0, :, cols(h)]) for h in heads]
    xo_ref[...] = x_new + _dot(jnp.concatenate(o, axis=-1).astype(bf16), wmo_ref[...])


def _merge(og, os_, proj, off_gg, x, wg, ws, wo, gq, wq, mem=None):
    M, D = x.shape
    tm = min(256, M)
    assert M % tm == 0 and off_gg % D == 0
    const = lambda shape: pl.BlockSpec(shape, lambda i: (0, 0), pipeline_mode=pl.Buffered(1))
    in_specs = [
        pl.BlockSpec((tm, GDN_W), lambda i: (i, 0)),
        pl.BlockSpec((tm, SWA_Q_W), lambda i: (i, 0)),
        pl.BlockSpec((tm, D), lambda i: (i, off_gg // D)),
        pl.BlockSpec((tm, D), lambda i: (i, off_gg // D + 1)),
        pl.BlockSpec((tm, D), lambda i: (i, 0)),
        const((GDN_W, D)),
        const((SWA_Q_W, D)),
        const((D, D)),
        const((1, D)),
        const((D, MEM_W)),
    ]
    args = [og, os_, proj, proj, x, wg, ws, wo, gq, wq]
    x_spec = pl.BlockSpec((tm, D), lambda i: (i, 0))
    x_shape = jax.ShapeDtypeStruct((M, D), f32)
    if mem is None:
        return pl.pallas_call(
            _merge_kernel,
            grid=(M // tm,),
            in_specs=in_specs,
            out_specs=[x_spec, pl.BlockSpec((tm, MEM_W), lambda i: (i, 0))],
            out_shape=[x_shape, jax.ShapeDtypeStruct((M, MEM_W), f32)],
            compiler_params=_cparams(("parallel",)),
            name="merge",
        )(*args)
    mem_k, mem_v, wmo, seq_rows = mem
    assert seq_rows % tm == 0
    mt = mem_k.shape[1]
    mem_spec = pl.BlockSpec((1, mt, MEM_W), lambda i: (i // (seq_rows // tm), 0, 0))
    return pl.pallas_call(
        _merge_mem_kernel,
        grid=(M // tm,),
        in_specs=in_specs + [mem_spec, mem_spec, const((MEM_W, D))],
        out_specs=x_spec,
        out_shape=x_shape,
        compiler_params=_cparams(("parallel",)),
        name="merge_mem",
    )(*args, mem_k, mem_v, wmo)


MEM_STEP_BATCH = 4


def _mem_attn_step_kernel(q_ref, k_ref, v_ref, o_ref):
    mt2 = k_ref.shape[1] // SUBLANES
    for i in range(q_ref.shape[0]):
        q8 = q_ref[i] * (MEM_HD ** -0.5)
        s = jnp.sum(k_ref[i].reshape(mt2, SUBLANES, MEM_HD) * q8, axis=-1, keepdims=True)
        m = jnp.max(s, axis=0)
        m = jnp.maximum(m, pltpu.roll(m, MEM_HEADS, 0))
        p = jnp.exp(s - m)
        l = jnp.sum(p, axis=0)
        l = l + pltpu.roll(l, MEM_HEADS, 0)
        o = jnp.sum(p * v_ref[i].reshape(mt2, SUBLANES, MEM_HD), axis=0)
        o_ref[i] = (o + pltpu.roll(o, MEM_HEADS, 0)) / l


def _mem_attn_step(q8, mem_k, mem_v):
    B, rows, _ = mem_k.shape
    assert 2 * MEM_HEADS == SUBLANES and rows % SUBLANES == 0
    bb = math.gcd(B, MEM_STEP_BATCH)
    return pl.pallas_call(
        _mem_attn_step_kernel,
        grid=(B // bb,),
        in_specs=[
            pl.BlockSpec((bb, SUBLANES, MEM_HD), lambda i: (i, 0, 0)),
            pl.BlockSpec((bb, rows, MEM_HD), lambda i: (i, 0, 0)),
            pl.BlockSpec((bb, rows, MEM_HD), lambda i: (i, 0, 0)),
        ],
        out_specs=pl.BlockSpec((bb, SUBLANES, MEM_HD), lambda i: (i, 0, 0)),
        out_shape=jax.ShapeDtypeStruct((B, SUBLANES, MEM_HD), f32),
        compiler_params=_cparams(("parallel",)),
        name="mem_attn_step",
    )(q8, mem_k, mem_v)


def _proj_residual_kernel(a_ref, w_ref, x_ref, o_ref):
    o_ref[...] = x_ref[...] + _dot(a_ref[...].astype(bf16), w_ref[...])


def _proj_residual(a, w, x):
    M, D = x.shape
    return pl.pallas_call(
        _proj_residual_kernel,
        out_shape=jax.ShapeDtypeStruct((M, D), f32),
        compiler_params=pltpu.CompilerParams(vmem_limit_bytes=VMEM_LIMIT),
        name="proj_residual",
    )(a, w, x)


def _rope_tables(pos):
    half = SWA_HD // 2
    inv_freq = ROPE_THETA ** (-jnp.arange(half, dtype=f32) / half)
    ang = pos.astype(f32)[:, None] * inv_freq[None, :]
    cos = jnp.cos(ang)
    sin = jnp.sin(ang)
    reps = SWA_KV_W // SWA_HD
    return jnp.tile(jnp.concatenate([cos, cos], -1), (1, reps)), jnp.tile(jnp.concatenate([-sin, sin], -1), (1, reps))


def kernel(x_prompt, x_sample, state_gdn, state_conv, cache_swa_k, cache_swa_v, cache_mem_k, cache_mem_v, mem_prompt, norm_ffn1, ffn1_w1, ffn1_w3, ffn1_w2, norm_mix, w_in, conv_w, gdn_A_log, gdn_dt_bias, gdn_norm, swa_sinks, w_br_gdn, w_br_swa, w_out, norm_mem_q, norm_mem_kv, w_mem_q, w_mem_k, w_mem_v, w_mem_o, norm_ffn2, ffn2_w1, ffn2_w3, ffn2_w2, norm_final):
    Bp, Tp, D = x_prompt.shape
    Bs, Ts, _ = x_sample.shape
    assert Ts == 1
    depth = norm_ffn1.shape[0]
    n_mem = mem_prompt.shape[1]
    group = SWA_HEADS // SWA_KV_HEADS

    off_gs = OFF_GG + D
    off_q = off_gs + D
    off_kv = off_q + SWA_Q_W
    off_ba = off_kv + 2 * SWA_KV_W
    d_in_pad = -(-(off_ba + LANES) // PROJ_TN) * PROJ_TN
    o_b = CONV_CH + GDN_W
    o_q = o_b + 2 * GDN_HEADS
    o_gg = o_q + SWA_Q_W + 2 * SWA_KV_W

    cos_p, sin_p = _rope_tables(jnp.arange(Tp, dtype=jnp.int32))
    cos_s, sin_s = _rope_tables(PAST_LEN + jnp.arange(Ts, dtype=jnp.int32))
    eye_kv = jnp.eye(SWA_KV_HEADS, dtype=f32)
    row = lambda v: v.reshape(1, -1)

    hp = x_prompt.reshape(Bp * Tp, D)
    hs = x_sample.reshape(Bs, D)
    outs = [[] for _ in range(10)]
    for l in range(depth):
        wi = w_in[l]
        w_in_r = jnp.concatenate(
            [wi[:, :o_b].astype(bf16), wi[:, o_gg:].astype(bf16), wi[:, o_q:o_gg].astype(bf16), wi[:, o_b:o_q].astype(bf16),
             jnp.zeros((D, d_in_pad - off_ba - 2 * GDN_HEADS), bf16)], axis=1)
        ffn1 = (row(norm_ffn1[l]), ffn1_w1[l].astype(bf16), ffn1_w3[l].astype(bf16), ffn1_w2[l].astype(bf16))
        ffn2 = (row(norm_ffn2[l]), ffn2_w1[l].astype(bf16), ffn2_w3[l].astype(bf16), ffn2_w2[l].astype(bf16))
        last = l == depth - 1
        gfin = row(norm_final)
        avec = jnp.zeros((1, LANES), f32).at[0, GDN_HEADS : 2 * GDN_HEADS].set(gdn_A_log[l])
        dvec = jnp.zeros((1, LANES), f32).at[0, GDN_HEADS : 2 * GDN_HEADS].set(gdn_dt_bias[l])
        gdn_common = (conv_w[l], avec, dvec, row(gdn_norm[l]))
        merge_w = (w_br_gdn[l].astype(bf16), w_br_swa[l].astype(bf16), w_out[l].astype(bf16),
                   row(norm_mem_q[l]), w_mem_q[l].astype(bf16))
        wmo = w_mem_o[l].astype(bf16)
        tn_in = PROJ_TN

        x1 = _ffn(hp, *ffn1, gfin, False)
        proj = _norm_matmul(x1, row(norm_mix[l]), w_in_r, tn_in)
        og, s_new = _gdn_prompt(proj, Bp, Tp, off_ba, *gdn_common)
        os_, kc = _swa_prompt(proj, Bp, Tp, off_q, off_kv, cos_p, sin_p, swa_sinks[l])
        mem_x = mem_prompt.reshape(Bp * n_mem, D)
        mk = _norm_matmul(mem_x, row(norm_mem_kv[l]), w_mem_k[l].astype(bf16), MEM_W)
        mv = _norm_matmul(mem_x, row(norm_mem_kv[l]), w_mem_v[l].astype(bf16), MEM_W)
        mem = (mk.reshape(Bp, n_mem, MEM_W).astype(bf16), mv.reshape(Bp, n_mem, MEM_W).astype(bf16), wmo, Tp)
        x3 = _merge(og, os_, proj, OFF_GG, x1, *merge_w, mem=mem)
        hp = _ffn(x3, *ffn2, gfin, last)
        proj_b = proj.reshape(Bp, Tp, d_in_pad)
        outs[0].append(s_new)
        outs[1].append(proj_b[:, Tp - (GDN_CONV - 1) :, :CONV_CH])
        outs[2].append(kc.reshape(Bp, WINDOW, SWA_KV_HEADS, SWA_HD))
        outs[3].append(proj_b[:, Tp - WINDOW :, off_kv + SWA_KV_W : off_kv + 2 * SWA_KV_W].reshape(Bp, WINDOW, SWA_KV_HEADS, SWA_HD))
        outs[4].append(mk.reshape(Bp, n_mem, MEM_HEADS, MEM_HD))
        outs[5].append(mv.reshape(Bp, n_mem, MEM_HEADS, MEM_HD))

        x1 = _ffn(hs, *ffn1, gfin, False)
        proj = _norm_matmul(x1, row(norm_mix[l]), w_in_r, tn_in)
        proj3 = proj.reshape(Bs, 1, d_in_pad)
        og, s_new = _gdn_step(proj3, off_ba, state_conv[l], state_gdn[l], *gdn_common)
        q_raw = proj[:, off_q : off_q + SWA_Q_W].reshape(Bs, SWA_KV_HEADS, group, 1, SWA_HD)
        q_exp = (q_raw * eye_kv[None, :, None, :, None]).reshape(Bs, SWA_HEADS, SWA_KV_W)
        ck = cache_swa_k[l].reshape(Bs, WINDOW, SWA_KV_W)
        cv = cache_swa_v[l].reshape(Bs, WINDOW, SWA_KV_W)
        r, nk, nv = _swa_step(q_exp, proj3, off_kv, ck, cv, cos_s, sin_s, swa_sinks[l].reshape(SWA_HEADS, 1))
        r5 = r.reshape(Bs, SWA_KV_HEADS, group, SWA_KV_HEADS, SWA_HD)
        kvh = jnp.arange(SWA_KV_HEADS)
        os_ = jnp.transpose(r5[:, kvh, :, kvh, :], (1, 0, 2, 3)).reshape(Bs, SWA_Q_W).astype(bf16)
        x2, qm = _merge(og.reshape(Bs, GDN_W), os_, proj, OFF_GG, x1, *merge_w)
        q8 = jnp.tile(qm.reshape(Bs, MEM_HEADS, MEM_HD), (1, 2, 1))
        om = _mem_attn_step(q8, cache_mem_k[l].reshape(Bs, n_mem * MEM_HEADS, MEM_HD),
                            cache_mem_v[l].reshape(Bs, n_mem * MEM_HEADS, MEM_HD))
        x3 = _proj_residual(om[:, :MEM_HEADS].reshape(Bs, MEM_W), wmo, x2)
        hs = _ffn(x3, *ffn2, gfin, last)
        outs[6].append(s_new)
        outs[7].append(jnp.concatenate([state_conv[l][:, 1:], proj3[:, :, :CONV_CH]], axis=1))
        outs[8].append(nk.reshape(Bs, WINDOW, SWA_KV_HEADS, SWA_HD))
        outs[9].append(nv.reshape(Bs, WINDOW, SWA_KV_HEADS, SWA_HD))

    return (hp.reshape(Bp, Tp, D), hs.reshape(Bs, Ts, D), *(jnp.stack(o) for o in outs))
```

```python
import functools
import math

import jax
import jax.numpy as jnp
from jax import lax
from jax.experimental import pallas as pl
from jax.experimental.pallas import tpu as pltpu

f32 = jnp.float32
bf16 = jnp.bfloat16

PAST_LEN = 16384
GDN_HEADS = 8
GDN_D = 128
GDN_CONV = 4
SWA_HEADS = 16
SWA_KV_HEADS = 4
SWA_HD = 64
WINDOW = 128
ROPE_THETA = 10000.0
MEM_HEADS = 4
MEM_HD = 128
EPS = 1e-6
L2_EPS = 1e-6

LANES = 128
SUBLANES = 8
CHUNK = 128
GDN_STEP_CHUNKS = 2
FFN_SLAB = 64
PROJ_TN = 768
VMEM_LIMIT = 58 * 1024 * 1024

GDN_W = GDN_HEADS * GDN_D
CONV_CH = 3 * GDN_W
SWA_Q_W = SWA_HEADS * SWA_HD
SWA_KV_W = SWA_KV_HEADS * SWA_HD
MEM_W = MEM_HEADS * MEM_HD

OFF_Z = CONV_CH
OFF_GG = OFF_Z + GDN_W


def _cparams(semantics):
    return pltpu.CompilerParams(dimension_semantics=semantics, vmem_limit_bytes=VMEM_LIMIT)


def _dot(a, b):
    return jnp.dot(a, b, preferred_element_type=f32)


def _dot_nt(a, b):
    return lax.dot_general(a, b, (((1,), (1,)), ((), ())), preferred_element_type=f32)


def _rmsnorm(x, g):
    return x * lax.rsqrt(jnp.mean(x * x, -1, keepdims=True) + EPS) * g


def _silu(x):
    return x * jax.nn.sigmoid(x)


def _softplus(x):
    return jnp.maximum(x, 0.0) + jnp.log1p(jnp.exp(-jnp.abs(x)))


def _ffn_kernel(x_ref, g_ref, w1_ref, w3_ref, w2_ref, gf_ref, o_ref, h_ref, *, final_norm):
    j = pl.program_id(1)
    slab = min(FFN_SLAB, x_ref.shape[0])
    n_slabs = x_ref.shape[0] // slab

    def over_slabs(body):
        def step(s, carry):
            body(pl.ds(pl.multiple_of(s * slab, slab), slab))
            return carry
        lax.fori_loop(0, n_slabs, step, 0, unroll=min(4, n_slabs))

    @pl.when(j == 0)
    def _():
        def prologue(rows):
            h_ref[rows, :] = _rmsnorm(x_ref[rows, :], g_ref[...]).astype(bf16)
            o_ref[rows, :] = jnp.zeros((slab, o_ref.shape[1]), f32)
        over_slabs(prologue)

    h = h_ref[...]
    a = _dot(h, w1_ref[...])
    b = _dot(h, w3_ref[...])
    o_ref[...] += _dot((_silu(a) * b).astype(bf16), w2_ref[...])

    @pl.when(j == pl.num_programs(1) - 1)
    def _():
        def epilogue(rows):
            y = x_ref[rows, :] + 0.5 * o_ref[rows, :]
            if final_norm:
                y = _rmsnorm(y, gf_ref[...])
            o_ref[rows, :] = y
        over_slabs(epilogue)


def _ffn(x, g, w1, w3, w2, gf, final_norm):
    M, D = x.shape
    F = w1.shape[1]
    tm = min(1024, M)
    tf = 512
    assert M % tm == 0 and F % tf == 0
    return pl.pallas_call(
        functools.partial(_ffn_kernel, final_norm=final_norm),
        grid=(M // tm, F // tf),
        in_specs=[
            pl.BlockSpec((tm, D), lambda i, j: (i, 0)),
            pl.BlockSpec((1, D), lambda i, j: (0, 0)),
            pl.BlockSpec((D, tf), lambda i, j: (0, j)),
            pl.BlockSpec((D, tf), lambda i, j: (0, j)),
            pl.BlockSpec((tf, D), lambda i, j: (j, 0)),
            pl.BlockSpec((1, D), lambda i, j: (0, 0)),
        ],
        out_specs=pl.BlockSpec((tm, D), lambda i, j: (i, 0)),
        out_shape=jax.ShapeDtypeStruct((M, D), f32),
        scratch_shapes=[pltpu.VMEM((tm, D), bf16)],
        compiler_params=_cparams(("parallel", "arbitrary")),
        name="ffn",
    )(x, g, w1, w3, w2, gf)


def _norm_matmul_kernel(x_ref, g_ref, w_ref, o_ref, h_ref):
    @pl.when(pl.program_id(1) == 0)
    def _():
        h_ref[...] = _rmsnorm(x_ref[...], g_ref[...]).astype(bf16)

    o_ref[...] = _dot(h_ref[...], w_ref[...])


def _norm_matmul(x, g, w, tn):
    M, D = x.shape
    N = w.shape[1]
    tm = min(1024, M)
    assert M % tm == 0 and N % tn == 0
    return pl.pallas_call(
        _norm_matmul_kernel,
        grid=(M // tm, N // tn),
        in_specs=[
            pl.BlockSpec((tm, D), lambda i, j: (i, 0)),
            pl.BlockSpec((1, D), lambda i, j: (0, 0)),
            pl.BlockSpec((D, tn), lambda i, j: (0, j)),
        ],
        out_specs=pl.BlockSpec((tm, tn), lambda i, j: (i, j)),
        out_shape=jax.ShapeDtypeStruct((M, N), f32),
        scratch_shapes=[pltpu.VMEM((tm, D), bf16)],
        compiler_params=_cparams(("parallel", "arbitrary")),
        name="norm_matmul",
    )(x, g, w)


def _bdot(a, b):
    return _dot(a.astype(bf16), b.astype(bf16))


def _unit_lower_inverse(nmats, row, col):
    eye = jnp.where(row == col, 1.0, 0.0).astype(f32)

    def same_block(size):
        return (row // size) == (col // size)

    blk = same_block(SUBLANES)
    a1 = [jnp.where(blk, n, 0.0) for n in nmats]
    a2 = [_bdot(a, a) for a in a1]
    a4 = [_bdot(a, a) for a in a2]
    ts = [_bdot(eye - x1, eye + x2) for x1, x2 in zip(a1, a2)]
    ts = [_bdot(t, eye + x4) for t, x4 in zip(ts, a4)]
    size = SUBLANES
    while size < CHUNK:
        nxt = same_block(2 * size)
        sel = jnp.logical_and(nxt, jnp.logical_not(blk))
        tbs = [t.astype(bf16) for t in ts]
        xs = [_dot(jnp.where(sel, n, 0.0).astype(bf16), tb) for n, tb in zip(nmats, tbs)]
        ts = [t - _dot(tb, x.astype(bf16)) for t, tb, x in zip(ts, tbs, xs)]
        blk = nxt
        size *= 2
    return ts


def _conv_silu_slab(x_ref, top_ref, w_ref, sl):
    rows = x_ref.shape[0]
    y = None
    for tap in range(GDN_CONV):
        back = GDN_CONV - 1 - tap
        window = jnp.concatenate(
            [top_ref[SUBLANES - back : 2 * SUBLANES - back, sl], x_ref[SUBLANES - back : rows - back, sl]], axis=0)
        term = window * w_ref[tap : tap + 1, sl]
        y = term if y is None else y + term
    return _silu(y)


def _delta_rule_prepare(pairs, gcum, gcum_t, beta_t, row, col, q_ref, k_ref, v_ref):
    n = range(len(pairs))
    incl = row >= col
    strict = row > col
    rows_of = lambda ci: slice(ci * CHUNK, (ci + 1) * CHUNK)
    gc = [gcum[rows_of(ci), GDN_HEADS + h : GDN_HEADS + h + 1] for ci, h in pairs]
    gr = [gcum_t[ci][GDN_HEADS + h : GDN_HEADS + h + 1, :] for ci, h in pairs]
    beta = [beta_t[rows_of(ci), h : h + 1] for ci, h in pairs]
    k = [k_ref[h, rows_of(ci), :] for ci, h in pairs]
    kb = [x.astype(bf16) for x in k]
    kk = [_dot_nt(kb[i], kb[i]) for i in n]
    qk = [_dot_nt(q_ref[h, rows_of(ci), :].astype(bf16), kb[i]) for i, (ci, h) in enumerate(pairs)]
    e = [jnp.exp(jnp.where(incl, gc[i] - gr[i], 0.0)) for i in n]
    nmat = [beta[i] * kk[i] * jnp.where(strict, e[i], 0.0) for i in n]
    qkd = [(qk[i] * jnp.where(incl, e[i], 0.0)).astype(bf16) for i in n]
    tinv = _unit_lower_inverse(nmat, row, col)
    gamma = [jnp.exp(gc[i]) for i in n]
    rhs = [jnp.concatenate([beta[i] * v_ref[h, rows_of(ci), :], (beta[i] * gamma[i]) * k[i]], axis=-1).astype(bf16)
           for i, (ci, h) in enumerate(pairs)]
    sol = [_dot(tinv[i].astype(bf16), rhs[i]) for i in n]
    return gc, gamma, sol, qkd


def _delta_rule_apply(ci, prepared, q_ref, k_ref, z_ref, gn_ref, s_ref, og_ref):
    gc, gamma, sol, qkd = prepared
    heads = range(GDN_HEADS)
    rows = slice(ci * CHUNK, (ci + 1) * CHUNK)
    sb = [s_ref[h].astype(bf16) for h in heads]
    ub = [(sol[h][:, :GDN_D] - _dot(sol[h][:, GDN_D:].astype(bf16), sb[h])).astype(bf16) for h in heads]
    o = [_dot((q_ref[h, rows, :] * gamma[h]).astype(bf16), sb[h]) + _dot(qkd[h], ub[h]) for h in heads]
    for h in heads:
        g_last = gc[h][CHUNK - 1 : CHUNK, :]
        k_end = k_ref[h, rows, :] * jnp.exp(g_last - gc[h])
        s_ref[h] = jnp.exp(g_last) * s_ref[h] + _dot(k_end.T.astype(bf16), ub[h])
        zh = z_ref[rows, h * GDN_D : (h + 1) * GDN_D]
        og_ref[rows, h * GDN_D : (h + 1) * GDN_D] = (_rmsnorm(o[h], gn_ref[...]) * _silu(zh)).astype(bf16)


def _gdn_prompt_kernel(*refs, n_cast):
    qkv_ref, z_ref, ba_ref, cw_ref, av_ref, dv_ref, gn_ref = refs[:7]
    cast_in = refs[7 : 7 + n_cast]
    og_ref, so_ref = refs[7 + n_cast : 9 + n_cast]
    cast_out = refs[9 + n_cast : 9 + 2 * n_cast]
    s_ref, carry_ref, q_ref, k_ref, v_ref = refs[9 + 2 * n_cast :]
    for src, dst in zip(cast_in, cast_out):
        dst[...] = src[...].astype(bf16)

    c = pl.program_id(1)
    rows = qkv_ref.shape[0]
    n_chunks = rows // CHUNK

    @pl.when(c == 0)
    def _():
        s_ref[...] = jnp.zeros_like(s_ref)
        carry_ref[0:SUBLANES, :] = jnp.zeros((SUBLANES, CONV_CH), f32)

    carry_ref[SUBLANES : 2 * SUBLANES, :] = qkv_ref[0:SUBLANES, :]
    for j in range(3 * GDN_HEADS):
        sl = slice(j * LANES, (j + 1) * LANES)
        y = _conv_silu_slab(qkv_ref, carry_ref, cw_ref, sl)
        h = j % GDN_HEADS
        if j < 2 * GDN_HEADS:
            y = y * lax.rsqrt(jnp.sum(y * y, -1, keepdims=True) + L2_EPS)
            if j < GDN_HEADS:
                q_ref[h] = y * (GDN_D ** -0.5)
            else:
                k_ref[h] = y
        else:
            v_ref[h] = y
    carry_ref[0:SUBLANES, :] = qkv_ref[rows - SUBLANES : rows, :]

    ba = ba_ref[...]
    beta_t = jax.nn.sigmoid(ba)
    g_t = -jnp.exp(av_ref[...]) * _softplus(ba + dv_ref[...])
    row_in_chunk = lax.broadcasted_iota(jnp.int32, (rows, LANES), 0) % CHUNK
    gcum = g_t
    shift = 1
    while shift < CHUNK:
        gcum = gcum + jnp.where(row_in_chunk >= shift, pltpu.roll(gcum, shift, 0), 0.0)
        shift *= 2
    gcum_t = [gcum[ci * CHUNK : (ci + 1) * CHUNK, :].T for ci in range(n_chunks)]

    row = lax.broadcasted_iota(jnp.int32, (CHUNK, CHUNK), 0)
    col = lax.broadcasted_iota(jnp.int32, (CHUNK, CHUNK), 1)
    pairs = [(ci, h) for ci in range(n_chunks) for h in range(GDN_HEADS)]
    gc, gamma, sol, qkd = _delta_rule_prepare(pairs, gcum, gcum_t, beta_t, row, col, q_ref, k_ref, v_ref)
    for ci in range(n_chunks):
        mine = slice(ci * GDN_HEADS, (ci + 1) * GDN_HEADS)
        _delta_rule_apply(ci, (gc[mine], gamma[mine], sol[mine], qkd[mine]), q_ref, k_ref, z_ref, gn_ref, s_ref, og_ref)

    @pl.when(c == pl.num_programs(1) - 1)
    def _():
        so_ref[0] = s_ref[...]


def _cast_block(shape, steps):
    R, C = shape
    for col_blocks in (1, 2, 4, 8, 16):
        row_blocks = steps // col_blocks
        if (steps % col_blocks == 0 and R % row_blocks == 0 and C % col_blocks == 0
                and (R // row_blocks) % (2 * SUBLANES) == 0 and (C // col_blocks) % LANES == 0):
            return (R // row_blocks, C // col_blocks), col_blocks
    return None


def _gdn_prompt(proj, B, T, off_ba, cw, avec, dvec, gn, weights):
    rows = GDN_STEP_CHUNKS * CHUNK
    assert T % rows == 0
    ns = T // rows
    step = lambda b, c: b * ns + c
    plans = [_cast_block(w.shape, B * ns) for w in weights]
    riders = [w for w, p in zip(weights, plans) if p is not None]
    cast_specs = [pl.BlockSpec(blk, functools.partial(lambda b, c, cb: (step(b, c) // cb, step(b, c) % cb), cb=cb))
                  for blk, cb in (p for p in plans if p is not None)]
    head_scratch = pltpu.VMEM((GDN_HEADS, rows, GDN_D), f32)
    out = pl.pallas_call(
        functools.partial(_gdn_prompt_kernel, n_cast=len(riders)),
        grid=(B, ns),
        in_specs=[
            pl.BlockSpec((rows, CONV_CH), lambda b, c: (step(b, c), 0)),
            pl.BlockSpec((rows, GDN_W), lambda b, c: (step(b, c), OFF_Z // GDN_W)),
            pl.BlockSpec((rows, LANES), lambda b, c: (step(b, c), off_ba // LANES)),
            pl.BlockSpec((GDN_CONV, CONV_CH), lambda b, c: (0, 0)),
            pl.BlockSpec((1, LANES), lambda b, c: (0, 0)),
            pl.BlockSpec((1, LANES), lambda b, c: (0, 0)),
            pl.BlockSpec((1, GDN_D), lambda b, c: (0, 0)),
        ] + cast_specs,
        out_specs=[
            pl.BlockSpec((rows, GDN_W), lambda b, c: (step(b, c), 0)),
            pl.BlockSpec((1, GDN_HEADS, GDN_D, GDN_D), lambda b, c: (b, 0, 0, 0)),
        ] + cast_specs,
        out_shape=[
            jax.ShapeDtypeStruct((B * T, GDN_W), bf16),
            jax.ShapeDtypeStruct((B, GDN_HEADS, GDN_D, GDN_D), f32),
        ] + [jax.ShapeDtypeStruct(w.shape, bf16) for w in riders],
        scratch_shapes=[
            pltpu.VMEM((GDN_HEADS, GDN_D, GDN_D), f32),
            pltpu.VMEM((2 * SUBLANES, CONV_CH), f32),
            head_scratch, head_scratch, head_scratch,
        ],
        compiler_params=_cparams(("arbitrary", "arbitrary")),
        name="gdn_prompt",
    )(proj, proj, proj, cw, avec, dvec, gn, *riders)
    converted = iter(out[2:])
    return out[0], out[1], [next(converted) if p is not None else w.astype(bf16) for w, p in zip(weights, plans)]


def _gdn_step_kernel(qkv_ref, z_ref, ba_ref, sc_ref, s0_ref, cw_ref, av_ref, dv_ref, gn_ref, og_ref, so_ref):
    x_new = qkv_ref[0]
    taps = cw_ref[...]
    y = jnp.sum(sc_ref[0] * taps[0 : GDN_CONV - 1, :], axis=0, keepdims=True) + x_new * taps[GDN_CONV - 1 : GDN_CONV, :]
    y = _silu(y)
    ba = ba_ref[0]
    beta_t = jax.nn.sigmoid(ba)
    gamma_t = jnp.exp(-jnp.exp(av_ref[...]) * _softplus(ba + dv_ref[...]))
    z = z_ref[0]
    heads = range(GDN_HEADS)
    head = lambda base, h: y[:, base + h * GDN_D : base + (h + 1) * GDN_D]
    q = [head(0, h) for h in heads]
    k = [head(GDN_W, h) for h in heads]
    v = [head(2 * GDN_W, h) for h in heads]
    q = [q[h] * lax.rsqrt(jnp.sum(q[h] * q[h], -1, keepdims=True) + L2_EPS) * (GDN_D ** -0.5) for h in heads]
    k = [k[h] * lax.rsqrt(jnp.sum(k[h] * k[h], -1, keepdims=True) + L2_EPS) for h in heads]
    beta = [beta_t[:, h : h + 1] for h in heads]
    gamma = [gamma_t[:, GDN_HEADS + h : GDN_HEADS + h + 1] for h in heads]
    k_col = [jnp.broadcast_to(k[h], (GDN_D, GDN_D)).T for h in heads]
    q_col = [jnp.broadcast_to(q[h], (GDN_D, GDN_D)).T for h in heads]
    k_s = [jnp.sum(k_col[h] * s0_ref[0, h], axis=0, keepdims=True) for h in heads]
    q_s = [jnp.sum(q_col[h] * s0_ref[0, h], axis=0, keepdims=True) for h in heads]
    u = [beta[h] * v[h] - (beta[h] * gamma[h]) * k_s[h] for h in heads]
    o = [gamma[h] * q_s[h] + jnp.sum(q[h] * k[h], -1, keepdims=True) * u[h] for h in heads]
    for h in heads:
        so_ref[0, h] = gamma[h] * s0_ref[0, h] + k_col[h] * u[h]
        zh = z[:, h * GDN_D : (h + 1) * GDN_D]
        og_ref[0, :, h * GDN_D : (h + 1) * GDN_D] = (_rmsnorm(o[h], gn_ref[...]) * _silu(zh)).astype(bf16)


def _gdn_step(proj3, off_ba, state_conv, state_gdn, cw, avec, dvec, gn):
    B = proj3.shape[0]
    return pl.pallas_call(
        _gdn_step_kernel,
        grid=(B,),
        in_specs=[
            pl.BlockSpec((1, 1, CONV_CH), lambda b: (b, 0, 0)),
            pl.BlockSpec((1, 1, GDN_W), lambda b: (b, 0, OFF_Z // GDN_W)),
            pl.BlockSpec((1, 1, LANES), lambda b: (b, 0, off_ba // LANES)),
            pl.BlockSpec((1, GDN_CONV - 1, CONV_CH), lambda b: (b, 0, 0)),
            pl.BlockSpec((1, GDN_HEADS, GDN_D, GDN_D), lambda b: (b, 0, 0, 0)),
            pl.BlockSpec((GDN_CONV, CONV_CH), lambda b: (0, 0)),
            pl.BlockSpec((1, LANES), lambda b: (0, 0)),
            pl.BlockSpec((1, LANES), lambda b: (0, 0)),
            pl.BlockSpec((1, GDN_D), lambda b: (0, 0)),
        ],
        out_specs=[
            pl.BlockSpec((1, 1, GDN_W), lambda b: (b, 0, 0)),
            pl.BlockSpec((1, GDN_HEADS, GDN_D, GDN_D), lambda b: (b, 0, 0, 0)),
        ],
        out_shape=[
            jax.ShapeDtypeStruct((B, 1, GDN_W), bf16),
            jax.ShapeDtypeStruct((B, GDN_HEADS, GDN_D, GDN_D), f32),
        ],
        compiler_params=_cparams(("parallel",)),
        name="gdn_step",
    )(proj3, proj3, proj3, state_conv, state_gdn, cw, avec, dvec, gn)


def _rope(x, cos, sin_signed):
    width = x.shape[-1]
    lane = lax.broadcasted_iota(jnp.int32, x.shape, x.ndim - 1)
    first_half = (lane % SWA_HD) < (SWA_HD // 2)
    rot = jnp.where(first_half, pltpu.roll(x, width - SWA_HD // 2, x.ndim - 1), pltpu.roll(x, SWA_HD // 2, x.ndim - 1))
    return x * cos + rot * sin_signed


def _head_halves(x2, head_parity, lane):
    swapped = pltpu.roll(x2, SWA_HD, 1)
    lo_src, hi_src = (x2, swapped) if head_parity == 0 else (swapped, x2)
    return jnp.where(lane < SWA_HD, lo_src, 0.0), jnp.where(lane >= SWA_HD, hi_src, 0.0)


def _swa_prompt_kernel(sinks_ref, q_ref, kv_ref, cos_ref, sin_ref, os_ref, kc_ref, kprev_ref, vprev_ref):
    n = pl.program_id(1)

    @pl.when(n == 0)
    def _():
        kprev_ref[...] = jnp.zeros_like(kprev_ref)
        vprev_ref[...] = jnp.zeros_like(vprev_ref)

    cos = cos_ref[...]
    sin = sin_ref[...]
    kv = kv_ref[...]
    k_cur = _rope(kv[:, :SWA_KV_W], cos, sin)
    v_cur = kv[:, SWA_KV_W:]
    kc_ref[0] = k_cur
    k_prev = kprev_ref[...]
    v_prev = vprev_ref[...]
    kprev_ref[...] = k_cur
    vprev_ref[...] = v_cur

    row = lax.broadcasted_iota(jnp.int32, (WINDOW, WINDOW), 0)
    col = lax.broadcasted_iota(jnp.int32, (WINDOW, WINDOW), 1)
    own = col <= row
    prev_bias = jnp.where(n > 0, 0.0, -jnp.inf)
    lane = lax.broadcasted_iota(jnp.int32, (WINDOW, LANES), 1)
    scale = SWA_HD ** -0.5
    group = SWA_HEADS // SWA_KV_HEADS
    k_own, k_pre, v_own, v_pre, q2 = [], [], [], [], []
    for h in range(SWA_KV_HEADS):
        pair = slice((h // 2) * LANES, (h // 2 + 1) * LANES)
        k_own.append([a.astype(bf16) for a in _head_halves(k_cur[:, pair], h % 2, lane)])
        k_pre.append([a.astype(bf16) for a in _head_halves(k_prev[:, pair], h % 2, lane)])
        v_own.append([a.astype(bf16) for a in _head_halves(v_cur[:, pair], h % 2, lane)])
        v_pre.append([a.astype(bf16) for a in _head_halves(v_prev[:, pair], h % 2, lane)])
        q_h = _rope(q_ref[:, h * group * SWA_HD : (h + 1) * group * SWA_HD], cos, sin) * scale
        q2.append([q_h[:, j * LANES : (j + 1) * LANES].astype(bf16) for j in range(group // 2)])
    heads = [(h, j, par) for h in range(SWA_KV_HEADS) for j in range(group // 2) for par in range(2)]
    idx = range(len(heads))
    sink = [sinks_ref[h * group + 2 * j + par] for h, j, par in heads]
    s = [jnp.where(own, _dot_nt(q2[h][j], k_own[h][par]), _dot_nt(q2[h][j], k_pre[h][par]) + prev_bias)
         for h, j, par in heads]
    m = [jnp.maximum(jnp.max(s[i], -1, keepdims=True), sink[i]) for i in idx]
    p = [jnp.exp(s[i] - m[i]) for i in idx]
    inv = [1.0 / (jnp.sum(p[i], -1, keepdims=True) + jnp.exp(sink[i] - m[i])) for i in idx]
    o = [(_dot(jnp.where(own, p[i], 0.0).astype(bf16), v_own[h][par])
          + _dot(jnp.where(own, 0.0, p[i]).astype(bf16), v_pre[h][par])) * inv[i] for i, (h, j, par) in enumerate(heads)]
    for i in range(0, len(heads), 2):
        os_ref[:, i * SWA_HD : i * SWA_HD + LANES] = (o[i] + o[i + 1]).astype(bf16)


def _swa_prompt(proj, B, T, off_q, off_kv, cos, sin, sinks):
    assert T % WINDOW == 0 and SWA_KV_W == 2 * LANES
    nb = T // WINDOW
    return pl.pallas_call(
        _swa_prompt_kernel,
        grid=(B, nb),
        in_specs=[
            pl.BlockSpec(memory_space=pltpu.SMEM),
            pl.BlockSpec((WINDOW, SWA_Q_W), lambda b, n: (b * nb + n, off_q // SWA_Q_W)),
            pl.BlockSpec((WINDOW, 2 * SWA_KV_W), lambda b, n: (b * nb + n, off_kv // (2 * SWA_KV_W))),
            pl.BlockSpec((WINDOW, SWA_KV_W), lambda b, n: (n, 0)),
            pl.BlockSpec((WINDOW, SWA_KV_W), lambda b, n: (n, 0)),
        ],
        out_specs=[
            pl.BlockSpec((WINDOW, SWA_Q_W), lambda b, n: (b * nb + n, 0)),
            pl.BlockSpec((1, WINDOW, SWA_KV_W), lambda b, n: (b, 0, 0)),
        ],
        out_shape=[
            jax.ShapeDtypeStruct((B * T, SWA_Q_W), bf16),
            jax.ShapeDtypeStruct((B, WINDOW, SWA_KV_W), f32),
        ],
        scratch_shapes=[pltpu.VMEM((WINDOW, SWA_KV_W), f32), pltpu.VMEM((WINDOW, SWA_KV_W), f32)],
        compiler_params=_cparams(("parallel", "arbitrary")),
        name="swa_prompt",
    )(sinks, proj, proj, cos, sin)


SWA_STEP_BATCH = 8


def _swa_step_kernel(qe_ref, kv_ref, ck_ref, cv_ref, cos_ref, sin_ref, sinks_ref, r_ref, nk_ref, nv_ref):
    cos = cos_ref[...]
    sin = sin_ref[...]
    sink = sinks_ref[...]
    row = lax.broadcasted_iota(jnp.int32, (WINDOW, SWA_KV_W), 0)
    scale = SWA_HD ** -0.5
    seqs = range(qe_ref.shape[0])
    kv = [kv_ref[i] for i in seqs]
    k_new = [_rope(kv[i][:, :SWA_KV_W], cos, sin) for i in seqs]
    keys = [jnp.where(row == WINDOW - 1, k_new[i], pltpu.roll(ck_ref[i], WINDOW - 1, 0)) for i in seqs]
    vals = [jnp.where(row == WINDOW - 1, kv[i][:, SWA_KV_W:], pltpu.roll(cv_ref[i], WINDOW - 1, 0)) for i in seqs]
    for i in seqs:
        nk_ref[i] = keys[i]
        nv_ref[i] = vals[i]
    q = [_rope(qe_ref[i], cos, sin) for i in seqs]
    s = [_dot_nt(q[i].astype(bf16), keys[i].astype(bf16)) * scale for i in seqs]
    m = [jnp.maximum(jnp.max(s[i], -1, keepdims=True), sink) for i in seqs]
    p = [jnp.exp(s[i] - m[i]) for i in seqs]
    denom = [jnp.sum(p[i], -1, keepdims=True) + jnp.exp(sink - m[i]) for i in seqs]
    for i in seqs:
        r_ref[i] = _dot((p[i] / denom[i]).astype(bf16), vals[i].astype(bf16))


def _swa_step(q_exp, proj3, off_kv, cache_k, cache_v, cos, sin, sinks_col):
    B = q_exp.shape[0]
    bb = math.gcd(B, SWA_STEP_BATCH)
    assert cache_k.shape[1] == WINDOW
    return pl.pallas_call(
        _swa_step_kernel,
        grid=(B // bb,),
        in_specs=[
            pl.BlockSpec((bb, SWA_HEADS, SWA_KV_W), lambda i: (i, 0, 0)),
            pl.BlockSpec((bb, 1, 2 * SWA_KV_W), lambda i: (i, 0, off_kv // (2 * SWA_KV_W))),
            pl.BlockSpec((bb, WINDOW, SWA_KV_W), lambda i: (i, 0, 0)),
            pl.BlockSpec((bb, WINDOW, SWA_KV_W), lambda i: (i, 0, 0)),
            pl.BlockSpec((1, SWA_KV_W), lambda i: (0, 0)),
            pl.BlockSpec((1, SWA_KV_W), lambda i: (0, 0)),
            pl.BlockSpec((SWA_HEADS, 1), lambda i: (0, 0)),
        ],
        out_specs=[
            pl.BlockSpec((bb, SWA_HEADS, SWA_KV_W), lambda i: (i, 0, 0)),
            pl.BlockSpec((bb, WINDOW, SWA_KV_W), lambda i: (i, 0, 0)),
            pl.BlockSpec((bb, WINDOW, SWA_KV_W), lambda i: (i, 0, 0)),
        ],
        out_shape=[
            jax.ShapeDtypeStruct((B, SWA_HEADS, SWA_KV_W), f32),
            jax.ShapeDtypeStruct((B, WINDOW, SWA_KV_W), f32),
            jax.ShapeDtypeStruct((B, WINDOW, SWA_KV_W), f32),
        ],
        compiler_params=_cparams(("parallel",)),
        name="swa_step",
    )(q_exp, proj3, cache_k, cache_v, cos, sin, sinks_col)


def _merge_core(og_ref, os_ref, gg_ref, gs_ref, x_ref, wg_ref, ws_ref, wo_ref, gq_ref, wq_ref):
    p_gdn = _dot(og_ref[...], wg_ref[...])
    p_swa = _dot(os_ref[...], ws_ref[...])
    merged = jax.nn.sigmoid(gg_ref[...]) * p_gdn + jax.nn.sigmoid(gs_ref[...]) * p_swa
    x_new = x_ref[...] + _dot(merged.astype(bf16), wo_ref[...])
    return x_new, _dot(_rmsnorm(x_new, gq_ref[...]).astype(bf16), wq_ref[...])


def _merge_kernel(*refs):
    xo_ref, qm_ref = refs[-2:]
    xo_ref[...], qm_ref[...] = _merge_core(*refs[:-2])


def _merge_mem_kernel(*refs):
    mk_ref, mv_ref, wmo_ref, xo_ref = refs[-4:]
    x_new, q = _merge_core(*refs[:-4])
    q = q.astype(bf16)
    scale = MEM_HD ** -0.5
    heads = range(MEM_HEADS)
    cols = lambda h: slice(h * MEM_HD, (h + 1) * MEM_HD)
    s = [_dot_nt(q[:, cols(h)], mk_ref[0, :, cols(h)]) * scale for h in heads]
    p = [jnp.exp(s[h] - jnp.max(s[h], -1, keepdims=True)) for h in heads]
    p = [p[h] / jnp.sum(p[h], -1, keepdims=True) for h in heads]
    o = [_dot(p[h].astype(bf16), mv_ref[0, :, cols(h)]) for h in heads]
    xo_ref[...] = x_new + _dot(jnp.concatenate(o, axis=-1).astype(bf16), wmo_ref[...])


def _merge(og, os_, proj, off_gg, x, wg, ws, wo, gq, wq, mem=None):
    M, D = x.shape
    tm = min(256, M)
    assert M % tm == 0 and off_gg % D == 0
    const = lambda shape: pl.BlockSpec(shape, lambda i: (0, 0), pipeline_mode=pl.Buffered(1))
    in_specs = [
        pl.BlockSpec((tm, GDN_W), lambda i: (i, 0)),
        pl.BlockSpec((tm, SWA_Q_W), lambda i: (i, 0)),
        pl.BlockSpec((tm, D), lambda i: (i, off_gg // D)),
        pl.BlockSpec((tm, D), lambda i: (i, off_gg // D + 1)),
        pl.BlockSpec((tm, D), lambda i: (i, 0)),
        const((GDN_W, D)),
        const((SWA_Q_W, D)),
        const((D, D)),
        const((1, D)),
        const((D, MEM_W)),
    ]
    args = [og, os_, proj, proj, x, wg, ws, wo, gq, wq]
    x_spec = pl.BlockSpec((tm, D), lambda i: (i, 0))
    x_shape = jax.ShapeDtypeStruct((M, D), f32)
    if mem is None:
        return pl.pallas_call(
            _merge_kernel,
            grid=(M // tm,),
            in_specs=in_specs,
            out_specs=[x_spec, pl.BlockSpec((tm, MEM_W), lambda i: (i, 0))],
            out_shape=[x_shape, jax.ShapeDtypeStruct((M, MEM_W), f32)],
            compiler_params=_cparams(("parallel",)),
            name="merge",
        )(*args)
    mem_k, mem_v, wmo, seq_rows = mem
    assert seq_rows % tm == 0
    mt = mem_k.shape[1]
    mem_spec = pl.BlockSpec((1, mt, MEM_W), lambda i: (i // (seq_rows // tm), 0, 0))
    return pl.pallas_call(
        _merge_mem_kernel,
        grid=(M // tm,),
        in_specs=in_specs + [mem_spec, mem_spec, const((MEM_W, D))],
        out_specs=x_spec,
        out_shape=x_shape,
        compiler_params=_cparams(("parallel",)),
        name="merge_mem",
    )(*args, mem_k, mem_v, wmo)


MEM_STEP_BATCH = 4


def _mem_attn_step_kernel(q_ref, k_ref, v_ref, o_ref):
    mt2 = k_ref.shape[1] // SUBLANES
    for i in range(q_ref.shape[0]):
        q8 = q_ref[i] * (MEM_HD ** -0.5)
        s = jnp.sum(k_ref[i].reshape(mt2, SUBLANES, MEM_HD) * q8, axis=-1, keepdims=True)
        m = jnp.max(s, axis=0)
        m = jnp.maximum(m, pltpu.roll(m, MEM_HEADS, 0))
        p = jnp.exp(s - m)
        l = jnp.sum(p, axis=0)
        l = l + pltpu.roll(l, MEM_HEADS, 0)
        o = jnp.sum(p * v_ref[i].reshape(mt2, SUBLANES, MEM_HD), axis=0)
        o_ref[i] = (o + pltpu.roll(o, MEM_HEADS, 0)) / l


def _mem_attn_step(q8, mem_k, mem_v):
    B, rows, _ = mem_k.shape
    assert 2 * MEM_HEADS == SUBLANES and rows % SUBLANES == 0
    bb = math.gcd(B, MEM_STEP_BATCH)
    return pl.pallas_call(
        _mem_attn_step_kernel,
        grid=(B // bb,),
        in_specs=[
            pl.BlockSpec((bb, SUBLANES, MEM_HD), lambda i: (i, 0, 0)),
            pl.BlockSpec((bb, rows, MEM_HD), lambda i: (i, 0, 0)),
            pl.BlockSpec((bb, rows, MEM_HD), lambda i: (i, 0, 0)),
        ],
        out_specs=pl.BlockSpec((bb, SUBLANES, MEM_HD), lambda i: (i, 0, 0)),
        out_shape=jax.ShapeDtypeStruct((B, SUBLANES, MEM_HD), f32),
        compiler_params=_cparams(("parallel",)),
        name="mem_attn_step",
    )(q8, mem_k, mem_v)


def _proj_residual_kernel(a_ref, w_ref, x_ref, o_ref):
    o_ref[...] = x_ref[...] + _dot(a_ref[...].astype(bf16), w_ref[...])


def _proj_residual(a, w, x):
    M, D = x.shape
    return pl.pallas_call(
        _proj_residual_kernel,
        out_shape=jax.ShapeDtypeStruct((M, D), f32),
        compiler_params=pltpu.CompilerParams(vmem_limit_bytes=VMEM_LIMIT),
        name="proj_residual",
    )(a, w, x)


def _reorder_w_in_kernel(w_ref, o_ref, *, pieces, pad):
    w = w_ref[...]
    parts = [w[:, lo:hi].astype(bf16) for lo, hi in pieces]
    o_ref[...] = jnp.concatenate(parts + [jnp.zeros((w.shape[0], pad), bf16)], axis=1)


def _reorder_w_in(w, pieces, width):
    D, d_in = w.shape
    rows = 2 * SUBLANES
    pad = width - sum(hi - lo for lo, hi in pieces)
    return pl.pallas_call(
        functools.partial(_reorder_w_in_kernel, pieces=pieces, pad=pad),
        grid=(D // rows,),
        in_specs=[pl.BlockSpec((rows, d_in), lambda i: (i, 0))],
        out_specs=pl.BlockSpec((rows, width), lambda i: (i, 0)),
        out_shape=jax.ShapeDtypeStruct((D, width), bf16),
        compiler_params=_cparams(("parallel",)),
        name="reorder_w_in",
    )(w)


def _rope_tables(pos):
    half = SWA_HD // 2
    inv_freq = ROPE_THETA ** (-jnp.arange(half, dtype=f32) / half)
    ang = pos.astype(f32)[:, None] * inv_freq[None, :]
    cos = jnp.cos(ang)
    sin = jnp.sin(ang)
    reps = SWA_KV_W // SWA_HD
    return jnp.tile(jnp.concatenate([cos, cos], -1), (1, reps)), jnp.tile(jnp.concatenate([-sin, sin], -1), (1, reps))


def kernel(x_prompt, x_sample, state_gdn, state_conv, cache_swa_k, cache_swa_v, cache_mem_k, cache_mem_v, mem_prompt, norm_ffn1, ffn1_w1, ffn1_w3, ffn1_w2, norm_mix, w_in, conv_w, gdn_A_log, gdn_dt_bias, gdn_norm, swa_sinks, w_br_gdn, w_br_swa, w_out, norm_mem_q, norm_mem_kv, w_mem_q, w_mem_k, w_mem_v, w_mem_o, norm_ffn2, ffn2_w1, ffn2_w3, ffn2_w2, norm_final):
    Bp, Tp, D = x_prompt.shape
    Bs, Ts, _ = x_sample.shape
    assert Ts == 1
    depth = norm_ffn1.shape[0]
    n_mem = mem_prompt.shape[1]
    group = SWA_HEADS // SWA_KV_HEADS

    off_gs = OFF_GG + D
    off_q = off_gs + D
    off_kv = off_q + SWA_Q_W
    off_ba = off_kv + 2 * SWA_KV_W
    d_in_pad = -(-(off_ba + LANES) // PROJ_TN) * PROJ_TN
    o_b = CONV_CH + GDN_W
    o_q = o_b + 2 * GDN_HEADS
    o_gg = o_q + SWA_Q_W + 2 * SWA_KV_W

    cos_p, sin_p = _rope_tables(jnp.arange(Tp, dtype=jnp.int32))
    cos_s, sin_s = _rope_tables(PAST_LEN + jnp.arange(Ts, dtype=jnp.int32))
    eye_kv = jnp.eye(SWA_KV_HEADS, dtype=f32)
    row = lambda v: v.reshape(1, -1)

    hp = x_prompt.reshape(Bp * Tp, D)
    hs = x_sample.reshape(Bs, D)
    outs = [[] for _ in range(10)]
    for l in range(depth):
        wi = w_in[l]
        w_in_r = _reorder_w_in(wi, ((0, o_b), (o_gg, wi.shape[1]), (o_q, o_gg), (o_b, o_q)), d_in_pad)
        ffn1 = (row(norm_ffn1[l]), ffn1_w1[l].astype(bf16), ffn1_w3[l].astype(bf16), ffn1_w2[l].astype(bf16))
        last = l == depth - 1
        gfin = row(norm_final)
        avec = jnp.zeros((1, LANES), f32).at[0, GDN_HEADS : 2 * GDN_HEADS].set(gdn_A_log[l])
        dvec = jnp.zeros((1, LANES), f32).at[0, GDN_HEADS : 2 * GDN_HEADS].set(gdn_dt_bias[l])
        gdn_common = (conv_w[l], avec, dvec, row(gdn_norm[l]))
        tn_in = PROJ_TN

        x1 = _ffn(hp, *ffn1, gfin, False)
        proj = _norm_matmul(x1, row(norm_mix[l]), w_in_r, tn_in)
        later = (ffn2_w1[l], ffn2_w3[l], ffn2_w2[l], w_br_gdn[l], w_br_swa[l], w_out[l], w_mem_q[l],
                 w_mem_k[l], w_mem_v[l], w_mem_o[l])
        og, s_new, later = _gdn_prompt(proj, Bp, Tp, off_ba, *gdn_common, later)
        ffn2 = (row(norm_ffn2[l]), *later[0:3])
        merge_w = (*later[3:6], row(norm_mem_q[l]), later[6])
        wmk, wmv, wmo = later[7:10]
        os_, kc = _swa_prompt(proj, Bp, Tp, off_q, off_kv, cos_p, sin_p, swa_sinks[l])
        mem_x = mem_prompt.reshape(Bp * n_mem, D)
        mk = _norm_matmul(mem_x, row(norm_mem_kv[l]), wmk, MEM_W)
        mv = _norm_matmul(mem_x, row(norm_mem_kv[l]), wmv, MEM_W)
        mem = (mk.reshape(Bp, n_mem, MEM_W).astype(bf16), mv.reshape(Bp, n_mem, MEM_W).astype(bf16), wmo, Tp)
        x3 = _merge(og, os_, proj, OFF_GG, x1, *merge_w, mem=mem)
        hp = _ffn(x3, *ffn2, gfin, last)
        proj_b = proj.reshape(Bp, Tp, d_in_pad)
        outs[0].append(s_new)
        outs[1].append(proj_b[:, Tp - (GDN_CONV - 1) :, :CONV_CH])
        outs[2].append(kc.reshape(Bp, WINDOW, SWA_KV_HEADS, SWA_HD))
        outs[3].append(proj_b[:, Tp - WINDOW :, off_kv + SWA_KV_W : off_kv + 2 * SWA_KV_W].reshape(Bp, WINDOW, SWA_KV_HEADS, SWA_HD))
        outs[4].append(mk.reshape(Bp, n_mem, MEM_HEADS, MEM_HD))
        outs[5].append(mv.reshape(Bp, n_mem, MEM_HEADS, MEM_HD))

        x1 = _ffn(hs, *ffn1, gfin, False)
        proj = _norm_matmul(x1, row(norm_mix[l]), w_in_r, tn_in)
        proj3 = proj.reshape(Bs, 1, d_in_pad)
        og, s_new = _gdn_step(proj3, off_ba, state_conv[l], state_gdn[l], *gdn_common)
        q_raw = proj[:, off_q : off_q + SWA_Q_W].reshape(Bs, SWA_KV_HEADS, group, 1, SWA_HD)
        q_exp = (q_raw * eye_kv[None, :, None, :, None]).reshape(Bs, SWA_HEADS, SWA_KV_W)
        ck = cache_swa_k[l].reshape(Bs, WINDOW, SWA_KV_W)
        cv = cache_swa_v[l].reshape(Bs, WINDOW, SWA_KV_W)
        r, nk, nv = _swa_step(q_exp, proj3, off_kv, ck, cv, cos_s, sin_s, swa_sinks[l].reshape(SWA_HEADS, 1))
        r5 = r.reshape(Bs, SWA_KV_HEADS, group, SWA_KV_HEADS, SWA_HD)
        kvh = jnp.arange(SWA_KV_HEADS)
        os_ = jnp.transpose(r5[:, kvh, :, kvh, :], (1, 0, 2, 3)).reshape(Bs, SWA_Q_W).astype(bf16)
        x2, qm = _merge(og.reshape(Bs, GDN_W), os_, proj, OFF_GG, x1, *merge_w)
        q8 = jnp.tile(qm.reshape(Bs, MEM_HEADS, MEM_HD), (1, 2, 1))
        om = _mem_attn_step(q8, cache_mem_k[l].reshape(Bs, n_mem * MEM_HEADS, MEM_HD),
                            cache_mem_v[l].reshape(Bs, n_mem * MEM_HEADS, MEM_HD))
        x3 = _proj_residual(om[:, :MEM_HEADS].reshape(Bs, MEM_W), wmo, x2)
        hs = _ffn(x3, *ffn2, gfin, last)
        outs[6].append(s_new)
        outs[7].append(jnp.concatenate([state_conv[l][:, 1:], proj3[:, :, :CONV_CH]], axis=1))
        outs[8].append(nk.reshape(Bs, WINDOW, SWA_KV_HEADS, SWA_HD))
        outs[9].append(nv.reshape(Bs, WINDOW, SWA_KV_HEADS, SWA_HD))

    return (hp.reshape(Bp, Tp, D), hs.reshape(Bs, Ts, D), *(jnp.stack(o) for o in outs))
```

```python
import functools
import math

import jax
import jax.numpy as jnp
from jax import lax
from jax.experimental import pallas as pl
from jax.experimental.pallas import tpu as pltpu

f32 = jnp.float32
bf16 = jnp.bfloat16

PAST_LEN = 16384
GDN_HEADS = 8
GDN_D = 128
GDN_CONV = 4
SWA_HEADS = 16
SWA_KV_HEADS = 4
SWA_HD = 64
WINDOW = 128
ROPE_THETA = 10000.0
MEM_HEADS = 4
MEM_HD = 128
EPS = 1e-6
L2_EPS = 1e-6

LANES = 128
SUBLANES = 8
CHUNK = 128
GDN_STEP_CHUNKS = 2
FFN_SLAB = 64
PROJ_TN = 768
VMEM_LIMIT = 58 * 1024 * 1024

GDN_W = GDN_HEADS * GDN_D
CONV_CH = 3 * GDN_W
SWA_Q_W = SWA_HEADS * SWA_HD
SWA_KV_W = SWA_KV_HEADS * SWA_HD
MEM_W = MEM_HEADS * MEM_HD

OFF_Z = CONV_CH
OFF_GG = OFF_Z + GDN_W


def _cparams(semantics):
    return pltpu.CompilerParams(dimension_semantics=semantics, vmem_limit_bytes=VMEM_LIMIT)


def _dot(a, b):
    return jnp.dot(a, b, preferred_element_type=f32)


def _dot_nt(a, b):
    return lax.dot_general(a, b, (((1,), (1,)), ((), ())), preferred_element_type=f32)


def _rmsnorm(x, g):
    return x * lax.rsqrt(jnp.mean(x * x, -1, keepdims=True) + EPS) * g


def _silu(x):
    return x * jax.nn.sigmoid(x)


def _softplus(x):
    return jnp.maximum(x, 0.0) + jnp.log1p(jnp.exp(-jnp.abs(x)))


def _ffn_kernel(x_ref, g_ref, w1_ref, w3_ref, w2_ref, gf_ref, o_ref, h_ref, *, final_norm):
    j = pl.program_id(1)
    slab = min(FFN_SLAB, x_ref.shape[0])
    n_slabs = x_ref.shape[0] // slab

    def over_slabs(body):
        def step(s, carry):
            body(pl.ds(pl.multiple_of(s * slab, slab), slab))
            return carry
        lax.fori_loop(0, n_slabs, step, 0, unroll=min(4, n_slabs))

    @pl.when(j == 0)
    def _():
        def prologue(rows):
            h_ref[rows, :] = _rmsnorm(x_ref[rows, :], g_ref[...]).astype(bf16)
            o_ref[rows, :] = jnp.zeros((slab, o_ref.shape[1]), f32)
        over_slabs(prologue)

    h = h_ref[...]
    a = _dot(h, w1_ref[...])
    b = _dot(h, w3_ref[...])
    o_ref[...] += _dot((_silu(a) * b).astype(bf16), w2_ref[...])

    @pl.when(j == pl.num_programs(1) - 1)
    def _():
        def epilogue(rows):
            y = x_ref[rows, :] + 0.5 * o_ref[rows, :]
            if final_norm:
                y = _rmsnorm(y, gf_ref[...])
            o_ref[rows, :] = y
        over_slabs(epilogue)


def _ffn(x, g, w1, w3, w2, gf, final_norm):
    M, D = x.shape
    F = w1.shape[1]
    tm = min(1024, M)
    tf = 512
    assert M % tm == 0 and F % tf == 0
    return pl.pallas_call(
        functools.partial(_ffn_kernel, final_norm=final_norm),
        grid=(M // tm, F // tf),
        in_specs=[
            pl.BlockSpec((tm, D), lambda i, j: (i, 0)),
            pl.BlockSpec((1, D), lambda i, j: (0, 0)),
            pl.BlockSpec((D, tf), lambda i, j: (0, j)),
            pl.BlockSpec((D, tf), lambda i, j: (0, j)),
            pl.BlockSpec((tf, D), lambda i, j: (j, 0)),
            pl.BlockSpec((1, D), lambda i, j: (0, 0)),
        ],
        out_specs=pl.BlockSpec((tm, D), lambda i, j: (i, 0)),
        out_shape=jax.ShapeDtypeStruct((M, D), f32),
        scratch_shapes=[pltpu.VMEM((tm, D), bf16)],
        compiler_params=_cparams(("parallel", "arbitrary")),
        name="ffn",
    )(x, g, w1, w3, w2, gf)


def _norm_matmul_kernel(x_ref, g_ref, w_ref, o_ref, h_ref):
    @pl.when(pl.program_id(1) == 0)
    def _():
        h_ref[...] = _rmsnorm(x_ref[...], g_ref[...]).astype(bf16)

    o_ref[...] = _dot(h_ref[...], w_ref[...])


def _norm_matmul(x, g, w, tn):
    M, D = x.shape
    N = w.shape[1]
    tm = min(1024, M)
    assert M % tm == 0 and N % tn == 0
    return pl.pallas_call(
        _norm_matmul_kernel,
        grid=(M // tm, N // tn),
        in_specs=[
            pl.BlockSpec((tm, D), lambda i, j: (i, 0)),
            pl.BlockSpec((1, D), lambda i, j: (0, 0)),
            pl.BlockSpec((D, tn), lambda i, j: (0, j)),
        ],
        out_specs=pl.BlockSpec((tm, tn), lambda i, j: (i, j)),
        out_shape=jax.ShapeDtypeStruct((M, N), f32),
        scratch_shapes=[pltpu.VMEM((tm, D), bf16)],
        compiler_params=_cparams(("parallel", "arbitrary")),
        name="norm_matmul",
    )(x, g, w)


def _bdot(a, b):
    return _dot(a.astype(bf16), b.astype(bf16))


def _unit_lower_inverse(nmats, row, col):
    eye = jnp.where(row == col, 1.0, 0.0).astype(f32)

    def same_block(size):
        return (row // size) == (col // size)

    blk = same_block(SUBLANES)
    a1 = [jnp.where(blk, n, 0.0) for n in nmats]
    a2 = [_bdot(a, a) for a in a1]
    a4 = [_bdot(a, a) for a in a2]
    ts = [_bdot(eye - x1, eye + x2) for x1, x2 in zip(a1, a2)]
    ts = [_bdot(t, eye + x4) for t, x4 in zip(ts, a4)]
    size = SUBLANES
    while size < CHUNK:
        nxt = same_block(2 * size)
        sel = jnp.logical_and(nxt, jnp.logical_not(blk))
        tbs = [t.astype(bf16) for t in ts]
        xs = [_dot(jnp.where(sel, n, 0.0).astype(bf16), tb) for n, tb in zip(nmats, tbs)]
        ts = [t - _dot(tb, x.astype(bf16)) for t, tb, x in zip(ts, tbs, xs)]
        blk = nxt
        size *= 2
    return ts


def _conv_silu_slab(x_ref, top_ref, w_ref, sl):
    rows = x_ref.shape[0]
    y = None
    for tap in range(GDN_CONV):
        back = GDN_CONV - 1 - tap
        window = jnp.concatenate(
            [top_ref[SUBLANES - back : 2 * SUBLANES - back, sl], x_ref[SUBLANES - back : rows - back, sl]], axis=0)
        term = window * w_ref[tap : tap + 1, sl]
        y = term if y is None else y + term
    return _silu(y)


def _delta_rule_prepare(pairs, gcum, gcum_t, beta_t, row, col, q_ref, k_ref, v_ref):
    n = range(len(pairs))
    incl = row >= col
    strict = row > col
    rows_of = lambda ci: slice(ci * CHUNK, (ci + 1) * CHUNK)
    gc = [gcum[rows_of(ci), GDN_HEADS + h : GDN_HEADS + h + 1] for ci, h in pairs]
    gr = [gcum_t[ci][GDN_HEADS + h : GDN_HEADS + h + 1, :] for ci, h in pairs]
    beta = [beta_t[rows_of(ci), h : h + 1] for ci, h in pairs]
    k = [k_ref[h, rows_of(ci), :] for ci, h in pairs]
    kb = [x.astype(bf16) for x in k]
    kk = [_dot_nt(kb[i], kb[i]) for i in n]
    qk = [_dot_nt(q_ref[h, rows_of(ci), :].astype(bf16), kb[i]) for i, (ci, h) in enumerate(pairs)]
    e = [jnp.exp(jnp.where(incl, gc[i] - gr[i], 0.0)) for i in n]
    nmat = [beta[i] * kk[i] * jnp.where(strict, e[i], 0.0) for i in n]
    qkd = [(qk[i] * jnp.where(incl, e[i], 0.0)).astype(bf16) for i in n]
    tinv = _unit_lower_inverse(nmat, row, col)
    gamma = [jnp.exp(gc[i]) for i in n]
    rhs = [jnp.concatenate([beta[i] * v_ref[h, rows_of(ci), :], (beta[i] * gamma[i]) * k[i]], axis=-1).astype(bf16)
           for i, (ci, h) in enumerate(pairs)]
    sol = [_dot(tinv[i].astype(bf16), rhs[i]) for i in n]
    return gc, gamma, sol, qkd


def _delta_rule_apply(ci, prepared, q_ref, k_ref, z_ref, gn_ref, s_ref, og_ref):
    gc, gamma, sol, qkd = prepared
    heads = range(GDN_HEADS)
    rows = slice(ci * CHUNK, (ci + 1) * CHUNK)
    sb = [s_ref[h].astype(bf16) for h in heads]
    ub = [(sol[h][:, :GDN_D] - _dot(sol[h][:, GDN_D:].astype(bf16), sb[h])).astype(bf16) for h in heads]
    o = [_dot((q_ref[h, rows, :] * gamma[h]).astype(bf16), sb[h]) + _dot(qkd[h], ub[h]) for h in heads]
    for h in heads:
        g_last = gc[h][CHUNK - 1 : CHUNK, :]
        k_end = k_ref[h, rows, :] * jnp.exp(g_last - gc[h])
        s_ref[h] = jnp.exp(g_last) * s_ref[h] + _dot(k_end.T.astype(bf16), ub[h])
        zh = z_ref[rows, h * GDN_D : (h + 1) * GDN_D]
        og_ref[rows, h * GDN_D : (h + 1) * GDN_D] = (_rmsnorm(o[h], gn_ref[...]) * _silu(zh)).astype(bf16)


def _gdn_prompt_kernel(*refs, n_cast):
    qkv_ref, z_ref, ba_ref, cw_ref, av_ref, dv_ref, gn_ref = refs[:7]
    cast_in = refs[7 : 7 + n_cast]
    og_ref, so_ref = refs[7 + n_cast : 9 + n_cast]
    cast_out = refs[9 + n_cast : 9 + 2 * n_cast]
    s_ref, carry_ref, q_ref, k_ref, v_ref = refs[9 + 2 * n_cast :]
    for src, dst in zip(cast_in, cast_out):
        dst[...] = src[...].astype(bf16)

    c = pl.program_id(1)
    rows = qkv_ref.shape[0]
    n_chunks = rows // CHUNK

    @pl.when(c == 0)
    def _():
        s_ref[...] = jnp.zeros_like(s_ref)
        carry_ref[0:SUBLANES, :] = jnp.zeros((SUBLANES, CONV_CH), f32)

    carry_ref[SUBLANES : 2 * SUBLANES, :] = qkv_ref[0:SUBLANES, :]
    for j in range(3 * GDN_HEADS):
        sl = slice(j * LANES, (j + 1) * LANES)
        y = _conv_silu_slab(qkv_ref, carry_ref, cw_ref, sl)
        h = j % GDN_HEADS
        if j < 2 * GDN_HEADS:
            y = y * lax.rsqrt(jnp.sum(y * y, -1, keepdims=True) + L2_EPS)
            if j < GDN_HEADS:
                q_ref[h] = y * (GDN_D ** -0.5)
            else:
                k_ref[h] = y
        else:
            v_ref[h] = y
    carry_ref[0:SUBLANES, :] = qkv_ref[rows - SUBLANES : rows, :]

    ba = ba_ref[...]
    beta_t = jax.nn.sigmoid(ba)
    g_t = -jnp.exp(av_ref[...]) * _softplus(ba + dv_ref[...])
    row_in_chunk = lax.broadcasted_iota(jnp.int32, (rows, LANES), 0) % CHUNK
    gcum = g_t
    shift = 1
    while shift < CHUNK:
        gcum = gcum + jnp.where(row_in_chunk >= shift, pltpu.roll(gcum, shift, 0), 0.0)
        shift *= 2
    gcum_t = [gcum[ci * CHUNK : (ci + 1) * CHUNK, :].T for ci in range(n_chunks)]

    row = lax.broadcasted_iota(jnp.int32, (CHUNK, CHUNK), 0)
    col = lax.broadcasted_iota(jnp.int32, (CHUNK, CHUNK), 1)
    pairs = [(ci, h) for ci in range(n_chunks) for h in range(GDN_HEADS)]
    gc, gamma, sol, qkd = _delta_rule_prepare(pairs, gcum, gcum_t, beta_t, row, col, q_ref, k_ref, v_ref)
    for ci in range(n_chunks):
        mine = slice(ci * GDN_HEADS, (ci + 1) * GDN_HEADS)
        _delta_rule_apply(ci, (gc[mine], gamma[mine], sol[mine], qkd[mine]), q_ref, k_ref, z_ref, gn_ref, s_ref, og_ref)

    @pl.when(c == pl.num_programs(1) - 1)
    def _():
        so_ref[0] = s_ref[...]


def _cast_block(shape, steps):
    R, C = shape
    for col_blocks in (1, 2, 4, 8, 16):
        row_blocks = steps // col_blocks
        if (steps % col_blocks == 0 and R % row_blocks == 0 and C % col_blocks == 0
                and (R // row_blocks) % (2 * SUBLANES) == 0 and (C // col_blocks) % LANES == 0):
            return (R // row_blocks, C // col_blocks), col_blocks
    return None


def _gdn_prompt(proj, B, T, off_ba, cw, avec, dvec, gn, weights):
    rows = GDN_STEP_CHUNKS * CHUNK
    assert T % rows == 0
    ns = T // rows
    step = lambda b, c: b * ns + c
    plans = [_cast_block(w.shape, B * ns) for w in weights]
    riders = [w for w, p in zip(weights, plans) if p is not None]
    cast_specs = [pl.BlockSpec(blk, functools.partial(lambda b, c, cb: (step(b, c) // cb, step(b, c) % cb), cb=cb))
                  for blk, cb in (p for p in plans if p is not None)]
    head_scratch = pltpu.VMEM((GDN_HEADS, rows, GDN_D), f32)
    out = pl.pallas_call(
        functools.partial(_gdn_prompt_kernel, n_cast=len(riders)),
        grid=(B, ns),
        in_specs=[
            pl.BlockSpec((rows, CONV_CH), lambda b, c: (step(b, c), 0)),
            pl.BlockSpec((rows, GDN_W), lambda b, c: (step(b, c), OFF_Z // GDN_W)),
            pl.BlockSpec((rows, LANES), lambda b, c: (step(b, c), off_ba // LANES)),
            pl.BlockSpec((GDN_CONV, CONV_CH), lambda b, c: (0, 0)),
            pl.BlockSpec((1, LANES), lambda b, c: (0, 0)),
            pl.BlockSpec((1, LANES), lambda b, c: (0, 0)),
            pl.BlockSpec((1, GDN_D), lambda b, c: (0, 0)),
        ] + cast_specs,
        out_specs=[
            pl.BlockSpec((rows, GDN_W), lambda b, c: (step(b, c), 0)),
            pl.BlockSpec((1, GDN_HEADS, GDN_D, GDN_D), lambda b, c: (b, 0, 0, 0)),
        ] + cast_specs,
        out_shape=[
            jax.ShapeDtypeStruct((B * T, GDN_W), bf16),
            jax.ShapeDtypeStruct((B, GDN_HEADS, GDN_D, GDN_D), f32),
        ] + [jax.ShapeDtypeStruct(w.shape, bf16) for w in riders],
        scratch_shapes=[
            pltpu.VMEM((GDN_HEADS, GDN_D, GDN_D), f32),
            pltpu.VMEM((2 * SUBLANES, CONV_CH), f32),
            head_scratch, head_scratch, head_scratch,
        ],
        compiler_params=_cparams(("arbitrary", "arbitrary")),
        name="gdn_prompt",
    )(proj, proj, proj, cw, avec, dvec, gn, *riders)
    converted = iter(out[2:])
    return out[0], out[1], [next(converted) if p is not None else w.astype(bf16) for w, p in zip(weights, plans)]


def _gdn_step_kernel(qkv_ref, z_ref, ba_ref, sc_ref, s0_ref, cw_ref, av_ref, dv_ref, gn_ref, og_ref, so_ref):
    x_new = qkv_ref[0]
    taps = cw_ref[...]
    y = jnp.sum(sc_ref[0] * taps[0 : GDN_CONV - 1, :], axis=0, keepdims=True) + x_new * taps[GDN_CONV - 1 : GDN_CONV, :]
    y = _silu(y)
    ba = ba_ref[0]
    beta_t = jax.nn.sigmoid(ba)
    gamma_t = jnp.exp(-jnp.exp(av_ref[...]) * _softplus(ba + dv_ref[...]))
    z = z_ref[0]
    heads = range(GDN_HEADS)
    head = lambda base, h: y[:, base + h * GDN_D : base + (h + 1) * GDN_D]
    q = [head(0, h) for h in heads]
    k = [head(GDN_W, h) for h in heads]
    v = [head(2 * GDN_W, h) for h in heads]
    q = [q[h] * lax.rsqrt(jnp.sum(q[h] * q[h], -1, keepdims=True) + L2_EPS) * (GDN_D ** -0.5) for h in heads]
    k = [k[h] * lax.rsqrt(jnp.sum(k[h] * k[h], -1, keepdims=True) + L2_EPS) for h in heads]
    beta = [beta_t[:, h : h + 1] for h in heads]
    gamma = [gamma_t[:, GDN_HEADS + h : GDN_HEADS + h + 1] for h in heads]
    k_col = [jnp.broadcast_to(k[h], (GDN_D, GDN_D)).T for h in heads]
    q_col = [jnp.broadcast_to(q[h], (GDN_D, GDN_D)).T for h in heads]
    k_s = [jnp.sum(k_col[h] * s0_ref[0, h], axis=0, keepdims=True) for h in heads]
    q_s = [jnp.sum(q_col[h] * s0_ref[0, h], axis=0, keepdims=True) for h in heads]
    u = [beta[h] * v[h] - (beta[h] * gamma[h]) * k_s[h] for h in heads]
    o = [gamma[h] * q_s[h] + jnp.sum(q[h] * k[h], -1, keepdims=True) * u[h] for h in heads]
    for h in heads:
        so_ref[0, h] = gamma[h] * s0_ref[0, h] + k_col[h] * u[h]
        zh = z[:, h * GDN_D : (h + 1) * GDN_D]
        og_ref[0, :, h * GDN_D : (h + 1) * GDN_D] = (_rmsnorm(o[h], gn_ref[...]) * _silu(zh)).astype(bf16)


def _gdn_step(proj3, off_ba, state_conv, state_gdn, cw, avec, dvec, gn):
    B = proj3.shape[0]
    return pl.pallas_call(
        _gdn_step_kernel,
        grid=(B,),
        in_specs=[
            pl.BlockSpec((1, 1, CONV_CH), lambda b: (b, 0, 0)),
            pl.BlockSpec((1, 1, GDN_W), lambda b: (b, 0, OFF_Z // GDN_W)),
            pl.BlockSpec((1, 1, LANES), lambda b: (b, 0, off_ba // LANES)),
            pl.BlockSpec((1, GDN_CONV - 1, CONV_CH), lambda b: (b, 0, 0)),
            pl.BlockSpec((1, GDN_HEADS, GDN_D, GDN_D), lambda b: (b, 0, 0, 0)),
            pl.BlockSpec((GDN_CONV, CONV_CH), lambda b: (0, 0)),
            pl.BlockSpec((1, LANES), lambda b: (0, 0)),
            pl.BlockSpec((1, LANES), lambda b: (0, 0)),
            pl.BlockSpec((1, GDN_D), lambda b: (0, 0)),
        ],
        out_specs=[
            pl.BlockSpec((1, 1, GDN_W), lambda b: (b, 0, 0)),
            pl.BlockSpec((1, GDN_HEADS, GDN_D, GDN_D), lambda b: (b, 0, 0, 0)),
        ],
        out_shape=[
            jax.ShapeDtypeStruct((B, 1, GDN_W), bf16),
            jax.ShapeDtypeStruct((B, GDN_HEADS, GDN_D, GDN_D), f32),
        ],
        compiler_params=_cparams(("parallel",)),
        name="gdn_step",
    )(proj3, proj3, proj3, state_conv, state_gdn, cw, avec, dvec, gn)


def _rope(x, cos, sin_signed):
    width = x.shape[-1]
    lane = lax.broadcasted_iota(jnp.int32, x.shape, x.ndim - 1)
    first_half = (lane % SWA_HD) < (SWA_HD // 2)
    rot = jnp.where(first_half, pltpu.roll(x, width - SWA_HD // 2, x.ndim - 1), pltpu.roll(x, SWA_HD // 2, x.ndim - 1))
    return x * cos + rot * sin_signed


def _head_halves(x2, head_parity, lane):
    swapped = pltpu.roll(x2, SWA_HD, 1)
    lo_src, hi_src = (x2, swapped) if head_parity == 0 else (swapped, x2)
    return jnp.where(lane < SWA_HD, lo_src, 0.0), jnp.where(lane >= SWA_HD, hi_src, 0.0)


def _swa_prompt_kernel(sinks_ref, q_ref, kv_ref, cos_ref, sin_ref, os_ref, kc_ref, kprev_ref, vprev_ref):
    n = pl.program_id(1)

    @pl.when(n == 0)
    def _():
        kprev_ref[...] = jnp.zeros_like(kprev_ref)
        vprev_ref[...] = jnp.zeros_like(vprev_ref)

    cos = cos_ref[...]
    sin = sin_ref[...]
    kv = kv_ref[...]
    k_cur = _rope(kv[:, :SWA_KV_W], cos, sin)
    v_cur = kv[:, SWA_KV_W:]
    kc_ref[0] = k_cur
    k_prev = kprev_ref[...]
    v_prev = vprev_ref[...]
    kprev_ref[...] = k_cur
    vprev_ref[...] = v_cur

    row = lax.broadcasted_iota(jnp.int32, (WINDOW, WINDOW), 0)
    col = lax.broadcasted_iota(jnp.int32, (WINDOW, WINDOW), 1)
    own = col <= row
    prev_bias = jnp.where(n > 0, 0.0, -jnp.inf)
    lane = lax.broadcasted_iota(jnp.int32, (WINDOW, LANES), 1)
    scale = SWA_HD ** -0.5
    group = SWA_HEADS // SWA_KV_HEADS
    k_own, k_pre, v_own, v_pre, q2 = [], [], [], [], []
    for h in range(SWA_KV_HEADS):
        pair = slice((h // 2) * LANES, (h // 2 + 1) * LANES)
        k_own.append([a.astype(bf16) for a in _head_halves(k_cur[:, pair], h % 2, lane)])
        k_pre.append([a.astype(bf16) for a in _head_halves(k_prev[:, pair], h % 2, lane)])
        v_own.append([a.astype(bf16) for a in _head_halves(v_cur[:, pair], h % 2, lane)])
        v_pre.append([a.astype(bf16) for a in _head_halves(v_prev[:, pair], h % 2, lane)])
        q_h = _rope(q_ref[:, h * group * SWA_HD : (h + 1) * group * SWA_HD], cos, sin) * scale
        q2.append([q_h[:, j * LANES : (j + 1) * LANES].astype(bf16) for j in range(group // 2)])
    heads = [(h, j, par) for h in range(SWA_KV_HEADS) for j in range(group // 2) for par in range(2)]
    idx = range(len(heads))
    sink = [sinks_ref[h * group + 2 * j + par] for h, j, par in heads]
    s = [jnp.where(own, _dot_nt(q2[h][j], k_own[h][par]), _dot_nt(q2[h][j], k_pre[h][par]) + prev_bias)
         for h, j, par in heads]
    m = [jnp.maximum(jnp.max(s[i], -1, keepdims=True), sink[i]) for i in idx]
    p = [jnp.exp(s[i] - m[i]) for i in idx]
    inv = [1.0 / (jnp.sum(p[i], -1, keepdims=True) + jnp.exp(sink[i] - m[i])) for i in idx]
    o = [(_dot(jnp.where(own, p[i], 0.0).astype(bf16), v_own[h][par])
          + _dot(jnp.where(own, 0.0, p[i]).astype(bf16), v_pre[h][par])) * inv[i] for i, (h, j, par) in enumerate(heads)]
    for i in range(0, len(heads), 2):
        os_ref[:, i * SWA_HD : i * SWA_HD + LANES] = (o[i] + o[i + 1]).astype(bf16)


def _swa_prompt(proj, B, T, off_q, off_kv, cos, sin, sinks):
    assert T % WINDOW == 0 and SWA_KV_W == 2 * LANES
    nb = T // WINDOW
    return pl.pallas_call(
        _swa_prompt_kernel,
        grid=(B, nb),
        in_specs=[
            pl.BlockSpec(memory_space=pltpu.SMEM),
            pl.BlockSpec((WINDOW, SWA_Q_W), lambda b, n: (b * nb + n, off_q // SWA_Q_W)),
            pl.BlockSpec((WINDOW, 2 * SWA_KV_W), lambda b, n: (b * nb + n, off_kv // (2 * SWA_KV_W))),
            pl.BlockSpec((WINDOW, SWA_KV_W), lambda b, n: (n, 0)),
            pl.BlockSpec((WINDOW, SWA_KV_W), lambda b, n: (n, 0)),
        ],
        out_specs=[
            pl.BlockSpec((WINDOW, SWA_Q_W), lambda b, n: (b * nb + n, 0)),
            pl.BlockSpec((1, WINDOW, SWA_KV_W), lambda b, n: (b, 0, 0)),
        ],
        out_shape=[
            jax.ShapeDtypeStruct((B * T, SWA_Q_W), bf16),
            jax.ShapeDtypeStruct((B, WINDOW, SWA_KV_W), f32),
        ],
        scratch_shapes=[pltpu.VMEM((WINDOW, SWA_KV_W), f32), pltpu.VMEM((WINDOW, SWA_KV_W), f32)],
        compiler_params=_cparams(("parallel", "arbitrary")),
        name="swa_prompt",
    )(sinks, proj, proj, cos, sin)


SWA_STEP_BATCH = 8


def _swa_step_kernel(qe_ref, kv_ref, ck_ref, cv_ref, cos_ref, sin_ref, sinks_ref, r_ref, nk_ref, nv_ref):
    cos = cos_ref[...]
    sin = sin_ref[...]
    sink = sinks_ref[...]
    row = lax.broadcasted_iota(jnp.int32, (WINDOW, SWA_KV_W), 0)
    scale = SWA_HD ** -0.5
    seqs = range(qe_ref.shape[0])
    kv = [kv_ref[i] for i in seqs]
    k_new = [_rope(kv[i][:, :SWA_KV_W], cos, sin) for i in seqs]
    keys = [jnp.where(row == WINDOW - 1, k_new[i], pltpu.roll(ck_ref[i], WINDOW - 1, 0)) for i in seqs]
    vals = [jnp.where(row == WINDOW - 1, kv[i][:, SWA_KV_W:], pltpu.roll(cv_ref[i], WINDOW - 1, 0)) for i in seqs]
    for i in seqs:
        nk_ref[i] = keys[i]
        nv_ref[i] = vals[i]
    q = [_rope(qe_ref[i], cos, sin) for i in seqs]
    s = [_dot_nt(q[i].astype(bf16), keys[i].astype(bf16)) * scale for i in seqs]
    m = [jnp.maximum(jnp.max(s[i], -1, keepdims=True), sink) for i in seqs]
    p = [jnp.exp(s[i] - m[i]) for i in seqs]
    denom = [jnp.sum(p[i], -1, keepdims=True) + jnp.exp(sink - m[i]) for i in seqs]
    for i in seqs:
        r_ref[i] = _dot((p[i] / denom[i]).astype(bf16), vals[i].astype(bf16))


def _swa_step(q_exp, proj3, off_kv, cache_k, cache_v, cos, sin, sinks_col):
    B = q_exp.shape[0]
    bb = math.gcd(B, SWA_STEP_BATCH)
    assert cache_k.shape[1] == WINDOW
    return pl.pallas_call(
        _swa_step_kernel,
        grid=(B // bb,),
        in_specs=[
            pl.BlockSpec((bb, SWA_HEADS, SWA_KV_W), lambda i: (i, 0, 0)),
            pl.BlockSpec((bb, 1, 2 * SWA_KV_W), lambda i: (i, 0, off_kv // (2 * SWA_KV_W))),
            pl.BlockSpec((bb, WINDOW, SWA_KV_W), lambda i: (i, 0, 0)),
            pl.BlockSpec((bb, WINDOW, SWA_KV_W), lambda i: (i, 0, 0)),
            pl.BlockSpec((1, SWA_KV_W), lambda i: (0, 0)),
            pl.BlockSpec((1, SWA_KV_W), lambda i: (0, 0)),
            pl.BlockSpec((SWA_HEADS, 1), lambda i: (0, 0)),
        ],
        out_specs=[
            pl.BlockSpec((bb, SWA_HEADS, SWA_KV_W), lambda i: (i, 0, 0)),
            pl.BlockSpec((bb, WINDOW, SWA_KV_W), lambda i: (i, 0, 0)),
            pl.BlockSpec((bb, WINDOW, SWA_KV_W), lambda i: (i, 0, 0)),
        ],
        out_shape=[
            jax.ShapeDtypeStruct((B, SWA_HEADS, SWA_KV_W), f32),
            jax.ShapeDtypeStruct((B, WINDOW, SWA_KV_W), f32),
            jax.ShapeDtypeStruct((B, WINDOW, SWA_KV_W), f32),
        ],
        compiler_params=_cparams(("parallel",)),
        name="swa_step",
    )(q_exp, proj3, cache_k, cache_v, cos, sin, sinks_col)


def _merge_core(og_ref, os_ref, gg_ref, gs_ref, x_ref, wg_ref, ws_ref, wo_ref, gq_ref, wq_ref):
    p_gdn = _dot(og_ref[...], wg_ref[...])
    p_swa = _dot(os_ref[...], ws_ref[...])
    merged = jax.nn.sigmoid(gg_ref[...]) * p_gdn + jax.nn.sigmoid(gs_ref[...]) * p_swa
    x_new = x_ref[...] + _dot(merged.astype(bf16), wo_ref[...])
    return x_new, _dot(_rmsnorm(x_new, gq_ref[...]).astype(bf16), wq_ref[...])


def _merge_kernel(*refs):
    xo_ref, qm_ref = refs[-2:]
    xo_ref[...], qm_ref[...] = _merge_core(*refs[:-2])


def _merge_mem_kernel(*refs):
    mk_ref, mv_ref, wmo_ref, xo_ref = refs[-4:]
    x_new, q = _merge_core(*refs[:-4])
    q = q.astype(bf16)
    scale = MEM_HD ** -0.5
    heads = range(MEM_HEADS)
    cols = lambda h: slice(h * MEM_HD, (h + 1) * MEM_HD)
    s = [_dot_nt(q[:, cols(h)], mk_ref[0, :, cols(h)]) * scale for h in heads]
    p = [jnp.exp(s[h] - jnp.max(s[h], -1, keepdims=True)) for h in heads]
    p = [p[h] / jnp.sum(p[h], -1, keepdims=True) for h in heads]
    o = [_dot(p[h].astype(bf16), mv_ref[0, :, cols(h)]) for h in heads]
    xo_ref[...] = x_new + _dot(jnp.concatenate(o, axis=-1).astype(bf16), wmo_ref[...])


def _merge(og, os_, proj, off_gg, x, wg, ws, wo, gq, wq, mem=None):
    M, D = x.shape
    tm = min(256, M)
    assert M % tm == 0 and off_gg % D == 0
    const = lambda shape: pl.BlockSpec(shape, lambda i: (0, 0), pipeline_mode=pl.Buffered(1))
    in_specs = [
        pl.BlockSpec((tm, GDN_W), lambda i: (i, 0)),
        pl.BlockSpec((tm, SWA_Q_W), lambda i: (i, 0)),
        pl.BlockSpec((tm, D), lambda i: (i, off_gg // D)),
        pl.BlockSpec((tm, D), lambda i: (i, off_gg // D + 1)),
        pl.BlockSpec((tm, D), lambda i: (i, 0)),
        const((GDN_W, D)),
        const((SWA_Q_W, D)),
        const((D, D)),
        const((1, D)),
        const((D, MEM_W)),
    ]
    args = [og, os_, proj, proj, x, wg, ws, wo, gq, wq]
    x_spec = pl.BlockSpec((tm, D), lambda i: (i, 0))
    x_shape = jax.ShapeDtypeStruct((M, D), f32)
    if mem is None:
        return pl.pallas_call(
            _merge_kernel,
            grid=(M // tm,),
            in_specs=in_specs,
            out_specs=[x_spec, pl.BlockSpec((tm, MEM_W), lambda i: (i, 0))],
            out_shape=[x_shape, jax.ShapeDtypeStruct((M, MEM_W), f32)],
            compiler_params=_cparams(("parallel",)),
            name="merge",
        )(*args)
    mem_k, mem_v, wmo, seq_rows = mem
    assert seq_rows % tm == 0
    mt = mem_k.shape[1]
    mem_spec = pl.BlockSpec((1, mt, MEM_W), lambda i: (i // (seq_rows // tm), 0, 0))
    return pl.pallas_call(
        _merge_mem_kernel,
        grid=(M // tm,),
        in_specs=in_specs + [mem_spec, mem_spec, const((MEM_W, D))],
        out_specs=x_spec,
        out_shape=x_shape,
        compiler_params=_cparams(("parallel",)),
        name="merge_mem",
    )(*args, mem_k, mem_v, wmo)


MEM_STEP_BATCH = 4


def _mem_attn_step_kernel(q_ref, k_ref, v_ref, o_ref):
    mt2 = k_ref.shape[1] // SUBLANES
    for i in range(q_ref.shape[0]):
        q8 = q_ref[i] * (MEM_HD ** -0.5)
        s = jnp.sum(k_ref[i].reshape(mt2, SUBLANES, MEM_HD) * q8, axis=-1, keepdims=True)
        m = jnp.max(s, axis=0)
        m = jnp.maximum(m, pltpu.roll(m, MEM_HEADS, 0))
        p = jnp.exp(s - m)
        l = jnp.sum(p, axis=0)
        l = l + pltpu.roll(l, MEM_HEADS, 0)
        o = jnp.sum(p * v_ref[i].reshape(mt2, SUBLANES, MEM_HD), axis=0)
        o_ref[i] = (o + pltpu.roll(o, MEM_HEADS, 0)) / l


def _mem_attn_step(q8, mem_k, mem_v):
    B, rows, _ = mem_k.shape
    assert 2 * MEM_HEADS == SUBLANES and rows % SUBLANES == 0
    bb = math.gcd(B, MEM_STEP_BATCH)
    return pl.pallas_call(
        _mem_attn_step_kernel,
        grid=(B // bb,),
        in_specs=[
            pl.BlockSpec((bb, SUBLANES, MEM_HD), lambda i: (i, 0, 0)),
            pl.BlockSpec((bb, rows, MEM_HD), lambda i: (i, 0, 0)),
            pl.BlockSpec((bb, rows, MEM_HD), lambda i: (i, 0, 0)),
        ],
        out_specs=pl.BlockSpec((bb, SUBLANES, MEM_HD), lambda i: (i, 0, 0)),
        out_shape=jax.ShapeDtypeStruct((B, SUBLANES, MEM_HD), f32),
        compiler_params=_cparams(("parallel",)),
        name="mem_attn_step",
    )(q8, mem_k, mem_v)


def _proj_residual_kernel(a_ref, w_ref, x_ref, o_ref):
    o_ref[...] = x_ref[...] + _dot(a_ref[...].astype(bf16), w_ref[...])


def _proj_residual(a, w, x):
    M, D = x.shape
    return pl.pallas_call(
        _proj_residual_kernel,
        out_shape=jax.ShapeDtypeStruct((M, D), f32),
        compiler_params=pltpu.CompilerParams(vmem_limit_bytes=VMEM_LIMIT),
        name="proj_residual",
    )(a, w, x)


def _reorder_w_in_kernel(w_ref, o_ref, *, pieces, pad):
    w = w_ref[...]
    parts = [w[:, lo:hi].astype(bf16) for lo, hi in pieces]
    o_ref[...] = jnp.concatenate(parts + [jnp.zeros((w.shape[0], pad), bf16)], axis=1)


def _reorder_w_in(w, layer, pieces, width):
    _, D, d_in = w.shape
    rows = math.gcd(D, 64)
    pad = width - sum(hi - lo for lo, hi in pieces)
    return pl.pallas_call(
        functools.partial(_reorder_w_in_kernel, pieces=pieces, pad=pad),
        grid=(D // rows,),
        in_specs=[pl.BlockSpec((None, rows, d_in), lambda i: (layer, i, 0))],
        out_specs=pl.BlockSpec((rows, width), lambda i: (i, 0)),
        out_shape=jax.ShapeDtypeStruct((D, width), bf16),
        compiler_params=_cparams(("parallel",)),
        name="reorder_w_in",
    )(w)


def _rope_tables(pos):
    half = SWA_HD // 2
    inv_freq = ROPE_THETA ** (-jnp.arange(half, dtype=f32) / half)
    ang = pos.astype(f32)[:, None] * inv_freq[None, :]
    cos = jnp.cos(ang)
    sin = jnp.sin(ang)
    reps = SWA_KV_W // SWA_HD
    return jnp.tile(jnp.concatenate([cos, cos], -1), (1, reps)), jnp.tile(jnp.concatenate([-sin, sin], -1), (1, reps))


def kernel(x_prompt, x_sample, state_gdn, state_conv, cache_swa_k, cache_swa_v, cache_mem_k, cache_mem_v, mem_prompt, norm_ffn1, ffn1_w1, ffn1_w3, ffn1_w2, norm_mix, w_in, conv_w, gdn_A_log, gdn_dt_bias, gdn_norm, swa_sinks, w_br_gdn, w_br_swa, w_out, norm_mem_q, norm_mem_kv, w_mem_q, w_mem_k, w_mem_v, w_mem_o, norm_ffn2, ffn2_w1, ffn2_w3, ffn2_w2, norm_final):
    Bp, Tp, D = x_prompt.shape
    Bs, Ts, _ = x_sample.shape
    assert Ts == 1
    depth = norm_ffn1.shape[0]
    n_mem = mem_prompt.shape[1]
    group = SWA_HEADS // SWA_KV_HEADS

    off_gs = OFF_GG + D
    off_q = off_gs + D
    off_kv = off_q + SWA_Q_W
    off_ba = off_kv + 2 * SWA_KV_W
    d_in_pad = -(-(off_ba + LANES) // PROJ_TN) * PROJ_TN
    o_b = CONV_CH + GDN_W
    o_q = o_b + 2 * GDN_HEADS
    o_gg = o_q + SWA_Q_W + 2 * SWA_KV_W

    cos_p, sin_p = _rope_tables(jnp.arange(Tp, dtype=jnp.int32))
    cos_s, sin_s = _rope_tables(PAST_LEN + jnp.arange(Ts, dtype=jnp.int32))
    eye_kv = jnp.eye(SWA_KV_HEADS, dtype=f32)
    row = lambda v: v.reshape(1, -1)

    hp = x_prompt.reshape(Bp * Tp, D)
    hs = x_sample.reshape(Bs, D)
    outs = [[] for _ in range(10)]
    for l in range(depth):
        w_in_r = _reorder_w_in(w_in, l, ((0, o_b), (o_gg, w_in.shape[2]), (o_q, o_gg), (o_b, o_q)), d_in_pad)
        ffn1 = (row(norm_ffn1[l]), ffn1_w1[l].astype(bf16), ffn1_w3[l].astype(bf16), ffn1_w2[l].astype(bf16))
        last = l == depth - 1
        gfin = row(norm_final)
        avec = jnp.zeros((1, LANES), f32).at[0, GDN_HEADS : 2 * GDN_HEADS].set(gdn_A_log[l])
        dvec = jnp.zeros((1, LANES), f32).at[0, GDN_HEADS : 2 * GDN_HEADS].set(gdn_dt_bias[l])
        gdn_common = (conv_w[l], avec, dvec, row(gdn_norm[l]))
        tn_in = PROJ_TN

        x1 = _ffn(hp, *ffn1, gfin, False)
        proj = _norm_matmul(x1, row(norm_mix[l]), w_in_r, tn_in)
        later = (ffn2_w1[l], ffn2_w3[l], ffn2_w2[l], w_br_gdn[l], w_br_swa[l], w_out[l], w_mem_q[l],
                 w_mem_k[l], w_mem_v[l], w_mem_o[l])
        og, s_new, later = _gdn_prompt(proj, Bp, Tp, off_ba, *gdn_common, later)
        ffn2 = (row(norm_ffn2[l]), *later[0:3])
        merge_w = (*later[3:6], row(norm_mem_q[l]), later[6])
        wmk, wmv, wmo = later[7:10]
        os_, kc = _swa_prompt(proj, Bp, Tp, off_q, off_kv, cos_p, sin_p, swa_sinks[l])
        mem_x = mem_prompt.reshape(Bp * n_mem, D)
        mk = _norm_matmul(mem_x, row(norm_mem_kv[l]), wmk, MEM_W)
        mv = _norm_matmul(mem_x, row(norm_mem_kv[l]), wmv, MEM_W)
        mem = (mk.reshape(Bp, n_mem, MEM_W).astype(bf16), mv.reshape(Bp, n_mem, MEM_W).astype(bf16), wmo, Tp)
        x3 = _merge(og, os_, proj, OFF_GG, x1, *merge_w, mem=mem)
        hp = _ffn(x3, *ffn2, gfin, last)
        proj_b = proj.reshape(Bp, Tp, d_in_pad)
        outs[0].append(s_new)
        outs[1].append(proj_b[:, Tp - (GDN_CONV - 1) :, :CONV_CH])
        outs[2].append(kc.reshape(Bp, WINDOW, SWA_KV_HEADS, SWA_HD))
        outs[3].append(proj_b[:, Tp - WINDOW :, off_kv + SWA_KV_W : off_kv + 2 * SWA_KV_W].reshape(Bp, WINDOW, SWA_KV_HEADS, SWA_HD))
        outs[4].append(mk.reshape(Bp, n_mem, MEM_HEADS, MEM_HD))
        outs[5].append(mv.reshape(Bp, n_mem, MEM_HEADS, MEM_HD))

        x1 = _ffn(hs, *ffn1, gfin, False)
        proj = _norm_matmul(x1, row(norm_mix[l]), w_in_r, tn_in)
        proj3 = proj.reshape(Bs, 1, d_in_pad)
        og, s_new = _gdn_step(proj3, off_ba, state_conv[l], state_gdn[l], *gdn_common)
        q_raw = proj[:, off_q : off_q + SWA_Q_W].reshape(Bs, SWA_KV_HEADS, group, 1, SWA_HD)
        q_exp = (q_raw * eye_kv[None, :, None, :, None]).reshape(Bs, SWA_HEADS, SWA_KV_W)
        ck = cache_swa_k[l].reshape(Bs, WINDOW, SWA_KV_W)
        cv = cache_swa_v[l].reshape(Bs, WINDOW, SWA_KV_W)
        r, nk, nv = _swa_step(q_exp, proj3, off_kv, ck, cv, cos_s, sin_s, swa_sinks[l].reshape(SWA_HEADS, 1))
        r5 = r.reshape(Bs, SWA_KV_HEADS, group, SWA_KV_HEADS, SWA_HD)
        kvh = jnp.arange(SWA_KV_HEADS)
        os_ = jnp.transpose(r5[:, kvh, :, kvh, :], (1, 0, 2, 3)).reshape(Bs, SWA_Q_W).astype(bf16)
        x2, qm = _merge(og.reshape(Bs, GDN_W), os_, proj, OFF_GG, x1, *merge_w)
        q8 = jnp.tile(qm.reshape(Bs, MEM_HEADS, MEM_HD), (1, 2, 1))
        om = _mem_attn_step(q8, cache_mem_k[l].reshape(Bs, n_mem * MEM_HEADS, MEM_HD),
                            cache_mem_v[l].reshape(Bs, n_mem * MEM_HEADS, MEM_HD))
        x3 = _proj_residual(om[:, :MEM_HEADS].reshape(Bs, MEM_W), wmo, x2)
        hs = _ffn(x3, *ffn2, gfin, last)
        outs[6].append(s_new)
        outs[7].append(jnp.concatenate([state_conv[l][:, 1:], proj3[:, :, :CONV_CH]], axis=1))
        outs[8].append(nk.reshape(Bs, WINDOW, SWA_KV_HEADS, SWA_HD))
        outs[9].append(nv.reshape(Bs, WINDOW, SWA_KV_HEADS, SWA_HD))

    return (hp.reshape(Bp, Tp, D), hs.reshape(Bs, Ts, D), *(jnp.stack(o) for o in outs))
```

```python
import functools
import math

import jax
import jax.numpy as jnp
from jax import lax
from jax.experimental import pallas as pl
from jax.experimental.pallas import tpu as pltpu

f32 = jnp.float32
bf16 = jnp.bfloat16

PAST_LEN = 16384
GDN_HEADS = 8
GDN_D = 128
GDN_CONV = 4
SWA_HEADS = 16
SWA_KV_HEADS = 4
SWA_HD = 64
WINDOW = 128
ROPE_THETA = 10000.0
MEM_HEADS = 4
MEM_HD = 128
EPS = 1e-6
L2_EPS = 1e-6

LANES = 128
SUBLANES = 8
CHUNK = 128
GDN_STEP_CHUNKS = 2
FFN_SLAB = 64
PROJ_TN = 768
VMEM_LIMIT = 58 * 1024 * 1024

GDN_W = GDN_HEADS * GDN_D
CONV_CH = 3 * GDN_W
SWA_Q_W = SWA_HEADS * SWA_HD
SWA_KV_W = SWA_KV_HEADS * SWA_HD
MEM_W = MEM_HEADS * MEM_HD

OFF_Z = CONV_CH
OFF_GG = OFF_Z + GDN_W


def _cparams(semantics):
    return pltpu.CompilerParams(dimension_semantics=semantics, vmem_limit_bytes=VMEM_LIMIT)


def _dot(a, b):
    return jnp.dot(a, b, preferred_element_type=f32)


def _dot_nt(a, b):
    return lax.dot_general(a, b, (((1,), (1,)), ((), ())), preferred_element_type=f32)


def _rmsnorm(x, g):
    return x * lax.rsqrt(jnp.mean(x * x, -1, keepdims=True) + EPS) * g


def _silu(x):
    return x * jax.nn.sigmoid(x)


def _softplus(x):
    return jnp.maximum(x, 0.0) + jnp.log1p(jnp.exp(-jnp.abs(x)))


def _ffn_kernel(x_ref, g_ref, w1_ref, w3_ref, w2_ref, gf_ref, o_ref, h_ref, *, final_norm):
    j = pl.program_id(1)
    slab = min(FFN_SLAB, x_ref.shape[0])
    n_slabs = x_ref.shape[0] // slab

    def over_slabs(body):
        def step(s, carry):
            body(pl.ds(pl.multiple_of(s * slab, slab), slab))
            return carry
        lax.fori_loop(0, n_slabs, step, 0, unroll=min(4, n_slabs))

    @pl.when(j == 0)
    def _():
        def prologue(rows):
            h_ref[rows, :] = _rmsnorm(x_ref[rows, :], g_ref[...]).astype(bf16)
            o_ref[rows, :] = jnp.zeros((slab, o_ref.shape[1]), f32)
        over_slabs(prologue)

    h = h_ref[...]
    a = _dot(h, w1_ref[...])
    b = _dot(h, w3_ref[...])
    o_ref[...] += _dot((_silu(a) * b).astype(bf16), w2_ref[...])

    @pl.when(j == pl.num_programs(1) - 1)
    def _():
        def epilogue(rows):
            y = x_ref[rows, :] + 0.5 * o_ref[rows, :]
            if final_norm:
                y = _rmsnorm(y, gf_ref[...])
            o_ref[rows, :] = y
        over_slabs(epilogue)


def _ffn(x, g, w1, w3, w2, gf, final_norm):
    M, D = x.shape
    F = w1.shape[1]
    tm = min(1024, M)
    tf = 512
    assert M % tm == 0 and F % tf == 0
    return pl.pallas_call(
        functools.partial(_ffn_kernel, final_norm=final_norm),
        grid=(M // tm, F // tf),
        in_specs=[
            pl.BlockSpec((tm, D), lambda i, j: (i, 0)),
            pl.BlockSpec((1, D), lambda i, j: (0, 0)),
            pl.BlockSpec((D, tf), lambda i, j: (0, j)),
            pl.BlockSpec((D, tf), lambda i, j: (0, j)),
            pl.BlockSpec((tf, D), lambda i, j: (j, 0)),
            pl.BlockSpec((1, D), lambda i, j: (0, 0)),
        ],
        out_specs=pl.BlockSpec((tm, D), lambda i, j: (i, 0)),
        out_shape=jax.ShapeDtypeStruct((M, D), f32),
        scratch_shapes=[pltpu.VMEM((tm, D), bf16)],
        compiler_params=_cparams(("parallel", "arbitrary")),
        name="ffn",
    )(x, g, w1, w3, w2, gf)


def _norm_matmul_kernel(x_ref, g_ref, w_ref, o_ref, h_ref, *, w_transposed):
    @pl.when(pl.program_id(1) == 0)
    def _():
        h_ref[...] = _rmsnorm(x_ref[...], g_ref[...]).astype(bf16)

    o_ref[...] = (_dot_nt if w_transposed else _dot)(h_ref[...], w_ref[...])


def _norm_matmul(x, g, w, tn, w_transposed=False):
    M, D = x.shape
    N = w.shape[0] if w_transposed else w.shape[1]
    tm = min(1024, M)
    assert M % tm == 0 and N % tn == 0
    w_spec = pl.BlockSpec((tn, D), lambda i, j: (j, 0)) if w_transposed else pl.BlockSpec((D, tn), lambda i, j: (0, j))
    return pl.pallas_call(
        functools.partial(_norm_matmul_kernel, w_transposed=w_transposed),
        grid=(M // tm, N // tn),
        in_specs=[
            pl.BlockSpec((tm, D), lambda i, j: (i, 0)),
            pl.BlockSpec((1, D), lambda i, j: (0, 0)),
            w_spec,
        ],
        out_specs=pl.BlockSpec((tm, tn), lambda i, j: (i, j)),
        out_shape=jax.ShapeDtypeStruct((M, N), f32),
        scratch_shapes=[pltpu.VMEM((tm, D), bf16)],
        compiler_params=_cparams(("parallel", "arbitrary")),
        name="norm_matmul",
    )(x, g, w)


def _bdot(a, b):
    return _dot(a.astype(bf16), b.astype(bf16))


def _unit_lower_inverse(nmats, row, col):
    eye = jnp.where(row == col, 1.0, 0.0).astype(f32)

    def same_block(size):
        return (row // size) == (col // size)

    blk = same_block(SUBLANES)
    a1 = [jnp.where(blk, n, 0.0) for n in nmats]
    a2 = [_bdot(a, a) for a in a1]
    a4 = [_bdot(a, a) for a in a2]
    ts = [_bdot(eye - x1, eye + x2) for x1, x2 in zip(a1, a2)]
    ts = [_bdot(t, eye + x4) for t, x4 in zip(ts, a4)]
    size = SUBLANES
    while size < CHUNK:
        nxt = same_block(2 * size)
        sel = jnp.logical_and(nxt, jnp.logical_not(blk))
        tbs = [t.astype(bf16) for t in ts]
        xs = [_dot(jnp.where(sel, n, 0.0).astype(bf16), tb) for n, tb in zip(nmats, tbs)]
        ts = [t - _dot(tb, x.astype(bf16)) for t, tb, x in zip(ts, tbs, xs)]
        blk = nxt
        size *= 2
    return ts


def _conv_silu_slab(x_ref, top_ref, w_ref, sl):
    rows = x_ref.shape[0]
    y = None
    for tap in range(GDN_CONV):
        back = GDN_CONV - 1 - tap
        window = jnp.concatenate(
            [top_ref[SUBLANES - back : 2 * SUBLANES - back, sl], x_ref[SUBLANES - back : rows - back, sl]], axis=0)
        term = window * w_ref[tap : tap + 1, sl]
        y = term if y is None else y + term
    return _silu(y)


def _delta_rule_prepare(pairs, gcum, gcum_t, beta_t, row, col, q_ref, k_ref, v_ref):
    n = range(len(pairs))
    incl = row >= col
    strict = row > col
    rows_of = lambda ci: slice(ci * CHUNK, (ci + 1) * CHUNK)
    gc = [gcum[rows_of(ci), GDN_HEADS + h : GDN_HEADS + h + 1] for ci, h in pairs]
    gr = [gcum_t[ci][GDN_HEADS + h : GDN_HEADS + h + 1, :] for ci, h in pairs]
    beta = [beta_t[rows_of(ci), h : h + 1] for ci, h in pairs]
    k = [k_ref[h, rows_of(ci), :] for ci, h in pairs]
    kb = [x.astype(bf16) for x in k]
    kk = [_dot_nt(kb[i], kb[i]) for i in n]
    qk = [_dot_nt(q_ref[h, rows_of(ci), :].astype(bf16), kb[i]) for i, (ci, h) in enumerate(pairs)]
    e = [jnp.exp(jnp.where(incl, gc[i] - gr[i], 0.0)) for i in n]
    nmat = [beta[i] * kk[i] * jnp.where(strict, e[i], 0.0) for i in n]
    qkd = [(qk[i] * jnp.where(incl, e[i], 0.0)).astype(bf16) for i in n]
    tinv = _unit_lower_inverse(nmat, row, col)
    gamma = [jnp.exp(gc[i]) for i in n]
    rhs = [jnp.concatenate([beta[i] * v_ref[h, rows_of(ci), :], (beta[i] * gamma[i]) * k[i]], axis=-1).astype(bf16)
           for i, (ci, h) in enumerate(pairs)]
    sol = [_dot(tinv[i].astype(bf16), rhs[i]) for i in n]
    return gc, gamma, sol, qkd


def _delta_rule_apply(ci, prepared, q_ref, k_ref, z_ref, gn_ref, s_ref, og_ref):
    gc, gamma, sol, qkd = prepared
    heads = range(GDN_HEADS)
    rows = slice(ci * CHUNK, (ci + 1) * CHUNK)
    sb = [s_ref[h].astype(bf16) for h in heads]
    ub = [(sol[h][:, :GDN_D] - _dot(sol[h][:, GDN_D:].astype(bf16), sb[h])).astype(bf16) for h in heads]
    o = [_dot((q_ref[h, rows, :] * gamma[h]).astype(bf16), sb[h]) + _dot(qkd[h], ub[h]) for h in heads]
    for h in heads:
        g_last = gc[h][CHUNK - 1 : CHUNK, :]
        k_end = k_ref[h, rows, :] * jnp.exp(g_last - gc[h])
        s_ref[h] = jnp.exp(g_last) * s_ref[h] + _dot(k_end.T.astype(bf16), ub[h])
        zh = z_ref[rows, h * GDN_D : (h + 1) * GDN_D]
        og_ref[rows, h * GDN_D : (h + 1) * GDN_D] = (_rmsnorm(o[h], gn_ref[...]) * _silu(zh)).astype(bf16)


def _gdn_prompt_kernel(*refs, n_cast):
    qkv_ref, z_ref, ba_ref, cw_ref, av_ref, dv_ref, gn_ref = refs[:7]
    cast_in = refs[7 : 7 + n_cast]
    og_ref, so_ref = refs[7 + n_cast : 9 + n_cast]
    cast_out = refs[9 + n_cast : 9 + 2 * n_cast]
    s_ref, carry_ref, q_ref, k_ref, v_ref = refs[9 + 2 * n_cast :]
    for src, dst in zip(cast_in, cast_out):
        dst[...] = src[...].astype(bf16)

    c = pl.program_id(1)
    rows = qkv_ref.shape[0]
    n_chunks = rows // CHUNK

    @pl.when(c == 0)
    def _():
        s_ref[...] = jnp.zeros_like(s_ref)
        carry_ref[0:SUBLANES, :] = jnp.zeros((SUBLANES, CONV_CH), f32)

    carry_ref[SUBLANES : 2 * SUBLANES, :] = qkv_ref[0:SUBLANES, :]
    for j in range(3 * GDN_HEADS):
        sl = slice(j * LANES, (j + 1) * LANES)
        y = _conv_silu_slab(qkv_ref, carry_ref, cw_ref, sl)
        h = j % GDN_HEADS
        if j < 2 * GDN_HEADS:
            y = y * lax.rsqrt(jnp.sum(y * y, -1, keepdims=True) + L2_EPS)
            if j < GDN_HEADS:
                q_ref[h] = y * (GDN_D ** -0.5)
            else:
                k_ref[h] = y
        else:
            v_ref[h] = y
    carry_ref[0:SUBLANES, :] = qkv_ref[rows - SUBLANES : rows, :]

    ba = ba_ref[...]
    beta_t = jax.nn.sigmoid(ba)
    g_t = -jnp.exp(av_ref[...]) * _softplus(ba + dv_ref[...])
    row_in_chunk = lax.broadcasted_iota(jnp.int32, (rows, LANES), 0) % CHUNK
    gcum = g_t
    shift = 1
    while shift < CHUNK:
        gcum = gcum + jnp.where(row_in_chunk >= shift, pltpu.roll(gcum, shift, 0), 0.0)
        shift *= 2
    gcum_t = [gcum[ci * CHUNK : (ci + 1) * CHUNK, :].T for ci in range(n_chunks)]

    row = lax.broadcasted_iota(jnp.int32, (CHUNK, CHUNK), 0)
    col = lax.broadcasted_iota(jnp.int32, (CHUNK, CHUNK), 1)
    pairs = [(ci, h) for ci in range(n_chunks) for h in range(GDN_HEADS)]
    gc, gamma, sol, qkd = _delta_rule_prepare(pairs, gcum, gcum_t, beta_t, row, col, q_ref, k_ref, v_ref)
    for ci in range(n_chunks):
        mine = slice(ci * GDN_HEADS, (ci + 1) * GDN_HEADS)
        _delta_rule_apply(ci, (gc[mine], gamma[mine], sol[mine], qkd[mine]), q_ref, k_ref, z_ref, gn_ref, s_ref, og_ref)

    @pl.when(c == pl.num_programs(1) - 1)
    def _():
        so_ref[0] = s_ref[...]


def _cast_block(shape, steps):
    R, C = shape
    for col_blocks in (1, 2, 4, 8, 16):
        row_blocks = steps // col_blocks
        if (steps % col_blocks == 0 and R % row_blocks == 0 and C % col_blocks == 0
                and (R // row_blocks) % (2 * SUBLANES) == 0 and (C // col_blocks) % LANES == 0):
            return (R // row_blocks, C // col_blocks), col_blocks
    return None


def _gdn_prompt(proj, B, T, off_ba, cw, avec, dvec, gn, weights):
    rows = GDN_STEP_CHUNKS * CHUNK
    assert T % rows == 0
    ns = T // rows
    step = lambda b, c: b * ns + c
    plans = [_cast_block(w.shape, B * ns) for w in weights]
    riders = [w for w, p in zip(weights, plans) if p is not None]
    cast_specs = [pl.BlockSpec(blk, functools.partial(lambda b, c, cb: (step(b, c) // cb, step(b, c) % cb), cb=cb))
                  for blk, cb in (p for p in plans if p is not None)]
    head_scratch = pltpu.VMEM((GDN_HEADS, rows, GDN_D), f32)
    out = pl.pallas_call(
        functools.partial(_gdn_prompt_kernel, n_cast=len(riders)),
        grid=(B, ns),
        in_specs=[
            pl.BlockSpec((rows, CONV_CH), lambda b, c: (step(b, c), 0)),
            pl.BlockSpec((rows, GDN_W), lambda b, c: (step(b, c), OFF_Z // GDN_W)),
            pl.BlockSpec((rows, LANES), lambda b, c: (step(b, c), off_ba // LANES)),
            pl.BlockSpec((GDN_CONV, CONV_CH), lambda b, c: (0, 0)),
            pl.BlockSpec((1, LANES), lambda b, c: (0, 0)),
            pl.BlockSpec((1, LANES), lambda b, c: (0, 0)),
            pl.BlockSpec((1, GDN_D), lambda b, c: (0, 0)),
        ] + cast_specs,
        out_specs=[
            pl.BlockSpec((rows, GDN_W), lambda b, c: (step(b, c), 0)),
            pl.BlockSpec((1, GDN_HEADS, GDN_D, GDN_D), lambda b, c: (b, 0, 0, 0)),
        ] + cast_specs,
        out_shape=[
            jax.ShapeDtypeStruct((B * T, GDN_W), bf16),
            jax.ShapeDtypeStruct((B, GDN_HEADS, GDN_D, GDN_D), f32),
        ] + [jax.ShapeDtypeStruct(w.shape, bf16) for w in riders],
        scratch_shapes=[
            pltpu.VMEM((GDN_HEADS, GDN_D, GDN_D), f32),
            pltpu.VMEM((2 * SUBLANES, CONV_CH), f32),
            head_scratch, head_scratch, head_scratch,
        ],
        compiler_params=_cparams(("arbitrary", "arbitrary")),
        name="gdn_prompt",
    )(proj, proj, proj, cw, avec, dvec, gn, *riders)
    converted = iter(out[2:])
    return out[0], out[1], [next(converted) if p is not None else w.astype(bf16) for w, p in zip(weights, plans)]


def _gdn_step_kernel(qkv_ref, z_ref, ba_ref, sc_ref, s0_ref, cw_ref, av_ref, dv_ref, gn_ref, og_ref, so_ref):
    x_new = qkv_ref[0]
    taps = cw_ref[...]
    y = jnp.sum(sc_ref[0] * taps[0 : GDN_CONV - 1, :], axis=0, keepdims=True) + x_new * taps[GDN_CONV - 1 : GDN_CONV, :]
    y = _silu(y)
    ba = ba_ref[0]
    beta_t = jax.nn.sigmoid(ba)
    gamma_t = jnp.exp(-jnp.exp(av_ref[...]) * _softplus(ba + dv_ref[...]))
    z = z_ref[0]
    heads = range(GDN_HEADS)
    head = lambda base, h: y[:, base + h * GDN_D : base + (h + 1) * GDN_D]
    q = [head(0, h) for h in heads]
    k = [head(GDN_W, h) for h in heads]
    v = [head(2 * GDN_W, h) for h in heads]
    q = [q[h] * lax.rsqrt(jnp.sum(q[h] * q[h], -1, keepdims=True) + L2_EPS) * (GDN_D ** -0.5) for h in heads]
    k = [k[h] * lax.rsqrt(jnp.sum(k[h] * k[h], -1, keepdims=True) + L2_EPS) for h in heads]
    beta = [beta_t[:, h : h + 1] for h in heads]
    gamma = [gamma_t[:, GDN_HEADS + h : GDN_HEADS + h + 1] for h in heads]
    k_col = [jnp.broadcast_to(k[h], (GDN_D, GDN_D)).T for h in heads]
    q_col = [jnp.broadcast_to(q[h], (GDN_D, GDN_D)).T for h in heads]
    k_s = [jnp.sum(k_col[h] * s0_ref[0, h], axis=0, keepdims=True) for h in heads]
    q_s = [jnp.sum(q_col[h] * s0_ref[0, h], axis=0, keepdims=True) for h in heads]
    u = [beta[h] * v[h] - (beta[h] * gamma[h]) * k_s[h] for h in heads]
    o = [gamma[h] * q_s[h] + jnp.sum(q[h] * k[h], -1, keepdims=True) * u[h] for h in heads]
    for h in heads:
        so_ref[0, h] = gamma[h] * s0_ref[0, h] + k_col[h] * u[h]
        zh = z[:, h * GDN_D : (h + 1) * GDN_D]
        og_ref[0, :, h * GDN_D : (h + 1) * GDN_D] = (_rmsnorm(o[h], gn_ref[...]) * _silu(zh)).astype(bf16)


def _gdn_step(proj3, off_ba, state_conv, state_gdn, cw, avec, dvec, gn):
    B = proj3.shape[0]
    return pl.pallas_call(
        _gdn_step_kernel,
        grid=(B,),
        in_specs=[
            pl.BlockSpec((1, 1, CONV_CH), lambda b: (b, 0, 0)),
            pl.BlockSpec((1, 1, GDN_W), lambda b: (b, 0, OFF_Z // GDN_W)),
            pl.BlockSpec((1, 1, LANES), lambda b: (b, 0, off_ba // LANES)),
            pl.BlockSpec((1, GDN_CONV - 1, CONV_CH), lambda b: (b, 0, 0)),
            pl.BlockSpec((1, GDN_HEADS, GDN_D, GDN_D), lambda b: (b, 0, 0, 0)),
            pl.BlockSpec((GDN_CONV, CONV_CH), lambda b: (0, 0)),
            pl.BlockSpec((1, LANES), lambda b: (0, 0)),
            pl.BlockSpec((1, LANES), lambda b: (0, 0)),
            pl.BlockSpec((1, GDN_D), lambda b: (0, 0)),
        ],
        out_specs=[
            pl.BlockSpec((1, 1, GDN_W), lambda b: (b, 0, 0)),
            pl.BlockSpec((1, GDN_HEADS, GDN_D, GDN_D), lambda b: (b, 0, 0, 0)),
        ],
        out_shape=[
            jax.ShapeDtypeStruct((B, 1, GDN_W), bf16),
            jax.ShapeDtypeStruct((B, GDN_HEADS, GDN_D, GDN_D), f32),
        ],
        compiler_params=_cparams(("parallel",)),
        name="gdn_step",
    )(proj3, proj3, proj3, state_conv, state_gdn, cw, avec, dvec, gn)


def _rope(x, cos, sin_signed):
    width = x.shape[-1]
    lane = lax.broadcasted_iota(jnp.int32, x.shape, x.ndim - 1)
    first_half = (lane % SWA_HD) < (SWA_HD // 2)
    rot = jnp.where(first_half, pltpu.roll(x, width - SWA_HD // 2, x.ndim - 1), pltpu.roll(x, SWA_HD // 2, x.ndim - 1))
    return x * cos + rot * sin_signed


def _head_halves(x2, head_parity, lane):
    swapped = pltpu.roll(x2, SWA_HD, 1)
    lo_src, hi_src = (x2, swapped) if head_parity == 0 else (swapped, x2)
    return jnp.where(lane < SWA_HD, lo_src, 0.0), jnp.where(lane >= SWA_HD, hi_src, 0.0)


def _swa_prompt_kernel(sinks_ref, q_ref, kv_ref, cos_ref, sin_ref, os_ref, kc_ref, kprev_ref, vprev_ref):
    n = pl.program_id(1)

    @pl.when(n == 0)
    def _():
        kprev_ref[...] = jnp.zeros_like(kprev_ref)
        vprev_ref[...] = jnp.zeros_like(vprev_ref)

    cos = cos_ref[...]
    sin = sin_ref[...]
    kv = kv_ref[...]
    k_cur = _rope(kv[:, :SWA_KV_W], cos, sin)
    v_cur = kv[:, SWA_KV_W:]
    kc_ref[0] = k_cur
    k_prev = kprev_ref[...]
    v_prev = vprev_ref[...]
    kprev_ref[...] = k_cur
    vprev_ref[...] = v_cur

    row = lax.broadcasted_iota(jnp.int32, (WINDOW, WINDOW), 0)
    col = lax.broadcasted_iota(jnp.int32, (WINDOW, WINDOW), 1)
    own = col <= row
    prev_bias = jnp.where(n > 0, 0.0, -jnp.inf)
    lane = lax.broadcasted_iota(jnp.int32, (WINDOW, LANES), 1)
    scale = SWA_HD ** -0.5
    group = SWA_HEADS // SWA_KV_HEADS
    k_own, k_pre, v_own, v_pre, q2 = [], [], [], [], []
    for h in range(SWA_KV_HEADS):
        pair = slice((h // 2) * LANES, (h // 2 + 1) * LANES)
        k_own.append([a.astype(bf16) for a in _head_halves(k_cur[:, pair], h % 2, lane)])
        k_pre.append([a.astype(bf16) for a in _head_halves(k_prev[:, pair], h % 2, lane)])
        v_own.append([a.astype(bf16) for a in _head_halves(v_cur[:, pair], h % 2, lane)])
        v_pre.append([a.astype(bf16) for a in _head_halves(v_prev[:, pair], h % 2, lane)])
        q_h = _rope(q_ref[:, h * group * SWA_HD : (h + 1) * group * SWA_HD], cos, sin) * scale
        q2.append([q_h[:, j * LANES : (j + 1) * LANES].astype(bf16) for j in range(group // 2)])
    heads = [(h, j, par) for h in range(SWA_KV_HEADS) for j in range(group // 2) for par in range(2)]
    idx = range(len(heads))
    sink = [sinks_ref[h * group + 2 * j + par] for h, j, par in heads]
    s = [jnp.where(own, _dot_nt(q2[h][j], k_own[h][par]), _dot_nt(q2[h][j], k_pre[h][par]) + prev_bias)
         for h, j, par in heads]
    m = [jnp.maximum(jnp.max(s[i], -1, keepdims=True), sink[i]) for i in idx]
    p = [jnp.exp(s[i] - m[i]) for i in idx]
    inv = [1.0 / (jnp.sum(p[i], -1, keepdims=True) + jnp.exp(sink[i] - m[i])) for i in idx]
    o = [(_dot(jnp.where(own, p[i], 0.0).astype(bf16), v_own[h][par])
          + _dot(jnp.where(own, 0.0, p[i]).astype(bf16), v_pre[h][par])) * inv[i] for i, (h, j, par) in enumerate(heads)]
    for i in range(0, len(heads), 2):
        os_ref[:, i * SWA_HD : i * SWA_HD + LANES] = (o[i] + o[i + 1]).astype(bf16)


def _swa_prompt(proj, B, T, off_q, off_kv, cos, sin, sinks):
    assert T % WINDOW == 0 and SWA_KV_W == 2 * LANES
    nb = T // WINDOW
    return pl.pallas_call(
        _swa_prompt_kernel,
        grid=(B, nb),
        in_specs=[
            pl.BlockSpec(memory_space=pltpu.SMEM),
            pl.BlockSpec((WINDOW, SWA_Q_W), lambda b, n: (b * nb + n, off_q // SWA_Q_W)),
            pl.BlockSpec((WINDOW, 2 * SWA_KV_W), lambda b, n: (b * nb + n, off_kv // (2 * SWA_KV_W))),
            pl.BlockSpec((WINDOW, SWA_KV_W), lambda b, n: (n, 0)),
            pl.BlockSpec((WINDOW, SWA_KV_W), lambda b, n: (n, 0)),
        ],
        out_specs=[
            pl.BlockSpec((WINDOW, SWA_Q_W), lambda b, n: (b * nb + n, 0)),
            pl.BlockSpec((1, WINDOW, SWA_KV_W), lambda b, n: (b, 0, 0)),
        ],
        out_shape=[
            jax.ShapeDtypeStruct((B * T, SWA_Q_W), bf16),
            jax.ShapeDtypeStruct((B, WINDOW, SWA_KV_W), f32),
        ],
        scratch_shapes=[pltpu.VMEM((WINDOW, SWA_KV_W), f32), pltpu.VMEM((WINDOW, SWA_KV_W), f32)],
        compiler_params=_cparams(("parallel", "arbitrary")),
        name="swa_prompt",
    )(sinks, proj, proj, cos, sin)


SWA_STEP_BATCH = 8


def _swa_step_kernel(qe_ref, kv_ref, ck_ref, cv_ref, cos_ref, sin_ref, sinks_ref, r_ref, nk_ref, nv_ref):
    cos = cos_ref[...]
    sin = sin_ref[...]
    sink = sinks_ref[...]
    row = lax.broadcasted_iota(jnp.int32, (WINDOW, SWA_KV_W), 0)
    scale = SWA_HD ** -0.5
    seqs = range(qe_ref.shape[0])
    kv = [kv_ref[i] for i in seqs]
    k_new = [_rope(kv[i][:, :SWA_KV_W], cos, sin) for i in seqs]
    keys = [jnp.where(row == WINDOW - 1, k_new[i], pltpu.roll(ck_ref[i], WINDOW - 1, 0)) for i in seqs]
    vals = [jnp.where(row == WINDOW - 1, kv[i][:, SWA_KV_W:], pltpu.roll(cv_ref[i], WINDOW - 1, 0)) for i in seqs]
    for i in seqs:
        nk_ref[i] = keys[i]
        nv_ref[i] = vals[i]
    q = [_rope(qe_ref[i], cos, sin) for i in seqs]
    s = [_dot_nt(q[i].astype(bf16), keys[i].astype(bf16)) * scale for i in seqs]
    m = [jnp.maximum(jnp.max(s[i], -1, keepdims=True), sink) for i in seqs]
    p = [jnp.exp(s[i] - m[i]) for i in seqs]
    denom = [jnp.sum(p[i], -1, keepdims=True) + jnp.exp(sink - m[i]) for i in seqs]
    for i in seqs:
        r_ref[i] = _dot((p[i] / denom[i]).astype(bf16), vals[i].astype(bf16))


def _swa_step(q_exp, proj3, off_kv, cache_k, cache_v, cos, sin, sinks_col):
    B = q_exp.shape[0]
    bb = math.gcd(B, SWA_STEP_BATCH)
    assert cache_k.shape[1] == WINDOW
    return pl.pallas_call(
        _swa_step_kernel,
        grid=(B // bb,),
        in_specs=[
            pl.BlockSpec((bb, SWA_HEADS, SWA_KV_W), lambda i: (i, 0, 0)),
            pl.BlockSpec((bb, 1, 2 * SWA_KV_W), lambda i: (i, 0, off_kv // (2 * SWA_KV_W))),
            pl.BlockSpec((bb, WINDOW, SWA_KV_W), lambda i: (i, 0, 0)),
            pl.BlockSpec((bb, WINDOW, SWA_KV_W), lambda i: (i, 0, 0)),
            pl.BlockSpec((1, SWA_KV_W), lambda i: (0, 0)),
            pl.BlockSpec((1, SWA_KV_W), lambda i: (0, 0)),
            pl.BlockSpec((SWA_HEADS, 1), lambda i: (0, 0)),
        ],
        out_specs=[
            pl.BlockSpec((bb, SWA_HEADS, SWA_KV_W), lambda i: (i, 0, 0)),
            pl.BlockSpec((bb, WINDOW, SWA_KV_W), lambda i: (i, 0, 0)),
            pl.BlockSpec((bb, WINDOW, SWA_KV_W), lambda i: (i, 0, 0)),
        ],
        out_shape=[
            jax.ShapeDtypeStruct((B, SWA_HEADS, SWA_KV_W), f32),
            jax.ShapeDtypeStruct((B, WINDOW, SWA_KV_W), f32),
            jax.ShapeDtypeStruct((B, WINDOW, SWA_KV_W), f32),
        ],
        compiler_params=_cparams(("parallel",)),
        name="swa_step",
    )(q_exp, proj3, cache_k, cache_v, cos, sin, sinks_col)


def _merge_core(og_ref, os_ref, gg_ref, gs_ref, x_ref, wg_ref, ws_ref, wo_ref, gq_ref, wq_ref):
    p_gdn = _dot(og_ref[...], wg_ref[...])
    p_swa = _dot(os_ref[...], ws_ref[...])
    merged = jax.nn.sigmoid(gg_ref[...]) * p_gdn + jax.nn.sigmoid(gs_ref[...]) * p_swa
    x_new = x_ref[...] + _dot(merged.astype(bf16), wo_ref[...])
    return x_new, _dot(_rmsnorm(x_new, gq_ref[...]).astype(bf16), wq_ref[...])


def _merge_kernel(*refs):
    xo_ref, qm_ref = refs[-2:]
    xo_ref[...], qm_ref[...] = _merge_core(*refs[:-2])


def _merge_mem_kernel(*refs):
    mk_ref, mv_ref, wmo_ref, xo_ref = refs[-4:]
    x_new, q = _merge_core(*refs[:-4])
    q = q.astype(bf16)
    scale = MEM_HD ** -0.5
    heads = range(MEM_HEADS)
    cols = lambda h: slice(h * MEM_HD, (h + 1) * MEM_HD)
    s = [_dot_nt(q[:, cols(h)], mk_ref[0, :, cols(h)]) * scale for h in heads]
    p = [jnp.exp(s[h] - jnp.max(s[h], -1, keepdims=True)) for h in heads]
    p = [p[h] / jnp.sum(p[h], -1, keepdims=True) for h in heads]
    o = [_dot(p[h].astype(bf16), mv_ref[0, :, cols(h)]) for h in heads]
    xo_ref[...] = x_new + _dot(jnp.concatenate(o, axis=-1).astype(bf16), wmo_ref[...])


def _merge(og, os_, proj, off_gg, x, wg, ws, wo, gq, wq, mem=None):
    M, D = x.shape
    tm = min(256, M)
    assert M % tm == 0 and off_gg % D == 0
    const = lambda shape: pl.BlockSpec(shape, lambda i: (0, 0), pipeline_mode=pl.Buffered(1))
    in_specs = [
        pl.BlockSpec((tm, GDN_W), lambda i: (i, 0)),
        pl.BlockSpec((tm, SWA_Q_W), lambda i: (i, 0)),
        pl.BlockSpec((tm, D), lambda i: (i, off_gg // D)),
        pl.BlockSpec((tm, D), lambda i: (i, off_gg // D + 1)),
        pl.BlockSpec((tm, D), lambda i: (i, 0)),
        const((GDN_W, D)),
        const((SWA_Q_W, D)),
        const((D, D)),
        const((1, D)),
        const((D, MEM_W)),
    ]
    args = [og, os_, proj, proj, x, wg, ws, wo, gq, wq]
    x_spec = pl.BlockSpec((tm, D), lambda i: (i, 0))
    x_shape = jax.ShapeDtypeStruct((M, D), f32)
    if mem is None:
        return pl.pallas_call(
            _merge_kernel,
            grid=(M // tm,),
            in_specs=in_specs,
            out_specs=[x_spec, pl.BlockSpec((tm, MEM_W), lambda i: (i, 0))],
            out_shape=[x_shape, jax.ShapeDtypeStruct((M, MEM_W), f32)],
            compiler_params=_cparams(("parallel",)),
            name="merge",
        )(*args)
    mem_k, mem_v, wmo, seq_rows = mem
    assert seq_rows % tm == 0
    mt = mem_k.shape[1]
    mem_spec = pl.BlockSpec((1, mt, MEM_W), lambda i: (i // (seq_rows // tm), 0, 0))
    return pl.pallas_call(
        _merge_mem_kernel,
        grid=(M // tm,),
        in_specs=in_specs + [mem_spec, mem_spec, const((MEM_W, D))],
        out_specs=x_spec,
        out_shape=x_shape,
        compiler_params=_cparams(("parallel",)),
        name="merge_mem",
    )(*args, mem_k, mem_v, wmo)


MEM_STEP_BATCH = 4


def _mem_attn_step_kernel(q_ref, k_ref, v_ref, o_ref):
    mt2 = k_ref.shape[1] // SUBLANES
    for i in range(q_ref.shape[0]):
        q8 = q_ref[i] * (MEM_HD ** -0.5)
        s = jnp.sum(k_ref[i].reshape(mt2, SUBLANES, MEM_HD) * q8, axis=-1, keepdims=True)
        m = jnp.max(s, axis=0)
        m = jnp.maximum(m, pltpu.roll(m, MEM_HEADS, 0))
        p = jnp.exp(s - m)
        l = jnp.sum(p, axis=0)
        l = l + pltpu.roll(l, MEM_HEADS, 0)
        o = jnp.sum(p * v_ref[i].reshape(mt2, SUBLANES, MEM_HD), axis=0)
        o_ref[i] = (o + pltpu.roll(o, MEM_HEADS, 0)) / l


def _mem_attn_step(q8, mem_k, mem_v):
    B, rows, _ = mem_k.shape
    assert 2 * MEM_HEADS == SUBLANES and rows % SUBLANES == 0
    bb = math.gcd(B, MEM_STEP_BATCH)
    return pl.pallas_call(
        _mem_attn_step_kernel,
        grid=(B // bb,),
        in_specs=[
            pl.BlockSpec((bb, SUBLANES, MEM_HD), lambda i: (i, 0, 0)),
            pl.BlockSpec((bb, rows, MEM_HD), lambda i: (i, 0, 0)),
            pl.BlockSpec((bb, rows, MEM_HD), lambda i: (i, 0, 0)),
        ],
        out_specs=pl.BlockSpec((bb, SUBLANES, MEM_HD), lambda i: (i, 0, 0)),
        out_shape=jax.ShapeDtypeStruct((B, SUBLANES, MEM_HD), f32),
        compiler_params=_cparams(("parallel",)),
        name="mem_attn_step",
    )(q8, mem_k, mem_v)


def _proj_residual_kernel(a_ref, w_ref, x_ref, o_ref):
    o_ref[...] = x_ref[...] + _dot(a_ref[...].astype(bf16), w_ref[...])


def _proj_residual(a, w, x):
    M, D = x.shape
    return pl.pallas_call(
        _proj_residual_kernel,
        out_shape=jax.ShapeDtypeStruct((M, D), f32),
        compiler_params=pltpu.CompilerParams(vmem_limit_bytes=VMEM_LIMIT),
        name="proj_residual",
    )(a, w, x)


def _reorder_rows_kernel(w_ref, o_ref, *, pieces):
    at = 0
    for lo, hi in pieces:
        o_ref[at : at + hi - lo, :] = w_ref[lo:hi, :].astype(bf16)
        at += hi - lo
    o_ref[at:, :] = jnp.zeros((o_ref.shape[0] - at, o_ref.shape[1]), bf16)


def _reorder_rows(wt, layer, pieces, height):
    _, d_in, D = wt.shape
    assert all(lo % (2 * SUBLANES) == 0 and hi % (2 * SUBLANES) == 0 for lo, hi in pieces) and D % LANES == 0
    return pl.pallas_call(
        functools.partial(_reorder_rows_kernel, pieces=pieces),
        grid=(D // LANES,),
        in_specs=[pl.BlockSpec((None, d_in, LANES), lambda i: (layer, 0, i))],
        out_specs=pl.BlockSpec((height, LANES), lambda i: (0, i)),
        out_shape=jax.ShapeDtypeStruct((height, D), bf16),
        compiler_params=_cparams(("parallel",)),
        name="reorder_rows",
    )(wt)


def _rope_tables(pos):
    half = SWA_HD // 2
    inv_freq = ROPE_THETA ** (-jnp.arange(half, dtype=f32) / half)
    ang = pos.astype(f32)[:, None] * inv_freq[None, :]
    cos = jnp.cos(ang)
    sin = jnp.sin(ang)
    reps = SWA_KV_W // SWA_HD
    return jnp.tile(jnp.concatenate([cos, cos], -1), (1, reps)), jnp.tile(jnp.concatenate([-sin, sin], -1), (1, reps))


def kernel(x_prompt, x_sample, state_gdn, state_conv, cache_swa_k, cache_swa_v, cache_mem_k, cache_mem_v, mem_prompt, norm_ffn1, ffn1_w1, ffn1_w3, ffn1_w2, norm_mix, w_in, conv_w, gdn_A_log, gdn_dt_bias, gdn_norm, swa_sinks, w_br_gdn, w_br_swa, w_out, norm_mem_q, norm_mem_kv, w_mem_q, w_mem_k, w_mem_v, w_mem_o, norm_ffn2, ffn2_w1, ffn2_w3, ffn2_w2, norm_final):
    Bp, Tp, D = x_prompt.shape
    Bs, Ts, _ = x_sample.shape
    assert Ts == 1
    depth = norm_ffn1.shape[0]
    n_mem = mem_prompt.shape[1]
    group = SWA_HEADS // SWA_KV_HEADS

    off_gs = OFF_GG + D
    off_q = off_gs + D
    off_kv = off_q + SWA_Q_W
    off_ba = off_kv + 2 * SWA_KV_W
    d_in_pad = -(-(off_ba + LANES) // PROJ_TN) * PROJ_TN
    o_b = CONV_CH + GDN_W
    o_q = o_b + 2 * GDN_HEADS
    o_gg = o_q + SWA_Q_W + 2 * SWA_KV_W

    cos_p, sin_p = _rope_tables(jnp.arange(Tp, dtype=jnp.int32))
    cos_s, sin_s = _rope_tables(PAST_LEN + jnp.arange(Ts, dtype=jnp.int32))
    eye_kv = jnp.eye(SWA_KV_HEADS, dtype=f32)
    row = lambda v: v.reshape(1, -1)

    hp = x_prompt.reshape(Bp * Tp, D)
    hs = x_sample.reshape(Bs, D)
    outs = [[] for _ in range(10)]
    for l in range(depth):
        w_in_rt = _reorder_rows(jnp.swapaxes(w_in, 1, 2), l, ((0, o_b), (o_gg, w_in.shape[2]), (o_q, o_gg), (o_b, o_q)), d_in_pad)
        ffn1 = (row(norm_ffn1[l]), ffn1_w1[l].astype(bf16), ffn1_w3[l].astype(bf16), ffn1_w2[l].astype(bf16))
        last = l == depth - 1
        gfin = row(norm_final)
        avec = jnp.zeros((1, LANES), f32).at[0, GDN_HEADS : 2 * GDN_HEADS].set(gdn_A_log[l])
        dvec = jnp.zeros((1, LANES), f32).at[0, GDN_HEADS : 2 * GDN_HEADS].set(gdn_dt_bias[l])
        gdn_common = (conv_w[l], avec, dvec, row(gdn_norm[l]))
        tn_in = PROJ_TN

        x1 = _ffn(hp, *ffn1, gfin, False)
        proj = _norm_matmul(x1, row(norm_mix[l]), w_in_rt, tn_in, w_transposed=True)
        later = (ffn2_w1[l], ffn2_w3[l], ffn2_w2[l], w_br_gdn[l], w_br_swa[l], w_out[l], w_mem_q[l],
                 w_mem_k[l], w_mem_v[l], w_mem_o[l])
        og, s_new, later = _gdn_prompt(proj, Bp, Tp, off_ba, *gdn_common, later)
        ffn2 = (row(norm_ffn2[l]), *later[0:3])
        merge_w = (*later[3:6], row(norm_mem_q[l]), later[6])
        wmk, wmv, wmo = later[7:10]
        os_, kc = _swa_prompt(proj, Bp, Tp, off_q, off_kv, cos_p, sin_p, swa_sinks[l])
        mem_x = mem_prompt.reshape(Bp * n_mem, D)
        mk = _norm_matmul(mem_x, row(norm_mem_kv[l]), wmk, MEM_W)
        mv = _norm_matmul(mem_x, row(norm_mem_kv[l]), wmv, MEM_W)
        mem = (mk.reshape(Bp, n_mem, MEM_W).astype(bf16), mv.reshape(Bp, n_mem, MEM_W).astype(bf16), wmo, Tp)
        x3 = _merge(og, os_, proj, OFF_GG, x1, *merge_w, mem=mem)
        hp = _ffn(x3, *ffn2, gfin, last)
        proj_b = proj.reshape(Bp, Tp, d_in_pad)
        outs[0].append(s_new)
        outs[1].append(proj_b[:, Tp - (GDN_CONV - 1) :, :CONV_CH])
        outs[2].append(kc.reshape(Bp, WINDOW, SWA_KV_HEADS, SWA_HD))
        outs[3].append(proj_b[:, Tp - WINDOW :, off_kv + SWA_KV_W : off_kv + 2 * SWA_KV_W].reshape(Bp, WINDOW, SWA_KV_HEADS, SWA_HD))
        outs[4].append(mk.reshape(Bp, n_mem, MEM_HEADS, MEM_HD))
        outs[5].append(mv.reshape(Bp, n_mem, MEM_HEADS, MEM_HD))

        x1 = _ffn(hs, *ffn1, gfin, False)
        proj = _norm_matmul(x1, row(norm_mix[l]), w_in_rt, tn_in, w_transposed=True)
        proj3 = proj.reshape(Bs, 1, d_in_pad)
        og, s_new = _gdn_step(proj3, off_ba, state_conv[l], state_gdn[l], *gdn_common)
        q_raw = proj[:, off_q : off_q + SWA_Q_W].reshape(Bs, SWA_KV_HEADS, group, 1, SWA_HD)
        q_exp = (q_raw * eye_kv[None, :, None, :, None]).reshape(Bs, SWA_HEADS, SWA_KV_W)
        ck = cache_swa_k[l].reshape(Bs, WINDOW, SWA_KV_W)
        cv = cache_swa_v[l].reshape(Bs, WINDOW, SWA_KV_W)
        r, nk, nv = _swa_step(q_exp, proj3, off_kv, ck, cv, cos_s, sin_s, swa_sinks[l].reshape(SWA_HEADS, 1))
        r5 = r.reshape(Bs, SWA_KV_HEADS, group, SWA_KV_HEADS, SWA_HD)
        kvh = jnp.arange(SWA_KV_HEADS)
        os_ = jnp.transpose(r5[:, kvh, :, kvh, :], (1, 0, 2, 3)).reshape(Bs, SWA_Q_W).astype(bf16)
        x2, qm = _merge(og.reshape(Bs, GDN_W), os_, proj, OFF_GG, x1, *merge_w)
        q8 = jnp.tile(qm.reshape(Bs, MEM_HEADS, MEM_HD), (1, 2, 1))
        om = _mem_attn_step(q8, cache_mem_k[l].reshape(Bs, n_mem * MEM_HEADS, MEM_HD),
                            cache_mem_v[l].reshape(Bs, n_mem * MEM_HEADS, MEM_HD))
        x3 = _proj_residual(om[:, :MEM_HEADS].reshape(Bs, MEM_W), wmo, x2)
        hs = _ffn(x3, *ffn2, gfin, last)
        outs[6].append(s_new)
        outs[7].append(jnp.concatenate([state_conv[l][:, 1:], proj3[:, :, :CONV_CH]], axis=1))
        outs[8].append(nk.reshape(Bs, WINDOW, SWA_KV_HEADS, SWA_HD))
        outs[9].append(nv.reshape(Bs, WINDOW, SWA_KV_HEADS, SWA_HD))

    return (hp.reshape(Bp, Tp, D), hs.reshape(Bs, Ts, D), *(jnp.stack(o) for o in outs))
```

```python
import functools
import math

import jax
import jax.numpy as jnp
from jax import lax
from jax.experimental import pallas as pl
from jax.experimental.pallas import tpu as pltpu

f32 = jnp.float32
bf16 = jnp.bfloat16

PAST_LEN = 16384
GDN_HEADS = 8
GDN_D = 128
GDN_CONV = 4
SWA_HEADS = 16
SWA_KV_HEADS = 4
SWA_HD = 64
WINDOW = 128
ROPE_THETA = 10000.0
MEM_HEADS = 4
MEM_HD = 128
EPS = 1e-6
L2_EPS = 1e-6

LANES = 128
SUBLANES = 8
CHUNK = 128
GDN_STEP_CHUNKS = 2
FFN_SLAB = 64
PROJ_TN = 768
VMEM_LIMIT = 58 * 1024 * 1024

GDN_W = GDN_HEADS * GDN_D
CONV_CH = 3 * GDN_W
SWA_Q_W = SWA_HEADS * SWA_HD
SWA_KV_W = SWA_KV_HEADS * SWA_HD
MEM_W = MEM_HEADS * MEM_HD

OFF_Z = CONV_CH
OFF_GG = OFF_Z + GDN_W


def _cparams(semantics):
    return pltpu.CompilerParams(dimension_semantics=semantics, vmem_limit_bytes=VMEM_LIMIT)


def _dot(a, b):
    return jnp.dot(a, b, preferred_element_type=f32)


def _dot_nt(a, b):
    return lax.dot_general(a, b, (((1,), (1,)), ((), ())), preferred_element_type=f32)


def _rmsnorm(x, g):
    return x * lax.rsqrt(jnp.mean(x * x, -1, keepdims=True) + EPS) * g


def _silu(x):
    return x * jax.nn.sigmoid(x)


def _softplus(x):
    return jnp.maximum(x, 0.0) + jnp.log1p(jnp.exp(-jnp.abs(x)))


def _ffn_kernel(x_ref, g_ref, w1_ref, w3_ref, w2_ref, gf_ref, o_ref, h_ref, *, final_norm):
    j = pl.program_id(1)
    slab = min(FFN_SLAB, x_ref.shape[0])
    n_slabs = x_ref.shape[0] // slab

    def over_slabs(body):
        def step(s, carry):
            body(pl.ds(pl.multiple_of(s * slab, slab), slab))
            return carry
        lax.fori_loop(0, n_slabs, step, 0, unroll=min(4, n_slabs))

    @pl.when(j == 0)
    def _():
        def prologue(rows):
            h_ref[rows, :] = _rmsnorm(x_ref[rows, :], g_ref[...]).astype(bf16)
            o_ref[rows, :] = jnp.zeros((slab, o_ref.shape[1]), f32)
        over_slabs(prologue)

    h = h_ref[...]
    a = _dot(h, w1_ref[...])
    b = _dot(h, w3_ref[...])
    o_ref[...] += _dot((_silu(a) * b).astype(bf16), w2_ref[...])

    @pl.when(j == pl.num_programs(1) - 1)
    def _():
        def epilogue(rows):
            y = x_ref[rows, :] + 0.5 * o_ref[rows, :]
            if final_norm:
                y = _rmsnorm(y, gf_ref[...])
            o_ref[rows, :] = y
        over_slabs(epilogue)


def _ffn(x, g, w1, w3, w2, gf, final_norm):
    M, D = x.shape
    F = w1.shape[1]
    tm = min(1024, M)
    tf = 512
    assert M % tm == 0 and F % tf == 0
    return pl.pallas_call(
        functools.partial(_ffn_kernel, final_norm=final_norm),
        grid=(M // tm, F // tf),
        in_specs=[
            pl.BlockSpec((tm, D), lambda i, j: (i, 0)),
            pl.BlockSpec((1, D), lambda i, j: (0, 0)),
            pl.BlockSpec((D, tf), lambda i, j: (0, j)),
            pl.BlockSpec((D, tf), lambda i, j: (0, j)),
            pl.BlockSpec((tf, D), lambda i, j: (j, 0)),
            pl.BlockSpec((1, D), lambda i, j: (0, 0)),
        ],
        out_specs=pl.BlockSpec((tm, D), lambda i, j: (i, 0)),
        out_shape=jax.ShapeDtypeStruct((M, D), f32),
        scratch_shapes=[pltpu.VMEM((tm, D), bf16)],
        compiler_params=_cparams(("parallel", "arbitrary")),
        name="ffn",
    )(x, g, w1, w3, w2, gf)


def _norm_matmul_kernel(x_ref, g_ref, w_ref, o_ref, h_ref, *, w_transposed):
    @pl.when(pl.program_id(1) == 0)
    def _():
        h_ref[...] = _rmsnorm(x_ref[...], g_ref[...]).astype(bf16)

    o_ref[...] = (_dot_nt if w_transposed else _dot)(h_ref[...], w_ref[...])


def _norm_matmul(x, g, w, tn, w_transposed=False):
    M, D = x.shape
    N = w.shape[0] if w_transposed else w.shape[1]
    tm = min(1024, M)
    assert M % tm == 0 and N % tn == 0
    w_spec = pl.BlockSpec((tn, D), lambda i, j: (j, 0)) if w_transposed else pl.BlockSpec((D, tn), lambda i, j: (0, j))
    return pl.pallas_call(
        functools.partial(_norm_matmul_kernel, w_transposed=w_transposed),
        grid=(M // tm, N // tn),
        in_specs=[
            pl.BlockSpec((tm, D), lambda i, j: (i, 0)),
            pl.BlockSpec((1, D), lambda i, j: (0, 0)),
            w_spec,
        ],
        out_specs=pl.BlockSpec((tm, tn), lambda i, j: (i, j)),
        out_shape=jax.ShapeDtypeStruct((M, N), f32),
        scratch_shapes=[pltpu.VMEM((tm, D), bf16)],
        compiler_params=_cparams(("parallel", "arbitrary")),
        name="norm_matmul",
    )(x, g, w)


def _bdot(a, b):
    return _dot(a.astype(bf16), b.astype(bf16))


def _unit_lower_inverse(nmats, row, col):
    eye = jnp.where(row == col, 1.0, 0.0).astype(f32)

    def same_block(size):
        return (row // size) == (col // size)

    blk = same_block(SUBLANES)
    a1 = [jnp.where(blk, n, 0.0) for n in nmats]
    a2 = [_bdot(a, a) for a in a1]
    a4 = [_bdot(a, a) for a in a2]
    ts = [_bdot(eye - x1, eye + x2) for x1, x2 in zip(a1, a2)]
    ts = [_bdot(t, eye + x4) for t, x4 in zip(ts, a4)]
    size = SUBLANES
    while size < CHUNK:
        nxt = same_block(2 * size)
        sel = jnp.logical_and(nxt, jnp.logical_not(blk))
        tbs = [t.astype(bf16) for t in ts]
        xs = [_dot(jnp.where(sel, n, 0.0).astype(bf16), tb) for n, tb in zip(nmats, tbs)]
        ts = [t - _dot(tb, x.astype(bf16)) for t, tb, x in zip(ts, tbs, xs)]
        blk = nxt
        size *= 2
    return ts


def _conv_silu_slab(x_ref, top_ref, w_ref, sl):
    rows = x_ref.shape[0]
    y = None
    for tap in range(GDN_CONV):
        back = GDN_CONV - 1 - tap
        window = jnp.concatenate(
            [top_ref[SUBLANES - back : 2 * SUBLANES - back, sl], x_ref[SUBLANES - back : rows - back, sl]], axis=0)
        term = window * w_ref[tap : tap + 1, sl]
        y = term if y is None else y + term
    return _silu(y)


def _delta_rule_prepare(pairs, gcum, gcum_t, beta_t, row, col, q_ref, k_ref, v_ref):
    n = range(len(pairs))
    incl = row >= col
    strict = row > col
    rows_of = lambda ci: slice(ci * CHUNK, (ci + 1) * CHUNK)
    gc = [gcum[rows_of(ci), GDN_HEADS + h : GDN_HEADS + h + 1] for ci, h in pairs]
    gr = [gcum_t[ci][GDN_HEADS + h : GDN_HEADS + h + 1, :] for ci, h in pairs]
    beta = [beta_t[rows_of(ci), h : h + 1] for ci, h in pairs]
    k = [k_ref[h, rows_of(ci), :] for ci, h in pairs]
    kb = [x.astype(bf16) for x in k]
    kk = [_dot_nt(kb[i], kb[i]) for i in n]
    qk = [_dot_nt(q_ref[h, rows_of(ci), :].astype(bf16), kb[i]) for i, (ci, h) in enumerate(pairs)]
    e = [jnp.exp(jnp.where(incl, gc[i] - gr[i], 0.0)) for i in n]
    nmat = [beta[i] * kk[i] * jnp.where(strict, e[i], 0.0) for i in n]
    qkd = [(qk[i] * jnp.where(incl, e[i], 0.0)).astype(bf16) for i in n]
    tinv = _unit_lower_inverse(nmat, row, col)
    gamma = [jnp.exp(gc[i]) for i in n]
    rhs = [jnp.concatenate([beta[i] * v_ref[h, rows_of(ci), :], (beta[i] * gamma[i]) * k[i]], axis=-1).astype(bf16)
           for i, (ci, h) in enumerate(pairs)]
    sol = [_dot(tinv[i].astype(bf16), rhs[i]) for i in n]
    return gc, gamma, sol, qkd


def _delta_rule_apply(ci, prepared, q_ref, k_ref, z_ref, gn_ref, s_ref, og_ref):
    gc, gamma, sol, qkd = prepared
    heads = range(GDN_HEADS)
    rows = slice(ci * CHUNK, (ci + 1) * CHUNK)
    sb = [s_ref[h].astype(bf16) for h in heads]
    ub = [(sol[h][:, :GDN_D] - _dot(sol[h][:, GDN_D:].astype(bf16), sb[h])).astype(bf16) for h in heads]
    o = [_dot((q_ref[h, rows, :] * gamma[h]).astype(bf16), sb[h]) + _dot(qkd[h], ub[h]) for h in heads]
    for h in heads:
        g_last = gc[h][CHUNK - 1 : CHUNK, :]
        k_end = k_ref[h, rows, :] * jnp.exp(g_last - gc[h])
        s_ref[h] = jnp.exp(g_last) * s_ref[h] + _dot(k_end.T.astype(bf16), ub[h])
        zh = z_ref[rows, h * GDN_D : (h + 1) * GDN_D]
        og_ref[rows, h * GDN_D : (h + 1) * GDN_D] = (_rmsnorm(o[h], gn_ref[...]) * _silu(zh)).astype(bf16)


def _gdn_prompt_kernel(*refs, n_cast):
    qkv_ref, z_ref, ba_ref, cw_ref, av_ref, dv_ref, gn_ref = refs[:7]
    cast_in = refs[7 : 7 + n_cast]
    og_ref, so_ref = refs[7 + n_cast : 9 + n_cast]
    cast_out = refs[9 + n_cast : 9 + 2 * n_cast]
    s_ref, carry_ref, q_ref, k_ref, v_ref = refs[9 + 2 * n_cast :]
    for src, dst in zip(cast_in, cast_out):
        dst[...] = src[...].astype(bf16)

    c = pl.program_id(1)
    rows = qkv_ref.shape[0]
    n_chunks = rows // CHUNK

    @pl.when(c == 0)
    def _():
        s_ref[...] = jnp.zeros_like(s_ref)
        carry_ref[0:SUBLANES, :] = jnp.zeros((SUBLANES, CONV_CH), f32)

    carry_ref[SUBLANES : 2 * SUBLANES, :] = qkv_ref[0:SUBLANES, :]
    for j in range(3 * GDN_HEADS):
        sl = slice(j * LANES, (j + 1) * LANES)
        y = _conv_silu_slab(qkv_ref, carry_ref, cw_ref, sl)
        h = j % GDN_HEADS
        if j < 2 * GDN_HEADS:
            y = y * lax.rsqrt(jnp.sum(y * y, -1, keepdims=True) + L2_EPS)
            if j < GDN_HEADS:
                q_ref[h] = y * (GDN_D ** -0.5)
            else:
                k_ref[h] = y
        else:
            v_ref[h] = y
    carry_ref[0:SUBLANES, :] = qkv_ref[rows - SUBLANES : rows, :]

    ba = ba_ref[...]
    beta_t = jax.nn.sigmoid(ba)
    g_t = -jnp.exp(av_ref[...]) * _softplus(ba + dv_ref[...])
    row_in_chunk = lax.broadcasted_iota(jnp.int32, (rows, LANES), 0) % CHUNK
    gcum = g_t
    shift = 1
    while shift < CHUNK:
        gcum = gcum + jnp.where(row_in_chunk >= shift, pltpu.roll(gcum, shift, 0), 0.0)
        shift *= 2
    gcum_t = [gcum[ci * CHUNK : (ci + 1) * CHUNK, :].T for ci in range(n_chunks)]

    row = lax.broadcasted_iota(jnp.int32, (CHUNK, CHUNK), 0)
    col = lax.broadcasted_iota(jnp.int32, (CHUNK, CHUNK), 1)
    pairs = [(ci, h) for ci in range(n_chunks) for h in range(GDN_HEADS)]
    gc, gamma, sol, qkd = _delta_rule_prepare(pairs, gcum, gcum_t, beta_t, row, col, q_ref, k_ref, v_ref)
    for ci in range(n_chunks):
        mine = slice(ci * GDN_HEADS, (ci + 1) * GDN_HEADS)
        _delta_rule_apply(ci, (gc[mine], gamma[mine], sol[mine], qkd[mine]), q_ref, k_ref, z_ref, gn_ref, s_ref, og_ref)

    @pl.when(c == pl.num_programs(1) - 1)
    def _():
        so_ref[0] = s_ref[...]


def _cast_block(shape, steps):
    R, C = shape
    for col_blocks in (1, 2, 4, 8, 16):
        row_blocks = steps // col_blocks
        if (steps % col_blocks == 0 and R % row_blocks == 0 and C % col_blocks == 0
                and (R // row_blocks) % (2 * SUBLANES) == 0 and (C // col_blocks) % LANES == 0):
            return (R // row_blocks, C // col_blocks), col_blocks
    return None


def _gdn_prompt(proj, B, T, off_ba, cw, avec, dvec, gn, weights):
    rows = GDN_STEP_CHUNKS * CHUNK
    assert T % rows == 0
    ns = T // rows
    step = lambda b, c: b * ns + c
    plans = [_cast_block(w.shape, B * ns) for w in weights]
    riders = [w for w, p in zip(weights, plans) if p is not None]
    cast_specs = [pl.BlockSpec(blk, functools.partial(lambda b, c, cb: (step(b, c) // cb, step(b, c) % cb), cb=cb))
                  for blk, cb in (p for p in plans if p is not None)]
    head_scratch = pltpu.VMEM((GDN_HEADS, rows, GDN_D), f32)
    out = pl.pallas_call(
        functools.partial(_gdn_prompt_kernel, n_cast=len(riders)),
        grid=(B, ns),
        in_specs=[
            pl.BlockSpec((rows, CONV_CH), lambda b, c: (step(b, c), 0)),
            pl.BlockSpec((rows, GDN_W), lambda b, c: (step(b, c), OFF_Z // GDN_W)),
            pl.BlockSpec((rows, LANES), lambda b, c: (step(b, c), off_ba // LANES)),
            pl.BlockSpec((GDN_CONV, CONV_CH), lambda b, c: (0, 0)),
            pl.BlockSpec((1, LANES), lambda b, c: (0, 0)),
            pl.BlockSpec((1, LANES), lambda b, c: (0, 0)),
            pl.BlockSpec((1, GDN_D), lambda b, c: (0, 0)),
        ] + cast_specs,
        out_specs=[
            pl.BlockSpec((rows, GDN_W), lambda b, c: (step(b, c), 0)),
            pl.BlockSpec((1, GDN_HEADS, GDN_D, GDN_D), lambda b, c: (b, 0, 0, 0)),
        ] + cast_specs,
        out_shape=[
            jax.ShapeDtypeStruct((B * T, GDN_W), bf16),
            jax.ShapeDtypeStruct((B, GDN_HEADS, GDN_D, GDN_D), f32),
        ] + [jax.ShapeDtypeStruct(w.shape, bf16) for w in riders],
        scratch_shapes=[
            pltpu.VMEM((GDN_HEADS, GDN_D, GDN_D), f32),
            pltpu.VMEM((2 * SUBLANES, CONV_CH), f32),
            head_scratch, head_scratch, head_scratch,
        ],
        compiler_params=_cparams(("arbitrary", "arbitrary")),
        name="gdn_prompt",
    )(proj, proj, proj, cw, avec, dvec, gn, *riders)
    converted = iter(out[2:])
    return out[0], out[1], [next(converted) if p is not None else w.astype(bf16) for w, p in zip(weights, plans)]


def _gdn_step_kernel(qkv_ref, z_ref, ba_ref, sc_ref, s0_ref, cw_ref, av_ref, dv_ref, gn_ref, og_ref, so_ref):
    x_new = qkv_ref[0]
    taps = cw_ref[...]
    y = jnp.sum(sc_ref[0] * taps[0 : GDN_CONV - 1, :], axis=0, keepdims=True) + x_new * taps[GDN_CONV - 1 : GDN_CONV, :]
    y = _silu(y)
    ba = ba_ref[0]
    beta_t = jax.nn.sigmoid(ba)
    gamma_t = jnp.exp(-jnp.exp(av_ref[...]) * _softplus(ba + dv_ref[...]))
    z = z_ref[0]
    heads = range(GDN_HEADS)
    head = lambda base, h: y[:, base + h * GDN_D : base + (h + 1) * GDN_D]
    q = [head(0, h) for h in heads]
    k = [head(GDN_W, h) for h in heads]
    v = [head(2 * GDN_W, h) for h in heads]
    q = [q[h] * lax.rsqrt(jnp.sum(q[h] * q[h], -1, keepdims=True) + L2_EPS) * (GDN_D ** -0.5) for h in heads]
    k = [k[h] * lax.rsqrt(jnp.sum(k[h] * k[h], -1, keepdims=True) + L2_EPS) for h in heads]
    beta = [beta_t[:, h : h + 1] for h in heads]
    gamma = [gamma_t[:, GDN_HEADS + h : GDN_HEADS + h + 1] for h in heads]
    k_col = [jnp.broadcast_to(k[h], (GDN_D, GDN_D)).T for h in heads]
    q_col = [jnp.broadcast_to(q[h], (GDN_D, GDN_D)).T for h in heads]
    k_s = [jnp.sum(k_col[h] * s0_ref[0, h], axis=0, keepdims=True) for h in heads]
    q_s = [jnp.sum(q_col[h] * s0_ref[0, h], axis=0, keepdims=True) for h in heads]
    u = [beta[h] * v[h] - (beta[h] * gamma[h]) * k_s[h] for h in heads]
    o = [gamma[h] * q_s[h] + jnp.sum(q[h] * k[h], -1, keepdims=True) * u[h] for h in heads]
    for h in heads:
        so_ref[0, h] = gamma[h] * s0_ref[0, h] + k_col[h] * u[h]
        zh = z[:, h * GDN_D : (h + 1) * GDN_D]
        og_ref[0, :, h * GDN_D : (h + 1) * GDN_D] = (_rmsnorm(o[h], gn_ref[...]) * _silu(zh)).astype(bf16)


def _gdn_step(proj3, off_ba, state_conv, state_gdn, cw, avec, dvec, gn):
    B = proj3.shape[0]
    return pl.pallas_call(
        _gdn_step_kernel,
        grid=(B,),
        in_specs=[
            pl.BlockSpec((1, 1, CONV_CH), lambda b: (b, 0, 0)),
            pl.BlockSpec((1, 1, GDN_W), lambda b: (b, 0, OFF_Z // GDN_W)),
            pl.BlockSpec((1, 1, LANES), lambda b: (b, 0, off_ba // LANES)),
            pl.BlockSpec((1, GDN_CONV - 1, CONV_CH), lambda b: (b, 0, 0)),
            pl.BlockSpec((1, GDN_HEADS, GDN_D, GDN_D), lambda b: (b, 0, 0, 0)),
            pl.BlockSpec((GDN_CONV, CONV_CH), lambda b: (0, 0)),
            pl.BlockSpec((1, LANES), lambda b: (0, 0)),
            pl.BlockSpec((1, LANES), lambda b: (0, 0)),
            pl.BlockSpec((1, GDN_D), lambda b: (0, 0)),
        ],
        out_specs=[
            pl.BlockSpec((1, 1, GDN_W), lambda b: (b, 0, 0)),
            pl.BlockSpec((1, GDN_HEADS, GDN_D, GDN_D), lambda b: (b, 0, 0, 0)),
        ],
        out_shape=[
            jax.ShapeDtypeStruct((B, 1, GDN_W), bf16),
            jax.ShapeDtypeStruct((B, GDN_HEADS, GDN_D, GDN_D), f32),
        ],
        compiler_params=_cparams(("parallel",)),
        name="gdn_step",
    )(proj3, proj3, proj3, state_conv, state_gdn, cw, avec, dvec, gn)


def _rope(x, cos, sin_signed):
    width = x.shape[-1]
    lane = lax.broadcasted_iota(jnp.int32, x.shape, x.ndim - 1)
    first_half = (lane % SWA_HD) < (SWA_HD // 2)
    rot = jnp.where(first_half, pltpu.roll(x, width - SWA_HD // 2, x.ndim - 1), pltpu.roll(x, SWA_HD // 2, x.ndim - 1))
    return x * cos + rot * sin_signed


def _head_halves(x2, head_parity, lane):
    swapped = pltpu.roll(x2, SWA_HD, 1)
    lo_src, hi_src = (x2, swapped) if head_parity == 0 else (swapped, x2)
    return jnp.where(lane < SWA_HD, lo_src, 0.0), jnp.where(lane >= SWA_HD, hi_src, 0.0)


def _swa_prompt_kernel(sinks_ref, q_ref, kv_ref, cos_ref, sin_ref, os_ref, kc_ref, kprev_ref, vprev_ref):
    n = pl.program_id(1)

    @pl.when(n == 0)
    def _():
        kprev_ref[...] = jnp.zeros_like(kprev_ref)
        vprev_ref[...] = jnp.zeros_like(vprev_ref)

    cos = cos_ref[...]
    sin = sin_ref[...]
    kv = kv_ref[...]
    k_cur = _rope(kv[:, :SWA_KV_W], cos, sin)
    v_cur = kv[:, SWA_KV_W:]
    kc_ref[0] = k_cur
    k_prev = kprev_ref[...]
    v_prev = vprev_ref[...]
    kprev_ref[...] = k_cur
    vprev_ref[...] = v_cur

    row = lax.broadcasted_iota(jnp.int32, (WINDOW, WINDOW), 0)
    col = lax.broadcasted_iota(jnp.int32, (WINDOW, WINDOW), 1)
    own = col <= row
    prev_bias = jnp.where(n > 0, 0.0, -jnp.inf)
    lane = lax.broadcasted_iota(jnp.int32, (WINDOW, LANES), 1)
    scale = SWA_HD ** -0.5
    group = SWA_HEADS // SWA_KV_HEADS
    k_own, k_pre, v_own, v_pre, q2 = [], [], [], [], []
    for h in range(SWA_KV_HEADS):
        pair = slice((h // 2) * LANES, (h // 2 + 1) * LANES)
        k_own.append([a.astype(bf16) for a in _head_halves(k_cur[:, pair], h % 2, lane)])
        k_pre.append([a.astype(bf16) for a in _head_halves(k_prev[:, pair], h % 2, lane)])
        v_own.append([a.astype(bf16) for a in _head_halves(v_cur[:, pair], h % 2, lane)])
        v_pre.append([a.astype(bf16) for a in _head_halves(v_prev[:, pair], h % 2, lane)])
        q_h = _rope(q_ref[:, h * group * SWA_HD : (h + 1) * group * SWA_HD], cos, sin) * scale
        q2.append([q_h[:, j * LANES : (j + 1) * LANES].astype(bf16) for j in range(group // 2)])
    heads = [(h, j, par) for h in range(SWA_KV_HEADS) for j in range(group // 2) for par in range(2)]
    idx = range(len(heads))
    sink = [sinks_ref[h * group + 2 * j + par] for h, j, par in heads]
    s = [jnp.where(own, _dot_nt(q2[h][j], k_own[h][par]), _dot_nt(q2[h][j], k_pre[h][par]) + prev_bias)
         for h, j, par in heads]
    m = [jnp.maximum(jnp.max(s[i], -1, keepdims=True), sink[i]) for i in idx]
    p = [jnp.exp(s[i] - m[i]) for i in idx]
    inv = [1.0 / (jnp.sum(p[i], -1, keepdims=True) + jnp.exp(sink[i] - m[i])) for i in idx]
    o = [(_dot(jnp.where(own, p[i], 0.0).astype(bf16), v_own[h][par])
          + _dot(jnp.where(own, 0.0, p[i]).astype(bf16), v_pre[h][par])) * inv[i] for i, (h, j, par) in enumerate(heads)]
    for i in range(0, len(heads), 2):
        os_ref[:, i * SWA_HD : i * SWA_HD + LANES] = (o[i] + o[i + 1]).astype(bf16)


def _swa_prompt(proj, B, T, off_q, off_kv, cos, sin, sinks):
    assert T % WINDOW == 0 and SWA_KV_W == 2 * LANES
    nb = T // WINDOW
    return pl.pallas_call(
        _swa_prompt_kernel,
        grid=(B, nb),
        in_specs=[
            pl.BlockSpec(memory_space=pltpu.SMEM),
            pl.BlockSpec((WINDOW, SWA_Q_W), lambda b, n: (b * nb + n, off_q // SWA_Q_W)),
            pl.BlockSpec((WINDOW, 2 * SWA_KV_W), lambda b, n: (b * nb + n, off_kv // (2 * SWA_KV_W))),
            pl.BlockSpec((WINDOW, SWA_KV_W), lambda b, n: (n, 0)),
            pl.BlockSpec((WINDOW, SWA_KV_W), lambda b, n: (n, 0)),
        ],
        out_specs=[
            pl.BlockSpec((WINDOW, SWA_Q_W), lambda b, n: (b * nb + n, 0)),
            pl.BlockSpec((1, WINDOW, SWA_KV_W), lambda b, n: (b, 0, 0)),
        ],
        out_shape=[
            jax.ShapeDtypeStruct((B * T, SWA_Q_W), bf16),
            jax.ShapeDtypeStruct((B, WINDOW, SWA_KV_W), f32),
        ],
        scratch_shapes=[pltpu.VMEM((WINDOW, SWA_KV_W), f32), pltpu.VMEM((WINDOW, SWA_KV_W), f32)],
        compiler_params=_cparams(("parallel", "arbitrary")),
        name="swa_prompt",
    )(sinks, proj, proj, cos, sin)


SWA_STEP_BATCH = 8


def _swa_step_kernel(qe_ref, kv_ref, ck_ref, cv_ref, cos_ref, sin_ref, sinks_ref, r_ref, nk_ref, nv_ref):
    cos = cos_ref[...]
    sin = sin_ref[...]
    sink = sinks_ref[...]
    row = lax.broadcasted_iota(jnp.int32, (WINDOW, SWA_KV_W), 0)
    scale = SWA_HD ** -0.5
    seqs = range(qe_ref.shape[0])
    kv = [kv_ref[i] for i in seqs]
    k_new = [_rope(kv[i][:, :SWA_KV_W], cos, sin) for i in seqs]
    keys = [jnp.where(row == WINDOW - 1, k_new[i], pltpu.roll(ck_ref[i].T, WINDOW - 1, 0)) for i in seqs]
    vals = [jnp.where(row == WINDOW - 1, kv[i][:, SWA_KV_W:], pltpu.roll(cv_ref[i].T, WINDOW - 1, 0)) for i in seqs]
    for i in seqs:
        nk_ref[i] = keys[i].T
        nv_ref[i] = vals[i].T
    q = [_rope(qe_ref[i], cos, sin) for i in seqs]
    s = [_dot_nt(q[i].astype(bf16), keys[i].astype(bf16)) * scale for i in seqs]
    m = [jnp.maximum(jnp.max(s[i], -1, keepdims=True), sink) for i in seqs]
    p = [jnp.exp(s[i] - m[i]) for i in seqs]
    denom = [jnp.sum(p[i], -1, keepdims=True) + jnp.exp(sink - m[i]) for i in seqs]
    for i in seqs:
        r_ref[i] = _dot((p[i] / denom[i]).astype(bf16), vals[i].astype(bf16))


def _swa_step(q_exp, proj3, off_kv, cache_k, cache_v, cos, sin, sinks_col):
    B = q_exp.shape[0]
    bb = math.gcd(B, SWA_STEP_BATCH)
    assert cache_k.shape[2] == WINDOW
    return pl.pallas_call(
        _swa_step_kernel,
        grid=(B // bb,),
        in_specs=[
            pl.BlockSpec((bb, SWA_HEADS, SWA_KV_W), lambda i: (i, 0, 0)),
            pl.BlockSpec((bb, 1, 2 * SWA_KV_W), lambda i: (i, 0, off_kv // (2 * SWA_KV_W))),
            pl.BlockSpec((bb, SWA_KV_W, WINDOW), lambda i: (i, 0, 0)),
            pl.BlockSpec((bb, SWA_KV_W, WINDOW), lambda i: (i, 0, 0)),
            pl.BlockSpec((1, SWA_KV_W), lambda i: (0, 0)),
            pl.BlockSpec((1, SWA_KV_W), lambda i: (0, 0)),
            pl.BlockSpec((SWA_HEADS, 1), lambda i: (0, 0)),
        ],
        out_specs=[
            pl.BlockSpec((bb, SWA_HEADS, SWA_KV_W), lambda i: (i, 0, 0)),
            pl.BlockSpec((bb, SWA_KV_W, WINDOW), lambda i: (i, 0, 0)),
            pl.BlockSpec((bb, SWA_KV_W, WINDOW), lambda i: (i, 0, 0)),
        ],
        out_shape=[
            jax.ShapeDtypeStruct((B, SWA_HEADS, SWA_KV_W), f32),
            jax.ShapeDtypeStruct((B, SWA_KV_W, WINDOW), f32),
            jax.ShapeDtypeStruct((B, SWA_KV_W, WINDOW), f32),
        ],
        compiler_params=_cparams(("parallel",)),
        name="swa_step",
    )(q_exp, proj3, cache_k, cache_v, cos, sin, sinks_col)


def _merge_core(og_ref, os_ref, gg_ref, gs_ref, x_ref, wg_ref, ws_ref, wo_ref, gq_ref, wq_ref):
    p_gdn = _dot(og_ref[...], wg_ref[...])
    p_swa = _dot(os_ref[...], ws_ref[...])
    merged = jax.nn.sigmoid(gg_ref[...]) * p_gdn + jax.nn.sigmoid(gs_ref[...]) * p_swa
    x_new = x_ref[...] + _dot(merged.astype(bf16), wo_ref[...])
    return x_new, _dot(_rmsnorm(x_new, gq_ref[...]).astype(bf16), wq_ref[...])


def _merge_kernel(*refs):
    xo_ref, qm_ref = refs[-2:]
    xo_ref[...], qm_ref[...] = _merge_core(*refs[:-2])


def _merge_mem_kernel(*refs):
    mk_ref, mv_ref, wmo_ref, xo_ref = refs[-4:]
    x_new, q = _merge_core(*refs[:-4])
    q = q.astype(bf16)
    scale = MEM_HD ** -0.5
    heads = range(MEM_HEADS)
    cols = lambda h: slice(h * MEM_HD, (h + 1) * MEM_HD)
    s = [_dot_nt(q[:, cols(h)], mk_ref[0, :, cols(h)]) * scale for h in heads]
    p = [jnp.exp(s[h] - jnp.max(s[h], -1, keepdims=True)) for h in heads]
    p = [p[h] / jnp.sum(p[h], -1, keepdims=True) for h in heads]
    o = [_dot(p[h].astype(bf16), mv_ref[0, :, cols(h)]) for h in heads]
    xo_ref[...] = x_new + _dot(jnp.concatenate(o, axis=-1).astype(bf16), wmo_ref[...])


def _merge(og, os_, proj, off_gg, x, wg, ws, wo, gq, wq, mem=None):
    M, D = x.shape
    tm = min(256, M)
    assert M % tm == 0 and off_gg % D == 0
    const = lambda shape: pl.BlockSpec(shape, lambda i: (0, 0), pipeline_mode=pl.Buffered(1))
    in_specs = [
        pl.BlockSpec((tm, GDN_W), lambda i: (i, 0)),
        pl.BlockSpec((tm, SWA_Q_W), lambda i: (i, 0)),
        pl.BlockSpec((tm, D), lambda i: (i, off_gg // D)),
        pl.BlockSpec((tm, D), lambda i: (i, off_gg // D + 1)),
        pl.BlockSpec((tm, D), lambda i: (i, 0)),
        const((GDN_W, D)),
        const((SWA_Q_W, D)),
        const((D, D)),
        const((1, D)),
        const((D, MEM_W)),
    ]
    args = [og, os_, proj, proj, x, wg, ws, wo, gq, wq]
    x_spec = pl.BlockSpec((tm, D), lambda i: (i, 0))
    x_shape = jax.ShapeDtypeStruct((M, D), f32)
    if mem is None:
        return pl.pallas_call(
            _merge_kernel,
            grid=(M // tm,),
            in_specs=in_specs,
            out_specs=[x_spec, pl.BlockSpec((tm, MEM_W), lambda i: (i, 0))],
            out_shape=[x_shape, jax.ShapeDtypeStruct((M, MEM_W), f32)],
            compiler_params=_cparams(("parallel",)),
            name="merge",
        )(*args)
    mem_k, mem_v, wmo, seq_rows = mem
    assert seq_rows % tm == 0
    mt = mem_k.shape[1]
    mem_spec = pl.BlockSpec((1, mt, MEM_W), lambda i: (i // (seq_rows // tm), 0, 0))
    return pl.pallas_call(
        _merge_mem_kernel,
        grid=(M // tm,),
        in_specs=in_specs + [mem_spec, mem_spec, const((MEM_W, D))],
        out_specs=x_spec,
        out_shape=x_shape,
        compiler_params=_cparams(("parallel",)),
        name="merge_mem",
    )(*args, mem_k, mem_v, wmo)


MEM_STEP_BATCH = 4


def _mem_attn_step_kernel(q_ref, k_ref, v_ref, o_ref):
    mt2 = k_ref.shape[1] // SUBLANES
    for i in range(q_ref.shape[0]):
        q8 = q_ref[i] * (MEM_HD ** -0.5)
        s = jnp.sum(k_ref[i].reshape(mt2, SUBLANES, MEM_HD) * q8, axis=-1, keepdims=True)
        m = jnp.max(s, axis=0)
        m = jnp.maximum(m, pltpu.roll(m, MEM_HEADS, 0))
        p = jnp.exp(s - m)
        l = jnp.sum(p, axis=0)
        l = l + pltpu.roll(l, MEM_HEADS, 0)
        o = jnp.sum(p * v_ref[i].reshape(mt2, SUBLANES, MEM_HD), axis=0)
        o_ref[i] = (o + pltpu.roll(o, MEM_HEADS, 0)) / l


def _mem_attn_step(q8, mem_k, mem_v):
    B, rows, _ = mem_k.shape
    assert 2 * MEM_HEADS == SUBLANES and rows % SUBLANES == 0
    bb = math.gcd(B, MEM_STEP_BATCH)
    return pl.pallas_call(
        _mem_attn_step_kernel,
        grid=(B // bb,),
        in_specs=[
            pl.BlockSpec((bb, SUBLANES, MEM_HD), lambda i: (i, 0, 0)),
            pl.BlockSpec((bb, rows, MEM_HD), lambda i: (i, 0, 0)),
            pl.BlockSpec((bb, rows, MEM_HD), lambda i: (i, 0, 0)),
        ],
        out_specs=pl.BlockSpec((bb, SUBLANES, MEM_HD), lambda i: (i, 0, 0)),
        out_shape=jax.ShapeDtypeStruct((B, SUBLANES, MEM_HD), f32),
        compiler_params=_cparams(("parallel",)),
        name="mem_attn_step",
    )(q8, mem_k, mem_v)


def _proj_residual_kernel(a_ref, w_ref, x_ref, o_ref):
    o_ref[...] = x_ref[...] + _dot(a_ref[...].astype(bf16), w_ref[...])


def _proj_residual(a, w, x):
    M, D = x.shape
    return pl.pallas_call(
        _proj_residual_kernel,
        out_shape=jax.ShapeDtypeStruct((M, D), f32),
        compiler_params=pltpu.CompilerParams(vmem_limit_bytes=VMEM_LIMIT),
        name="proj_residual",
    )(a, w, x)


def _reorder_rows_kernel(w_ref, o_ref, *, pieces):
    at = 0
    for lo, hi in pieces:
        o_ref[at : at + hi - lo, :] = w_ref[lo:hi, :].astype(bf16)
        at += hi - lo
    o_ref[at:, :] = jnp.zeros((o_ref.shape[0] - at, o_ref.shape[1]), bf16)


def _reorder_rows(wt, layer, pieces, height):
    _, d_in, D = wt.shape
    assert all(lo % (2 * SUBLANES) == 0 and hi % (2 * SUBLANES) == 0 for lo, hi in pieces) and D % LANES == 0
    return pl.pallas_call(
        functools.partial(_reorder_rows_kernel, pieces=pieces),
        grid=(D // LANES,),
        in_specs=[pl.BlockSpec((None, d_in, LANES), lambda i: (layer, 0, i))],
        out_specs=pl.BlockSpec((height, LANES), lambda i: (0, i)),
        out_shape=jax.ShapeDtypeStruct((height, D), bf16),
        compiler_params=_cparams(("parallel",)),
        name="reorder_rows",
    )(wt)


def _rope_tables(pos):
    half = SWA_HD // 2
    inv_freq = ROPE_THETA ** (-jnp.arange(half, dtype=f32) / half)
    ang = pos.astype(f32)[:, None] * inv_freq[None, :]
    cos = jnp.cos(ang)
    sin = jnp.sin(ang)
    reps = SWA_KV_W // SWA_HD
    return jnp.tile(jnp.concatenate([cos, cos], -1), (1, reps)), jnp.tile(jnp.concatenate([-sin, sin], -1), (1, reps))


def kernel(x_prompt, x_sample, state_gdn, state_conv, cache_swa_k, cache_swa_v, cache_mem_k, cache_mem_v, mem_prompt, norm_ffn1, ffn1_w1, ffn1_w3, ffn1_w2, norm_mix, w_in, conv_w, gdn_A_log, gdn_dt_bias, gdn_norm, swa_sinks, w_br_gdn, w_br_swa, w_out, norm_mem_q, norm_mem_kv, w_mem_q, w_mem_k, w_mem_v, w_mem_o, norm_ffn2, ffn2_w1, ffn2_w3, ffn2_w2, norm_final):
    Bp, Tp, D = x_prompt.shape
    Bs, Ts, _ = x_sample.shape
    assert Ts == 1
    depth = norm_ffn1.shape[0]
    n_mem = mem_prompt.shape[1]
    group = SWA_HEADS // SWA_KV_HEADS

    off_gs = OFF_GG + D
    off_q = off_gs + D
    off_kv = off_q + SWA_Q_W
    off_ba = off_kv + 2 * SWA_KV_W
    d_in_pad = -(-(off_ba + LANES) // PROJ_TN) * PROJ_TN
    o_b = CONV_CH + GDN_W
    o_q = o_b + 2 * GDN_HEADS
    o_gg = o_q + SWA_Q_W + 2 * SWA_KV_W

    cos_p, sin_p = _rope_tables(jnp.arange(Tp, dtype=jnp.int32))
    cos_s, sin_s = _rope_tables(PAST_LEN + jnp.arange(Ts, dtype=jnp.int32))
    eye_kv = jnp.eye(SWA_KV_HEADS, dtype=f32)
    row = lambda v: v.reshape(1, -1)

    hp = x_prompt.reshape(Bp * Tp, D)
    hs = x_sample.reshape(Bs, D)
    outs = [[] for _ in range(10)]
    for l in range(depth):
        w_in_rt = _reorder_rows(jnp.swapaxes(w_in, 1, 2), l, ((0, o_b), (o_gg, w_in.shape[2]), (o_q, o_gg), (o_b, o_q)), d_in_pad)
        ffn1 = (row(norm_ffn1[l]), ffn1_w1[l].astype(bf16), ffn1_w3[l].astype(bf16), ffn1_w2[l].astype(bf16))
        last = l == depth - 1
        gfin = row(norm_final)
        avec = jnp.zeros((1, LANES), f32).at[0, GDN_HEADS : 2 * GDN_HEADS].set(gdn_A_log[l])
        dvec = jnp.zeros((1, LANES), f32).at[0, GDN_HEADS : 2 * GDN_HEADS].set(gdn_dt_bias[l])
        gdn_common = (conv_w[l], avec, dvec, row(gdn_norm[l]))
        tn_in = PROJ_TN

        x1 = _ffn(hp, *ffn1, gfin, False)
        proj = _norm_matmul(x1, row(norm_mix[l]), w_in_rt, tn_in, w_transposed=True)
        later = (ffn2_w1[l], ffn2_w3[l], ffn2_w2[l], w_br_gdn[l], w_br_swa[l], w_out[l], w_mem_q[l],
                 w_mem_k[l], w_mem_v[l], w_mem_o[l])
        og, s_new, later = _gdn_prompt(proj, Bp, Tp, off_ba, *gdn_common, later)
        ffn2 = (row(norm_ffn2[l]), *later[0:3])
        merge_w = (*later[3:6], row(norm_mem_q[l]), later[6])
        wmk, wmv, wmo = later[7:10]
        os_, kc = _swa_prompt(proj, Bp, Tp, off_q, off_kv, cos_p, sin_p, swa_sinks[l])
        mem_x = mem_prompt.reshape(Bp * n_mem, D)
        mk = _norm_matmul(mem_x, row(norm_mem_kv[l]), wmk, MEM_W)
        mv = _norm_matmul(mem_x, row(norm_mem_kv[l]), wmv, MEM_W)
        mem = (mk.reshape(Bp, n_mem, MEM_W).astype(bf16), mv.reshape(Bp, n_mem, MEM_W).astype(bf16), wmo, Tp)
        x3 = _merge(og, os_, proj, OFF_GG, x1, *merge_w, mem=mem)
        hp = _ffn(x3, *ffn2, gfin, last)
        proj_b = proj.reshape(Bp, Tp, d_in_pad)
        outs[0].append(s_new)
        outs[1].append(proj_b[:, Tp - (GDN_CONV - 1) :, :CONV_CH])
        outs[2].append(kc.reshape(Bp, WINDOW, SWA_KV_HEADS, SWA_HD))
        outs[3].append(proj_b[:, Tp - WINDOW :, off_kv + SWA_KV_W : off_kv + 2 * SWA_KV_W].reshape(Bp, WINDOW, SWA_KV_HEADS, SWA_HD))
        outs[4].append(mk.reshape(Bp, n_mem, MEM_HEADS, MEM_HD))
        outs[5].append(mv.reshape(Bp, n_mem, MEM_HEADS, MEM_HD))

        x1 = _ffn(hs, *ffn1, gfin, False)
        proj = _norm_matmul(x1, row(norm_mix[l]), w_in_rt, tn_in, w_transposed=True)
        proj3 = proj.reshape(Bs, 1, d_in_pad)
        og, s_new = _gdn_step(proj3, off_ba, state_conv[l], state_gdn[l], *gdn_common)
        q_raw = proj[:, off_q : off_q + SWA_Q_W].reshape(Bs, SWA_KV_HEADS, group, 1, SWA_HD)
        q_exp = (q_raw * eye_kv[None, :, None, :, None]).reshape(Bs, SWA_HEADS, SWA_KV_W)
        ck = jnp.swapaxes(cache_swa_k[l].reshape(Bs, WINDOW, SWA_KV_W), 1, 2)
        cv = jnp.swapaxes(cache_swa_v[l].reshape(Bs, WINDOW, SWA_KV_W), 1, 2)
        r, nk, nv = _swa_step(q_exp, proj3, off_kv, ck, cv, cos_s, sin_s, swa_sinks[l].reshape(SWA_HEADS, 1))
        r5 = r.reshape(Bs, SWA_KV_HEADS, group, SWA_KV_HEADS, SWA_HD)
        kvh = jnp.arange(SWA_KV_HEADS)
        os_ = jnp.transpose(r5[:, kvh, :, kvh, :], (1, 0, 2, 3)).reshape(Bs, SWA_Q_W).astype(bf16)
        x2, qm = _merge(og.reshape(Bs, GDN_W), os_, proj, OFF_GG, x1, *merge_w)
        q8 = jnp.tile(qm.reshape(Bs, MEM_HEADS, MEM_HD), (1, 2, 1))
        om = _mem_attn_step(q8, cache_mem_k[l].reshape(Bs, n_mem * MEM_HEADS, MEM_HD),
                            cache_mem_v[l].reshape(Bs, n_mem * MEM_HEADS, MEM_HD))
        x3 = _proj_residual(om[:, :MEM_HEADS].reshape(Bs, MEM_W), wmo, x2)
        hs = _ffn(x3, *ffn2, gfin, last)
        outs[6].append(s_new)
        outs[7].append(jnp.concatenate([state_conv[l][:, 1:], proj3[:, :, :CONV_CH]], axis=1))
        outs[8].append(jnp.swapaxes(nk, 1, 2).reshape(Bs, WINDOW, SWA_KV_HEADS, SWA_HD))
        outs[9].append(jnp.swapaxes(nv, 1, 2).reshape(Bs, WINDOW, SWA_KV_HEADS, SWA_HD))

    return (hp.reshape(Bp, Tp, D), hs.reshape(Bs, Ts, D), *(jnp.stack(o) for o in outs))
```

```python
import functools
import math

import jax
import jax.numpy as jnp
from jax import lax
from jax.experimental import pallas as pl
from jax.experimental.pallas import tpu as pltpu

f32 = jnp.float32
bf16 = jnp.bfloat16

PAST_LEN = 16384
GDN_HEADS = 8
GDN_D = 128
GDN_CONV = 4
SWA_HEADS = 16
SWA_KV_HEADS = 4
SWA_HD = 64
WINDOW = 128
ROPE_THETA = 10000.0
MEM_HEADS = 4
MEM_HD = 128
EPS = 1e-6
L2_EPS = 1e-6

LANES = 128
SUBLANES = 8
CHUNK = 128
GDN_STEP_CHUNKS = 2
FFN_SLAB = 64
PROJ_TN = 768
VMEM_LIMIT = 58 * 1024 * 1024

GDN_W = GDN_HEADS * GDN_D
CONV_CH = 3 * GDN_W
SWA_Q_W = SWA_HEADS * SWA_HD
SWA_KV_W = SWA_KV_HEADS * SWA_HD
MEM_W = MEM_HEADS * MEM_HD

OFF_Z = CONV_CH
OFF_GG = OFF_Z + GDN_W


def _cparams(semantics):
    return pltpu.CompilerParams(dimension_semantics=semantics, vmem_limit_bytes=VMEM_LIMIT)


def _dot(a, b):
    return jnp.dot(a, b, preferred_element_type=f32)


def _dot_nt(a, b):
    return lax.dot_general(a, b, (((1,), (1,)), ((), ())), preferred_element_type=f32)


def _rmsnorm(x, g):
    return x * lax.rsqrt(jnp.mean(x * x, -1, keepdims=True) + EPS) * g


def _silu(x):
    return x * jax.nn.sigmoid(x)


def _softplus(x):
    return jnp.maximum(x, 0.0) + jnp.log1p(jnp.exp(-jnp.abs(x)))


def _ffn_kernel(x_ref, g_ref, w1_ref, w3_ref, w2_ref, gf_ref, o_ref, h_ref, *, final_norm):
    j = pl.program_id(1)
    slab = min(FFN_SLAB, x_ref.shape[0])
    n_slabs = x_ref.shape[0] // slab

    def over_slabs(body):
        def step(s, carry):
            body(pl.ds(pl.multiple_of(s * slab, slab), slab))
            return carry
        lax.fori_loop(0, n_slabs, step, 0, unroll=min(4, n_slabs))

    @pl.when(j == 0)
    def _():
        def prologue(rows):
            h_ref[rows, :] = _rmsnorm(x_ref[rows, :], g_ref[...]).astype(bf16)
            o_ref[rows, :] = jnp.zeros((slab, o_ref.shape[1]), f32)
        over_slabs(prologue)

    h = h_ref[...]
    a = _dot(h, w1_ref[...])
    b = _dot(h, w3_ref[...])
    o_ref[...] += _dot((_silu(a) * b).astype(bf16), w2_ref[...])

    @pl.when(j == pl.num_programs(1) - 1)
    def _():
        def epilogue(rows):
            y = x_ref[rows, :] + 0.5 * o_ref[rows, :]
            if final_norm:
                y = _rmsnorm(y, gf_ref[...])
            o_ref[rows, :] = y
        over_slabs(epilogue)


def _ffn(x, g, w1, w3, w2, gf, final_norm):
    M, D = x.shape
    F = w1.shape[1]
    tm = min(1024, M)
    tf = 512
    assert M % tm == 0 and F % tf == 0
    return pl.pallas_call(
        functools.partial(_ffn_kernel, final_norm=final_norm),
        grid=(M // tm, F // tf),
        in_specs=[
            pl.BlockSpec((tm, D), lambda i, j: (i, 0)),
            pl.BlockSpec((1, D), lambda i, j: (0, 0)),
            pl.BlockSpec((D, tf), lambda i, j: (0, j)),
            pl.BlockSpec((D, tf), lambda i, j: (0, j)),
            pl.BlockSpec((tf, D), lambda i, j: (j, 0)),
            pl.BlockSpec((1, D), lambda i, j: (0, 0)),
        ],
        out_specs=pl.BlockSpec((tm, D), lambda i, j: (i, 0)),
        out_shape=jax.ShapeDtypeStruct((M, D), f32),
        scratch_shapes=[pltpu.VMEM((tm, D), bf16)],
        compiler_params=_cparams(("parallel", "arbitrary")),
        name="ffn",
    )(x, g, w1, w3, w2, gf)


def _norm_matmul_kernel(x_ref, g_ref, w_ref, o_ref, h_ref, *, w_transposed):
    @pl.when(pl.program_id(1) == 0)
    def _():
        h_ref[...] = _rmsnorm(x_ref[...], g_ref[...]).astype(bf16)

    o_ref[...] = (_dot_nt if w_transposed else _dot)(h_ref[...], w_ref[...])


def _norm_matmul(x, g, w, tn, w_transposed=False):
    M, D = x.shape
    N = w.shape[0] if w_transposed else w.shape[1]
    tm = min(1024, M)
    assert M % tm == 0 and N % tn == 0
    w_spec = pl.BlockSpec((tn, D), lambda i, j: (j, 0)) if w_transposed else pl.BlockSpec((D, tn), lambda i, j: (0, j))
    return pl.pallas_call(
        functools.partial(_norm_matmul_kernel, w_transposed=w_transposed),
        grid=(M // tm, N // tn),
        in_specs=[
            pl.BlockSpec((tm, D), lambda i, j: (i, 0)),
            pl.BlockSpec((1, D), lambda i, j: (0, 0)),
            w_spec,
        ],
        out_specs=pl.BlockSpec((tm, tn), lambda i, j: (i, j)),
        out_shape=jax.ShapeDtypeStruct((M, N), f32),
        scratch_shapes=[pltpu.VMEM((tm, D), bf16)],
        compiler_params=_cparams(("parallel", "arbitrary")),
        name="norm_matmul",
    )(x, g, w)


def _mem_kv_kernel(x_ref, g_ref, wk_ref, wv_ref, k_ref, v_ref, kb_ref, vb_ref):
    h = _rmsnorm(x_ref[...], g_ref[...]).astype(bf16)
    k = _dot(h, wk_ref[...])
    v = _dot(h, wv_ref[...])
    k_ref[...] = k
    v_ref[...] = v
    kb_ref[...] = k.astype(bf16)
    vb_ref[...] = v.astype(bf16)


def _mem_kv(x, g, wk, wv):
    M, D = x.shape
    N = wk.shape[1]
    tm = min(1024, M)
    assert M % tm == 0
    row_spec = pl.BlockSpec((tm, N), lambda i: (i, 0))
    const = lambda shape: pl.BlockSpec(shape, lambda i: (0, 0))
    return pl.pallas_call(
        _mem_kv_kernel,
        grid=(M // tm,),
        in_specs=[pl.BlockSpec((tm, D), lambda i: (i, 0)), const((1, D)), const((D, N)), const((D, N))],
        out_specs=[row_spec] * 4,
        out_shape=[jax.ShapeDtypeStruct((M, N), f32)] * 2 + [jax.ShapeDtypeStruct((M, N), bf16)] * 2,
        compiler_params=_cparams(("parallel",)),
        name="mem_kv",
    )(x, g, wk, wv)


def _bdot(a, b):
    return _dot(a.astype(bf16), b.astype(bf16))


def _unit_lower_inverse(nmats, row, col):
    eye = jnp.where(row == col, 1.0, 0.0).astype(f32)

    def same_block(size):
        return (row // size) == (col // size)

    blk = same_block(SUBLANES)
    a1 = [jnp.where(blk, n, 0.0) for n in nmats]
    a2 = [_bdot(a, a) for a in a1]
    a4 = [_bdot(a, a) for a in a2]
    ts = [_bdot(eye - x1, eye + x2) for x1, x2 in zip(a1, a2)]
    ts = [_bdot(t, eye + x4) for t, x4 in zip(ts, a4)]
    size = SUBLANES
    while size < CHUNK:
        nxt = same_block(2 * size)
        sel = jnp.logical_and(nxt, jnp.logical_not(blk))
        tbs = [t.astype(bf16) for t in ts]
        xs = [_dot(jnp.where(sel, n, 0.0).astype(bf16), tb) for n, tb in zip(nmats, tbs)]
        ts = [t - _dot(tb, x.astype(bf16)) for t, tb, x in zip(ts, tbs, xs)]
        blk = nxt
        size *= 2
    return ts


def _conv_silu_slab(x_ref, top_ref, w_ref, sl):
    rows = x_ref.shape[0]
    y = None
    for tap in range(GDN_CONV):
        back = GDN_CONV - 1 - tap
        window = jnp.concatenate(
            [top_ref[SUBLANES - back : 2 * SUBLANES - back, sl], x_ref[SUBLANES - back : rows - back, sl]], axis=0)
        term = window * w_ref[tap : tap + 1, sl]
        y = term if y is None else y + term
    return _silu(y)


def _delta_rule_prepare(pairs, gcum, gcum_t, beta_t, row, col, q_ref, k_ref, v_ref):
    n = range(len(pairs))
    incl = row >= col
    strict = row > col
    rows_of = lambda ci: slice(ci * CHUNK, (ci + 1) * CHUNK)
    gc = [gcum[rows_of(ci), GDN_HEADS + h : GDN_HEADS + h + 1] for ci, h in pairs]
    gr = [gcum_t[ci][GDN_HEADS + h : GDN_HEADS + h + 1, :] for ci, h in pairs]
    beta = [beta_t[rows_of(ci), h : h + 1] for ci, h in pairs]
    k = [k_ref[h, rows_of(ci), :] for ci, h in pairs]
    kb = [x.astype(bf16) for x in k]
    kk = [_dot_nt(kb[i], kb[i]) for i in n]
    qk = [_dot_nt(q_ref[h, rows_of(ci), :].astype(bf16), kb[i]) for i, (ci, h) in enumerate(pairs)]
    e = [jnp.exp(jnp.where(incl, gc[i] - gr[i], 0.0)) for i in n]
    nmat = [beta[i] * kk[i] * jnp.where(strict, e[i], 0.0) for i in n]
    qkd = [(qk[i] * jnp.where(incl, e[i], 0.0)).astype(bf16) for i in n]
    tinv = _unit_lower_inverse(nmat, row, col)
    gamma = [jnp.exp(gc[i]) for i in n]
    rhs = [jnp.concatenate([beta[i] * v_ref[h, rows_of(ci), :], (beta[i] * gamma[i]) * k[i]], axis=-1).astype(bf16)
           for i, (ci, h) in enumerate(pairs)]
    sol = [_dot(tinv[i].astype(bf16), rhs[i]) for i in n]
    return gc, gamma, sol, qkd


def _delta_rule_apply(ci, prepared, q_ref, k_ref, z_ref, gn_ref, s_ref, og_ref):
    gc, gamma, sol, qkd = prepared
    heads = range(GDN_HEADS)
    rows = slice(ci * CHUNK, (ci + 1) * CHUNK)
    sb = [s_ref[h].astype(bf16) for h in heads]
    ub = [(sol[h][:, :GDN_D] - _dot(sol[h][:, GDN_D:].astype(bf16), sb[h])).astype(bf16) for h in heads]
    o = [_dot((q_ref[h, rows, :] * gamma[h]).astype(bf16), sb[h]) + _dot(qkd[h], ub[h]) for h in heads]
    for h in heads:
        g_last = gc[h][CHUNK - 1 : CHUNK, :]
        k_end = k_ref[h, rows, :] * jnp.exp(g_last - gc[h])
        s_ref[h] = jnp.exp(g_last) * s_ref[h] + _dot(k_end.T.astype(bf16), ub[h])
        zh = z_ref[rows, h * GDN_D : (h + 1) * GDN_D]
        og_ref[rows, h * GDN_D : (h + 1) * GDN_D] = (_rmsnorm(o[h], gn_ref[...]) * _silu(zh)).astype(bf16)


def _gdn_prompt_kernel(*refs, n_cast):
    qkv_ref, z_ref, ba_ref, cw_ref, av_ref, dv_ref, gn_ref = refs[:7]
    cast_in = refs[7 : 7 + n_cast]
    og_ref, so_ref = refs[7 + n_cast : 9 + n_cast]
    cast_out = refs[9 + n_cast : 9 + 2 * n_cast]
    s_ref, carry_ref, q_ref, k_ref, v_ref = refs[9 + 2 * n_cast :]
    for src, dst in zip(cast_in, cast_out):
        dst[...] = src[...].astype(bf16)

    c = pl.program_id(1)
    rows = qkv_ref.shape[0]
    n_chunks = rows // CHUNK

    @pl.when(c == 0)
    def _():
        s_ref[...] = jnp.zeros_like(s_ref)
        carry_ref[0:SUBLANES, :] = jnp.zeros((SUBLANES, CONV_CH), f32)

    carry_ref[SUBLANES : 2 * SUBLANES, :] = qkv_ref[0:SUBLANES, :]
    for j in range(3 * GDN_HEADS):
        sl = slice(j * LANES, (j + 1) * LANES)
        y = _conv_silu_slab(qkv_ref, carry_ref, cw_ref, sl)
        h = j % GDN_HEADS
        if j < 2 * GDN_HEADS:
            y = y * lax.rsqrt(jnp.sum(y * y, -1, keepdims=True) + L2_EPS)
            if j < GDN_HEADS:
                q_ref[h] = y * (GDN_D ** -0.5)
            else:
                k_ref[h] = y
        else:
            v_ref[h] = y
    carry_ref[0:SUBLANES, :] = qkv_ref[rows - SUBLANES : rows, :]

    ba = ba_ref[...]
    beta_t = jax.nn.sigmoid(ba)
    g_t = -jnp.exp(av_ref[...]) * _softplus(ba + dv_ref[...])
    row_in_chunk = lax.broadcasted_iota(jnp.int32, (rows, LANES), 0) % CHUNK
    gcum = g_t
    shift = 1
    while shift < CHUNK:
        gcum = gcum + jnp.where(row_in_chunk >= shift, pltpu.roll(gcum, shift, 0), 0.0)
        shift *= 2
    gcum_t = [gcum[ci * CHUNK : (ci + 1) * CHUNK, :].T for ci in range(n_chunks)]

    row = lax.broadcasted_iota(jnp.int32, (CHUNK, CHUNK), 0)
    col = lax.broadcasted_iota(jnp.int32, (CHUNK, CHUNK), 1)
    pairs = [(ci, h) for ci in range(n_chunks) for h in range(GDN_HEADS)]
    gc, gamma, sol, qkd = _delta_rule_prepare(pairs, gcum, gcum_t, beta_t, row, col, q_ref, k_ref, v_ref)
    for ci in range(n_chunks):
        mine = slice(ci * GDN_HEADS, (ci + 1) * GDN_HEADS)
        _delta_rule_apply(ci, (gc[mine], gamma[mine], sol[mine], qkd[mine]), q_ref, k_ref, z_ref, gn_ref, s_ref, og_ref)

    @pl.when(c == pl.num_programs(1) - 1)
    def _():
        so_ref[0] = s_ref[...]


def _cast_block(shape, steps):
    R, C = shape
    for col_blocks in (1, 2, 4, 8, 16):
        row_blocks = steps // col_blocks
        if (steps % col_blocks == 0 and R % row_blocks == 0 and C % col_blocks == 0
                and (R // row_blocks) % (2 * SUBLANES) == 0 and (C // col_blocks) % LANES == 0):
            return (R // row_blocks, C // col_blocks), col_blocks
    return None


def _gdn_prompt(proj, B, T, off_ba, cw, avec, dvec, gn, weights):
    rows = GDN_STEP_CHUNKS * CHUNK
    assert T % rows == 0
    ns = T // rows
    step = lambda b, c: b * ns + c
    plans = [_cast_block(w.shape, B * ns) for w in weights]
    riders = [w for w, p in zip(weights, plans) if p is not None]
    cast_specs = [pl.BlockSpec(blk, functools.partial(lambda b, c, cb: (step(b, c) // cb, step(b, c) % cb), cb=cb))
                  for blk, cb in (p for p in plans if p is not None)]
    head_scratch = pltpu.VMEM((GDN_HEADS, rows, GDN_D), f32)
    out = pl.pallas_call(
        functools.partial(_gdn_prompt_kernel, n_cast=len(riders)),
        grid=(B, ns),
        in_specs=[
            pl.BlockSpec((rows, CONV_CH), lambda b, c: (step(b, c), 0)),
            pl.BlockSpec((rows, GDN_W), lambda b, c: (step(b, c), OFF_Z // GDN_W)),
            pl.BlockSpec((rows, LANES), lambda b, c: (step(b, c), off_ba // LANES)),
            pl.BlockSpec((GDN_CONV, CONV_CH), lambda b, c: (0, 0)),
            pl.BlockSpec((1, LANES), lambda b, c: (0, 0)),
            pl.BlockSpec((1, LANES), lambda b, c: (0, 0)),
            pl.BlockSpec((1, GDN_D), lambda b, c: (0, 0)),
        ] + cast_specs,
        out_specs=[
            pl.BlockSpec((rows, GDN_W), lambda b, c: (step(b, c), 0)),
            pl.BlockSpec((1, GDN_HEADS, GDN_D, GDN_D), lambda b, c: (b, 0, 0, 0)),
        ] + cast_specs,
        out_shape=[
            jax.ShapeDtypeStruct((B * T, GDN_W), bf16),
            jax.ShapeDtypeStruct((B, GDN_HEADS, GDN_D, GDN_D), f32),
        ] + [jax.ShapeDtypeStruct(w.shape, bf16) for w in riders],
        scratch_shapes=[
            pltpu.VMEM((GDN_HEADS, GDN_D, GDN_D), f32),
            pltpu.VMEM((2 * SUBLANES, CONV_CH), f32),
            head_scratch, head_scratch, head_scratch,
        ],
        compiler_params=_cparams(("arbitrary", "arbitrary")),
        name="gdn_prompt",
    )(proj, proj, proj, cw, avec, dvec, gn, *riders)
    converted = iter(out[2:])
    return out[0], out[1], [next(converted) if p is not None else w.astype(bf16) for w, p in zip(weights, plans)]


def _gdn_step_kernel(qkv_ref, z_ref, ba_ref, sc_ref, s0_ref, cw_ref, av_ref, dv_ref, gn_ref, og_ref, so_ref):
    x_new = qkv_ref[0]
    taps = cw_ref[...]
    y = jnp.sum(sc_ref[0] * taps[0 : GDN_CONV - 1, :], axis=0, keepdims=True) + x_new * taps[GDN_CONV - 1 : GDN_CONV, :]
    y = _silu(y)
    ba = ba_ref[0]
    beta_t = jax.nn.sigmoid(ba)
    gamma_t = jnp.exp(-jnp.exp(av_ref[...]) * _softplus(ba + dv_ref[...]))
    z = z_ref[0]
    heads = range(GDN_HEADS)
    head = lambda base, h: y[:, base + h * GDN_D : base + (h + 1) * GDN_D]
    q = [head(0, h) for h in heads]
    k = [head(GDN_W, h) for h in heads]
    v = [head(2 * GDN_W, h) for h in heads]
    q = [q[h] * lax.rsqrt(jnp.sum(q[h] * q[h], -1, keepdims=True) + L2_EPS) * (GDN_D ** -0.5) for h in heads]
    k = [k[h] * lax.rsqrt(jnp.sum(k[h] * k[h], -1, keepdims=True) + L2_EPS) for h in heads]
    beta = [beta_t[:, h : h + 1] for h in heads]
    gamma = [gamma_t[:, GDN_HEADS + h : GDN_HEADS + h + 1] for h in heads]
    k_col = [jnp.broadcast_to(k[h], (GDN_D, GDN_D)).T for h in heads]
    sub = lax.broadcasted_iota(jnp.int32, (SUBLANES, GDN_D), 0)
    kq = [jnp.where(sub == 0, k[h], jnp.where(sub == 1, q[h], 0.0)).astype(bf16) for h in heads]
    kq_s = [_dot(kq[h], s0_ref[0, h].astype(bf16)) for h in heads]
    k_s = [kq_s[h][0:1] for h in heads]
    q_s = [kq_s[h][1:2] for h in heads]
    u = [beta[h] * v[h] - (beta[h] * gamma[h]) * k_s[h] for h in heads]
    o = [gamma[h] * q_s[h] + jnp.sum(q[h] * k[h], -1, keepdims=True) * u[h] for h in heads]
    for h in heads:
        so_ref[0, h] = gamma[h] * s0_ref[0, h] + k_col[h] * u[h]
        zh = z[:, h * GDN_D : (h + 1) * GDN_D]
        og_ref[0, :, h * GDN_D : (h + 1) * GDN_D] = (_rmsnorm(o[h], gn_ref[...]) * _silu(zh)).astype(bf16)


def _gdn_step(proj3, off_ba, state_conv, state_gdn, cw, avec, dvec, gn):
    B = proj3.shape[0]
    return pl.pallas_call(
        _gdn_step_kernel,
        grid=(B,),
        in_specs=[
            pl.BlockSpec((1, 1, CONV_CH), lambda b: (b, 0, 0)),
            pl.BlockSpec((1, 1, GDN_W), lambda b: (b, 0, OFF_Z // GDN_W)),
            pl.BlockSpec((1, 1, LANES), lambda b: (b, 0, off_ba // LANES)),
            pl.BlockSpec((1, GDN_CONV - 1, CONV_CH), lambda b: (b, 0, 0)),
            pl.BlockSpec((1, GDN_HEADS, GDN_D, GDN_D), lambda b: (b, 0, 0, 0)),
            pl.BlockSpec((GDN_CONV, CONV_CH), lambda b: (0, 0)),
            pl.BlockSpec((1, LANES), lambda b: (0, 0)),
            pl.BlockSpec((1, LANES), lambda b: (0, 0)),
            pl.BlockSpec((1, GDN_D), lambda b: (0, 0)),
        ],
        out_specs=[
            pl.BlockSpec((1, 1, GDN_W), lambda b: (b, 0, 0)),
            pl.BlockSpec((1, GDN_HEADS, GDN_D, GDN_D), lambda b: (b, 0, 0, 0)),
        ],
        out_shape=[
            jax.ShapeDtypeStruct((B, 1, GDN_W), bf16),
            jax.ShapeDtypeStruct((B, GDN_HEADS, GDN_D, GDN_D), f32),
        ],
        compiler_params=_cparams(("parallel",)),
        name="gdn_step",
    )(proj3, proj3, proj3, state_conv, state_gdn, cw, avec, dvec, gn)


def _rope(x, cos, sin_signed):
    width = x.shape[-1]
    lane = lax.broadcasted_iota(jnp.int32, x.shape, x.ndim - 1)
    first_half = (lane % SWA_HD) < (SWA_HD // 2)
    rot = jnp.where(first_half, pltpu.roll(x, width - SWA_HD // 2, x.ndim - 1), pltpu.roll(x, SWA_HD // 2, x.ndim - 1))
    return x * cos + rot * sin_signed


def _head_halves(x2, head_parity, lane):
    swapped = pltpu.roll(x2, SWA_HD, 1)
    lo_src, hi_src = (x2, swapped) if head_parity == 0 else (swapped, x2)
    return jnp.where(lane < SWA_HD, lo_src, 0.0), jnp.where(lane >= SWA_HD, hi_src, 0.0)


def _swa_prompt_kernel(sinks_ref, q_ref, kv_ref, cos_ref, sin_ref, os_ref, kc_ref, kprev_ref, vprev_ref):
    n = pl.program_id(1)

    @pl.when(n == 0)
    def _():
        kprev_ref[...] = jnp.zeros_like(kprev_ref)
        vprev_ref[...] = jnp.zeros_like(vprev_ref)

    cos = cos_ref[...]
    sin = sin_ref[...]
    kv = kv_ref[...]
    k_cur = _rope(kv[:, :SWA_KV_W], cos, sin)
    v_cur = kv[:, SWA_KV_W:]
    kc_ref[0] = k_cur

    row = lax.broadcasted_iota(jnp.int32, (WINDOW, WINDOW), 0)
    col = lax.broadcasted_iota(jnp.int32, (WINDOW, WINDOW), 1)
    own = col <= row
    prev_bias = jnp.where(n > 0, 0.0, -jnp.inf)
    lane = lax.broadcasted_iota(jnp.int32, (WINDOW, LANES), 1)
    scale = SWA_HD ** -0.5
    group = SWA_HEADS // SWA_KV_HEADS
    k_own, k_pre, v_own, v_pre, q2 = [], [], [], [], []
    for h in range(SWA_KV_HEADS):
        pair = slice((h // 2) * LANES, (h // 2 + 1) * LANES)
        k_own.append([a.astype(bf16) for a in _head_halves(k_cur[:, pair], h % 2, lane)])
        v_own.append([a.astype(bf16) for a in _head_halves(v_cur[:, pair], h % 2, lane)])
        k_pre.append([kprev_ref[h, par] for par in range(2)])
        v_pre.append([vprev_ref[h, par] for par in range(2)])
        for par in range(2):
            kprev_ref[h, par] = k_own[h][par]
            vprev_ref[h, par] = v_own[h][par]
        q_h = _rope(q_ref[:, h * group * SWA_HD : (h + 1) * group * SWA_HD], cos, sin) * scale
        q2.append([q_h[:, j * LANES : (j + 1) * LANES].astype(bf16) for j in range(group // 2)])
    heads = [(h, j, par) for h in range(SWA_KV_HEADS) for j in range(group // 2) for par in range(2)]
    idx = range(len(heads))
    sink = [sinks_ref[h * group + 2 * j + par] for h, j, par in heads]
    s = [jnp.where(own, _dot_nt(q2[h][j], k_own[h][par]), _dot_nt(q2[h][j], k_pre[h][par]) + prev_bias)
         for h, j, par in heads]
    m = [jnp.maximum(jnp.max(s[i], -1, keepdims=True), sink[i]) for i in idx]
    p = [jnp.exp(s[i] - m[i]) for i in idx]
    inv = [1.0 / (jnp.sum(p[i], -1, keepdims=True) + jnp.exp(sink[i] - m[i])) for i in idx]
    o = [(_dot(jnp.where(own, p[i], 0.0).astype(bf16), v_own[h][par])
          + _dot(jnp.where(own, 0.0, p[i]).astype(bf16), v_pre[h][par])) * inv[i] for i, (h, j, par) in enumerate(heads)]
    for i in range(0, len(heads), 2):
        os_ref[:, i * SWA_HD : i * SWA_HD + LANES] = (o[i] + o[i + 1]).astype(bf16)


def _swa_prompt(proj, B, T, off_q, off_kv, cos, sin, sinks):
    assert T % WINDOW == 0 and SWA_KV_W == 2 * LANES
    nb = T // WINDOW
    return pl.pallas_call(
        _swa_prompt_kernel,
        grid=(B, nb),
        in_specs=[
            pl.BlockSpec(memory_space=pltpu.SMEM),
            pl.BlockSpec((WINDOW, SWA_Q_W), lambda b, n: (b * nb + n, off_q // SWA_Q_W)),
            pl.BlockSpec((WINDOW, 2 * SWA_KV_W), lambda b, n: (b * nb + n, off_kv // (2 * SWA_KV_W))),
            pl.BlockSpec((WINDOW, SWA_KV_W), lambda b, n: (n, 0)),
            pl.BlockSpec((WINDOW, SWA_KV_W), lambda b, n: (n, 0)),
        ],
        out_specs=[
            pl.BlockSpec((WINDOW, SWA_Q_W), lambda b, n: (b * nb + n, 0)),
            pl.BlockSpec((1, WINDOW, SWA_KV_W), lambda b, n: (b, 0, 0)),
        ],
        out_shape=[
            jax.ShapeDtypeStruct((B * T, SWA_Q_W), bf16),
            jax.ShapeDtypeStruct((B, WINDOW, SWA_KV_W), f32),
        ],
        scratch_shapes=[pltpu.VMEM((SWA_KV_HEADS, 2, WINDOW, LANES), bf16)] * 2,
        compiler_params=_cparams(("parallel", "arbitrary")),
        name="swa_prompt",
    )(sinks, proj, proj, cos, sin)


SWA_STEP_BATCH = 8


def _swa_step_kernel(qe_ref, kv_ref, ck_ref, cv_ref, cos_ref, sin_ref, sinks_ref, r_ref, nk_ref, nv_ref):
    cos = cos_ref[...]
    sin = sin_ref[...]
    sink = sinks_ref[...]
    row = lax.broadcasted_iota(jnp.int32, (WINDOW, SWA_KV_W), 0)
    scale = SWA_HD ** -0.5
    seqs = range(qe_ref.shape[0])
    kv = [kv_ref[i] for i in seqs]
    k_new = [_rope(kv[i][:, :SWA_KV_W], cos, sin) for i in seqs]
    keys = [jnp.where(row == WINDOW - 1, k_new[i], pltpu.roll(ck_ref[i].T, WINDOW - 1, 0)) for i in seqs]
    vals = [jnp.where(row == WINDOW - 1, kv[i][:, SWA_KV_W:], pltpu.roll(cv_ref[i].T, WINDOW - 1, 0)) for i in seqs]
    for i in seqs:
        nk_ref[i] = keys[i].T
        nv_ref[i] = vals[i].T
    q = [_rope(qe_ref[i], cos, sin) for i in seqs]
    s = [_dot_nt(q[i].astype(bf16), keys[i].astype(bf16)) * scale for i in seqs]
    m = [jnp.maximum(jnp.max(s[i], -1, keepdims=True), sink) for i in seqs]
    p = [jnp.exp(s[i] - m[i]) for i in seqs]
    denom = [jnp.sum(p[i], -1, keepdims=True) + jnp.exp(sink - m[i]) for i in seqs]
    for i in seqs:
        r_ref[i] = _dot((p[i] / denom[i]).astype(bf16), vals[i].astype(bf16))


def _swa_step(q_exp, proj3, off_kv, cache_k, cache_v, cos, sin, sinks_col):
    B = q_exp.shape[0]
    bb = math.gcd(B, SWA_STEP_BATCH)
    assert cache_k.shape[2] == WINDOW
    return pl.pallas_call(
        _swa_step_kernel,
        grid=(B // bb,),
        in_specs=[
            pl.BlockSpec((bb, SWA_HEADS, SWA_KV_W), lambda i: (i, 0, 0)),
            pl.BlockSpec((bb, 1, 2 * SWA_KV_W), lambda i: (i, 0, off_kv // (2 * SWA_KV_W))),
            pl.BlockSpec((bb, SWA_KV_W, WINDOW), lambda i: (i, 0, 0)),
            pl.BlockSpec((bb, SWA_KV_W, WINDOW), lambda i: (i, 0, 0)),
            pl.BlockSpec((1, SWA_KV_W), lambda i: (0, 0)),
            pl.BlockSpec((1, SWA_KV_W), lambda i: (0, 0)),
            pl.BlockSpec((SWA_HEADS, 1), lambda i: (0, 0)),
        ],
        out_specs=[
            pl.BlockSpec((bb, SWA_HEADS, SWA_KV_W), lambda i: (i, 0, 0)),
            pl.BlockSpec((bb, SWA_KV_W, WINDOW), lambda i: (i, 0, 0)),
            pl.BlockSpec((bb, SWA_KV_W, WINDOW), lambda i: (i, 0, 0)),
        ],
        out_shape=[
            jax.ShapeDtypeStruct((B, SWA_HEADS, SWA_KV_W), f32),
            jax.ShapeDtypeStruct((B, SWA_KV_W, WINDOW), f32),
            jax.ShapeDtypeStruct((B, SWA_KV_W, WINDOW), f32),
        ],
        compiler_params=_cparams(("parallel",)),
        name="swa_step",
    )(q_exp, proj3, cache_k, cache_v, cos, sin, sinks_col)


def _merge_core(og_ref, os_ref, gg_ref, gs_ref, x_ref, wg_ref, ws_ref, wo_ref, gq_ref, wq_ref):
    p_gdn = _dot(og_ref[...], wg_ref[...])
    p_swa = _dot(os_ref[...], ws_ref[...])
    merged = jax.nn.sigmoid(gg_ref[...]) * p_gdn + jax.nn.sigmoid(gs_ref[...]) * p_swa
    x_new = x_ref[...] + _dot(merged.astype(bf16), wo_ref[...])
    return x_new, _dot(_rmsnorm(x_new, gq_ref[...]).astype(bf16), wq_ref[...])


def _merge_kernel(*refs):
    xo_ref, qm_ref = refs[-2:]
    xo_ref[...], qm_ref[...] = _merge_core(*refs[:-2])


def _merge_mem_kernel(*refs):
    mk_ref, mv_ref, wmo_ref, xo_ref = refs[-4:]
    x_new, q = _merge_core(*refs[:-4])
    q = q.astype(bf16)
    scale = MEM_HD ** -0.5
    heads = range(MEM_HEADS)
    cols = lambda h: slice(h * MEM_HD, (h + 1) * MEM_HD)
    s = [_dot_nt(q[:, cols(h)], mk_ref[0, :, cols(h)]) * scale for h in heads]
    p = [jnp.exp(s[h] - jnp.max(s[h], -1, keepdims=True)) for h in heads]
    p = [p[h] / jnp.sum(p[h], -1, keepdims=True) for h in heads]
    o = [_dot(p[h].astype(bf16), mv_ref[0, :, cols(h)]) for h in heads]
    xo_ref[...] = x_new + _dot(jnp.concatenate(o, axis=-1).astype(bf16), wmo_ref[...])


def _merge(og, os_, proj, off_gg, x, wg, ws, wo, gq, wq, mem=None):
    M, D = x.shape
    tm = min(256, M)
    assert M % tm == 0 and off_gg % D == 0
    const = lambda shape: pl.BlockSpec(shape, lambda i: (0, 0), pipeline_mode=pl.Buffered(1))
    in_specs = [
        pl.BlockSpec((tm, GDN_W), lambda i: (i, 0)),
        pl.BlockSpec((tm, SWA_Q_W), lambda i: (i, 0)),
        pl.BlockSpec((tm, D), lambda i: (i, off_gg // D)),
        pl.BlockSpec((tm, D), lambda i: (i, off_gg // D + 1)),
        pl.BlockSpec((tm, D), lambda i: (i, 0)),
        const((GDN_W, D)),
        const((SWA_Q_W, D)),
        const((D, D)),
        const((1, D)),
        const((D, MEM_W)),
    ]
    args = [og, os_, proj, proj, x, wg, ws, wo, gq, wq]
    x_spec = pl.BlockSpec((tm, D), lambda i: (i, 0))
    x_shape = jax.ShapeDtypeStruct((M, D), f32)
    if mem is None:
        return pl.pallas_call(
            _merge_kernel,
            grid=(M // tm,),
            in_specs=in_specs,
            out_specs=[x_spec, pl.BlockSpec((tm, MEM_W), lambda i: (i, 0))],
            out_shape=[x_shape, jax.ShapeDtypeStruct((M, MEM_W), f32)],
            compiler_params=_cparams(("parallel",)),
            name="merge",
        )(*args)
    mem_k, mem_v, wmo, seq_rows = mem
    assert seq_rows % tm == 0
    mt = mem_k.shape[1]
    mem_spec = pl.BlockSpec((1, mt, MEM_W), lambda i: (i // (seq_rows // tm), 0, 0))
    return pl.pallas_call(
        _merge_mem_kernel,
        grid=(M // tm,),
        in_specs=in_specs + [mem_spec, mem_spec, const((MEM_W, D))],
        out_specs=x_spec,
        out_shape=x_shape,
        compiler_params=_cparams(("parallel",)),
        name="merge_mem",
    )(*args, mem_k, mem_v, wmo)


MEM_STEP_BATCH = 4


def _mem_attn_step_kernel(q_ref, k_ref, v_ref, o_ref):
    mt2 = k_ref.shape[1] // SUBLANES
    for i in range(q_ref.shape[0]):
        q8 = q_ref[i] * (MEM_HD ** -0.5)
        s = jnp.sum(k_ref[i].reshape(mt2, SUBLANES, MEM_HD) * q8, axis=-1, keepdims=True)
        m = jnp.max(s, axis=0)
        m = jnp.maximum(m, pltpu.roll(m, MEM_HEADS, 0))
        p = jnp.exp(s - m)
        l = jnp.sum(p, axis=0)
        l = l + pltpu.roll(l, MEM_HEADS, 0)
        o = jnp.sum(p * v_ref[i].reshape(mt2, SUBLANES, MEM_HD), axis=0)
        o_ref[i] = (o + pltpu.roll(o, MEM_HEADS, 0)) / l


def _mem_attn_step(q8, mem_k, mem_v):
    B, rows, _ = mem_k.shape
    assert 2 * MEM_HEADS == SUBLANES and rows % SUBLANES == 0
    bb = math.gcd(B, MEM_STEP_BATCH)
    return pl.pallas_call(
        _mem_attn_step_kernel,
        grid=(B // bb,),
        in_specs=[
            pl.BlockSpec((bb, SUBLANES, MEM_HD), lambda i: (i, 0, 0)),
            pl.BlockSpec((bb, rows, MEM_HD), lambda i: (i, 0, 0)),
            pl.BlockSpec((bb, rows, MEM_HD), lambda i: (i, 0, 0)),
        ],
        out_specs=pl.BlockSpec((bb, SUBLANES, MEM_HD), lambda i: (i, 0, 0)),
        out_shape=jax.ShapeDtypeStruct((B, SUBLANES, MEM_HD), f32),
        compiler_params=_cparams(("parallel",)),
        name="mem_attn_step",
    )(q8, mem_k, mem_v)


def _proj_residual_kernel(a_ref, w_ref, x_ref, o_ref):
    o_ref[...] = x_ref[...] + _dot(a_ref[...].astype(bf16), w_ref[...])


def _proj_residual(a, w, x):
    M, D = x.shape
    return pl.pallas_call(
        _proj_residual_kernel,
        out_shape=jax.ShapeDtypeStruct((M, D), f32),
        compiler_params=pltpu.CompilerParams(vmem_limit_bytes=VMEM_LIMIT),
        name="proj_residual",
    )(a, w, x)


def _reorder_rows_kernel(w_ref, o_ref, *, pieces):
    at = 0
    for lo, hi in pieces:
        o_ref[at : at + hi - lo, :] = w_ref[lo:hi, :].astype(bf16)
        at += hi - lo
    o_ref[at:, :] = jnp.zeros((o_ref.shape[0] - at, o_ref.shape[1]), bf16)


def _reorder_rows(wt, layer, pieces, height):
    _, d_in, D = wt.shape
    assert all(lo % (2 * SUBLANES) == 0 and hi % (2 * SUBLANES) == 0 for lo, hi in pieces) and D % LANES == 0
    return pl.pallas_call(
        functools.partial(_reorder_rows_kernel, pieces=pieces),
        grid=(D // LANES,),
        in_specs=[pl.BlockSpec((None, d_in, LANES), lambda i: (layer, 0, i))],
        out_specs=pl.BlockSpec((height, LANES), lambda i: (0, i)),
        out_shape=jax.ShapeDtypeStruct((height, D), bf16),
        compiler_params=_cparams(("parallel",)),
        name="reorder_rows",
    )(wt)


def _rope_tables(pos):
    half = SWA_HD // 2
    inv_freq = ROPE_THETA ** (-jnp.arange(half, dtype=f32) / half)
    ang = pos.astype(f32)[:, None] * inv_freq[None, :]
    cos = jnp.cos(ang)
    sin = jnp.sin(ang)
    reps = SWA_KV_W // SWA_HD
    return jnp.tile(jnp.concatenate([cos, cos], -1), (1, reps)), jnp.tile(jnp.concatenate([-sin, sin], -1), (1, reps))


def kernel(x_prompt, x_sample, state_gdn, state_conv, cache_swa_k, cache_swa_v, cache_mem_k, cache_mem_v, mem_prompt, norm_ffn1, ffn1_w1, ffn1_w3, ffn1_w2, norm_mix, w_in, conv_w, gdn_A_log, gdn_dt_bias, gdn_norm, swa_sinks, w_br_gdn, w_br_swa, w_out, norm_mem_q, norm_mem_kv, w_mem_q, w_mem_k, w_mem_v, w_mem_o, norm_ffn2, ffn2_w1, ffn2_w3, ffn2_w2, norm_final):
    Bp, Tp, D = x_prompt.shape
    Bs, Ts, _ = x_sample.shape
    assert Ts == 1
    depth = norm_ffn1.shape[0]
    n_mem = mem_prompt.shape[1]
    group = SWA_HEADS // SWA_KV_HEADS

    off_gs = OFF_GG + D
    off_q = off_gs + D
    off_kv = off_q + SWA_Q_W
    off_ba = off_kv + 2 * SWA_KV_W
    d_in_pad = -(-(off_ba + LANES) // PROJ_TN) * PROJ_TN
    o_b = CONV_CH + GDN_W
    o_q = o_b + 2 * GDN_HEADS
    o_gg = o_q + SWA_Q_W + 2 * SWA_KV_W

    cos_p, sin_p = _rope_tables(jnp.arange(Tp, dtype=jnp.int32))
    cos_s, sin_s = _rope_tables(PAST_LEN + jnp.arange(Ts, dtype=jnp.int32))
    eye_kv = jnp.eye(SWA_KV_HEADS, dtype=f32)
    row = lambda v: v.reshape(1, -1)

    hp = x_prompt.reshape(Bp * Tp, D)
    hs = x_sample.reshape(Bs, D)
    outs = [[] for _ in range(10)]
    for l in range(depth):
        w_in_rt = _reorder_rows(jnp.swapaxes(w_in, 1, 2), l, ((0, o_b), (o_gg, w_in.shape[2]), (o_q, o_gg), (o_b, o_q)), d_in_pad)
        ffn1 = (row(norm_ffn1[l]), ffn1_w1[l].astype(bf16), ffn1_w3[l].astype(bf16), ffn1_w2[l].astype(bf16))
        last = l == depth - 1
        gfin = row(norm_final)
        avec = jnp.zeros((1, LANES), f32).at[0, GDN_HEADS : 2 * GDN_HEADS].set(gdn_A_log[l])
        dvec = jnp.zeros((1, LANES), f32).at[0, GDN_HEADS : 2 * GDN_HEADS].set(gdn_dt_bias[l])
        gdn_common = (conv_w[l], avec, dvec, row(gdn_norm[l]))
        tn_in = PROJ_TN

        x1 = _ffn(hp, *ffn1, gfin, False)
        proj = _norm_matmul(x1, row(norm_mix[l]), w_in_rt, tn_in, w_transposed=True)
        later = (ffn2_w1[l], ffn2_w3[l], ffn2_w2[l], w_br_gdn[l], w_br_swa[l], w_out[l], w_mem_q[l],
                 w_mem_k[l], w_mem_v[l], w_mem_o[l])
        og, s_new, later = _gdn_prompt(proj, Bp, Tp, off_ba, *gdn_common, later)
        ffn2 = (row(norm_ffn2[l]), *later[0:3])
        merge_w = (*later[3:6], row(norm_mem_q[l]), later[6])
        wmk, wmv, wmo = later[7:10]
        os_, kc = _swa_prompt(proj, Bp, Tp, off_q, off_kv, cos_p, sin_p, swa_sinks[l])
        mem_x = mem_prompt.reshape(Bp * n_mem, D)
        mk, mv, mkb, mvb = _mem_kv(mem_x, row(norm_mem_kv[l]), wmk, wmv)
        mem = (mkb.reshape(Bp, n_mem, MEM_W), mvb.reshape(Bp, n_mem, MEM_W), wmo, Tp)
        x3 = _merge(og, os_, proj, OFF_GG, x1, *merge_w, mem=mem)
        hp = _ffn(x3, *ffn2, gfin, last)
        proj_b = proj.reshape(Bp, Tp, d_in_pad)
        outs[0].append(s_new)
        outs[1].append(proj_b[:, Tp - (GDN_CONV - 1) :, :CONV_CH])
        outs[2].append(kc.reshape(Bp, WINDOW, SWA_KV_HEADS, SWA_HD))
        outs[3].append(proj_b[:, Tp - WINDOW :, off_kv + SWA_KV_W : off_kv + 2 * SWA_KV_W].reshape(Bp, WINDOW, SWA_KV_HEADS, SWA_HD))
        outs[4].append(mk.reshape(Bp, n_mem, MEM_HEADS, MEM_HD))
        outs[5].append(mv.reshape(Bp, n_mem, MEM_HEADS, MEM_HD))

        x1 = _ffn(hs, *ffn1, gfin, False)
        proj = _norm_matmul(x1, row(norm_mix[l]), w_in_rt, tn_in, w_transposed=True)
        proj3 = proj.reshape(Bs, 1, d_in_pad)
        og, s_new = _gdn_step(proj3, off_ba, state_conv[l], state_gdn[l], *gdn_common)
        q_raw = proj[:, off_q : off_q + SWA_Q_W].reshape(Bs, SWA_KV_HEADS, group, 1, SWA_HD)
        q_exp = (q_raw * eye_kv[None, :, None, :, None]).reshape(Bs, SWA_HEADS, SWA_KV_W)
        ck = jnp.swapaxes(cache_swa_k[l].reshape(Bs, WINDOW, SWA_KV_W), 1, 2)
        cv = jnp.swapaxes(cache_swa_v[l].reshape(Bs, WINDOW, SWA_KV_W), 1, 2)
        r, nk, nv = _swa_step(q_exp, proj3, off_kv, ck, cv, cos_s, sin_s, swa_sinks[l].reshape(SWA_HEADS, 1))
        r5 = r.reshape(Bs, SWA_KV_HEADS, group, SWA_KV_HEADS, SWA_HD)
        kvh = jnp.arange(SWA_KV_HEADS)
        os_ = jnp.transpose(r5[:, kvh, :, kvh, :], (1, 0, 2, 3)).reshape(Bs, SWA_Q_W).astype(bf16)
        x2, qm = _merge(og.reshape(Bs, GDN_W), os_, proj, OFF_GG, x1, *merge_w)
        q8 = jnp.tile(qm.reshape(Bs, MEM_HEADS, MEM_HD), (1, 2, 1))
        om = _mem_attn_step(q8, cache_mem_k[l].reshape(Bs, n_mem * MEM_HEADS, MEM_HD),
                            cache_mem_v[l].reshape(Bs, n_mem * MEM_HEADS, MEM_HD))
        x3 = _proj_residual(om[:, :MEM_HEADS].reshape(Bs, MEM_W), wmo, x2)
        hs = _ffn(x3, *ffn2, gfin, last)
        outs[6].append(s_new)
        outs[7].append(jnp.concatenate([state_conv[l][:, 1:], proj3[:, :, :CONV_CH]], axis=1))
        outs[8].append(jnp.swapaxes(nk, 1, 2).reshape(Bs, WINDOW, SWA_KV_HEADS, SWA_HD))
        outs[9].append(jnp.swapaxes(nv, 1, 2).reshape(Bs, WINDOW, SWA_KV_HEADS, SWA_HD))

    return (hp.reshape(Bp, Tp, D), hs.reshape(Bs, Ts, D), *(jnp.stack(o) for o in outs))
```

```python
import functools
import math

import jax
import jax.numpy as jnp
from jax import lax
from jax.experimental import pallas as pl
from jax.experimental.pallas import tpu as pltpu

f32 = jnp.float32
bf16 = jnp.bfloat16

PAST_LEN = 16384
GDN_HEADS = 8
GDN_D = 128
GDN_CONV = 4
SWA_HEADS = 16
SWA_KV_HEADS = 4
SWA_HD = 64
WINDOW = 128
ROPE_THETA = 10000.0
MEM_HEADS = 4
MEM_HD = 128
EPS = 1e-6
L2_EPS = 1e-6

LANES = 128
SUBLANES = 8
CHUNK = 128
GDN_STEP_CHUNKS = 2
FFN_SLAB = 64
PROJ_TN = 768
VMEM_LIMIT = 58 * 1024 * 1024

GDN_W = GDN_HEADS * GDN_D
CONV_CH = 3 * GDN_W
SWA_Q_W = SWA_HEADS * SWA_HD
SWA_KV_W = SWA_KV_HEADS * SWA_HD
MEM_W = MEM_HEADS * MEM_HD

OFF_Z = CONV_CH
OFF_GG = OFF_Z + GDN_W


def _cparams(semantics):
    return pltpu.CompilerParams(dimension_semantics=semantics, vmem_limit_bytes=VMEM_LIMIT)


def _dot(a, b):
    return jnp.dot(a, b, preferred_element_type=f32)


def _dot_nt(a, b):
    return lax.dot_general(a, b, (((1,), (1,)), ((), ())), preferred_element_type=f32)


def _rmsnorm(x, g):
    return x * lax.rsqrt(jnp.mean(x * x, -1, keepdims=True) + EPS) * g


def _silu(x):
    return x * jax.nn.sigmoid(x)


def _softplus(x):
    return jnp.maximum(x, 0.0) + jnp.log1p(jnp.exp(-jnp.abs(x)))


def _ffn_kernel(x_ref, g_ref, w1_ref, w3_ref, w2_ref, gf_ref, o_ref, h_ref, *, final_norm):
    j = pl.program_id(1)
    slab = min(FFN_SLAB, x_ref.shape[0])
    n_slabs = x_ref.shape[0] // slab

    def over_slabs(body):
        def step(s, carry):
            body(pl.ds(pl.multiple_of(s * slab, slab), slab))
            return carry
        lax.fori_loop(0, n_slabs, step, 0, unroll=min(4, n_slabs))

    @pl.when(j == 0)
    def _():
        def prologue(rows):
            h_ref[rows, :] = _rmsnorm(x_ref[rows, :], g_ref[...]).astype(bf16)
            o_ref[rows, :] = jnp.zeros((slab, o_ref.shape[1]), f32)
        over_slabs(prologue)

    h = h_ref[...]
    a = _dot(h, w1_ref[...])
    b = _dot(h, w3_ref[...])
    o_ref[...] += _dot((_silu(a) * b).astype(bf16), w2_ref[...])

    @pl.when(j == pl.num_programs(1) - 1)
    def _():
        def epilogue(rows):
            y = x_ref[rows, :] + 0.5 * o_ref[rows, :]
            if final_norm:
                y = _rmsnorm(y, gf_ref[...])
            o_ref[rows, :] = y
        over_slabs(epilogue)


def _ffn(x, g, w1, w3, w2, gf, final_norm):
    M, D = x.shape
    F = w1.shape[1]
    tm = min(1024, M)
    tf = 512
    assert M % tm == 0 and F % tf == 0
    return pl.pallas_call(
        functools.partial(_ffn_kernel, final_norm=final_norm),
        grid=(M // tm, F // tf),
        in_specs=[
            pl.BlockSpec((tm, D), lambda i, j: (i, 0)),
            pl.BlockSpec((1, D), lambda i, j: (0, 0)),
            pl.BlockSpec((D, tf), lambda i, j: (0, j)),
            pl.BlockSpec((D, tf), lambda i, j: (0, j)),
            pl.BlockSpec((tf, D), lambda i, j: (j, 0)),
            pl.BlockSpec((1, D), lambda i, j: (0, 0)),
        ],
        out_specs=pl.BlockSpec((tm, D), lambda i, j: (i, 0)),
        out_shape=jax.ShapeDtypeStruct((M, D), f32),
        scratch_shapes=[pltpu.VMEM((tm, D), bf16)],
        compiler_params=_cparams(("parallel", "arbitrary")),
        name="ffn",
    )(x, g, w1, w3, w2, gf)


def _norm_matmul_kernel(x_ref, g_ref, w_ref, o_ref, h_ref, *, w_transposed):
    @pl.when(pl.program_id(1) == 0)
    def _():
        h_ref[...] = _rmsnorm(x_ref[...], g_ref[...]).astype(bf16)

    o_ref[...] = (_dot_nt if w_transposed else _dot)(h_ref[...], w_ref[...])


def _norm_matmul(x, g, w, tn, w_transposed=False):
    M, D = x.shape
    N = w.shape[0] if w_transposed else w.shape[1]
    tm = min(1024, M)
    assert M % tm == 0 and N % tn == 0
    w_spec = pl.BlockSpec((tn, D), lambda i, j: (j, 0)) if w_transposed else pl.BlockSpec((D, tn), lambda i, j: (0, j))
    return pl.pallas_call(
        functools.partial(_norm_matmul_kernel, w_transposed=w_transposed),
        grid=(M // tm, N // tn),
        in_specs=[
            pl.BlockSpec((tm, D), lambda i, j: (i, 0)),
            pl.BlockSpec((1, D), lambda i, j: (0, 0)),
            w_spec,
        ],
        out_specs=pl.BlockSpec((tm, tn), lambda i, j: (i, j)),
        out_shape=jax.ShapeDtypeStruct((M, N), f32),
        scratch_shapes=[pltpu.VMEM((tm, D), bf16)],
        compiler_params=_cparams(("parallel", "arbitrary")),
        name="norm_matmul",
    )(x, g, w)


def _mem_kv_kernel(x_ref, g_ref, wk_ref, wv_ref, k_ref, v_ref, kb_ref, vb_ref):
    h = _rmsnorm(x_ref[...], g_ref[...]).astype(bf16)
    k = _dot(h, wk_ref[...])
    v = _dot(h, wv_ref[...])
    kb_ref[...] = k.astype(bf16)
    vb_ref[...] = v.astype(bf16)
    tokens = x_ref.shape[0]
    for hd in range(MEM_HEADS):
        k_ref[pl.ds(hd, tokens, stride=MEM_HEADS), :] = k[:, hd * MEM_HD : (hd + 1) * MEM_HD]
        v_ref[pl.ds(hd, tokens, stride=MEM_HEADS), :] = v[:, hd * MEM_HD : (hd + 1) * MEM_HD]


def _mem_kv(x, g, wk, wv):
    M, D = x.shape
    N = wk.shape[1]
    tm = min(1024, M)
    assert M % tm == 0 and N == MEM_W
    row_spec = pl.BlockSpec((tm, N), lambda i: (i, 0))
    head_row_spec = pl.BlockSpec((tm * MEM_HEADS, MEM_HD), lambda i: (i, 0))
    const = lambda shape: pl.BlockSpec(shape, lambda i: (0, 0))
    return pl.pallas_call(
        _mem_kv_kernel,
        grid=(M // tm,),
        in_specs=[pl.BlockSpec((tm, D), lambda i: (i, 0)), const((1, D)), const((D, N)), const((D, N))],
        out_specs=[head_row_spec] * 2 + [row_spec] * 2,
        out_shape=[jax.ShapeDtypeStruct((M * MEM_HEADS, MEM_HD), f32)] * 2 + [jax.ShapeDtypeStruct((M, N), bf16)] * 2,
        compiler_params=_cparams(("parallel",)),
        name="mem_kv",
    )(x, g, wk, wv)


def _bdot(a, b):
    return _dot(a.astype(bf16), b.astype(bf16))


def _unit_lower_inverse(nmats, row, col):
    eye = jnp.where(row == col, 1.0, 0.0).astype(f32)

    def same_block(size):
        return (row // size) == (col // size)

    blk = same_block(SUBLANES)
    a1 = [jnp.where(blk, n, 0.0) for n in nmats]
    a2 = [_bdot(a, a) for a in a1]
    a4 = [_bdot(a, a) for a in a2]
    ts = [_bdot(eye - x1, eye + x2) for x1, x2 in zip(a1, a2)]
    ts = [_bdot(t, eye + x4) for t, x4 in zip(ts, a4)]
    size = SUBLANES
    while size < CHUNK:
        nxt = same_block(2 * size)
        sel = jnp.logical_and(nxt, jnp.logical_not(blk))
        tbs = [t.astype(bf16) for t in ts]
        xs = [_dot(jnp.where(sel, n, 0.0).astype(bf16), tb) for n, tb in zip(nmats, tbs)]
        ts = [t - _dot(tb, x.astype(bf16)) for t, tb, x in zip(ts, tbs, xs)]
        blk = nxt
        size *= 2
    return ts


def _conv_silu_slab(x_ref, top_ref, w_ref, sl):
    rows = x_ref.shape[0]
    y = None
    for tap in range(GDN_CONV):
        back = GDN_CONV - 1 - tap
        window = jnp.concatenate(
            [top_ref[SUBLANES - back : 2 * SUBLANES - back, sl], x_ref[SUBLANES - back : rows - back, sl]], axis=0)
        term = window * w_ref[tap : tap + 1, sl]
        y = term if y is None else y + term
    return _silu(y)


def _delta_rule_prepare(pairs, gcum, gcum_t, beta_t, row, col, q_ref, k_ref, v_ref):
    n = range(len(pairs))
    incl = row >= col
    strict = row > col
    rows_of = lambda ci: slice(ci * CHUNK, (ci + 1) * CHUNK)
    gc = [gcum[rows_of(ci), GDN_HEADS + h : GDN_HEADS + h + 1] for ci, h in pairs]
    gr = [gcum_t[ci][GDN_HEADS + h : GDN_HEADS + h + 1, :] for ci, h in pairs]
    beta = [beta_t[rows_of(ci), h : h + 1] for ci, h in pairs]
    k = [k_ref[h, rows_of(ci), :] for ci, h in pairs]
    kb = [x.astype(bf16) for x in k]
    kk = [_dot_nt(kb[i], kb[i]) for i in n]
    qk = [_dot_nt(q_ref[h, rows_of(ci), :].astype(bf16), kb[i]) for i, (ci, h) in enumerate(pairs)]
    e = [jnp.exp(jnp.where(incl, gc[i] - gr[i], 0.0)) for i in n]
    nmat = [beta[i] * kk[i] * jnp.where(strict, e[i], 0.0) for i in n]
    qkd = [(qk[i] * jnp.where(incl, e[i], 0.0)).astype(bf16) for i in n]
    tinv = _unit_lower_inverse(nmat, row, col)
    gamma = [jnp.exp(gc[i]) for i in n]
    rhs = [jnp.concatenate([beta[i] * v_ref[h, rows_of(ci), :], (beta[i] * gamma[i]) * k[i]], axis=-1).astype(bf16)
           for i, (ci, h) in enumerate(pairs)]
    sol = [_dot(tinv[i].astype(bf16), rhs[i]) for i in n]
    return gc, gamma, sol, qkd


def _delta_rule_apply(ci, prepared, q_ref, k_ref, z_ref, gn_ref, s_ref, og_ref):
    gc, gamma, sol, qkd = prepared
    heads = range(GDN_HEADS)
    rows = slice(ci * CHUNK, (ci + 1) * CHUNK)
    sb = [s_ref[h].astype(bf16) for h in heads]
    ub = [(sol[h][:, :GDN_D] - _dot(sol[h][:, GDN_D:].astype(bf16), sb[h])).astype(bf16) for h in heads]
    o = [_dot((q_ref[h, rows, :] * gamma[h]).astype(bf16), sb[h]) + _dot(qkd[h], ub[h]) for h in heads]
    for h in heads:
        g_last = gc[h][CHUNK - 1 : CHUNK, :]
        k_end = k_ref[h, rows, :] * jnp.exp(g_last - gc[h])
        s_ref[h] = jnp.exp(g_last) * s_ref[h] + _dot(k_end.T.astype(bf16), ub[h])
        zh = z_ref[rows, h * GDN_D : (h + 1) * GDN_D]
        og_ref[rows, h * GDN_D : (h + 1) * GDN_D] = (_rmsnorm(o[h], gn_ref[...]) * _silu(zh)).astype(bf16)


def _gdn_prompt_kernel(*refs, n_cast):
    qkv_ref, z_ref, ba_ref, cw_ref, av_ref, dv_ref, gn_ref = refs[:7]
    cast_in = refs[7 : 7 + n_cast]
    og_ref, so_ref = refs[7 + n_cast : 9 + n_cast]
    cast_out = refs[9 + n_cast : 9 + 2 * n_cast]
    s_ref, carry_ref, q_ref, k_ref, v_ref = refs[9 + 2 * n_cast :]
    for src, dst in zip(cast_in, cast_out):
        dst[...] = src[...].astype(bf16)

    c = pl.program_id(1)
    rows = qkv_ref.shape[0]
    n_chunks = rows // CHUNK

    @pl.when(c == 0)
    def _():
        s_ref[...] = jnp.zeros_like(s_ref)
        carry_ref[0:SUBLANES, :] = jnp.zeros((SUBLANES, CONV_CH), f32)

    carry_ref[SUBLANES : 2 * SUBLANES, :] = qkv_ref[0:SUBLANES, :]
    for j in range(3 * GDN_HEADS):
        sl = slice(j * LANES, (j + 1) * LANES)
        y = _conv_silu_slab(qkv_ref, carry_ref, cw_ref, sl)
        h = j % GDN_HEADS
        if j < 2 * GDN_HEADS:
            y = y * lax.rsqrt(jnp.sum(y * y, -1, keepdims=True) + L2_EPS)
            if j < GDN_HEADS:
                q_ref[h] = y * (GDN_D ** -0.5)
            else:
                k_ref[h] = y
        else:
            v_ref[h] = y
    carry_ref[0:SUBLANES, :] = qkv_ref[rows - SUBLANES : rows, :]

    ba = ba_ref[...]
    beta_t = jax.nn.sigmoid(ba)
    g_t = -jnp.exp(av_ref[...]) * _softplus(ba + dv_ref[...])
    row_in_chunk = lax.broadcasted_iota(jnp.int32, (rows, LANES), 0) % CHUNK
    gcum = g_t
    shift = 1
    while shift < CHUNK:
        gcum = gcum + jnp.where(row_in_chunk >= shift, pltpu.roll(gcum, shift, 0), 0.0)
        shift *= 2
    gcum_t = [gcum[ci * CHUNK : (ci + 1) * CHUNK, :].T for ci in range(n_chunks)]

    row = lax.broadcasted_iota(jnp.int32, (CHUNK, CHUNK), 0)
    col = lax.broadcasted_iota(jnp.int32, (CHUNK, CHUNK), 1)
    pairs = [(ci, h) for ci in range(n_chunks) for h in range(GDN_HEADS)]
    gc, gamma, sol, qkd = _delta_rule_prepare(pairs, gcum, gcum_t, beta_t, row, col, q_ref, k_ref, v_ref)
    for ci in range(n_chunks):
        mine = slice(ci * GDN_HEADS, (ci + 1) * GDN_HEADS)
        _delta_rule_apply(ci, (gc[mine], gamma[mine], sol[mine], qkd[mine]), q_ref, k_ref, z_ref, gn_ref, s_ref, og_ref)

    @pl.when(c == pl.num_programs(1) - 1)
    def _():
        so_ref[0] = s_ref[...]


def _cast_block(shape, steps):
    R, C = shape
    for col_blocks in (1, 2, 4, 8, 16):
        row_blocks = steps // col_blocks
        if (steps % col_blocks == 0 and R % row_blocks == 0 and C % col_blocks == 0
                and (R // row_blocks) % (2 * SUBLANES) == 0 and (C // col_blocks) % LANES == 0):
            return (R // row_blocks, C // col_blocks), col_blocks
    return None


def _gdn_prompt(proj, B, T, off_ba, cw, avec, dvec, gn, weights):
    rows = GDN_STEP_CHUNKS * CHUNK
    assert T % rows == 0
    ns = T // rows
    step = lambda b, c: b * ns + c
    plans = [_cast_block(w.shape, B * ns) for w in weights]
    riders = [w for w, p in zip(weights, plans) if p is not None]
    cast_specs = [pl.BlockSpec(blk, functools.partial(lambda b, c, cb: (step(b, c) // cb, step(b, c) % cb), cb=cb))
                  for blk, cb in (p for p in plans if p is not None)]
    head_scratch = pltpu.VMEM((GDN_HEADS, rows, GDN_D), f32)
    out = pl.pallas_call(
        functools.partial(_gdn_prompt_kernel, n_cast=len(riders)),
        grid=(B, ns),
        in_specs=[
            pl.BlockSpec((rows, CONV_CH), lambda b, c: (step(b, c), 0)),
            pl.BlockSpec((rows, GDN_W), lambda b, c: (step(b, c), OFF_Z // GDN_W)),
            pl.BlockSpec((rows, LANES), lambda b, c: (step(b, c), off_ba // LANES)),
            pl.BlockSpec((GDN_CONV, CONV_CH), lambda b, c: (0, 0)),
            pl.BlockSpec((1, LANES), lambda b, c: (0, 0)),
            pl.BlockSpec((1, LANES), lambda b, c: (0, 0)),
            pl.BlockSpec((1, GDN_D), lambda b, c: (0, 0)),
        ] + cast_specs,
        out_specs=[
            pl.BlockSpec((rows, GDN_W), lambda b, c: (step(b, c), 0)),
            pl.BlockSpec((1, GDN_HEADS, GDN_D, GDN_D), lambda b, c: (b, 0, 0, 0)),
        ] + cast_specs,
        out_shape=[
            jax.ShapeDtypeStruct((B * T, GDN_W), bf16),
            jax.ShapeDtypeStruct((B, GDN_HEADS, GDN_D, GDN_D), f32),
        ] + [jax.ShapeDtypeStruct(w.shape, bf16) for w in riders],
        scratch_shapes=[
            pltpu.VMEM((GDN_HEADS, GDN_D, GDN_D), f32),
            pltpu.VMEM((2 * SUBLANES, CONV_CH), f32),
            head_scratch, head_scratch, head_scratch,
        ],
        compiler_params=_cparams(("arbitrary", "arbitrary")),
        name="gdn_prompt",
    )(proj, proj, proj, cw, avec, dvec, gn, *riders)
    converted = iter(out[2:])
    return out[0], out[1], [next(converted) if p is not None else w.astype(bf16) for w, p in zip(weights, plans)]


def _gdn_step_kernel(qkv_ref, z_ref, ba_ref, sc_ref, s0_ref, cw_ref, av_ref, dv_ref, gn_ref, og_ref, so_ref):
    x_new = qkv_ref[0]
    taps = cw_ref[...]
    y = jnp.sum(sc_ref[0] * taps[0 : GDN_CONV - 1, :], axis=0, keepdims=True) + x_new * taps[GDN_CONV - 1 : GDN_CONV, :]
    y = _silu(y)
    ba = ba_ref[0]
    beta_t = jax.nn.sigmoid(ba)
    gamma_t = jnp.exp(-jnp.exp(av_ref[...]) * _softplus(ba + dv_ref[...]))
    z = z_ref[0]
    heads = range(GDN_HEADS)
    head = lambda base, h: y[:, base + h * GDN_D : base + (h + 1) * GDN_D]
    q = [head(0, h) for h in heads]
    k = [head(GDN_W, h) for h in heads]
    v = [head(2 * GDN_W, h) for h in heads]
    q = [q[h] * lax.rsqrt(jnp.sum(q[h] * q[h], -1, keepdims=True) + L2_EPS) * (GDN_D ** -0.5) for h in heads]
    k = [k[h] * lax.rsqrt(jnp.sum(k[h] * k[h], -1, keepdims=True) + L2_EPS) for h in heads]
    beta = [beta_t[:, h : h + 1] for h in heads]
    gamma = [gamma_t[:, GDN_HEADS + h : GDN_HEADS + h + 1] for h in heads]
    k_col = [jnp.broadcast_to(k[h], (GDN_D, GDN_D)).T for h in heads]
    sub = lax.broadcasted_iota(jnp.int32, (SUBLANES, GDN_D), 0)
    kq = [jnp.where(sub == 0, k[h], jnp.where(sub == 1, q[h], 0.0)).astype(bf16) for h in heads]
    kq_s = [_dot(kq[h], s0_ref[0, h].astype(bf16)) for h in heads]
    k_s = [kq_s[h][0:1] for h in heads]
    q_s = [kq_s[h][1:2] for h in heads]
    u = [beta[h] * v[h] - (beta[h] * gamma[h]) * k_s[h] for h in heads]
    o = [gamma[h] * q_s[h] + jnp.sum(q[h] * k[h], -1, keepdims=True) * u[h] for h in heads]
    for h in heads:
        so_ref[0, h] = gamma[h] * s0_ref[0, h] + k_col[h] * u[h]
        zh = z[:, h * GDN_D : (h + 1) * GDN_D]
        og_ref[0, :, h * GDN_D : (h + 1) * GDN_D] = (_rmsnorm(o[h], gn_ref[...]) * _silu(zh)).astype(bf16)


def _gdn_step(proj3, off_ba, state_conv, state_gdn, cw, avec, dvec, gn):
    B = proj3.shape[0]
    return pl.pallas_call(
        _gdn_step_kernel,
        grid=(B,),
        in_specs=[
            pl.BlockSpec((1, 1, CONV_CH), lambda b: (b, 0, 0)),
            pl.BlockSpec((1, 1, GDN_W), lambda b: (b, 0, OFF_Z // GDN_W)),
            pl.BlockSpec((1, 1, LANES), lambda b: (b, 0, off_ba // LANES)),
            pl.BlockSpec((1, GDN_CONV - 1, CONV_CH), lambda b: (b, 0, 0)),
            pl.BlockSpec((1, GDN_HEADS, GDN_D, GDN_D), lambda b: (b, 0, 0, 0)),
            pl.BlockSpec((GDN_CONV, CONV_CH), lambda b: (0, 0)),
            pl.BlockSpec((1, LANES), lambda b: (0, 0)),
            pl.BlockSpec((1, LANES), lambda b: (0, 0)),
            pl.BlockSpec((1, GDN_D), lambda b: (0, 0)),
        ],
        out_specs=[
            pl.BlockSpec((1, 1, GDN_W), lambda b: (b, 0, 0)),
            pl.BlockSpec((1, GDN_HEADS, GDN_D, GDN_D), lambda b: (b, 0, 0, 0)),
        ],
        out_shape=[
            jax.ShapeDtypeStruct((B, 1, GDN_W), bf16),
            jax.ShapeDtypeStruct((B, GDN_HEADS, GDN_D, GDN_D), f32),
        ],
        compiler_params=_cparams(("parallel",)),
        name="gdn_step",
    )(proj3, proj3, proj3, state_conv, state_gdn, cw, avec, dvec, gn)


def _rope(x, cos, sin_signed):
    width = x.shape[-1]
    lane = lax.broadcasted_iota(jnp.int32, x.shape, x.ndim - 1)
    first_half = (lane % SWA_HD) < (SWA_HD // 2)
    rot = jnp.where(first_half, pltpu.roll(x, width - SWA_HD // 2, x.ndim - 1), pltpu.roll(x, SWA_HD // 2, x.ndim - 1))
    return x * cos + rot * sin_signed


def _head_halves(x2, head_parity, lane):
    swapped = pltpu.roll(x2, SWA_HD, 1)
    lo_src, hi_src = (x2, swapped) if head_parity == 0 else (swapped, x2)
    return jnp.where(lane < SWA_HD, lo_src, 0.0), jnp.where(lane >= SWA_HD, hi_src, 0.0)


def _swa_prompt_kernel(sinks_ref, q_ref, kv_ref, cos_ref, sin_ref, os_ref, kc_ref, kprev_ref, vprev_ref):
    n = pl.program_id(1)

    @pl.when(n == 0)
    def _():
        kprev_ref[...] = jnp.zeros_like(kprev_ref)
        vprev_ref[...] = jnp.zeros_like(vprev_ref)

    cos = cos_ref[...]
    sin = sin_ref[...]
    kv = kv_ref[...]
    k_cur = _rope(kv[:, :SWA_KV_W], cos, sin)
    v_cur = kv[:, SWA_KV_W:]
    kc_ref[0] = k_cur

    row = lax.broadcasted_iota(jnp.int32, (WINDOW, WINDOW), 0)
    col = lax.broadcasted_iota(jnp.int32, (WINDOW, WINDOW), 1)
    own = col <= row
    prev_bias = jnp.where(n > 0, 0.0, -jnp.inf)
    lane = lax.broadcasted_iota(jnp.int32, (WINDOW, LANES), 1)
    scale = SWA_HD ** -0.5
    group = SWA_HEADS // SWA_KV_HEADS
    k_own, k_pre, v_own, v_pre, q2 = [], [], [], [], []
    for h in range(SWA_KV_HEADS):
        pair = slice((h // 2) * LANES, (h // 2 + 1) * LANES)
        k_own.append([a.astype(bf16) for a in _head_halves(k_cur[:, pair], h % 2, lane)])
        v_own.append([a.astype(bf16) for a in _head_halves(v_cur[:, pair], h % 2, lane)])
        k_pre.append([kprev_ref[h, par] for par in range(2)])
        v_pre.append([vprev_ref[h, par] for par in range(2)])
        for par in range(2):
            kprev_ref[h, par] = k_own[h][par]
            vprev_ref[h, par] = v_own[h][par]
        q_h = _rope(q_ref[:, h * group * SWA_HD : (h + 1) * group * SWA_HD], cos, sin) * scale
        q2.append([q_h[:, j * LANES : (j + 1) * LANES].astype(bf16) for j in range(group // 2)])
    heads = [(h, j, par) for h in range(SWA_KV_HEADS) for j in range(group // 2) for par in range(2)]
    idx = range(len(heads))
    sink = [sinks_ref[h * group + 2 * j + par] for h, j, par in heads]
    s = [jnp.where(own, _dot_nt(q2[h][j], k_own[h][par]), _dot_nt(q2[h][j], k_pre[h][par]) + prev_bias)
         for h, j, par in heads]
    m = [jnp.maximum(jnp.max(s[i], -1, keepdims=True), sink[i]) for i in idx]
    p = [jnp.exp(s[i] - m[i]) for i in idx]
    inv = [1.0 / (jnp.sum(p[i], -1, keepdims=True) + jnp.exp(sink[i] - m[i])) for i in idx]
    o = [(_dot(jnp.where(own, p[i], 0.0).astype(bf16), v_own[h][par])
          + _dot(jnp.where(own, 0.0, p[i]).astype(bf16), v_pre[h][par])) * inv[i] for i, (h, j, par) in enumerate(heads)]
    for i in range(0, len(heads), 2):
        os_ref[:, i * SWA_HD : i * SWA_HD + LANES] = (o[i] + o[i + 1]).astype(bf16)


def _swa_prompt(proj, B, T, off_q, off_kv, cos, sin, sinks):
    assert T % WINDOW == 0 and SWA_KV_W == 2 * LANES
    nb = T // WINDOW
    return pl.pallas_call(
        _swa_prompt_kernel,
        grid=(B, nb),
        in_specs=[
            pl.BlockSpec(memory_space=pltpu.SMEM),
            pl.BlockSpec((WINDOW, SWA_Q_W), lambda b, n: (b * nb + n, off_q // SWA_Q_W)),
            pl.BlockSpec((WINDOW, 2 * SWA_KV_W), lambda b, n: (b * nb + n, off_kv // (2 * SWA_KV_W))),
            pl.BlockSpec((WINDOW, SWA_KV_W), lambda b, n: (n, 0)),
            pl.BlockSpec((WINDOW, SWA_KV_W), lambda b, n: (n, 0)),
        ],
        out_specs=[
            pl.BlockSpec((WINDOW, SWA_Q_W), lambda b, n: (b * nb + n, 0)),
            pl.BlockSpec((1, WINDOW, SWA_KV_W), lambda b, n: (b, 0, 0)),
        ],
        out_shape=[
            jax.ShapeDtypeStruct((B * T, SWA_Q_W), bf16),
            jax.ShapeDtypeStruct((B, WINDOW, SWA_KV_W), f32),
        ],
        scratch_shapes=[pltpu.VMEM((SWA_KV_HEADS, 2, WINDOW, LANES), bf16)] * 2,
        compiler_params=_cparams(("parallel", "arbitrary")),
        name="swa_prompt",
    )(sinks, proj, proj, cos, sin)


SWA_STEP_BATCH = 8


def _swa_step_kernel(qe_ref, kv_ref, ck_ref, cv_ref, cos_ref, sin_ref, sinks_ref, r_ref, nk_ref, nv_ref):
    cos = cos_ref[...]
    sin = sin_ref[...]
    sink = sinks_ref[...]
    row = lax.broadcasted_iota(jnp.int32, (WINDOW, SWA_KV_W), 0)
    scale = SWA_HD ** -0.5
    seqs = range(qe_ref.shape[0])
    kv = [kv_ref[i] for i in seqs]
    k_new = [_rope(kv[i][:, :SWA_KV_W], cos, sin) for i in seqs]
    keys = [jnp.where(row == WINDOW - 1, k_new[i], pltpu.roll(ck_ref[i].T, WINDOW - 1, 0)) for i in seqs]
    vals = [jnp.where(row == WINDOW - 1, kv[i][:, SWA_KV_W:], pltpu.roll(cv_ref[i].T, WINDOW - 1, 0)) for i in seqs]
    for i in seqs:
        nk_ref[i] = keys[i].T
        nv_ref[i] = vals[i].T
    q = [_rope(qe_ref[i], cos, sin) for i in seqs]
    s = [_dot_nt(q[i].astype(bf16), keys[i].astype(bf16)) * scale for i in seqs]
    m = [jnp.maximum(jnp.max(s[i], -1, keepdims=True), sink) for i in seqs]
    p = [jnp.exp(s[i] - m[i]) for i in seqs]
    denom = [jnp.sum(p[i], -1, keepdims=True) + jnp.exp(sink - m[i]) for i in seqs]
    for i in seqs:
        r_ref[i] = _dot((p[i] / denom[i]).astype(bf16), vals[i].astype(bf16))


def _swa_step(q_exp, proj3, off_kv, cache_k, cache_v, cos, sin, sinks_col):
    B = q_exp.shape[0]
    bb = math.gcd(B, SWA_STEP_BATCH)
    assert cache_k.shape[2] == WINDOW
    return pl.pallas_call(
        _swa_step_kernel,
        grid=(B // bb,),
        in_specs=[
            pl.BlockSpec((bb, SWA_HEADS, SWA_KV_W), lambda i: (i, 0, 0)),
            pl.BlockSpec((bb, 1, 2 * SWA_KV_W), lambda i: (i, 0, off_kv // (2 * SWA_KV_W))),
            pl.BlockSpec((bb, SWA_KV_W, WINDOW), lambda i: (i, 0, 0)),
            pl.BlockSpec((bb, SWA_KV_W, WINDOW), lambda i: (i, 0, 0)),
            pl.BlockSpec((1, SWA_KV_W), lambda i: (0, 0)),
            pl.BlockSpec((1, SWA_KV_W), lambda i: (0, 0)),
            pl.BlockSpec((SWA_HEADS, 1), lambda i: (0, 0)),
        ],
        out_specs=[
            pl.BlockSpec((bb, SWA_HEADS, SWA_KV_W), lambda i: (i, 0, 0)),
            pl.BlockSpec((bb, SWA_KV_W, WINDOW), lambda i: (i, 0, 0)),
            pl.BlockSpec((bb, SWA_KV_W, WINDOW), lambda i: (i, 0, 0)),
        ],
        out_shape=[
            jax.ShapeDtypeStruct((B, SWA_HEADS, SWA_KV_W), f32),
            jax.ShapeDtypeStruct((B, SWA_KV_W, WINDOW), f32),
            jax.ShapeDtypeStruct((B, SWA_KV_W, WINDOW), f32),
        ],
        compiler_params=_cparams(("parallel",)),
        name="swa_step",
    )(q_exp, proj3, cache_k, cache_v, cos, sin, sinks_col)


def _merge_core(og_ref, os_ref, gg_ref, gs_ref, x_ref, wg_ref, ws_ref, wo_ref, gq_ref, wq_ref):
    p_gdn = _dot(og_ref[...], wg_ref[...])
    p_swa = _dot(os_ref[...], ws_ref[...])
    merged = jax.nn.sigmoid(gg_ref[...]) * p_gdn + jax.nn.sigmoid(gs_ref[...]) * p_swa
    x_new = x_ref[...] + _dot(merged.astype(bf16), wo_ref[...])
    return x_new, _dot(_rmsnorm(x_new, gq_ref[...]).astype(bf16), wq_ref[...])


def _merge_kernel(*refs):
    xo_ref, qm_ref = refs[-2:]
    xo_ref[...], qm_ref[...] = _merge_core(*refs[:-2])


def _merge_mem_kernel(*refs):
    mk_ref, mv_ref, wmo_ref, xo_ref = refs[-4:]
    x_new, q = _merge_core(*refs[:-4])
    q = q.astype(bf16)
    scale = MEM_HD ** -0.5
    heads = range(MEM_HEADS)
    cols = lambda h: slice(h * MEM_HD, (h + 1) * MEM_HD)
    s = [_dot_nt(q[:, cols(h)], mk_ref[0, :, cols(h)]) * scale for h in heads]
    p = [jnp.exp(s[h] - jnp.max(s[h], -1, keepdims=True)) for h in heads]
    p = [p[h] / jnp.sum(p[h], -1, keepdims=True) for h in heads]
    o = [_dot(p[h].astype(bf16), mv_ref[0, :, cols(h)]) for h in heads]
    xo_ref[...] = x_new + _dot(jnp.concatenate(o, axis=-1).astype(bf16), wmo_ref[...])


def _merge(og, os_, proj, off_gg, x, wg, ws, wo, gq, wq, mem=None):
    M, D = x.shape
    tm = min(256, M)
    assert M % tm == 0 and off_gg % D == 0
    const = lambda shape: pl.BlockSpec(shape, lambda i: (0, 0), pipeline_mode=pl.Buffered(1))
    in_specs = [
        pl.BlockSpec((tm, GDN_W), lambda i: (i, 0)),
        pl.BlockSpec((tm, SWA_Q_W), lambda i: (i, 0)),
        pl.BlockSpec((tm, D), lambda i: (i, off_gg // D)),
        pl.BlockSpec((tm, D), lambda i: (i, off_gg // D + 1)),
        pl.BlockSpec((tm, D), lambda i: (i, 0)),
        const((GDN_W, D)),
        const((SWA_Q_W, D)),
        const((D, D)),
        const((1, D)),
        const((D, MEM_W)),
    ]
    args = [og, os_, proj, proj, x, wg, ws, wo, gq, wq]
    x_spec = pl.BlockSpec((tm, D), lambda i: (i, 0))
    x_shape = jax.ShapeDtypeStruct((M, D), f32)
    if mem is None:
        return pl.pallas_call(
            _merge_kernel,
            grid=(M // tm,),
            in_specs=in_specs,
            out_specs=[x_spec, pl.BlockSpec((tm, MEM_W), lambda i: (i, 0))],
            out_shape=[x_shape, jax.ShapeDtypeStruct((M, MEM_W), f32)],
            compiler_params=_cparams(("parallel",)),
            name="merge",
        )(*args)
    mem_k, mem_v, wmo, seq_rows = mem
    assert seq_rows % tm == 0
    mt = mem_k.shape[1]
    mem_spec = pl.BlockSpec((1, mt, MEM_W), lambda i: (i // (seq_rows // tm), 0, 0))
    return pl.pallas_call(
        _merge_mem_kernel,
        grid=(M // tm,),
        in_specs=in_specs + [mem_spec, mem_spec, const((MEM_W, D))],
        out_specs=x_spec,
        out_shape=x_shape,
        compiler_params=_cparams(("parallel",)),
        name="merge_mem",
    )(*args, mem_k, mem_v, wmo)


MEM_STEP_BATCH = 4


def _mem_attn_step_kernel(q_ref, k_ref, v_ref, o_ref):
    mt2 = k_ref.shape[1] // SUBLANES
    for i in range(q_ref.shape[0]):
        q8 = q_ref[i] * (MEM_HD ** -0.5)
        s = jnp.sum(k_ref[i].reshape(mt2, SUBLANES, MEM_HD) * q8, axis=-1, keepdims=True)
        m = jnp.max(s, axis=0)
        m = jnp.maximum(m, pltpu.roll(m, MEM_HEADS, 0))
        p = jnp.exp(s - m)
        l = jnp.sum(p, axis=0)
        l = l + pltpu.roll(l, MEM_HEADS, 0)
        o = jnp.sum(p * v_ref[i].reshape(mt2, SUBLANES, MEM_HD), axis=0)
        o_ref[i] = (o + pltpu.roll(o, MEM_HEADS, 0)) / l


def _mem_attn_step(q8, mem_k, mem_v):
    B, rows, _ = mem_k.shape
    assert 2 * MEM_HEADS == SUBLANES and rows % SUBLANES == 0
    bb = math.gcd(B, MEM_STEP_BATCH)
    return pl.pallas_call(
        _mem_attn_step_kernel,
        grid=(B // bb,),
        in_specs=[
            pl.BlockSpec((bb, SUBLANES, MEM_HD), lambda i: (i, 0, 0)),
            pl.BlockSpec((bb, rows, MEM_HD), lambda i: (i, 0, 0)),
            pl.BlockSpec((bb, rows, MEM_HD), lambda i: (i, 0, 0)),
        ],
        out_specs=pl.BlockSpec((bb, SUBLANES, MEM_HD), lambda i: (i, 0, 0)),
        out_shape=jax.ShapeDtypeStruct((B, SUBLANES, MEM_HD), f32),
        compiler_params=_cparams(("parallel",)),
        name="mem_attn_step",
    )(q8, mem_k, mem_v)


def _proj_residual_kernel(a_ref, w_ref, x_ref, o_ref):
    o_ref[...] = x_ref[...] + _dot(a_ref[...].astype(bf16), w_ref[...])


def _proj_residual(a, w, x):
    M, D = x.shape
    return pl.pallas_call(
        _proj_residual_kernel,
        out_shape=jax.ShapeDtypeStruct((M, D), f32),
        compiler_params=pltpu.CompilerParams(vmem_limit_bytes=VMEM_LIMIT),
        name="proj_residual",
    )(a, w, x)


def _reorder_rows_kernel(w_ref, o_ref, *, pieces):
    at = 0
    for lo, hi in pieces:
        o_ref[at : at + hi - lo, :] = w_ref[lo:hi, :].astype(bf16)
        at += hi - lo
    o_ref[at:, :] = jnp.zeros((o_ref.shape[0] - at, o_ref.shape[1]), bf16)


def _reorder_rows(wt, layer, pieces, height):
    _, d_in, D = wt.shape
    assert all(lo % (2 * SUBLANES) == 0 and hi % (2 * SUBLANES) == 0 for lo, hi in pieces) and D % LANES == 0
    return pl.pallas_call(
        functools.partial(_reorder_rows_kernel, pieces=pieces),
        grid=(D // LANES,),
        in_specs=[pl.BlockSpec((None, d_in, LANES), lambda i: (layer, 0, i))],
        out_specs=pl.BlockSpec((height, LANES), lambda i: (0, i)),
        out_shape=jax.ShapeDtypeStruct((height, D), bf16),
        compiler_params=_cparams(("parallel",)),
        name="reorder_rows",
    )(wt)


def _rope_tables(pos):
    half = SWA_HD // 2
    inv_freq = ROPE_THETA ** (-jnp.arange(half, dtype=f32) / half)
    ang = pos.astype(f32)[:, None] * inv_freq[None, :]
    cos = jnp.cos(ang)
    sin = jnp.sin(ang)
    reps = SWA_KV_W // SWA_HD
    return jnp.tile(jnp.concatenate([cos, cos], -1), (1, reps)), jnp.tile(jnp.concatenate([-sin, sin], -1), (1, reps))


def kernel(x_prompt, x_sample, state_gdn, state_conv, cache_swa_k, cache_swa_v, cache_mem_k, cache_mem_v, mem_prompt, norm_ffn1, ffn1_w1, ffn1_w3, ffn1_w2, norm_mix, w_in, conv_w, gdn_A_log, gdn_dt_bias, gdn_norm, swa_sinks, w_br_gdn, w_br_swa, w_out, norm_mem_q, norm_mem_kv, w_mem_q, w_mem_k, w_mem_v, w_mem_o, norm_ffn2, ffn2_w1, ffn2_w3, ffn2_w2, norm_final):
    Bp, Tp, D = x_prompt.shape
    Bs, Ts, _ = x_sample.shape
    assert Ts == 1
    depth = norm_ffn1.shape[0]
    n_mem = mem_prompt.shape[1]
    group = SWA_HEADS // SWA_KV_HEADS

    off_gs = OFF_GG + D
    off_q = off_gs + D
    off_kv = off_q + SWA_Q_W
    off_ba = off_kv + 2 * SWA_KV_W
    d_in_pad = -(-(off_ba + LANES) // PROJ_TN) * PROJ_TN
    o_b = CONV_CH + GDN_W
    o_q = o_b + 2 * GDN_HEADS
    o_gg = o_q + SWA_Q_W + 2 * SWA_KV_W

    cos_p, sin_p = _rope_tables(jnp.arange(Tp, dtype=jnp.int32))
    cos_s, sin_s = _rope_tables(PAST_LEN + jnp.arange(Ts, dtype=jnp.int32))
    eye_kv = jnp.eye(SWA_KV_HEADS, dtype=f32)
    row = lambda v: v.reshape(1, -1)

    hp = x_prompt.reshape(Bp * Tp, D)
    hs = x_sample.reshape(Bs, D)
    outs = [[] for _ in range(10)]
    for l in range(depth):
        w_in_rt = _reorder_rows(jnp.swapaxes(w_in, 1, 2), l, ((0, o_b), (o_gg, w_in.shape[2]), (o_q, o_gg), (o_b, o_q)), d_in_pad)
        ffn1 = (row(norm_ffn1[l]), ffn1_w1[l].astype(bf16), ffn1_w3[l].astype(bf16), ffn1_w2[l].astype(bf16))
        last = l == depth - 1
        gfin = row(norm_final)
        avec = jnp.zeros((1, LANES), f32).at[0, GDN_HEADS : 2 * GDN_HEADS].set(gdn_A_log[l])
        dvec = jnp.zeros((1, LANES), f32).at[0, GDN_HEADS : 2 * GDN_HEADS].set(gdn_dt_bias[l])
        gdn_common = (conv_w[l], avec, dvec, row(gdn_norm[l]))
        tn_in = PROJ_TN

        x1 = _ffn(hp, *ffn1, gfin, False)
        proj = _norm_matmul(x1, row(norm_mix[l]), w_in_rt, tn_in, w_transposed=True)
        later = (ffn2_w1[l], ffn2_w3[l], ffn2_w2[l], w_br_gdn[l], w_br_swa[l], w_out[l], w_mem_q[l],
                 w_mem_k[l], w_mem_v[l], w_mem_o[l])
        og, s_new, later = _gdn_prompt(proj, Bp, Tp, off_ba, *gdn_common, later)
        ffn2 = (row(norm_ffn2[l]), *later[0:3])
        merge_w = (*later[3:6], row(norm_mem_q[l]), later[6])
        wmk, wmv, wmo = later[7:10]
        os_, kc = _swa_prompt(proj, Bp, Tp, off_q, off_kv, cos_p, sin_p, swa_sinks[l])
        mem_x = mem_prompt.reshape(Bp * n_mem, D)
        mk, mv, mkb, mvb = _mem_kv(mem_x, row(norm_mem_kv[l]), wmk, wmv)
        mem = (mkb.reshape(Bp, n_mem, MEM_W), mvb.reshape(Bp, n_mem, MEM_W), wmo, Tp)
        x3 = _merge(og, os_, proj, OFF_GG, x1, *merge_w, mem=mem)
        hp = _ffn(x3, *ffn2, gfin, last)
        proj_b = proj.reshape(Bp, Tp, d_in_pad)
        outs[0].append(s_new)
        outs[1].append(proj_b[:, Tp - (GDN_CONV - 1) :, :CONV_CH])
        outs[2].append(kc.reshape(Bp, WINDOW, SWA_KV_HEADS, SWA_HD))
        outs[3].append(proj_b[:, Tp - WINDOW :, off_kv + SWA_KV_W : off_kv + 2 * SWA_KV_W].reshape(Bp, WINDOW, SWA_KV_HEADS, SWA_HD))
        outs[4].append(mk.reshape(Bp, n_mem, MEM_HEADS, MEM_HD))
        outs[5].append(mv.reshape(Bp, n_mem, MEM_HEADS, MEM_HD))

        x1 = _ffn(hs, *ffn1, gfin, False)
        proj = _norm_matmul(x1, row(norm_mix[l]), w_in_rt, tn_in, w_transposed=True)
        proj3 = proj.reshape(Bs, 1, d_in_pad)
        og, s_new = _gdn_step(proj3, off_ba, state_conv[l], state_gdn[l], *gdn_common)
        q_raw = proj[:, off_q : off_q + SWA_Q_W].reshape(Bs, SWA_KV_HEADS, group, 1, SWA_HD)
        q_exp = (q_raw * eye_kv[None, :, None, :, None]).reshape(Bs, SWA_HEADS, SWA_KV_W)
        ck = jnp.swapaxes(cache_swa_k[l].reshape(Bs, WINDOW, SWA_KV_W), 1, 2)
        cv = jnp.swapaxes(cache_swa_v[l].reshape(Bs, WINDOW, SWA_KV_W), 1, 2)
        r, nk, nv = _swa_step(q_exp, proj3, off_kv, ck, cv, cos_s, sin_s, swa_sinks[l].reshape(SWA_HEADS, 1))
        r5 = r.reshape(Bs, SWA_KV_HEADS, group, SWA_KV_HEADS, SWA_HD)
        kvh = jnp.arange(SWA_KV_HEADS)
        os_ = jnp.transpose(r5[:, kvh, :, kvh, :], (1, 0, 2, 3)).reshape(Bs, SWA_Q_W).astype(bf16)
        x2, qm = _merge(og.reshape(Bs, GDN_W), os_, proj, OFF_GG, x1, *merge_w)
        q8 = jnp.tile(qm.reshape(Bs, MEM_HEADS, MEM_HD), (1, 2, 1))
        om = _mem_attn_step(q8, cache_mem_k[l].reshape(Bs, n_mem * MEM_HEADS, MEM_HD),
                            cache_mem_v[l].reshape(Bs, n_mem * MEM_HEADS, MEM_HD))
        x3 = _proj_residual(om[:, :MEM_HEADS].reshape(Bs, MEM_W), wmo, x2)
        hs = _ffn(x3, *ffn2, gfin, last)
        outs[6].append(s_new)
        outs[7].append(jnp.concatenate([state_conv[l][:, 1:], proj3[:, :, :CONV_CH]], axis=1))
        outs[8].append(jnp.swapaxes(nk, 1, 2).reshape(Bs, WINDOW, SWA_KV_HEADS, SWA_HD))
        outs[9].append(jnp.swapaxes(nv, 1, 2).reshape(Bs, WINDOW, SWA_KV_HEADS, SWA_HD))

    return (hp.reshape(Bp, Tp, D), hs.reshape(Bs, Ts, D), *(jnp.stack(o) for o in outs))
```

```python
import functools
import math

import jax
import jax.numpy as jnp
from jax import lax
from jax.experimental import pallas as pl
from jax.experimental.pallas import tpu as pltpu

f32 = jnp.float32
bf16 = jnp.bfloat16

PAST_LEN = 16384
GDN_HEADS = 8
GDN_D = 128
GDN_CONV = 4
SWA_HEADS = 16
SWA_KV_HEADS = 4
SWA_HD = 64
WINDOW = 128
ROPE_THETA = 10000.0
MEM_HEADS = 4
MEM_HD = 128
EPS = 1e-6
L2_EPS = 1e-6

LANES = 128
SUBLANES = 8
CHUNK = 128
GDN_STEP_CHUNKS = 2
SWA_STEP_BLOCKS = 4
FFN_SLAB = 64
PROJ_TN = 768
VMEM_LIMIT = 58 * 1024 * 1024

GDN_W = GDN_HEADS * GDN_D
CONV_CH = 3 * GDN_W
SWA_Q_W = SWA_HEADS * SWA_HD
SWA_KV_W = SWA_KV_HEADS * SWA_HD
MEM_W = MEM_HEADS * MEM_HD

OFF_Z = CONV_CH
OFF_GG = OFF_Z + GDN_W


def _cparams(semantics):
    return pltpu.CompilerParams(dimension_semantics=semantics, vmem_limit_bytes=VMEM_LIMIT)


def _dot(a, b):
    return jnp.dot(a, b, preferred_element_type=f32)


def _dot_nt(a, b):
    return lax.dot_general(a, b, (((1,), (1,)), ((), ())), preferred_element_type=f32)


def _rmsnorm(x, g):
    return x * lax.rsqrt(jnp.mean(x * x, -1, keepdims=True) + EPS) * g


def _silu(x):
    return x * jax.nn.sigmoid(x)


def _softplus(x):
    return jnp.maximum(x, 0.0) + jnp.log1p(jnp.exp(-jnp.abs(x)))


def _ffn_kernel(x_ref, g_ref, w1_ref, w3_ref, w2_ref, gf_ref, o_ref, h_ref, *, final_norm):
    j = pl.program_id(1)
    slab = min(FFN_SLAB, x_ref.shape[0])
    n_slabs = x_ref.shape[0] // slab

    def over_slabs(body):
        def step(s, carry):
            body(pl.ds(pl.multiple_of(s * slab, slab), slab))
            return carry
        lax.fori_loop(0, n_slabs, step, 0, unroll=min(4, n_slabs))

    @pl.when(j == 0)
    def _():
        def prologue(rows):
            h_ref[rows, :] = _rmsnorm(x_ref[rows, :], g_ref[...]).astype(bf16)
            o_ref[rows, :] = jnp.zeros((slab, o_ref.shape[1]), f32)
        over_slabs(prologue)

    h = h_ref[...]
    a = _dot(h, w1_ref[...])
    b = _dot(h, w3_ref[...])
    o_ref[...] += _dot((_silu(a) * b).astype(bf16), w2_ref[...])

    @pl.when(j == pl.num_programs(1) - 1)
    def _():
        def epilogue(rows):
            y = x_ref[rows, :] + 0.5 * o_ref[rows, :]
            if final_norm:
                y = _rmsnorm(y, gf_ref[...])
            o_ref[rows, :] = y
        over_slabs(epilogue)


def _ffn(x, g, w1, w3, w2, gf, final_norm):
    M, D = x.shape
    F = w1.shape[1]
    tm = min(1024, M)
    tf = 512
    assert M % tm == 0 and F % tf == 0
    return pl.pallas_call(
        functools.partial(_ffn_kernel, final_norm=final_norm),
        grid=(M // tm, F // tf),
        in_specs=[
            pl.BlockSpec((tm, D), lambda i, j: (i, 0)),
            pl.BlockSpec((1, D), lambda i, j: (0, 0)),
            pl.BlockSpec((D, tf), lambda i, j: (0, j)),
            pl.BlockSpec((D, tf), lambda i, j: (0, j)),
            pl.BlockSpec((tf, D), lambda i, j: (j, 0)),
            pl.BlockSpec((1, D), lambda i, j: (0, 0)),
        ],
        out_specs=pl.BlockSpec((tm, D), lambda i, j: (i, 0)),
        out_shape=jax.ShapeDtypeStruct((M, D), f32),
        scratch_shapes=[pltpu.VMEM((tm, D), bf16)],
        compiler_params=_cparams(("parallel", "arbitrary")),
        name="ffn",
    )(x, g, w1, w3, w2, gf)


def _norm_matmul_kernel(x_ref, g_ref, w_ref, o_ref, h_ref, *, w_transposed):
    @pl.when(pl.program_id(1) == 0)
    def _():
        h_ref[...] = _rmsnorm(x_ref[...], g_ref[...]).astype(bf16)

    o_ref[...] = (_dot_nt if w_transposed else _dot)(h_ref[...], w_ref[...])


def _norm_matmul(x, g, w, tn, w_transposed=False):
    M, D = x.shape
    N = w.shape[0] if w_transposed else w.shape[1]
    tm = min(1024, M)
    assert M % tm == 0 and N % tn == 0
    w_spec = pl.BlockSpec((tn, D), lambda i, j: (j, 0)) if w_transposed else pl.BlockSpec((D, tn), lambda i, j: (0, j))
    return pl.pallas_call(
        functools.partial(_norm_matmul_kernel, w_transposed=w_transposed),
        grid=(M // tm, N // tn),
        in_specs=[
            pl.BlockSpec((tm, D), lambda i, j: (i, 0)),
            pl.BlockSpec((1, D), lambda i, j: (0, 0)),
            w_spec,
        ],
        out_specs=pl.BlockSpec((tm, tn), lambda i, j: (i, j)),
        out_shape=jax.ShapeDtypeStruct((M, N), f32),
        scratch_shapes=[pltpu.VMEM((tm, D), bf16)],
        compiler_params=_cparams(("parallel", "arbitrary")),
        name="norm_matmul",
    )(x, g, w)


def _mem_kv_kernel(x_ref, g_ref, wk_ref, wv_ref, k_ref, v_ref, kb_ref, vb_ref):
    h = _rmsnorm(x_ref[...], g_ref[...]).astype(bf16)
    k = _dot(h, wk_ref[...])
    v = _dot(h, wv_ref[...])
    kb_ref[...] = k.astype(bf16)
    vb_ref[...] = v.astype(bf16)
    tokens = x_ref.shape[0]
    for hd in range(MEM_HEADS):
        k_ref[pl.ds(hd, tokens, stride=MEM_HEADS), :] = k[:, hd * MEM_HD : (hd + 1) * MEM_HD]
        v_ref[pl.ds(hd, tokens, stride=MEM_HEADS), :] = v[:, hd * MEM_HD : (hd + 1) * MEM_HD]


def _mem_kv(x, g, wk, wv):
    M, D = x.shape
    N = wk.shape[1]
    tm = min(1024, M)
    assert M % tm == 0 and N == MEM_W
    row_spec = pl.BlockSpec((tm, N), lambda i: (i, 0))
    head_row_spec = pl.BlockSpec((tm * MEM_HEADS, MEM_HD), lambda i: (i, 0))
    const = lambda shape: pl.BlockSpec(shape, lambda i: (0, 0))
    return pl.pallas_call(
        _mem_kv_kernel,
        grid=(M // tm,),
        in_specs=[pl.BlockSpec((tm, D), lambda i: (i, 0)), const((1, D)), const((D, N)), const((D, N))],
        out_specs=[head_row_spec] * 2 + [row_spec] * 2,
        out_shape=[jax.ShapeDtypeStruct((M * MEM_HEADS, MEM_HD), f32)] * 2 + [jax.ShapeDtypeStruct((M, N), bf16)] * 2,
        compiler_params=_cparams(("parallel",)),
        name="mem_kv",
    )(x, g, wk, wv)


def _bdot(a, b):
    return _dot(a.astype(bf16), b.astype(bf16))


def _unit_lower_inverse(nmats, row, col):
    eye = jnp.where(row == col, 1.0, 0.0).astype(f32)

    def same_block(size):
        return (row // size) == (col // size)

    blk = same_block(SUBLANES)
    a1 = [jnp.where(blk, n, 0.0) for n in nmats]
    a2 = [_bdot(a, a) for a in a1]
    a4 = [_bdot(a, a) for a in a2]
    ts = [_bdot(eye - x1, eye + x2) for x1, x2 in zip(a1, a2)]
    ts = [_bdot(t, eye + x4) for t, x4 in zip(ts, a4)]
    size = SUBLANES
    while size < CHUNK:
        nxt = same_block(2 * size)
        sel = jnp.logical_and(nxt, jnp.logical_not(blk))
        tbs = [t.astype(bf16) for t in ts]
        xs = [_dot(jnp.where(sel, n, 0.0).astype(bf16), tb) for n, tb in zip(nmats, tbs)]
        ts = [t - _dot(tb, x.astype(bf16)) for t, tb, x in zip(ts, tbs, xs)]
        blk = nxt
        size *= 2
    return ts


def _conv_silu_slab(x_ref, top_ref, w_ref, sl):
    rows = x_ref.shape[0]
    y = None
    for tap in range(GDN_CONV):
        back = GDN_CONV - 1 - tap
        window = jnp.concatenate(
            [top_ref[SUBLANES - back : 2 * SUBLANES - back, sl], x_ref[SUBLANES - back : rows - back, sl]], axis=0)
        term = window * w_ref[tap : tap + 1, sl]
        y = term if y is None else y + term
    return _silu(y)


def _delta_rule_prepare(pairs, gcum, gcum_t, beta_t, row, col, q_ref, k_ref, v_ref):
    n = range(len(pairs))
    incl = row >= col
    strict = row > col
    rows_of = lambda ci: slice(ci * CHUNK, (ci + 1) * CHUNK)
    gc = [gcum[rows_of(ci), GDN_HEADS + h : GDN_HEADS + h + 1] for ci, h in pairs]
    gr = [gcum_t[ci][GDN_HEADS + h : GDN_HEADS + h + 1, :] for ci, h in pairs]
    beta = [beta_t[rows_of(ci), h : h + 1] for ci, h in pairs]
    k = [k_ref[h, rows_of(ci), :] for ci, h in pairs]
    kb = [x.astype(bf16) for x in k]
    kk = [_dot_nt(kb[i], kb[i]) for i in n]
    qk = [_dot_nt(q_ref[h, rows_of(ci), :].astype(bf16), kb[i]) for i, (ci, h) in enumerate(pairs)]
    e = [jnp.exp(jnp.where(incl, gc[i] - gr[i], 0.0)) for i in n]
    nmat = [beta[i] * kk[i] * jnp.where(strict, e[i], 0.0) for i in n]
    qkd = [(qk[i] * jnp.where(incl, e[i], 0.0)).astype(bf16) for i in n]
    tinv = _unit_lower_inverse(nmat, row, col)
    gamma = [jnp.exp(gc[i]) for i in n]
    rhs = [jnp.concatenate([beta[i] * v_ref[h, rows_of(ci), :], (beta[i] * gamma[i]) * k[i]], axis=-1).astype(bf16)
           for i, (ci, h) in enumerate(pairs)]
    sol = [_dot(tinv[i].astype(bf16), rhs[i]) for i in n]
    return gc, gamma, sol, qkd


def _delta_rule_apply(ci, prepared, q_ref, k_ref, z_ref, gn_ref, s_ref, og_ref):
    gc, gamma, sol, qkd = prepared
    heads = range(GDN_HEADS)
    rows = slice(ci * CHUNK, (ci + 1) * CHUNK)
    sb = [s_ref[h].astype(bf16) for h in heads]
    ub = [(sol[h][:, :GDN_D] - _dot(sol[h][:, GDN_D:].astype(bf16), sb[h])).astype(bf16) for h in heads]
    o = [_dot((q_ref[h, rows, :] * gamma[h]).astype(bf16), sb[h]) + _dot(qkd[h], ub[h]) for h in heads]
    for h in heads:
        g_last = gc[h][CHUNK - 1 : CHUNK, :]
        k_end = k_ref[h, rows, :] * jnp.exp(g_last - gc[h])
        s_ref[h] = jnp.exp(g_last) * s_ref[h] + _dot(k_end.T.astype(bf16), ub[h])
        zh = z_ref[rows, h * GDN_D : (h + 1) * GDN_D]
        og_ref[rows, h * GDN_D : (h + 1) * GDN_D] = (_rmsnorm(o[h], gn_ref[...]) * _silu(zh)).astype(bf16)


def _gdn_prompt_kernel(*refs, n_cast):
    qkv_ref, z_ref, ba_ref, cw_ref, av_ref, dv_ref, gn_ref = refs[:7]
    cast_in = refs[7 : 7 + n_cast]
    og_ref, so_ref = refs[7 + n_cast : 9 + n_cast]
    cast_out = refs[9 + n_cast : 9 + 2 * n_cast]
    s_ref, carry_ref, q_ref, k_ref, v_ref = refs[9 + 2 * n_cast :]
    for src, dst in zip(cast_in, cast_out):
        dst[...] = src[...].astype(bf16)

    c = pl.program_id(1)
    rows = qkv_ref.shape[0]
    n_chunks = rows // CHUNK

    @pl.when(c == 0)
    def _():
        s_ref[...] = jnp.zeros_like(s_ref)
        carry_ref[0:SUBLANES, :] = jnp.zeros((SUBLANES, CONV_CH), f32)

    carry_ref[SUBLANES : 2 * SUBLANES, :] = qkv_ref[0:SUBLANES, :]
    for j in range(3 * GDN_HEADS):
        sl = slice(j * LANES, (j + 1) * LANES)
        y = _conv_silu_slab(qkv_ref, carry_ref, cw_ref, sl)
        h = j % GDN_HEADS
        if j < 2 * GDN_HEADS:
            y = y * lax.rsqrt(jnp.sum(y * y, -1, keepdims=True) + L2_EPS)
            if j < GDN_HEADS:
                q_ref[h] = y * (GDN_D ** -0.5)
            else:
                k_ref[h] = y
        else:
            v_ref[h] = y
    carry_ref[0:SUBLANES, :] = qkv_ref[rows - SUBLANES : rows, :]

    ba = ba_ref[...]
    beta_t = jax.nn.sigmoid(ba)
    g_t = -jnp.exp(av_ref[...]) * _softplus(ba + dv_ref[...])
    row_in_chunk = lax.broadcasted_iota(jnp.int32, (rows, LANES), 0) % CHUNK
    gcum = g_t
    shift = 1
    while shift < CHUNK:
        gcum = gcum + jnp.where(row_in_chunk >= shift, pltpu.roll(gcum, shift, 0), 0.0)
        shift *= 2
    gcum_t = [gcum[ci * CHUNK : (ci + 1) * CHUNK, :].T for ci in range(n_chunks)]

    row = lax.broadcasted_iota(jnp.int32, (CHUNK, CHUNK), 0)
    col = lax.broadcasted_iota(jnp.int32, (CHUNK, CHUNK), 1)
    pairs = [(ci, h) for ci in range(n_chunks) for h in range(GDN_HEADS)]
    gc, gamma, sol, qkd = _delta_rule_prepare(pairs, gcum, gcum_t, beta_t, row, col, q_ref, k_ref, v_ref)
    for ci in range(n_chunks):
        mine = slice(ci * GDN_HEADS, (ci + 1) * GDN_HEADS)
        _delta_rule_apply(ci, (gc[mine], gamma[mine], sol[mine], qkd[mine]), q_ref, k_ref, z_ref, gn_ref, s_ref, og_ref)

    @pl.when(c == pl.num_programs(1) - 1)
    def _():
        so_ref[0] = s_ref[...]


def _cast_block(shape, steps):
    R, C = shape
    for col_blocks in (1, 2, 4, 8, 16):
        row_blocks = steps // col_blocks
        if (steps % col_blocks == 0 and R % row_blocks == 0 and C % col_blocks == 0
                and (R // row_blocks) % (2 * SUBLANES) == 0 and (C // col_blocks) % LANES == 0):
            return (R // row_blocks, C // col_blocks), col_blocks
    return None


def _gdn_prompt(proj, B, T, off_ba, cw, avec, dvec, gn, weights):
    rows = GDN_STEP_CHUNKS * CHUNK
    assert T % rows == 0
    ns = T // rows
    step = lambda b, c: b * ns + c
    plans = [_cast_block(w.shape, B * ns) for w in weights]
    riders = [w for w, p in zip(weights, plans) if p is not None]
    cast_specs = [pl.BlockSpec(blk, functools.partial(lambda b, c, cb: (step(b, c) // cb, step(b, c) % cb), cb=cb))
                  for blk, cb in (p for p in plans if p is not None)]
    head_scratch = pltpu.VMEM((GDN_HEADS, rows, GDN_D), f32)
    out = pl.pallas_call(
        functools.partial(_gdn_prompt_kernel, n_cast=len(riders)),
        grid=(B, ns),
        in_specs=[
            pl.BlockSpec((rows, CONV_CH), lambda b, c: (step(b, c), 0)),
            pl.BlockSpec((rows, GDN_W), lambda b, c: (step(b, c), OFF_Z // GDN_W)),
            pl.BlockSpec((rows, LANES), lambda b, c: (step(b, c), off_ba // LANES)),
            pl.BlockSpec((GDN_CONV, CONV_CH), lambda b, c: (0, 0)),
            pl.BlockSpec((1, LANES), lambda b, c: (0, 0)),
            pl.BlockSpec((1, LANES), lambda b, c: (0, 0)),
            pl.BlockSpec((1, GDN_D), lambda b, c: (0, 0)),
        ] + cast_specs,
        out_specs=[
            pl.BlockSpec((rows, GDN_W), lambda b, c: (step(b, c), 0)),
            pl.BlockSpec((1, GDN_HEADS, GDN_D, GDN_D), lambda b, c: (b, 0, 0, 0)),
        ] + cast_specs,
        out_shape=[
            jax.ShapeDtypeStruct((B * T, GDN_W), bf16),
            jax.ShapeDtypeStruct((B, GDN_HEADS, GDN_D, GDN_D), f32),
        ] + [jax.ShapeDtypeStruct(w.shape, bf16) for w in riders],
        scratch_shapes=[
            pltpu.VMEM((GDN_HEADS, GDN_D, GDN_D), f32),
            pltpu.VMEM((2 * SUBLANES, CONV_CH), f32),
            head_scratch, head_scratch, head_scratch,
        ],
        compiler_params=_cparams(("arbitrary", "arbitrary")),
        name="gdn_prompt",
    )(proj, proj, proj, cw, avec, dvec, gn, *riders)
    converted = iter(out[2:])
    return out[0], out[1], [next(converted) if p is not None else w.astype(bf16) for w, p in zip(weights, plans)]


def _gdn_step_kernel(qkv_ref, z_ref, ba_ref, sc_ref, s0_ref, cw_ref, av_ref, dv_ref, gn_ref, og_ref, so_ref):
    x_new = qkv_ref[0]
    taps = cw_ref[...]
    y = jnp.sum(sc_ref[0] * taps[0 : GDN_CONV - 1, :], axis=0, keepdims=True) + x_new * taps[GDN_CONV - 1 : GDN_CONV, :]
    y = _silu(y)
    ba = ba_ref[0]
    beta_t = jax.nn.sigmoid(ba)
    gamma_t = jnp.exp(-jnp.exp(av_ref[...]) * _softplus(ba + dv_ref[...]))
    z = z_ref[0]
    heads = range(GDN_HEADS)
    head = lambda base, h: y[:, base + h * GDN_D : base + (h + 1) * GDN_D]
    q = [head(0, h) for h in heads]
    k = [head(GDN_W, h) for h in heads]
    v = [head(2 * GDN_W, h) for h in heads]
    q = [q[h] * lax.rsqrt(jnp.sum(q[h] * q[h], -1, keepdims=True) + L2_EPS) * (GDN_D ** -0.5) for h in heads]
    k = [k[h] * lax.rsqrt(jnp.sum(k[h] * k[h], -1, keepdims=True) + L2_EPS) for h in heads]
    beta = [beta_t[:, h : h + 1] for h in heads]
    gamma = [gamma_t[:, GDN_HEADS + h : GDN_HEADS + h + 1] for h in heads]
    k_col = [jnp.broadcast_to(k[h], (GDN_D, GDN_D)).T for h in heads]
    sub = lax.broadcasted_iota(jnp.int32, (SUBLANES, GDN_D), 0)
    kq = [jnp.where(sub == 0, k[h], jnp.where(sub == 1, q[h], 0.0)).astype(bf16) for h in heads]
    kq_s = [_dot(kq[h], s0_ref[0, h].astype(bf16)) for h in heads]
    k_s = [kq_s[h][0:1] for h in heads]
    q_s = [kq_s[h][1:2] for h in heads]
    u = [beta[h] * v[h] - (beta[h] * gamma[h]) * k_s[h] for h in heads]
    o = [gamma[h] * q_s[h] + jnp.sum(q[h] * k[h], -1, keepdims=True) * u[h] for h in heads]
    for h in heads:
        so_ref[0, h] = gamma[h] * s0_ref[0, h] + k_col[h] * u[h]
        zh = z[:, h * GDN_D : (h + 1) * GDN_D]
        og_ref[0, :, h * GDN_D : (h + 1) * GDN_D] = (_rmsnorm(o[h], gn_ref[...]) * _silu(zh)).astype(bf16)


def _gdn_step(proj3, off_ba, state_conv, state_gdn, cw, avec, dvec, gn):
    B = proj3.shape[0]
    return pl.pallas_call(
        _gdn_step_kernel,
        grid=(B,),
        in_specs=[
            pl.BlockSpec((1, 1, CONV_CH), lambda b: (b, 0, 0)),
            pl.BlockSpec((1, 1, GDN_W), lambda b: (b, 0, OFF_Z // GDN_W)),
            pl.BlockSpec((1, 1, LANES), lambda b: (b, 0, off_ba // LANES)),
            pl.BlockSpec((1, GDN_CONV - 1, CONV_CH), lambda b: (b, 0, 0)),
            pl.BlockSpec((1, GDN_HEADS, GDN_D, GDN_D), lambda b: (b, 0, 0, 0)),
            pl.BlockSpec((GDN_CONV, CONV_CH), lambda b: (0, 0)),
            pl.BlockSpec((1, LANES), lambda b: (0, 0)),
            pl.BlockSpec((1, LANES), lambda b: (0, 0)),
            pl.BlockSpec((1, GDN_D), lambda b: (0, 0)),
        ],
        out_specs=[
            pl.BlockSpec((1, 1, GDN_W), lambda b: (b, 0, 0)),
            pl.BlockSpec((1, GDN_HEADS, GDN_D, GDN_D), lambda b: (b, 0, 0, 0)),
        ],
        out_shape=[
            jax.ShapeDtypeStruct((B, 1, GDN_W), bf16),
            jax.ShapeDtypeStruct((B, GDN_HEADS, GDN_D, GDN_D), f32),
        ],
        compiler_params=_cparams(("parallel",)),
        name="gdn_step",
    )(proj3, proj3, proj3, state_conv, state_gdn, cw, avec, dvec, gn)


def _rope(x, cos, sin_signed):
    width = x.shape[-1]
    lane = lax.broadcasted_iota(jnp.int32, x.shape, x.ndim - 1)
    first_half = (lane % SWA_HD) < (SWA_HD // 2)
    rot = jnp.where(first_half, pltpu.roll(x, width - SWA_HD // 2, x.ndim - 1), pltpu.roll(x, SWA_HD // 2, x.ndim - 1))
    return x * cos + rot * sin_signed


def _head_halves(x2, head_parity, lane):
    swapped = pltpu.roll(x2, SWA_HD, 1)
    lo_src, hi_src = (x2, swapped) if head_parity == 0 else (swapped, x2)
    return jnp.where(lane < SWA_HD, lo_src, 0.0), jnp.where(lane >= SWA_HD, hi_src, 0.0)


def _swa_prompt_kernel(sinks_ref, q_ref, kv_ref, cos_ref, sin_ref, os_ref, kc_ref, kprev_ref, vprev_ref):
    n = pl.program_id(1)
    n_blk = q_ref.shape[0] // WINDOW

    @pl.when(n == 0)
    def _():
        kprev_ref[...] = jnp.zeros_like(kprev_ref)
        vprev_ref[...] = jnp.zeros_like(vprev_ref)

    row = lax.broadcasted_iota(jnp.int32, (WINDOW, WINDOW), 0)
    col = lax.broadcasted_iota(jnp.int32, (WINDOW, WINDOW), 1)
    own = col <= row
    first_bias = jnp.where(n > 0, 0.0, -jnp.inf)
    lane = lax.broadcasted_iota(jnp.int32, (WINDOW, LANES), 1)
    scale = SWA_HD ** -0.5
    group = SWA_HEADS // SWA_KV_HEADS

    k_own, v_own, k_pre, v_pre, q2 = [], [], [], [], []
    for blk in range(n_blk):
        rows = slice(blk * WINDOW, (blk + 1) * WINDOW)
        cos = cos_ref[rows, :]
        sin = sin_ref[rows, :]
        k_cur = _rope(kv_ref[rows, :SWA_KV_W], cos, sin)
        v_cur = kv_ref[rows, SWA_KV_W:]
        if blk == n_blk - 1:
            kc_ref[0] = k_cur
        k_own.append([]), v_own.append([]), k_pre.append([]), v_pre.append([]), q2.append([])
        for h in range(SWA_KV_HEADS):
            pair = slice((h // 2) * LANES, (h // 2 + 1) * LANES)
            k_own[blk].append([a.astype(bf16) for a in _head_halves(k_cur[:, pair], h % 2, lane)])
            v_own[blk].append([a.astype(bf16) for a in _head_halves(v_cur[:, pair], h % 2, lane)])
            if blk == 0:
                k_pre[blk].append([kprev_ref[h, par] for par in range(2)])
                v_pre[blk].append([vprev_ref[h, par] for par in range(2)])
            else:
                k_pre[blk].append(k_own[blk - 1][h])
                v_pre[blk].append(v_own[blk - 1][h])
            if blk == n_blk - 1:
                for par in range(2):
                    kprev_ref[h, par] = k_own[blk][h][par]
                    vprev_ref[h, par] = v_own[blk][h][par]
            q_h = _rope(q_ref[rows, h * group * SWA_HD : (h + 1) * group * SWA_HD], cos, sin) * scale
            q2[blk].append([q_h[:, j * LANES : (j + 1) * LANES].astype(bf16) for j in range(group // 2)])

    heads = [(blk, h, j, par) for blk in range(n_blk) for h in range(SWA_KV_HEADS)
             for j in range(group // 2) for par in range(2)]
    idx = range(len(heads))
    sink = [sinks_ref[h * group + 2 * j + par] for blk, h, j, par in heads]
    s = [jnp.where(own, _dot_nt(q2[blk][h][j], k_own[blk][h][par]),
                   _dot_nt(q2[blk][h][j], k_pre[blk][h][par]) + (first_bias if blk == 0 else 0.0))
         for blk, h, j, par in heads]
    m = [jnp.maximum(jnp.max(s[i], -1, keepdims=True), sink[i]) for i in idx]
    p = [jnp.exp(s[i] - m[i]) for i in idx]
    inv = [1.0 / (jnp.sum(p[i], -1, keepdims=True) + jnp.exp(sink[i] - m[i])) for i in idx]
    o = [(_dot(jnp.where(own, p[i], 0.0).astype(bf16), v_own[blk][h][par])
          + _dot(jnp.where(own, 0.0, p[i]).astype(bf16), v_pre[blk][h][par])) * inv[i]
         for i, (blk, h, j, par) in enumerate(heads)]
    for i in range(0, len(heads), 2):
        blk, head = divmod(i, SWA_HEADS)
        os_ref[blk * WINDOW : (blk + 1) * WINDOW, head * SWA_HD : head * SWA_HD + LANES] = (o[i] + o[i + 1]).astype(bf16)


def _swa_prompt(proj, B, T, off_q, off_kv, cos, sin, sinks):
    assert T % WINDOW == 0 and SWA_KV_W == 2 * LANES
    rows = math.gcd(SWA_STEP_BLOCKS, T // WINDOW) * WINDOW
    nb = T // rows
    return pl.pallas_call(
        _swa_prompt_kernel,
        grid=(B, nb),
        in_specs=[
            pl.BlockSpec(memory_space=pltpu.SMEM),
            pl.BlockSpec((rows, SWA_Q_W), lambda b, n: (b * nb + n, off_q // SWA_Q_W)),
            pl.BlockSpec((rows, 2 * SWA_KV_W), lambda b, n: (b * nb + n, off_kv // (2 * SWA_KV_W))),
            pl.BlockSpec((rows, SWA_KV_W), lambda b, n: (n, 0)),
            pl.BlockSpec((rows, SWA_KV_W), lambda b, n: (n, 0)),
        ],
        out_specs=[
            pl.BlockSpec((rows, SWA_Q_W), lambda b, n: (b * nb + n, 0)),
            pl.BlockSpec((1, WINDOW, SWA_KV_W), lambda b, n: (b, 0, 0)),
        ],
        out_shape=[
            jax.ShapeDtypeStruct((B * T, SWA_Q_W), bf16),
            jax.ShapeDtypeStruct((B, WINDOW, SWA_KV_W), f32),
        ],
        scratch_shapes=[pltpu.VMEM((SWA_KV_HEADS, 2, WINDOW, LANES), bf16)] * 2,
        compiler_params=_cparams(("parallel", "arbitrary")),
        name="swa_prompt",
    )(sinks, proj, proj, cos, sin)


SWA_STEP_BATCH = 8


def _swa_step_kernel(qe_ref, kv_ref, ck_ref, cv_ref, cos_ref, sin_ref, sinks_ref, r_ref, nk_ref, nv_ref):
    cos = cos_ref[...]
    sin = sin_ref[...]
    sink = sinks_ref[...]
    row = lax.broadcasted_iota(jnp.int32, (WINDOW, SWA_KV_W), 0)
    scale = SWA_HD ** -0.5
    seqs = range(qe_ref.shape[0])
    kv = [kv_ref[i] for i in seqs]
    k_new = [_rope(kv[i][:, :SWA_KV_W], cos, sin) for i in seqs]
    keys = [jnp.where(row == WINDOW - 1, k_new[i], pltpu.roll(ck_ref[i].T, WINDOW - 1, 0)) for i in seqs]
    vals = [jnp.where(row == WINDOW - 1, kv[i][:, SWA_KV_W:], pltpu.roll(cv_ref[i].T, WINDOW - 1, 0)) for i in seqs]
    for i in seqs:
        nk_ref[i] = keys[i].T
        nv_ref[i] = vals[i].T
    q = [_rope(qe_ref[i], cos, sin) for i in seqs]
    s = [_dot_nt(q[i].astype(bf16), keys[i].astype(bf16)) * scale for i in seqs]
    m = [jnp.maximum(jnp.max(s[i], -1, keepdims=True), sink) for i in seqs]
    p = [jnp.exp(s[i] - m[i]) for i in seqs]
    denom = [jnp.sum(p[i], -1, keepdims=True) + jnp.exp(sink - m[i]) for i in seqs]
    for i in seqs:
        r_ref[i] = _dot((p[i] / denom[i]).astype(bf16), vals[i].astype(bf16))


def _swa_step(q_exp, proj3, off_kv, cache_k, cache_v, cos, sin, sinks_col):
    B = q_exp.shape[0]
    bb = math.gcd(B, SWA_STEP_BATCH)
    assert cache_k.shape[2] == WINDOW
    return pl.pallas_call(
        _swa_step_kernel,
        grid=(B // bb,),
        in_specs=[
            pl.BlockSpec((bb, SWA_HEADS, SWA_KV_W), lambda i: (i, 0, 0)),
            pl.BlockSpec((bb, 1, 2 * SWA_KV_W), lambda i: (i, 0, off_kv // (2 * SWA_KV_W))),
            pl.BlockSpec((bb, SWA_KV_W, WINDOW), lambda i: (i, 0, 0)),
            pl.BlockSpec((bb, SWA_KV_W, WINDOW), lambda i: (i, 0, 0)),
            pl.BlockSpec((1, SWA_KV_W), lambda i: (0, 0)),
            pl.BlockSpec((1, SWA_KV_W), lambda i: (0, 0)),
            pl.BlockSpec((SWA_HEADS, 1), lambda i: (0, 0)),
        ],
        out_specs=[
            pl.BlockSpec((bb, SWA_HEADS, SWA_KV_W), lambda i: (i, 0, 0)),
            pl.BlockSpec((bb, SWA_KV_W, WINDOW), lambda i: (i, 0, 0)),
            pl.BlockSpec((bb, SWA_KV_W, WINDOW), lambda i: (i, 0, 0)),
        ],
        out_shape=[
            jax.ShapeDtypeStruct((B, SWA_HEADS, SWA_KV_W), f32),
            jax.ShapeDtypeStruct((B, SWA_KV_W, WINDOW), f32),
            jax.ShapeDtypeStruct((B, SWA_KV_W, WINDOW), f32),
        ],
        compiler_params=_cparams(("parallel",)),
        name="swa_step",
    )(q_exp, proj3, cache_k, cache_v, cos, sin, sinks_col)


def _merge_core(og_ref, os_ref, gg_ref, gs_ref, x_ref, wg_ref, ws_ref, wo_ref, gq_ref, wq_ref):
    p_gdn = _dot(og_ref[...], wg_ref[...])
    p_swa = _dot(os_ref[...], ws_ref[...])
    merged = jax.nn.sigmoid(gg_ref[...]) * p_gdn + jax.nn.sigmoid(gs_ref[...]) * p_swa
    x_new = x_ref[...] + _dot(merged.astype(bf16), wo_ref[...])
    return x_new, _dot(_rmsnorm(x_new, gq_ref[...]).astype(bf16), wq_ref[...])


def _merge_kernel(*refs):
    xo_ref, qm_ref = refs[-2:]
    xo_ref[...], qm_ref[...] = _merge_core(*refs[:-2])


def _merge_mem_kernel(*refs):
    mk_ref, mv_ref, wmo_ref, xo_ref = refs[-4:]
    x_new, q = _merge_core(*refs[:-4])
    q = q.astype(bf16)
    scale = MEM_HD ** -0.5
    heads = range(MEM_HEADS)
    cols = lambda h: slice(h * MEM_HD, (h + 1) * MEM_HD)
    s = [_dot_nt(q[:, cols(h)], mk_ref[0, :, cols(h)]) * scale for h in heads]
    p = [jnp.exp(s[h] - jnp.max(s[h], -1, keepdims=True)) for h in heads]
    p = [p[h] / jnp.sum(p[h], -1, keepdims=True) for h in heads]
    o = [_dot(p[h].astype(bf16), mv_ref[0, :, cols(h)]) for h in heads]
    xo_ref[...] = x_new + _dot(jnp.concatenate(o, axis=-1).astype(bf16), wmo_ref[...])


def _merge(og, os_, proj, off_gg, x, wg, ws, wo, gq, wq, mem=None):
    M, D = x.shape
    tm = min(256, M)
    assert M % tm == 0 and off_gg % D == 0
    const = lambda shape: pl.BlockSpec(shape, lambda i: (0, 0), pipeline_mode=pl.Buffered(1))
    in_specs = [
        pl.BlockSpec((tm, GDN_W), lambda i: (i, 0)),
        pl.BlockSpec((tm, SWA_Q_W), lambda i: (i, 0)),
        pl.BlockSpec((tm, D), lambda i: (i, off_gg // D)),
        pl.BlockSpec((tm, D), lambda i: (i, off_gg // D + 1)),
        pl.BlockSpec((tm, D), lambda i: (i, 0)),
        const((GDN_W, D)),
        const((SWA_Q_W, D)),
        const((D, D)),
        const((1, D)),
        const((D, MEM_W)),
    ]
    args = [og, os_, proj, proj, x, wg, ws, wo, gq, wq]
    x_spec = pl.BlockSpec((tm, D), lambda i: (i, 0))
    x_shape = jax.ShapeDtypeStruct((M, D), f32)
    if mem is None:
        return pl.pallas_call(
            _merge_kernel,
            grid=(M // tm,),
            in_specs=in_specs,
            out_specs=[x_spec, pl.BlockSpec((tm, MEM_W), lambda i: (i, 0))],
            out_shape=[x_shape, jax.ShapeDtypeStruct((M, MEM_W), f32)],
            compiler_params=_cparams(("parallel",)),
            name="merge",
        )(*args)
    mem_k, mem_v, wmo, seq_rows = mem
    assert seq_rows % tm == 0
    mt = mem_k.shape[1]
    mem_spec = pl.BlockSpec((1, mt, MEM_W), lambda i: (i // (seq_rows // tm), 0, 0))
    return pl.pallas_call(
        _merge_mem_kernel,
        grid=(M // tm,),
        in_specs=in_specs + [mem_spec, mem_spec, const((MEM_W, D))],
        out_specs=x_spec,
        out_shape=x_shape,
        compiler_params=_cparams(("parallel",)),
        name="merge_mem",
    )(*args, mem_k, mem_v, wmo)


MEM_STEP_BATCH = 4


def _mem_attn_step_kernel(q_ref, k_ref, v_ref, o_ref):
    mt2 = k_ref.shape[1] // SUBLANES
    for i in range(q_ref.shape[0]):
        q8 = q_ref[i] * (MEM_HD ** -0.5)
        s = jnp.sum(k_ref[i].reshape(mt2, SUBLANES, MEM_HD) * q8, axis=-1, keepdims=True)
        m = jnp.max(s, axis=0)
        m = jnp.maximum(m, pltpu.roll(m, MEM_HEADS, 0))
        p = jnp.exp(s - m)
        l = jnp.sum(p, axis=0)
        l = l + pltpu.roll(l, MEM_HEADS, 0)
        o = jnp.sum(p * v_ref[i].reshape(mt2, SUBLANES, MEM_HD), axis=0)
        o_ref[i] = (o + pltpu.roll(o, MEM_HEADS, 0)) / l


def _mem_attn_step(q8, mem_k, mem_v):
    B, rows, _ = mem_k.shape
    assert 2 * MEM_HEADS == SUBLANES and rows % SUBLANES == 0
    bb = math.gcd(B, MEM_STEP_BATCH)
    return pl.pallas_call(
        _mem_attn_step_kernel,
        grid=(B // bb,),
        in_specs=[
            pl.BlockSpec((bb, SUBLANES, MEM_HD), lambda i: (i, 0, 0)),
            pl.BlockSpec((bb, rows, MEM_HD), lambda i: (i, 0, 0)),
            pl.BlockSpec((bb, rows, MEM_HD), lambda i: (i, 0, 0)),
        ],
        out_specs=pl.BlockSpec((bb, SUBLANES, MEM_HD), lambda i: (i, 0, 0)),
        out_shape=jax.ShapeDtypeStruct((B, SUBLANES, MEM_HD), f32),
        compiler_params=_cparams(("parallel",)),
        name="mem_attn_step",
    )(q8, mem_k, mem_v)


def _proj_residual_kernel(a_ref, w_ref, x_ref, o_ref):
    o_ref[...] = x_ref[...] + _dot(a_ref[...].astype(bf16), w_ref[...])


def _proj_residual(a, w, x):
    M, D = x.shape
    return pl.pallas_call(
        _proj_residual_kernel,
        out_shape=jax.ShapeDtypeStruct((M, D), f32),
        compiler_params=pltpu.CompilerParams(vmem_limit_bytes=VMEM_LIMIT),
        name="proj_residual",
    )(a, w, x)


def _reorder_rows_kernel(w_ref, o_ref, *, pieces):
    at = 0
    for lo, hi in pieces:
        o_ref[at : at + hi - lo, :] = w_ref[lo:hi, :].astype(bf16)
        at += hi - lo
    o_ref[at:, :] = jnp.zeros((o_ref.shape[0] - at, o_ref.shape[1]), bf16)


def _reorder_rows(wt, layer, pieces, height):
    _, d_in, D = wt.shape
    assert all(lo % (2 * SUBLANES) == 0 and hi % (2 * SUBLANES) == 0 for lo, hi in pieces) and D % LANES == 0
    return pl.pallas_call(
        functools.partial(_reorder_rows_kernel, pieces=pieces),
        grid=(D // LANES,),
        in_specs=[pl.BlockSpec((None, d_in, LANES), lambda i: (layer, 0, i))],
        out_specs=pl.BlockSpec((height, LANES), lambda i: (0, i)),
        out_shape=jax.ShapeDtypeStruct((height, D), bf16),
        compiler_params=_cparams(("parallel",)),
        name="reorder_rows",
    )(wt)


def _rope_tables(pos):
    half = SWA_HD // 2
    inv_freq = ROPE_THETA ** (-jnp.arange(half, dtype=f32) / half)
    ang = pos.astype(f32)[:, None] * inv_freq[None, :]
    cos = jnp.cos(ang)
    sin = jnp.sin(ang)
    reps = SWA_KV_W // SWA_HD
    return jnp.tile(jnp.concatenate([cos, cos], -1), (1, reps)), jnp.tile(jnp.concatenate([-sin, sin], -1), (1, reps))


def kernel(x_prompt, x_sample, state_gdn, state_conv, cache_swa_k, cache_swa_v, cache_mem_k, cache_mem_v, mem_prompt, norm_ffn1, ffn1_w1, ffn1_w3, ffn1_w2, norm_mix, w_in, conv_w, gdn_A_log, gdn_dt_bias, gdn_norm, swa_sinks, w_br_gdn, w_br_swa, w_out, norm_mem_q, norm_mem_kv, w_mem_q, w_mem_k, w_mem_v, w_mem_o, norm_ffn2, ffn2_w1, ffn2_w3, ffn2_w2, norm_final):
    Bp, Tp, D = x_prompt.shape
    Bs, Ts, _ = x_sample.shape
    assert Ts == 1
    depth = norm_ffn1.shape[0]
    n_mem = mem_prompt.shape[1]
    group = SWA_HEADS // SWA_KV_HEADS

    off_gs = OFF_GG + D
    off_q = off_gs + D
    off_kv = off_q + SWA_Q_W
    off_ba = off_kv + 2 * SWA_KV_W
    d_in_pad = -(-(off_ba + LANES) // PROJ_TN) * PROJ_TN
    o_b = CONV_CH + GDN_W
    o_q = o_b + 2 * GDN_HEADS
    o_gg = o_q + SWA_Q_W + 2 * SWA_KV_W

    cos_p, sin_p = _rope_tables(jnp.arange(Tp, dtype=jnp.int32))
    cos_s, sin_s = _rope_tables(PAST_LEN + jnp.arange(Ts, dtype=jnp.int32))
    eye_kv = jnp.eye(SWA_KV_HEADS, dtype=f32)
    row = lambda v: v.reshape(1, -1)

    hp = x_prompt.reshape(Bp * Tp, D)
    hs = x_sample.reshape(Bs, D)
    outs = [[] for _ in range(10)]
    for l in range(depth):
        w_in_rt = _reorder_rows(jnp.swapaxes(w_in, 1, 2), l, ((0, o_b), (o_gg, w_in.shape[2]), (o_q, o_gg), (o_b, o_q)), d_in_pad)
        ffn1 = (row(norm_ffn1[l]), ffn1_w1[l].astype(bf16), ffn1_w3[l].astype(bf16), ffn1_w2[l].astype(bf16))
        last = l == depth - 1
        gfin = row(norm_final)
        avec = jnp.zeros((1, LANES), f32).at[0, GDN_HEADS : 2 * GDN_HEADS].set(gdn_A_log[l])
        dvec = jnp.zeros((1, LANES), f32).at[0, GDN_HEADS : 2 * GDN_HEADS].set(gdn_dt_bias[l])
        gdn_common = (conv_w[l], avec, dvec, row(gdn_norm[l]))
        tn_in = PROJ_TN

        x1 = _ffn(hp, *ffn1, gfin, False)
        proj = _norm_matmul(x1, row(norm_mix[l]), w_in_rt, tn_in, w_transposed=True)
        later = (ffn2_w1[l], ffn2_w3[l], ffn2_w2[l], w_br_gdn[l], w_br_swa[l], w_out[l], w_mem_q[l],
                 w_mem_k[l], w_mem_v[l], w_mem_o[l])
        og, s_new, later = _gdn_prompt(proj, Bp, Tp, off_ba, *gdn_common, later)
        ffn2 = (row(norm_ffn2[l]), *later[0:3])
        merge_w = (*later[3:6], row(norm_mem_q[l]), later[6])
        wmk, wmv, wmo = later[7:10]
        os_, kc = _swa_prompt(proj, Bp, Tp, off_q, off_kv, cos_p, sin_p, swa_sinks[l])
        mem_x = mem_prompt.reshape(Bp * n_mem, D)
        mk, mv, mkb, mvb = _mem_kv(mem_x, row(norm_mem_kv[l]), wmk, wmv)
        mem = (mkb.reshape(Bp, n_mem, MEM_W), mvb.reshape(Bp, n_mem, MEM_W), wmo, Tp)
        x3 = _merge(og, os_, proj, OFF_GG, x1, *merge_w, mem=mem)
        hp = _ffn(x3, *ffn2, gfin, last)
        proj_b = proj.reshape(Bp, Tp, d_in_pad)
        outs[0].append(s_new)
        outs[1].append(proj_b[:, Tp - (GDN_CONV - 1) :, :CONV_CH])
        outs[2].append(kc.reshape(Bp, WINDOW, SWA_KV_HEADS, SWA_HD))
        outs[3].append(proj_b[:, Tp - WINDOW :, off_kv + SWA_KV_W : off_kv + 2 * SWA_KV_W].reshape(Bp, WINDOW, SWA_KV_HEADS, SWA_HD))
        outs[4].append(mk.reshape(Bp, n_mem, MEM_HEADS, MEM_HD))
        outs[5].append(mv.reshape(Bp, n_mem, MEM_HEADS, MEM_HD))

        x1 = _ffn(hs, *ffn1, gfin, False)
        proj = _norm_matmul(x1, row(norm_mix[l]), w_in_rt, tn_in, w_transposed=True)
        proj3 = proj.reshape(Bs, 1, d_in_pad)
        og, s_new = _gdn_step(proj3, off_ba, state_conv[l], state_gdn[l], *gdn_common)
        q_raw = proj[:, off_q : off_q + SWA_Q_W].reshape(Bs, SWA_KV_HEADS, group, 1, SWA_HD)
        q_exp = (q_raw * eye_kv[None, :, None, :, None]).reshape(Bs, SWA_HEADS, SWA_KV_W)
        ck = jnp.swapaxes(cache_swa_k[l].reshape(Bs, WINDOW, SWA_KV_W), 1, 2)
        cv = jnp.swapaxes(cache_swa_v[l].reshape(Bs, WINDOW, SWA_KV_W), 1, 2)
        r, nk, nv = _swa_step(q_exp, proj3, off_kv, ck, cv, cos_s, sin_s, swa_sinks[l].reshape(SWA_HEADS, 1))
        r5 = r.reshape(Bs, SWA_KV_HEADS, group, SWA_KV_HEADS, SWA_HD)
        kvh = jnp.arange(SWA_KV_HEADS)
        os_ = jnp.transpose(r5[:, kvh, :, kvh, :], (1, 0, 2, 3)).reshape(Bs, SWA_Q_W).astype(bf16)
        x2, qm = _merge(og.reshape(Bs, GDN_W), os_, proj, OFF_GG, x1, *merge_w)
        q8 = jnp.tile(qm.reshape(Bs, MEM_HEADS, MEM_HD), (1, 2, 1))
        om = _mem_attn_step(q8, cache_mem_k[l].reshape(Bs, n_mem * MEM_HEADS, MEM_HD),
                            cache_mem_v[l].reshape(Bs, n_mem * MEM_HEADS, MEM_HD))
        x3 = _proj_residual(om[:, :MEM_HEADS].reshape(Bs, MEM_W), wmo, x2)
        hs = _ffn(x3, *ffn2, gfin, last)
        outs[6].append(s_new)
        outs[7].append(jnp.concatenate([state_conv[l][:, 1:], proj3[:, :, :CONV_CH]], axis=1))
        outs[8].append(jnp.swapaxes(nk, 1, 2).reshape(Bs, WINDOW, SWA_KV_HEADS, SWA_HD))
        outs[9].append(jnp.swapaxes(nv, 1, 2).reshape(Bs, WINDOW, SWA_KV_HEADS, SWA_HD))

    return (hp.reshape(Bp, Tp, D), hs.reshape(Bs, Ts, D), *(jnp.stack(o) for o in outs))
```

```python
import functools
import math

import jax
import jax.numpy as jnp
from jax import lax
from jax.experimental import pallas as pl
from jax.experimental.pallas import tpu as pltpu

f32 = jnp.float32
bf16 = jnp.bfloat16

PAST_LEN = 16384
GDN_HEADS = 8
GDN_D = 128
GDN_CONV = 4
SWA_HEADS = 16
SWA_KV_HEADS = 4
SWA_HD = 64
WINDOW = 128
ROPE_THETA = 10000.0
MEM_HEADS = 4
MEM_HD = 128
EPS = 1e-6
L2_EPS = 1e-6

LANES = 128
SUBLANES = 8
VMEM_LIMIT = 58 * 1024 * 1024

FFN_TM = 1024
FFN_TF = 512
FFN_SLAB = 64
PROJ_TM = 1024
PROJ_TN = 768
MERGE_TM = 256
CHUNK = 128
GDN_STEP_CHUNKS = 2
SWA_STEP_BLOCKS = 4
SWA_STEP_BATCH = 8
MEM_STEP_BATCH = 4

GDN_W = GDN_HEADS * GDN_D
CONV_CH = 3 * GDN_W
SWA_Q_W = SWA_HEADS * SWA_HD
SWA_KV_W = SWA_KV_HEADS * SWA_HD
MEM_W = MEM_HEADS * MEM_HD

OFF_Z = CONV_CH
OFF_GG = OFF_Z + GDN_W


def _cparams(semantics):
    return pltpu.CompilerParams(dimension_semantics=semantics, vmem_limit_bytes=VMEM_LIMIT)


def _dot(a, b):
    return jnp.dot(a, b, preferred_element_type=f32)


def _dot_nt(a, b):
    return lax.dot_general(a, b, (((1,), (1,)), ((), ())), preferred_element_type=f32)


def _rmsnorm(x, g):
    return x * lax.rsqrt(jnp.mean(x * x, -1, keepdims=True) + EPS) * g


def _silu(x):
    return x * jax.nn.sigmoid(x)


def _softplus(x):
    return jnp.maximum(x, 0.0) + jnp.log1p(jnp.exp(-jnp.abs(x)))


def _ffn_kernel(x_ref, g_ref, w1_ref, w3_ref, w2_ref, gf_ref, o_ref, h_ref, *, final_norm):
    j = pl.program_id(1)
    slab = min(FFN_SLAB, x_ref.shape[0])
    n_slabs = x_ref.shape[0] // slab

    def over_slabs(body):
        def step(s, carry):
            body(pl.ds(pl.multiple_of(s * slab, slab), slab))
            return carry
        lax.fori_loop(0, n_slabs, step, 0, unroll=min(4, n_slabs))

    @pl.when(j == 0)
    def _():
        def prologue(rows):
            h_ref[rows, :] = _rmsnorm(x_ref[rows, :], g_ref[...]).astype(bf16)
            o_ref[rows, :] = jnp.zeros((slab, o_ref.shape[1]), f32)
        over_slabs(prologue)

    h = h_ref[...]
    a = _dot(h, w1_ref[...])
    b = _dot(h, w3_ref[...])
    o_ref[...] += _dot((_silu(a) * b).astype(bf16), w2_ref[...])

    @pl.when(j == pl.num_programs(1) - 1)
    def _():
        def epilogue(rows):
            y = x_ref[rows, :] + 0.5 * o_ref[rows, :]
            if final_norm:
                y = _rmsnorm(y, gf_ref[...])
            o_ref[rows, :] = y
        over_slabs(epilogue)


def _ffn(x, g, w1, w3, w2, gf, final_norm):
    M, D = x.shape
    F = w1.shape[1]
    tm = min(FFN_TM, M)
    tf = FFN_TF
    assert M % tm == 0 and F % tf == 0
    return pl.pallas_call(
        functools.partial(_ffn_kernel, final_norm=final_norm),
        grid=(M // tm, F // tf),
        in_specs=[
            pl.BlockSpec((tm, D), lambda i, j: (i, 0)),
            pl.BlockSpec((1, D), lambda i, j: (0, 0)),
            pl.BlockSpec((D, tf), lambda i, j: (0, j)),
            pl.BlockSpec((D, tf), lambda i, j: (0, j)),
            pl.BlockSpec((tf, D), lambda i, j: (j, 0)),
            pl.BlockSpec((1, D), lambda i, j: (0, 0)),
        ],
        out_specs=pl.BlockSpec((tm, D), lambda i, j: (i, 0)),
        out_shape=jax.ShapeDtypeStruct((M, D), f32),
        scratch_shapes=[pltpu.VMEM((tm, D), bf16)],
        compiler_params=_cparams(("parallel", "arbitrary")),
        name="ffn",
    )(x, g, w1, w3, w2, gf)


def _norm_matmul_kernel(x_ref, g_ref, w_ref, o_ref, h_ref, *, w_transposed):
    @pl.when(pl.program_id(1) == 0)
    def _():
        h_ref[...] = _rmsnorm(x_ref[...], g_ref[...]).astype(bf16)

    o_ref[...] = (_dot_nt if w_transposed else _dot)(h_ref[...], w_ref[...])


def _norm_matmul(x, g, w, tn, w_transposed=False):
    M, D = x.shape
    N = w.shape[0] if w_transposed else w.shape[1]
    tm = min(PROJ_TM, M)
    assert M % tm == 0 and N % tn == 0
    w_spec = pl.BlockSpec((tn, D), lambda i, j: (j, 0)) if w_transposed else pl.BlockSpec((D, tn), lambda i, j: (0, j))
    return pl.pallas_call(
        functools.partial(_norm_matmul_kernel, w_transposed=w_transposed),
        grid=(M // tm, N // tn),
        in_specs=[
            pl.BlockSpec((tm, D), lambda i, j: (i, 0)),
            pl.BlockSpec((1, D), lambda i, j: (0, 0)),
            w_spec,
        ],
        out_specs=pl.BlockSpec((tm, tn), lambda i, j: (i, j)),
        out_shape=jax.ShapeDtypeStruct((M, N), f32),
        scratch_shapes=[pltpu.VMEM((tm, D), bf16)],
        compiler_params=_cparams(("parallel", "arbitrary")),
        name="norm_matmul",
    )(x, g, w)


def _mem_kv_kernel(x_ref, g_ref, wk_ref, wv_ref, k_ref, v_ref, kb_ref, vb_ref):
    h = _rmsnorm(x_ref[...], g_ref[...]).astype(bf16)
    k = _dot(h, wk_ref[...])
    v = _dot(h, wv_ref[...])
    kb_ref[...] = k.astype(bf16)
    vb_ref[...] = v.astype(bf16)
    tokens = x_ref.shape[0]
    for hd in range(MEM_HEADS):
        k_ref[pl.ds(hd, tokens, stride=MEM_HEADS), :] = k[:, hd * MEM_HD : (hd + 1) * MEM_HD]
        v_ref[pl.ds(hd, tokens, stride=MEM_HEADS), :] = v[:, hd * MEM_HD : (hd + 1) * MEM_HD]


def _mem_kv(x, g, wk, wv):
    M, D = x.shape
    N = wk.shape[1]
    tm = min(PROJ_TM, M)
    assert M % tm == 0 and N == MEM_W
    row_spec = pl.BlockSpec((tm, N), lambda i: (i, 0))
    head_row_spec = pl.BlockSpec((tm * MEM_HEADS, MEM_HD), lambda i: (i, 0))
    const = lambda shape: pl.BlockSpec(shape, lambda i: (0, 0))
    return pl.pallas_call(
        _mem_kv_kernel,
        grid=(M // tm,),
        in_specs=[pl.BlockSpec((tm, D), lambda i: (i, 0)), const((1, D)), const((D, N)), const((D, N))],
        out_specs=[head_row_spec] * 2 + [row_spec] * 2,
        out_shape=[jax.ShapeDtypeStruct((M * MEM_HEADS, MEM_HD), f32)] * 2 + [jax.ShapeDtypeStruct((M, N), bf16)] * 2,
        compiler_params=_cparams(("parallel",)),
        name="mem_kv",
    )(x, g, wk, wv)


def _bdot(a, b):
    return _dot(a.astype(bf16), b.astype(bf16))


def _unit_lower_inverse(nmats, row, col):
    eye = jnp.where(row == col, 1.0, 0.0).astype(f32)

    def same_block(size):
        return (row // size) == (col // size)

    blk = same_block(SUBLANES)
    a1 = [jnp.where(blk, n, 0.0) for n in nmats]
    a2 = [_bdot(a, a) for a in a1]
    a4 = [_bdot(a, a) for a in a2]
    ts = [_bdot(eye - x1, eye + x2) for x1, x2 in zip(a1, a2)]
    ts = [_bdot(t, eye + x4) for t, x4 in zip(ts, a4)]
    size = SUBLANES
    while size < CHUNK:
        nxt = same_block(2 * size)
        sel = jnp.logical_and(nxt, jnp.logical_not(blk))
        tbs = [t.astype(bf16) for t in ts]
        xs = [_dot(jnp.where(sel, n, 0.0).astype(bf16), tb) for n, tb in zip(nmats, tbs)]
        ts = [t - _dot(tb, x.astype(bf16)) for t, tb, x in zip(ts, tbs, xs)]
        blk = nxt
        size *= 2
    return ts


def _conv_silu_slab(x_ref, top_ref, w_ref, sl):
    rows = x_ref.shape[0]
    y = None
    for tap in range(GDN_CONV):
        back = GDN_CONV - 1 - tap
        window = jnp.concatenate(
            [top_ref[SUBLANES - back : 2 * SUBLANES - back, sl], x_ref[SUBLANES - back : rows - back, sl]], axis=0)
        term = window * w_ref[tap : tap + 1, sl]
        y = term if y is None else y + term
    return _silu(y)


def _delta_rule_prepare(pairs, gcum, gcum_t, beta_t, row, col, q_ref, k_ref, v_ref):
    n = range(len(pairs))
    incl = row >= col
    strict = row > col
    rows_of = lambda ci: slice(ci * CHUNK, (ci + 1) * CHUNK)
    gc = [gcum[rows_of(ci), GDN_HEADS + h : GDN_HEADS + h + 1] for ci, h in pairs]
    gr = [gcum_t[ci][GDN_HEADS + h : GDN_HEADS + h + 1, :] for ci, h in pairs]
    beta = [beta_t[rows_of(ci), h : h + 1] for ci, h in pairs]
    k = [k_ref[h, rows_of(ci), :] for ci, h in pairs]
    kb = [x.astype(bf16) for x in k]
    kk = [_dot_nt(kb[i], kb[i]) for i in n]
    qk = [_dot_nt(q_ref[h, rows_of(ci), :].astype(bf16), kb[i]) for i, (ci, h) in enumerate(pairs)]
    e = [jnp.exp(jnp.where(incl, gc[i] - gr[i], 0.0)) for i in n]
    nmat = [beta[i] * kk[i] * jnp.where(strict, e[i], 0.0) for i in n]
    qkd = [(qk[i] * jnp.where(incl, e[i], 0.0)).astype(bf16) for i in n]
    tinv = _unit_lower_inverse(nmat, row, col)
    gamma = [jnp.exp(gc[i]) for i in n]
    rhs = [jnp.concatenate([beta[i] * v_ref[h, rows_of(ci), :], (beta[i] * gamma[i]) * k[i]], axis=-1).astype(bf16)
           for i, (ci, h) in enumerate(pairs)]
    sol = [_dot(tinv[i].astype(bf16), rhs[i]) for i in n]
    return gc, gamma, sol, qkd


def _delta_rule_apply(ci, prepared, q_ref, k_ref, z_ref, gn_ref, s_ref, og_ref):
    gc, gamma, sol, qkd = prepared
    heads = range(GDN_HEADS)
    rows = slice(ci * CHUNK, (ci + 1) * CHUNK)
    sb = [s_ref[h].astype(bf16) for h in heads]
    ub = [(sol[h][:, :GDN_D] - _dot(sol[h][:, GDN_D:].astype(bf16), sb[h])).astype(bf16) for h in heads]
    o = [_dot((q_ref[h, rows, :] * gamma[h]).astype(bf16), sb[h]) + _dot(qkd[h], ub[h]) for h in heads]
    for h in heads:
        g_last = gc[h][CHUNK - 1 : CHUNK, :]
        k_end = k_ref[h, rows, :] * jnp.exp(g_last - gc[h])
        s_ref[h] = jnp.exp(g_last) * s_ref[h] + _dot(k_end.T.astype(bf16), ub[h])
        zh = z_ref[rows, h * GDN_D : (h + 1) * GDN_D]
        og_ref[rows, h * GDN_D : (h + 1) * GDN_D] = (_rmsnorm(o[h], gn_ref[...]) * _silu(zh)).astype(bf16)


def _gdn_prompt_kernel(*refs, n_cast):
    qkv_ref, z_ref, ba_ref, cw_ref, av_ref, dv_ref, gn_ref = refs[:7]
    cast_in = refs[7 : 7 + n_cast]
    og_ref, so_ref = refs[7 + n_cast : 9 + n_cast]
    cast_out = refs[9 + n_cast : 9 + 2 * n_cast]
    s_ref, carry_ref, q_ref, k_ref, v_ref = refs[9 + 2 * n_cast :]
    for src, dst in zip(cast_in, cast_out):
        dst[...] = src[...].astype(bf16)

    c = pl.program_id(1)
    rows = qkv_ref.shape[0]
    n_chunks = rows // CHUNK

    @pl.when(c == 0)
    def _():
        s_ref[...] = jnp.zeros_like(s_ref)
        carry_ref[0:SUBLANES, :] = jnp.zeros((SUBLANES, CONV_CH), f32)

    carry_ref[SUBLANES : 2 * SUBLANES, :] = qkv_ref[0:SUBLANES, :]
    for j in range(3 * GDN_HEADS):
        sl = slice(j * LANES, (j + 1) * LANES)
        y = _conv_silu_slab(qkv_ref, carry_ref, cw_ref, sl)
        h = j % GDN_HEADS
        if j < 2 * GDN_HEADS:
            y = y * lax.rsqrt(jnp.sum(y * y, -1, keepdims=True) + L2_EPS)
            if j < GDN_HEADS:
                q_ref[h] = y * (GDN_D ** -0.5)
            else:
                k_ref[h] = y
        else:
            v_ref[h] = y
    carry_ref[0:SUBLANES, :] = qkv_ref[rows - SUBLANES : rows, :]

    ba = ba_ref[...]
    beta_t = jax.nn.sigmoid(ba)
    g_t = -jnp.exp(av_ref[...]) * _softplus(ba + dv_ref[...])
    row_in_chunk = lax.broadcasted_iota(jnp.int32, (rows, LANES), 0) % CHUNK
    gcum = g_t
    shift = 1
    while shift < CHUNK:
        gcum = gcum + jnp.where(row_in_chunk >= shift, pltpu.roll(gcum, shift, 0), 0.0)
        shift *= 2
    gcum_t = [gcum[ci * CHUNK : (ci + 1) * CHUNK, :].T for ci in range(n_chunks)]

    row = lax.broadcasted_iota(jnp.int32, (CHUNK, CHUNK), 0)
    col = lax.broadcasted_iota(jnp.int32, (CHUNK, CHUNK), 1)
    pairs = [(ci, h) for ci in range(n_chunks) for h in range(GDN_HEADS)]
    gc, gamma, sol, qkd = _delta_rule_prepare(pairs, gcum, gcum_t, beta_t, row, col, q_ref, k_ref, v_ref)
    for ci in range(n_chunks):
        mine = slice(ci * GDN_HEADS, (ci + 1) * GDN_HEADS)
        _delta_rule_apply(ci, (gc[mine], gamma[mine], sol[mine], qkd[mine]), q_ref, k_ref, z_ref, gn_ref, s_ref, og_ref)

    @pl.when(c == pl.num_programs(1) - 1)
    def _():
        so_ref[0] = s_ref[...]


def _cast_block(shape, steps):
    R, C = shape
    for col_blocks in (1, 2, 4, 8, 16):
        row_blocks = steps // col_blocks
        if (steps % col_blocks == 0 and R % row_blocks == 0 and C % col_blocks == 0
                and (R // row_blocks) % (2 * SUBLANES) == 0 and (C // col_blocks) % LANES == 0):
            return (R // row_blocks, C // col_blocks), col_blocks
    return None


def _gdn_prompt(proj, B, T, off_ba, cw, avec, dvec, gn, weights):
    rows = GDN_STEP_CHUNKS * CHUNK
    assert T % rows == 0
    ns = T // rows
    step = lambda b, c: b * ns + c
    plans = [_cast_block(w.shape, B * ns) for w in weights]
    riders = [w for w, p in zip(weights, plans) if p is not None]
    cast_specs = [pl.BlockSpec(blk, functools.partial(lambda b, c, cb: (step(b, c) // cb, step(b, c) % cb), cb=cb))
                  for blk, cb in (p for p in plans if p is not None)]
    head_scratch = pltpu.VMEM((GDN_HEADS, rows, GDN_D), f32)
    out = pl.pallas_call(
        functools.partial(_gdn_prompt_kernel, n_cast=len(riders)),
        grid=(B, ns),
        in_specs=[
            pl.BlockSpec((rows, CONV_CH), lambda b, c: (step(b, c), 0)),
            pl.BlockSpec((rows, GDN_W), lambda b, c: (step(b, c), OFF_Z // GDN_W)),
            pl.BlockSpec((rows, LANES), lambda b, c: (step(b, c), off_ba // LANES)),
            pl.BlockSpec((GDN_CONV, CONV_CH), lambda b, c: (0, 0)),
            pl.BlockSpec((1, LANES), lambda b, c: (0, 0)),
            pl.BlockSpec((1, LANES), lambda b, c: (0, 0)),
            pl.BlockSpec((1, GDN_D), lambda b, c: (0, 0)),
        ] + cast_specs,
        out_specs=[
            pl.BlockSpec((rows, GDN_W), lambda b, c: (step(b, c), 0)),
            pl.BlockSpec((1, GDN_HEADS, GDN_D, GDN_D), lambda b, c: (b, 0, 0, 0)),
        ] + cast_specs,
        out_shape=[
            jax.ShapeDtypeStruct((B * T, GDN_W), bf16),
            jax.ShapeDtypeStruct((B, GDN_HEADS, GDN_D, GDN_D), f32),
        ] + [jax.ShapeDtypeStruct(w.shape, bf16) for w in riders],
        scratch_shapes=[
            pltpu.VMEM((GDN_HEADS, GDN_D, GDN_D), f32),
            pltpu.VMEM((2 * SUBLANES, CONV_CH), f32),
            head_scratch, head_scratch, head_scratch,
        ],
        compiler_params=_cparams(("arbitrary", "arbitrary")),
        name="gdn_prompt",
    )(proj, proj, proj, cw, avec, dvec, gn, *riders)
    converted = iter(out[2:])
    return out[0], out[1], [next(converted) if p is not None else w.astype(bf16) for w, p in zip(weights, plans)]


def _gdn_step_kernel(qkv_ref, z_ref, ba_ref, sc_ref, s0_ref, cw_ref, av_ref, dv_ref, gn_ref, og_ref, so_ref):
    x_new = qkv_ref[0]
    taps = cw_ref[...]
    y = jnp.sum(sc_ref[0] * taps[0 : GDN_CONV - 1, :], axis=0, keepdims=True) + x_new * taps[GDN_CONV - 1 : GDN_CONV, :]
    y = _silu(y)
    ba = ba_ref[0]
    beta_t = jax.nn.sigmoid(ba)
    gamma_t = jnp.exp(-jnp.exp(av_ref[...]) * _softplus(ba + dv_ref[...]))
    z = z_ref[0]
    heads = range(GDN_HEADS)
    head = lambda base, h: y[:, base + h * GDN_D : base + (h + 1) * GDN_D]
    q = [head(0, h) for h in heads]
    k = [head(GDN_W, h) for h in heads]
    v = [head(2 * GDN_W, h) for h in heads]
    q = [q[h] * lax.rsqrt(jnp.sum(q[h] * q[h], -1, keepdims=True) + L2_EPS) * (GDN_D ** -0.5) for h in heads]
    k = [k[h] * lax.rsqrt(jnp.sum(k[h] * k[h], -1, keepdims=True) + L2_EPS) for h in heads]
    beta = [beta_t[:, h : h + 1] for h in heads]
    gamma = [gamma_t[:, GDN_HEADS + h : GDN_HEADS + h + 1] for h in heads]
    k_col = [jnp.broadcast_to(k[h], (GDN_D, GDN_D)).T for h in heads]
    sub = lax.broadcasted_iota(jnp.int32, (SUBLANES, GDN_D), 0)
    kq = [jnp.where(sub == 0, k[h], jnp.where(sub == 1, q[h], 0.0)).astype(bf16) for h in heads]
    kq_s = [_dot(kq[h], s0_ref[0, h].astype(bf16)) for h in heads]
    k_s = [kq_s[h][0:1] for h in heads]
    q_s = [kq_s[h][1:2] for h in heads]
    u = [beta[h] * v[h] - (beta[h] * gamma[h]) * k_s[h] for h in heads]
    o = [gamma[h] * q_s[h] + jnp.sum(q[h] * k[h], -1, keepdims=True) * u[h] for h in heads]
    for h in heads:
        so_ref[0, h] = gamma[h] * s0_ref[0, h] + k_col[h] * u[h]
        zh = z[:, h * GDN_D : (h + 1) * GDN_D]
        og_ref[0, :, h * GDN_D : (h + 1) * GDN_D] = (_rmsnorm(o[h], gn_ref[...]) * _silu(zh)).astype(bf16)


def _gdn_step(proj3, off_ba, state_conv, state_gdn, cw, avec, dvec, gn):
    B = proj3.shape[0]
    return pl.pallas_call(
        _gdn_step_kernel,
        grid=(B,),
        in_specs=[
            pl.BlockSpec((1, 1, CONV_CH), lambda b: (b, 0, 0)),
            pl.BlockSpec((1, 1, GDN_W), lambda b: (b, 0, OFF_Z // GDN_W)),
            pl.BlockSpec((1, 1, LANES), lambda b: (b, 0, off_ba // LANES)),
            pl.BlockSpec((1, GDN_CONV - 1, CONV_CH), lambda b: (b, 0, 0)),
            pl.BlockSpec((1, GDN_HEADS, GDN_D, GDN_D), lambda b: (b, 0, 0, 0)),
            pl.BlockSpec((GDN_CONV, CONV_CH), lambda b: (0, 0)),
            pl.BlockSpec((1, LANES), lambda b: (0, 0)),
            pl.BlockSpec((1, LANES), lambda b: (0, 0)),
            pl.BlockSpec((1, GDN_D), lambda b: (0, 0)),
        ],
        out_specs=[
            pl.BlockSpec((1, 1, GDN_W), lambda b: (b, 0, 0)),
            pl.BlockSpec((1, GDN_HEADS, GDN_D, GDN_D), lambda b: (b, 0, 0, 0)),
        ],
        out_shape=[
            jax.ShapeDtypeStruct((B, 1, GDN_W), bf16),
            jax.ShapeDtypeStruct((B, GDN_HEADS, GDN_D, GDN_D), f32),
        ],
        compiler_params=_cparams(("parallel",)),
        name="gdn_step",
    )(proj3, proj3, proj3, state_conv, state_gdn, cw, avec, dvec, gn)


def _rope(x, cos, sin_signed):
    width = x.shape[-1]
    lane = lax.broadcasted_iota(jnp.int32, x.shape, x.ndim - 1)
    first_half = (lane % SWA_HD) < (SWA_HD // 2)
    rot = jnp.where(first_half, pltpu.roll(x, width - SWA_HD // 2, x.ndim - 1), pltpu.roll(x, SWA_HD // 2, x.ndim - 1))
    return x * cos + rot * sin_signed


def _head_halves(x2, head_parity, lane):
    swapped = pltpu.roll(x2, SWA_HD, 1)
    lo_src, hi_src = (x2, swapped) if head_parity == 0 else (swapped, x2)
    return jnp.where(lane < SWA_HD, lo_src, 0.0), jnp.where(lane >= SWA_HD, hi_src, 0.0)


def _swa_prompt_kernel(sinks_ref, q_ref, kv_ref, cos_ref, sin_ref, os_ref, kc_ref, kprev_ref, vprev_ref):
    n = pl.program_id(1)
    n_blk = q_ref.shape[0] // WINDOW

    @pl.when(n == 0)
    def _():
        kprev_ref[...] = jnp.zeros_like(kprev_ref)
        vprev_ref[...] = jnp.zeros_like(vprev_ref)

    row = lax.broadcasted_iota(jnp.int32, (WINDOW, WINDOW), 0)
    col = lax.broadcasted_iota(jnp.int32, (WINDOW, WINDOW), 1)
    own = col <= row
    first_bias = jnp.where(n > 0, 0.0, -jnp.inf)
    lane = lax.broadcasted_iota(jnp.int32, (WINDOW, LANES), 1)
    scale = SWA_HD ** -0.5
    group = SWA_HEADS // SWA_KV_HEADS

    k_own, v_own, k_pre, v_pre, q2 = [], [], [], [], []
    for blk in range(n_blk):
        rows = slice(blk * WINDOW, (blk + 1) * WINDOW)
        cos = cos_ref[rows, :]
        sin = sin_ref[rows, :]
        k_cur = _rope(kv_ref[rows, :SWA_KV_W], cos, sin)
        v_cur = kv_ref[rows, SWA_KV_W:]
        if blk == n_blk - 1:
            kc_ref[0] = k_cur
        k_own.append([]), v_own.append([]), k_pre.append([]), v_pre.append([]), q2.append([])
        for h in range(SWA_KV_HEADS):
            pair = slice((h // 2) * LANES, (h // 2 + 1) * LANES)
            k_own[blk].append([a.astype(bf16) for a in _head_halves(k_cur[:, pair], h % 2, lane)])
            v_own[blk].append([a.astype(bf16) for a in _head_halves(v_cur[:, pair], h % 2, lane)])
            if blk == 0:
                k_pre[blk].append([kprev_ref[h, par] for par in range(2)])
                v_pre[blk].append([vprev_ref[h, par] for par in range(2)])
            else:
                k_pre[blk].append(k_own[blk - 1][h])
                v_pre[blk].append(v_own[blk - 1][h])
            if blk == n_blk - 1:
                for par in range(2):
                    kprev_ref[h, par] = k_own[blk][h][par]
                    vprev_ref[h, par] = v_own[blk][h][par]
            q_h = _rope(q_ref[rows, h * group * SWA_HD : (h + 1) * group * SWA_HD], cos, sin) * scale
            q2[blk].append([q_h[:, j * LANES : (j + 1) * LANES].astype(bf16) for j in range(group // 2)])

    heads = [(blk, h, j, par) for blk in range(n_blk) for h in range(SWA_KV_HEADS)
             for j in range(group // 2) for par in range(2)]
    idx = range(len(heads))
    sink = [sinks_ref[h * group + 2 * j + par] for blk, h, j, par in heads]
    s = [jnp.where(own, _dot_nt(q2[blk][h][j], k_own[blk][h][par]),
                   _dot_nt(q2[blk][h][j], k_pre[blk][h][par]) + (first_bias if blk == 0 else 0.0))
         for blk, h, j, par in heads]
    m = [jnp.maximum(jnp.max(s[i], -1, keepdims=True), sink[i]) for i in idx]
    p = [jnp.exp(s[i] - m[i]) for i in idx]
    inv = [1.0 / (jnp.sum(p[i], -1, keepdims=True) + jnp.exp(sink[i] - m[i])) for i in idx]
    o = [(_dot(jnp.where(own, p[i], 0.0).astype(bf16), v_own[blk][h][par])
          + _dot(jnp.where(own, 0.0, p[i]).astype(bf16), v_pre[blk][h][par])) * inv[i]
         for i, (blk, h, j, par) in enumerate(heads)]
    for i in range(0, len(heads), 2):
        blk, head = divmod(i, SWA_HEADS)
        os_ref[blk * WINDOW : (blk + 1) * WINDOW, head * SWA_HD : head * SWA_HD + LANES] = (o[i] + o[i + 1]).astype(bf16)


def _swa_prompt(proj, B, T, off_q, off_kv, cos, sin, sinks):
    assert T % WINDOW == 0 and SWA_KV_W == 2 * LANES
    rows = math.gcd(SWA_STEP_BLOCKS, T // WINDOW) * WINDOW
    nb = T // rows
    return pl.pallas_call(
        _swa_prompt_kernel,
        grid=(B, nb),
        in_specs=[
            pl.BlockSpec(memory_space=pltpu.SMEM),
            pl.BlockSpec((rows, SWA_Q_W), lambda b, n: (b * nb + n, off_q // SWA_Q_W)),
            pl.BlockSpec((rows, 2 * SWA_KV_W), lambda b, n: (b * nb + n, off_kv // (2 * SWA_KV_W))),
            pl.BlockSpec((rows, SWA_KV_W), lambda b, n: (n, 0)),
            pl.BlockSpec((rows, SWA_KV_W), lambda b, n: (n, 0)),
        ],
        out_specs=[
            pl.BlockSpec((rows, SWA_Q_W), lambda b, n: (b * nb + n, 0)),
            pl.BlockSpec((1, WINDOW, SWA_KV_W), lambda b, n: (b, 0, 0)),
        ],
        out_shape=[
            jax.ShapeDtypeStruct((B * T, SWA_Q_W), bf16),
            jax.ShapeDtypeStruct((B, WINDOW, SWA_KV_W), f32),
        ],
        scratch_shapes=[pltpu.VMEM((SWA_KV_HEADS, 2, WINDOW, LANES), bf16)] * 2,
        compiler_params=_cparams(("parallel", "arbitrary")),
        name="swa_prompt",
    )(sinks, proj, proj, cos, sin)


def _swa_step_kernel(qe_ref, kv_ref, ck_ref, cv_ref, cos_ref, sin_ref, sinks_ref, r_ref, nk_ref, nv_ref):
    cos = cos_ref[...]
    sin = sin_ref[...]
    sink = sinks_ref[...]
    row = lax.broadcasted_iota(jnp.int32, (WINDOW, SWA_KV_W), 0)
    scale = SWA_HD ** -0.5
    seqs = range(qe_ref.shape[0])
    kv = [kv_ref[i] for i in seqs]
    k_new = [_rope(kv[i][:, :SWA_KV_W], cos, sin) for i in seqs]
    keys = [jnp.where(row == WINDOW - 1, k_new[i], pltpu.roll(ck_ref[i].T, WINDOW - 1, 0)) for i in seqs]
    vals = [jnp.where(row == WINDOW - 1, kv[i][:, SWA_KV_W:], pltpu.roll(cv_ref[i].T, WINDOW - 1, 0)) for i in seqs]
    for i in seqs:
        nk_ref[i] = keys[i].T
        nv_ref[i] = vals[i].T
    q = [_rope(qe_ref[i], cos, sin) for i in seqs]
    s = [_dot_nt(q[i].astype(bf16), keys[i].astype(bf16)) * scale for i in seqs]
    m = [jnp.maximum(jnp.max(s[i], -1, keepdims=True), sink) for i in seqs]
    p = [jnp.exp(s[i] - m[i]) for i in seqs]
    denom = [jnp.sum(p[i], -1, keepdims=True) + jnp.exp(sink - m[i]) for i in seqs]
    for i in seqs:
        r_ref[i] = _dot((p[i] / denom[i]).astype(bf16), vals[i].astype(bf16))


def _swa_step(q_exp, proj3, off_kv, cache_k, cache_v, cos, sin, sinks_col):
    B = q_exp.shape[0]
    bb = math.gcd(B, SWA_STEP_BATCH)
    assert cache_k.shape[2] == WINDOW
    return pl.pallas_call(
        _swa_step_kernel,
        grid=(B // bb,),
        in_specs=[
            pl.BlockSpec((bb, SWA_HEADS, SWA_KV_W), lambda i: (i, 0, 0)),
            pl.BlockSpec((bb, 1, 2 * SWA_KV_W), lambda i: (i, 0, off_kv // (2 * SWA_KV_W))),
            pl.BlockSpec((bb, SWA_KV_W, WINDOW), lambda i: (i, 0, 0)),
            pl.BlockSpec((bb, SWA_KV_W, WINDOW), lambda i: (i, 0, 0)),
            pl.BlockSpec((1, SWA_KV_W), lambda i: (0, 0)),
            pl.BlockSpec((1, SWA_KV_W), lambda i: (0, 0)),
            pl.BlockSpec((SWA_HEADS, 1), lambda i: (0, 0)),
        ],
        out_specs=[
            pl.BlockSpec((bb, SWA_HEADS, SWA_KV_W), lambda i: (i, 0, 0)),
            pl.BlockSpec((bb, SWA_KV_W, WINDOW), lambda i: (i, 0, 0)),
            pl.BlockSpec((bb, SWA_KV_W, WINDOW), lambda i: (i, 0, 0)),
        ],
        out_shape=[
            jax.ShapeDtypeStruct((B, SWA_HEADS, SWA_KV_W), f32),
            jax.ShapeDtypeStruct((B, SWA_KV_W, WINDOW), f32),
            jax.ShapeDtypeStruct((B, SWA_KV_W, WINDOW), f32),
        ],
        compiler_params=_cparams(("parallel",)),
        name="swa_step",
    )(q_exp, proj3, cache_k, cache_v, cos, sin, sinks_col)


def _merge_core(og_ref, os_ref, gg_ref, gs_ref, x_ref, wg_ref, ws_ref, wo_ref, gq_ref, wq_ref):
    p_gdn = _dot(og_ref[...], wg_ref[...])
    p_swa = _dot(os_ref[...], ws_ref[...])
    merged = jax.nn.sigmoid(gg_ref[...]) * p_gdn + jax.nn.sigmoid(gs_ref[...]) * p_swa
    x_new = x_ref[...] + _dot(merged.astype(bf16), wo_ref[...])
    return x_new, _dot(_rmsnorm(x_new, gq_ref[...]).astype(bf16), wq_ref[...])


def _merge_kernel(*refs):
    xo_ref, qm_ref = refs[-2:]
    xo_ref[...], qm_ref[...] = _merge_core(*refs[:-2])


def _merge_mem_kernel(*refs):
    mk_ref, mv_ref, wmo_ref, xo_ref = refs[-4:]
    x_new, q = _merge_core(*refs[:-4])
    q = q.astype(bf16)
    scale = MEM_HD ** -0.5
    heads = range(MEM_HEADS)
    cols = lambda h: slice(h * MEM_HD, (h + 1) * MEM_HD)
    s = [_dot_nt(q[:, cols(h)], mk_ref[0, :, cols(h)]) * scale for h in heads]
    p = [jnp.exp(s[h] - jnp.max(s[h], -1, keepdims=True)) for h in heads]
    p = [p[h] / jnp.sum(p[h], -1, keepdims=True) for h in heads]
    o = [_dot(p[h].astype(bf16), mv_ref[0, :, cols(h)]) for h in heads]
    xo_ref[...] = x_new + _dot(jnp.concatenate(o, axis=-1).astype(bf16), wmo_ref[...])


def _merge(og, os_, proj, off_gg, x, wg, ws, wo, gq, wq, mem=None):
    M, D = x.shape
    tm = min(MERGE_TM, M)
    assert M % tm == 0 and off_gg % D == 0
    const = lambda shape: pl.BlockSpec(shape, lambda i: (0, 0), pipeline_mode=pl.Buffered(1))
    in_specs = [
        pl.BlockSpec((tm, GDN_W), lambda i: (i, 0)),
        pl.BlockSpec((tm, SWA_Q_W), lambda i: (i, 0)),
        pl.BlockSpec((tm, D), lambda i: (i, off_gg // D)),
        pl.BlockSpec((tm, D), lambda i: (i, off_gg // D + 1)),
        pl.BlockSpec((tm, D), lambda i: (i, 0)),
        const((GDN_W, D)),
        const((SWA_Q_W, D)),
        const((D, D)),
        const((1, D)),
        const((D, MEM_W)),
    ]
    args = [og, os_, proj, proj, x, wg, ws, wo, gq, wq]
    x_spec = pl.BlockSpec((tm, D), lambda i: (i, 0))
    x_shape = jax.ShapeDtypeStruct((M, D), f32)
    if mem is None:
        return pl.pallas_call(
            _merge_kernel,
            grid=(M // tm,),
            in_specs=in_specs,
            out_specs=[x_spec, pl.BlockSpec((tm, MEM_W), lambda i: (i, 0))],
            out_shape=[x_shape, jax.ShapeDtypeStruct((M, MEM_W), f32)],
            compiler_params=_cparams(("parallel",)),
            name="merge",
        )(*args)
    mem_k, mem_v, wmo, seq_rows = mem
    assert seq_rows % tm == 0
    mt = mem_k.shape[1]
    mem_spec = pl.BlockSpec((1, mt, MEM_W), lambda i: (i // (seq_rows // tm), 0, 0))
    return pl.pallas_call(
        _merge_mem_kernel,
        grid=(M // tm,),
        in_specs=in_specs + [mem_spec, mem_spec, const((MEM_W, D))],
        out_specs=x_spec,
        out_shape=x_shape,
        compiler_params=_cparams(("parallel",)),
        name="merge_mem",
    )(*args, mem_k, mem_v, wmo)


def _mem_attn_step_kernel(q_ref, k_ref, v_ref, o_ref):
    mt2 = k_ref.shape[1] // SUBLANES
    for i in range(q_ref.shape[0]):
        q8 = q_ref[i] * (MEM_HD ** -0.5)
        s = jnp.sum(k_ref[i].reshape(mt2, SUBLANES, MEM_HD) * q8, axis=-1, keepdims=True)
        m = jnp.max(s, axis=0)
        m = jnp.maximum(m, pltpu.roll(m, MEM_HEADS, 0))
        p = jnp.exp(s - m)
        l = jnp.sum(p, axis=0)
        l = l + pltpu.roll(l, MEM_HEADS, 0)
        o = jnp.sum(p * v_ref[i].reshape(mt2, SUBLANES, MEM_HD), axis=0)
        o_ref[i] = (o + pltpu.roll(o, MEM_HEADS, 0)) / l


def _mem_attn_step(q8, mem_k, mem_v):
    B, rows, _ = mem_k.shape
    assert 2 * MEM_HEADS == SUBLANES and rows % SUBLANES == 0
    bb = math.gcd(B, MEM_STEP_BATCH)
    return pl.pallas_call(
        _mem_attn_step_kernel,
        grid=(B // bb,),
        in_specs=[
            pl.BlockSpec((bb, SUBLANES, MEM_HD), lambda i: (i, 0, 0)),
            pl.BlockSpec((bb, rows, MEM_HD), lambda i: (i, 0, 0)),
            pl.BlockSpec((bb, rows, MEM_HD), lambda i: (i, 0, 0)),
        ],
        out_specs=pl.BlockSpec((bb, SUBLANES, MEM_HD), lambda i: (i, 0, 0)),
        out_shape=jax.ShapeDtypeStruct((B, SUBLANES, MEM_HD), f32),
        compiler_params=_cparams(("parallel",)),
        name="mem_attn_step",
    )(q8, mem_k, mem_v)


def _proj_residual_kernel(a_ref, w_ref, x_ref, o_ref):
    o_ref[...] = x_ref[...] + _dot(a_ref[...].astype(bf16), w_ref[...])


def _proj_residual(a, w, x):
    M, D = x.shape
    return pl.pallas_call(
        _proj_residual_kernel,
        out_shape=jax.ShapeDtypeStruct((M, D), f32),
        compiler_params=pltpu.CompilerParams(vmem_limit_bytes=VMEM_LIMIT),
        name="proj_residual",
    )(a, w, x)


def _reorder_rows_kernel(w_ref, o_ref, *, pieces):
    at = 0
    for lo, hi in pieces:
        o_ref[at : at + hi - lo, :] = w_ref[lo:hi, :].astype(bf16)
        at += hi - lo
    o_ref[at:, :] = jnp.zeros((o_ref.shape[0] - at, o_ref.shape[1]), bf16)


def _reorder_rows(wt, layer, pieces, height):
    _, d_in, D = wt.shape
    assert all(lo % (2 * SUBLANES) == 0 and hi % (2 * SUBLANES) == 0 for lo, hi in pieces) and D % LANES == 0
    return pl.pallas_call(
        functools.partial(_reorder_rows_kernel, pieces=pieces),
        grid=(D // LANES,),
        in_specs=[pl.BlockSpec((None, d_in, LANES), lambda i: (layer, 0, i))],
        out_specs=pl.BlockSpec((height, LANES), lambda i: (0, i)),
        out_shape=jax.ShapeDtypeStruct((height, D), bf16),
        compiler_params=_cparams(("parallel",)),
        name="reorder_rows",
    )(wt)


def _rope_tables(pos):
    half = SWA_HD // 2
    inv_freq = ROPE_THETA ** (-jnp.arange(half, dtype=f32) / half)
    ang = pos.astype(f32)[:, None] * inv_freq[None, :]
    cos = jnp.cos(ang)
    sin = jnp.sin(ang)
    reps = SWA_KV_W // SWA_HD
    return jnp.tile(jnp.concatenate([cos, cos], -1), (1, reps)), jnp.tile(jnp.concatenate([-sin, sin], -1), (1, reps))


def kernel(x_prompt, x_sample, state_gdn, state_conv, cache_swa_k, cache_swa_v, cache_mem_k, cache_mem_v, mem_prompt, norm_ffn1, ffn1_w1, ffn1_w3, ffn1_w2, norm_mix, w_in, conv_w, gdn_A_log, gdn_dt_bias, gdn_norm, swa_sinks, w_br_gdn, w_br_swa, w_out, norm_mem_q, norm_mem_kv, w_mem_q, w_mem_k, w_mem_v, w_mem_o, norm_ffn2, ffn2_w1, ffn2_w3, ffn2_w2, norm_final):
    Bp, Tp, D = x_prompt.shape
    Bs, Ts, _ = x_sample.shape
    assert Ts == 1
    depth = norm_ffn1.shape[0]
    n_mem = mem_prompt.shape[1]
    group = SWA_HEADS // SWA_KV_HEADS

    off_gs = OFF_GG + D
    off_q = off_gs + D
    off_kv = off_q + SWA_Q_W
    off_ba = off_kv + 2 * SWA_KV_W
    d_in_pad = -(-(off_ba + LANES) // PROJ_TN) * PROJ_TN
    o_b = CONV_CH + GDN_W
    o_q = o_b + 2 * GDN_HEADS
    o_gg = o_q + SWA_Q_W + 2 * SWA_KV_W

    cos_p, sin_p = _rope_tables(jnp.arange(Tp, dtype=jnp.int32))
    cos_s, sin_s = _rope_tables(PAST_LEN + jnp.arange(Ts, dtype=jnp.int32))
    eye_kv = jnp.eye(SWA_KV_HEADS, dtype=f32)
    row = lambda v: v.reshape(1, -1)

    hp = x_prompt.reshape(Bp * Tp, D)
    hs = x_sample.reshape(Bs, D)
    outs = [[] for _ in range(10)]
    for l in range(depth):
        w_in_rt = _reorder_rows(jnp.swapaxes(w_in, 1, 2), l, ((0, o_b), (o_gg, w_in.shape[2]), (o_q, o_gg), (o_b, o_q)), d_in_pad)
        ffn1 = (row(norm_ffn1[l]), ffn1_w1[l].astype(bf16), ffn1_w3[l].astype(bf16), ffn1_w2[l].astype(bf16))
        last = l == depth - 1
        gfin = row(norm_final)
        avec = jnp.zeros((1, LANES), f32).at[0, GDN_HEADS : 2 * GDN_HEADS].set(gdn_A_log[l])
        dvec = jnp.zeros((1, LANES), f32).at[0, GDN_HEADS : 2 * GDN_HEADS].set(gdn_dt_bias[l])
        gdn_common = (conv_w[l], avec, dvec, row(gdn_norm[l]))

        x1 = _ffn(hp, *ffn1, gfin, False)
        proj = _norm_matmul(x1, row(norm_mix[l]), w_in_rt, PROJ_TN, w_transposed=True)
        later = (ffn2_w1[l], ffn2_w3[l], ffn2_w2[l], w_br_gdn[l], w_br_swa[l], w_out[l], w_mem_q[l],
                 w_mem_k[l], w_mem_v[l], w_mem_o[l])
        og, s_new, later = _gdn_prompt(proj, Bp, Tp, off_ba, *gdn_common, later)
        ffn2 = (row(norm_ffn2[l]), *later[0:3])
        merge_w = (*later[3:6], row(norm_mem_q[l]), later[6])
        wmk, wmv, wmo = later[7:10]
        os_, kc = _swa_prompt(proj, Bp, Tp, off_q, off_kv, cos_p, sin_p, swa_sinks[l])
        mem_x = mem_prompt.reshape(Bp * n_mem, D)
        mk, mv, mkb, mvb = _mem_kv(mem_x, row(norm_mem_kv[l]), wmk, wmv)
        mem = (mkb.reshape(Bp, n_mem, MEM_W), mvb.reshape(Bp, n_mem, MEM_W), wmo, Tp)
        x3 = _merge(og, os_, proj, OFF_GG, x1, *merge_w, mem=mem)
        hp = _ffn(x3, *ffn2, gfin, last)
        proj_b = proj.reshape(Bp, Tp, d_in_pad)
        outs[0].append(s_new)
        outs[1].append(proj_b[:, Tp - (GDN_CONV - 1) :, :CONV_CH])
        outs[2].append(kc.reshape(Bp, WINDOW, SWA_KV_HEADS, SWA_HD))
        outs[3].append(proj_b[:, Tp - WINDOW :, off_kv + SWA_KV_W : off_kv + 2 * SWA_KV_W].reshape(Bp, WINDOW, SWA_KV_HEADS, SWA_HD))
        outs[4].append(mk.reshape(Bp, n_mem, MEM_HEADS, MEM_HD))
        outs[5].append(mv.reshape(Bp, n_mem, MEM_HEADS, MEM_HD))

        x1 = _ffn(hs, *ffn1, gfin, False)
        proj = _norm_matmul(x1, row(norm_mix[l]), w_in_rt, PROJ_TN, w_transposed=True)
        proj3 = proj.reshape(Bs, 1, d_in_pad)
        og, s_new = _gdn_step(proj3, off_ba, state_conv[l], state_gdn[l], *gdn_common)
        q_raw = proj[:, off_q : off_q + SWA_Q_W].reshape(Bs, SWA_KV_HEADS, group, 1, SWA_HD)
        q_exp = (q_raw * eye_kv[None, :, None, :, None]).reshape(Bs, SWA_HEADS, SWA_KV_W)
        ck = jnp.swapaxes(cache_swa_k[l].reshape(Bs, WINDOW, SWA_KV_W), 1, 2)
        cv = jnp.swapaxes(cache_swa_v[l].reshape(Bs, WINDOW, SWA_KV_W), 1, 2)
        r, nk, nv = _swa_step(q_exp, proj3, off_kv, ck, cv, cos_s, sin_s, swa_sinks[l].reshape(SWA_HEADS, 1))
        r5 = r.reshape(Bs, SWA_KV_HEADS, group, SWA_KV_HEADS, SWA_HD)
        kvh = jnp.arange(SWA_KV_HEADS)
        os_ = jnp.transpose(r5[:, kvh, :, kvh, :], (1, 0, 2, 3)).reshape(Bs, SWA_Q_W).astype(bf16)
        x2, qm = _merge(og.reshape(Bs, GDN_W), os_, proj, OFF_GG, x1, *merge_w)
        q8 = jnp.tile(qm.reshape(Bs, MEM_HEADS, MEM_HD), (1, 2, 1))
        om = _mem_attn_step(q8, cache_mem_k[l].reshape(Bs, n_mem * MEM_HEADS, MEM_HD),
                            cache_mem_v[l].reshape(Bs, n_mem * MEM_HEADS, MEM_HD))
        x3 = _proj_residual(om[:, :MEM_HEADS].reshape(Bs, MEM_W), wmo, x2)
        hs = _ffn(x3, *ffn2, gfin, last)
        outs[6].append(s_new)
        outs[7].append(jnp.concatenate([state_conv[l][:, 1:], proj3[:, :, :CONV_CH]], axis=1))
        outs[8].append(jnp.swapaxes(nk, 1, 2).reshape(Bs, WINDOW, SWA_KV_HEADS, SWA_HD))
        outs[9].append(jnp.swapaxes(nv, 1, 2).reshape(Bs, WINDOW, SWA_KV_HEADS, SWA_HD))

    return (hp.reshape(Bp, Tp, D), hs.reshape(Bs, Ts, D), *(jnp.stack(o) for o in outs))
```

```python
import functools
import math

import jax
import jax.numpy as jnp
from jax import lax
from jax.experimental import pallas as pl
from jax.experimental.pallas import tpu as pltpu

f32 = jnp.float32
bf16 = jnp.bfloat16

PAST_LEN = 16384
GDN_HEADS = 8
GDN_D = 128
GDN_CONV = 4
SWA_HEADS = 16
SWA_KV_HEADS = 4
SWA_HD = 64
WINDOW = 128
ROPE_THETA = 10000.0
MEM_HEADS = 4
MEM_HD = 128
EPS = 1e-6
L2_EPS = 1e-6

LANES = 128
SUBLANES = 8
VMEM_LIMIT = 60 * 1024 * 1024

FFN_TM = 1024
FFN_TF = 512
FFN_SLAB = 64
PROJ_TM = 1024
PROJ_TN = 768
MERGE_TM = 256
CHUNK = 128
GDN_STEP_CHUNKS = 2
SWA_STEP_BLOCKS = 4
SWA_STEP_BATCH = 8
MEM_STEP_BATCH = 4

GDN_W = GDN_HEADS * GDN_D
CONV_CH = 3 * GDN_W
SWA_Q_W = SWA_HEADS * SWA_HD
SWA_KV_W = SWA_KV_HEADS * SWA_HD
MEM_W = MEM_HEADS * MEM_HD

OFF_Z = CONV_CH
OFF_GG = OFF_Z + GDN_W


def _cparams(semantics):
    return pltpu.CompilerParams(dimension_semantics=semantics, vmem_limit_bytes=VMEM_LIMIT)


def _dot(a, b):
    return jnp.dot(a, b, preferred_element_type=f32)


def _dot_nt(a, b):
    return lax.dot_general(a, b, (((1,), (1,)), ((), ())), preferred_element_type=f32)


def _rmsnorm(x, g):
    return x * lax.rsqrt(jnp.mean(x * x, -1, keepdims=True) + EPS) * g


def _silu(x):
    return x * jax.nn.sigmoid(x)


def _softplus(x):
    return jnp.maximum(x, 0.0) + jnp.log1p(jnp.exp(-jnp.abs(x)))


def _ffn_kernel(x_ref, g_ref, w1_ref, w3_ref, w2_ref, gf_ref, o_ref, h_ref, *, final_norm):
    j = pl.program_id(1)
    slab = min(FFN_SLAB, x_ref.shape[0])
    n_slabs = x_ref.shape[0] // slab

    def over_slabs(body):
        def step(s, carry):
            body(pl.ds(pl.multiple_of(s * slab, slab), slab))
            return carry
        lax.fori_loop(0, n_slabs, step, 0, unroll=min(4, n_slabs))

    @pl.when(j == 0)
    def _():
        def prologue(rows):
            h_ref[rows, :] = _rmsnorm(x_ref[rows, :], g_ref[...]).astype(bf16)
            o_ref[rows, :] = jnp.zeros((slab, o_ref.shape[1]), f32)
        over_slabs(prologue)

    h = h_ref[...]
    half = w1_ref.shape[1] // 2
    down = []
    for cols in (slice(0, half), slice(half, 2 * half)):
        a = _dot(h, w1_ref[:, cols])
        b = _dot(h, w3_ref[:, cols])
        down.append(_dot((_silu(a) * b).astype(bf16), w2_ref[cols, :]))
    o_ref[...] += down[0] + down[1]

    @pl.when(j == pl.num_programs(1) - 1)
    def _():
        def epilogue(rows):
            y = x_ref[rows, :] + 0.5 * o_ref[rows, :]
            if final_norm:
                y = _rmsnorm(y, gf_ref[...])
            o_ref[rows, :] = y
        over_slabs(epilogue)


def _ffn(x, g, w1, w3, w2, gf, final_norm):
    M, D = x.shape
    F = w1.shape[1]
    tm = min(FFN_TM, M)
    tf = FFN_TF
    assert M % tm == 0 and F % tf == 0
    return pl.pallas_call(
        functools.partial(_ffn_kernel, final_norm=final_norm),
        grid=(M // tm, F // tf),
        in_specs=[
            pl.BlockSpec((tm, D), lambda i, j: (i, 0)),
            pl.BlockSpec((1, D), lambda i, j: (0, 0)),
            pl.BlockSpec((D, tf), lambda i, j: (0, j)),
            pl.BlockSpec((D, tf), lambda i, j: (0, j)),
            pl.BlockSpec((tf, D), lambda i, j: (j, 0)),
            pl.BlockSpec((1, D), lambda i, j: (0, 0)),
        ],
        out_specs=pl.BlockSpec((tm, D), lambda i, j: (i, 0)),
        out_shape=jax.ShapeDtypeStruct((M, D), f32),
        scratch_shapes=[pltpu.VMEM((tm, D), bf16)],
        compiler_params=_cparams(("parallel", "arbitrary")),
        name="ffn",
    )(x, g, w1, w3, w2, gf)


def _norm_matmul_kernel(x_ref, g_ref, w_ref, o_ref, h_ref, *, w_transposed):
    @pl.when(pl.program_id(1) == 0)
    def _():
        h_ref[...] = _rmsnorm(x_ref[...], g_ref[...]).astype(bf16)

    o_ref[...] = (_dot_nt if w_transposed else _dot)(h_ref[...], w_ref[...])


def _norm_matmul(x, g, w, tn, w_transposed=False):
    M, D = x.shape
    N = w.shape[0] if w_transposed else w.shape[1]
    tm = min(PROJ_TM, M)
    assert M % tm == 0 and N % tn == 0
    w_spec = pl.BlockSpec((tn, D), lambda i, j: (j, 0)) if w_transposed else pl.BlockSpec((D, tn), lambda i, j: (0, j))
    return pl.pallas_call(
        functools.partial(_norm_matmul_kernel, w_transposed=w_transposed),
        grid=(M // tm, N // tn),
        in_specs=[
            pl.BlockSpec((tm, D), lambda i, j: (i, 0)),
            pl.BlockSpec((1, D), lambda i, j: (0, 0)),
            w_spec,
        ],
        out_specs=pl.BlockSpec((tm, tn), lambda i, j: (i, j)),
        out_shape=jax.ShapeDtypeStruct((M, N), f32),
        scratch_shapes=[pltpu.VMEM((tm, D), bf16)],
        compiler_params=_cparams(("parallel", "arbitrary")),
        name="norm_matmul",
    )(x, g, w)


def _mem_kv_kernel(x_ref, g_ref, wk_ref, wv_ref, k_ref, v_ref, kb_ref, vb_ref):
    h = _rmsnorm(x_ref[...], g_ref[...]).astype(bf16)
    k = _dot(h, wk_ref[...])
    v = _dot(h, wv_ref[...])
    kb_ref[...] = k.astype(bf16)
    vb_ref[...] = v.astype(bf16)
    tokens = x_ref.shape[0]
    for hd in range(MEM_HEADS):
        k_ref[pl.ds(hd, tokens, stride=MEM_HEADS), :] = k[:, hd * MEM_HD : (hd + 1) * MEM_HD]
        v_ref[pl.ds(hd, tokens, stride=MEM_HEADS), :] = v[:, hd * MEM_HD : (hd + 1) * MEM_HD]


def _mem_kv(x, g, wk, wv):
    M, D = x.shape
    N = wk.shape[1]
    tm = min(PROJ_TM, M)
    assert M % tm == 0 and N == MEM_W
    row_spec = pl.BlockSpec((tm, N), lambda i: (i, 0))
    head_row_spec = pl.BlockSpec((tm * MEM_HEADS, MEM_HD), lambda i: (i, 0))
    const = lambda shape: pl.BlockSpec(shape, lambda i: (0, 0))
    return pl.pallas_call(
        _mem_kv_kernel,
        grid=(M // tm,),
        in_specs=[pl.BlockSpec((tm, D), lambda i: (i, 0)), const((1, D)), const((D, N)), const((D, N))],
        out_specs=[head_row_spec] * 2 + [row_spec] * 2,
        out_shape=[jax.ShapeDtypeStruct((M * MEM_HEADS, MEM_HD), f32)] * 2 + [jax.ShapeDtypeStruct((M, N), bf16)] * 2,
        compiler_params=_cparams(("parallel",)),
        name="mem_kv",
    )(x, g, wk, wv)


def _bdot(a, b):
    return _dot(a.astype(bf16), b.astype(bf16))


def _unit_lower_inverse(nmats, row, col):
    eye = jnp.where(row == col, 1.0, 0.0).astype(f32)

    def same_block(size):
        return (row // size) == (col // size)

    blk = same_block(SUBLANES)
    a1 = [jnp.where(blk, n, 0.0) for n in nmats]
    a2 = [_bdot(a, a) for a in a1]
    a4 = [_bdot(a, a) for a in a2]
    ts = [_bdot(eye - x1, eye + x2) for x1, x2 in zip(a1, a2)]
    ts = [_bdot(t, eye + x4) for t, x4 in zip(ts, a4)]
    size = SUBLANES
    while size < CHUNK:
        nxt = same_block(2 * size)
        sel = jnp.logical_and(nxt, jnp.logical_not(blk))
        tbs = [t.astype(bf16) for t in ts]
        xs = [_dot(jnp.where(sel, n, 0.0).astype(bf16), tb) for n, tb in zip(nmats, tbs)]
        ts = [t - _dot(tb, x.astype(bf16)) for t, tb, x in zip(ts, tbs, xs)]
        blk = nxt
        size *= 2
    return ts


def _conv_silu_slab(x_ref, top_ref, w_ref, sl):
    rows = x_ref.shape[0]
    y = None
    for tap in range(GDN_CONV):
        back = GDN_CONV - 1 - tap
        window = jnp.concatenate(
            [top_ref[SUBLANES - back : 2 * SUBLANES - back, sl], x_ref[SUBLANES - back : rows - back, sl]], axis=0)
        term = window * w_ref[tap : tap + 1, sl]
        y = term if y is None else y + term
    return _silu(y)


def _delta_rule_prepare(pairs, gcum, gcum_t, beta_t, row, col, q_ref, k_ref, v_ref):
    n = range(len(pairs))
    incl = row >= col
    strict = row > col
    rows_of = lambda ci: slice(ci * CHUNK, (ci + 1) * CHUNK)
    gc = [gcum[rows_of(ci), GDN_HEADS + h : GDN_HEADS + h + 1] for ci, h in pairs]
    gr = [gcum_t[ci][GDN_HEADS + h : GDN_HEADS + h + 1, :] for ci, h in pairs]
    beta = [beta_t[rows_of(ci), h : h + 1] for ci, h in pairs]
    k = [k_ref[h, rows_of(ci), :] for ci, h in pairs]
    kb = [x.astype(bf16) for x in k]
    kk = [_dot_nt(kb[i], kb[i]) for i in n]
    qk = [_dot_nt(q_ref[h, rows_of(ci), :].astype(bf16), kb[i]) for i, (ci, h) in enumerate(pairs)]
    e = [jnp.exp(jnp.where(incl, gc[i] - gr[i], 0.0)) for i in n]
    nmat = [beta[i] * kk[i] * jnp.where(strict, e[i], 0.0) for i in n]
    qkd = [(qk[i] * jnp.where(incl, e[i], 0.0)).astype(bf16) for i in n]
    tinv = _unit_lower_inverse(nmat, row, col)
    gamma = [jnp.exp(gc[i]) for i in n]
    rhs = [jnp.concatenate([beta[i] * v_ref[h, rows_of(ci), :], (beta[i] * gamma[i]) * k[i]], axis=-1).astype(bf16)
           for i, (ci, h) in enumerate(pairs)]
    sol = [_dot(tinv[i].astype(bf16), rhs[i]) for i in n]
    return gc, gamma, sol, qkd


def _delta_rule_apply(ci, prepared, q_ref, k_ref, z_ref, gn_ref, s_ref, og_ref):
    gc, gamma, sol, qkd = prepared
    heads = range(GDN_HEADS)
    rows = slice(ci * CHUNK, (ci + 1) * CHUNK)
    sb = [s_ref[h].astype(bf16) for h in heads]
    ub = [(sol[h][:, :GDN_D] - _dot(sol[h][:, GDN_D:].astype(bf16), sb[h])).astype(bf16) for h in heads]
    o = [_dot((q_ref[h, rows, :] * gamma[h]).astype(bf16), sb[h]) + _dot(qkd[h], ub[h]) for h in heads]
    for h in heads:
        g_last = gc[h][CHUNK - 1 : CHUNK, :]
        k_end = k_ref[h, rows, :] * jnp.exp(g_last - gc[h])
        s_ref[h] = jnp.exp(g_last) * s_ref[h] + _dot(k_end.T.astype(bf16), ub[h])
        zh = z_ref[rows, h * GDN_D : (h + 1) * GDN_D]
        og_ref[rows, h * GDN_D : (h + 1) * GDN_D] = (_rmsnorm(o[h], gn_ref[...]) * _silu(zh)).astype(bf16)


def _gdn_prompt_kernel(*refs, n_cast):
    qkv_ref, z_ref, ba_ref, cw_ref, av_ref, dv_ref, gn_ref = refs[:7]
    cast_in = refs[7 : 7 + n_cast]
    og_ref, so_ref = refs[7 + n_cast : 9 + n_cast]
    cast_out = refs[9 + n_cast : 9 + 2 * n_cast]
    s_ref, carry_ref, q_ref, k_ref, v_ref = refs[9 + 2 * n_cast :]
    for src, dst in zip(cast_in, cast_out):
        dst[...] = src[...].astype(bf16)

    c = pl.program_id(1)
    rows = qkv_ref.shape[0]
    n_chunks = rows // CHUNK

    @pl.when(c == 0)
    def _():
        s_ref[...] = jnp.zeros_like(s_ref)
        carry_ref[0:SUBLANES, :] = jnp.zeros((SUBLANES, CONV_CH), f32)

    carry_ref[SUBLANES : 2 * SUBLANES, :] = qkv_ref[0:SUBLANES, :]
    for j in range(3 * GDN_HEADS):
        sl = slice(j * LANES, (j + 1) * LANES)
        y = _conv_silu_slab(qkv_ref, carry_ref, cw_ref, sl)
        h = j % GDN_HEADS
        if j < 2 * GDN_HEADS:
            y = y * lax.rsqrt(jnp.sum(y * y, -1, keepdims=True) + L2_EPS)
            if j < GDN_HEADS:
                q_ref[h] = y * (GDN_D ** -0.5)
            else:
                k_ref[h] = y
        else:
            v_ref[h] = y
    carry_ref[0:SUBLANES, :] = qkv_ref[rows - SUBLANES : rows, :]

    ba = ba_ref[...]
    beta_t = jax.nn.sigmoid(ba)
    g_t = -jnp.exp(av_ref[...]) * _softplus(ba + dv_ref[...])
    row_in_chunk = lax.broadcasted_iota(jnp.int32, (rows, LANES), 0) % CHUNK
    gcum = g_t
    shift = 1
    while shift < CHUNK:
        gcum = gcum + jnp.where(row_in_chunk >= shift, pltpu.roll(gcum, shift, 0), 0.0)
        shift *= 2
    gcum_t = [gcum[ci * CHUNK : (ci + 1) * CHUNK, :].T for ci in range(n_chunks)]

    row = lax.broadcasted_iota(jnp.int32, (CHUNK, CHUNK), 0)
    col = lax.broadcasted_iota(jnp.int32, (CHUNK, CHUNK), 1)
    pairs = [(ci, h) for ci in range(n_chunks) for h in range(GDN_HEADS)]
    gc, gamma, sol, qkd = _delta_rule_prepare(pairs, gcum, gcum_t, beta_t, row, col, q_ref, k_ref, v_ref)
    for ci in range(n_chunks):
        mine = slice(ci * GDN_HEADS, (ci + 1) * GDN_HEADS)
        _delta_rule_apply(ci, (gc[mine], gamma[mine], sol[mine], qkd[mine]), q_ref, k_ref, z_ref, gn_ref, s_ref, og_ref)

    @pl.when(c == pl.num_programs(1) - 1)
    def _():
        so_ref[0] = s_ref[...]


def _cast_block(shape, steps):
    R, C = shape
    for col_blocks in (1, 2, 4, 8, 16):
        row_blocks = steps // col_blocks
        if (steps % col_blocks == 0 and R % row_blocks == 0 and C % col_blocks == 0
                and (R // row_blocks) % (2 * SUBLANES) == 0 and (C // col_blocks) % LANES == 0):
            return (R // row_blocks, C // col_blocks), col_blocks
    return None


def _gdn_prompt(proj, B, T, off_ba, cw, avec, dvec, gn, weights):
    rows = GDN_STEP_CHUNKS * CHUNK
    assert T % rows == 0
    ns = T // rows
    step = lambda b, c: b * ns + c
    plans = [_cast_block(w.shape, B * ns) for w in weights]
    riders = [w for w, p in zip(weights, plans) if p is not None]
    cast_specs = [pl.BlockSpec(blk, functools.partial(lambda b, c, cb: (step(b, c) // cb, step(b, c) % cb), cb=cb))
                  for blk, cb in (p for p in plans if p is not None)]
    head_scratch = pltpu.VMEM((GDN_HEADS, rows, GDN_D), f32)
    out = pl.pallas_call(
        functools.partial(_gdn_prompt_kernel, n_cast=len(riders)),
        grid=(B, ns),
        in_specs=[
            pl.BlockSpec((rows, CONV_CH), lambda b, c: (step(b, c), 0)),
            pl.BlockSpec((rows, GDN_W), lambda b, c: (step(b, c), OFF_Z // GDN_W)),
            pl.BlockSpec((rows, LANES), lambda b, c: (step(b, c), off_ba // LANES)),
            pl.BlockSpec((GDN_CONV, CONV_CH), lambda b, c: (0, 0)),
            pl.BlockSpec((1, LANES), lambda b, c: (0, 0)),
            pl.BlockSpec((1, LANES), lambda b, c: (0, 0)),
            pl.BlockSpec((1, GDN_D), lambda b, c: (0, 0)),
        ] + cast_specs,
        out_specs=[
            pl.BlockSpec((rows, GDN_W), lambda b, c: (step(b, c), 0)),
            pl.BlockSpec((1, GDN_HEADS, GDN_D, GDN_D), lambda b, c: (b, 0, 0, 0)),
        ] + cast_specs,
        out_shape=[
            jax.ShapeDtypeStruct((B * T, GDN_W), bf16),
            jax.ShapeDtypeStruct((B, GDN_HEADS, GDN_D, GDN_D), f32),
        ] + [jax.ShapeDtypeStruct(w.shape, bf16) for w in riders],
        scratch_shapes=[
            pltpu.VMEM((GDN_HEADS, GDN_D, GDN_D), f32),
            pltpu.VMEM((2 * SUBLANES, CONV_CH), f32),
            head_scratch, head_scratch, head_scratch,
        ],
        compiler_params=_cparams(("arbitrary", "arbitrary")),
        name="gdn_prompt",
    )(proj, proj, proj, cw, avec, dvec, gn, *riders)
    converted = iter(out[2:])
    return out[0], out[1], [next(converted) if p is not None else w.astype(bf16) for w, p in zip(weights, plans)]


def _gdn_step_kernel(qkv_ref, z_ref, ba_ref, sc_ref, s0_ref, cw_ref, av_ref, dv_ref, gn_ref, og_ref, so_ref):
    x_new = qkv_ref[0]
    taps = cw_ref[...]
    y = jnp.sum(sc_ref[0] * taps[0 : GDN_CONV - 1, :], axis=0, keepdims=True) + x_new * taps[GDN_CONV - 1 : GDN_CONV, :]
    y = _silu(y)
    ba = ba_ref[0]
    beta_t = jax.nn.sigmoid(ba)
    gamma_t = jnp.exp(-jnp.exp(av_ref[...]) * _softplus(ba + dv_ref[...]))
    z = z_ref[0]
    heads = range(GDN_HEADS)
    head = lambda base, h: y[:, base + h * GDN_D : base + (h + 1) * GDN_D]
    q = [head(0, h) for h in heads]
    k = [head(GDN_W, h) for h in heads]
    v = [head(2 * GDN_W, h) for h in heads]
    q = [q[h] * lax.rsqrt(jnp.sum(q[h] * q[h], -1, keepdims=True) + L2_EPS) * (GDN_D ** -0.5) for h in heads]
    k = [k[h] * lax.rsqrt(jnp.sum(k[h] * k[h], -1, keepdims=True) + L2_EPS) for h in heads]
    beta = [beta_t[:, h : h + 1] for h in heads]
    gamma = [gamma_t[:, GDN_HEADS + h : GDN_HEADS + h + 1] for h in heads]
    k_col = [jnp.broadcast_to(k[h], (GDN_D, GDN_D)).T for h in heads]
    sub = lax.broadcasted_iota(jnp.int32, (SUBLANES, GDN_D), 0)
    kq = [jnp.where(sub == 0, k[h], jnp.where(sub == 1, q[h], 0.0)).astype(bf16) for h in heads]
    kq_s = [_dot(kq[h], s0_ref[0, h].astype(bf16)) for h in heads]
    k_s = [kq_s[h][0:1] for h in heads]
    q_s = [kq_s[h][1:2] for h in heads]
    u = [beta[h] * v[h] - (beta[h] * gamma[h]) * k_s[h] for h in heads]
    o = [gamma[h] * q_s[h] + jnp.sum(q[h] * k[h], -1, keepdims=True) * u[h] for h in heads]
    for h in heads:
        so_ref[0, h] = gamma[h] * s0_ref[0, h] + k_col[h] * u[h]
        zh = z[:, h * GDN_D : (h + 1) * GDN_D]
        og_ref[0, :, h * GDN_D : (h + 1) * GDN_D] = (_rmsnorm(o[h], gn_ref[...]) * _silu(zh)).astype(bf16)


def _gdn_step(proj3, off_ba, state_conv, state_gdn, cw, avec, dvec, gn):
    B = proj3.shape[0]
    return pl.pallas_call(
        _gdn_step_kernel,
        grid=(B,),
        in_specs=[
            pl.BlockSpec((1, 1, CONV_CH), lambda b: (b, 0, 0)),
            pl.BlockSpec((1, 1, GDN_W), lambda b: (b, 0, OFF_Z // GDN_W)),
            pl.BlockSpec((1, 1, LANES), lambda b: (b, 0, off_ba // LANES)),
            pl.BlockSpec((1, GDN_CONV - 1, CONV_CH), lambda b: (b, 0, 0)),
            pl.BlockSpec((1, GDN_HEADS, GDN_D, GDN_D), lambda b: (b, 0, 0, 0)),
            pl.BlockSpec((GDN_CONV, CONV_CH), lambda b: (0, 0)),
            pl.BlockSpec((1, LANES), lambda b: (0, 0)),
            pl.BlockSpec((1, LANES), lambda b: (0, 0)),
            pl.BlockSpec((1, GDN_D), lambda b: (0, 0)),
        ],
        out_specs=[
            pl.BlockSpec((1, 1, GDN_W), lambda b: (b, 0, 0)),
            pl.BlockSpec((1, GDN_HEADS, GDN_D, GDN_D), lambda b: (b, 0, 0, 0)),
        ],
        out_shape=[
            jax.ShapeDtypeStruct((B, 1, GDN_W), bf16),
            jax.ShapeDtypeStruct((B, GDN_HEADS, GDN_D, GDN_D), f32),
        ],
        compiler_params=_cparams(("parallel",)),
        name="gdn_step",
    )(proj3, proj3, proj3, state_conv, state_gdn, cw, avec, dvec, gn)


def _rope(x, cos, sin_signed):
    width = x.shape[-1]
    lane = lax.broadcasted_iota(jnp.int32, x.shape, x.ndim - 1)
    first_half = (lane % SWA_HD) < (SWA_HD // 2)
    rot = jnp.where(first_half, pltpu.roll(x, width - SWA_HD // 2, x.ndim - 1), pltpu.roll(x, SWA_HD // 2, x.ndim - 1))
    return x * cos + rot * sin_signed


def _head_halves(x2, head_parity, lane):
    swapped = pltpu.roll(x2, SWA_HD, 1)
    lo_src, hi_src = (x2, swapped) if head_parity == 0 else (swapped, x2)
    return jnp.where(lane < SWA_HD, lo_src, 0.0), jnp.where(lane >= SWA_HD, hi_src, 0.0)


def _swa_prompt_kernel(sinks_ref, q_ref, kv_ref, cos_ref, sin_ref, os_ref, kc_ref, kprev_ref, vprev_ref):
    n = pl.program_id(1)
    n_blk = q_ref.shape[0] // WINDOW

    @pl.when(n == 0)
    def _():
        kprev_ref[...] = jnp.zeros_like(kprev_ref)
        vprev_ref[...] = jnp.zeros_like(vprev_ref)

    row = lax.broadcasted_iota(jnp.int32, (WINDOW, WINDOW), 0)
    col = lax.broadcasted_iota(jnp.int32, (WINDOW, WINDOW), 1)
    own = col <= row
    first_bias = jnp.where(n > 0, 0.0, -jnp.inf)
    lane = lax.broadcasted_iota(jnp.int32, (WINDOW, LANES), 1)
    scale = SWA_HD ** -0.5
    group = SWA_HEADS // SWA_KV_HEADS

    k_own, v_own, k_pre, v_pre, q2 = [], [], [], [], []
    for blk in range(n_blk):
        rows = slice(blk * WINDOW, (blk + 1) * WINDOW)
        cos = cos_ref[rows, :]
        sin = sin_ref[rows, :]
        k_cur = _rope(kv_ref[rows, :SWA_KV_W], cos, sin)
        v_cur = kv_ref[rows, SWA_KV_W:]
        if blk == n_blk - 1:
            kc_ref[0] = k_cur
        k_own.append([]), v_own.append([]), k_pre.append([]), v_pre.append([]), q2.append([])
        for h in range(SWA_KV_HEADS):
            pair = slice((h // 2) * LANES, (h // 2 + 1) * LANES)
            k_own[blk].append([a.astype(bf16) for a in _head_halves(k_cur[:, pair], h % 2, lane)])
            v_own[blk].append([a.astype(bf16) for a in _head_halves(v_cur[:, pair], h % 2, lane)])
            if blk == 0:
                k_pre[blk].append([kprev_ref[h, par] for par in range(2)])
                v_pre[blk].append([vprev_ref[h, par] for par in range(2)])
            else:
                k_pre[blk].append(k_own[blk - 1][h])
                v_pre[blk].append(v_own[blk - 1][h])
            if blk == n_blk - 1:
                for par in range(2):
                    kprev_ref[h, par] = k_own[blk][h][par]
                    vprev_ref[h, par] = v_own[blk][h][par]
            q_h = _rope(q_ref[rows, h * group * SWA_HD : (h + 1) * group * SWA_HD], cos, sin) * scale
            q2[blk].append([q_h[:, j * LANES : (j + 1) * LANES].astype(bf16) for j in range(group // 2)])

    heads = [(blk, h, j, par) for blk in range(n_blk) for h in range(SWA_KV_HEADS)
             for j in range(group // 2) for par in range(2)]
    idx = range(len(heads))
    sink = [sinks_ref[h * group + 2 * j + par] for blk, h, j, par in heads]
    s = [jnp.where(own, _dot_nt(q2[blk][h][j], k_own[blk][h][par]),
                   _dot_nt(q2[blk][h][j], k_pre[blk][h][par]) + (first_bias if blk == 0 else 0.0))
         for blk, h, j, par in heads]
    m = [jnp.maximum(jnp.max(s[i], -1, keepdims=True), sink[i]) for i in idx]
    p = [jnp.exp(s[i] - m[i]) for i in idx]
    inv = [1.0 / (jnp.sum(p[i], -1, keepdims=True) + jnp.exp(sink[i] - m[i])) for i in idx]
    o = [(_dot(jnp.where(own, p[i], 0.0).astype(bf16), v_own[blk][h][par])
          + _dot(jnp.where(own, 0.0, p[i]).astype(bf16), v_pre[blk][h][par])) * inv[i]
         for i, (blk, h, j, par) in enumerate(heads)]
    for i in range(0, len(heads), 2):
        blk, head = divmod(i, SWA_HEADS)
        os_ref[blk * WINDOW : (blk + 1) * WINDOW, head * SWA_HD : head * SWA_HD + LANES] = (o[i] + o[i + 1]).astype(bf16)


def _swa_prompt(proj, B, T, off_q, off_kv, cos, sin, sinks):
    assert T % WINDOW == 0 and SWA_KV_W == 2 * LANES
    rows = math.gcd(SWA_STEP_BLOCKS, T // WINDOW) * WINDOW
    nb = T // rows
    return pl.pallas_call(
        _swa_prompt_kernel,
        grid=(B, nb),
        in_specs=[
            pl.BlockSpec(memory_space=pltpu.SMEM),
            pl.BlockSpec((rows, SWA_Q_W), lambda b, n: (b * nb + n, off_q // SWA_Q_W)),
            pl.BlockSpec((rows, 2 * SWA_KV_W), lambda b, n: (b * nb + n, off_kv // (2 * SWA_KV_W))),
            pl.BlockSpec((rows, SWA_KV_W), lambda b, n: (n, 0)),
            pl.BlockSpec((rows, SWA_KV_W), lambda b, n: (n, 0)),
        ],
        out_specs=[
            pl.BlockSpec((rows, SWA_Q_W), lambda b, n: (b * nb + n, 0)),
            pl.BlockSpec((1, WINDOW, SWA_KV_W), lambda b, n: (b, 0, 0)),
        ],
        out_shape=[
            jax.ShapeDtypeStruct((B * T, SWA_Q_W), bf16),
            jax.ShapeDtypeStruct((B, WINDOW, SWA_KV_W), f32),
        ],
        scratch_shapes=[pltpu.VMEM((SWA_KV_HEADS, 2, WINDOW, LANES), bf16)] * 2,
        compiler_params=_cparams(("parallel", "arbitrary")),
        name="swa_prompt",
    )(sinks, proj, proj, cos, sin)


def _swa_step_kernel(qe_ref, kv_ref, ck_ref, cv_ref, cos_ref, sin_ref, sinks_ref, r_ref, nk_ref, nv_ref):
    cos = cos_ref[...]
    sin = sin_ref[...]
    sink = sinks_ref[...]
    row = lax.broadcasted_iota(jnp.int32, (WINDOW, SWA_KV_W), 0)
    scale = SWA_HD ** -0.5
    seqs = range(qe_ref.shape[0])
    kv = [kv_ref[i] for i in seqs]
    k_new = [_rope(kv[i][:, :SWA_KV_W], cos, sin) for i in seqs]
    keys = [jnp.where(row == WINDOW - 1, k_new[i], pltpu.roll(ck_ref[i].T, WINDOW - 1, 0)) for i in seqs]
    vals = [jnp.where(row == WINDOW - 1, kv[i][:, SWA_KV_W:], pltpu.roll(cv_ref[i].T, WINDOW - 1, 0)) for i in seqs]
    for i in seqs:
        nk_ref[i] = keys[i].T
        nv_ref[i] = vals[i].T
    q = [_rope(qe_ref[i], cos, sin) for i in seqs]
    s = [_dot_nt(q[i].astype(bf16), keys[i].astype(bf16)) * scale for i in seqs]
    m = [jnp.maximum(jnp.max(s[i], -1, keepdims=True), sink) for i in seqs]
    p = [jnp.exp(s[i] - m[i]) for i in seqs]
    denom = [jnp.sum(p[i], -1, keepdims=True) + jnp.exp(sink - m[i]) for i in seqs]
    for i in seqs:
        r_ref[i] = _dot((p[i] / denom[i]).astype(bf16), vals[i].astype(bf16))


def _swa_step(q_exp, proj3, off_kv, cache_k, cache_v, cos, sin, sinks_col):
    B = q_exp.shape[0]
    bb = math.gcd(B, SWA_STEP_BATCH)
    assert cache_k.shape[2] == WINDOW
    return pl.pallas_call(
        _swa_step_kernel,
        grid=(B // bb,),
        in_specs=[
            pl.BlockSpec((bb, SWA_HEADS, SWA_KV_W), lambda i: (i, 0, 0)),
            pl.BlockSpec((bb, 1, 2 * SWA_KV_W), lambda i: (i, 0, off_kv // (2 * SWA_KV_W))),
            pl.BlockSpec((bb, SWA_KV_W, WINDOW), lambda i: (i, 0, 0)),
            pl.BlockSpec((bb, SWA_KV_W, WINDOW), lambda i: (i, 0, 0)),
            pl.BlockSpec((1, SWA_KV_W), lambda i: (0, 0)),
            pl.BlockSpec((1, SWA_KV_W), lambda i: (0, 0)),
            pl.BlockSpec((SWA_HEADS, 1), lambda i: (0, 0)),
        ],
        out_specs=[
            pl.BlockSpec((bb, SWA_HEADS, SWA_KV_W), lambda i: (i, 0, 0)),
            pl.BlockSpec((bb, SWA_KV_W, WINDOW), lambda i: (i, 0, 0)),
            pl.BlockSpec((bb, SWA_KV_W, WINDOW), lambda i: (i, 0, 0)),
        ],
        out_shape=[
            jax.ShapeDtypeStruct((B, SWA_HEADS, SWA_KV_W), f32),
            jax.ShapeDtypeStruct((B, SWA_KV_W, WINDOW), f32),
            jax.ShapeDtypeStruct((B, SWA_KV_W, WINDOW), f32),
        ],
        compiler_params=_cparams(("parallel",)),
        name="swa_step",
    )(q_exp, proj3, cache_k, cache_v, cos, sin, sinks_col)


def _merge_core(og_ref, os_ref, gg_ref, gs_ref, x_ref, wg_ref, ws_ref, wo_ref, gq_ref, wq_ref):
    p_gdn = _dot(og_ref[...], wg_ref[...])
    p_swa = _dot(os_ref[...], ws_ref[...])
    merged = jax.nn.sigmoid(gg_ref[...]) * p_gdn + jax.nn.sigmoid(gs_ref[...]) * p_swa
    x_new = x_ref[...] + _dot(merged.astype(bf16), wo_ref[...])
    return x_new, _dot(_rmsnorm(x_new, gq_ref[...]).astype(bf16), wq_ref[...])


def _merge_kernel(*refs):
    xo_ref, qm_ref = refs[-2:]
    xo_ref[...], qm_ref[...] = _merge_core(*refs[:-2])


def _merge_mem_kernel(*refs):
    mk_ref, mv_ref, wmo_ref, xo_ref = refs[-4:]
    x_new, q = _merge_core(*refs[:-4])
    q = q.astype(bf16)
    scale = MEM_HD ** -0.5
    heads = range(MEM_HEADS)
    cols = lambda h: slice(h * MEM_HD, (h + 1) * MEM_HD)
    s = [_dot_nt(q[:, cols(h)], mk_ref[0, :, cols(h)]) * scale for h in heads]
    p = [jnp.exp(s[h] - jnp.max(s[h], -1, keepdims=True)) for h in heads]
    p = [p[h] / jnp.sum(p[h], -1, keepdims=True) for h in heads]
    o = [_dot(p[h].astype(bf16), mv_ref[0, :, cols(h)]) for h in heads]
    xo_ref[...] = x_new + _dot(jnp.concatenate(o, axis=-1).astype(bf16), wmo_ref[...])


def _merge(og, os_, proj, off_gg, x, wg, ws, wo, gq, wq, mem=None):
    M, D = x.shape
    tm = min(MERGE_TM, M)
    assert M % tm == 0 and off_gg % D == 0
    const = lambda shape: pl.BlockSpec(shape, lambda i: (0, 0), pipeline_mode=pl.Buffered(1))
    in_specs = [
        pl.BlockSpec((tm, GDN_W), lambda i: (i, 0)),
        pl.BlockSpec((tm, SWA_Q_W), lambda i: (i, 0)),
        pl.BlockSpec((tm, D), lambda i: (i, off_gg // D)),
        pl.BlockSpec((tm, D), lambda i: (i, off_gg // D + 1)),
        pl.BlockSpec((tm, D), lambda i: (i, 0)),
        const((GDN_W, D)),
        const((SWA_Q_W, D)),
        const((D, D)),
        const((1, D)),
        const((D, MEM_W)),
    ]
    args = [og, os_, proj, proj, x, wg, ws, wo, gq, wq]
    x_spec = pl.BlockSpec((tm, D), lambda i: (i, 0))
    x_shape = jax.ShapeDtypeStruct((M, D), f32)
    if mem is None:
        return pl.pallas_call(
            _merge_kernel,
            grid=(M // tm,),
            in_specs=in_specs,
            out_specs=[x_spec, pl.BlockSpec((tm, MEM_W), lambda i: (i, 0))],
            out_shape=[x_shape, jax.ShapeDtypeStruct((M, MEM_W), f32)],
            compiler_params=_cparams(("parallel",)),
            name="merge",
        )(*args)
    mem_k, mem_v, wmo, seq_rows = mem
    assert seq_rows % tm == 0
    mt = mem_k.shape[1]
    mem_spec = pl.BlockSpec((1, mt, MEM_W), lambda i: (i // (seq_rows // tm), 0, 0))
    return pl.pallas_call(
        _merge_mem_kernel,
        grid=(M // tm,),
        in_specs=in_specs + [mem_spec, mem_spec, const((MEM_W, D))],
        out_specs=x_spec,
        out_shape=x_shape,
        compiler_params=_cparams(("parallel",)),
        name="merge_mem",
    )(*args, mem_k, mem_v, wmo)


def _mem_attn_step_kernel(q_ref, k_ref, v_ref, o_ref):
    mt2 = k_ref.shape[1] // SUBLANES
    for i in range(q_ref.shape[0]):
        q8 = q_ref[i] * (MEM_HD ** -0.5)
        s = jnp.sum(k_ref[i].reshape(mt2, SUBLANES, MEM_HD) * q8, axis=-1, keepdims=True)
        m = jnp.max(s, axis=0)
        m = jnp.maximum(m, pltpu.roll(m, MEM_HEADS, 0))
        p = jnp.exp(s - m)
        l = jnp.sum(p, axis=0)
        l = l + pltpu.roll(l, MEM_HEADS, 0)
        o = jnp.sum(p * v_ref[i].reshape(mt2, SUBLANES, MEM_HD), axis=0)
        o_ref[i] = (o + pltpu.roll(o, MEM_HEADS, 0)) / l


def _mem_attn_step(q8, mem_k, mem_v):
    B, rows, _ = mem_k.shape
    assert 2 * MEM_HEADS == SUBLANES and rows % SUBLANES == 0
    bb = math.gcd(B, MEM_STEP_BATCH)
    return pl.pallas_call(
        _mem_attn_step_kernel,
        grid=(B // bb,),
        in_specs=[
            pl.BlockSpec((bb, SUBLANES, MEM_HD), lambda i: (i, 0, 0)),
            pl.BlockSpec((bb, rows, MEM_HD), lambda i: (i, 0, 0)),
            pl.BlockSpec((bb, rows, MEM_HD), lambda i: (i, 0, 0)),
        ],
        out_specs=pl.BlockSpec((bb, SUBLANES, MEM_HD), lambda i: (i, 0, 0)),
        out_shape=jax.ShapeDtypeStruct((B, SUBLANES, MEM_HD), f32),
        compiler_params=_cparams(("parallel",)),
        name="mem_attn_step",
    )(q8, mem_k, mem_v)


def _proj_residual_kernel(a_ref, w_ref, x_ref, o_ref):
    o_ref[...] = x_ref[...] + _dot(a_ref[...].astype(bf16), w_ref[...])


def _proj_residual(a, w, x):
    M, D = x.shape
    return pl.pallas_call(
        _proj_residual_kernel,
        out_shape=jax.ShapeDtypeStruct((M, D), f32),
        compiler_params=pltpu.CompilerParams(vmem_limit_bytes=VMEM_LIMIT),
        name="proj_residual",
    )(a, w, x)


def _reorder_rows_kernel(w_ref, o_ref, *, pieces):
    at = 0
    for lo, hi in pieces:
        o_ref[at : at + hi - lo, :] = w_ref[lo:hi, :].astype(bf16)
        at += hi - lo
    o_ref[at:, :] = jnp.zeros((o_ref.shape[0] - at, o_ref.shape[1]), bf16)


def _reorder_rows(wt, layer, pieces, height):
    _, d_in, D = wt.shape
    assert all(lo % (2 * SUBLANES) == 0 and hi % (2 * SUBLANES) == 0 for lo, hi in pieces) and D % LANES == 0
    return pl.pallas_call(
        functools.partial(_reorder_rows_kernel, pieces=pieces),
        grid=(D // LANES,),
        in_specs=[pl.BlockSpec((None, d_in, LANES), lambda i: (layer, 0, i))],
        out_specs=pl.BlockSpec((height, LANES), lambda i: (0, i)),
        out_shape=jax.ShapeDtypeStruct((height, D), bf16),
        compiler_params=_cparams(("parallel",)),
        name="reorder_rows",
    )(wt)


def _rope_tables(pos):
    half = SWA_HD // 2
    inv_freq = ROPE_THETA ** (-jnp.arange(half, dtype=f32) / half)
    ang = pos.astype(f32)[:, None] * inv_freq[None, :]
    cos = jnp.cos(ang)
    sin = jnp.sin(ang)
    reps = SWA_KV_W // SWA_HD
    return jnp.tile(jnp.concatenate([cos, cos], -1), (1, reps)), jnp.tile(jnp.concatenate([-sin, sin], -1), (1, reps))


def kernel(x_prompt, x_sample, state_gdn, state_conv, cache_swa_k, cache_swa_v, cache_mem_k, cache_mem_v, mem_prompt, norm_ffn1, ffn1_w1, ffn1_w3, ffn1_w2, norm_mix, w_in, conv_w, gdn_A_log, gdn_dt_bias, gdn_norm, swa_sinks, w_br_gdn, w_br_swa, w_out, norm_mem_q, norm_mem_kv, w_mem_q, w_mem_k, w_mem_v, w_mem_o, norm_ffn2, ffn2_w1, ffn2_w3, ffn2_w2, norm_final):
    Bp, Tp, D = x_prompt.shape
    Bs, Ts, _ = x_sample.shape
    assert Ts == 1
    depth = norm_ffn1.shape[0]
    n_mem = mem_prompt.shape[1]
    group = SWA_HEADS // SWA_KV_HEADS

    off_gs = OFF_GG + D
    off_q = off_gs + D
    off_kv = off_q + SWA_Q_W
    off_ba = off_kv + 2 * SWA_KV_W
    d_in_pad = -(-(off_ba + LANES) // PROJ_TN) * PROJ_TN
    o_b = CONV_CH + GDN_W
    o_q = o_b + 2 * GDN_HEADS
    o_gg = o_q + SWA_Q_W + 2 * SWA_KV_W

    cos_p, sin_p = _rope_tables(jnp.arange(Tp, dtype=jnp.int32))
    cos_s, sin_s = _rope_tables(PAST_LEN + jnp.arange(Ts, dtype=jnp.int32))
    eye_kv = jnp.eye(SWA_KV_HEADS, dtype=f32)
    row = lambda v: v.reshape(1, -1)

    hp = x_prompt.reshape(Bp * Tp, D)
    hs = x_sample.reshape(Bs, D)
    outs = [[] for _ in range(10)]
    for l in range(depth):
        w_in_rt = _reorder_rows(jnp.swapaxes(w_in, 1, 2), l, ((0, o_b), (o_gg, w_in.shape[2]), (o_q, o_gg), (o_b, o_q)), d_in_pad)
        ffn1 = (row(norm_ffn1[l]), ffn1_w1[l].astype(bf16), ffn1_w3[l].astype(bf16), ffn1_w2[l].astype(bf16))
        last = l == depth - 1
        gfin = row(norm_final)
        avec = jnp.zeros((1, LANES), f32).at[0, GDN_HEADS : 2 * GDN_HEADS].set(gdn_A_log[l])
        dvec = jnp.zeros((1, LANES), f32).at[0, GDN_HEADS : 2 * GDN_HEADS].set(gdn_dt_bias[l])
        gdn_common = (conv_w[l], avec, dvec, row(gdn_norm[l]))

        x1 = _ffn(hp, *ffn1, gfin, False)
        proj = _norm_matmul(x1, row(norm_mix[l]), w_in_rt, PROJ_TN, w_transposed=True)
        later = (ffn2_w1[l], ffn2_w3[l], ffn2_w2[l], w_br_gdn[l], w_br_swa[l], w_out[l], w_mem_q[l],
                 w_mem_k[l], w_mem_v[l], w_mem_o[l])
        og, s_new, later = _gdn_prompt(proj, Bp, Tp, off_ba, *gdn_common, later)
        ffn2 = (row(norm_ffn2[l]), *later[0:3])
        merge_w = (*later[3:6], row(norm_mem_q[l]), later[6])
        wmk, wmv, wmo = later[7:10]
        os_, kc = _swa_prompt(proj, Bp, Tp, off_q, off_kv, cos_p, sin_p, swa_sinks[l])
        mem_x = mem_prompt.reshape(Bp * n_mem, D)
        mk, mv, mkb, mvb = _mem_kv(mem_x, row(norm_mem_kv[l]), wmk, wmv)
        mem = (mkb.reshape(Bp, n_mem, MEM_W), mvb.reshape(Bp, n_mem, MEM_W), wmo, Tp)
        x3 = _merge(og, os_, proj, OFF_GG, x1, *merge_w, mem=mem)
        hp = _ffn(x3, *ffn2, gfin, last)
        proj_b = proj.reshape(Bp, Tp, d_in_pad)
        outs[0].append(s_new)
        outs[1].append(proj_b[:, Tp - (GDN_CONV - 1) :, :CONV_CH])
        outs[2].append(kc.reshape(Bp, WINDOW, SWA_KV_HEADS, SWA_HD))
        outs[3].append(proj_b[:, Tp - WINDOW :, off_kv + SWA_KV_W : off_kv + 2 * SWA_KV_W].reshape(Bp, WINDOW, SWA_KV_HEADS, SWA_HD))
        outs[4].append(mk.reshape(Bp, n_mem, MEM_HEADS, MEM_HD))
        outs[5].append(mv.reshape(Bp, n_mem, MEM_HEADS, MEM_HD))

        x1 = _ffn(hs, *ffn1, gfin, False)
        proj = _norm_matmul(x1, row(norm_mix[l]), w_in_rt, PROJ_TN, w_transposed=True)
        proj3 = proj.reshape(Bs, 1, d_in_pad)
        og, s_new = _gdn_step(proj3, off_ba, state_conv[l], state_gdn[l], *gdn_common)
        q_raw = proj[:, off_q : off_q + SWA_Q_W].reshape(Bs, SWA_KV_HEADS, group, 1, SWA_HD)
        q_exp = (q_raw * eye_kv[None, :, None, :, None]).reshape(Bs, SWA_HEADS, SWA_KV_W)
        ck = jnp.swapaxes(cache_swa_k[l].reshape(Bs, WINDOW, SWA_KV_W), 1, 2)
        cv = jnp.swapaxes(cache_swa_v[l].reshape(Bs, WINDOW, SWA_KV_W), 1, 2)
        r, nk, nv = _swa_step(q_exp, proj3, off_kv, ck, cv, cos_s, sin_s, swa_sinks[l].reshape(SWA_HEADS, 1))
        r5 = r.reshape(Bs, SWA_KV_HEADS, group, SWA_KV_HEADS, SWA_HD)
        kvh = jnp.arange(SWA_KV_HEADS)
        os_ = jnp.transpose(r5[:, kvh, :, kvh, :], (1, 0, 2, 3)).reshape(Bs, SWA_Q_W).astype(bf16)
        x2, qm = _merge(og.reshape(Bs, GDN_W), os_, proj, OFF_GG, x1, *merge_w)
        q8 = jnp.tile(qm.reshape(Bs, MEM_HEADS, MEM_HD), (1, 2, 1))
        om = _mem_attn_step(q8, cache_mem_k[l].reshape(Bs, n_mem * MEM_HEADS, MEM_HD),
                            cache_mem_v[l].reshape(Bs, n_mem * MEM_HEADS, MEM_HD))
        x3 = _proj_residual(om[:, :MEM_HEADS].reshape(Bs, MEM_W), wmo, x2)
        hs = _ffn(x3, *ffn2, gfin, last)
        outs[6].append(s_new)
        outs[7].append(jnp.concatenate([state_conv[l][:, 1:], proj3[:, :, :CONV_CH]], axis=1))
        outs[8].append(jnp.swapaxes(nk, 1, 2).reshape(Bs, WINDOW, SWA_KV_HEADS, SWA_HD))
        outs[9].append(jnp.swapaxes(nv, 1, 2).reshape(Bs, WINDOW, SWA_KV_HEADS, SWA_HD))

    return (hp.reshape(Bp, Tp, D), hs.reshape(Bs, Ts, D), *(jnp.stack(o) for o in outs))
```

```python
import functools
import math

import jax
import jax.numpy as jnp
from jax import lax
from jax.experimental import pallas as pl
from jax.experimental.pallas import tpu as pltpu

f32 = jnp.float32
bf16 = jnp.bfloat16

PAST_LEN = 16384
GDN_HEADS = 8
GDN_D = 128
GDN_CONV = 4
SWA_HEADS = 16
SWA_KV_HEADS = 4
SWA_HD = 64
WINDOW = 128
ROPE_THETA = 10000.0
MEM_HEADS = 4
MEM_HD = 128
EPS = 1e-6
L2_EPS = 1e-6

LANES = 128
SUBLANES = 8
VMEM_LIMIT = 60 * 1024 * 1024

FFN_TM = 1024
FFN_TF = 512
FFN_SLAB = 64
PROJ_TM = 1024
PROJ_TN = 768
MERGE_TM = 256
CHUNK = 128
GDN_STEP_CHUNKS = 2
SWA_STEP_BLOCKS = 4
GDN_STEP_BATCH = 8
SWA_STEP_BATCH = 8
MEM_STEP_BATCH = 4

GDN_W = GDN_HEADS * GDN_D
CONV_CH = 3 * GDN_W
SWA_Q_W = SWA_HEADS * SWA_HD
SWA_KV_W = SWA_KV_HEADS * SWA_HD
MEM_W = MEM_HEADS * MEM_HD

OFF_Z = CONV_CH
OFF_GG = OFF_Z + GDN_W


def _cparams(semantics):
    return pltpu.CompilerParams(dimension_semantics=semantics, vmem_limit_bytes=VMEM_LIMIT)


def _dot(a, b):
    return jnp.dot(a, b, preferred_element_type=f32)


def _dot_nt(a, b):
    return lax.dot_general(a, b, (((1,), (1,)), ((), ())), preferred_element_type=f32)


def _rmsnorm(x, g):
    return x * lax.rsqrt(jnp.mean(x * x, -1, keepdims=True) + EPS) * g


def _silu(x):
    return x * jax.nn.sigmoid(x)


def _softplus(x):
    return jnp.maximum(x, 0.0) + jnp.log1p(jnp.exp(-jnp.abs(x)))


def _ffn_kernel(x_ref, g_ref, w1_ref, w3_ref, w2_ref, gf_ref, o_ref, h_ref, *, final_norm):
    j = pl.program_id(1)
    slab = min(FFN_SLAB, x_ref.shape[0])
    n_slabs = x_ref.shape[0] // slab

    def over_slabs(body):
        def step(s, carry):
            body(pl.ds(pl.multiple_of(s * slab, slab), slab))
            return carry
        lax.fori_loop(0, n_slabs, step, 0, unroll=min(4, n_slabs))

    @pl.when(j == 0)
    def _():
        def prologue(rows):
            h_ref[rows, :] = _rmsnorm(x_ref[rows, :], g_ref[...]).astype(bf16)
            o_ref[rows, :] = jnp.zeros((slab, o_ref.shape[1]), f32)
        over_slabs(prologue)

    h = h_ref[...]
    half = w1_ref.shape[1] // 2
    down = []
    for cols in (slice(0, half), slice(half, 2 * half)):
        a = _dot(h, w1_ref[:, cols])
        b = _dot(h, w3_ref[:, cols])
        down.append(_dot((_silu(a) * b).astype(bf16), w2_ref[cols, :]))
    o_ref[...] += down[0] + down[1]

    @pl.when(j == pl.num_programs(1) - 1)
    def _():
        def epilogue(rows):
            y = x_ref[rows, :] + 0.5 * o_ref[rows, :]
            if final_norm:
                y = _rmsnorm(y, gf_ref[...])
            o_ref[rows, :] = y
        over_slabs(epilogue)


def _ffn(x, g, w1, w3, w2, gf, final_norm):
    M, D = x.shape
    F = w1.shape[1]
    tm = min(FFN_TM, M)
    tf = FFN_TF
    assert M % tm == 0 and F % tf == 0
    return pl.pallas_call(
        functools.partial(_ffn_kernel, final_norm=final_norm),
        grid=(M // tm, F // tf),
        in_specs=[
            pl.BlockSpec((tm, D), lambda i, j: (i, 0)),
            pl.BlockSpec((1, D), lambda i, j: (0, 0)),
            pl.BlockSpec((D, tf), lambda i, j: (0, j)),
            pl.BlockSpec((D, tf), lambda i, j: (0, j)),
            pl.BlockSpec((tf, D), lambda i, j: (j, 0)),
            pl.BlockSpec((1, D), lambda i, j: (0, 0)),
        ],
        out_specs=pl.BlockSpec((tm, D), lambda i, j: (i, 0)),
        out_shape=jax.ShapeDtypeStruct((M, D), f32),
        scratch_shapes=[pltpu.VMEM((tm, D), bf16)],
        compiler_params=_cparams(("parallel", "arbitrary")),
        name="ffn",
    )(x, g, w1, w3, w2, gf)


def _norm_matmul_kernel(x_ref, g_ref, w_ref, o_ref, h_ref, *, w_transposed):
    @pl.when(pl.program_id(1) == 0)
    def _():
        h_ref[...] = _rmsnorm(x_ref[...], g_ref[...]).astype(bf16)

    o_ref[...] = (_dot_nt if w_transposed else _dot)(h_ref[...], w_ref[...])


def _norm_matmul(x, g, w, tn, w_transposed=False):
    M, D = x.shape
    N = w.shape[0] if w_transposed else w.shape[1]
    tm = min(PROJ_TM, M)
    assert M % tm == 0 and N % tn == 0
    w_spec = pl.BlockSpec((tn, D), lambda i, j: (j, 0)) if w_transposed else pl.BlockSpec((D, tn), lambda i, j: (0, j))
    return pl.pallas_call(
        functools.partial(_norm_matmul_kernel, w_transposed=w_transposed),
        grid=(M // tm, N // tn),
        in_specs=[
            pl.BlockSpec((tm, D), lambda i, j: (i, 0)),
            pl.BlockSpec((1, D), lambda i, j: (0, 0)),
            w_spec,
        ],
        out_specs=pl.BlockSpec((tm, tn), lambda i, j: (i, j)),
        out_shape=jax.ShapeDtypeStruct((M, N), f32),
        scratch_shapes=[pltpu.VMEM((tm, D), bf16)],
        compiler_params=_cparams(("parallel", "arbitrary")),
        name="norm_matmul",
    )(x, g, w)


def _mem_kv_kernel(x_ref, g_ref, wk_ref, wv_ref, k_ref, v_ref, kb_ref, vb_ref):
    h = _rmsnorm(x_ref[...], g_ref[...]).astype(bf16)
    k = _dot(h, wk_ref[...])
    v = _dot(h, wv_ref[...])
    kb_ref[...] = k.astype(bf16)
    vb_ref[...] = v.astype(bf16)
    tokens = x_ref.shape[0]
    for hd in range(MEM_HEADS):
        k_ref[pl.ds(hd, tokens, stride=MEM_HEADS), :] = k[:, hd * MEM_HD : (hd + 1) * MEM_HD]
        v_ref[pl.ds(hd, tokens, stride=MEM_HEADS), :] = v[:, hd * MEM_HD : (hd + 1) * MEM_HD]


def _mem_kv(x, g, wk, wv):
    M, D = x.shape
    N = wk.shape[1]
    tm = min(PROJ_TM, M)
    assert M % tm == 0 and N == MEM_W
    row_spec = pl.BlockSpec((tm, N), lambda i: (i, 0))
    head_row_spec = pl.BlockSpec((tm * MEM_HEADS, MEM_HD), lambda i: (i, 0))
    const = lambda shape: pl.BlockSpec(shape, lambda i: (0, 0))
    return pl.pallas_call(
        _mem_kv_kernel,
        grid=(M // tm,),
        in_specs=[pl.BlockSpec((tm, D), lambda i: (i, 0)), const((1, D)), const((D, N)), const((D, N))],
        out_specs=[head_row_spec] * 2 + [row_spec] * 2,
        out_shape=[jax.ShapeDtypeStruct((M * MEM_HEADS, MEM_HD), f32)] * 2 + [jax.ShapeDtypeStruct((M, N), bf16)] * 2,
        compiler_params=_cparams(("parallel",)),
        name="mem_kv",
    )(x, g, wk, wv)


def _bdot(a, b):
    return _dot(a.astype(bf16), b.astype(bf16))


def _unit_lower_inverse(nmats, row, col):
    eye = jnp.where(row == col, 1.0, 0.0).astype(f32)

    def same_block(size):
        return (row // size) == (col // size)

    blk = same_block(SUBLANES)
    a1 = [jnp.where(blk, n, 0.0) for n in nmats]
    a2 = [_bdot(a, a) for a in a1]
    a4 = [_bdot(a, a) for a in a2]
    ts = [_bdot(eye - x1, eye + x2) for x1, x2 in zip(a1, a2)]
    ts = [_bdot(t, eye + x4) for t, x4 in zip(ts, a4)]
    size = SUBLANES
    while size < CHUNK:
        nxt = same_block(2 * size)
        sel = jnp.logical_and(nxt, jnp.logical_not(blk))
        tbs = [t.astype(bf16) for t in ts]
        xs = [_dot(jnp.where(sel, n, 0.0).astype(bf16), tb) for n, tb in zip(nmats, tbs)]
        ts = [t - _dot(tb, x.astype(bf16)) for t, tb, x in zip(ts, tbs, xs)]
        blk = nxt
        size *= 2
    return ts


def _conv_silu_slab(x_ref, top_ref, w_ref, sl):
    rows = x_ref.shape[0]
    y = None
    for tap in range(GDN_CONV):
        back = GDN_CONV - 1 - tap
        window = jnp.concatenate(
            [top_ref[SUBLANES - back : 2 * SUBLANES - back, sl], x_ref[SUBLANES - back : rows - back, sl]], axis=0)
        term = window * w_ref[tap : tap + 1, sl]
        y = term if y is None else y + term
    return _silu(y)


def _delta_rule_prepare(pairs, gcum, gcum_t, beta_t, row, col, q_ref, k_ref, v_ref):
    n = range(len(pairs))
    incl = row >= col
    strict = row > col
    rows_of = lambda ci: slice(ci * CHUNK, (ci + 1) * CHUNK)
    gc = [gcum[rows_of(ci), GDN_HEADS + h : GDN_HEADS + h + 1] for ci, h in pairs]
    gr = [gcum_t[ci][GDN_HEADS + h : GDN_HEADS + h + 1, :] for ci, h in pairs]
    beta = [beta_t[rows_of(ci), h : h + 1] for ci, h in pairs]
    k = [k_ref[h, rows_of(ci), :] for ci, h in pairs]
    kb = [x.astype(bf16) for x in k]
    kk = [_dot_nt(kb[i], kb[i]) for i in n]
    qk = [_dot_nt(q_ref[h, rows_of(ci), :].astype(bf16), kb[i]) for i, (ci, h) in enumerate(pairs)]
    e = [jnp.exp(jnp.where(incl, gc[i] - gr[i], 0.0)) for i in n]
    nmat = [beta[i] * kk[i] * jnp.where(strict, e[i], 0.0) for i in n]
    qkd = [(qk[i] * jnp.where(incl, e[i], 0.0)).astype(bf16) for i in n]
    tinv = _unit_lower_inverse(nmat, row, col)
    gamma = [jnp.exp(gc[i]) for i in n]
    rhs = [jnp.concatenate([beta[i] * v_ref[h, rows_of(ci), :], (beta[i] * gamma[i]) * k[i]], axis=-1).astype(bf16)
           for i, (ci, h) in enumerate(pairs)]
    sol = [_dot(tinv[i].astype(bf16), rhs[i]) for i in n]
    return gc, gamma, sol, qkd


def _delta_rule_apply(ci, prepared, q_ref, k_ref, z_ref, gn_ref, s_ref, og_ref):
    gc, gamma, sol, qkd = prepared
    heads = range(GDN_HEADS)
    rows = slice(ci * CHUNK, (ci + 1) * CHUNK)
    sb = [s_ref[h].astype(bf16) for h in heads]
    ub = [(sol[h][:, :GDN_D] - _dot(sol[h][:, GDN_D:].astype(bf16), sb[h])).astype(bf16) for h in heads]
    o = [_dot((q_ref[h, rows, :] * gamma[h]).astype(bf16), sb[h]) + _dot(qkd[h], ub[h]) for h in heads]
    for h in heads:
        g_last = gc[h][CHUNK - 1 : CHUNK, :]
        k_end = k_ref[h, rows, :] * jnp.exp(g_last - gc[h])
        s_ref[h] = jnp.exp(g_last) * s_ref[h] + _dot(k_end.T.astype(bf16), ub[h])
        zh = z_ref[rows, h * GDN_D : (h + 1) * GDN_D]
        og_ref[rows, h * GDN_D : (h + 1) * GDN_D] = (_rmsnorm(o[h], gn_ref[...]) * _silu(zh)).astype(bf16)


def _gdn_prompt_kernel(*refs, n_cast):
    qkv_ref, z_ref, ba_ref, cw_ref, av_ref, dv_ref, gn_ref = refs[:7]
    cast_in = refs[7 : 7 + n_cast]
    og_ref, so_ref = refs[7 + n_cast : 9 + n_cast]
    cast_out = refs[9 + n_cast : 9 + 2 * n_cast]
    s_ref, carry_ref, q_ref, k_ref, v_ref = refs[9 + 2 * n_cast :]
    for src, dst in zip(cast_in, cast_out):
        dst[...] = src[...].astype(bf16)

    c = pl.program_id(1)
    rows = qkv_ref.shape[0]
    n_chunks = rows // CHUNK

    @pl.when(c == 0)
    def _():
        s_ref[...] = jnp.zeros_like(s_ref)
        carry_ref[0:SUBLANES, :] = jnp.zeros((SUBLANES, CONV_CH), f32)

    carry_ref[SUBLANES : 2 * SUBLANES, :] = qkv_ref[0:SUBLANES, :]
    for j in range(3 * GDN_HEADS):
        sl = slice(j * LANES, (j + 1) * LANES)
        y = _conv_silu_slab(qkv_ref, carry_ref, cw_ref, sl)
        h = j % GDN_HEADS
        if j < 2 * GDN_HEADS:
            y = y * lax.rsqrt(jnp.sum(y * y, -1, keepdims=True) + L2_EPS)
            if j < GDN_HEADS:
                q_ref[h] = y * (GDN_D ** -0.5)
            else:
                k_ref[h] = y
        else:
            v_ref[h] = y
    carry_ref[0:SUBLANES, :] = qkv_ref[rows - SUBLANES : rows, :]

    ba = ba_ref[...]
    beta_t = jax.nn.sigmoid(ba)
    g_t = -jnp.exp(av_ref[...]) * _softplus(ba + dv_ref[...])
    row_in_chunk = lax.broadcasted_iota(jnp.int32, (rows, LANES), 0) % CHUNK
    gcum = g_t
    shift = 1
    while shift < CHUNK:
        gcum = gcum + jnp.where(row_in_chunk >= shift, pltpu.roll(gcum, shift, 0), 0.0)
        shift *= 2
    gcum_t = [gcum[ci * CHUNK : (ci + 1) * CHUNK, :].T for ci in range(n_chunks)]

    row = lax.broadcasted_iota(jnp.int32, (CHUNK, CHUNK), 0)
    col = lax.broadcasted_iota(jnp.int32, (CHUNK, CHUNK), 1)
    pairs = [(ci, h) for ci in range(n_chunks) for h in range(GDN_HEADS)]
    gc, gamma, sol, qkd = _delta_rule_prepare(pairs, gcum, gcum_t, beta_t, row, col, q_ref, k_ref, v_ref)
    for ci in range(n_chunks):
        mine = slice(ci * GDN_HEADS, (ci + 1) * GDN_HEADS)
        _delta_rule_apply(ci, (gc[mine], gamma[mine], sol[mine], qkd[mine]), q_ref, k_ref, z_ref, gn_ref, s_ref, og_ref)

    @pl.when(c == pl.num_programs(1) - 1)
    def _():
        so_ref[0] = s_ref[...]


def _cast_block(shape, steps):
    R, C = shape
    for col_blocks in (1, 2, 4, 8, 16):
        row_blocks = steps // col_blocks
        if (steps % col_blocks == 0 and R % row_blocks == 0 and C % col_blocks == 0
                and (R // row_blocks) % (2 * SUBLANES) == 0 and (C // col_blocks) % LANES == 0):
            return (R // row_blocks, C // col_blocks), col_blocks
    return None


def _gdn_prompt(proj, B, T, off_ba, cw, avec, dvec, gn, weights):
    rows = GDN_STEP_CHUNKS * CHUNK
    assert T % rows == 0
    ns = T // rows
    step = lambda b, c: b * ns + c
    plans = [_cast_block(w.shape, B * ns) for w in weights]
    riders = [w for w, p in zip(weights, plans) if p is not None]
    cast_specs = [pl.BlockSpec(blk, functools.partial(lambda b, c, cb: (step(b, c) // cb, step(b, c) % cb), cb=cb))
                  for blk, cb in (p for p in plans if p is not None)]
    head_scratch = pltpu.VMEM((GDN_HEADS, rows, GDN_D), f32)
    out = pl.pallas_call(
        functools.partial(_gdn_prompt_kernel, n_cast=len(riders)),
        grid=(B, ns),
        in_specs=[
            pl.BlockSpec((rows, CONV_CH), lambda b, c: (step(b, c), 0)),
            pl.BlockSpec((rows, GDN_W), lambda b, c: (step(b, c), OFF_Z // GDN_W)),
            pl.BlockSpec((rows, LANES), lambda b, c: (step(b, c), off_ba // LANES)),
            pl.BlockSpec((GDN_CONV, CONV_CH), lambda b, c: (0, 0)),
            pl.BlockSpec((1, LANES), lambda b, c: (0, 0)),
            pl.BlockSpec((1, LANES), lambda b, c: (0, 0)),
            pl.BlockSpec((1, GDN_D), lambda b, c: (0, 0)),
        ] + cast_specs,
        out_specs=[
            pl.BlockSpec((rows, GDN_W), lambda b, c: (step(b, c), 0)),
            pl.BlockSpec((1, GDN_HEADS, GDN_D, GDN_D), lambda b, c: (b, 0, 0, 0)),
        ] + cast_specs,
        out_shape=[
            jax.ShapeDtypeStruct((B * T, GDN_W), bf16),
            jax.ShapeDtypeStruct((B, GDN_HEADS, GDN_D, GDN_D), f32),
        ] + [jax.ShapeDtypeStruct(w.shape, bf16) for w in riders],
        scratch_shapes=[
            pltpu.VMEM((GDN_HEADS, GDN_D, GDN_D), f32),
            pltpu.VMEM((2 * SUBLANES, CONV_CH), f32),
            head_scratch, head_scratch, head_scratch,
        ],
        compiler_params=_cparams(("arbitrary", "arbitrary")),
        name="gdn_prompt",
    )(proj, proj, proj, cw, avec, dvec, gn, *riders)
    converted = iter(out[2:])
    return out[0], out[1], [next(converted) if p is not None else w.astype(bf16) for w, p in zip(weights, plans)]


def _gdn_step_kernel(qkv_ref, z_ref, ba_ref, sc_ref, s0_ref, cw_ref, av_ref, dv_ref, gn_ref, og_ref, so_ref):
    taps = cw_ref[...]
    seqs = range(qkv_ref.shape[0])
    y, z, beta_t, gamma_t = [], [], [], []
    for i in seqs:
        x_new = qkv_ref[i]
        conv = jnp.sum(sc_ref[i] * taps[0 : GDN_CONV - 1, :], axis=0, keepdims=True) + x_new * taps[GDN_CONV - 1 : GDN_CONV, :]
        y.append(_silu(conv))
        ba = ba_ref[i]
        beta_t.append(jax.nn.sigmoid(ba))
        gamma_t.append(jnp.exp(-jnp.exp(av_ref[...]) * _softplus(ba + dv_ref[...])))
        z.append(z_ref[i])
    pairs = [(i, h) for i in seqs for h in range(GDN_HEADS)]
    n = range(len(pairs))
    head = lambda i, base, h: y[i][:, base + h * GDN_D : base + (h + 1) * GDN_D]
    q = [head(i, 0, h) for i, h in pairs]
    k = [head(i, GDN_W, h) for i, h in pairs]
    v = [head(i, 2 * GDN_W, h) for i, h in pairs]
    q = [q[p] * lax.rsqrt(jnp.sum(q[p] * q[p], -1, keepdims=True) + L2_EPS) * (GDN_D ** -0.5) for p in n]
    k = [k[p] * lax.rsqrt(jnp.sum(k[p] * k[p], -1, keepdims=True) + L2_EPS) for p in n]
    beta = [beta_t[i][:, h : h + 1] for i, h in pairs]
    gamma = [gamma_t[i][:, GDN_HEADS + h : GDN_HEADS + h + 1] for i, h in pairs]
    k_col = [jnp.broadcast_to(k[p], (GDN_D, GDN_D)).T for p in n]
    sub = lax.broadcasted_iota(jnp.int32, (SUBLANES, GDN_D), 0)
    kq = [jnp.where(sub == 0, k[p], jnp.where(sub == 1, q[p], 0.0)).astype(bf16) for p in n]
    kq_s = [_dot(kq[p], s0_ref[i, h].astype(bf16)) for p, (i, h) in enumerate(pairs)]
    u = [beta[p] * v[p] - (beta[p] * gamma[p]) * kq_s[p][0:1] for p in n]
    o = [gamma[p] * kq_s[p][1:2] + jnp.sum(q[p] * k[p], -1, keepdims=True) * u[p] for p in n]
    for p, (i, h) in enumerate(pairs):
        so_ref[i, h] = gamma[p] * s0_ref[i, h] + k_col[p] * u[p]
        zh = z[i][:, h * GDN_D : (h + 1) * GDN_D]
        og_ref[i, :, h * GDN_D : (h + 1) * GDN_D] = (_rmsnorm(o[p], gn_ref[...]) * _silu(zh)).astype(bf16)


def _gdn_step(proj3, off_ba, state_conv, state_gdn, cw, avec, dvec, gn):
    B = proj3.shape[0]
    bb = math.gcd(B, GDN_STEP_BATCH)
    return pl.pallas_call(
        _gdn_step_kernel,
        grid=(B // bb,),
        in_specs=[
            pl.BlockSpec((bb, 1, CONV_CH), lambda b: (b, 0, 0)),
            pl.BlockSpec((bb, 1, GDN_W), lambda b: (b, 0, OFF_Z // GDN_W)),
            pl.BlockSpec((bb, 1, LANES), lambda b: (b, 0, off_ba // LANES)),
            pl.BlockSpec((bb, GDN_CONV - 1, CONV_CH), lambda b: (b, 0, 0)),
            pl.BlockSpec((bb, GDN_HEADS, GDN_D, GDN_D), lambda b: (b, 0, 0, 0)),
            pl.BlockSpec((GDN_CONV, CONV_CH), lambda b: (0, 0)),
            pl.BlockSpec((1, LANES), lambda b: (0, 0)),
            pl.BlockSpec((1, LANES), lambda b: (0, 0)),
            pl.BlockSpec((1, GDN_D), lambda b: (0, 0)),
        ],
        out_specs=[
            pl.BlockSpec((bb, 1, GDN_W), lambda b: (b, 0, 0)),
            pl.BlockSpec((bb, GDN_HEADS, GDN_D, GDN_D), lambda b: (b, 0, 0, 0)),
        ],
        out_shape=[
            jax.ShapeDtypeStruct((B, 1, GDN_W), bf16),
            jax.ShapeDtypeStruct((B, GDN_HEADS, GDN_D, GDN_D), f32),
        ],
        compiler_params=_cparams(("parallel",)),
        name="gdn_step",
    )(proj3, proj3, proj3, state_conv, state_gdn, cw, avec, dvec, gn)


def _rope(x, cos, sin_signed):
    width = x.shape[-1]
    lane = lax.broadcasted_iota(jnp.int32, x.shape, x.ndim - 1)
    first_half = (lane % SWA_HD) < (SWA_HD // 2)
    rot = jnp.where(first_half, pltpu.roll(x, width - SWA_HD // 2, x.ndim - 1), pltpu.roll(x, SWA_HD // 2, x.ndim - 1))
    return x * cos + rot * sin_signed


def _head_halves(x2, head_parity, lane):
    swapped = pltpu.roll(x2, SWA_HD, 1)
    lo_src, hi_src = (x2, swapped) if head_parity == 0 else (swapped, x2)
    return jnp.where(lane < SWA_HD, lo_src, 0.0), jnp.where(lane >= SWA_HD, hi_src, 0.0)


def _swa_prompt_kernel(sinks_ref, q_ref, kv_ref, cos_ref, sin_ref, os_ref, kc_ref, kprev_ref, vprev_ref):
    n = pl.program_id(1)
    n_blk = q_ref.shape[0] // WINDOW

    @pl.when(n == 0)
    def _():
        kprev_ref[...] = jnp.zeros_like(kprev_ref)
        vprev_ref[...] = jnp.zeros_like(vprev_ref)

    row = lax.broadcasted_iota(jnp.int32, (WINDOW, WINDOW), 0)
    col = lax.broadcasted_iota(jnp.int32, (WINDOW, WINDOW), 1)
    own = col <= row
    first_bias = jnp.where(n > 0, 0.0, -jnp.inf)
    lane = lax.broadcasted_iota(jnp.int32, (WINDOW, LANES), 1)
    scale = SWA_HD ** -0.5
    group = SWA_HEADS // SWA_KV_HEADS

    k_own, v_own, k_pre, v_pre, q2 = [], [], [], [], []
    for blk in range(n_blk):
        rows = slice(blk * WINDOW, (blk + 1) * WINDOW)
        cos = cos_ref[rows, :]
        sin = sin_ref[rows, :]
        k_cur = _rope(kv_ref[rows, :SWA_KV_W], cos, sin)
        v_cur = kv_ref[rows, SWA_KV_W:]
        if blk == n_blk - 1:
            kc_ref[0] = k_cur
        k_own.append([]), v_own.append([]), k_pre.append([]), v_pre.append([]), q2.append([])
        for h in range(SWA_KV_HEADS):
            pair = slice((h // 2) * LANES, (h // 2 + 1) * LANES)
            k_own[blk].append([a.astype(bf16) for a in _head_halves(k_cur[:, pair], h % 2, lane)])
            v_own[blk].append([a.astype(bf16) for a in _head_halves(v_cur[:, pair], h % 2, lane)])
            if blk == 0:
                k_pre[blk].append([kprev_ref[h, par] for par in range(2)])
                v_pre[blk].append([vprev_ref[h, par] for par in range(2)])
            else:
                k_pre[blk].append(k_own[blk - 1][h])
                v_pre[blk].append(v_own[blk - 1][h])
            if blk == n_blk - 1:
                for par in range(2):
                    kprev_ref[h, par] = k_own[blk][h][par]
                    vprev_ref[h, par] = v_own[blk][h][par]
            q_h = _rope(q_ref[rows, h * group * SWA_HD : (h + 1) * group * SWA_HD], cos, sin) * scale
            q2[blk].append([q_h[:, j * LANES : (j + 1) * LANES].astype(bf16) for j in range(group // 2)])

    heads = [(blk, h, j, par) for blk in range(n_blk) for h in range(SWA_KV_HEADS)
             for j in range(group // 2) for par in range(2)]
    idx = range(len(heads))
    sink = [sinks_ref[h * group + 2 * j + par] for blk, h, j, par in heads]
    s = [jnp.where(own, _dot_nt(q2[blk][h][j], k_own[blk][h][par]),
                   _dot_nt(q2[blk][h][j], k_pre[blk][h][par]) + (first_bias if blk == 0 else 0.0))
         for blk, h, j, par in heads]
    m = [jnp.maximum(jnp.max(s[i], -1, keepdims=True), sink[i]) for i in idx]
    p = [jnp.exp(s[i] - m[i]) for i in idx]
    inv = [1.0 / (jnp.sum(p[i], -1, keepdims=True) + jnp.exp(sink[i] - m[i])) for i in idx]
    o = [(_dot(jnp.where(own, p[i], 0.0).astype(bf16), v_own[blk][h][par])
          + _dot(jnp.where(own, 0.0, p[i]).astype(bf16), v_pre[blk][h][par])) * inv[i]
         for i, (blk, h, j, par) in enumerate(heads)]
    for i in range(0, len(heads), 2):
        blk, head = divmod(i, SWA_HEADS)
        os_ref[blk * WINDOW : (blk + 1) * WINDOW, head * SWA_HD : head * SWA_HD + LANES] = (o[i] + o[i + 1]).astype(bf16)


def _swa_prompt(proj, B, T, off_q, off_kv, cos, sin, sinks):
    assert T % WINDOW == 0 and SWA_KV_W == 2 * LANES
    rows = math.gcd(SWA_STEP_BLOCKS, T // WINDOW) * WINDOW
    nb = T // rows
    return pl.pallas_call(
        _swa_prompt_kernel,
        grid=(B, nb),
        in_specs=[
            pl.BlockSpec(memory_space=pltpu.SMEM),
            pl.BlockSpec((rows, SWA_Q_W), lambda b, n: (b * nb + n, off_q // SWA_Q_W)),
            pl.BlockSpec((rows, 2 * SWA_KV_W), lambda b, n: (b * nb + n, off_kv // (2 * SWA_KV_W))),
            pl.BlockSpec((rows, SWA_KV_W), lambda b, n: (n, 0)),
            pl.BlockSpec((rows, SWA_KV_W), lambda b, n: (n, 0)),
        ],
        out_specs=[
            pl.BlockSpec((rows, SWA_Q_W), lambda b, n: (b * nb + n, 0)),
            pl.BlockSpec((1, WINDOW, SWA_KV_W), lambda b, n: (b, 0, 0)),
        ],
        out_shape=[
            jax.ShapeDtypeStruct((B * T, SWA_Q_W), bf16),
            jax.ShapeDtypeStruct((B, WINDOW, SWA_KV_W), f32),
        ],
        scratch_shapes=[pltpu.VMEM((SWA_KV_HEADS, 2, WINDOW, LANES), bf16)] * 2,
        compiler_params=_cparams(("parallel", "arbitrary")),
        name="swa_prompt",
    )(sinks, proj, proj, cos, sin)


def _swa_step_kernel(qe_ref, kv_ref, ck_ref, cv_ref, cos_ref, sin_ref, sinks_ref, r_ref, nk_ref, nv_ref):
    cos = cos_ref[...]
    sin = sin_ref[...]
    sink = sinks_ref[...]
    row = lax.broadcasted_iota(jnp.int32, (WINDOW, SWA_KV_W), 0)
    scale = SWA_HD ** -0.5
    seqs = range(qe_ref.shape[0])
    kv = [kv_ref[i] for i in seqs]
    k_new = [_rope(kv[i][:, :SWA_KV_W], cos, sin) for i in seqs]
    keys = [jnp.where(row == WINDOW - 1, k_new[i], pltpu.roll(ck_ref[i].T, WINDOW - 1, 0)) for i in seqs]
    vals = [jnp.where(row == WINDOW - 1, kv[i][:, SWA_KV_W:], pltpu.roll(cv_ref[i].T, WINDOW - 1, 0)) for i in seqs]
    for i in seqs:
        nk_ref[i] = keys[i].T
        nv_ref[i] = vals[i].T
    q = [_rope(qe_ref[i], cos, sin) for i in seqs]
    s = [_dot_nt(q[i].astype(bf16), keys[i].astype(bf16)) * scale for i in seqs]
    m = [jnp.maximum(jnp.max(s[i], -1, keepdims=True), sink) for i in seqs]
    p = [jnp.exp(s[i] - m[i]) for i in seqs]
    denom = [jnp.sum(p[i], -1, keepdims=True) + jnp.exp(sink - m[i]) for i in seqs]
    for i in seqs:
        r_ref[i] = _dot((p[i] / denom[i]).astype(bf16), vals[i].astype(bf16))


def _swa_step(q_exp, proj3, off_kv, cache_k, cache_v, cos, sin, sinks_col):
    B = q_exp.shape[0]
    bb = math.gcd(B, SWA_STEP_BATCH)
    assert cache_k.shape[2] == WINDOW
    return pl.pallas_call(
        _swa_step_kernel,
        grid=(B // bb,),
        in_specs=[
            pl.BlockSpec((bb, SWA_HEADS, SWA_KV_W), lambda i: (i, 0, 0)),
            pl.BlockSpec((bb, 1, 2 * SWA_KV_W), lambda i: (i, 0, off_kv // (2 * SWA_KV_W))),
            pl.BlockSpec((bb, SWA_KV_W, WINDOW), lambda i: (i, 0, 0)),
            pl.BlockSpec((bb, SWA_KV_W, WINDOW), lambda i: (i, 0, 0)),
            pl.BlockSpec((1, SWA_KV_W), lambda i: (0, 0)),
            pl.BlockSpec((1, SWA_KV_W), lambda i: (0, 0)),
            pl.BlockSpec((SWA_HEADS, 1), lambda i: (0, 0)),
        ],
        out_specs=[
            pl.BlockSpec((bb, SWA_HEADS, SWA_KV_W), lambda i: (i, 0, 0)),
            pl.BlockSpec((bb, SWA_KV_W, WINDOW), lambda i: (i, 0, 0)),
            pl.BlockSpec((bb, SWA_KV_W, WINDOW), lambda i: (i, 0, 0)),
        ],
        out_shape=[
            jax.ShapeDtypeStruct((B, SWA_HEADS, SWA_KV_W), f32),
            jax.ShapeDtypeStruct((B, SWA_KV_W, WINDOW), f32),
            jax.ShapeDtypeStruct((B, SWA_KV_W, WINDOW), f32),
        ],
        compiler_params=_cparams(("parallel",)),
        name="swa_step",
    )(q_exp, proj3, cache_k, cache_v, cos, sin, sinks_col)


def _merge_core(og_ref, os_ref, gg_ref, gs_ref, x_ref, wg_ref, ws_ref, wo_ref, gq_ref, wq_ref):
    p_gdn = _dot(og_ref[...], wg_ref[...])
    p_swa = _dot(os_ref[...], ws_ref[...])
    merged = jax.nn.sigmoid(gg_ref[...]) * p_gdn + jax.nn.sigmoid(gs_ref[...]) * p_swa
    x_new = x_ref[...] + _dot(merged.astype(bf16), wo_ref[...])
    return x_new, _dot(_rmsnorm(x_new, gq_ref[...]).astype(bf16), wq_ref[...])


def _merge_kernel(*refs):
    xo_ref, qm_ref = refs[-2:]
    xo_ref[...], qm_ref[...] = _merge_core(*refs[:-2])


def _merge_mem_kernel(*refs):
    mk_ref, mv_ref, wmo_ref, xo_ref = refs[-4:]
    x_new, q = _merge_core(*refs[:-4])
    q = q.astype(bf16)
    scale = MEM_HD ** -0.5
    heads = range(MEM_HEADS)
    cols = lambda h: slice(h * MEM_HD, (h + 1) * MEM_HD)
    s = [_dot_nt(q[:, cols(h)], mk_ref[0, :, cols(h)]) * scale for h in heads]
    p = [jnp.exp(s[h] - jnp.max(s[h], -1, keepdims=True)) for h in heads]
    p = [p[h] / jnp.sum(p[h], -1, keepdims=True) for h in heads]
    o = [_dot(p[h].astype(bf16), mv_ref[0, :, cols(h)]) for h in heads]
    xo_ref[...] = x_new + _dot(jnp.concatenate(o, axis=-1).astype(bf16), wmo_ref[...])


def _merge(og, os_, proj, off_gg, x, wg, ws, wo, gq, wq, mem=None):
    M, D = x.shape
    tm = min(MERGE_TM, M)
    assert M % tm == 0 and off_gg % D == 0
    const = lambda shape: pl.BlockSpec(shape, lambda i: (0, 0), pipeline_mode=pl.Buffered(1))
    in_specs = [
        pl.BlockSpec((tm, GDN_W), lambda i: (i, 0)),
        pl.BlockSpec((tm, SWA_Q_W), lambda i: (i, 0)),
        pl.BlockSpec((tm, D), lambda i: (i, off_gg // D)),
        pl.BlockSpec((tm, D), lambda i: (i, off_gg // D + 1)),
        pl.BlockSpec((tm, D), lambda i: (i, 0)),
        const((GDN_W, D)),
        const((SWA_Q_W, D)),
        const((D, D)),
        const((1, D)),
        const((D, MEM_W)),
    ]
    args = [og, os_, proj, proj, x, wg, ws, wo, gq, wq]
    x_spec = pl.BlockSpec((tm, D), lambda i: (i, 0))
    x_shape = jax.ShapeDtypeStruct((M, D), f32)
    if mem is None:
        return pl.pallas_call(
            _merge_kernel,
            grid=(M // tm,),
            in_specs=in_specs,
            out_specs=[x_spec, pl.BlockSpec((tm, MEM_W), lambda i: (i, 0))],
            out_shape=[x_shape, jax.ShapeDtypeStruct((M, MEM_W), f32)],
            compiler_params=_cparams(("parallel",)),
            name="merge",
        )(*args)
    mem_k, mem_v, wmo, seq_rows = mem
    assert seq_rows % tm == 0
    mt = mem_k.shape[1]
    mem_spec = pl.BlockSpec((1, mt, MEM_W), lambda i: (i // (seq_rows // tm), 0, 0))
    return pl.pallas_call(
        _merge_mem_kernel,
        grid=(M // tm,),
        in_specs=in_specs + [mem_spec, mem_spec, const((MEM_W, D))],
        out_specs=x_spec,
        out_shape=x_shape,
        compiler_params=_cparams(("parallel",)),
        name="merge_mem",
    )(*args, mem_k, mem_v, wmo)


def _mem_attn_step_kernel(q_ref, k_ref, v_ref, o_ref):
    mt2 = k_ref.shape[1] // SUBLANES
    for i in range(q_ref.shape[0]):
        q8 = q_ref[i] * (MEM_HD ** -0.5)
        s = jnp.sum(k_ref[i].reshape(mt2, SUBLANES, MEM_HD) * q8, axis=-1, keepdims=True)
        m = jnp.max(s, axis=0)
        m = jnp.maximum(m, pltpu.roll(m, MEM_HEADS, 0))
        p = jnp.exp(s - m)
        l = jnp.sum(p, axis=0)
        l = l + pltpu.roll(l, MEM_HEADS, 0)
        o = jnp.sum(p * v_ref[i].reshape(mt2, SUBLANES, MEM_HD), axis=0)
        o_ref[i] = (o + pltpu.roll(o, MEM_HEADS, 0)) / l


def _mem_attn_step(q8, mem_k, mem_v):
    B, rows, _ = mem_k.shape
    assert 2 * MEM_HEADS == SUBLANES and rows % SUBLANES == 0
    bb = math.gcd(B, MEM_STEP_BATCH)
    return pl.pallas_call(
        _mem_attn_step_kernel,
        grid=(B // bb,),
        in_specs=[
            pl.BlockSpec((bb, SUBLANES, MEM_HD), lambda i: (i, 0, 0)),
            pl.BlockSpec((bb, rows, MEM_HD), lambda i: (i, 0, 0)),
            pl.BlockSpec((bb, rows, MEM_HD), lambda i: (i, 0, 0)),
        ],
        out_specs=pl.BlockSpec((bb, SUBLANES, MEM_HD), lambda i: (i, 0, 0)),
        out_shape=jax.ShapeDtypeStruct((B, SUBLANES, MEM_HD), f32),
        compiler_params=_cparams(("parallel",)),
        name="mem_attn_step",
    )(q8, mem_k, mem_v)


def _proj_residual_kernel(a_ref, w_ref, x_ref, o_ref):
    o_ref[...] = x_ref[...] + _dot(a_ref[...].astype(bf16), w_ref[...])


def _proj_residual(a, w, x):
    M, D = x.shape
    return pl.pallas_call(
        _proj_residual_kernel,
        out_shape=jax.ShapeDtypeStruct((M, D), f32),
        compiler_params=pltpu.CompilerParams(vmem_limit_bytes=VMEM_LIMIT),
        name="proj_residual",
    )(a, w, x)


def _reorder_rows_kernel(w_ref, o_ref, *, pieces):
    at = 0
    for lo, hi in pieces:
        o_ref[at : at + hi - lo, :] = w_ref[lo:hi, :].astype(bf16)
        at += hi - lo
    o_ref[at:, :] = jnp.zeros((o_ref.shape[0] - at, o_ref.shape[1]), bf16)


def _reorder_rows(wt, layer, pieces, height):
    _, d_in, D = wt.shape
    assert all(lo % (2 * SUBLANES) == 0 and hi % (2 * SUBLANES) == 0 for lo, hi in pieces) and D % LANES == 0
    return pl.pallas_call(
        functools.partial(_reorder_rows_kernel, pieces=pieces),
        grid=(D // LANES,),
        in_specs=[pl.BlockSpec((None, d_in, LANES), lambda i: (layer, 0, i))],
        out_specs=pl.BlockSpec((height, LANES), lambda i: (0, i)),
        out_shape=jax.ShapeDtypeStruct((height, D), bf16),
        compiler_params=_cparams(("parallel",)),
        name="reorder_rows",
    )(wt)


def _rope_tables(pos):
    half = SWA_HD // 2
    inv_freq = ROPE_THETA ** (-jnp.arange(half, dtype=f32) / half)
    ang = pos.astype(f32)[:, None] * inv_freq[None, :]
    cos = jnp.cos(ang)
    sin = jnp.sin(ang)
    reps = SWA_KV_W // SWA_HD
    return jnp.tile(jnp.concatenate([cos, cos], -1), (1, reps)), jnp.tile(jnp.concatenate([-sin, sin], -1), (1, reps))


def kernel(x_prompt, x_sample, state_gdn, state_conv, cache_swa_k, cache_swa_v, cache_mem_k, cache_mem_v, mem_prompt, norm_ffn1, ffn1_w1, ffn1_w3, ffn1_w2, norm_mix, w_in, conv_w, gdn_A_log, gdn_dt_bias, gdn_norm, swa_sinks, w_br_gdn, w_br_swa, w_out, norm_mem_q, norm_mem_kv, w_mem_q, w_mem_k, w_mem_v, w_mem_o, norm_ffn2, ffn2_w1, ffn2_w3, ffn2_w2, norm_final):
    Bp, Tp, D = x_prompt.shape
    Bs, Ts, _ = x_sample.shape
    assert Ts == 1
    depth = norm_ffn1.shape[0]
    n_mem = mem_prompt.shape[1]
    group = SWA_HEADS // SWA_KV_HEADS

    off_gs = OFF_GG + D
    off_q = off_gs + D
    off_kv = off_q + SWA_Q_W
    off_ba = off_kv + 2 * SWA_KV_W
    d_in_pad = -(-(off_ba + LANES) // PROJ_TN) * PROJ_TN
    o_b = CONV_CH + GDN_W
    o_q = o_b + 2 * GDN_HEADS
    o_gg = o_q + SWA_Q_W + 2 * SWA_KV_W

    cos_p, sin_p = _rope_tables(jnp.arange(Tp, dtype=jnp.int32))
    cos_s, sin_s = _rope_tables(PAST_LEN + jnp.arange(Ts, dtype=jnp.int32))
    eye_kv = jnp.eye(SWA_KV_HEADS, dtype=f32)
    row = lambda v: v.reshape(1, -1)

    hp = x_prompt.reshape(Bp * Tp, D)
    hs = x_sample.reshape(Bs, D)
    outs = [[] for _ in range(10)]
    for l in range(depth):
        w_in_rt = _reorder_rows(jnp.swapaxes(w_in, 1, 2), l, ((0, o_b), (o_gg, w_in.shape[2]), (o_q, o_gg), (o_b, o_q)), d_in_pad)
        ffn1 = (row(norm_ffn1[l]), ffn1_w1[l].astype(bf16), ffn1_w3[l].astype(bf16), ffn1_w2[l].astype(bf16))
        last = l == depth - 1
        gfin = row(norm_final)
        avec = jnp.zeros((1, LANES), f32).at[0, GDN_HEADS : 2 * GDN_HEADS].set(gdn_A_log[l])
        dvec = jnp.zeros((1, LANES), f32).at[0, GDN_HEADS : 2 * GDN_HEADS].set(gdn_dt_bias[l])
        gdn_common = (conv_w[l], avec, dvec, row(gdn_norm[l]))

        x1 = _ffn(hp, *ffn1, gfin, False)
        proj = _norm_matmul(x1, row(norm_mix[l]), w_in_rt, PROJ_TN, w_transposed=True)
        later = (ffn2_w1[l], ffn2_w3[l], ffn2_w2[l], w_br_gdn[l], w_br_swa[l], w_out[l], w_mem_q[l],
                 w_mem_k[l], w_mem_v[l], w_mem_o[l])
        og, s_new, later = _gdn_prompt(proj, Bp, Tp, off_ba, *gdn_common, later)
        ffn2 = (row(norm_ffn2[l]), *later[0:3])
        merge_w = (*later[3:6], row(norm_mem_q[l]), later[6])
        wmk, wmv, wmo = later[7:10]
        os_, kc = _swa_prompt(proj, Bp, Tp, off_q, off_kv, cos_p, sin_p, swa_sinks[l])
        mem_x = mem_prompt.reshape(Bp * n_mem, D)
        mk, mv, mkb, mvb = _mem_kv(mem_x, row(norm_mem_kv[l]), wmk, wmv)
        mem = (mkb.reshape(Bp, n_mem, MEM_W), mvb.reshape(Bp, n_mem, MEM_W), wmo, Tp)
        x3 = _merge(og, os_, proj, OFF_GG, x1, *merge_w, mem=mem)
        hp = _ffn(x3, *ffn2, gfin, last)
        proj_b = proj.reshape(Bp, Tp, d_in_pad)
        outs[0].append(s_new)
        outs[1].append(proj_b[:, Tp - (GDN_CONV - 1) :, :CONV_CH])
        outs[2].append(kc.reshape(Bp, WINDOW, SWA_KV_HEADS, SWA_HD))
        outs[3].append(proj_b[:, Tp - WINDOW :, off_kv + SWA_KV_W : off_kv + 2 * SWA_KV_W].reshape(Bp, WINDOW, SWA_KV_HEADS, SWA_HD))
        outs[4].append(mk.reshape(Bp, n_mem, MEM_HEADS, MEM_HD))
        outs[5].append(mv.reshape(Bp, n_mem, MEM_HEADS, MEM_HD))

        x1 = _ffn(hs, *ffn1, gfin, False)
        proj = _norm_matmul(x1, row(norm_mix[l]), w_in_rt, PROJ_TN, w_transposed=True)
        proj3 = proj.reshape(Bs, 1, d_in_pad)
        og, s_new = _gdn_step(proj3, off_ba, state_conv[l], state_gdn[l], *gdn_common)
        q_raw = proj[:, off_q : off_q + SWA_Q_W].reshape(Bs, SWA_KV_HEADS, group, 1, SWA_HD)
        q_exp = (q_raw * eye_kv[None, :, None, :, None]).reshape(Bs, SWA_HEADS, SWA_KV_W)
        ck = jnp.swapaxes(cache_swa_k[l].reshape(Bs, WINDOW, SWA_KV_W), 1, 2)
        cv = jnp.swapaxes(cache_swa_v[l].reshape(Bs, WINDOW, SWA_KV_W), 1, 2)
        r, nk, nv = _swa_step(q_exp, proj3, off_kv, ck, cv, cos_s, sin_s, swa_sinks[l].reshape(SWA_HEADS, 1))
        r5 = r.reshape(Bs, SWA_KV_HEADS, group, SWA_KV_HEADS, SWA_HD)
        kvh = jnp.arange(SWA_KV_HEADS)
        os_ = jnp.transpose(r5[:, kvh, :, kvh, :], (1, 0, 2, 3)).reshape(Bs, SWA_Q_W).astype(bf16)
        x2, qm = _merge(og.reshape(Bs, GDN_W), os_, proj, OFF_GG, x1, *merge_w)
        q8 = jnp.tile(qm.reshape(Bs, MEM_HEADS, MEM_HD), (1, 2, 1))
        om = _mem_attn_step(q8, cache_mem_k[l].reshape(Bs, n_mem * MEM_HEADS, MEM_HD),
                            cache_mem_v[l].reshape(Bs, n_mem * MEM_HEADS, MEM_HD))
        x3 = _proj_residual(om[:, :MEM_HEADS].reshape(Bs, MEM_W), wmo, x2)
        hs = _ffn(x3, *ffn2, gfin, last)
        outs[6].append(s_new)
        outs[7].append(jnp.concatenate([state_conv[l][:, 1:], proj3[:, :, :CONV_CH]], axis=1))
        outs[8].append(jnp.swapaxes(nk, 1, 2).reshape(Bs, WINDOW, SWA_KV_HEADS, SWA_HD))
        outs[9].append(jnp.swapaxes(nv, 1, 2).reshape(Bs, WINDOW, SWA_KV_HEADS, SWA_HD))

    return (hp.reshape(Bp, Tp, D), hs.reshape(Bs, Ts, D), *(jnp.stack(o) for o in outs))
```

```python
import functools
import math

import jax
import jax.numpy as jnp
from jax import lax
from jax.experimental import pallas as pl
from jax.experimental.pallas import tpu as pltpu

f32 = jnp.float32
bf16 = jnp.bfloat16

PAST_LEN = 16384
GDN_HEADS = 8
GDN_D = 128
GDN_CONV = 4
SWA_HEADS = 16
SWA_KV_HEADS = 4
SWA_HD = 64
WINDOW = 128
ROPE_THETA = 10000.0
MEM_HEADS = 4
MEM_HD = 128
EPS = 1e-6
L2_EPS = 1e-6

LANES = 128
SUBLANES = 8
VMEM_LIMIT = 60 * 1024 * 1024

FFN_TM = 1024
FFN_TF = 512
FFN_SLAB = 64
PROJ_TM = 1024
PROJ_TN = 768
MERGE_TM = 256
CHUNK = 128
GDN_STEP_CHUNKS = 2
SWA_STEP_BLOCKS = 4
GDN_STEP_BATCH = 8
SWA_STEP_BATCH = 8
MEM_STEP_BATCH = 8

GDN_W = GDN_HEADS * GDN_D
CONV_CH = 3 * GDN_W
SWA_Q_W = SWA_HEADS * SWA_HD
SWA_KV_W = SWA_KV_HEADS * SWA_HD
MEM_W = MEM_HEADS * MEM_HD

OFF_Z = CONV_CH
OFF_GG = OFF_Z + GDN_W


def _cparams(semantics):
    return pltpu.CompilerParams(dimension_semantics=semantics, vmem_limit_bytes=VMEM_LIMIT)


def _dot(a, b):
    return jnp.dot(a, b, preferred_element_type=f32)


def _dot_nt(a, b):
    return lax.dot_general(a, b, (((1,), (1,)), ((), ())), preferred_element_type=f32)


def _rmsnorm(x, g):
    return x * lax.rsqrt(jnp.mean(x * x, -1, keepdims=True) + EPS) * g


def _silu(x):
    return x * jax.nn.sigmoid(x)


def _softplus(x):
    return jnp.maximum(x, 0.0) + jnp.log1p(jnp.exp(-jnp.abs(x)))


def _ffn_kernel(x_ref, g_ref, w1_ref, w3_ref, w2_ref, gf_ref, o_ref, h_ref, *, final_norm):
    j = pl.program_id(1)
    slab = min(FFN_SLAB, x_ref.shape[0])
    n_slabs = x_ref.shape[0] // slab

    def over_slabs(body):
        def step(s, carry):
            body(pl.ds(pl.multiple_of(s * slab, slab), slab))
            return carry
        lax.fori_loop(0, n_slabs, step, 0, unroll=min(4, n_slabs))

    @pl.when(j == 0)
    def _():
        def prologue(rows):
            h_ref[rows, :] = _rmsnorm(x_ref[rows, :], g_ref[...]).astype(bf16)
            o_ref[rows, :] = jnp.zeros((slab, o_ref.shape[1]), f32)
        over_slabs(prologue)

    h = h_ref[...]
    half = w1_ref.shape[1] // 2
    down = []
    for cols in (slice(0, half), slice(half, 2 * half)):
        a = _dot(h, w1_ref[:, cols])
        b = _dot(h, w3_ref[:, cols])
        down.append(_dot((_silu(a) * b).astype(bf16), w2_ref[cols, :]))
    o_ref[...] += down[0] + down[1]

    @pl.when(j == pl.num_programs(1) - 1)
    def _():
        def epilogue(rows):
            y = x_ref[rows, :] + 0.5 * o_ref[rows, :]
            if final_norm:
                y = _rmsnorm(y, gf_ref[...])
            o_ref[rows, :] = y
        over_slabs(epilogue)


def _ffn(x, g, w1, w3, w2, gf, final_norm):
    M, D = x.shape
    F = w1.shape[1]
    tm = min(FFN_TM, M)
    tf = FFN_TF
    assert M % tm == 0 and F % tf == 0
    return pl.pallas_call(
        functools.partial(_ffn_kernel, final_norm=final_norm),
        grid=(M // tm, F // tf),
        in_specs=[
            pl.BlockSpec((tm, D), lambda i, j: (i, 0)),
            pl.BlockSpec((1, D), lambda i, j: (0, 0)),
            pl.BlockSpec((D, tf), lambda i, j: (0, j)),
            pl.BlockSpec((D, tf), lambda i, j: (0, j)),
            pl.BlockSpec((tf, D), lambda i, j: (j, 0)),
            pl.BlockSpec((1, D), lambda i, j: (0, 0)),
        ],
        out_specs=pl.BlockSpec((tm, D), lambda i, j: (i, 0)),
        out_shape=jax.ShapeDtypeStruct((M, D), f32),
        scratch_shapes=[pltpu.VMEM((tm, D), bf16)],
        compiler_params=_cparams(("parallel", "arbitrary")),
        name="ffn",
    )(x, g, w1, w3, w2, gf)


def _norm_matmul_kernel(x_ref, g_ref, w_ref, o_ref, h_ref, *, w_transposed):
    @pl.when(pl.program_id(1) == 0)
    def _():
        h_ref[...] = _rmsnorm(x_ref[...], g_ref[...]).astype(bf16)

    o_ref[...] = (_dot_nt if w_transposed else _dot)(h_ref[...], w_ref[...])


def _norm_matmul(x, g, w, tn, w_transposed=False):
    M, D = x.shape
    N = w.shape[0] if w_transposed else w.shape[1]
    tm = min(PROJ_TM, M)
    assert M % tm == 0 and N % tn == 0
    w_spec = pl.BlockSpec((tn, D), lambda i, j: (j, 0)) if w_transposed else pl.BlockSpec((D, tn), lambda i, j: (0, j))
    return pl.pallas_call(
        functools.partial(_norm_matmul_kernel, w_transposed=w_transposed),
        grid=(M // tm, N // tn),
        in_specs=[
            pl.BlockSpec((tm, D), lambda i, j: (i, 0)),
            pl.BlockSpec((1, D), lambda i, j: (0, 0)),
            w_spec,
        ],
        out_specs=pl.BlockSpec((tm, tn), lambda i, j: (i, j)),
        out_shape=jax.ShapeDtypeStruct((M, N), f32),
        scratch_shapes=[pltpu.VMEM((tm, D), bf16)],
        compiler_params=_cparams(("parallel", "arbitrary")),
        name="norm_matmul",
    )(x, g, w)


def _mem_kv_kernel(x_ref, g_ref, wk_ref, wv_ref, k_ref, v_ref, kb_ref, vb_ref):
    h = _rmsnorm(x_ref[...], g_ref[...]).astype(bf16)
    k = _dot(h, wk_ref[...])
    v = _dot(h, wv_ref[...])
    kb_ref[...] = k.astype(bf16)
    vb_ref[...] = v.astype(bf16)
    tokens = x_ref.shape[0]
    for hd in range(MEM_HEADS):
        k_ref[pl.ds(hd, tokens, stride=MEM_HEADS), :] = k[:, hd * MEM_HD : (hd + 1) * MEM_HD]
        v_ref[pl.ds(hd, tokens, stride=MEM_HEADS), :] = v[:, hd * MEM_HD : (hd + 1) * MEM_HD]


def _mem_kv(x, g, wk, wv):
    M, D = x.shape
    N = wk.shape[1]
    tm = min(PROJ_TM, M)
    assert M % tm == 0 and N == MEM_W
    row_spec = pl.BlockSpec((tm, N), lambda i: (i, 0))
    head_row_spec = pl.BlockSpec((tm * MEM_HEADS, MEM_HD), lambda i: (i, 0))
    const = lambda shape: pl.BlockSpec(shape, lambda i: (0, 0))
    return pl.pallas_call(
        _mem_kv_kernel,
        grid=(M // tm,),
        in_specs=[pl.BlockSpec((tm, D), lambda i: (i, 0)), const((1, D)), const((D, N)), const((D, N))],
        out_specs=[head_row_spec] * 2 + [row_spec] * 2,
        out_shape=[jax.ShapeDtypeStruct((M * MEM_HEADS, MEM_HD), f32)] * 2 + [jax.ShapeDtypeStruct((M, N), bf16)] * 2,
        compiler_params=_cparams(("parallel",)),
        name="mem_kv",
    )(x, g, wk, wv)


def _bdot(a, b):
    return _dot(a.astype(bf16), b.astype(bf16))


def _unit_lower_inverse(nmats, row, col):
    eye = jnp.where(row == col, 1.0, 0.0).astype(f32)

    def same_block(size):
        return (row // size) == (col // size)

    blk = same_block(SUBLANES)
    a1 = [jnp.where(blk, n, 0.0) for n in nmats]
    a2 = [_bdot(a, a) for a in a1]
    a4 = [_bdot(a, a) for a in a2]
    ts = [_bdot(eye - x1, eye + x2) for x1, x2 in zip(a1, a2)]
    ts = [_bdot(t, eye + x4) for t, x4 in zip(ts, a4)]
    size = SUBLANES
    while size < CHUNK:
        nxt = same_block(2 * size)
        sel = jnp.logical_and(nxt, jnp.logical_not(blk))
        tbs = [t.astype(bf16) for t in ts]
        xs = [_dot(jnp.where(sel, n, 0.0).astype(bf16), tb) for n, tb in zip(nmats, tbs)]
        ts = [t - _dot(tb, x.astype(bf16)) for t, tb, x in zip(ts, tbs, xs)]
        blk = nxt
        size *= 2
    return ts


def _conv_silu_slab(x_ref, top_ref, w_ref, sl):
    rows = x_ref.shape[0]
    y = None
    for tap in range(GDN_CONV):
        back = GDN_CONV - 1 - tap
        window = jnp.concatenate(
            [top_ref[SUBLANES - back : 2 * SUBLANES - back, sl], x_ref[SUBLANES - back : rows - back, sl]], axis=0)
        term = window * w_ref[tap : tap + 1, sl]
        y = term if y is None else y + term
    return _silu(y)


def _delta_rule_prepare(pairs, gcum, gcum_t, beta_t, row, col, q_ref, k_ref, v_ref):
    n = range(len(pairs))
    incl = row >= col
    strict = row > col
    rows_of = lambda ci: slice(ci * CHUNK, (ci + 1) * CHUNK)
    gc = [gcum[rows_of(ci), GDN_HEADS + h : GDN_HEADS + h + 1] for ci, h in pairs]
    gr = [gcum_t[ci][GDN_HEADS + h : GDN_HEADS + h + 1, :] for ci, h in pairs]
    beta = [beta_t[rows_of(ci), h : h + 1] for ci, h in pairs]
    k = [k_ref[h, rows_of(ci), :] for ci, h in pairs]
    kb = [x.astype(bf16) for x in k]
    kk = [_dot_nt(kb[i], kb[i]) for i in n]
    qk = [_dot_nt(q_ref[h, rows_of(ci), :].astype(bf16), kb[i]) for i, (ci, h) in enumerate(pairs)]
    e = [jnp.exp(jnp.where(incl, gc[i] - gr[i], 0.0)) for i in n]
    nmat = [beta[i] * kk[i] * jnp.where(strict, e[i], 0.0) for i in n]
    qkd = [(qk[i] * jnp.where(incl, e[i], 0.0)).astype(bf16) for i in n]
    tinv = _unit_lower_inverse(nmat, row, col)
    gamma = [jnp.exp(gc[i]) for i in n]
    rhs = [jnp.concatenate([beta[i] * v_ref[h, rows_of(ci), :], (beta[i] * gamma[i]) * k[i]], axis=-1).astype(bf16)
           for i, (ci, h) in enumerate(pairs)]
    sol = [_dot(tinv[i].astype(bf16), rhs[i]) for i in n]
    return gc, gamma, sol, qkd


def _delta_rule_apply(ci, prepared, q_ref, k_ref, z_ref, gn_ref, s_ref, og_ref):
    gc, gamma, sol, qkd = prepared
    heads = range(GDN_HEADS)
    rows = slice(ci * CHUNK, (ci + 1) * CHUNK)
    sb = [s_ref[h].astype(bf16) for h in heads]
    ub = [(sol[h][:, :GDN_D] - _dot(sol[h][:, GDN_D:].astype(bf16), sb[h])).astype(bf16) for h in heads]
    o = [_dot((q_ref[h, rows, :] * gamma[h]).astype(bf16), sb[h]) + _dot(qkd[h], ub[h]) for h in heads]
    for h in heads:
        g_last = gc[h][CHUNK - 1 : CHUNK, :]
        k_end = k_ref[h, rows, :] * jnp.exp(g_last - gc[h])
        s_ref[h] = jnp.exp(g_last) * s_ref[h] + _dot(k_end.T.astype(bf16), ub[h])
        zh = z_ref[rows, h * GDN_D : (h + 1) * GDN_D]
        og_ref[rows, h * GDN_D : (h + 1) * GDN_D] = (_rmsnorm(o[h], gn_ref[...]) * _silu(zh)).astype(bf16)


def _gdn_prompt_kernel(*refs, n_cast):
    qkv_ref, z_ref, ba_ref, cw_ref, av_ref, dv_ref, gn_ref = refs[:7]
    cast_in = refs[7 : 7 + n_cast]
    og_ref, so_ref = refs[7 + n_cast : 9 + n_cast]
    cast_out = refs[9 + n_cast : 9 + 2 * n_cast]
    s_ref, carry_ref, q_ref, k_ref, v_ref = refs[9 + 2 * n_cast :]
    for src, dst in zip(cast_in, cast_out):
        dst[...] = src[...].astype(bf16)

    c = pl.program_id(1)
    rows = qkv_ref.shape[0]
    n_chunks = rows // CHUNK

    @pl.when(c == 0)
    def _():
        s_ref[...] = jnp.zeros_like(s_ref)
        carry_ref[0:SUBLANES, :] = jnp.zeros((SUBLANES, CONV_CH), f32)

    carry_ref[SUBLANES : 2 * SUBLANES, :] = qkv_ref[0:SUBLANES, :]
    for j in range(3 * GDN_HEADS):
        sl = slice(j * LANES, (j + 1) * LANES)
        y = _conv_silu_slab(qkv_ref, carry_ref, cw_ref, sl)
        h = j % GDN_HEADS
        if j < 2 * GDN_HEADS:
            y = y * lax.rsqrt(jnp.sum(y * y, -1, keepdims=True) + L2_EPS)
            if j < GDN_HEADS:
                q_ref[h] = y * (GDN_D ** -0.5)
            else:
                k_ref[h] = y
        else:
            v_ref[h] = y
    carry_ref[0:SUBLANES, :] = qkv_ref[rows - SUBLANES : rows, :]

    ba = ba_ref[...]
    beta_t = jax.nn.sigmoid(ba)
    g_t = -jnp.exp(av_ref[...]) * _softplus(ba + dv_ref[...])
    row_in_chunk = lax.broadcasted_iota(jnp.int32, (rows, LANES), 0) % CHUNK
    gcum = g_t
    shift = 1
    while shift < CHUNK:
        gcum = gcum + jnp.where(row_in_chunk >= shift, pltpu.roll(gcum, shift, 0), 0.0)
        shift *= 2
    gcum_t = [gcum[ci * CHUNK : (ci + 1) * CHUNK, :].T for ci in range(n_chunks)]

    row = lax.broadcasted_iota(jnp.int32, (CHUNK, CHUNK), 0)
    col = lax.broadcasted_iota(jnp.int32, (CHUNK, CHUNK), 1)
    pairs = [(ci, h) for ci in range(n_chunks) for h in range(GDN_HEADS)]
    gc, gamma, sol, qkd = _delta_rule_prepare(pairs, gcum, gcum_t, beta_t, row, col, q_ref, k_ref, v_ref)
    for ci in range(n_chunks):
        mine = slice(ci * GDN_HEADS, (ci + 1) * GDN_HEADS)
        _delta_rule_apply(ci, (gc[mine], gamma[mine], sol[mine], qkd[mine]), q_ref, k_ref, z_ref, gn_ref, s_ref, og_ref)

    @pl.when(c == pl.num_programs(1) - 1)
    def _():
        so_ref[0] = s_ref[...]


def _cast_block(shape, steps):
    R, C = shape
    for col_blocks in (1, 2, 4, 8, 16):
        row_blocks = steps // col_blocks
        if (steps % col_blocks == 0 and R % row_blocks == 0 and C % col_blocks == 0
                and (R // row_blocks) % (2 * SUBLANES) == 0 and (C // col_blocks) % LANES == 0):
            return (R // row_blocks, C // col_blocks), col_blocks
    return None


def _gdn_prompt(proj, B, T, off_ba, cw, avec, dvec, gn, weights):
    rows = GDN_STEP_CHUNKS * CHUNK
    assert T % rows == 0
    ns = T // rows
    step = lambda b, c: b * ns + c
    plans = [_cast_block(w.shape, B * ns) for w in weights]
    riders = [w for w, p in zip(weights, plans) if p is not None]
    cast_specs = [pl.BlockSpec(blk, functools.partial(lambda b, c, cb: (step(b, c) // cb, step(b, c) % cb), cb=cb))
                  for blk, cb in (p for p in plans if p is not None)]
    head_scratch = pltpu.VMEM((GDN_HEADS, rows, GDN_D), f32)
    out = pl.pallas_call(
        functools.partial(_gdn_prompt_kernel, n_cast=len(riders)),
        grid=(B, ns),
        in_specs=[
            pl.BlockSpec((rows, CONV_CH), lambda b, c: (step(b, c), 0)),
            pl.BlockSpec((rows, GDN_W), lambda b, c: (step(b, c), OFF_Z // GDN_W)),
            pl.BlockSpec((rows, LANES), lambda b, c: (step(b, c), off_ba // LANES)),
            pl.BlockSpec((GDN_CONV, CONV_CH), lambda b, c: (0, 0)),
            pl.BlockSpec((1, LANES), lambda b, c: (0, 0)),
            pl.BlockSpec((1, LANES), lambda b, c: (0, 0)),
            pl.BlockSpec((1, GDN_D), lambda b, c: (0, 0)),
        ] + cast_specs,
        out_specs=[
            pl.BlockSpec((rows, GDN_W), lambda b, c: (step(b, c), 0)),
            pl.BlockSpec((1, GDN_HEADS, GDN_D, GDN_D), lambda b, c: (b, 0, 0, 0)),
        ] + cast_specs,
        out_shape=[
            jax.ShapeDtypeStruct((B * T, GDN_W), bf16),
            jax.ShapeDtypeStruct((B, GDN_HEADS, GDN_D, GDN_D), f32),
        ] + [jax.ShapeDtypeStruct(w.shape, bf16) for w in riders],
        scratch_shapes=[
            pltpu.VMEM((GDN_HEADS, GDN_D, GDN_D), f32),
            pltpu.VMEM((2 * SUBLANES, CONV_CH), f32),
            head_scratch, head_scratch, head_scratch,
        ],
        compiler_params=_cparams(("arbitrary", "arbitrary")),
        name="gdn_prompt",
    )(proj, proj, proj, cw, avec, dvec, gn, *riders)
    converted = iter(out[2:])
    return out[0], out[1], [next(converted) if p is not None else w.astype(bf16) for w, p in zip(weights, plans)]


def _gdn_step_kernel(qkv_ref, z_ref, ba_ref, sc_ref, s0_ref, cw_ref, av_ref, dv_ref, gn_ref, og_ref, so_ref):
    taps = cw_ref[...]
    seqs = range(qkv_ref.shape[0])
    y, z, beta_t, gamma_t = [], [], [], []
    for i in seqs:
        x_new = qkv_ref[i]
        conv = jnp.sum(sc_ref[i] * taps[0 : GDN_CONV - 1, :], axis=0, keepdims=True) + x_new * taps[GDN_CONV - 1 : GDN_CONV, :]
        y.append(_silu(conv))
        ba = ba_ref[i]
        beta_t.append(jax.nn.sigmoid(ba))
        gamma_t.append(jnp.exp(-jnp.exp(av_ref[...]) * _softplus(ba + dv_ref[...])))
        z.append(z_ref[i])
    pairs = [(i, h) for i in seqs for h in range(GDN_HEADS)]
    n = range(len(pairs))
    head = lambda i, base, h: y[i][:, base + h * GDN_D : base + (h + 1) * GDN_D]
    q = [head(i, 0, h) for i, h in pairs]
    k = [head(i, GDN_W, h) for i, h in pairs]
    v = [head(i, 2 * GDN_W, h) for i, h in pairs]
    q = [q[p] * lax.rsqrt(jnp.sum(q[p] * q[p], -1, keepdims=True) + L2_EPS) * (GDN_D ** -0.5) for p in n]
    k = [k[p] * lax.rsqrt(jnp.sum(k[p] * k[p], -1, keepdims=True) + L2_EPS) for p in n]
    beta = [beta_t[i][:, h : h + 1] for i, h in pairs]
    gamma = [gamma_t[i][:, GDN_HEADS + h : GDN_HEADS + h + 1] for i, h in pairs]
    k_col = [jnp.broadcast_to(k[p], (GDN_D, GDN_D)).T for p in n]
    sub = lax.broadcasted_iota(jnp.int32, (SUBLANES, GDN_D), 0)
    kq = [jnp.where(sub == 0, k[p], jnp.where(sub == 1, q[p], 0.0)).astype(bf16) for p in n]
    kq_s = [_dot(kq[p], s0_ref[i, h].astype(bf16)) for p, (i, h) in enumerate(pairs)]
    u = [beta[p] * v[p] - (beta[p] * gamma[p]) * kq_s[p][0:1] for p in n]
    o = [gamma[p] * kq_s[p][1:2] + jnp.sum(q[p] * k[p], -1, keepdims=True) * u[p] for p in n]
    for p, (i, h) in enumerate(pairs):
        so_ref[i, h] = gamma[p] * s0_ref[i, h] + k_col[p] * u[p]
        zh = z[i][:, h * GDN_D : (h + 1) * GDN_D]
        og_ref[i, :, h * GDN_D : (h + 1) * GDN_D] = (_rmsnorm(o[p], gn_ref[...]) * _silu(zh)).astype(bf16)


def _gdn_step(proj3, off_ba, state_conv, state_gdn, cw, avec, dvec, gn):
    B = proj3.shape[0]
    bb = math.gcd(B, GDN_STEP_BATCH)
    return pl.pallas_call(
        _gdn_step_kernel,
        grid=(B // bb,),
        in_specs=[
            pl.BlockSpec((bb, 1, CONV_CH), lambda b: (b, 0, 0)),
            pl.BlockSpec((bb, 1, GDN_W), lambda b: (b, 0, OFF_Z // GDN_W)),
            pl.BlockSpec((bb, 1, LANES), lambda b: (b, 0, off_ba // LANES)),
            pl.BlockSpec((bb, GDN_CONV - 1, CONV_CH), lambda b: (b, 0, 0)),
            pl.BlockSpec((bb, GDN_HEADS, GDN_D, GDN_D), lambda b: (b, 0, 0, 0)),
            pl.BlockSpec((GDN_CONV, CONV_CH), lambda b: (0, 0)),
            pl.BlockSpec((1, LANES), lambda b: (0, 0)),
            pl.BlockSpec((1, LANES), lambda b: (0, 0)),
            pl.BlockSpec((1, GDN_D), lambda b: (0, 0)),
        ],
        out_specs=[
            pl.BlockSpec((bb, 1, GDN_W), lambda b: (b, 0, 0)),
            pl.BlockSpec((bb, GDN_HEADS, GDN_D, GDN_D), lambda b: (b, 0, 0, 0)),
        ],
        out_shape=[
            jax.ShapeDtypeStruct((B, 1, GDN_W), bf16),
            jax.ShapeDtypeStruct((B, GDN_HEADS, GDN_D, GDN_D), f32),
        ],
        compiler_params=_cparams(("parallel",)),
        name="gdn_step",
    )(proj3, proj3, proj3, state_conv, state_gdn, cw, avec, dvec, gn)


def _rope(x, cos, sin_signed):
    width = x.shape[-1]
    lane = lax.broadcasted_iota(jnp.int32, x.shape, x.ndim - 1)
    first_half = (lane % SWA_HD) < (SWA_HD // 2)
    rot = jnp.where(first_half, pltpu.roll(x, width - SWA_HD // 2, x.ndim - 1), pltpu.roll(x, SWA_HD // 2, x.ndim - 1))
    return x * cos + rot * sin_signed


def _head_halves(x2, head_parity, lane):
    swapped = pltpu.roll(x2, SWA_HD, 1)
    lo_src, hi_src = (x2, swapped) if head_parity == 0 else (swapped, x2)
    return jnp.where(lane < SWA_HD, lo_src, 0.0), jnp.where(lane >= SWA_HD, hi_src, 0.0)


def _swa_prompt_kernel(sinks_ref, q_ref, kv_ref, cos_ref, sin_ref, os_ref, kc_ref, kprev_ref, vprev_ref):
    n = pl.program_id(1)
    n_blk = q_ref.shape[0] // WINDOW

    @pl.when(n == 0)
    def _():
        kprev_ref[...] = jnp.zeros_like(kprev_ref)
        vprev_ref[...] = jnp.zeros_like(vprev_ref)

    row = lax.broadcasted_iota(jnp.int32, (WINDOW, WINDOW), 0)
    col = lax.broadcasted_iota(jnp.int32, (WINDOW, WINDOW), 1)
    own = col <= row
    first_bias = jnp.where(n > 0, 0.0, -jnp.inf)
    lane = lax.broadcasted_iota(jnp.int32, (WINDOW, LANES), 1)
    scale = SWA_HD ** -0.5
    group = SWA_HEADS // SWA_KV_HEADS

    k_own, v_own, k_pre, v_pre, q2 = [], [], [], [], []
    for blk in range(n_blk):
        rows = slice(blk * WINDOW, (blk + 1) * WINDOW)
        cos = cos_ref[rows, :]
        sin = sin_ref[rows, :]
        k_cur = _rope(kv_ref[rows, :SWA_KV_W], cos, sin)
        v_cur = kv_ref[rows, SWA_KV_W:]
        if blk == n_blk - 1:
            kc_ref[0] = k_cur
        k_own.append([]), v_own.append([]), k_pre.append([]), v_pre.append([]), q2.append([])
        for h in range(SWA_KV_HEADS):
            pair = slice((h // 2) * LANES, (h // 2 + 1) * LANES)
            k_own[blk].append([a.astype(bf16) for a in _head_halves(k_cur[:, pair], h % 2, lane)])
            v_own[blk].append([a.astype(bf16) for a in _head_halves(v_cur[:, pair], h % 2, lane)])
            if blk == 0:
                k_pre[blk].append([kprev_ref[h, par] for par in range(2)])
                v_pre[blk].append([vprev_ref[h, par] for par in range(2)])
            else:
                k_pre[blk].append(k_own[blk - 1][h])
                v_pre[blk].append(v_own[blk - 1][h])
            if blk == n_blk - 1:
                for par in range(2):
                    kprev_ref[h, par] = k_own[blk][h][par]
                    vprev_ref[h, par] = v_own[blk][h][par]
            q_h = _rope(q_ref[rows, h * group * SWA_HD : (h + 1) * group * SWA_HD], cos, sin) * scale
            q2[blk].append([q_h[:, j * LANES : (j + 1) * LANES].astype(bf16) for j in range(group // 2)])

    heads = [(blk, h, j, par) for blk in range(n_blk) for h in range(SWA_KV_HEADS)
             for j in range(group // 2) for par in range(2)]
    idx = range(len(heads))
    sink = [sinks_ref[h * group + 2 * j + par] for blk, h, j, par in heads]
    s = [jnp.where(own, _dot_nt(q2[blk][h][j], k_own[blk][h][par]),
                   _dot_nt(q2[blk][h][j], k_pre[blk][h][par]) + (first_bias if blk == 0 else 0.0))
         for blk, h, j, par in heads]
    m = [jnp.maximum(jnp.max(s[i], -1, keepdims=True), sink[i]) for i in idx]
    p = [jnp.exp(s[i] - m[i]) for i in idx]
    inv = [1.0 / (jnp.sum(p[i], -1, keepdims=True) + jnp.exp(sink[i] - m[i])) for i in idx]
    o = [(_dot(jnp.where(own, p[i], 0.0).astype(bf16), v_own[blk][h][par])
          + _dot(jnp.where(own, 0.0, p[i]).astype(bf16), v_pre[blk][h][par])) * inv[i]
         for i, (blk, h, j, par) in enumerate(heads)]
    for i in range(0, len(heads), 2):
        blk, head = divmod(i, SWA_HEADS)
        os_ref[blk * WINDOW : (blk + 1) * WINDOW, head * SWA_HD : head * SWA_HD + LANES] = (o[i] + o[i + 1]).astype(bf16)


def _swa_prompt(proj, B, T, off_q, off_kv, cos, sin, sinks):
    assert T % WINDOW == 0 and SWA_KV_W == 2 * LANES
    rows = math.gcd(SWA_STEP_BLOCKS, T // WINDOW) * WINDOW
    nb = T // rows
    return pl.pallas_call(
        _swa_prompt_kernel,
        grid=(B, nb),
        in_specs=[
            pl.BlockSpec(memory_space=pltpu.SMEM),
            pl.BlockSpec((rows, SWA_Q_W), lambda b, n: (b * nb + n, off_q // SWA_Q_W)),
            pl.BlockSpec((rows, 2 * SWA_KV_W), lambda b, n: (b * nb + n, off_kv // (2 * SWA_KV_W))),
            pl.BlockSpec((rows, SWA_KV_W), lambda b, n: (n, 0)),
            pl.BlockSpec((rows, SWA_KV_W), lambda b, n: (n, 0)),
        ],
        out_specs=[
            pl.BlockSpec((rows, SWA_Q_W), lambda b, n: (b * nb + n, 0)),
            pl.BlockSpec((1, WINDOW, SWA_KV_W), lambda b, n: (b, 0, 0)),
        ],
        out_shape=[
            jax.ShapeDtypeStruct((B * T, SWA_Q_W), bf16),
            jax.ShapeDtypeStruct((B, WINDOW, SWA_KV_W), f32),
        ],
        scratch_shapes=[pltpu.VMEM((SWA_KV_HEADS, 2, WINDOW, LANES), bf16)] * 2,
        compiler_params=_cparams(("parallel", "arbitrary")),
        name="swa_prompt",
    )(sinks, proj, proj, cos, sin)


def _swa_step_kernel(qe_ref, kv_ref, ck_ref, cv_ref, cos_ref, sin_ref, sinks_ref, r_ref, nk_ref, nv_ref):
    cos = cos_ref[...]
    sin = sin_ref[...]
    sink = sinks_ref[...]
    row = lax.broadcasted_iota(jnp.int32, (WINDOW, SWA_KV_W), 0)
    scale = SWA_HD ** -0.5
    seqs = range(qe_ref.shape[0])
    kv = [kv_ref[i] for i in seqs]
    k_new = [_rope(kv[i][:, :SWA_KV_W], cos, sin) for i in seqs]
    keys = [jnp.where(row == WINDOW - 1, k_new[i], pltpu.roll(ck_ref[i].T, WINDOW - 1, 0)) for i in seqs]
    vals = [jnp.where(row == WINDOW - 1, kv[i][:, SWA_KV_W:], pltpu.roll(cv_ref[i].T, WINDOW - 1, 0)) for i in seqs]
    for i in seqs:
        nk_ref[i] = keys[i].T
        nv_ref[i] = vals[i].T
    q = [_rope(qe_ref[i], cos, sin) for i in seqs]
    s = [_dot_nt(q[i].astype(bf16), keys[i].astype(bf16)) * scale for i in seqs]
    m = [jnp.maximum(jnp.max(s[i], -1, keepdims=True), sink) for i in seqs]
    p = [jnp.exp(s[i] - m[i]) for i in seqs]
    denom = [jnp.sum(p[i], -1, keepdims=True) + jnp.exp(sink - m[i]) for i in seqs]
    for i in seqs:
        r_ref[i] = _dot((p[i] / denom[i]).astype(bf16), vals[i].astype(bf16))


def _swa_step(q_exp, proj3, off_kv, cache_k, cache_v, cos, sin, sinks_col):
    B = q_exp.shape[0]
    bb = math.gcd(B, SWA_STEP_BATCH)
    assert cache_k.shape[2] == WINDOW
    return pl.pallas_call(
        _swa_step_kernel,
        grid=(B // bb,),
        in_specs=[
            pl.BlockSpec((bb, SWA_HEADS, SWA_KV_W), lambda i: (i, 0, 0)),
            pl.BlockSpec((bb, 1, 2 * SWA_KV_W), lambda i: (i, 0, off_kv // (2 * SWA_KV_W))),
            pl.BlockSpec((bb, SWA_KV_W, WINDOW), lambda i: (i, 0, 0)),
            pl.BlockSpec((bb, SWA_KV_W, WINDOW), lambda i: (i, 0, 0)),
            pl.BlockSpec((1, SWA_KV_W), lambda i: (0, 0)),
            pl.BlockSpec((1, SWA_KV_W), lambda i: (0, 0)),
            pl.BlockSpec((SWA_HEADS, 1), lambda i: (0, 0)),
        ],
        out_specs=[
            pl.BlockSpec((bb, SWA_HEADS, SWA_KV_W), lambda i: (i, 0, 0)),
            pl.BlockSpec((bb, SWA_KV_W, WINDOW), lambda i: (i, 0, 0)),
            pl.BlockSpec((bb, SWA_KV_W, WINDOW), lambda i: (i, 0, 0)),
        ],
        out_shape=[
            jax.ShapeDtypeStruct((B, SWA_HEADS, SWA_KV_W), f32),
            jax.ShapeDtypeStruct((B, SWA_KV_W, WINDOW), f32),
            jax.ShapeDtypeStruct((B, SWA_KV_W, WINDOW), f32),
        ],
        compiler_params=_cparams(("parallel",)),
        name="swa_step",
    )(q_exp, proj3, cache_k, cache_v, cos, sin, sinks_col)


def _merge_core(og_ref, os_ref, gg_ref, gs_ref, x_ref, wg_ref, ws_ref, wo_ref, gq_ref, wq_ref):
    p_gdn = _dot(og_ref[...], wg_ref[...])
    p_swa = _dot(os_ref[...], ws_ref[...])
    merged = jax.nn.sigmoid(gg_ref[...]) * p_gdn + jax.nn.sigmoid(gs_ref[...]) * p_swa
    x_new = x_ref[...] + _dot(merged.astype(bf16), wo_ref[...])
    return x_new, _dot(_rmsnorm(x_new, gq_ref[...]).astype(bf16), wq_ref[...])


def _merge_kernel(*refs):
    xo_ref, qm_ref = refs[-2:]
    xo_ref[...], qm_ref[...] = _merge_core(*refs[:-2])


def _merge_mem_kernel(*refs):
    mk_ref, mv_ref, wmo_ref, xo_ref = refs[-4:]
    x_new, q = _merge_core(*refs[:-4])
    q = q.astype(bf16)
    scale = MEM_HD ** -0.5
    heads = range(MEM_HEADS)
    cols = lambda h: slice(h * MEM_HD, (h + 1) * MEM_HD)
    s = [_dot_nt(q[:, cols(h)], mk_ref[0, :, cols(h)]) * scale for h in heads]
    p = [jnp.exp(s[h] - jnp.max(s[h], -1, keepdims=True)) for h in heads]
    p = [p[h] / jnp.sum(p[h], -1, keepdims=True) for h in heads]
    o = [_dot(p[h].astype(bf16), mv_ref[0, :, cols(h)]) for h in heads]
    xo_ref[...] = x_new + _dot(jnp.concatenate(o, axis=-1).astype(bf16), wmo_ref[...])


def _merge(og, os_, proj, off_gg, x, wg, ws, wo, gq, wq, mem=None):
    M, D = x.shape
    tm = min(MERGE_TM, M)
    assert M % tm == 0 and off_gg % D == 0
    const = lambda shape: pl.BlockSpec(shape, lambda i: (0, 0), pipeline_mode=pl.Buffered(1))
    in_specs = [
        pl.BlockSpec((tm, GDN_W), lambda i: (i, 0)),
        pl.BlockSpec((tm, SWA_Q_W), lambda i: (i, 0)),
        pl.BlockSpec((tm, D), lambda i: (i, off_gg // D)),
        pl.BlockSpec((tm, D), lambda i: (i, off_gg // D + 1)),
        pl.BlockSpec((tm, D), lambda i: (i, 0)),
        const((GDN_W, D)),
        const((SWA_Q_W, D)),
        const((D, D)),
        const((1, D)),
        const((D, MEM_W)),
    ]
    args = [og, os_, proj, proj, x, wg, ws, wo, gq, wq]
    x_spec = pl.BlockSpec((tm, D), lambda i: (i, 0))
    x_shape = jax.ShapeDtypeStruct((M, D), f32)
    if mem is None:
        return pl.pallas_call(
            _merge_kernel,
            grid=(M // tm,),
            in_specs=in_specs,
            out_specs=[x_spec, pl.BlockSpec((tm, MEM_W), lambda i: (i, 0))],
            out_shape=[x_shape, jax.ShapeDtypeStruct((M, MEM_W), f32)],
            compiler_params=_cparams(("parallel",)),
            name="merge",
        )(*args)
    mem_k, mem_v, wmo, seq_rows = mem
    assert seq_rows % tm == 0
    mt = mem_k.shape[1]
    mem_spec = pl.BlockSpec((1, mt, MEM_W), lambda i: (i // (seq_rows // tm), 0, 0))
    return pl.pallas_call(
        _merge_mem_kernel,
        grid=(M // tm,),
        in_specs=in_specs + [mem_spec, mem_spec, const((MEM_W, D))],
        out_specs=x_spec,
        out_shape=x_shape,
        compiler_params=_cparams(("parallel",)),
        name="merge_mem",
    )(*args, mem_k, mem_v, wmo)


def _mem_attn_step_kernel(q_ref, k_ref, v_ref, o_ref):
    mt2 = k_ref.shape[1] // SUBLANES
    seqs = range(q_ref.shape[0])
    q8 = [q_ref[i] * (MEM_HD ** -0.5) for i in seqs]
    s = [jnp.sum(k_ref[i].reshape(mt2, SUBLANES, MEM_HD) * q8[i], axis=-1, keepdims=True) for i in seqs]
    m = [jnp.max(s[i], axis=0) for i in seqs]
    m = [jnp.maximum(m[i], pltpu.roll(m[i], MEM_HEADS, 0)) for i in seqs]
    p = [jnp.exp(s[i] - m[i]) for i in seqs]
    l = [jnp.sum(p[i], axis=0) for i in seqs]
    l = [l[i] + pltpu.roll(l[i], MEM_HEADS, 0) for i in seqs]
    o = [jnp.sum(p[i] * v_ref[i].reshape(mt2, SUBLANES, MEM_HD), axis=0) for i in seqs]
    for i in seqs:
        o_ref[i] = (o[i] + pltpu.roll(o[i], MEM_HEADS, 0)) / l[i]


def _mem_attn_step(q8, mem_k, mem_v):
    B, rows, _ = mem_k.shape
    assert 2 * MEM_HEADS == SUBLANES and rows % SUBLANES == 0
    bb = math.gcd(B, MEM_STEP_BATCH)
    return pl.pallas_call(
        _mem_attn_step_kernel,
        grid=(B // bb,),
        in_specs=[
            pl.BlockSpec((bb, SUBLANES, MEM_HD), lambda i: (i, 0, 0)),
            pl.BlockSpec((bb, rows, MEM_HD), lambda i: (i, 0, 0)),
            pl.BlockSpec((bb, rows, MEM_HD), lambda i: (i, 0, 0)),
        ],
        out_specs=pl.BlockSpec((bb, SUBLANES, MEM_HD), lambda i: (i, 0, 0)),
        out_shape=jax.ShapeDtypeStruct((B, SUBLANES, MEM_HD), f32),
        compiler_params=_cparams(("parallel",)),
        name="mem_attn_step",
    )(q8, mem_k, mem_v)


def _proj_residual_kernel(a_ref, w_ref, x_ref, o_ref):
    o_ref[...] = x_ref[...] + _dot(a_ref[...].astype(bf16), w_ref[...])


def _proj_residual(a, w, x):
    M, D = x.shape
    return pl.pallas_call(
        _proj_residual_kernel,
        out_shape=jax.ShapeDtypeStruct((M, D), f32),
        compiler_params=pltpu.CompilerParams(vmem_limit_bytes=VMEM_LIMIT),
        name="proj_residual",
    )(a, w, x)


def _reorder_rows_kernel(w_ref, o_ref, *, pieces):
    at = 0
    for lo, hi in pieces:
        o_ref[at : at + hi - lo, :] = w_ref[lo:hi, :].astype(bf16)
        at += hi - lo
    o_ref[at:, :] = jnp.zeros((o_ref.shape[0] - at, o_ref.shape[1]), bf16)


def _reorder_rows(wt, layer, pieces, height):
    _, d_in, D = wt.shape
    assert all(lo % (2 * SUBLANES) == 0 and hi % (2 * SUBLANES) == 0 for lo, hi in pieces) and D % LANES == 0
    return pl.pallas_call(
        functools.partial(_reorder_rows_kernel, pieces=pieces),
        grid=(D // LANES,),
        in_specs=[pl.BlockSpec((None, d_in, LANES), lambda i: (layer, 0, i))],
        out_specs=pl.BlockSpec((height, LANES), lambda i: (0, i)),
        out_shape=jax.ShapeDtypeStruct((height, D), bf16),
        compiler_params=_cparams(("parallel",)),
        name="reorder_rows",
    )(wt)


def _rope_tables(pos):
    half = SWA_HD // 2
    inv_freq = ROPE_THETA ** (-jnp.arange(half, dtype=f32) / half)
    ang = pos.astype(f32)[:, None] * inv_freq[None, :]
    cos = jnp.cos(ang)
    sin = jnp.sin(ang)
    reps = SWA_KV_W // SWA_HD
    return jnp.tile(jnp.concatenate([cos, cos], -1), (1, reps)), jnp.tile(jnp.concatenate([-sin, sin], -1), (1, reps))


def kernel(x_prompt, x_sample, state_gdn, state_conv, cache_swa_k, cache_swa_v, cache_mem_k, cache_mem_v, mem_prompt, norm_ffn1, ffn1_w1, ffn1_w3, ffn1_w2, norm_mix, w_in, conv_w, gdn_A_log, gdn_dt_bias, gdn_norm, swa_sinks, w_br_gdn, w_br_swa, w_out, norm_mem_q, norm_mem_kv, w_mem_q, w_mem_k, w_mem_v, w_mem_o, norm_ffn2, ffn2_w1, ffn2_w3, ffn2_w2, norm_final):
    Bp, Tp, D = x_prompt.shape
    Bs, Ts, _ = x_sample.shape
    assert Ts == 1
    depth = norm_ffn1.shape[0]
    n_mem = mem_prompt.shape[1]
    group = SWA_HEADS // SWA_KV_HEADS

    off_gs = OFF_GG + D
    off_q = off_gs + D
    off_kv = off_q + SWA_Q_W
    off_ba = off_kv + 2 * SWA_KV_W
    d_in_pad = -(-(off_ba + LANES) // PROJ_TN) * PROJ_TN
    o_b = CONV_CH + GDN_W
    o_q = o_b + 2 * GDN_HEADS
    o_gg = o_q + SWA_Q_W + 2 * SWA_KV_W

    cos_p, sin_p = _rope_tables(jnp.arange(Tp, dtype=jnp.int32))
    cos_s, sin_s = _rope_tables(PAST_LEN + jnp.arange(Ts, dtype=jnp.int32))
    eye_kv = jnp.eye(SWA_KV_HEADS, dtype=f32)
    row = lambda v: v.reshape(1, -1)

    hp = x_prompt.reshape(Bp * Tp, D)
    hs = x_sample.reshape(Bs, D)
    outs = [[] for _ in range(10)]
    for l in range(depth):
        w_in_rt = _reorder_rows(jnp.swapaxes(w_in, 1, 2), l, ((0, o_b), (o_gg, w_in.shape[2]), (o_q, o_gg), (o_b, o_q)), d_in_pad)
        ffn1 = (row(norm_ffn1[l]), ffn1_w1[l].astype(bf16), ffn1_w3[l].astype(bf16), ffn1_w2[l].astype(bf16))
        last = l == depth - 1
        gfin = row(norm_final)
        avec = jnp.zeros((1, LANES), f32).at[0, GDN_HEADS : 2 * GDN_HEADS].set(gdn_A_log[l])
        dvec = jnp.zeros((1, LANES), f32).at[0, GDN_HEADS : 2 * GDN_HEADS].set(gdn_dt_bias[l])
        gdn_common = (conv_w[l], avec, dvec, row(gdn_norm[l]))

        x1 = _ffn(hp, *ffn1, gfin, False)
        proj = _norm_matmul(x1, row(norm_mix[l]), w_in_rt, PROJ_TN, w_transposed=True)
        later = (ffn2_w1[l], ffn2_w3[l], ffn2_w2[l], w_br_gdn[l], w_br_swa[l], w_out[l], w_mem_q[l],
                 w_mem_k[l], w_mem_v[l], w_mem_o[l])
        og, s_new, later = _gdn_prompt(proj, Bp, Tp, off_ba, *gdn_common, later)
        ffn2 = (row(norm_ffn2[l]), *later[0:3])
        merge_w = (*later[3:6], row(norm_mem_q[l]), later[6])
        wmk, wmv, wmo = later[7:10]
        os_, kc = _swa_prompt(proj, Bp, Tp, off_q, off_kv, cos_p, sin_p, swa_sinks[l])
        mem_x = mem_prompt.reshape(Bp * n_mem, D)
        mk, mv, mkb, mvb = _mem_kv(mem_x, row(norm_mem_kv[l]), wmk, wmv)
        mem = (mkb.reshape(Bp, n_mem, MEM_W), mvb.reshape(Bp, n_mem, MEM_W), wmo, Tp)
        x3 = _merge(og, os_, proj, OFF_GG, x1, *merge_w, mem=mem)
        hp = _ffn(x3, *ffn2, gfin, last)
        proj_b = proj.reshape(Bp, Tp, d_in_pad)
        outs[0].append(s_new)
        outs[1].append(proj_b[:, Tp - (GDN_CONV - 1) :, :CONV_CH])
        outs[2].append(kc.reshape(Bp, WINDOW, SWA_KV_HEADS, SWA_HD))
        outs[3].append(proj_b[:, Tp - WINDOW :, off_kv + SWA_KV_W : off_kv + 2 * SWA_KV_W].reshape(Bp, WINDOW, SWA_KV_HEADS, SWA_HD))
        outs[4].append(mk.reshape(Bp, n_mem, MEM_HEADS, MEM_HD))
        outs[5].append(mv.reshape(Bp, n_mem, MEM_HEADS, MEM_HD))

        x1 = _ffn(hs, *ffn1, gfin, False)
        proj = _norm_matmul(x1, row(norm_mix[l]), w_in_rt, PROJ_TN, w_transposed=True)
        proj3 = proj.reshape(Bs, 1, d_in_pad)
        og, s_new = _gdn_step(proj3, off_ba, state_conv[l], state_gdn[l], *gdn_common)
        q_raw = proj[:, off_q : off_q + SWA_Q_W].reshape(Bs, SWA_KV_HEADS, group, 1, SWA_HD)
        q_exp = (q_raw * eye_kv[None, :, None, :, None]).reshape(Bs, SWA_HEADS, SWA_KV_W)
        ck = jnp.swapaxes(cache_swa_k[l].reshape(Bs, WINDOW, SWA_KV_W), 1, 2)
        cv = jnp.swapaxes(cache_swa_v[l].reshape(Bs, WINDOW, SWA_KV_W), 1, 2)
        r, nk, nv = _swa_step(q_exp, proj3, off_kv, ck, cv, cos_s, sin_s, swa_sinks[l].reshape(SWA_HEADS, 1))
        r5 = r.reshape(Bs, SWA_KV_HEADS, group, SWA_KV_HEADS, SWA_HD)
        kvh = jnp.arange(SWA_KV_HEADS)
        os_ = jnp.transpose(r5[:, kvh, :, kvh, :], (1, 0, 2, 3)).reshape(Bs, SWA_Q_W).astype(bf16)
        x2, qm = _merge(og.reshape(Bs, GDN_W), os_, proj, OFF_GG, x1, *merge_w)
        q8 = jnp.tile(qm.reshape(Bs, MEM_HEADS, MEM_HD), (1, 2, 1))
        om = _mem_attn_step(q8, cache_mem_k[l].reshape(Bs, n_mem * MEM_HEADS, MEM_HD),
                            cache_mem_v[l].reshape(Bs, n_mem * MEM_HEADS, MEM_HD))
        x3 = _proj_residual(om[:, :MEM_HEADS].reshape(Bs, MEM_W), wmo, x2)
        hs = _ffn(x3, *ffn2, gfin, last)
        outs[6].append(s_new)
        outs[7].append(jnp.concatenate([state_conv[l][:, 1:], proj3[:, :, :CONV_CH]], axis=1))
        outs[8].append(jnp.swapaxes(nk, 1, 2).reshape(Bs, WINDOW, SWA_KV_HEADS, SWA_HD))
        outs[9].append(jnp.swapaxes(nv, 1, 2).reshape(Bs, WINDOW, SWA_KV_HEADS, SWA_HD))

    return (hp.reshape(Bp, Tp, D), hs.reshape(Bs, Ts, D), *(jnp.stack(o) for o in outs))
```

```python
import functools
import math

import jax
import jax.numpy as jnp
from jax import lax
from jax.experimental import pallas as pl
from jax.experimental.pallas import tpu as pltpu

f32 = jnp.float32
bf16 = jnp.bfloat16

PAST_LEN = 16384
GDN_HEADS = 8
GDN_D = 128
GDN_CONV = 4
SWA_HEADS = 16
SWA_KV_HEADS = 4
SWA_HD = 64
WINDOW = 128
ROPE_THETA = 10000.0
MEM_HEADS = 4
MEM_HD = 128
EPS = 1e-6
L2_EPS = 1e-6

LANES = 128
SUBLANES = 8
VMEM_LIMIT = 60 * 1024 * 1024

FFN_TM = 1024
FFN_TF = 512
FFN_SLAB = 64
PROJ_TM = 1024
PROJ_TN = 768
MERGE_TM = 256
CHUNK = 128
GDN_STEP_CHUNKS = 2
SWA_STEP_BLOCKS = 4
GDN_STEP_BATCH = 8
SWA_STEP_BATCH = 8
MEM_STEP_BATCH = 8

GDN_W = GDN_HEADS * GDN_D
CONV_CH = 3 * GDN_W
SWA_Q_W = SWA_HEADS * SWA_HD
SWA_KV_W = SWA_KV_HEADS * SWA_HD
MEM_W = MEM_HEADS * MEM_HD

OFF_Z = CONV_CH
OFF_GG = OFF_Z + GDN_W


def _cparams(semantics):
    return pltpu.CompilerParams(dimension_semantics=semantics, vmem_limit_bytes=VMEM_LIMIT)


def _dot(a, b):
    return jnp.dot(a, b, preferred_element_type=f32)


def _dot_nt(a, b):
    return lax.dot_general(a, b, (((1,), (1,)), ((), ())), preferred_element_type=f32)


def _rmsnorm(x, g):
    return x * lax.rsqrt(jnp.mean(x * x, -1, keepdims=True) + EPS) * g


def _silu(x):
    return x * jax.nn.sigmoid(x)


def _softplus(x):
    return jnp.maximum(x, 0.0) + jnp.log1p(jnp.exp(-jnp.abs(x)))


def _ffn_kernel(x_ref, g_ref, w1_ref, w3_ref, w2_ref, gf_ref, o_ref, h_ref, *, final_norm):
    j = pl.program_id(1)
    slab = min(FFN_SLAB, x_ref.shape[0])
    n_slabs = x_ref.shape[0] // slab

    def over_slabs(body):
        def step(s, carry):
            body(pl.ds(pl.multiple_of(s * slab, slab), slab))
            return carry
        lax.fori_loop(0, n_slabs, step, 0, unroll=min(4, n_slabs))

    @pl.when(j == 0)
    def _():
        def prologue(rows):
            h_ref[rows, :] = _rmsnorm(x_ref[rows, :], g_ref[...]).astype(bf16)
            o_ref[rows, :] = jnp.zeros((slab, o_ref.shape[1]), f32)
        over_slabs(prologue)

    h = h_ref[...]
    half = w1_ref.shape[1] // 2
    down = []
    for cols in (slice(0, half), slice(half, 2 * half)):
        a = _dot(h, w1_ref[:, cols])
        b = _dot(h, w3_ref[:, cols])
        down.append(_dot((_silu(a) * b).astype(bf16), w2_ref[cols, :]))
    o_ref[...] += down[0] + down[1]

    @pl.when(j == pl.num_programs(1) - 1)
    def _():
        def epilogue(rows):
            y = x_ref[rows, :] + 0.5 * o_ref[rows, :]
            if final_norm:
                y = _rmsnorm(y, gf_ref[...])
            o_ref[rows, :] = y
        over_slabs(epilogue)


def _ffn(x, g, w1, w3, w2, gf, final_norm):
    M, D = x.shape
    F = w1.shape[1]
    tm = min(FFN_TM, M)
    tf = FFN_TF
    assert M % tm == 0 and F % tf == 0
    return pl.pallas_call(
        functools.partial(_ffn_kernel, final_norm=final_norm),
        grid=(M // tm, F // tf),
        in_specs=[
            pl.BlockSpec((tm, D), lambda i, j: (i, 0)),
            pl.BlockSpec((1, D), lambda i, j: (0, 0)),
            pl.BlockSpec((D, tf), lambda i, j: (0, j)),
            pl.BlockSpec((D, tf), lambda i, j: (0, j)),
            pl.BlockSpec((tf, D), lambda i, j: (j, 0)),
            pl.BlockSpec((1, D), lambda i, j: (0, 0)),
        ],
        out_specs=pl.BlockSpec((tm, D), lambda i, j: (i, 0)),
        out_shape=jax.ShapeDtypeStruct((M, D), f32),
        scratch_shapes=[pltpu.VMEM((tm, D), bf16)],
        compiler_params=_cparams(("parallel", "arbitrary")),
        name="ffn",
    )(x, g, w1, w3, w2, gf)


def _norm_matmul_kernel(x_ref, g_ref, w_ref, o_ref, h_ref, *, w_transposed):
    @pl.when(pl.program_id(1) == 0)
    def _():
        h_ref[...] = _rmsnorm(x_ref[...], g_ref[...]).astype(bf16)

    o_ref[...] = (_dot_nt if w_transposed else _dot)(h_ref[...], w_ref[...])


def _norm_matmul(x, g, w, tn, w_transposed=False):
    M, D = x.shape
    N = w.shape[0] if w_transposed else w.shape[1]
    tm = min(PROJ_TM, M)
    assert M % tm == 0 and N % tn == 0
    w_spec = pl.BlockSpec((tn, D), lambda i, j: (j, 0)) if w_transposed else pl.BlockSpec((D, tn), lambda i, j: (0, j))
    return pl.pallas_call(
        functools.partial(_norm_matmul_kernel, w_transposed=w_transposed),
        grid=(M // tm, N // tn),
        in_specs=[
            pl.BlockSpec((tm, D), lambda i, j: (i, 0)),
            pl.BlockSpec((1, D), lambda i, j: (0, 0)),
            w_spec,
        ],
        out_specs=pl.BlockSpec((tm, tn), lambda i, j: (i, j)),
        out_shape=jax.ShapeDtypeStruct((M, N), f32),
        scratch_shapes=[pltpu.VMEM((tm, D), bf16)],
        compiler_params=_cparams(("parallel", "arbitrary")),
        name="norm_matmul",
    )(x, g, w)


def _norm_matmul_nt_lookahead(x, g, wt, tn):
    M, D = x.shape
    N = wt.shape[0]
    tm = min(PROJ_TM, M)
    assert M % tm == 0 and N % tn == 0

    def body(indices, x_ref, g_ref, w_ref, o_ref, h_ref):
        @pl.when(indices[1] == 0)
        def _():
            h_ref[...] = _rmsnorm(x_ref[...], g_ref[...]).astype(bf16)

        o_ref[...] = _dot_nt(h_ref[...], w_ref[...])

    def outer(x_hbm, g_hbm, w_hbm, o_hbm, h_ref):
        pltpu.emit_pipeline(
            body,
            grid=(M // tm, N // tn),
            in_specs=[
                pl.BlockSpec((tm, D), lambda i, j: (i, 0), pipeline_mode=pl.Buffered(2, use_lookahead=True)),
                pl.BlockSpec((1, D), lambda i, j: (0, 0)),
                pl.BlockSpec((tn, D), lambda i, j: (j, 0)),
            ],
            out_specs=[pl.BlockSpec((tm, tn), lambda i, j: (i, j))],
            _explicit_indices=True,
        )(x_hbm, g_hbm, w_hbm, o_hbm, scratches=(h_ref,))

    any_spec = pl.BlockSpec(memory_space=pl.ANY)
    return pl.pallas_call(
        outer,
        in_specs=[any_spec, any_spec, any_spec],
        out_specs=any_spec,
        out_shape=jax.ShapeDtypeStruct((M, N), f32),
        scratch_shapes=[pltpu.VMEM((tm, D), bf16)],
        compiler_params=pltpu.CompilerParams(vmem_limit_bytes=VMEM_LIMIT),
        name="norm_matmul_lookahead",
    )(x, g, wt)


def _mem_kv_kernel(x_ref, g_ref, wk_ref, wv_ref, k_ref, v_ref, kb_ref, vb_ref):
    h = _rmsnorm(x_ref[...], g_ref[...]).astype(bf16)
    k = _dot(h, wk_ref[...])
    v = _dot(h, wv_ref[...])
    kb_ref[...] = k.astype(bf16)
    vb_ref[...] = v.astype(bf16)
    tokens = x_ref.shape[0]
    for hd in range(MEM_HEADS):
        k_ref[pl.ds(hd, tokens, stride=MEM_HEADS), :] = k[:, hd * MEM_HD : (hd + 1) * MEM_HD]
        v_ref[pl.ds(hd, tokens, stride=MEM_HEADS), :] = v[:, hd * MEM_HD : (hd + 1) * MEM_HD]


def _mem_kv(x, g, wk, wv):
    M, D = x.shape
    N = wk.shape[1]
    tm = min(PROJ_TM, M)
    assert M % tm == 0 and N == MEM_W
    row_spec = pl.BlockSpec((tm, N), lambda i: (i, 0))
    head_row_spec = pl.BlockSpec((tm * MEM_HEADS, MEM_HD), lambda i: (i, 0))
    const = lambda shape: pl.BlockSpec(shape, lambda i: (0, 0))
    return pl.pallas_call(
        _mem_kv_kernel,
        grid=(M // tm,),
        in_specs=[pl.BlockSpec((tm, D), lambda i: (i, 0)), const((1, D)), const((D, N)), const((D, N))],
        out_specs=[head_row_spec] * 2 + [row_spec] * 2,
        out_shape=[jax.ShapeDtypeStruct((M * MEM_HEADS, MEM_HD), f32)] * 2 + [jax.ShapeDtypeStruct((M, N), bf16)] * 2,
        compiler_params=_cparams(("parallel",)),
        name="mem_kv",
    )(x, g, wk, wv)


def _bdot(a, b):
    return _dot(a.astype(bf16), b.astype(bf16))


def _unit_lower_inverse(nmats, row, col):
    eye = jnp.where(row == col, 1.0, 0.0).astype(f32)

    def same_block(size):
        return (row // size) == (col // size)

    blk = same_block(SUBLANES)
    a1 = [jnp.where(blk, n, 0.0) for n in nmats]
    a2 = [_bdot(a, a) for a in a1]
    a4 = [_bdot(a, a) for a in a2]
    ts = [_bdot(eye - x1, eye + x2) for x1, x2 in zip(a1, a2)]
    ts = [_bdot(t, eye + x4) for t, x4 in zip(ts, a4)]
    size = SUBLANES
    while size < CHUNK:
        nxt = same_block(2 * size)
        sel = jnp.logical_and(nxt, jnp.logical_not(blk))
        tbs = [t.astype(bf16) for t in ts]
        xs = [_dot(jnp.where(sel, n, 0.0).astype(bf16), tb) for n, tb in zip(nmats, tbs)]
        ts = [t - _dot(tb, x.astype(bf16)) for t, tb, x in zip(ts, tbs, xs)]
        blk = nxt
        size *= 2
    return ts


def _conv_silu_slab(x_ref, top_ref, w_ref, sl):
    rows = x_ref.shape[0]
    y = None
    for tap in range(GDN_CONV):
        back = GDN_CONV - 1 - tap
        window = jnp.concatenate(
            [top_ref[SUBLANES - back : 2 * SUBLANES - back, sl], x_ref[SUBLANES - back : rows - back, sl]], axis=0)
        term = window * w_ref[tap : tap + 1, sl]
        y = term if y is None else y + term
    return _silu(y)


def _delta_rule_prepare(pairs, gcum, gcum_t, beta_t, row, col, q_ref, k_ref, v_ref):
    n = range(len(pairs))
    incl = row >= col
    strict = row > col
    rows_of = lambda ci: slice(ci * CHUNK, (ci + 1) * CHUNK)
    gc = [gcum[rows_of(ci), GDN_HEADS + h : GDN_HEADS + h + 1] for ci, h in pairs]
    gr = [gcum_t[ci][GDN_HEADS + h : GDN_HEADS + h + 1, :] for ci, h in pairs]
    beta = [beta_t[rows_of(ci), h : h + 1] for ci, h in pairs]
    k = [k_ref[h, rows_of(ci), :] for ci, h in pairs]
    kb = [x.astype(bf16) for x in k]
    kk = [_dot_nt(kb[i], kb[i]) for i in n]
    qk = [_dot_nt(q_ref[h, rows_of(ci), :].astype(bf16), kb[i]) for i, (ci, h) in enumerate(pairs)]
    e = [jnp.exp(jnp.where(incl, gc[i] - gr[i], 0.0)) for i in n]
    nmat = [beta[i] * kk[i] * jnp.where(strict, e[i], 0.0) for i in n]
    qkd = [(qk[i] * jnp.where(incl, e[i], 0.0)).astype(bf16) for i in n]
    tinv = _unit_lower_inverse(nmat, row, col)
    gamma = [jnp.exp(gc[i]) for i in n]
    rhs = [jnp.concatenate([beta[i] * v_ref[h, rows_of(ci), :], (beta[i] * gamma[i]) * k[i]], axis=-1).astype(bf16)
           for i, (ci, h) in enumerate(pairs)]
    sol = [_dot(tinv[i].astype(bf16), rhs[i]) for i in n]
    return gc, gamma, sol, qkd


def _delta_rule_apply(ci, prepared, q_ref, k_ref, z_ref, gn_ref, s_ref, og_ref):
    gc, gamma, sol, qkd = prepared
    heads = range(GDN_HEADS)
    rows = slice(ci * CHUNK, (ci + 1) * CHUNK)
    sb = [s_ref[h].astype(bf16) for h in heads]
    ub = [(sol[h][:, :GDN_D] - _dot(sol[h][:, GDN_D:].astype(bf16), sb[h])).astype(bf16) for h in heads]
    o = [_dot((q_ref[h, rows, :] * gamma[h]).astype(bf16), sb[h]) + _dot(qkd[h], ub[h]) for h in heads]
    for h in heads:
        g_last = gc[h][CHUNK - 1 : CHUNK, :]
        k_end = k_ref[h, rows, :] * jnp.exp(g_last - gc[h])
        s_ref[h] = jnp.exp(g_last) * s_ref[h] + _dot(k_end.T.astype(bf16), ub[h])
        zh = z_ref[rows, h * GDN_D : (h + 1) * GDN_D]
        og_ref[rows, h * GDN_D : (h + 1) * GDN_D] = (_rmsnorm(o[h], gn_ref[...]) * _silu(zh)).astype(bf16)


def _gdn_prompt_kernel(*refs, n_cast):
    qkv_ref, z_ref, ba_ref, cw_ref, av_ref, dv_ref, gn_ref = refs[:7]
    cast_in = refs[7 : 7 + n_cast]
    og_ref, so_ref = refs[7 + n_cast : 9 + n_cast]
    cast_out = refs[9 + n_cast : 9 + 2 * n_cast]
    s_ref, carry_ref, q_ref, k_ref, v_ref = refs[9 + 2 * n_cast :]
    for src, dst in zip(cast_in, cast_out):
        dst[...] = src[...].astype(bf16)

    c = pl.program_id(1)
    rows = qkv_ref.shape[0]
    n_chunks = rows // CHUNK

    @pl.when(c == 0)
    def _():
        s_ref[...] = jnp.zeros_like(s_ref)
        carry_ref[0:SUBLANES, :] = jnp.zeros((SUBLANES, CONV_CH), f32)

    carry_ref[SUBLANES : 2 * SUBLANES, :] = qkv_ref[0:SUBLANES, :]
    for j in range(3 * GDN_HEADS):
        sl = slice(j * LANES, (j + 1) * LANES)
        y = _conv_silu_slab(qkv_ref, carry_ref, cw_ref, sl)
        h = j % GDN_HEADS
        if j < 2 * GDN_HEADS:
            y = y * lax.rsqrt(jnp.sum(y * y, -1, keepdims=True) + L2_EPS)
            if j < GDN_HEADS:
                q_ref[h] = y * (GDN_D ** -0.5)
            else:
                k_ref[h] = y
        else:
            v_ref[h] = y
    carry_ref[0:SUBLANES, :] = qkv_ref[rows - SUBLANES : rows, :]

    ba = ba_ref[...]
    beta_t = jax.nn.sigmoid(ba)
    g_t = -jnp.exp(av_ref[...]) * _softplus(ba + dv_ref[...])
    row_in_chunk = lax.broadcasted_iota(jnp.int32, (rows, LANES), 0) % CHUNK
    gcum = g_t
    shift = 1
    while shift < CHUNK:
        gcum = gcum + jnp.where(row_in_chunk >= shift, pltpu.roll(gcum, shift, 0), 0.0)
        shift *= 2
    gcum_t = [gcum[ci * CHUNK : (ci + 1) * CHUNK, :].T for ci in range(n_chunks)]

    row = lax.broadcasted_iota(jnp.int32, (CHUNK, CHUNK), 0)
    col = lax.broadcasted_iota(jnp.int32, (CHUNK, CHUNK), 1)
    pairs = [(ci, h) for ci in range(n_chunks) for h in range(GDN_HEADS)]
    gc, gamma, sol, qkd = _delta_rule_prepare(pairs, gcum, gcum_t, beta_t, row, col, q_ref, k_ref, v_ref)
    for ci in range(n_chunks):
        mine = slice(ci * GDN_HEADS, (ci + 1) * GDN_HEADS)
        _delta_rule_apply(ci, (gc[mine], gamma[mine], sol[mine], qkd[mine]), q_ref, k_ref, z_ref, gn_ref, s_ref, og_ref)

    @pl.when(c == pl.num_programs(1) - 1)
    def _():
        so_ref[0] = s_ref[...]


def _cast_block(shape, steps):
    R, C = shape
    for col_blocks in (1, 2, 4, 8, 16):
        row_blocks = steps // col_blocks
        if (steps % col_blocks == 0 and R % row_blocks == 0 and C % col_blocks == 0
                and (R // row_blocks) % (2 * SUBLANES) == 0 and (C // col_blocks) % LANES == 0):
            return (R // row_blocks, C // col_blocks), col_blocks
    return None


def _gdn_prompt(proj, B, T, off_ba, cw, avec, dvec, gn, weights):
    rows = GDN_STEP_CHUNKS * CHUNK
    assert T % rows == 0
    ns = T // rows
    step = lambda b, c: b * ns + c
    plans = [_cast_block(w.shape, B * ns) for w in weights]
    riders = [w for w, p in zip(weights, plans) if p is not None]
    cast_specs = [pl.BlockSpec(blk, functools.partial(lambda b, c, cb: (step(b, c) // cb, step(b, c) % cb), cb=cb))
                  for blk, cb in (p for p in plans if p is not None)]
    head_scratch = pltpu.VMEM((GDN_HEADS, rows, GDN_D), f32)
    out = pl.pallas_call(
        functools.partial(_gdn_prompt_kernel, n_cast=len(riders)),
        grid=(B, ns),
        in_specs=[
            pl.BlockSpec((rows, CONV_CH), lambda b, c: (step(b, c), 0)),
            pl.BlockSpec((rows, GDN_W), lambda b, c: (step(b, c), OFF_Z // GDN_W)),
            pl.BlockSpec((rows, LANES), lambda b, c: (step(b, c), off_ba // LANES)),
            pl.BlockSpec((GDN_CONV, CONV_CH), lambda b, c: (0, 0)),
            pl.BlockSpec((1, LANES), lambda b, c: (0, 0)),
            pl.BlockSpec((1, LANES), lambda b, c: (0, 0)),
            pl.BlockSpec((1, GDN_D), lambda b, c: (0, 0)),
        ] + cast_specs,
        out_specs=[
            pl.BlockSpec((rows, GDN_W), lambda b, c: (step(b, c), 0)),
            pl.BlockSpec((1, GDN_HEADS, GDN_D, GDN_D), lambda b, c: (b, 0, 0, 0)),
        ] + cast_specs,
        out_shape=[
            jax.ShapeDtypeStruct((B * T, GDN_W), bf16),
            jax.ShapeDtypeStruct((B, GDN_HEADS, GDN_D, GDN_D), f32),
        ] + [jax.ShapeDtypeStruct(w.shape, bf16) for w in riders],
        scratch_shapes=[
            pltpu.VMEM((GDN_HEADS, GDN_D, GDN_D), f32),
            pltpu.VMEM((2 * SUBLANES, CONV_CH), f32),
            head_scratch, head_scratch, head_scratch,
        ],
        compiler_params=_cparams(("arbitrary", "arbitrary")),
        name="gdn_prompt",
    )(proj, proj, proj, cw, avec, dvec, gn, *riders)
    converted = iter(out[2:])
    return out[0], out[1], [next(converted) if p is not None else w.astype(bf16) for w, p in zip(weights, plans)]


def _gdn_step_kernel(qkv_ref, z_ref, ba_ref, sc_ref, s0_ref, cw_ref, av_ref, dv_ref, gn_ref, og_ref, so_ref):
    taps = cw_ref[...]
    seqs = range(qkv_ref.shape[0])
    y, z, beta_t, gamma_t = [], [], [], []
    for i in seqs:
        x_new = qkv_ref[i]
        conv = jnp.sum(sc_ref[i] * taps[0 : GDN_CONV - 1, :], axis=0, keepdims=True) + x_new * taps[GDN_CONV - 1 : GDN_CONV, :]
        y.append(_silu(conv))
        ba = ba_ref[i]
        beta_t.append(jax.nn.sigmoid(ba))
        gamma_t.append(jnp.exp(-jnp.exp(av_ref[...]) * _softplus(ba + dv_ref[...])))
        z.append(z_ref[i])
    pairs = [(i, h) for i in seqs for h in range(GDN_HEADS)]
    n = range(len(pairs))
    head = lambda i, base, h: y[i][:, base + h * GDN_D : base + (h + 1) * GDN_D]
    q = [head(i, 0, h) for i, h in pairs]
    k = [head(i, GDN_W, h) for i, h in pairs]
    v = [head(i, 2 * GDN_W, h) for i, h in pairs]
    q = [q[p] * lax.rsqrt(jnp.sum(q[p] * q[p], -1, keepdims=True) + L2_EPS) * (GDN_D ** -0.5) for p in n]
    k = [k[p] * lax.rsqrt(jnp.sum(k[p] * k[p], -1, keepdims=True) + L2_EPS) for p in n]
    beta = [beta_t[i][:, h : h + 1] for i, h in pairs]
    gamma = [gamma_t[i][:, GDN_HEADS + h : GDN_HEADS + h + 1] for i, h in pairs]
    k_col = [jnp.broadcast_to(k[p], (GDN_D, GDN_D)).T for p in n]
    sub = lax.broadcasted_iota(jnp.int32, (SUBLANES, GDN_D), 0)
    kq = [jnp.where(sub == 0, k[p], jnp.where(sub == 1, q[p], 0.0)).astype(bf16) for p in n]
    kq_s = [_dot(kq[p], s0_ref[i, h].astype(bf16)) for p, (i, h) in enumerate(pairs)]
    u = [beta[p] * v[p] - (beta[p] * gamma[p]) * kq_s[p][0:1] for p in n]
    o = [gamma[p] * kq_s[p][1:2] + jnp.sum(q[p] * k[p], -1, keepdims=True) * u[p] for p in n]
    for p, (i, h) in enumerate(pairs):
        so_ref[i, h] = gamma[p] * s0_ref[i, h] + k_col[p] * u[p]
        zh = z[i][:, h * GDN_D : (h + 1) * GDN_D]
        og_ref[i, :, h * GDN_D : (h + 1) * GDN_D] = (_rmsnorm(o[p], gn_ref[...]) * _silu(zh)).astype(bf16)


def _gdn_step(proj3, off_ba, state_conv, state_gdn, cw, avec, dvec, gn):
    B = proj3.shape[0]
    bb = math.gcd(B, GDN_STEP_BATCH)
    return pl.pallas_call(
        _gdn_step_kernel,
        grid=(B // bb,),
        in_specs=[
            pl.BlockSpec((bb, 1, CONV_CH), lambda b: (b, 0, 0)),
            pl.BlockSpec((bb, 1, GDN_W), lambda b: (b, 0, OFF_Z // GDN_W)),
            pl.BlockSpec((bb, 1, LANES), lambda b: (b, 0, off_ba // LANES)),
            pl.BlockSpec((bb, GDN_CONV - 1, CONV_CH), lambda b: (b, 0, 0)),
            pl.BlockSpec((bb, GDN_HEADS, GDN_D, GDN_D), lambda b: (b, 0, 0, 0)),
            pl.BlockSpec((GDN_CONV, CONV_CH), lambda b: (0, 0)),
            pl.BlockSpec((1, LANES), lambda b: (0, 0)),
            pl.BlockSpec((1, LANES), lambda b: (0, 0)),
            pl.BlockSpec((1, GDN_D), lambda b: (0, 0)),
        ],
        out_specs=[
            pl.BlockSpec((bb, 1, GDN_W), lambda b: (b, 0, 0)),
            pl.BlockSpec((bb, GDN_HEADS, GDN_D, GDN_D), lambda b: (b, 0, 0, 0)),
        ],
        out_shape=[
            jax.ShapeDtypeStruct((B, 1, GDN_W), bf16),
            jax.ShapeDtypeStruct((B, GDN_HEADS, GDN_D, GDN_D), f32),
        ],
        compiler_params=_cparams(("parallel",)),
        name="gdn_step",
    )(proj3, proj3, proj3, state_conv, state_gdn, cw, avec, dvec, gn)


def _rope(x, cos, sin_signed):
    width = x.shape[-1]
    lane = lax.broadcasted_iota(jnp.int32, x.shape, x.ndim - 1)
    first_half = (lane % SWA_HD) < (SWA_HD // 2)
    rot = jnp.where(first_half, pltpu.roll(x, width - SWA_HD // 2, x.ndim - 1), pltpu.roll(x, SWA_HD // 2, x.ndim - 1))
    return x * cos + rot * sin_signed


def _head_halves(x2, head_parity, lane):
    swapped = pltpu.roll(x2, SWA_HD, 1)
    lo_src, hi_src = (x2, swapped) if head_parity == 0 else (swapped, x2)
    return jnp.where(lane < SWA_HD, lo_src, 0.0), jnp.where(lane >= SWA_HD, hi_src, 0.0)


def _swa_prompt_kernel(sinks_ref, q_ref, kv_ref, cos_ref, sin_ref, os_ref, kc_ref, kprev_ref, vprev_ref):
    n = pl.program_id(1)
    n_blk = q_ref.shape[0] // WINDOW

    @pl.when(n == 0)
    def _():
        kprev_ref[...] = jnp.zeros_like(kprev_ref)
        vprev_ref[...] = jnp.zeros_like(vprev_ref)

    row = lax.broadcasted_iota(jnp.int32, (WINDOW, WINDOW), 0)
    col = lax.broadcasted_iota(jnp.int32, (WINDOW, WINDOW), 1)
    own = col <= row
    first_bias = jnp.where(n > 0, 0.0, -jnp.inf)
    lane = lax.broadcasted_iota(jnp.int32, (WINDOW, LANES), 1)
    scale = SWA_HD ** -0.5
    group = SWA_HEADS // SWA_KV_HEADS

    k_own, v_own, k_pre, v_pre, q2 = [], [], [], [], []
    for blk in range(n_blk):
        rows = slice(blk * WINDOW, (blk + 1) * WINDOW)
        cos = cos_ref[rows, :]
        sin = sin_ref[rows, :]
        k_cur = _rope(kv_ref[rows, :SWA_KV_W], cos, sin)
        v_cur = kv_ref[rows, SWA_KV_W:]
        if blk == n_blk - 1:
            kc_ref[0] = k_cur
        k_own.append([]), v_own.append([]), k_pre.append([]), v_pre.append([]), q2.append([])
        for h in range(SWA_KV_HEADS):
            pair = slice((h // 2) * LANES, (h // 2 + 1) * LANES)
            k_own[blk].append([a.astype(bf16) for a in _head_halves(k_cur[:, pair], h % 2, lane)])
            v_own[blk].append([a.astype(bf16) for a in _head_halves(v_cur[:, pair], h % 2, lane)])
            if blk == 0:
                k_pre[blk].append([kprev_ref[h, par] for par in range(2)])
                v_pre[blk].append([vprev_ref[h, par] for par in range(2)])
            else:
                k_pre[blk].append(k_own[blk - 1][h])
                v_pre[blk].append(v_own[blk - 1][h])
            if blk == n_blk - 1:
                for par in range(2):
                    kprev_ref[h, par] = k_own[blk][h][par]
                    vprev_ref[h, par] = v_own[blk][h][par]
            q_h = _rope(q_ref[rows, h * group * SWA_HD : (h + 1) * group * SWA_HD], cos, sin) * scale
            q2[blk].append([q_h[:, j * LANES : (j + 1) * LANES].astype(bf16) for j in range(group // 2)])

    heads = [(blk, h, j, par) for blk in range(n_blk) for h in range(SWA_KV_HEADS)
             for j in range(group // 2) for par in range(2)]
    idx = range(len(heads))
    sink = [sinks_ref[h * group + 2 * j + par] for blk, h, j, par in heads]
    s = [jnp.where(own, _dot_nt(q2[blk][h][j], k_own[blk][h][par]),
                   _dot_nt(q2[blk][h][j], k_pre[blk][h][par]) + (first_bias if blk == 0 else 0.0))
         for blk, h, j, par in heads]
    m = [jnp.maximum(jnp.max(s[i], -1, keepdims=True), sink[i]) for i in idx]
    p = [jnp.exp(s[i] - m[i]) for i in idx]
    inv = [1.0 / (jnp.sum(p[i], -1, keepdims=True) + jnp.exp(sink[i] - m[i])) for i in idx]
    o = [(_dot(jnp.where(own, p[i], 0.0).astype(bf16), v_own[blk][h][par])
          + _dot(jnp.where(own, 0.0, p[i]).astype(bf16), v_pre[blk][h][par])) * inv[i]
         for i, (blk, h, j, par) in enumerate(heads)]
    for i in range(0, len(heads), 2):
        blk, head = divmod(i, SWA_HEADS)
        os_ref[blk * WINDOW : (blk + 1) * WINDOW, head * SWA_HD : head * SWA_HD + LANES] = (o[i] + o[i + 1]).astype(bf16)


def _swa_prompt(proj, B, T, off_q, off_kv, cos, sin, sinks):
    assert T % WINDOW == 0 and SWA_KV_W == 2 * LANES
    rows = math.gcd(SWA_STEP_BLOCKS, T // WINDOW) * WINDOW
    nb = T // rows
    return pl.pallas_call(
        _swa_prompt_kernel,
        grid=(B, nb),
        in_specs=[
            pl.BlockSpec(memory_space=pltpu.SMEM),
            pl.BlockSpec((rows, SWA_Q_W), lambda b, n: (b * nb + n, off_q // SWA_Q_W)),
            pl.BlockSpec((rows, 2 * SWA_KV_W), lambda b, n: (b * nb + n, off_kv // (2 * SWA_KV_W))),
            pl.BlockSpec((rows, SWA_KV_W), lambda b, n: (n, 0)),
            pl.BlockSpec((rows, SWA_KV_W), lambda b, n: (n, 0)),
        ],
        out_specs=[
            pl.BlockSpec((rows, SWA_Q_W), lambda b, n: (b * nb + n, 0)),
            pl.BlockSpec((1, WINDOW, SWA_KV_W), lambda b, n: (b, 0, 0)),
        ],
        out_shape=[
            jax.ShapeDtypeStruct((B * T, SWA_Q_W), bf16),
            jax.ShapeDtypeStruct((B, WINDOW, SWA_KV_W), f32),
        ],
        scratch_shapes=[pltpu.VMEM((SWA_KV_HEADS, 2, WINDOW, LANES), bf16)] * 2,
        compiler_params=_cparams(("parallel", "arbitrary")),
        name="swa_prompt",
    )(sinks, proj, proj, cos, sin)


def _swa_step_kernel(qe_ref, kv_ref, ck_ref, cv_ref, cos_ref, sin_ref, sinks_ref, r_ref, nk_ref, nv_ref):
    cos = cos_ref[...]
    sin = sin_ref[...]
    sink = sinks_ref[...]
    row = lax.broadcasted_iota(jnp.int32, (WINDOW, SWA_KV_W), 0)
    scale = SWA_HD ** -0.5
    seqs = range(qe_ref.shape[0])
    kv = [kv_ref[i] for i in seqs]
    k_new = [_rope(kv[i][:, :SWA_KV_W], cos, sin) for i in seqs]
    keys = [jnp.where(row == WINDOW - 1, k_new[i], pltpu.roll(ck_ref[i].T, WINDOW - 1, 0)) for i in seqs]
    vals = [jnp.where(row == WINDOW - 1, kv[i][:, SWA_KV_W:], pltpu.roll(cv_ref[i].T, WINDOW - 1, 0)) for i in seqs]
    for i in seqs:
        nk_ref[i] = keys[i].T
        nv_ref[i] = vals[i].T
    q = [_rope(qe_ref[i], cos, sin) for i in seqs]
    s = [_dot_nt(q[i].astype(bf16), keys[i].astype(bf16)) * scale for i in seqs]
    m = [jnp.maximum(jnp.max(s[i], -1, keepdims=True), sink) for i in seqs]
    p = [jnp.exp(s[i] - m[i]) for i in seqs]
    denom = [jnp.sum(p[i], -1, keepdims=True) + jnp.exp(sink - m[i]) for i in seqs]
    for i in seqs:
        r_ref[i] = _dot((p[i] / denom[i]).astype(bf16), vals[i].astype(bf16))


def _swa_step(q_exp, proj3, off_kv, cache_k, cache_v, cos, sin, sinks_col):
    B = q_exp.shape[0]
    bb = math.gcd(B, SWA_STEP_BATCH)
    assert cache_k.shape[2] == WINDOW
    return pl.pallas_call(
        _swa_step_kernel,
        grid=(B // bb,),
        in_specs=[
            pl.BlockSpec((bb, SWA_HEADS, SWA_KV_W), lambda i: (i, 0, 0)),
            pl.BlockSpec((bb, 1, 2 * SWA_KV_W), lambda i: (i, 0, off_kv // (2 * SWA_KV_W))),
            pl.BlockSpec((bb, SWA_KV_W, WINDOW), lambda i: (i, 0, 0)),
            pl.BlockSpec((bb, SWA_KV_W, WINDOW), lambda i: (i, 0, 0)),
            pl.BlockSpec((1, SWA_KV_W), lambda i: (0, 0)),
            pl.BlockSpec((1, SWA_KV_W), lambda i: (0, 0)),
            pl.BlockSpec((SWA_HEADS, 1), lambda i: (0, 0)),
        ],
        out_specs=[
            pl.BlockSpec((bb, SWA_HEADS, SWA_KV_W), lambda i: (i, 0, 0)),
            pl.BlockSpec((bb, SWA_KV_W, WINDOW), lambda i: (i, 0, 0)),
            pl.BlockSpec((bb, SWA_KV_W, WINDOW), lambda i: (i, 0, 0)),
        ],
        out_shape=[
            jax.ShapeDtypeStruct((B, SWA_HEADS, SWA_KV_W), f32),
            jax.ShapeDtypeStruct((B, SWA_KV_W, WINDOW), f32),
            jax.ShapeDtypeStruct((B, SWA_KV_W, WINDOW), f32),
        ],
        compiler_params=_cparams(("parallel",)),
        name="swa_step",
    )(q_exp, proj3, cache_k, cache_v, cos, sin, sinks_col)


def _merge_core(og_ref, os_ref, gg_ref, gs_ref, x_ref, wg_ref, ws_ref, wo_ref, gq_ref, wq_ref):
    p_gdn = _dot(og_ref[...], wg_ref[...])
    p_swa = _dot(os_ref[...], ws_ref[...])
    merged = jax.nn.sigmoid(gg_ref[...]) * p_gdn + jax.nn.sigmoid(gs_ref[...]) * p_swa
    x_new = x_ref[...] + _dot(merged.astype(bf16), wo_ref[...])
    return x_new, _dot(_rmsnorm(x_new, gq_ref[...]).astype(bf16), wq_ref[...])


def _merge_kernel(*refs):
    xo_ref, qm_ref = refs[-2:]
    xo_ref[...], qm_ref[...] = _merge_core(*refs[:-2])


def _merge_mem_kernel(*refs):
    mk_ref, mv_ref, wmo_ref, xo_ref = refs[-4:]
    x_new, q = _merge_core(*refs[:-4])
    q = q.astype(bf16)
    scale = MEM_HD ** -0.5
    heads = range(MEM_HEADS)
    cols = lambda h: slice(h * MEM_HD, (h + 1) * MEM_HD)
    s = [_dot_nt(q[:, cols(h)], mk_ref[0, :, cols(h)]) * scale for h in heads]
    p = [jnp.exp(s[h] - jnp.max(s[h], -1, keepdims=True)) for h in heads]
    p = [p[h] / jnp.sum(p[h], -1, keepdims=True) for h in heads]
    o = [_dot(p[h].astype(bf16), mv_ref[0, :, cols(h)]) for h in heads]
    xo_ref[...] = x_new + _dot(jnp.concatenate(o, axis=-1).astype(bf16), wmo_ref[...])


def _merge(og, os_, proj, off_gg, x, wg, ws, wo, gq, wq, mem=None):
    M, D = x.shape
    tm = min(MERGE_TM, M)
    assert M % tm == 0 and off_gg % D == 0
    const = lambda shape: pl.BlockSpec(shape, lambda i: (0, 0), pipeline_mode=pl.Buffered(1))
    in_specs = [
        pl.BlockSpec((tm, GDN_W), lambda i: (i, 0)),
        pl.BlockSpec((tm, SWA_Q_W), lambda i: (i, 0)),
        pl.BlockSpec((tm, D), lambda i: (i, off_gg // D)),
        pl.BlockSpec((tm, D), lambda i: (i, off_gg // D + 1)),
        pl.BlockSpec((tm, D), lambda i: (i, 0)),
        const((GDN_W, D)),
        const((SWA_Q_W, D)),
        const((D, D)),
        const((1, D)),
        const((D, MEM_W)),
    ]
    args = [og, os_, proj, proj, x, wg, ws, wo, gq, wq]
    x_spec = pl.BlockSpec((tm, D), lambda i: (i, 0))
    x_shape = jax.ShapeDtypeStruct((M, D), f32)
    if mem is None:
        return pl.pallas_call(
            _merge_kernel,
            grid=(M // tm,),
            in_specs=in_specs,
            out_specs=[x_spec, pl.BlockSpec((tm, MEM_W), lambda i: (i, 0))],
            out_shape=[x_shape, jax.ShapeDtypeStruct((M, MEM_W), f32)],
            compiler_params=_cparams(("parallel",)),
            name="merge",
        )(*args)
    mem_k, mem_v, wmo, seq_rows = mem
    assert seq_rows % tm == 0
    mt = mem_k.shape[1]
    mem_spec = pl.BlockSpec((1, mt, MEM_W), lambda i: (i // (seq_rows // tm), 0, 0))
    return pl.pallas_call(
        _merge_mem_kernel,
        grid=(M // tm,),
        in_specs=in_specs + [mem_spec, mem_spec, const((MEM_W, D))],
        out_specs=x_spec,
        out_shape=x_shape,
        compiler_params=_cparams(("parallel",)),
        name="merge_mem",
    )(*args, mem_k, mem_v, wmo)


def _mem_attn_step_kernel(q_ref, k_ref, v_ref, o_ref):
    mt2 = k_ref.shape[1] // SUBLANES
    seqs = range(q_ref.shape[0])
    q8 = [q_ref[i] * (MEM_HD ** -0.5) for i in seqs]
    s = [jnp.sum(k_ref[i].reshape(mt2, SUBLANES, MEM_HD) * q8[i], axis=-1, keepdims=True) for i in seqs]
    m = [jnp.max(s[i], axis=0) for i in seqs]
    m = [jnp.maximum(m[i], pltpu.roll(m[i], MEM_HEADS, 0)) for i in seqs]
    p = [jnp.exp(s[i] - m[i]) for i in seqs]
    l = [jnp.sum(p[i], axis=0) for i in seqs]
    l = [l[i] + pltpu.roll(l[i], MEM_HEADS, 0) for i in seqs]
    o = [jnp.sum(p[i] * v_ref[i].reshape(mt2, SUBLANES, MEM_HD), axis=0) for i in seqs]
    for i in seqs:
        o_ref[i] = (o[i] + pltpu.roll(o[i], MEM_HEADS, 0)) / l[i]


def _mem_attn_step(q8, mem_k, mem_v):
    B, rows, _ = mem_k.shape
    assert 2 * MEM_HEADS == SUBLANES and rows % SUBLANES == 0
    bb = math.gcd(B, MEM_STEP_BATCH)
    return pl.pallas_call(
        _mem_attn_step_kernel,
        grid=(B // bb,),
        in_specs=[
            pl.BlockSpec((bb, SUBLANES, MEM_HD), lambda i: (i, 0, 0)),
            pl.BlockSpec((bb, rows, MEM_HD), lambda i: (i, 0, 0)),
            pl.BlockSpec((bb, rows, MEM_HD), lambda i: (i, 0, 0)),
        ],
        out_specs=pl.BlockSpec((bb, SUBLANES, MEM_HD), lambda i: (i, 0, 0)),
        out_shape=jax.ShapeDtypeStruct((B, SUBLANES, MEM_HD), f32),
        compiler_params=_cparams(("parallel",)),
        name="mem_attn_step",
    )(q8, mem_k, mem_v)


def _proj_residual_kernel(a_ref, w_ref, x_ref, o_ref):
    o_ref[...] = x_ref[...] + _dot(a_ref[...].astype(bf16), w_ref[...])


def _proj_residual(a, w, x):
    M, D = x.shape
    return pl.pallas_call(
        _proj_residual_kernel,
        out_shape=jax.ShapeDtypeStruct((M, D), f32),
        compiler_params=pltpu.CompilerParams(vmem_limit_bytes=VMEM_LIMIT),
        name="proj_residual",
    )(a, w, x)


def _reorder_rows_kernel(w_ref, o_ref, *, pieces):
    at = 0
    for lo, hi in pieces:
        o_ref[at : at + hi - lo, :] = w_ref[lo:hi, :].astype(bf16)
        at += hi - lo
    o_ref[at:, :] = jnp.zeros((o_ref.shape[0] - at, o_ref.shape[1]), bf16)


def _reorder_rows(wt, layer, pieces, height):
    _, d_in, D = wt.shape
    assert all(lo % (2 * SUBLANES) == 0 and hi % (2 * SUBLANES) == 0 for lo, hi in pieces) and D % LANES == 0
    return pl.pallas_call(
        functools.partial(_reorder_rows_kernel, pieces=pieces),
        grid=(D // LANES,),
        in_specs=[pl.BlockSpec((None, d_in, LANES), lambda i: (layer, 0, i))],
        out_specs=pl.BlockSpec((height, LANES), lambda i: (0, i)),
        out_shape=jax.ShapeDtypeStruct((height, D), bf16),
        compiler_params=_cparams(("parallel",)),
        name="reorder_rows",
    )(wt)


def _rope_tables(pos):
    half = SWA_HD // 2
    inv_freq = ROPE_THETA ** (-jnp.arange(half, dtype=f32) / half)
    ang = pos.astype(f32)[:, None] * inv_freq[None, :]
    cos = jnp.cos(ang)
    sin = jnp.sin(ang)
    reps = SWA_KV_W // SWA_HD
    return jnp.tile(jnp.concatenate([cos, cos], -1), (1, reps)), jnp.tile(jnp.concatenate([-sin, sin], -1), (1, reps))


def kernel(x_prompt, x_sample, state_gdn, state_conv, cache_swa_k, cache_swa_v, cache_mem_k, cache_mem_v, mem_prompt, norm_ffn1, ffn1_w1, ffn1_w3, ffn1_w2, norm_mix, w_in, conv_w, gdn_A_log, gdn_dt_bias, gdn_norm, swa_sinks, w_br_gdn, w_br_swa, w_out, norm_mem_q, norm_mem_kv, w_mem_q, w_mem_k, w_mem_v, w_mem_o, norm_ffn2, ffn2_w1, ffn2_w3, ffn2_w2, norm_final):
    Bp, Tp, D = x_prompt.shape
    Bs, Ts, _ = x_sample.shape
    assert Ts == 1
    depth = norm_ffn1.shape[0]
    n_mem = mem_prompt.shape[1]
    group = SWA_HEADS // SWA_KV_HEADS

    off_gs = OFF_GG + D
    off_q = off_gs + D
    off_kv = off_q + SWA_Q_W
    off_ba = off_kv + 2 * SWA_KV_W
    d_in_pad = -(-(off_ba + LANES) // PROJ_TN) * PROJ_TN
    o_b = CONV_CH + GDN_W
    o_q = o_b + 2 * GDN_HEADS
    o_gg = o_q + SWA_Q_W + 2 * SWA_KV_W

    cos_p, sin_p = _rope_tables(jnp.arange(Tp, dtype=jnp.int32))
    cos_s, sin_s = _rope_tables(PAST_LEN + jnp.arange(Ts, dtype=jnp.int32))
    eye_kv = jnp.eye(SWA_KV_HEADS, dtype=f32)
    row = lambda v: v.reshape(1, -1)

    hp = x_prompt.reshape(Bp * Tp, D)
    hs = x_sample.reshape(Bs, D)
    outs = [[] for _ in range(10)]
    for l in range(depth):
        w_in_rt = _reorder_rows(jnp.swapaxes(w_in, 1, 2), l, ((0, o_b), (o_gg, w_in.shape[2]), (o_q, o_gg), (o_b, o_q)), d_in_pad)
        ffn1 = (row(norm_ffn1[l]), ffn1_w1[l].astype(bf16), ffn1_w3[l].astype(bf16), ffn1_w2[l].astype(bf16))
        last = l == depth - 1
        gfin = row(norm_final)
        avec = jnp.zeros((1, LANES), f32).at[0, GDN_HEADS : 2 * GDN_HEADS].set(gdn_A_log[l])
        dvec = jnp.zeros((1, LANES), f32).at[0, GDN_HEADS : 2 * GDN_HEADS].set(gdn_dt_bias[l])
        gdn_common = (conv_w[l], avec, dvec, row(gdn_norm[l]))

        x1 = _ffn(hp, *ffn1, gfin, False)
        proj = _norm_matmul_nt_lookahead(x1, row(norm_mix[l]), w_in_rt, PROJ_TN)
        later = (ffn2_w1[l], ffn2_w3[l], ffn2_w2[l], w_br_gdn[l], w_br_swa[l], w_out[l], w_mem_q[l],
                 w_mem_k[l], w_mem_v[l], w_mem_o[l])
        og, s_new, later = _gdn_prompt(proj, Bp, Tp, off_ba, *gdn_common, later)
        ffn2 = (row(norm_ffn2[l]), *later[0:3])
        merge_w = (*later[3:6], row(norm_mem_q[l]), later[6])
        wmk, wmv, wmo = later[7:10]
        os_, kc = _swa_prompt(proj, Bp, Tp, off_q, off_kv, cos_p, sin_p, swa_sinks[l])
        mem_x = mem_prompt.reshape(Bp * n_mem, D)
        mk, mv, mkb, mvb = _mem_kv(mem_x, row(norm_mem_kv[l]), wmk, wmv)
        mem = (mkb.reshape(Bp, n_mem, MEM_W), mvb.reshape(Bp, n_mem, MEM_W), wmo, Tp)
        x3 = _merge(og, os_, proj, OFF_GG, x1, *merge_w, mem=mem)
        hp = _ffn(x3, *ffn2, gfin, last)
        proj_b = proj.reshape(Bp, Tp, d_in_pad)
        outs[0].append(s_new)
        outs[1].append(proj_b[:, Tp - (GDN_CONV - 1) :, :CONV_CH])
        outs[2].append(kc.reshape(Bp, WINDOW, SWA_KV_HEADS, SWA_HD))
        outs[3].append(proj_b[:, Tp - WINDOW :, off_kv + SWA_KV_W : off_kv + 2 * SWA_KV_W].reshape(Bp, WINDOW, SWA_KV_HEADS, SWA_HD))
        outs[4].append(mk.reshape(Bp, n_mem, MEM_HEADS, MEM_HD))
        outs[5].append(mv.reshape(Bp, n_mem, MEM_HEADS, MEM_HD))

        x1 = _ffn(hs, *ffn1, gfin, False)
        proj = _norm_matmul(x1, row(norm_mix[l]), w_in_rt, PROJ_TN, w_transposed=True)
        proj3 = proj.reshape(Bs, 1, d_in_pad)
        og, s_new = _gdn_step(proj3, off_ba, state_conv[l], state_gdn[l], *gdn_common)
        q_raw = proj[:, off_q : off_q + SWA_Q_W].reshape(Bs, SWA_KV_HEADS, group, 1, SWA_HD)
        q_exp = (q_raw * eye_kv[None, :, None, :, None]).reshape(Bs, SWA_HEADS, SWA_KV_W)
        ck = jnp.swapaxes(cache_swa_k[l].reshape(Bs, WINDOW, SWA_KV_W), 1, 2)
        cv = jnp.swapaxes(cache_swa_v[l].reshape(Bs, WINDOW, SWA_KV_W), 1, 2)
        r, nk, nv = _swa_step(q_exp, proj3, off_kv, ck, cv, cos_s, sin_s, swa_sinks[l].reshape(SWA_HEADS, 1))
        r5 = r.reshape(Bs, SWA_KV_HEADS, group, SWA_KV_HEADS, SWA_HD)
        kvh = jnp.arange(SWA_KV_HEADS)
        os_ = jnp.transpose(r5[:, kvh, :, kvh, :], (1, 0, 2, 3)).reshape(Bs, SWA_Q_W).astype(bf16)
        x2, qm = _merge(og.reshape(Bs, GDN_W), os_, proj, OFF_GG, x1, *merge_w)
        q8 = jnp.tile(qm.reshape(Bs, MEM_HEADS, MEM_HD), (1, 2, 1))
        om = _mem_attn_step(q8, cache_mem_k[l].reshape(Bs, n_mem * MEM_HEADS, MEM_HD),
                            cache_mem_v[l].reshape(Bs, n_mem * MEM_HEADS, MEM_HD))
        x3 = _proj_residual(om[:, :MEM_HEADS].reshape(Bs, MEM_W), wmo, x2)
        hs = _ffn(x3, *ffn2, gfin, last)
        outs[6].append(s_new)
        outs[7].append(jnp.concatenate([state_conv[l][:, 1:], proj3[:, :, :CONV_CH]], axis=1))
        outs[8].append(jnp.swapaxes(nk, 1, 2).reshape(Bs, WINDOW, SWA_KV_HEADS, SWA_HD))
        outs[9].append(jnp.swapaxes(nv, 1, 2).reshape(Bs, WINDOW, SWA_KV_HEADS, SWA_HD))

    return (hp.reshape(Bp, Tp, D), hs.reshape(Bs, Ts, D), *(jnp.stack(o) for o in outs))
```

```python
import functools
import math

import jax
import jax.numpy as jnp
from jax import lax
from jax.experimental import pallas as pl
from jax.experimental.pallas import tpu as pltpu

f32 = jnp.float32
bf16 = jnp.bfloat16

PAST_LEN = 16384
GDN_HEADS = 8
GDN_D = 128
GDN_CONV = 4
SWA_HEADS = 16
SWA_KV_HEADS = 4
SWA_HD = 64
WINDOW = 128
ROPE_THETA = 10000.0
MEM_HEADS = 4
MEM_HD = 128
EPS = 1e-6
L2_EPS = 1e-6

LANES = 128
SUBLANES = 8
VMEM_LIMIT = 60 * 1024 * 1024

FFN_TM = 1024
FFN_TF = 512
FFN_SLAB = 64
PROJ_TM = 1024
PROJ_TN = 768
MERGE_TM = 256
CHUNK = 128
GDN_STEP_CHUNKS = 2
SWA_STEP_BLOCKS = 4
GDN_STEP_BATCH = 8
SWA_STEP_BATCH = 8
MEM_STEP_BATCH = 8

GDN_W = GDN_HEADS * GDN_D
CONV_CH = 3 * GDN_W
SWA_Q_W = SWA_HEADS * SWA_HD
SWA_KV_W = SWA_KV_HEADS * SWA_HD
MEM_W = MEM_HEADS * MEM_HD

OFF_Z = CONV_CH
OFF_GG = OFF_Z + GDN_W


def _cparams(semantics):
    return pltpu.CompilerParams(dimension_semantics=semantics, vmem_limit_bytes=VMEM_LIMIT)


def _dot(a, b):
    return jnp.dot(a, b, preferred_element_type=f32)


def _dot_nt(a, b):
    return lax.dot_general(a, b, (((1,), (1,)), ((), ())), preferred_element_type=f32)


def _rmsnorm(x, g):
    return x * lax.rsqrt(jnp.mean(x * x, -1, keepdims=True) + EPS) * g


def _silu(x):
    return x * jax.nn.sigmoid(x)


def _softplus(x):
    return jnp.maximum(x, 0.0) + jnp.log1p(jnp.exp(-jnp.abs(x)))


def _ffn_kernel(x_ref, g_ref, w1_ref, w3_ref, w2_ref, gf_ref, o_ref, h_ref, *, final_norm):
    j = pl.program_id(1)
    slab = min(FFN_SLAB, x_ref.shape[0])
    n_slabs = x_ref.shape[0] // slab

    def over_slabs(body):
        def step(s, carry):
            body(pl.ds(pl.multiple_of(s * slab, slab), slab))
            return carry
        lax.fori_loop(0, n_slabs, step, 0, unroll=min(4, n_slabs))

    @pl.when(j == 0)
    def _():
        def prologue(rows):
            h_ref[rows, :] = _rmsnorm(x_ref[rows, :], g_ref[...]).astype(bf16)
            o_ref[rows, :] = jnp.zeros((slab, o_ref.shape[1]), f32)
        over_slabs(prologue)

    h = h_ref[...]
    half = w1_ref.shape[1] // 2
    down = []
    for cols in (slice(0, half), slice(half, 2 * half)):
        a = _dot(h, w1_ref[:, cols])
        b = _dot(h, w3_ref[:, cols])
        down.append(_dot((_silu(a) * b).astype(bf16), w2_ref[cols, :]))
    o_ref[...] += down[0] + down[1]

    @pl.when(j == pl.num_programs(1) - 1)
    def _():
        def epilogue(rows):
            y = x_ref[rows, :] + 0.5 * o_ref[rows, :]
            if final_norm:
                y = _rmsnorm(y, gf_ref[...])
            o_ref[rows, :] = y
        over_slabs(epilogue)


def _ffn(x, g, w1, w3, w2, gf, final_norm):
    M, D = x.shape
    F = w1.shape[1]
    tm = min(FFN_TM, M)
    tf = FFN_TF
    assert M % tm == 0 and F % tf == 0
    return pl.pallas_call(
        functools.partial(_ffn_kernel, final_norm=final_norm),
        grid=(M // tm, F // tf),
        in_specs=[
            pl.BlockSpec((tm, D), lambda i, j: (i, 0)),
            pl.BlockSpec((1, D), lambda i, j: (0, 0)),
            pl.BlockSpec((D, tf), lambda i, j: (0, j)),
            pl.BlockSpec((D, tf), lambda i, j: (0, j)),
            pl.BlockSpec((tf, D), lambda i, j: (j, 0)),
            pl.BlockSpec((1, D), lambda i, j: (0, 0)),
        ],
        out_specs=pl.BlockSpec((tm, D), lambda i, j: (i, 0)),
        out_shape=jax.ShapeDtypeStruct((M, D), f32),
        scratch_shapes=[pltpu.VMEM((tm, D), bf16)],
        compiler_params=_cparams(("parallel", "arbitrary")),
        name="ffn",
    )(x, g, w1, w3, w2, gf)


def _norm_matmul_kernel(x_ref, g_ref, w_ref, o_ref, h_ref, *, w_transposed):
    @pl.when(pl.program_id(1) == 0)
    def _():
        h_ref[...] = _rmsnorm(x_ref[...], g_ref[...]).astype(bf16)

    o_ref[...] = (_dot_nt if w_transposed else _dot)(h_ref[...], w_ref[...])


def _norm_matmul(x, g, w, tn, w_transposed=False):
    M, D = x.shape
    N = w.shape[0] if w_transposed else w.shape[1]
    tm = min(PROJ_TM, M)
    assert M % tm == 0 and N % tn == 0
    w_spec = pl.BlockSpec((tn, D), lambda i, j: (j, 0)) if w_transposed else pl.BlockSpec((D, tn), lambda i, j: (0, j))
    return pl.pallas_call(
        functools.partial(_norm_matmul_kernel, w_transposed=w_transposed),
        grid=(M // tm, N // tn),
        in_specs=[
            pl.BlockSpec((tm, D), lambda i, j: (i, 0)),
            pl.BlockSpec((1, D), lambda i, j: (0, 0)),
            w_spec,
        ],
        out_specs=pl.BlockSpec((tm, tn), lambda i, j: (i, j)),
        out_shape=jax.ShapeDtypeStruct((M, N), f32),
        scratch_shapes=[pltpu.VMEM((tm, D), bf16)],
        compiler_params=_cparams(("parallel", "arbitrary")),
        name="norm_matmul",
    )(x, g, w)


def _norm_matmul_nt_lookahead(x, g, wt, tn):
    M, D = x.shape
    N = wt.shape[0]
    tm = min(PROJ_TM, M)
    assert M % tm == 0 and N % tn == 0

    def body(indices, x_ref, g_ref, w_ref, o_ref, h_ref):
        @pl.when(indices[1] == 0)
        def _():
            h_ref[...] = _rmsnorm(x_ref[...], g_ref[...]).astype(bf16)

        o_ref[...] = _dot_nt(h_ref[...], w_ref[...])

    def outer(x_hbm, g_hbm, w_hbm, o_hbm, h_ref):
        pltpu.emit_pipeline(
            body,
            grid=(M // tm, N // tn),
            in_specs=[
                pl.BlockSpec((tm, D), lambda i, j: (i, 0), pipeline_mode=pl.Buffered(2, use_lookahead=True)),
                pl.BlockSpec((1, D), lambda i, j: (0, 0)),
                pl.BlockSpec((tn, D), lambda i, j: (j, 0), pipeline_mode=pl.Buffered(3)),
            ],
            out_specs=[pl.BlockSpec((tm, tn), lambda i, j: (i, j))],
            _explicit_indices=True,
        )(x_hbm, g_hbm, w_hbm, o_hbm, scratches=(h_ref,))

    any_spec = pl.BlockSpec(memory_space=pl.ANY)
    return pl.pallas_call(
        outer,
        in_specs=[any_spec, any_spec, any_spec],
        out_specs=any_spec,
        out_shape=jax.ShapeDtypeStruct((M, N), f32),
        scratch_shapes=[pltpu.VMEM((tm, D), bf16)],
        compiler_params=pltpu.CompilerParams(vmem_limit_bytes=VMEM_LIMIT),
        name="norm_matmul_lookahead",
    )(x, g, wt)


def _mem_kv_kernel(x_ref, g_ref, wk_ref, wv_ref, k_ref, v_ref, kb_ref, vb_ref):
    h = _rmsnorm(x_ref[...], g_ref[...]).astype(bf16)
    k = _dot(h, wk_ref[...])
    v = _dot(h, wv_ref[...])
    kb_ref[...] = k.astype(bf16)
    vb_ref[...] = v.astype(bf16)
    tokens = x_ref.shape[0]
    for hd in range(MEM_HEADS):
        k_ref[pl.ds(hd, tokens, stride=MEM_HEADS), :] = k[:, hd * MEM_HD : (hd + 1) * MEM_HD]
        v_ref[pl.ds(hd, tokens, stride=MEM_HEADS), :] = v[:, hd * MEM_HD : (hd + 1) * MEM_HD]


def _mem_kv(x, g, wk, wv):
    M, D = x.shape
    N = wk.shape[1]
    tm = min(PROJ_TM, M)
    assert M % tm == 0 and N == MEM_W
    row_spec = pl.BlockSpec((tm, N), lambda i: (i, 0))
    head_row_spec = pl.BlockSpec((tm * MEM_HEADS, MEM_HD), lambda i: (i, 0))
    const = lambda shape: pl.BlockSpec(shape, lambda i: (0, 0))
    return pl.pallas_call(
        _mem_kv_kernel,
        grid=(M // tm,),
        in_specs=[pl.BlockSpec((tm, D), lambda i: (i, 0)), const((1, D)), const((D, N)), const((D, N))],
        out_specs=[head_row_spec] * 2 + [row_spec] * 2,
        out_shape=[jax.ShapeDtypeStruct((M * MEM_HEADS, MEM_HD), f32)] * 2 + [jax.ShapeDtypeStruct((M, N), bf16)] * 2,
        compiler_params=_cparams(("parallel",)),
        name="mem_kv",
    )(x, g, wk, wv)


def _bdot(a, b):
    return _dot(a.astype(bf16), b.astype(bf16))


def _unit_lower_inverse(nmats, row, col):
    eye = jnp.where(row == col, 1.0, 0.0).astype(f32)

    def same_block(size):
        return (row // size) == (col // size)

    blk = same_block(SUBLANES)
    a1 = [jnp.where(blk, n, 0.0) for n in nmats]
    a2 = [_bdot(a, a) for a in a1]
    a4 = [_bdot(a, a) for a in a2]
    ts = [_bdot(eye - x1, eye + x2) for x1, x2 in zip(a1, a2)]
    ts = [_bdot(t, eye + x4) for t, x4 in zip(ts, a4)]
    size = SUBLANES
    while size < CHUNK:
        nxt = same_block(2 * size)
        sel = jnp.logical_and(nxt, jnp.logical_not(blk))
        tbs = [t.astype(bf16) for t in ts]
        xs = [_dot(jnp.where(sel, n, 0.0).astype(bf16), tb) for n, tb in zip(nmats, tbs)]
        ts = [t - _dot(tb, x.astype(bf16)) for t, tb, x in zip(ts, tbs, xs)]
        blk = nxt
        size *= 2
    return ts


def _conv_silu_slab(x_ref, top_ref, w_ref, sl):
    rows = x_ref.shape[0]
    y = None
    for tap in range(GDN_CONV):
        back = GDN_CONV - 1 - tap
        window = jnp.concatenate(
            [top_ref[SUBLANES - back : 2 * SUBLANES - back, sl], x_ref[SUBLANES - back : rows - back, sl]], axis=0)
        term = window * w_ref[tap : tap + 1, sl]
        y = term if y is None else y + term
    return _silu(y)


def _delta_rule_prepare(pairs, gcum, gcum_t, beta_t, row, col, q_ref, k_ref, v_ref):
    n = range(len(pairs))
    incl = row >= col
    strict = row > col
    rows_of = lambda ci: slice(ci * CHUNK, (ci + 1) * CHUNK)
    gc = [gcum[rows_of(ci), GDN_HEADS + h : GDN_HEADS + h + 1] for ci, h in pairs]
    gr = [gcum_t[ci][GDN_HEADS + h : GDN_HEADS + h + 1, :] for ci, h in pairs]
    beta = [beta_t[rows_of(ci), h : h + 1] for ci, h in pairs]
    k = [k_ref[h, rows_of(ci), :] for ci, h in pairs]
    kb = [x.astype(bf16) for x in k]
    kk = [_dot_nt(kb[i], kb[i]) for i in n]
    qk = [_dot_nt(q_ref[h, rows_of(ci), :].astype(bf16), kb[i]) for i, (ci, h) in enumerate(pairs)]
    e = [jnp.exp(jnp.where(incl, gc[i] - gr[i], 0.0)) for i in n]
    nmat = [beta[i] * kk[i] * jnp.where(strict, e[i], 0.0) for i in n]
    qkd = [(qk[i] * jnp.where(incl, e[i], 0.0)).astype(bf16) for i in n]
    tinv = _unit_lower_inverse(nmat, row, col)
    gamma = [jnp.exp(gc[i]) for i in n]
    rhs = [jnp.concatenate([beta[i] * v_ref[h, rows_of(ci), :], (beta[i] * gamma[i]) * k[i]], axis=-1).astype(bf16)
           for i, (ci, h) in enumerate(pairs)]
    sol = [_dot(tinv[i].astype(bf16), rhs[i]) for i in n]
    return gc, gamma, sol, qkd


def _delta_rule_apply(ci, prepared, q_ref, k_ref, z_ref, gn_ref, s_ref, og_ref):
    gc, gamma, sol, qkd = prepared
    heads = range(GDN_HEADS)
    rows = slice(ci * CHUNK, (ci + 1) * CHUNK)
    sb = [s_ref[h].astype(bf16) for h in heads]
    ub = [(sol[h][:, :GDN_D] - _dot(sol[h][:, GDN_D:].astype(bf16), sb[h])).astype(bf16) for h in heads]
    o = [_dot((q_ref[h, rows, :] * gamma[h]).astype(bf16), sb[h]) + _dot(qkd[h], ub[h]) for h in heads]
    for h in heads:
        g_last = gc[h][CHUNK - 1 : CHUNK, :]
        k_end = k_ref[h, rows, :] * jnp.exp(g_last - gc[h])
        s_ref[h] = jnp.exp(g_last) * s_ref[h] + _dot(k_end.T.astype(bf16), ub[h])
        zh = z_ref[rows, h * GDN_D : (h + 1) * GDN_D]
        og_ref[rows, h * GDN_D : (h + 1) * GDN_D] = (_rmsnorm(o[h], gn_ref[...]) * _silu(zh)).astype(bf16)


def _gdn_prompt_kernel(*refs, n_cast):
    qkv_ref, z_ref, ba_ref, cw_ref, av_ref, dv_ref, gn_ref = refs[:7]
    cast_in = refs[7 : 7 + n_cast]
    og_ref, so_ref = refs[7 + n_cast : 9 + n_cast]
    cast_out = refs[9 + n_cast : 9 + 2 * n_cast]
    s_ref, carry_ref, q_ref, k_ref, v_ref = refs[9 + 2 * n_cast :]
    for src, dst in zip(cast_in, cast_out):
        dst[...] = src[...].astype(bf16)

    c = pl.program_id(1)
    rows = qkv_ref.shape[0]
    n_chunks = rows // CHUNK

    @pl.when(c == 0)
    def _():
        s_ref[...] = jnp.zeros_like(s_ref)
        carry_ref[0:SUBLANES, :] = jnp.zeros((SUBLANES, CONV_CH), f32)

    carry_ref[SUBLANES : 2 * SUBLANES, :] = qkv_ref[0:SUBLANES, :]
    for j in range(3 * GDN_HEADS):
        sl = slice(j * LANES, (j + 1) * LANES)
        y = _conv_silu_slab(qkv_ref, carry_ref, cw_ref, sl)
        h = j % GDN_HEADS
        if j < 2 * GDN_HEADS:
            y = y * lax.rsqrt(jnp.sum(y * y, -1, keepdims=True) + L2_EPS)
            if j < GDN_HEADS:
                q_ref[h] = y * (GDN_D ** -0.5)
            else:
                k_ref[h] = y
        else:
            v_ref[h] = y
    carry_ref[0:SUBLANES, :] = qkv_ref[rows - SUBLANES : rows, :]

    ba = ba_ref[...]
    beta_t = jax.nn.sigmoid(ba)
    g_t = -jnp.exp(av_ref[...]) * _softplus(ba + dv_ref[...])
    row_in_chunk = lax.broadcasted_iota(jnp.int32, (rows, LANES), 0) % CHUNK
    gcum = g_t
    shift = 1
    while shift < CHUNK:
        gcum = gcum + jnp.where(row_in_chunk >= shift, pltpu.roll(gcum, shift, 0), 0.0)
        shift *= 2
    gcum_t = [gcum[ci * CHUNK : (ci + 1) * CHUNK, :].T for ci in range(n_chunks)]

    row = lax.broadcasted_iota(jnp.int32, (CHUNK, CHUNK), 0)
    col = lax.broadcasted_iota(jnp.int32, (CHUNK, CHUNK), 1)
    pairs = [(ci, h) for ci in range(n_chunks) for h in range(GDN_HEADS)]
    gc, gamma, sol, qkd = _delta_rule_prepare(pairs, gcum, gcum_t, beta_t, row, col, q_ref, k_ref, v_ref)
    for ci in range(n_chunks):
        mine = slice(ci * GDN_HEADS, (ci + 1) * GDN_HEADS)
        _delta_rule_apply(ci, (gc[mine], gamma[mine], sol[mine], qkd[mine]), q_ref, k_ref, z_ref, gn_ref, s_ref, og_ref)

    @pl.when(c == pl.num_programs(1) - 1)
    def _():
        so_ref[0] = s_ref[...]


def _cast_block(shape, steps):
    R, C = shape
    for col_blocks in (1, 2, 4, 8, 16):
        row_blocks = steps // col_blocks
        if (steps % col_blocks == 0 and R % row_blocks == 0 and C % col_blocks == 0
                and (R // row_blocks) % (2 * SUBLANES) == 0 and (C // col_blocks) % LANES == 0):
            return (R // row_blocks, C // col_blocks), col_blocks
    return None


def _gdn_prompt(proj, B, T, off_ba, cw, avec, dvec, gn, weights):
    rows = GDN_STEP_CHUNKS * CHUNK
    assert T % rows == 0
    ns = T // rows
    step = lambda b, c: b * ns + c
    plans = [_cast_block(w.shape, B * ns) for w in weights]
    riders = [w for w, p in zip(weights, plans) if p is not None]
    cast_specs = [pl.BlockSpec(blk, functools.partial(lambda b, c, cb: (step(b, c) // cb, step(b, c) % cb), cb=cb))
                  for blk, cb in (p for p in plans if p is not None)]
    head_scratch = pltpu.VMEM((GDN_HEADS, rows, GDN_D), f32)
    out = pl.pallas_call(
        functools.partial(_gdn_prompt_kernel, n_cast=len(riders)),
        grid=(B, ns),
        in_specs=[
            pl.BlockSpec((rows, CONV_CH), lambda b, c: (step(b, c), 0)),
            pl.BlockSpec((rows, GDN_W), lambda b, c: (step(b, c), OFF_Z // GDN_W)),
            pl.BlockSpec((rows, LANES), lambda b, c: (step(b, c), off_ba // LANES)),
            pl.BlockSpec((GDN_CONV, CONV_CH), lambda b, c: (0, 0)),
            pl.BlockSpec((1, LANES), lambda b, c: (0, 0)),
            pl.BlockSpec((1, LANES), lambda b, c: (0, 0)),
            pl.BlockSpec((1, GDN_D), lambda b, c: (0, 0)),
        ] + cast_specs,
        out_specs=[
            pl.BlockSpec((rows, GDN_W), lambda b, c: (step(b, c), 0)),
            pl.BlockSpec((1, GDN_HEADS, GDN_D, GDN_D), lambda b, c: (b, 0, 0, 0)),
        ] + cast_specs,
        out_shape=[
            jax.ShapeDtypeStruct((B * T, GDN_W), bf16),
            jax.ShapeDtypeStruct((B, GDN_HEADS, GDN_D, GDN_D), f32),
        ] + [jax.ShapeDtypeStruct(w.shape, bf16) for w in riders],
        scratch_shapes=[
            pltpu.VMEM((GDN_HEADS, GDN_D, GDN_D), f32),
            pltpu.VMEM((2 * SUBLANES, CONV_CH), f32),
            head_scratch, head_scratch, head_scratch,
        ],
        compiler_params=_cparams(("arbitrary", "arbitrary")),
        name="gdn_prompt",
    )(proj, proj, proj, cw, avec, dvec, gn, *riders)
    converted = iter(out[2:])
    return out[0], out[1], [next(converted) if p is not None else w.astype(bf16) for w, p in zip(weights, plans)]


def _gdn_step_kernel(qkv_ref, z_ref, ba_ref, sc_ref, s0_ref, cw_ref, av_ref, dv_ref, gn_ref, og_ref, so_ref):
    taps = cw_ref[...]
    seqs = range(qkv_ref.shape[0])
    y, z, beta_t, gamma_t = [], [], [], []
    for i in seqs:
        x_new = qkv_ref[i]
        conv = jnp.sum(sc_ref[i] * taps[0 : GDN_CONV - 1, :], axis=0, keepdims=True) + x_new * taps[GDN_CONV - 1 : GDN_CONV, :]
        y.append(_silu(conv))
        ba = ba_ref[i]
        beta_t.append(jax.nn.sigmoid(ba))
        gamma_t.append(jnp.exp(-jnp.exp(av_ref[...]) * _softplus(ba + dv_ref[...])))
        z.append(z_ref[i])
    pairs = [(i, h) for i in seqs for h in range(GDN_HEADS)]
    n = range(len(pairs))
    head = lambda i, base, h: y[i][:, base + h * GDN_D : base + (h + 1) * GDN_D]
    q = [head(i, 0, h) for i, h in pairs]
    k = [head(i, GDN_W, h) for i, h in pairs]
    v = [head(i, 2 * GDN_W, h) for i, h in pairs]
    q = [q[p] * lax.rsqrt(jnp.sum(q[p] * q[p], -1, keepdims=True) + L2_EPS) * (GDN_D ** -0.5) for p in n]
    k = [k[p] * lax.rsqrt(jnp.sum(k[p] * k[p], -1, keepdims=True) + L2_EPS) for p in n]
    beta = [beta_t[i][:, h : h + 1] for i, h in pairs]
    gamma = [gamma_t[i][:, GDN_HEADS + h : GDN_HEADS + h + 1] for i, h in pairs]
    k_col = [jnp.broadcast_to(k[p], (GDN_D, GDN_D)).T for p in n]
    sub = lax.broadcasted_iota(jnp.int32, (SUBLANES, GDN_D), 0)
    kq = [jnp.where(sub == 0, k[p], jnp.where(sub == 1, q[p], 0.0)).astype(bf16) for p in n]
    kq_s = [_dot(kq[p], s0_ref[i, h].astype(bf16)) for p, (i, h) in enumerate(pairs)]
    u = [beta[p] * v[p] - (beta[p] * gamma[p]) * kq_s[p][0:1] for p in n]
    o = [gamma[p] * kq_s[p][1:2] + jnp.sum(q[p] * k[p], -1, keepdims=True) * u[p] for p in n]
    for p, (i, h) in enumerate(pairs):
        so_ref[i, h] = gamma[p] * s0_ref[i, h] + k_col[p] * u[p]
        zh = z[i][:, h * GDN_D : (h + 1) * GDN_D]
        og_ref[i, :, h * GDN_D : (h + 1) * GDN_D] = (_rmsnorm(o[p], gn_ref[...]) * _silu(zh)).astype(bf16)


def _gdn_step(proj3, off_ba, state_conv, state_gdn, cw, avec, dvec, gn):
    B = proj3.shape[0]
    bb = math.gcd(B, GDN_STEP_BATCH)
    return pl.pallas_call(
        _gdn_step_kernel,
        grid=(B // bb,),
        in_specs=[
            pl.BlockSpec((bb, 1, CONV_CH), lambda b: (b, 0, 0)),
            pl.BlockSpec((bb, 1, GDN_W), lambda b: (b, 0, OFF_Z // GDN_W)),
            pl.BlockSpec((bb, 1, LANES), lambda b: (b, 0, off_ba // LANES)),
            pl.BlockSpec((bb, GDN_CONV - 1, CONV_CH), lambda b: (b, 0, 0)),
            pl.BlockSpec((bb, GDN_HEADS, GDN_D, GDN_D), lambda b: (b, 0, 0, 0)),
            pl.BlockSpec((GDN_CONV, CONV_CH), lambda b: (0, 0)),
            pl.BlockSpec((1, LANES), lambda b: (0, 0)),
            pl.BlockSpec((1, LANES), lambda b: (0, 0)),
            pl.BlockSpec((1, GDN_D), lambda b: (0, 0)),
        ],
        out_specs=[
            pl.BlockSpec((bb, 1, GDN_W), lambda b: (b, 0, 0)),
            pl.BlockSpec((bb, GDN_HEADS, GDN_D, GDN_D), lambda b: (b, 0, 0, 0)),
        ],
        out_shape=[
            jax.ShapeDtypeStruct((B, 1, GDN_W), bf16),
            jax.ShapeDtypeStruct((B, GDN_HEADS, GDN_D, GDN_D), f32),
        ],
        compiler_params=_cparams(("parallel",)),
        name="gdn_step",
    )(proj3, proj3, proj3, state_conv, state_gdn, cw, avec, dvec, gn)


def _rope(x, cos, sin_signed):
    width = x.shape[-1]
    lane = lax.broadcasted_iota(jnp.int32, x.shape, x.ndim - 1)
    first_half = (lane % SWA_HD) < (SWA_HD // 2)
    rot = jnp.where(first_half, pltpu.roll(x, width - SWA_HD // 2, x.ndim - 1), pltpu.roll(x, SWA_HD // 2, x.ndim - 1))
    return x * cos + rot * sin_signed


def _head_halves(x2, head_parity, lane):
    swapped = pltpu.roll(x2, SWA_HD, 1)
    lo_src, hi_src = (x2, swapped) if head_parity == 0 else (swapped, x2)
    return jnp.where(lane < SWA_HD, lo_src, 0.0), jnp.where(lane >= SWA_HD, hi_src, 0.0)


def _swa_prompt_kernel(sinks_ref, q_ref, kv_ref, cos_ref, sin_ref, os_ref, kc_ref, kprev_ref, vprev_ref):
    n = pl.program_id(1)
    n_blk = q_ref.shape[0] // WINDOW

    @pl.when(n == 0)
    def _():
        kprev_ref[...] = jnp.zeros_like(kprev_ref)
        vprev_ref[...] = jnp.zeros_like(vprev_ref)

    row = lax.broadcasted_iota(jnp.int32, (WINDOW, WINDOW), 0)
    col = lax.broadcasted_iota(jnp.int32, (WINDOW, WINDOW), 1)
    own = col <= row
    first_bias = jnp.where(n > 0, 0.0, -jnp.inf)
    lane = lax.broadcasted_iota(jnp.int32, (WINDOW, LANES), 1)
    scale = SWA_HD ** -0.5
    group = SWA_HEADS // SWA_KV_HEADS

    k_own, v_own, k_pre, v_pre, q2 = [], [], [], [], []
    for blk in range(n_blk):
        rows = slice(blk * WINDOW, (blk + 1) * WINDOW)
        cos = cos_ref[rows, :]
        sin = sin_ref[rows, :]
        k_cur = _rope(kv_ref[rows, :SWA_KV_W], cos, sin)
        v_cur = kv_ref[rows, SWA_KV_W:]
        if blk == n_blk - 1:
            kc_ref[0] = k_cur
        k_own.append([]), v_own.append([]), k_pre.append([]), v_pre.append([]), q2.append([])
        for h in range(SWA_KV_HEADS):
            pair = slice((h // 2) * LANES, (h // 2 + 1) * LANES)
            k_own[blk].append([a.astype(bf16) for a in _head_halves(k_cur[:, pair], h % 2, lane)])
            v_own[blk].append([a.astype(bf16) for a in _head_halves(v_cur[:, pair], h % 2, lane)])
            if blk == 0:
                k_pre[blk].append([kprev_ref[h, par] for par in range(2)])
                v_pre[blk].append([vprev_ref[h, par] for par in range(2)])
            else:
                k_pre[blk].append(k_own[blk - 1][h])
                v_pre[blk].append(v_own[blk - 1][h])
            if blk == n_blk - 1:
                for par in range(2):
                    kprev_ref[h, par] = k_own[blk][h][par]
                    vprev_ref[h, par] = v_own[blk][h][par]
            q_h = _rope(q_ref[rows, h * group * SWA_HD : (h + 1) * group * SWA_HD], cos, sin) * scale
            q2[blk].append([q_h[:, j * LANES : (j + 1) * LANES].astype(bf16) for j in range(group // 2)])

    heads = [(blk, h, j, par) for blk in range(n_blk) for h in range(SWA_KV_HEADS)
             for j in range(group // 2) for par in range(2)]
    idx = range(len(heads))
    sink = [sinks_ref[h * group + 2 * j + par] for blk, h, j, par in heads]
    s = [jnp.where(own, _dot_nt(q2[blk][h][j], k_own[blk][h][par]),
                   _dot_nt(q2[blk][h][j], k_pre[blk][h][par]) + (first_bias if blk == 0 else 0.0))
         for blk, h, j, par in heads]
    m = [jnp.maximum(jnp.max(s[i], -1, keepdims=True), sink[i]) for i in idx]
    p = [jnp.exp(s[i] - m[i]) for i in idx]
    inv = [1.0 / (jnp.sum(p[i], -1, keepdims=True) + jnp.exp(sink[i] - m[i])) for i in idx]
    o = [(_dot(jnp.where(own, p[i], 0.0).astype(bf16), v_own[blk][h][par])
          + _dot(jnp.where(own, 0.0, p[i]).astype(bf16), v_pre[blk][h][par])) * inv[i]
         for i, (blk, h, j, par) in enumerate(heads)]
    for i in range(0, len(heads), 2):
        blk, head = divmod(i, SWA_HEADS)
        os_ref[blk * WINDOW : (blk + 1) * WINDOW, head * SWA_HD : head * SWA_HD + LANES] = (o[i] + o[i + 1]).astype(bf16)


def _swa_prompt(proj, B, T, off_q, off_kv, cos, sin, sinks):
    assert T % WINDOW == 0 and SWA_KV_W == 2 * LANES
    rows = math.gcd(SWA_STEP_BLOCKS, T // WINDOW) * WINDOW
    nb = T // rows
    return pl.pallas_call(
        _swa_prompt_kernel,
        grid=(B, nb),
        in_specs=[
            pl.BlockSpec(memory_space=pltpu.SMEM),
            pl.BlockSpec((rows, SWA_Q_W), lambda b, n: (b * nb + n, off_q // SWA_Q_W)),
            pl.BlockSpec((rows, 2 * SWA_KV_W), lambda b, n: (b * nb + n, off_kv // (2 * SWA_KV_W))),
            pl.BlockSpec((rows, SWA_KV_W), lambda b, n: (n, 0)),
            pl.BlockSpec((rows, SWA_KV_W), lambda b, n: (n, 0)),
        ],
        out_specs=[
            pl.BlockSpec((rows, SWA_Q_W), lambda b, n: (b * nb + n, 0)),
            pl.BlockSpec((1, WINDOW, SWA_KV_W), lambda b, n: (b, 0, 0)),
        ],
        out_shape=[
            jax.ShapeDtypeStruct((B * T, SWA_Q_W), bf16),
            jax.ShapeDtypeStruct((B, WINDOW, SWA_KV_W), f32),
        ],
        scratch_shapes=[pltpu.VMEM((SWA_KV_HEADS, 2, WINDOW, LANES), bf16)] * 2,
        compiler_params=_cparams(("parallel", "arbitrary")),
        name="swa_prompt",
    )(sinks, proj, proj, cos, sin)


def _swa_step_kernel(qe_ref, kv_ref, ck_ref, cv_ref, cos_ref, sin_ref, sinks_ref, r_ref, nk_ref, nv_ref):
    cos = cos_ref[...]
    sin = sin_ref[...]
    sink = sinks_ref[...]
    row = lax.broadcasted_iota(jnp.int32, (WINDOW, SWA_KV_W), 0)
    scale = SWA_HD ** -0.5
    seqs = range(qe_ref.shape[0])
    kv = [kv_ref[i] for i in seqs]
    k_new = [_rope(kv[i][:, :SWA_KV_W], cos, sin) for i in seqs]
    keys = [jnp.where(row == WINDOW - 1, k_new[i], pltpu.roll(ck_ref[i].T, WINDOW - 1, 0)) for i in seqs]
    vals = [jnp.where(row == WINDOW - 1, kv[i][:, SWA_KV_W:], pltpu.roll(cv_ref[i].T, WINDOW - 1, 0)) for i in seqs]
    for i in seqs:
        nk_ref[i] = keys[i].T
        nv_ref[i] = vals[i].T
    q = [_rope(qe_ref[i], cos, sin) for i in seqs]
    s = [_dot_nt(q[i].astype(bf16), keys[i].astype(bf16)) * scale for i in seqs]
    m = [jnp.maximum(jnp.max(s[i], -1, keepdims=True), sink) for i in seqs]
    p = [jnp.exp(s[i] - m[i]) for i in seqs]
    denom = [jnp.sum(p[i], -1, keepdims=True) + jnp.exp(sink - m[i]) for i in seqs]
    for i in seqs:
        r_ref[i] = _dot((p[i] / denom[i]).astype(bf16), vals[i].astype(bf16))


def _swa_step(q_exp, proj3, off_kv, cache_k, cache_v, cos, sin, sinks_col):
    B = q_exp.shape[0]
    bb = math.gcd(B, SWA_STEP_BATCH)
    assert cache_k.shape[2] == WINDOW
    return pl.pallas_call(
        _swa_step_kernel,
        grid=(B // bb,),
        in_specs=[
            pl.BlockSpec((bb, SWA_HEADS, SWA_KV_W), lambda i: (i, 0, 0)),
            pl.BlockSpec((bb, 1, 2 * SWA_KV_W), lambda i: (i, 0, off_kv // (2 * SWA_KV_W))),
            pl.BlockSpec((bb, SWA_KV_W, WINDOW), lambda i: (i, 0, 0)),
            pl.BlockSpec((bb, SWA_KV_W, WINDOW), lambda i: (i, 0, 0)),
            pl.BlockSpec((1, SWA_KV_W), lambda i: (0, 0)),
            pl.BlockSpec((1, SWA_KV_W), lambda i: (0, 0)),
            pl.BlockSpec((SWA_HEADS, 1), lambda i: (0, 0)),
        ],
        out_specs=[
            pl.BlockSpec((bb, SWA_HEADS, SWA_KV_W), lambda i: (i, 0, 0)),
            pl.BlockSpec((bb, SWA_KV_W, WINDOW), lambda i: (i, 0, 0)),
            pl.BlockSpec((bb, SWA_KV_W, WINDOW), lambda i: (i, 0, 0)),
        ],
        out_shape=[
            jax.ShapeDtypeStruct((B, SWA_HEADS, SWA_KV_W), f32),
            jax.ShapeDtypeStruct((B, SWA_KV_W, WINDOW), f32),
            jax.ShapeDtypeStruct((B, SWA_KV_W, WINDOW), f32),
        ],
        compiler_params=_cparams(("parallel",)),
        name="swa_step",
    )(q_exp, proj3, cache_k, cache_v, cos, sin, sinks_col)


def _merge_core(og_ref, os_ref, gg_ref, gs_ref, x_ref, wg_ref, ws_ref, wo_ref, gq_ref, wq_ref):
    p_gdn = _dot(og_ref[...], wg_ref[...])
    p_swa = _dot(os_ref[...], ws_ref[...])
    merged = jax.nn.sigmoid(gg_ref[...]) * p_gdn + jax.nn.sigmoid(gs_ref[...]) * p_swa
    x_new = x_ref[...] + _dot(merged.astype(bf16), wo_ref[...])
    return x_new, _dot(_rmsnorm(x_new, gq_ref[...]).astype(bf16), wq_ref[...])


def _merge_kernel(*refs):
    xo_ref, qm_ref = refs[-2:]
    xo_ref[...], qm_ref[...] = _merge_core(*refs[:-2])


def _merge_mem_kernel(*refs):
    mk_ref, mv_ref, wmo_ref, xo_ref = refs[-4:]
    x_new, q = _merge_core(*refs[:-4])
    q = q.astype(bf16)
    scale = MEM_HD ** -0.5
    heads = range(MEM_HEADS)
    cols = lambda h: slice(h * MEM_HD, (h + 1) * MEM_HD)
    s = [_dot_nt(q[:, cols(h)], mk_ref[0, :, cols(h)]) * scale for h in heads]
    p = [jnp.exp(s[h] - jnp.max(s[h], -1, keepdims=True)) for h in heads]
    p = [p[h] / jnp.sum(p[h], -1, keepdims=True) for h in heads]
    o = [_dot(p[h].astype(bf16), mv_ref[0, :, cols(h)]) for h in heads]
    xo_ref[...] = x_new + _dot(jnp.concatenate(o, axis=-1).astype(bf16), wmo_ref[...])


def _merge(og, os_, proj, off_gg, x, wg, ws, wo, gq, wq, mem=None):
    M, D = x.shape
    tm = min(MERGE_TM, M)
    assert M % tm == 0 and off_gg % D == 0
    const = lambda shape: pl.BlockSpec(shape, lambda i: (0, 0), pipeline_mode=pl.Buffered(1))
    in_specs = [
        pl.BlockSpec((tm, GDN_W), lambda i: (i, 0)),
        pl.BlockSpec((tm, SWA_Q_W), lambda i: (i, 0)),
        pl.BlockSpec((tm, D), lambda i: (i, off_gg // D)),
        pl.BlockSpec((tm, D), lambda i: (i, off_gg // D + 1)),
        pl.BlockSpec((tm, D), lambda i: (i, 0)),
        const((GDN_W, D)),
        const((SWA_Q_W, D)),
        const((D, D)),
        const((1, D)),
        const((D, MEM_W)),
    ]
    args = [og, os_, proj, proj, x, wg, ws, wo, gq, wq]
    x_spec = pl.BlockSpec((tm, D), lambda i: (i, 0))
    x_shape = jax.ShapeDtypeStruct((M, D), f32)
    if mem is None:
        return pl.pallas_call(
            _merge_kernel,
            grid=(M // tm,),
            in_specs=in_specs,
            out_specs=[x_spec, pl.BlockSpec((tm, MEM_W), lambda i: (i, 0))],
            out_shape=[x_shape, jax.ShapeDtypeStruct((M, MEM_W), f32)],
            compiler_params=_cparams(("parallel",)),
            name="merge",
        )(*args)
    mem_k, mem_v, wmo, seq_rows = mem
    assert seq_rows % tm == 0
    mt = mem_k.shape[1]
    mem_spec = pl.BlockSpec((1, mt, MEM_W), lambda i: (i // (seq_rows // tm), 0, 0))
    return pl.pallas_call(
        _merge_mem_kernel,
        grid=(M // tm,),
        in_specs=in_specs + [mem_spec, mem_spec, const((MEM_W, D))],
        out_specs=x_spec,
        out_shape=x_shape,
        compiler_params=_cparams(("parallel",)),
        name="merge_mem",
    )(*args, mem_k, mem_v, wmo)


def _mem_attn_step_kernel(q_ref, k_ref, v_ref, o_ref):
    mt2 = k_ref.shape[1] // SUBLANES
    seqs = range(q_ref.shape[0])
    q8 = [q_ref[i] * (MEM_HD ** -0.5) for i in seqs]
    s = [jnp.sum(k_ref[i].reshape(mt2, SUBLANES, MEM_HD) * q8[i], axis=-1, keepdims=True) for i in seqs]
    m = [jnp.max(s[i], axis=0) for i in seqs]
    m = [jnp.maximum(m[i], pltpu.roll(m[i], MEM_HEADS, 0)) for i in seqs]
    p = [jnp.exp(s[i] - m[i]) for i in seqs]
    l = [jnp.sum(p[i], axis=0) for i in seqs]
    l = [l[i] + pltpu.roll(l[i], MEM_HEADS, 0) for i in seqs]
    o = [jnp.sum(p[i] * v_ref[i].reshape(mt2, SUBLANES, MEM_HD), axis=0) for i in seqs]
    for i in seqs:
        o_ref[i] = (o[i] + pltpu.roll(o[i], MEM_HEADS, 0)) / l[i]


def _mem_attn_step(q8, mem_k, mem_v):
    B, rows, _ = mem_k.shape
    assert 2 * MEM_HEADS == SUBLANES and rows % SUBLANES == 0
    bb = math.gcd(B, MEM_STEP_BATCH)
    return pl.pallas_call(
        _mem_attn_step_kernel,
        grid=(B // bb,),
        in_specs=[
            pl.BlockSpec((bb, SUBLANES, MEM_HD), lambda i: (i, 0, 0)),
            pl.BlockSpec((bb, rows, MEM_HD), lambda i: (i, 0, 0)),
            pl.BlockSpec((bb, rows, MEM_HD), lambda i: (i, 0, 0)),
        ],
        out_specs=pl.BlockSpec((bb, SUBLANES, MEM_HD), lambda i: (i, 0, 0)),
        out_shape=jax.ShapeDtypeStruct((B, SUBLANES, MEM_HD), f32),
        compiler_params=_cparams(("parallel",)),
        name="mem_attn_step",
    )(q8, mem_k, mem_v)


def _proj_residual_kernel(a_ref, w_ref, x_ref, o_ref):
    o_ref[...] = x_ref[...] + _dot(a_ref[...].astype(bf16), w_ref[...])


def _proj_residual(a, w, x):
    M, D = x.shape
    return pl.pallas_call(
        _proj_residual_kernel,
        out_shape=jax.ShapeDtypeStruct((M, D), f32),
        compiler_params=pltpu.CompilerParams(vmem_limit_bytes=VMEM_LIMIT),
        name="proj_residual",
    )(a, w, x)


def _reorder_rows_kernel(w_ref, o_ref, *, pieces):
    at = 0
    for lo, hi in pieces:
        o_ref[at : at + hi - lo, :] = w_ref[lo:hi, :].astype(bf16)
        at += hi - lo
    o_ref[at:, :] = jnp.zeros((o_ref.shape[0] - at, o_ref.shape[1]), bf16)


def _reorder_rows(wt, layer, pieces, height):
    _, d_in, D = wt.shape
    assert all(lo % (2 * SUBLANES) == 0 and hi % (2 * SUBLANES) == 0 for lo, hi in pieces) and D % LANES == 0
    return pl.pallas_call(
        functools.partial(_reorder_rows_kernel, pieces=pieces),
        grid=(D // LANES,),
        in_specs=[pl.BlockSpec((None, d_in, LANES), lambda i: (layer, 0, i))],
        out_specs=pl.BlockSpec((height, LANES), lambda i: (0, i)),
        out_shape=jax.ShapeDtypeStruct((height, D), bf16),
        compiler_params=_cparams(("parallel",)),
        name="reorder_rows",
    )(wt)


def _rope_tables(pos):
    half = SWA_HD // 2
    inv_freq = ROPE_THETA ** (-jnp.arange(half, dtype=f32) / half)
    ang = pos.astype(f32)[:, None] * inv_freq[None, :]
    cos = jnp.cos(ang)
    sin = jnp.sin(ang)
    reps = SWA_KV_W // SWA_HD
    return jnp.tile(jnp.concatenate([cos, cos], -1), (1, reps)), jnp.tile(jnp.concatenate([-sin, sin], -1), (1, reps))


def kernel(x_prompt, x_sample, state_gdn, state_conv, cache_swa_k, cache_swa_v, cache_mem_k, cache_mem_v, mem_prompt, norm_ffn1, ffn1_w1, ffn1_w3, ffn1_w2, norm_mix, w_in, conv_w, gdn_A_log, gdn_dt_bias, gdn_norm, swa_sinks, w_br_gdn, w_br_swa, w_out, norm_mem_q, norm_mem_kv, w_mem_q, w_mem_k, w_mem_v, w_mem_o, norm_ffn2, ffn2_w1, ffn2_w3, ffn2_w2, norm_final):
    Bp, Tp, D = x_prompt.shape
    Bs, Ts, _ = x_sample.shape
    assert Ts == 1
    depth = norm_ffn1.shape[0]
    n_mem = mem_prompt.shape[1]
    group = SWA_HEADS // SWA_KV_HEADS

    off_gs = OFF_GG + D
    off_q = off_gs + D
    off_kv = off_q + SWA_Q_W
    off_ba = off_kv + 2 * SWA_KV_W
    d_in_pad = -(-(off_ba + LANES) // PROJ_TN) * PROJ_TN
    o_b = CONV_CH + GDN_W
    o_q = o_b + 2 * GDN_HEADS
    o_gg = o_q + SWA_Q_W + 2 * SWA_KV_W

    cos_p, sin_p = _rope_tables(jnp.arange(Tp, dtype=jnp.int32))
    cos_s, sin_s = _rope_tables(PAST_LEN + jnp.arange(Ts, dtype=jnp.int32))
    eye_kv = jnp.eye(SWA_KV_HEADS, dtype=f32)
    row = lambda v: v.reshape(1, -1)

    hp = x_prompt.reshape(Bp * Tp, D)
    hs = x_sample.reshape(Bs, D)
    outs = [[] for _ in range(10)]
    for l in range(depth):
        w_in_rt = _reorder_rows(jnp.swapaxes(w_in, 1, 2), l, ((0, o_b), (o_gg, w_in.shape[2]), (o_q, o_gg), (o_b, o_q)), d_in_pad)
        ffn1 = (row(norm_ffn1[l]), ffn1_w1[l].astype(bf16), ffn1_w3[l].astype(bf16), ffn1_w2[l].astype(bf16))
        last = l == depth - 1
        gfin = row(norm_final)
        avec = jnp.zeros((1, LANES), f32).at[0, GDN_HEADS : 2 * GDN_HEADS].set(gdn_A_log[l])
        dvec = jnp.zeros((1, LANES), f32).at[0, GDN_HEADS : 2 * GDN_HEADS].set(gdn_dt_bias[l])
        gdn_common = (conv_w[l], avec, dvec, row(gdn_norm[l]))

        x1 = _ffn(hp, *ffn1, gfin, False)
        proj = _norm_matmul_nt_lookahead(x1, row(norm_mix[l]), w_in_rt, PROJ_TN)
        later = (ffn2_w1[l], ffn2_w3[l], ffn2_w2[l], w_br_gdn[l], w_br_swa[l], w_out[l], w_mem_q[l],
                 w_mem_k[l], w_mem_v[l], w_mem_o[l])
        og, s_new, later = _gdn_prompt(proj, Bp, Tp, off_ba, *gdn_common, later)
        ffn2 = (row(norm_ffn2[l]), *later[0:3])
        merge_w = (*later[3:6], row(norm_mem_q[l]), later[6])
        wmk, wmv, wmo = later[7:10]
        os_, kc = _swa_prompt(proj, Bp, Tp, off_q, off_kv, cos_p, sin_p, swa_sinks[l])
        mem_x = mem_prompt.reshape(Bp * n_mem, D)
        mk, mv, mkb, mvb = _mem_kv(mem_x, row(norm_mem_kv[l]), wmk, wmv)
        mem = (mkb.reshape(Bp, n_mem, MEM_W), mvb.reshape(Bp, n_mem, MEM_W), wmo, Tp)
        x3 = _merge(og, os_, proj, OFF_GG, x1, *merge_w, mem=mem)
        hp = _ffn(x3, *ffn2, gfin, last)
        proj_b = proj.reshape(Bp, Tp, d_in_pad)
        outs[0].append(s_new)
        outs[1].append(proj_b[:, Tp - (GDN_CONV - 1) :, :CONV_CH])
        outs[2].append(kc.reshape(Bp, WINDOW, SWA_KV_HEADS, SWA_HD))
        outs[3].append(proj_b[:, Tp - WINDOW :, off_kv + SWA_KV_W : off_kv + 2 * SWA_KV_W].reshape(Bp, WINDOW, SWA_KV_HEADS, SWA_HD))
        outs[4].append(mk.reshape(Bp, n_mem, MEM_HEADS, MEM_HD))
        outs[5].append(mv.reshape(Bp, n_mem, MEM_HEADS, MEM_HD))

        x1 = _ffn(hs, *ffn1, gfin, False)
        proj = _norm_matmul(x1, row(norm_mix[l]), w_in_rt, PROJ_TN, w_transposed=True)
        proj3 = proj.reshape(Bs, 1, d_in_pad)
        og, s_new = _gdn_step(proj3, off_ba, state_conv[l], state_gdn[l], *gdn_common)
        q_raw = proj[:, off_q : off_q + SWA_Q_W].reshape(Bs, SWA_KV_HEADS, group, 1, SWA_HD)
        q_exp = (q_raw * eye_kv[None, :, None, :, None]).reshape(Bs, SWA_HEADS, SWA_KV_W)
        ck = jnp.swapaxes(cache_swa_k[l].reshape(Bs, WINDOW, SWA_KV_W), 1, 2)
        cv = jnp.swapaxes(cache_swa_v[l].reshape(Bs, WINDOW, SWA_KV_W), 1, 2)
        r, nk, nv = _swa_step(q_exp, proj3, off_kv, ck, cv, cos_s, sin_s, swa_sinks[l].reshape(SWA_HEADS, 1))
        r5 = r.reshape(Bs, SWA_KV_HEADS, group, SWA_KV_HEADS, SWA_HD)
        kvh = jnp.arange(SWA_KV_HEADS)
        os_ = jnp.transpose(r5[:, kvh, :, kvh, :], (1, 0, 2, 3)).reshape(Bs, SWA_Q_W).astype(bf16)
        x2, qm = _merge(og.reshape(Bs, GDN_W), os_, proj, OFF_GG, x1, *merge_w)
        q8 = jnp.tile(qm.reshape(Bs, MEM_HEADS, MEM_HD), (1, 2, 1))
        om = _mem_attn_step(q8, cache_mem_k[l].reshape(Bs, n_mem * MEM_HEADS, MEM_HD),
                            cache_mem_v[l].reshape(Bs, n_mem * MEM_HEADS, MEM_HD))
        x3 = _proj_residual(om[:, :MEM_HEADS].reshape(Bs, MEM_W), wmo, x2)
        hs = _ffn(x3, *ffn2, gfin, last)
        outs[6].append(s_new)
        outs[7].append(jnp.concatenate([state_conv[l][:, 1:], proj3[:, :, :CONV_CH]], axis=1))
        outs[8].append(jnp.swapaxes(nk, 1, 2).reshape(Bs, WINDOW, SWA_KV_HEADS, SWA_HD))
        outs[9].append(jnp.swapaxes(nv, 1, 2).reshape(Bs, WINDOW, SWA_KV_HEADS, SWA_HD))

    return (hp.reshape(Bp, Tp, D), hs.reshape(Bs, Ts, D), *(jnp.stack(o) for o in outs))
```
